```python
import math
import jax, jax.numpy as jnp
from jax import lax
import numpy as np

D_MODEL = 1024
BATCH = 16
SEQ = 4096
DEPTH = 1

MIX_WIDTH = D_MODEL
SSM_WIDTH = MIX_WIDTH // 2
ATTN_WIDTH = MIX_WIDTH - SSM_WIDTH
SSM_GROUP = 16
SSM_GROUPS = SSM_WIDTH // SSM_GROUP
SSM_STATE = 64
HEAD_DIM = 64
N_HEADS = ATTN_WIDTH // HEAD_DIM
MOBA_BLOCK = 256
MOBA_TOPK = 3
Q_CHUNK = 16
ROT_DIM = HEAD_DIM // 4
ROPE_THETA = 500000.0
N_EXPERTS = 32
TOP_K = 4
D_FF = D_MODEL
SWIGLU_ALPHA = 1.702
SWIGLU_LIMIT = 7.0
NORM_EPS = 1e-5
N_MOD = 6
IN_WIDTH = SSM_WIDTH + 3 * ATTN_WIDTH
NEG_INF = -1e30

kernel_name = 'hybrid_s5_moba_moe_adaln_block'


def _rmsnorm(x, g):
    xf = x.astype(jnp.float32)
    out = xf * lax.rsqrt(jnp.mean(xf * xf, axis=-1, keepdims=True) + NORM_EPS) * g.astype(jnp.float32)
    return out.astype(x.dtype)


def _modulate(h, shift, scale):
    return h * (1.0 + scale[:, None, :]) + shift[:, None, :]


def _rope_tables(positions):
    inv_freq = ROPE_THETA ** (-jnp.arange(0, ROT_DIM, 2, dtype=jnp.float32) / ROT_DIM)
    ang = positions.astype(jnp.float32)[:, :, None] * inv_freq[None, None, :]
    return jnp.cos(ang)[:, None], jnp.sin(ang)[:, None]


def _partial_rope(x, cos, sin):
    half = ROT_DIM // 2
    xr = x[..., :ROT_DIM].astype(jnp.float32)
    x1, x2 = xr[..., :half], xr[..., half:]
    rot = jnp.concatenate([x1 * cos - x2 * sin, x2 * cos + x1 * sin], axis=-1)
    return jnp.concatenate([rot.astype(x.dtype), x[..., ROT_DIM:]], axis=-1)


def _complex_linear_combine(earlier, later):
    ar_e, ai_e, br_e, bi_e = earlier
    ar_l, ai_l, br_l, bi_l = later
    ar = ar_l * ar_e - ai_l * ai_e
    ai = ar_l * ai_e + ai_l * ar_e
    br = ar_l * br_e - ai_l * bi_e + br_l
    bi = ar_l * bi_e + ai_l * br_e + bi_l
    return (ar, ai, br, bi)


def _s5_mixer(u, lambda_re, lambda_im, log_dt, b_re, b_im, c_re, c_im, d_skip, w_glu, b_glu):
    bsz, seq, _ = u.shape
    f32 = jnp.float32
    uf = u.astype(f32)
    lr = lambda_re.astype(f32)
    li = lambda_im.astype(f32)
    dt = jnp.exp(log_dt.astype(f32))[:, None]
    mag = jnp.exp(lr * dt)
    abar_re = mag * jnp.cos(li * dt)
    abar_im = mag * jnp.sin(li * dt)
    zr, zi = abar_re - 1.0, abar_im
    den = lr * lr + li * li
    cr = (zr * lr + zi * li) / den
    ci = (zi * lr - zr * li) / den
    br, bi = b_re.astype(f32), b_im.astype(f32)
    bbar_re = cr[..., None] * br - ci[..., None] * bi
    bbar_im = cr[..., None] * bi + ci[..., None] * br
    cre, cim = c_re.astype(f32), c_im.astype(f32)

    def scan_one(u_b):
        bu_re = jnp.einsum('sgh,gph->sgp', u_b, bbar_re)
        bu_im = jnp.einsum('sgh,gph->sgp', u_b, bbar_im)
        a_re = jnp.broadcast_to(abar_re, bu_re.shape)
        a_im = jnp.broadcast_to(abar_im, bu_re.shape)
        _, _, h_re, h_im = lax.associative_scan(_complex_linear_combine, (a_re, a_im, bu_re, bu_im), axis=0)
        return jnp.einsum('gnp,sgp->sgn', cre, h_re) - jnp.einsum('gnp,sgp->sgn', cim, h_im)

    y = lax.map(scan_one, uf.reshape(bsz, seq, SSM_GROUPS, SSM_GROUP))
    y = y.reshape(bsz, seq, SSM_WIDTH) + d_skip.astype(f32) * uf
    g = jax.nn.gelu(y, approximate=False)
    out = g * jax.nn.sigmoid(g @ w_glu.astype(f32) + b_glu.astype(f32))
    return out.astype(u.dtype)


def _moba_attention(q, k, v):
    bsz, nh, seq, hd = q.shape
    f32 = jnp.float32
    nb = -(-seq // MOBA_BLOCK)
    s_pad = nb * MOBA_BLOCK
    pad = ((0, 0), (0, 0), (0, s_pad - seq), (0, 0))
    k_pad = jnp.pad(k, pad)
    v_pad = jnp.pad(v, pad)
    kb = k_pad.reshape(bsz, nh, nb, MOBA_BLOCK, hd)
    vb = v_pad.reshape(bsz, nh, nb, MOBA_BLOCK, hd)
    k_mean = jnp.mean(kb.astype(f32), axis=3)
    ksel = min(MOBA_TOPK, nb)
    b_ix = jnp.arange(bsz)[:, None, None, None]
    h_ix = jnp.arange(nh)[None, :, None, None]
    scale = HEAD_DIM ** -0.5

    def chunk(ci):
        start = ci * Q_CHUNK
        blk = start // MOBA_BLOCK
        q_c = lax.dynamic_slice_in_dim(q, start, Q_CHUNK, axis=2).astype(f32)
        gate = jnp.einsum('bhqd,bhnd->bhqn', q_c, k_mean)
        past = jnp.arange(nb) < blk
        gate = jnp.where(past, gate, NEG_INF)
        _, sel = lax.top_k(gate, ksel)
        valid = sel < blk
        k_sel = kb[b_ix, h_ix, sel].astype(f32)
        v_sel = vb[b_ix, h_ix, sel].astype(f32)
        s_sel = jnp.einsum('bhqd,bhqnkd->bhqnk', q_c, k_sel) * scale
        s_sel = jnp.where(valid[..., None], s_sel, NEG_INF).reshape(bsz, nh, Q_CHUNK, ksel * MOBA_BLOCK)
        k_own = lax.dynamic_slice_in_dim(k_pad, blk * MOBA_BLOCK, MOBA_BLOCK, axis=2).astype(f32)
        v_own = lax.dynamic_slice_in_dim(v_pad, blk * MOBA_BLOCK, MOBA_BLOCK, axis=2).astype(f32)
        s_own = jnp.einsum('bhqd,bhkd->bhqk', q_c, k_own) * scale
        q_pos = start + jnp.arange(Q_CHUNK)
        k_pos = blk * MOBA_BLOCK + jnp.arange(MOBA_BLOCK)
        s_own = jnp.where(k_pos[None, :] <= q_pos[:, None], s_own, NEG_INF)
        p = jax.nn.softmax(jnp.concatenate([s_sel, s_own], axis=-1), axis=-1)
        p_sel = p[..., :ksel * MOBA_BLOCK].reshape(bsz, nh, Q_CHUNK, ksel, MOBA_BLOCK)
        p_own = p[..., ksel * MOBA_BLOCK:]
        o = jnp.einsum('bhqnk,bhqnkd->bhqd', p_sel, v_sel) + jnp.einsum('bhqk,bhkd->bhqd', p_own, v_own)
        return o.astype(q.dtype)

    o = lax.map(chunk, jnp.arange(seq // Q_CHUNK))
    return o.transpose(1, 0, 3, 2, 4).reshape(bsz, seq, nh * hd)


def _hybrid_mixer(h, cos, sin, w_in, lambda_re, lambda_im, log_dt, b_re, b_im, c_re, c_im,
                  d_skip, w_glu, b_glu, out_norm_ssm_g, out_norm_attn_g, w_out):
    bsz, seq, _ = h.shape
    proj = h @ w_in
    u = proj[..., :SSM_WIDTH]
    qkv = proj[..., SSM_WIDTH:].reshape(bsz, seq, 3, N_HEADS, HEAD_DIM)
    q = qkv[:, :, 0].transpose(0, 2, 1, 3)
    k = qkv[:, :, 1].transpose(0, 2, 1, 3)
    v = qkv[:, :, 2].transpose(0, 2, 1, 3)
    q = _partial_rope(q, cos, sin)
    k = _partial_rope(k, cos, sin)
    attn_out = _moba_attention(q, k, v)
    ssm_out = _s5_mixer(u, lambda_re, lambda_im, log_dt, b_re, b_im, c_re, c_im, d_skip, w_glu, b_glu)
    merged = jnp.concatenate([_rmsnorm(ssm_out, out_norm_ssm_g), _rmsnorm(attn_out, out_norm_attn_g)], axis=-1)
    return merged @ w_out


def _moe(h, router_w, router_b, w_gate, b_gate, w_up, b_up, w_down, b_down):
    bsz, seq, dm = h.shape
    n_tok = bsz * seq
    xf = h.reshape(n_tok, dm)
    logits = (xf @ router_w + router_b).astype(jnp.float32)
    top_vals, top_idx = lax.top_k(logits, TOP_K)
    weights = jax.nn.softmax(top_vals, axis=-1)
    flat_e = top_idx.reshape(-1)
    order = jnp.argsort(flat_e)
    sorted_e = flat_e[order]
    tok = order // TOP_K
    xs = xf[tok]
    group_sizes = jnp.bincount(flat_e, length=N_EXPERTS).astype(jnp.int32)
    gate = lax.ragged_dot(xs, w_gate, group_sizes) + b_gate[sorted_e]
    lin = lax.ragged_dot(xs, w_up, group_sizes) + b_up[sorted_e]
    gate = jnp.minimum(gate, SWIGLU_LIMIT)
    lin = jnp.clip(lin, -SWIGLU_LIMIT, SWIGLU_LIMIT)
    act = gate * jax.nn.sigmoid(SWIGLU_ALPHA * gate) * (lin + 1.0)
    y = lax.ragged_dot(act, w_down, group_sizes) + b_down[sorted_e]
    y = y * weights.reshape(-1)[order][:, None].astype(y.dtype)
    out = jnp.zeros((n_tok, dm), y.dtype).at[tok].add(y)
    return out.reshape(bsz, seq, dm)


def setup_inputs(seed: int = 0) -> dict:
    key = jax.random.key(seed)
    ks = jax.random.split(key, 40)
    f32 = jnp.float32
    nrm = lambda k, shape, s: jax.random.normal(k, shape, f32) * s
    D, L = D_MODEL, DEPTH
    G, P, Hc = SSM_GROUPS, SSM_STATE, SSM_GROUP
    lam_re = -0.5 + nrm(ks[11], (L, G, P), 0.01)
    lam_im = math.pi * jnp.arange(P, dtype=f32)[None, None, :] + nrm(ks[12], (L, G, P), 0.01)
    log_dt = jax.random.uniform(ks[13], (L, G), f32, math.log(0.001), math.log(0.1))
    positions = (jnp.arange(SEQ, dtype=jnp.int32)[None, :]
                 + jax.random.randint(ks[2], (BATCH, 1), 0, 1024, jnp.int32))
    return {
        'x': nrm(ks[0], (BATCH, SEQ, D), 1.0),
        'c': nrm(ks[1], (BATCH, D), 1.0),
        'positions': positions,
        'ada_w': nrm(ks[3], (L, D, N_MOD * D), 0.5 * D ** -0.5),
        'ada_b': nrm(ks[4], (L, N_MOD * D), 0.1),
        'final_ada_w': nrm(ks[5], (D, 2 * D), 0.5 * D ** -0.5),
        'final_ada_b': nrm(ks[6], (2 * D,), 0.1),
        'norm_mix_g': 1.0 + nrm(ks[7], (L, D), 0.02),
        'norm_ffn_g': 1.0 + nrm(ks[8], (L, D), 0.02),
        'norm_final_g': 1.0 + nrm(ks[9], (D,), 0.02),
        'w_in': nrm(ks[10], (L, D, IN_WIDTH), D ** -0.5),
        'ssm_lambda_re': lam_re,
        'ssm_lambda_im': lam_im,
        'ssm_log_dt': log_dt,
        'ssm_b_re': nrm(ks[14], (L, G, P, Hc), (2 * Hc) ** -0.5),
        'ssm_b_im': nrm(ks[15], (L, G, P, Hc), (2 * Hc) ** -0.5),
        'ssm_c_re': nrm(ks[16], (L, G, Hc, P), (2 * P) ** -0.5),
        'ssm_c_im': nrm(ks[17], (L, G, Hc, P), (2 * P) ** -0.5),
        'ssm_d': nrm(ks[18], (L, SSM_WIDTH), 1.0),
        'ssm_w_glu': nrm(ks[19], (L, SSM_WIDTH, SSM_WIDTH), SSM_WIDTH ** -0.5),
        'ssm_b_glu': nrm(ks[20], (L, SSM_WIDTH), 0.01),
        'out_norm_ssm_g': 1.0 + nrm(ks[21], (L, SSM_WIDTH), 0.02),
        'out_norm_attn_g': 1.0 + nrm(ks[22], (L, ATTN_WIDTH), 0.02),
        'w_out': nrm(ks[23], (L, MIX_WIDTH, D), MIX_WIDTH ** -0.5),
        'router_w': nrm(ks[24], (L, D, N_EXPERTS), D ** -0.5),
        'router_b': nrm(ks[25], (L, N_EXPERTS), 0.01),
        'exp_w_gate': nrm(ks[26], (L, N_EXPERTS, D, D_FF), D ** -0.5),
        'exp_b_gate': nrm(ks[27], (L, N_EXPERTS, D_FF), 0.01),
        'exp_w_up': nrm(ks[28], (L, N_EXPERTS, D, D_FF), D ** -0.5),
        'exp_b_up': nrm(ks[29], (L, N_EXPERTS, D_FF), 0.01),
        'exp_w_down': nrm(ks[30], (L, N_EXPERTS, D_FF, D), D_FF ** -0.5),
        'exp_b_down': nrm(ks[31], (L, N_EXPERTS, D), 0.01),
    }


def reference(x, c, positions, ada_w, ada_b, final_ada_w, final_ada_b, norm_mix_g, norm_ffn_g,
              norm_final_g, w_in, ssm_lambda_re, ssm_lambda_im, ssm_log_dt, ssm_b_re, ssm_b_im,
              ssm_c_re, ssm_c_im, ssm_d, ssm_w_glu, ssm_b_glu, out_norm_ssm_g, out_norm_attn_g,
              w_out, router_w, router_b, exp_w_gate, exp_b_gate, exp_w_up, exp_b_up, exp_w_down,
              exp_b_down):
    cos, sin = _rope_tables(positions)
    c_act = jax.nn.silu(c)
    for l in range(DEPTH):
        mod = c_act @ ada_w[l] + ada_b[l]
        sh_m, sc_m, g_m, sh_f, sc_f, g_f = jnp.split(mod, N_MOD, axis=-1)
        h = _modulate(_rmsnorm(x, norm_mix_g[l]), sh_m, sc_m)
        mix = _hybrid_mixer(h, cos, sin, w_in[l], ssm_lambda_re[l], ssm_lambda_im[l], ssm_log_dt[l],
                            ssm_b_re[l], ssm_b_im[l], ssm_c_re[l], ssm_c_im[l], ssm_d[l],
                            ssm_w_glu[l], ssm_b_glu[l], out_norm_ssm_g[l], out_norm_attn_g[l], w_out[l])
        x = x + g_m[:, None, :] * mix
        h = _modulate(_rmsnorm(x, norm_ffn_g[l]), sh_f, sc_f)
        ffn = _moe(h, router_w[l], router_b[l], exp_w_gate[l], exp_b_gate[l], exp_w_up[l],
                   exp_b_up[l], exp_w_down[l], exp_b_down[l])
        x = x + g_f[:, None, :] * ffn
    fmod = c_act @ final_ada_w + final_ada_b
    sh_o, sc_o = jnp.split(fmod, 2, axis=-1)
    return _modulate(_rmsnorm(x, norm_final_g), sh_o, sc_o)
```

```python
import functools
import math

import jax
import jax.numpy as jnp
from jax import lax
from jax.experimental import pallas as pl
from jax.experimental.pallas import tpu as pltpu

F32 = jnp.float32
BF16 = jnp.bfloat16
I32 = jnp.int32
HIGHEST = lax.Precision.HIGHEST

LANES = 128
HEAD_DIM = 64
MOBA_BLOCK = 256
MOBA_TOPK = 3
ROT_DIM = HEAD_DIM // 4
ROPE_THETA = 500000.0
SSM_GROUP = 16
SSM_CHUNK = 16
TOP_K = 4
SWIGLU_ALPHA = 1.702
SWIGLU_LIMIT = 7.0
NORM_EPS = 1e-5
NEG_INF = -1e30
VMEM_LIMIT = 56 * 1024 * 1024


def _cparams(*sem):
    return pltpu.CompilerParams(dimension_semantics=sem, vmem_limit_bytes=VMEM_LIMIT)


def _dot(a, b):
    return jnp.dot(a, b, preferred_element_type=F32)


def _dot_nt(a, b):
    return lax.dot_general(a, b, (((1,), (1,)), ((), ())), preferred_element_type=F32)


def _rms(x, g):
    return x * lax.rsqrt(jnp.mean(x * x, axis=-1, keepdims=True) + NORM_EPS) * g


def _adaln_body(c_ref, w_ref, b_ref, o_ref):
    c = c_ref[...]
    ca = c * jax.nn.sigmoid(c)
    o_ref[...] = jnp.dot(ca, w_ref[...], preferred_element_type=F32, precision=HIGHEST) + b_ref[...]


def _adaln(c, w, b, tn=512):
    bsz, d = c.shape
    n = w.shape[1]
    return pl.pallas_call(
        _adaln_body,
        grid=(n // tn,),
        in_specs=[pl.BlockSpec((bsz, d), lambda j: (0, 0)),
                  pl.BlockSpec((d, tn), lambda j: (0, j)),
                  pl.BlockSpec((1, tn), lambda j: (0, j))],
        out_specs=pl.BlockSpec((bsz, tn), lambda j: (0, j)),
        out_shape=jax.ShapeDtypeStruct((bsz, n), F32),
        compiler_params=_cparams("parallel"),
        name="adaln",
    )(c, w, b.reshape(1, n))


def _inproj_body(x_ref, g_ref, sh_ref, sc_ref, w_ref, cos_ref, sin_ref, u_ref, q_ref, k_ref, v_ref):
    x = x_ref[0]
    tm = x.shape[0]
    h = _rms(x, g_ref[...]) * (1.0 + sc_ref[0]) + sh_ref[0]
    proj = _dot(h.astype(BF16), w_ref[...])
    ssm_w = u_ref.shape[0] * LANES
    aw = q_ref.shape[2]
    for j in range(u_ref.shape[0]):
        u_ref[j, 0] = proj[:, LANES * j:LANES * (j + 1)].astype(BF16)
    reps = aw // LANES
    cosf = jnp.concatenate([cos_ref[0]] * reps, axis=1)
    sinf = jnp.concatenate([sin_ref[0]] * reps, axis=1)
    lane = lax.broadcasted_iota(I32, (tm, aw), 1)
    first_half = (lane & (HEAD_DIM - 1)) < (ROT_DIM // 2)

    def rope(t):
        partner = jnp.where(first_half, pltpu.roll(t, aw - ROT_DIM // 2, 1), pltpu.roll(t, ROT_DIM // 2, 1))
        return t * cosf + partner * sinf

    q = rope(proj[:, ssm_w:ssm_w + aw]) * (HEAD_DIM ** -0.5)
    k = rope(proj[:, ssm_w + aw:ssm_w + 2 * aw])
    q_ref[0] = q.astype(BF16)
    k_ref[0] = k.astype(BF16)
    v_ref[0] = proj[:, ssm_w + 2 * aw:ssm_w + 3 * aw].astype(BF16)


def _inproj(x, gain, mod3, w_in_bf, cos_t, sin_t, ssm_w, attn_w, tm=512):
    bsz, seq, d = x.shape
    n_u = ssm_w // LANES
    tm = min(tm, seq)
    return pl.pallas_call(
        _inproj_body,
        grid=(bsz, seq // tm),
        in_specs=[pl.BlockSpec((1, tm, d), lambda b, i: (b, i, 0)),
                  pl.BlockSpec((1, d), lambda b, i: (0, 0)),
                  pl.BlockSpec((1, 1, d), lambda b, i: (b, 0, 0)),
                  pl.BlockSpec((1, 1, d), lambda b, i: (b, 0, 1)),
                  pl.BlockSpec(w_in_bf.shape, lambda b, i: (0, 0)),
                  pl.BlockSpec((1, tm, LANES), lambda b, i: (b, i, 0)),
                  pl.BlockSpec((1, tm, LANES), lambda b, i: (b, i, 0))],
        out_specs=[pl.BlockSpec((n_u, 1, tm, LANES), lambda b, i: (0, b, i, 0)),
                   pl.BlockSpec((1, tm, attn_w), lambda b, i: (b, i, 0)),
                   pl.BlockSpec((1, tm, attn_w), lambda b, i: (b, i, 0)),
                   pl.BlockSpec((1, tm, attn_w), lambda b, i: (b, i, 0))],
        out_shape=[jax.ShapeDtypeStruct((n_u, bsz, seq, LANES), BF16),
                   jax.ShapeDtypeStruct((bsz, seq, attn_w), BF16),
                   jax.ShapeDtypeStruct((bsz, seq, attn_w), BF16),
                   jax.ShapeDtypeStruct((bsz, seq, attn_w), BF16)],
        compiler_params=_cparams("parallel", "parallel"),
        name="inproj",
    )(x, gain.reshape(1, d), mod3, mod3, w_in_bf, cos_t, sin_t)


def _rope_tables(positions):
    half = ROT_DIM // 2
    inv_freq = ROPE_THETA ** (-jnp.arange(0, ROT_DIM, 2, dtype=F32) / ROT_DIM)
    ang = positions.astype(F32)[:, :, None] * inv_freq[None, None, :]
    cos, sin = jnp.cos(ang), jnp.sin(ang)
    ones = jnp.ones(cos.shape[:2] + (HEAD_DIM - ROT_DIM,), F32)
    cos_h = jnp.concatenate([cos, cos, ones], axis=-1)
    sin_h = jnp.concatenate([-sin, sin, 0.0 * ones], axis=-1)
    return jnp.concatenate([cos_h, cos_h], axis=-1), jnp.concatenate([sin_h, sin_h], axis=-1)


def _s5_params(lam_re, lam_im, log_dt, b_re, b_im, c_re, c_im, d_skip, n_chunks):
    g_all, p = lam_re.shape
    hc = b_re.shape[-1]
    t = SSM_CHUNK
    gpb = LANES // hc
    nblk = g_all // gpb
    lr, li = lam_re.astype(F32), lam_im.astype(F32)
    dt = jnp.exp(log_dt.astype(F32))[:, None]
    ldr, ldi = lr * dt, li * dt

    def apow(k):
        k = jnp.asarray(k, F32)[..., None, None]
        mag = jnp.exp(ldr * k)
        return mag * jnp.cos(ldi * k), mag * jnp.sin(ldi * k)

    ar, ai = apow(1.0)
    zr, zi = ar - 1.0, ai
    den = lr * lr + li * li
    cr = (zr * lr + zi * li) / den
    ci = (zi * lr - zr * li) / den
    bre, bim = b_re.astype(F32), b_im.astype(F32)
    bbr = cr[..., None] * bre - ci[..., None] * bim
    bbi = cr[..., None] * bim + ci[..., None] * bre
    cre, cim = c_re.astype(F32), c_im.astype(F32)
    eye = jnp.eye(gpb, dtype=F32)

    pr, pi = apow(jnp.arange(t))
    abr = pr[..., None] * bbr - pi[..., None] * bbi
    abi = pr[..., None] * bbi + pi[..., None] * bbr
    kf = (jnp.einsum('gnp,kgph->kghn', cre, abr, precision=HIGHEST)
          - jnp.einsum('gnp,kgph->kghn', cim, abi, precision=HIGHEST))
    kblk = jnp.einsum('kjghn,gf->kjghfn', kf.reshape(t, nblk, gpb, hc, hc), eye)
    kblk = kblk.reshape(t, nblk, LANES, LANES)
    kpad = jnp.concatenate([jnp.zeros_like(kblk[:1]), kblk], axis=0)
    pm = jnp.concatenate([kpad[0:t], kpad[1:t + 1]], axis=-1)
    kstack = jnp.transpose(pm[::-1], (1, 0, 2, 3)).reshape(nblk, t * LANES, 2 * LANES)

    def inject(ab):
        ab = ab[::-1].reshape(t, nblk, gpb, p, hc)
        return jnp.einsum('sjgph,gf->jsghfp', ab, eye).reshape(nblk, t * LANES, gpb * p)

    bm = jnp.concatenate([inject(abr), inject(abi)], axis=-1)

    qr, qi = apow(jnp.arange(t) + 1.0)
    c_from_re = cre[None] * qr[:, :, None, :] - cim[None] * qi[:, :, None, :]
    c_from_im = -cre[None] * qi[:, :, None, :] - cim[None] * qr[:, :, None, :]

    def readout(cc):
        cc = cc.reshape(t, nblk, gpb, hc, p)
        return jnp.einsum('tjgnp,gf->jgptfn', cc, eye).reshape(nblk, gpb * p, t * LANES)

    cm = jnp.concatenate([readout(c_from_re), readout(c_from_im)], axis=1)

    n_steps = max(1, int(math.log2(n_chunks)))
    shifts = [float(t * (1 << s)) for s in range(n_steps)]
    sr, si = apow(jnp.asarray(shifts))
    ap = jnp.concatenate([sr.reshape(n_steps, nblk, gpb * p), si.reshape(n_steps, nblk, gpb * p)], axis=-1)
    ap = jnp.transpose(ap, (1, 0, 2))
    dvec = jnp.tile(d_skip.astype(F32).reshape(nblk, 1, LANES), (1, 1, t))
    return kstack.astype(BF16), bm.astype(BF16), cm.astype(BF16), ap, dvec


def _s5_body(x_ref, ks_ref, bm_ref, cm_ref, ap_ref, d_ref, y_ref, h_ref):
    nc = x_ref.shape[2]
    t = SSM_CHUNK
    pad = h_ref.shape[0] - nc
    half = h_ref.shape[1] // 2
    x = x_ref[0, 0]
    h_ref[0:pad] = jnp.zeros((pad, h_ref.shape[1]), F32)
    h_ref[pad:pad + nc] = _dot(x, bm_ref[0])
    for step in range(ap_ref.shape[1]):
        d = 1 << step
        cur = h_ref[pad:pad + nc]
        sft = h_ref[pad - d:pad + nc - d]
        ar = ap_ref[0, step:step + 1, 0:half]
        ai = ap_ref[0, step:step + 1, half:]
        cr, ci = cur[:, :half], cur[:, half:]
        sr, si = sft[:, :half], sft[:, half:]
        h_ref[pad:pad + nc, 0:half] = cr + ar * sr - ai * si
        h_ref[pad:pad + nc, half:] = ci + ar * si + ai * sr
    h_prev = h_ref[pad - 1:pad + nc - 1]
    ys = _dot(h_prev.astype(BF16), cm_ref[0])
    for t0 in range(0, t, 2):
        lo, hi = t0 * LANES, (t0 + 2) * LANES
        conv = _dot(x[:, :hi], ks_ref[0, (t - 2 - t0) * LANES:, :])
        y = conv + ys[:, lo:hi] + d_ref[0][:, lo:hi] * x[:, lo:hi].astype(F32)
        y_ref[0, 0, :, lo:hi] = y.astype(y_ref.dtype)


def _s5(u4, kstack, bm, cm, ap, dvec):
    nblk, bsz, nc, w = u4.shape
    sw = bm.shape[-1]
    pad = max(nc // 2, 8)
    return pl.pallas_call(
        _s5_body,
        grid=(nblk, bsz),
        in_specs=[pl.BlockSpec((1, 1, nc, w), lambda j, b: (j, b, 0, 0)),
                  pl.BlockSpec((1,) + kstack.shape[1:], lambda j, b: (j, 0, 0)),
                  pl.BlockSpec((1,) + bm.shape[1:], lambda j, b: (j, 0, 0)),
                  pl.BlockSpec((1,) + cm.shape[1:], lambda j, b: (j, 0, 0)),
                  pl.BlockSpec((1,) + ap.shape[1:], lambda j, b: (j, 0, 0)),
                  pl.BlockSpec((1, 1, w), lambda j, b: (j, 0, 0))],
        out_specs=pl.BlockSpec((1, 1, nc, w), lambda j, b: (j, b, 0, 0)),
        out_shape=jax.ShapeDtypeStruct((nblk, bsz, nc, w), BF16),
        scratch_shapes=[pltpu.VMEM((pad + nc, sw), F32)],
        compiler_params=_cparams("parallel", "parallel"),
        name="s5",
    )(u4, kstack, bm, cm, ap, dvec)


def _attn_body(q_ref, k_ref, v_ref, o_ref, ka_ref, va_ref, km_ref):
    qi = pl.program_id(2)
    seq = k_ref.shape[1]
    tq = q_ref.shape[1]
    nb = seq // MOBA_BLOCK
    blk_shift = MOBA_BLOCK.bit_length() - 1

    @pl.when(qi == 0)
    def _build():
        k = k_ref[0].astype(F32)
        v = v_ref[0].astype(F32)
        lane = lax.broadcasted_iota(I32, (seq, LANES), 1)
        blk = lax.shift_right_logical(lax.broadcasted_iota(I32, (seq, LANES), 0), blk_shift)
        ka_ref[0] = jnp.where(lane < HEAD_DIM, k, jnp.where(lane - HEAD_DIM == blk, 1.0, 0.0)).astype(BF16)
        ka_ref[1] = jnp.where(lane >= HEAD_DIM, k, jnp.where(lane == blk, 1.0, 0.0)).astype(BF16)
        va_ref[0] = jnp.where(lane < HEAD_DIM, v, jnp.where(lane == HEAD_DIM, 1.0, 0.0)).astype(BF16)
        va_ref[1] = jnp.where(lane >= HEAD_DIM, v, jnp.where(lane == 0, 1.0, 0.0)).astype(BF16)
        km = jnp.mean(k.reshape(nb, MOBA_BLOCK, LANES), axis=1)
        lane_b = lax.broadcasted_iota(I32, (nb, LANES), 1)
        km_ref[...] = jnp.zeros(km_ref.shape, F32)
        km_ref[0, HEAD_DIM:HEAD_DIM + nb, :] = jnp.where(lane_b < HEAD_DIM, km, 0.0)
        km_ref[1, 0:nb, :] = jnp.where(lane_b >= HEAD_DIM, km, 0.0)

    q_f32 = q_ref[0].astype(F32)
    lane_q = lax.broadcasted_iota(I32, (tq, LANES), 1)
    qpos = lax.broadcasted_iota(I32, (tq, MOBA_BLOCK), 0)
    kpos = lax.broadcasted_iota(I32, (tq, MOBA_BLOCK), 1)
    own_off = pl.multiple_of(qi * MOBA_BLOCK, MOBA_BLOCK)

    def head(hx):
        r0 = HEAD_DIM if hx == 0 else 0
        gate_t = _dot_nt(km_ref[hx], q_f32)
        g = gate_t[r0:r0 + nb, :]
        jidx = lax.broadcasted_iota(I32, (nb, tq), 0)
        cnt = jnp.zeros((nb, tq), F32)
        for jp in range(nb):
            row = g[jp:jp + 1, :]
            beats = jnp.where(row > g, 1.0, jnp.where(row == g, jnp.where(jp < jidx, 1.0, 0.0), 0.0))
            cnt = cnt + beats * jnp.where(jp < qi, 1.0, 0.0)
        past_sel = jnp.where(jidx < qi, jnp.where(cnt < MOBA_TOPK, 1.0, 0.0), 0.0)
        sel = jnp.where(jidx == qi, 1.0, past_sel)
        bias = jnp.where(sel > 0.5, 0.0, NEG_INF)
        parts = []
        if r0 > 0:
            parts.append(jnp.zeros((r0, tq), F32))
        parts.append(bias)
        if LANES - r0 - nb > 0:
            parts.append(jnp.zeros((LANES - r0 - nb, tq), F32))
        bias_q = jnp.concatenate(parts, axis=0).T
        data = (lane_q < HEAD_DIM) if hx == 0 else (lane_q >= HEAD_DIM)
        q_aug = jnp.where(data, q_f32, bias_q).astype(BF16)

        s = _dot_nt(q_aug, ka_ref[hx, pl.ds(own_off, MOBA_BLOCK), :])
        s = jnp.where(kpos <= qpos, s, NEG_INF)
        m = jnp.max(s, axis=1, keepdims=True)
        p = jnp.exp(s - m)
        acc = _dot(p.astype(BF16), va_ref[hx, pl.ds(own_off, MOBA_BLOCK), :])

        def body(j, carry):
            m_c, acc_c = carry
            off = pl.multiple_of(j * MOBA_BLOCK, MOBA_BLOCK)
            s_j = _dot_nt(q_aug, ka_ref[hx, pl.ds(off, MOBA_BLOCK), :])
            m_n = jnp.maximum(m_c, jnp.max(s_j, axis=1, keepdims=True))
            alpha = jnp.exp(m_c - m_n)
            p_j = jnp.exp(s_j - m_n)
            acc_n = acc_c * alpha + _dot(p_j.astype(BF16), va_ref[hx, pl.ds(off, MOBA_BLOCK), :])
            return m_n, acc_n

        _, acc = lax.fori_loop(0, qi, body, (m, acc))
        return acc

    acc_a = head(0)
    acc_b = head(1)
    l_a = jnp.sum(jnp.where(lane_q == HEAD_DIM, acc_a, 0.0), axis=1, keepdims=True)
    l_b = jnp.sum(jnp.where(lane_q == 0, acc_b, 0.0), axis=1, keepdims=True)
    o_ref[0] = jnp.where(lane_q < HEAD_DIM, acc_a / l_a, acc_b / l_b).astype(o_ref.dtype)


def _attention(q, k, v):
    bsz, seq, aw = q.shape
    npair = aw // LANES
    tq = MOBA_BLOCK
    nb = seq // MOBA_BLOCK
    assert seq % MOBA_BLOCK == 0 and nb % 8 == 0 and nb <= HEAD_DIM
    return pl.pallas_call(
        _attn_body,
        grid=(bsz, npair, seq // tq),
        in_specs=[pl.BlockSpec((1, tq, LANES), lambda b, p, i: (b, i, p)),
                  pl.BlockSpec((1, seq, LANES), lambda b, p, i: (b, 0, p)),
                  pl.BlockSpec((1, seq, LANES), lambda b, p, i: (b, 0, p))],
        out_specs=pl.BlockSpec((1, tq, LANES), lambda b, p, i: (b, i, p)),
        out_shape=jax.ShapeDtypeStruct((bsz, seq, aw), BF16),
        scratch_shapes=[pltpu.VMEM((2, seq, LANES), BF16),
                        pltpu.VMEM((2, seq, LANES), BF16),
                        pltpu.VMEM((2, LANES, LANES), F32)],
        compiler_params=_cparams("parallel", "parallel", "arbitrary"),
        name="attn",
    )(q, k, v)


def _merge_body(y_ref, a_ref, x_ref, wglu_ref, bglu_ref, gs_ref, ga_ref, wout_ref, gm_ref,
                gf_ref, shf_ref, scf_ref, rw_ref, rb_ref, x1_ref, h2_ref, idx_ref, wt_ref):
    nblk = y_ref.shape[0]
    y = jnp.concatenate([y_ref[j, 0] for j in range(nblk)], axis=1).astype(F32)
    tm = y.shape[0]
    g = 0.5 * y * (1.0 + lax.erf(y * (2.0 ** -0.5)))
    glu = g * jax.nn.sigmoid(_dot(g.astype(BF16), wglu_ref[...]) + bglu_ref[...])
    ssm_n = _rms(glu, gs_ref[...])
    att_n = _rms(a_ref[0].astype(F32), ga_ref[...])
    merged = jnp.concatenate([ssm_n, att_n], axis=1).astype(BF16)
    mix = _dot(merged, wout_ref[...])
    x1 = x_ref[0] + gm_ref[0] * mix
    x1_ref[0] = x1
    h2 = _rms(x1, gf_ref[...]) * (1.0 + scf_ref[0]) + shf_ref[0]
    h2_ref[0] = h2
    logits = jnp.dot(h2, rw_ref[...], preferred_element_type=F32, precision=HIGHEST) + rb_ref[...]
    ne = logits.shape[1]
    lane = lax.broadcasted_iota(I32, (tm, ne), 1)
    vals, idxs = [], []
    for _ in range(TOP_K):
        mx = jnp.max(logits, axis=1, keepdims=True)
        ix = jnp.min(jnp.where(logits == mx, lane, ne), axis=1, keepdims=True)
        vals.append(mx)
        idxs.append(ix)
        logits = jnp.where(lane == ix, -jnp.inf, logits)
    exps = [jnp.exp(vv - vals[0]) for vv in vals]
    tot = exps[0]
    for e in exps[1:]:
        tot = tot + e
    lane_k = lax.broadcasted_iota(I32, (tm, TOP_K), 1)
    idx_out = jnp.zeros((tm, TOP_K), I32)
    wt_out = jnp.zeros((tm, TOP_K), F32)
    for kk in range(TOP_K):
        idx_out = jnp.where(lane_k == kk, idxs[kk], idx_out)
        wt_out = jnp.where(lane_k == kk, exps[kk] / tot, wt_out)
    idx_ref[0] = idx_out
    wt_ref[0] = wt_out


def _merge(y4, attn, x, w_glu_bf, b_glu, g_ssm, g_attn, w_out_bf, mod3, g_ffn, router_w, router_b, tm=512):
    bsz, seq, d = x.shape
    nblk = y4.shape[0]
    ssm_w = nblk * LANES
    aw = attn.shape[-1]
    ne = router_w.shape[1]
    tm = min(tm, seq)
    row = lambda n: pl.BlockSpec((1, n), lambda b, i: (0, 0))
    full = lambda a: pl.BlockSpec(a.shape, lambda b, i: (0,) * a.ndim)
    modv = lambda j: pl.BlockSpec((1, 1, d), lambda b, i: (b, 0, j))
    return pl.pallas_call(
        _merge_body,
        grid=(bsz, seq // tm),
        in_specs=[pl.BlockSpec((nblk, 1, tm, LANES), lambda b, i: (0, b, i, 0)),
                  pl.BlockSpec((1, tm, aw), lambda b, i: (b, i, 0)),
                  pl.BlockSpec((1, tm, d), lambda b, i: (b, i, 0)),
                  full(w_glu_bf), row(ssm_w), row(ssm_w), row(aw), full(w_out_bf),
                  modv(2), row(d), modv(3), modv(4), full(router_w), row(ne)],
        out_specs=[pl.BlockSpec((1, tm, d), lambda b, i: (b, i, 0)),
                   pl.BlockSpec((1, tm, d), lambda b, i: (b, i, 0)),
                   pl.BlockSpec((1, tm, TOP_K), lambda b, i: (b, i, 0)),
                   pl.BlockSpec((1, tm, TOP_K), lambda b, i: (b, i, 0))],
        out_shape=[jax.ShapeDtypeStruct((bsz, seq, d), F32),
                   jax.ShapeDtypeStruct((bsz, seq, d), F32),
                   jax.ShapeDtypeStruct((bsz, seq, TOP_K), I32),
                   jax.ShapeDtypeStruct((bsz, seq, TOP_K), F32)],
        compiler_params=_cparams("parallel", "parallel"),
        name="merge",
    )(y4, attn, x, w_glu_bf, b_glu.reshape(1, ssm_w), g_ssm.reshape(1, ssm_w), g_attn.reshape(1, aw),
      w_out_bf, mod3, g_ffn.reshape(1, d), mod3, mod3, router_w, router_b.reshape(1, ne))


def _rank_body(idx_ref, rank_ref, cnt_ref, carry_ref, *, ne):
    i = pl.program_id(0)
    tm = idx_ref.shape[0]

    @pl.when(i == 0)
    def _init():
        carry_ref[...] = jnp.zeros(carry_ref.shape, F32)

    idx = idx_ref[...]
    lane = lax.broadcasted_iota(I32, (tm, ne), 1)
    onehot = jnp.zeros((tm, ne), F32)
    for kk in range(TOP_K):
        onehot = onehot + jnp.where(idx[:, kk:kk + 1] == lane, 1.0, 0.0)
    r = lax.broadcasted_iota(I32, (tm, tm), 0)
    c = lax.broadcasted_iota(I32, (tm, tm), 1)
    tri = jnp.where(c < r, 1.0, 0.0).astype(BF16)
    before = _dot(tri, onehot.astype(BF16)) + carry_ref[...]
    lane_k = lax.broadcasted_iota(I32, (tm, TOP_K), 1)
    rank = jnp.zeros((tm, TOP_K), F32)
    for kk in range(TOP_K):
        rk = jnp.sum(jnp.where(idx[:, kk:kk + 1] == lane, before, 0.0), axis=1, keepdims=True)
        rank = jnp.where(lane_k == kk, rk, rank)
    rank_ref[...] = rank.astype(I32)
    carry_ref[...] = carry_ref[...] + jnp.sum(onehot, axis=0, keepdims=True)
    cnt_ref[...] = carry_ref[...].astype(I32)


def _rank(idx, ne, tm=512):
    n = idx.shape[0]
    tm = min(tm, n)
    return pl.pallas_call(
        functools.partial(_rank_body, ne=ne),
        grid=(n // tm,),
        in_specs=[pl.BlockSpec((tm, TOP_K), lambda i: (i, 0))],
        out_specs=[pl.BlockSpec((tm, TOP_K), lambda i: (i, 0)),
                   pl.BlockSpec((1, ne), lambda i: (0, 0))],
        out_shape=[jax.ShapeDtypeStruct((n, TOP_K), I32), jax.ShapeDtypeStruct((1, ne), I32)],
        scratch_shapes=[pltpu.VMEM((1, ne), F32)],
        compiler_params=_cparams("arbitrary"),
        name="rank",
    )(idx)


def _dispatch_body(pos_ref, h_ref, xs_ref, sem):
    tm = h_ref.shape[0]

    def copy(r, kk):
        dst = pos_ref[r * TOP_K + kk]
        return pltpu.make_async_copy(h_ref.at[pl.ds(r, 1), :], xs_ref.at[pl.ds(dst, 1), :], sem)

    def issue(r, _):
        for kk in range(TOP_K):
            copy(r, kk).start()
        return 0

    def drain(r, _):
        for kk in range(TOP_K):
            copy(r, kk).wait()
        return 0

    lax.fori_loop(0, tm, issue, 0)
    lax.fori_loop(0, tm, drain, 0)


def _dispatch(h2, pos_flat, tm=512):
    n, d = h2.shape
    tm = min(tm, n)
    return pl.pallas_call(
        _dispatch_body,
        grid=(n // tm,),
        in_specs=[pl.BlockSpec((tm * TOP_K,), lambda i: (i,), memory_space=pltpu.SMEM),
                  pl.BlockSpec((tm, d), lambda i: (i, 0))],
        out_specs=pl.BlockSpec(memory_space=pl.ANY),
        out_shape=jax.ShapeDtypeStruct((n * TOP_K, d), h2.dtype),
        scratch_shapes=[pltpu.SemaphoreType.DMA(())],
        compiler_params=pltpu.CompilerParams(dimension_semantics=("arbitrary",), vmem_limit_bytes=VMEM_LIMIT,
                                             has_side_effects=True),
        name="dispatch",
    )(pos_flat, h2)


def _experts_body(tile_ref, exp_ref, lo_ref, hi_ref, first_ref, newe_ref,
                  xs_ref, wg_ref, bg_ref, wu_ref, bu_ref, wd_ref, bd_ref, ys_ref,
                  wg_bf, wu_bf, wd_bf):
    i = pl.program_id(0)
    lo = lo_ref[i]
    hi = hi_ref[i]

    @pl.when(newe_ref[i] == 1)
    def _cast():
        wg_bf[...] = wg_ref[0].astype(BF16)
        wu_bf[...] = wu_ref[0].astype(BF16)
        wd_bf[...] = wd_ref[0].astype(BF16)

    @pl.when(hi > lo)
    def _compute():
        x = xs_ref[...].astype(BF16)
        gate = jnp.minimum(_dot(x, wg_bf[...]) + bg_ref[0], SWIGLU_LIMIT)
        lin = jnp.clip(_dot(x, wu_bf[...]) + bu_ref[0], -SWIGLU_LIMIT, SWIGLU_LIMIT)
        act = gate * jax.nn.sigmoid(SWIGLU_ALPHA * gate) * (lin + 1.0)
        y = _dot(act.astype(BF16), wd_bf[...]) + bd_ref[0]
        row = lax.broadcasted_iota(I32, y.shape, 0)
        mine = (row >= lo) & (row < hi)

        @pl.when(first_ref[i] == 1)
        def _():
            ys_ref[...] = jnp.where(mine, y, 0.0)

        @pl.when(first_ref[i] == 0)
        def _():
            ys_ref[...] = jnp.where(mine, y, ys_ref[...])


def _experts(xs, items, w_gate, b_gate, w_up, b_up, w_down, b_down, tg):
    nk, d = xs.shape
    ne, _, dff = w_gate.shape
    n_items = items[0].shape[0]
    wspec = lambda shp: pl.BlockSpec((1,) + shp, lambda i, t, e, lo, hi, f, nw: (e[i], 0, 0))
    return pl.pallas_call(
        _experts_body,
        grid_spec=pltpu.PrefetchScalarGridSpec(
            num_scalar_prefetch=6,
            grid=(n_items,),
            in_specs=[pl.BlockSpec((tg, d), lambda i, t, e, lo, hi, f, nw: (t[i], 0)),
                      wspec((d, dff)), wspec((1, dff)),
                      wspec((d, dff)), wspec((1, dff)),
                      wspec((dff, d)), wspec((1, d))],
            out_specs=pl.BlockSpec((tg, d), lambda i, t, e, lo, hi, f, nw: (t[i], 0)),
            scratch_shapes=[pltpu.VMEM((d, dff), BF16), pltpu.VMEM((d, dff), BF16), pltpu.VMEM((dff, d), BF16)],
        ),
        out_shape=jax.ShapeDtypeStruct((nk, d), F32),
        compiler_params=_cparams("arbitrary"),
        name="experts",
    )(*items, xs, w_gate, b_gate.reshape(ne, 1, dff), w_up, b_up.reshape(ne, 1, dff),
      w_down, b_down.reshape(ne, 1, d))


def _plan_items(counts, nk, tg):
    ne = counts.shape[0]
    n_tiles = nk // tg
    n_items = n_tiles + ne
    ends = jnp.cumsum(counts)
    offs = ends - counts
    first_tile = offs // tg
    last_tile = jnp.maximum(ends - 1, 0) // tg
    per_e = jnp.where(counts > 0, last_tile - first_tile + 1, 0)
    item_end = jnp.cumsum(per_e)
    item_start = item_end - per_e
    total = item_end[-1]
    ii = jnp.arange(n_items, dtype=I32)
    e_of = jnp.minimum(jnp.searchsorted(item_end, ii, side='right'), ne - 1).astype(I32)
    valid = ii < total
    last_e = jnp.minimum(jnp.searchsorted(item_end, total - 1, side='right'), ne - 1).astype(I32)
    e_of = jnp.where(valid, e_of, last_e)
    tile = jnp.where(valid, first_tile[e_of] + (ii - item_start[e_of]), n_tiles - 1).astype(I32)
    lo = jnp.clip(offs[e_of] - tile * tg, 0, tg)
    hi = jnp.clip(ends[e_of] - tile * tg, 0, tg)
    lo = jnp.where(valid, lo, 0).astype(I32)
    hi = jnp.where(valid, hi, 0).astype(I32)
    prev_tile = jnp.concatenate([jnp.full((1,), -1, I32), tile[:-1]])
    prev_e = jnp.concatenate([jnp.full((1,), -1, I32), e_of[:-1]])
    first = (tile != prev_tile).astype(I32)
    newe = (e_of != prev_e).astype(I32)
    return (tile, e_of, lo, hi, first, newe), offs


def _combine_body(pos_ref, x1_ref, wt_ref, gf_ref, gfin_ref, sho_ref, sco_ref, ys_ref, o_ref, buf, sem):
    tm = x1_ref.shape[1]

    def copy(r, kk):
        src = pos_ref[r * TOP_K + kk]
        return pltpu.make_async_copy(ys_ref.at[pl.ds(src, 1), :], buf.at[kk, pl.ds(r, 1), :], sem)

    def issue(r, _):
        for kk in range(TOP_K):
            copy(r, kk).start()
        return 0

    def drain(r, _):
        for kk in range(TOP_K):
            copy(r, kk).wait()
        return 0

    lax.fori_loop(0, tm, issue, 0)
    lax.fori_loop(0, tm, drain, 0)
    wt = wt_ref[0]
    ffn = wt[:, 0:1] * buf[0]
    for kk in range(1, TOP_K):
        ffn = ffn + wt[:, kk:kk + 1] * buf[kk]
    x2 = x1_ref[0] + gf_ref[0] * ffn
    o_ref[0] = _rms(x2, gfin_ref[...]) * (1.0 + sco_ref[0]) + sho_ref[0]


def _combine(pos_flat, x1, wt, mod3, g_final, fmod3, ys, tm=256):
    bsz, seq, d = x1.shape
    tm = min(tm, seq)
    spb = seq // tm
    return pl.pallas_call(
        _combine_body,
        grid=(bsz, spb),
        in_specs=[pl.BlockSpec((tm * TOP_K,), lambda b, i: (b * spb + i,), memory_space=pltpu.SMEM),
                  pl.BlockSpec((1, tm, d), lambda b, i: (b, i, 0)),
                  pl.BlockSpec((1, tm, TOP_K), lambda b, i: (b, i, 0)),
                  pl.BlockSpec((1, 1, d), lambda b, i: (b, 0, 5)),
                  pl.BlockSpec((1, d), lambda b, i: (0, 0)),
                  pl.BlockSpec((1, 1, d), lambda b, i: (b, 0, 0)),
                  pl.BlockSpec((1, 1, d), lambda b, i: (b, 0, 1)),
                  pl.BlockSpec(memory_space=pl.ANY)],
        out_specs=pl.BlockSpec((1, tm, d), lambda b, i: (b, i, 0)),
        out_shape=jax.ShapeDtypeStruct((bsz, seq, d), F32),
        scratch_shapes=[pltpu.VMEM((TOP_K, tm, d), F32), pltpu.SemaphoreType.DMA(())],
        compiler_params=_cparams("arbitrary", "arbitrary"),
        name="combine",
    )(pos_flat, x1, wt, mod3, g_final.reshape(1, d), fmod3, fmod3, ys)


def kernel(x, c, positions, ada_w, ada_b, final_ada_w, final_ada_b, norm_mix_g, norm_ffn_g, norm_final_g, w_in, ssm_lambda_re, ssm_lambda_im, ssm_log_dt, ssm_b_re, ssm_b_im, ssm_c_re, ssm_c_im, ssm_d, ssm_w_glu, ssm_b_glu, out_norm_ssm_g, out_norm_attn_g, w_out, router_w, router_b, exp_w_gate, exp_b_gate, exp_w_up, exp_b_up, exp_w_down, exp_b_down):
    bsz, seq, d = x.shape
    depth = ada_w.shape[0]
    ssm_w = ssm_d.shape[-1]
    attn_w = (w_in.shape[-1] - ssm_w) // 3
    ne = router_w.shape[-1]
    n_tok = bsz * seq
    nc = seq // SSM_CHUNK
    tg = 256

    cos_t, sin_t = _rope_tables(positions)
    fmod3 = _adaln(c, final_ada_w, final_ada_b).reshape(bsz, 1, 2 * d)
    for l in range(depth):
        mod3 = _adaln(c, ada_w[l], ada_b[l]).reshape(bsz, 1, -1)
        u4, q, k, v = _inproj(x, norm_mix_g[l], mod3, w_in[l].astype(BF16), cos_t, sin_t, ssm_w, attn_w)
        tables = _s5_params(ssm_lambda_re[l], ssm_lambda_im[l], ssm_log_dt[l], ssm_b_re[l], ssm_b_im[l],
                            ssm_c_re[l], ssm_c_im[l], ssm_d[l], nc)
        nblk = ssm_w // LANES
        y4 = _s5(u4.reshape(nblk, bsz, nc, SSM_CHUNK * LANES), *tables).reshape(nblk, bsz, seq, LANES)
        attn = _attention(q, k, v)
        x1, h2, idx, wt = _merge(y4, attn, x, ssm_w_glu[l].astype(BF16), ssm_b_glu[l], out_norm_ssm_g[l],
                                 out_norm_attn_g[l], w_out[l].astype(BF16), mod3, norm_ffn_g[l],
                                 router_w[l], router_b[l])
        idx2 = idx.reshape(n_tok, TOP_K)
        rank, counts = _rank(idx2, ne)
        items, offs = _plan_items(counts[0], n_tok * TOP_K, tg)
        pos_flat = (offs[idx2] + rank).reshape(-1).astype(I32)
        xs = _dispatch(h2.reshape(n_tok, d), pos_flat)
        ys = _experts(xs, items, exp_w_gate[l], exp_b_gate[l], exp_w_up[l], exp_b_up[l],
                      exp_w_down[l], exp_b_down[l], tg)
        if l + 1 < depth:
            raise NotImplementedError("depth > 1 needs the non-final combine")
        x = _combine(pos_flat, x1, wt, mod3, norm_final_g, fmod3, ys)
    return x
```

```python
import functools
import math

import jax
import jax.numpy as jnp
from jax import lax
from jax.experimental import pallas as pl
from jax.experimental.pallas import tpu as pltpu

F32 = jnp.float32
BF16 = jnp.bfloat16
I32 = jnp.int32
HIGHEST = lax.Precision.HIGHEST

LANES = 128
HEAD_DIM = 64
MOBA_BLOCK = 256
MOBA_TOPK = 3
ROT_DIM = HEAD_DIM // 4
ROPE_THETA = 500000.0
SSM_GROUP = 16
SSM_CHUNK = 16
TOP_K = 4
SWIGLU_ALPHA = 1.702
SWIGLU_LIMIT = 7.0
NORM_EPS = 1e-5
NEG_INF = -1e30
KV_TILE = 2 * MOBA_BLOCK
VMEM_LIMIT = 56 * 1024 * 1024


def _cparams(*sem):
    return pltpu.CompilerParams(dimension_semantics=sem, vmem_limit_bytes=VMEM_LIMIT)


def _dot(a, b):
    return jnp.dot(a, b, preferred_element_type=F32)


def _dot_nt(a, b):
    return lax.dot_general(a, b, (((1,), (1,)), ((), ())), preferred_element_type=F32)


def _rms(x, g):
    return x * lax.rsqrt(jnp.mean(x * x, axis=-1, keepdims=True) + NORM_EPS) * g


def _adaln_body(c_ref, w_ref, b_ref, o_ref):
    c = c_ref[...]
    ca = c * jax.nn.sigmoid(c)
    o_ref[...] = jnp.dot(ca, w_ref[...], preferred_element_type=F32, precision=HIGHEST) + b_ref[...]


def _adaln(c, w, b, tn=512):
    bsz, d = c.shape
    n = w.shape[1]
    return pl.pallas_call(
        _adaln_body,
        grid=(n // tn,),
        in_specs=[pl.BlockSpec((bsz, d), lambda j: (0, 0)),
                  pl.BlockSpec((d, tn), lambda j: (0, j)),
                  pl.BlockSpec((1, tn), lambda j: (0, j))],
        out_specs=pl.BlockSpec((bsz, tn), lambda j: (0, j)),
        out_shape=jax.ShapeDtypeStruct((bsz, n), F32),
        compiler_params=_cparams("parallel"),
        name="adaln",
    )(c, w, b.reshape(1, n))


def _inproj_body(x_ref, g_ref, sh_ref, sc_ref, w_ref, cos_ref, sin_ref, u_ref, q_ref, k_ref, v_ref):
    x = x_ref[0]
    tm = x.shape[0]
    h = _rms(x, g_ref[...]) * (1.0 + sc_ref[0]) + sh_ref[0]
    proj = _dot(h.astype(BF16), w_ref[...])
    ssm_w = u_ref.shape[0] * LANES
    aw = q_ref.shape[2]
    for j in range(u_ref.shape[0]):
        u_ref[j, 0] = proj[:, LANES * j:LANES * (j + 1)].astype(BF16)
    reps = aw // LANES
    cosf = jnp.concatenate([cos_ref[0]] * reps, axis=1)
    sinf = jnp.concatenate([sin_ref[0]] * reps, axis=1)
    lane = lax.broadcasted_iota(I32, (tm, aw), 1)
    first_half = (lane & (HEAD_DIM - 1)) < (ROT_DIM // 2)

    def rope(t):
        partner = jnp.where(first_half, pltpu.roll(t, aw - ROT_DIM // 2, 1), pltpu.roll(t, ROT_DIM // 2, 1))
        return t * cosf + partner * sinf

    q = rope(proj[:, ssm_w:ssm_w + aw]) * (HEAD_DIM ** -0.5)
    k = rope(proj[:, ssm_w + aw:ssm_w + 2 * aw])
    q_ref[0] = q.astype(BF16)
    k_ref[0] = k.astype(BF16)
    v_ref[0] = proj[:, ssm_w + 2 * aw:ssm_w + 3 * aw].astype(BF16)


def _inproj(x, gain, mod3, w_in_bf, cos_t, sin_t, ssm_w, attn_w, tm=512):
    bsz, seq, d = x.shape
    n_u = ssm_w // LANES
    tm = min(tm, seq)
    return pl.pallas_call(
        _inproj_body,
        grid=(bsz, seq // tm),
        in_specs=[pl.BlockSpec((1, tm, d), lambda b, i: (b, i, 0)),
                  pl.BlockSpec((1, d), lambda b, i: (0, 0)),
                  pl.BlockSpec((1, 1, d), lambda b, i: (b, 0, 0)),
                  pl.BlockSpec((1, 1, d), lambda b, i: (b, 0, 1)),
                  pl.BlockSpec(w_in_bf.shape, lambda b, i: (0, 0)),
                  pl.BlockSpec((1, tm, LANES), lambda b, i: (b, i, 0)),
                  pl.BlockSpec((1, tm, LANES), lambda b, i: (b, i, 0))],
        out_specs=[pl.BlockSpec((n_u, 1, tm, LANES), lambda b, i: (0, b, i, 0)),
                   pl.BlockSpec((1, tm, attn_w), lambda b, i: (b, i, 0)),
                   pl.BlockSpec((1, tm, attn_w), lambda b, i: (b, i, 0)),
                   pl.BlockSpec((1, tm, attn_w), lambda b, i: (b, i, 0))],
        out_shape=[jax.ShapeDtypeStruct((n_u, bsz, seq, LANES), BF16),
                   jax.ShapeDtypeStruct((bsz, seq, attn_w), BF16),
                   jax.ShapeDtypeStruct((bsz, seq, attn_w), BF16),
                   jax.ShapeDtypeStruct((bsz, seq, attn_w), BF16)],
        compiler_params=_cparams("parallel", "parallel"),
        name="inproj",
    )(x, gain.reshape(1, d), mod3, mod3, w_in_bf, cos_t, sin_t)


def _rope_tables(positions):
    half = ROT_DIM // 2
    inv_freq = ROPE_THETA ** (-jnp.arange(0, ROT_DIM, 2, dtype=F32) / ROT_DIM)
    ang = positions.astype(F32)[:, :, None] * inv_freq[None, None, :]
    cos, sin = jnp.cos(ang), jnp.sin(ang)
    ones = jnp.ones(cos.shape[:2] + (HEAD_DIM - ROT_DIM,), F32)
    cos_h = jnp.concatenate([cos, cos, ones], axis=-1)
    sin_h = jnp.concatenate([-sin, sin, 0.0 * ones], axis=-1)
    return jnp.concatenate([cos_h, cos_h], axis=-1), jnp.concatenate([sin_h, sin_h], axis=-1)


def _s5_params(lam_re, lam_im, log_dt, b_re, b_im, c_re, c_im, d_skip, n_chunks):
    g_all, p = lam_re.shape
    hc = b_re.shape[-1]
    t = SSM_CHUNK
    gpb = LANES // hc
    nblk = g_all // gpb
    lr, li = lam_re.astype(F32), lam_im.astype(F32)
    dt = jnp.exp(log_dt.astype(F32))[:, None]
    ldr, ldi = lr * dt, li * dt

    def apow(k):
        k = jnp.asarray(k, F32)[..., None, None]
        mag = jnp.exp(ldr * k)
        return mag * jnp.cos(ldi * k), mag * jnp.sin(ldi * k)

    ar, ai = apow(1.0)
    zr, zi = ar - 1.0, ai
    den = lr * lr + li * li
    cr = (zr * lr + zi * li) / den
    ci = (zi * lr - zr * li) / den
    bre, bim = b_re.astype(F32), b_im.astype(F32)
    bbr = cr[..., None] * bre - ci[..., None] * bim
    bbi = cr[..., None] * bim + ci[..., None] * bre
    cre, cim = c_re.astype(F32), c_im.astype(F32)
    eye = jnp.eye(gpb, dtype=F32)

    pr, pi = apow(jnp.arange(t))
    abr = pr[..., None] * bbr - pi[..., None] * bbi
    abi = pr[..., None] * bbi + pi[..., None] * bbr
    kf = (jnp.einsum('gnp,kgph->kghn', cre, abr, precision=HIGHEST)
          - jnp.einsum('gnp,kgph->kghn', cim, abi, precision=HIGHEST))
    kblk = jnp.einsum('kjghn,gf->kjghfn', kf.reshape(t, nblk, gpb, hc, hc), eye)
    kblk = kblk.reshape(t, nblk, LANES, LANES)
    kpad = jnp.concatenate([jnp.zeros_like(kblk[:1]), kblk], axis=0)
    pm = jnp.concatenate([kpad[0:t], kpad[1:t + 1]], axis=-1)
    kstack = jnp.transpose(pm[::-1], (1, 0, 2, 3)).reshape(nblk, t * LANES, 2 * LANES)

    def inject(ab):
        ab = ab[::-1].reshape(t, nblk, gpb, p, hc)
        return jnp.einsum('sjgph,gf->jsghfp', ab, eye).reshape(nblk, t * LANES, gpb * p)

    bm = jnp.concatenate([inject(abr), inject(abi)], axis=-1)

    qr, qi = apow(jnp.arange(t) + 1.0)
    c_from_re = cre[None] * qr[:, :, None, :] - cim[None] * qi[:, :, None, :]
    c_from_im = -cre[None] * qi[:, :, None, :] - cim[None] * qr[:, :, None, :]

    def readout(cc):
        cc = cc.reshape(t, nblk, gpb, hc, p)
        return jnp.einsum('tjgnp,gf->jgptfn', cc, eye).reshape(nblk, gpb * p, t * LANES)

    cm = jnp.concatenate([readout(c_from_re), readout(c_from_im)], axis=1)

    n_steps = max(1, int(math.log2(n_chunks)))
    shifts = [float(t * (1 << s)) for s in range(n_steps)]
    sr, si = apow(jnp.asarray(shifts))
    ap = jnp.concatenate([sr.reshape(n_steps, nblk, gpb * p), si.reshape(n_steps, nblk, gpb * p)], axis=-1)
    ap = jnp.transpose(ap, (1, 0, 2))
    dvec = jnp.tile(d_skip.astype(F32).reshape(nblk, 1, LANES), (1, 1, t))
    return kstack.astype(BF16), bm.astype(BF16), cm.astype(BF16), ap, dvec


def _s5_body(x_ref, ks_ref, bm_ref, cm_ref, ap_ref, d_ref, y_ref, h_ref):
    nc = x_ref.shape[2]
    t = SSM_CHUNK
    pad = h_ref.shape[0] - nc
    half = h_ref.shape[1] // 2
    x = x_ref[0, 0]
    h_ref[0:pad] = jnp.zeros((pad, h_ref.shape[1]), F32)
    h_ref[pad:pad + nc] = _dot(x, bm_ref[0])
    for step in range(ap_ref.shape[1]):
        d = 1 << step
        cur = h_ref[pad:pad + nc]
        sft = h_ref[pad - d:pad + nc - d]
        ar = ap_ref[0, step:step + 1, 0:half]
        ai = ap_ref[0, step:step + 1, half:]
        cr, ci = cur[:, :half], cur[:, half:]
        sr, si = sft[:, :half], sft[:, half:]
        h_ref[pad:pad + nc, 0:half] = cr + ar * sr - ai * si
        h_ref[pad:pad + nc, half:] = ci + ar * si + ai * sr
    h_prev = h_ref[pad - 1:pad + nc - 1]
    ys = _dot(h_prev.astype(BF16), cm_ref[0])
    for t0 in range(0, t, 2):
        lo, hi = t0 * LANES, (t0 + 2) * LANES
        conv = _dot(x[:, :hi], ks_ref[0, (t - 2 - t0) * LANES:, :])
        y = conv + ys[:, lo:hi] + d_ref[0][:, lo:hi] * x[:, lo:hi].astype(F32)
        y_ref[0, 0, :, lo:hi] = y.astype(y_ref.dtype)


def _s5(u4, kstack, bm, cm, ap, dvec):
    nblk, bsz, nc, w = u4.shape
    sw = bm.shape[-1]
    pad = max(nc // 2, 8)
    return pl.pallas_call(
        _s5_body,
        grid=(nblk, bsz),
        in_specs=[pl.BlockSpec((1, 1, nc, w), lambda j, b: (j, b, 0, 0)),
                  pl.BlockSpec((1,) + kstack.shape[1:], lambda j, b: (j, 0, 0)),
                  pl.BlockSpec((1,) + bm.shape[1:], lambda j, b: (j, 0, 0)),
                  pl.BlockSpec((1,) + cm.shape[1:], lambda j, b: (j, 0, 0)),
                  pl.BlockSpec((1,) + ap.shape[1:], lambda j, b: (j, 0, 0)),
                  pl.BlockSpec((1, 1, w), lambda j, b: (j, 0, 0))],
        out_specs=pl.BlockSpec((1, 1, nc, w), lambda j, b: (j, b, 0, 0)),
        out_shape=jax.ShapeDtypeStruct((nblk, bsz, nc, w), BF16),
        scratch_shapes=[pltpu.VMEM((pad + nc, sw), F32)],
        compiler_params=_cparams("parallel", "parallel"),
        name="s5",
    )(u4, kstack, bm, cm, ap, dvec)


def _attn_body(q_ref, k_ref, v_ref, o_ref, ka_ref, va_ref, km_ref):
    qi = pl.program_id(2)
    seq = k_ref.shape[1]
    tq = q_ref.shape[1]
    nb = seq // MOBA_BLOCK
    blk_shift = MOBA_BLOCK.bit_length() - 1

    @pl.when(qi == 0)
    def _build():
        k = k_ref[0].astype(F32)
        v = v_ref[0].astype(F32)
        lane = lax.broadcasted_iota(I32, (seq, LANES), 1)
        blk = lax.shift_right_logical(lax.broadcasted_iota(I32, (seq, LANES), 0), blk_shift)
        ka_ref[0] = jnp.where(lane < HEAD_DIM, k, jnp.where(lane - HEAD_DIM == blk, 1.0, 0.0)).astype(BF16)
        ka_ref[1] = jnp.where(lane >= HEAD_DIM, k, jnp.where(lane == blk, 1.0, 0.0)).astype(BF16)
        va_ref[0] = jnp.where(lane < HEAD_DIM, v, jnp.where(lane == HEAD_DIM, 1.0, 0.0)).astype(BF16)
        va_ref[1] = jnp.where(lane >= HEAD_DIM, v, jnp.where(lane == 0, 1.0, 0.0)).astype(BF16)
        km = jnp.mean(k.reshape(nb, MOBA_BLOCK, LANES), axis=1)
        lane_b = lax.broadcasted_iota(I32, (nb, LANES), 1)
        km_ref[...] = jnp.zeros(km_ref.shape, F32)
        km_ref[0, HEAD_DIM:HEAD_DIM + nb, :] = jnp.where(lane_b < HEAD_DIM, km, 0.0)
        km_ref[1, 0:nb, :] = jnp.where(lane_b >= HEAD_DIM, km, 0.0)

    q_f32 = q_ref[0].astype(F32)
    lane_q = lax.broadcasted_iota(I32, (tq, LANES), 1)

    def augmented_q(hx):
        r0 = HEAD_DIM if hx == 0 else 0
        gate_t = _dot_nt(km_ref[hx], q_f32)
        g = gate_t[r0:r0 + nb, :]
        jidx = lax.broadcasted_iota(I32, (nb, tq), 0)
        qblk = qi * (tq // MOBA_BLOCK) + lax.shift_right_logical(
            lax.broadcasted_iota(I32, (nb, tq), 1), blk_shift)
        cnt = jnp.zeros((nb, tq), F32)
        for jp in range(nb):
            row = g[jp:jp + 1, :]
            beats = jnp.where(row > g, 1.0, jnp.where(row == g, jnp.where(jp < jidx, 1.0, 0.0), 0.0))
            cnt = cnt + jnp.where(jp < qblk, beats, 0.0)
        past_sel = jnp.where(jidx < qblk, jnp.where(cnt < MOBA_TOPK, 1.0, 0.0), 0.0)
        sel = jnp.where(jidx == qblk, 1.0, past_sel)
        bias = jnp.where(sel > 0.5, 0.0, NEG_INF)
        parts = []
        if r0 > 0:
            parts.append(jnp.zeros((r0, tq), F32))
        parts.append(bias)
        if LANES - r0 - nb > 0:
            parts.append(jnp.zeros((LANES - r0 - nb, tq), F32))
        bias_q = jnp.concatenate(parts, axis=0).T
        data = (lane_q < HEAD_DIM) if hx == 0 else (lane_q >= HEAD_DIM)
        return jnp.where(data, q_f32, bias_q).astype(BF16)

    q_aug = [augmented_q(0), augmented_q(1)]

    own_off = pl.multiple_of(qi * KV_TILE, KV_TILE)
    causal = (lax.broadcasted_iota(I32, (tq, KV_TILE), 1) <= lax.broadcasted_iota(I32, (tq, KV_TILE), 0))
    state = []
    for hx in range(2):
        s = _dot_nt(q_aug[hx], ka_ref[hx, pl.ds(own_off, KV_TILE), :])
        s = jnp.where(causal, s, NEG_INF)
        m = jnp.max(s, axis=1, keepdims=True)
        p = jnp.exp(s - m)
        state += [m, _dot(p.astype(BF16), va_ref[hx, pl.ds(own_off, KV_TILE), :])]

    def body(jj, carry):
        off = pl.multiple_of(jj * KV_TILE, KV_TILE)
        out = []
        for hx in range(2):
            m_c, acc_c = carry[2 * hx], carry[2 * hx + 1]
            s_j = _dot_nt(q_aug[hx], ka_ref[hx, pl.ds(off, KV_TILE), :])
            m_n = jnp.maximum(m_c, jnp.max(s_j, axis=1, keepdims=True))
            alpha = jnp.exp(m_c - m_n)
            p_j = jnp.exp(s_j - m_n)
            out += [m_n, acc_c * alpha + _dot(p_j.astype(BF16), va_ref[hx, pl.ds(off, KV_TILE), :])]
        return tuple(out)

    _, acc_a, _, acc_b = lax.fori_loop(0, qi, body, tuple(state))
    l_a = jnp.sum(jnp.where(lane_q == HEAD_DIM, acc_a, 0.0), axis=1, keepdims=True)
    l_b = jnp.sum(jnp.where(lane_q == 0, acc_b, 0.0), axis=1, keepdims=True)
    o_ref[0] = jnp.where(lane_q < HEAD_DIM, acc_a / l_a, acc_b / l_b).astype(o_ref.dtype)


def _attention(q, k, v):
    bsz, seq, aw = q.shape
    npair = aw // LANES
    tq = KV_TILE
    nb = seq // MOBA_BLOCK
    assert seq % KV_TILE == 0 and nb % 8 == 0 and nb <= HEAD_DIM
    return pl.pallas_call(
        _attn_body,
        grid=(bsz, npair, seq // tq),
        in_specs=[pl.BlockSpec((1, tq, LANES), lambda b, p, i: (b, i, p)),
                  pl.BlockSpec((1, seq, LANES), lambda b, p, i: (b, 0, p)),
                  pl.BlockSpec((1, seq, LANES), lambda b, p, i: (b, 0, p))],
        out_specs=pl.BlockSpec((1, tq, LANES), lambda b, p, i: (b, i, p)),
        out_shape=jax.ShapeDtypeStruct((bsz, seq, aw), BF16),
        scratch_shapes=[pltpu.VMEM((2, seq, LANES), BF16),
                        pltpu.VMEM((2, seq, LANES), BF16),
                        pltpu.VMEM((2, LANES, LANES), F32)],
        compiler_params=_cparams("parallel", "parallel", "arbitrary"),
        name="attn",
    )(q, k, v)


def _merge_body(y_ref, a_ref, x_ref, wglu_ref, bglu_ref, gs_ref, ga_ref, wout_ref, gm_ref,
                gf_ref, shf_ref, scf_ref, rw_ref, rb_ref, x1_ref, h2_ref, idx_ref, wt_ref):
    nblk = y_ref.shape[0]
    y = jnp.concatenate([y_ref[j, 0] for j in range(nblk)], axis=1).astype(F32)
    tm = y.shape[0]
    g = 0.5 * y * (1.0 + lax.erf(y * (2.0 ** -0.5)))
    glu = g * jax.nn.sigmoid(_dot(g.astype(BF16), wglu_ref[...]) + bglu_ref[...])
    ssm_n = _rms(glu, gs_ref[...])
    att_n = _rms(a_ref[0].astype(F32), ga_ref[...])
    merged = jnp.concatenate([ssm_n, att_n], axis=1).astype(BF16)
    mix = _dot(merged, wout_ref[...])
    x1 = x_ref[0] + gm_ref[0] * mix
    x1_ref[0] = x1
    h2 = _rms(x1, gf_ref[...]) * (1.0 + scf_ref[0]) + shf_ref[0]
    h2_ref[0] = h2
    logits = jnp.dot(h2, rw_ref[...], preferred_element_type=F32, precision=HIGHEST) + rb_ref[...]
    ne = logits.shape[1]
    lane = lax.broadcasted_iota(I32, (tm, ne), 1)
    vals, idxs = [], []
    for _ in range(TOP_K):
        mx = jnp.max(logits, axis=1, keepdims=True)
        ix = jnp.min(jnp.where(logits == mx, lane, ne), axis=1, keepdims=True)
        vals.append(mx)
        idxs.append(ix)
        logits = jnp.where(lane == ix, -jnp.inf, logits)
    exps = [jnp.exp(vv - vals[0]) for vv in vals]
    tot = exps[0]
    for e in exps[1:]:
        tot = tot + e
    lane_k = lax.broadcasted_iota(I32, (tm, TOP_K), 1)
    idx_out = jnp.zeros((tm, TOP_K), I32)
    wt_out = jnp.zeros((tm, TOP_K), F32)
    for kk in range(TOP_K):
        idx_out = jnp.where(lane_k == kk, idxs[kk], idx_out)
        wt_out = jnp.where(lane_k == kk, exps[kk] / tot, wt_out)
    idx_ref[0] = idx_out
    wt_ref[0] = wt_out


def _merge(y4, attn, x, w_glu_bf, b_glu, g_ssm, g_attn, w_out_bf, mod3, g_ffn, router_w, router_b, tm=512):
    bsz, seq, d = x.shape
    nblk = y4.shape[0]
    ssm_w = nblk * LANES
    aw = attn.shape[-1]
    ne = router_w.shape[1]
    tm = min(tm, seq)
    row = lambda n: pl.BlockSpec((1, n), lambda b, i: (0, 0))
    full = lambda a: pl.BlockSpec(a.shape, lambda b, i: (0,) * a.ndim)
    modv = lambda j: pl.BlockSpec((1, 1, d), lambda b, i: (b, 0, j))
    return pl.pallas_call(
        _merge_body,
        grid=(bsz, seq // tm),
        in_specs=[pl.BlockSpec((nblk, 1, tm, LANES), lambda b, i: (0, b, i, 0)),
                  pl.BlockSpec((1, tm, aw), lambda b, i: (b, i, 0)),
                  pl.BlockSpec((1, tm, d), lambda b, i: (b, i, 0)),
                  full(w_glu_bf), row(ssm_w), row(ssm_w), row(aw), full(w_out_bf),
                  modv(2), row(d), modv(3), modv(4), full(router_w), row(ne)],
        out_specs=[pl.BlockSpec((1, tm, d), lambda b, i: (b, i, 0)),
                   pl.BlockSpec((1, tm, d), lambda b, i: (b, i, 0)),
                   pl.BlockSpec((1, tm, TOP_K), lambda b, i: (b, i, 0)),
                   pl.BlockSpec((1, tm, TOP_K), lambda b, i: (b, i, 0))],
        out_shape=[jax.ShapeDtypeStruct((bsz, seq, d), F32),
                   jax.ShapeDtypeStruct((bsz, seq, d), F32),
                   jax.ShapeDtypeStruct((bsz, seq, TOP_K), I32),
                   jax.ShapeDtypeStruct((bsz, seq, TOP_K), F32)],
        compiler_params=_cparams("parallel", "parallel"),
        name="merge",
    )(y4, attn, x, w_glu_bf, b_glu.reshape(1, ssm_w), g_ssm.reshape(1, ssm_w), g_attn.reshape(1, aw),
      w_out_bf, mod3, g_ffn.reshape(1, d), mod3, mod3, router_w, router_b.reshape(1, ne))


def _rank_body(idx_ref, rank_ref, cnt_ref, carry_ref, *, ne):
    i = pl.program_id(0)
    tm = idx_ref.shape[0]

    @pl.when(i == 0)
    def _init():
        carry_ref[...] = jnp.zeros(carry_ref.shape, F32)

    idx = idx_ref[...]
    lane = lax.broadcasted_iota(I32, (tm, ne), 1)
    onehot = jnp.zeros((tm, ne), F32)
    for kk in range(TOP_K):
        onehot = onehot + jnp.where(idx[:, kk:kk + 1] == lane, 1.0, 0.0)
    r = lax.broadcasted_iota(I32, (tm, tm), 0)
    c = lax.broadcasted_iota(I32, (tm, tm), 1)
    tri = jnp.where(c < r, 1.0, 0.0).astype(BF16)
    before = _dot(tri, onehot.astype(BF16)) + carry_ref[...]
    lane_k = lax.broadcasted_iota(I32, (tm, TOP_K), 1)
    rank = jnp.zeros((tm, TOP_K), F32)
    for kk in range(TOP_K):
        rk = jnp.sum(jnp.where(idx[:, kk:kk + 1] == lane, before, 0.0), axis=1, keepdims=True)
        rank = jnp.where(lane_k == kk, rk, rank)
    rank_ref[...] = rank.astype(I32)
    carry_ref[...] = carry_ref[...] + jnp.sum(onehot, axis=0, keepdims=True)
    cnt_ref[...] = carry_ref[...].astype(I32)


def _rank(idx, ne, tm=512):
    n = idx.shape[0]
    tm = min(tm, n)
    return pl.pallas_call(
        functools.partial(_rank_body, ne=ne),
        grid=(n // tm,),
        in_specs=[pl.BlockSpec((tm, TOP_K), lambda i: (i, 0))],
        out_specs=[pl.BlockSpec((tm, TOP_K), lambda i: (i, 0)),
                   pl.BlockSpec((1, ne), lambda i: (0, 0))],
        out_shape=[jax.ShapeDtypeStruct((n, TOP_K), I32), jax.ShapeDtypeStruct((1, ne), I32)],
        scratch_shapes=[pltpu.VMEM((1, ne), F32)],
        compiler_params=_cparams("arbitrary"),
        name="rank",
    )(idx)


def _dispatch_body(pos_ref, h_ref, xs_ref, sem):
    tm = h_ref.shape[0]

    def copy(r, kk):
        dst = pos_ref[r * TOP_K + kk]
        return pltpu.make_async_copy(h_ref.at[pl.ds(r, 1), :], xs_ref.at[pl.ds(dst, 1), :], sem)

    def issue(r, _):
        for kk in range(TOP_K):
            copy(r, kk).start()
        return 0

    def drain(r, _):
        for kk in range(TOP_K):
            copy(r, kk).wait()
        return 0

    lax.fori_loop(0, tm, issue, 0)
    lax.fori_loop(0, tm, drain, 0)


def _dispatch(h2, pos_flat, tm=512):
    n, d = h2.shape
    tm = min(tm, n)
    return pl.pallas_call(
        _dispatch_body,
        grid=(n // tm,),
        in_specs=[pl.BlockSpec((tm * TOP_K,), lambda i: (i,), memory_space=pltpu.SMEM),
                  pl.BlockSpec((tm, d), lambda i: (i, 0))],
        out_specs=pl.BlockSpec(memory_space=pl.ANY),
        out_shape=jax.ShapeDtypeStruct((n * TOP_K, d), h2.dtype),
        scratch_shapes=[pltpu.SemaphoreType.DMA(())],
        compiler_params=pltpu.CompilerParams(dimension_semantics=("arbitrary",), vmem_limit_bytes=VMEM_LIMIT,
                                             has_side_effects=True),
        name="dispatch",
    )(pos_flat, h2)


def _experts_body(tile_ref, exp_ref, lo_ref, hi_ref, first_ref, newe_ref,
                  xs_ref, wg_ref, bg_ref, wu_ref, bu_ref, wd_ref, bd_ref, ys_ref,
                  wg_bf, wu_bf, wd_bf):
    i = pl.program_id(0)
    lo = lo_ref[i]
    hi = hi_ref[i]

    @pl.when(newe_ref[i] == 1)
    def _cast():
        wg_bf[...] = wg_ref[0].astype(BF16)
        wu_bf[...] = wu_ref[0].astype(BF16)
        wd_bf[...] = wd_ref[0].astype(BF16)

    @pl.when(hi > lo)
    def _compute():
        x = xs_ref[...].astype(BF16)
        gate = jnp.minimum(_dot(x, wg_bf[...]) + bg_ref[0], SWIGLU_LIMIT)
        lin = jnp.clip(_dot(x, wu_bf[...]) + bu_ref[0], -SWIGLU_LIMIT, SWIGLU_LIMIT)
        act = gate * jax.nn.sigmoid(SWIGLU_ALPHA * gate) * (lin + 1.0)
        y = _dot(act.astype(BF16), wd_bf[...]) + bd_ref[0]
        row = lax.broadcasted_iota(I32, y.shape, 0)
        mine = (row >= lo) & (row < hi)

        @pl.when(first_ref[i] == 1)
        def _():
            ys_ref[...] = jnp.where(mine, y, 0.0)

        @pl.when(first_ref[i] == 0)
        def _():
            ys_ref[...] = jnp.where(mine, y, ys_ref[...])


def _experts(xs, items, w_gate, b_gate, w_up, b_up, w_down, b_down, tg):
    nk, d = xs.shape
    ne, _, dff = w_gate.shape
    n_items = items[0].shape[0]
    wspec = lambda shp: pl.BlockSpec((1,) + shp, lambda i, t, e, lo, hi, f, nw: (e[i], 0, 0))
    return pl.pallas_call(
        _experts_body,
        grid_spec=pltpu.PrefetchScalarGridSpec(
            num_scalar_prefetch=6,
            grid=(n_items,),
            in_specs=[pl.BlockSpec((tg, d), lambda i, t, e, lo, hi, f, nw: (t[i], 0)),
                      wspec((d, dff)), wspec((1, dff)),
                      wspec((d, dff)), wspec((1, dff)),
                      wspec((dff, d)), wspec((1, d))],
            out_specs=pl.BlockSpec((tg, d), lambda i, t, e, lo, hi, f, nw: (t[i], 0)),
            scratch_shapes=[pltpu.VMEM((d, dff), BF16), pltpu.VMEM((d, dff), BF16), pltpu.VMEM((dff, d), BF16)],
        ),
        out_shape=jax.ShapeDtypeStruct((nk, d), F32),
        compiler_params=_cparams("arbitrary"),
        name="experts",
    )(*items, xs, w_gate, b_gate.reshape(ne, 1, dff), w_up, b_up.reshape(ne, 1, dff),
      w_down, b_down.reshape(ne, 1, d))


def _plan_items(counts, nk, tg):
    ne = counts.shape[0]
    n_tiles = nk // tg
    n_items = n_tiles + ne
    ends = jnp.cumsum(counts)
    offs = ends - counts
    first_tile = offs // tg
    last_tile = jnp.maximum(ends - 1, 0) // tg
    per_e = jnp.where(counts > 0, last_tile - first_tile + 1, 0)
    item_end = jnp.cumsum(per_e)
    item_start = item_end - per_e
    total = item_end[-1]
    ii = jnp.arange(n_items, dtype=I32)
    e_of = jnp.minimum(jnp.searchsorted(item_end, ii, side='right'), ne - 1).astype(I32)
    valid = ii < total
    last_e = jnp.minimum(jnp.searchsorted(item_end, total - 1, side='right'), ne - 1).astype(I32)
    e_of = jnp.where(valid, e_of, last_e)
    tile = jnp.where(valid, first_tile[e_of] + (ii - item_start[e_of]), n_tiles - 1).astype(I32)
    lo = jnp.clip(offs[e_of] - tile * tg, 0, tg)
    hi = jnp.clip(ends[e_of] - tile * tg, 0, tg)
    lo = jnp.where(valid, lo, 0).astype(I32)
    hi = jnp.where(valid, hi, 0).astype(I32)
    prev_tile = jnp.concatenate([jnp.full((1,), -1, I32), tile[:-1]])
    prev_e = jnp.concatenate([jnp.full((1,), -1, I32), e_of[:-1]])
    first = (tile != prev_tile).astype(I32)
    newe = (e_of != prev_e).astype(I32)
    return (tile, e_of, lo, hi, first, newe), offs


def _combine_body(pos_ref, x1_ref, wt_ref, gf_ref, gfin_ref, sho_ref, sco_ref, ys_ref, o_ref, buf, sem):
    tm = x1_ref.shape[1]

    def copy(r, kk):
        src = pos_ref[r * TOP_K + kk]
        return pltpu.make_async_copy(ys_ref.at[pl.ds(src, 1), :], buf.at[kk, pl.ds(r, 1), :], sem)

    def issue(r, _):
        for kk in range(TOP_K):
            copy(r, kk).start()
        return 0

    def drain(r, _):
        for kk in range(TOP_K):
            copy(r, kk).wait()
        return 0

    lax.fori_loop(0, tm, issue, 0)
    lax.fori_loop(0, tm, drain, 0)
    wt = wt_ref[0]
    ffn = wt[:, 0:1] * buf[0]
    for kk in range(1, TOP_K):
        ffn = ffn + wt[:, kk:kk + 1] * buf[kk]
    x2 = x1_ref[0] + gf_ref[0] * ffn
    o_ref[0] = _rms(x2, gfin_ref[...]) * (1.0 + sco_ref[0]) + sho_ref[0]


def _combine(pos_flat, x1, wt, mod3, g_final, fmod3, ys, tm=256):
    bsz, seq, d = x1.shape
    tm = min(tm, seq)
    spb = seq // tm
    return pl.pallas_call(
        _combine_body,
        grid=(bsz, spb),
        in_specs=[pl.BlockSpec((tm * TOP_K,), lambda b, i: (b * spb + i,), memory_space=pltpu.SMEM),
                  pl.BlockSpec((1, tm, d), lambda b, i: (b, i, 0)),
                  pl.BlockSpec((1, tm, TOP_K), lambda b, i: (b, i, 0)),
                  pl.BlockSpec((1, 1, d), lambda b, i: (b, 0, 5)),
                  pl.BlockSpec((1, d), lambda b, i: (0, 0)),
                  pl.BlockSpec((1, 1, d), lambda b, i: (b, 0, 0)),
                  pl.BlockSpec((1, 1, d), lambda b, i: (b, 0, 1)),
                  pl.BlockSpec(memory_space=pl.ANY)],
        out_specs=pl.BlockSpec((1, tm, d), lambda b, i: (b, i, 0)),
        out_shape=jax.ShapeDtypeStruct((bsz, seq, d), F32),
        scratch_shapes=[pltpu.VMEM((TOP_K, tm, d), F32), pltpu.SemaphoreType.DMA(())],
        compiler_params=_cparams("arbitrary", "arbitrary"),
        name="combine",
    )(pos_flat, x1, wt, mod3, g_final.reshape(1, d), fmod3, fmod3, ys)


def kernel(x, c, positions, ada_w, ada_b, final_ada_w, final_ada_b, norm_mix_g, norm_ffn_g, norm_final_g, w_in, ssm_lambda_re, ssm_lambda_im, ssm_log_dt, ssm_b_re, ssm_b_im, ssm_c_re, ssm_c_im, ssm_d, ssm_w_glu, ssm_b_glu, out_norm_ssm_g, out_norm_attn_g, w_out, router_w, router_b, exp_w_gate, exp_b_gate, exp_w_up, exp_b_up, exp_w_down, exp_b_down):
    bsz, seq, d = x.shape
    depth = ada_w.shape[0]
    ssm_w = ssm_d.shape[-1]
    attn_w = (w_in.shape[-1] - ssm_w) // 3
    ne = router_w.shape[-1]
    n_tok = bsz * seq
    nc = seq // SSM_CHUNK
    tg = 256

    cos_t, sin_t = _rope_tables(positions)
    fmod3 = _adaln(c, final_ada_w, final_ada_b).reshape(bsz, 1, 2 * d)
    for l in range(depth):
        mod3 = _adaln(c, ada_w[l], ada_b[l]).reshape(bsz, 1, -1)
        u4, q, k, v = _inproj(x, norm_mix_g[l], mod3, w_in[l].astype(BF16), cos_t, sin_t, ssm_w, attn_w)
        tables = _s5_params(ssm_lambda_re[l], ssm_lambda_im[l], ssm_log_dt[l], ssm_b_re[l], ssm_b_im[l],
                            ssm_c_re[l], ssm_c_im[l], ssm_d[l], nc)
        nblk = ssm_w // LANES
        y4 = _s5(u4.reshape(nblk, bsz, nc, SSM_CHUNK * LANES), *tables).reshape(nblk, bsz, seq, LANES)
        attn = _attention(q, k, v)
        x1, h2, idx, wt = _merge(y4, attn, x, ssm_w_glu[l].astype(BF16), ssm_b_glu[l], out_norm_ssm_g[l],
                                 out_norm_attn_g[l], w_out[l].astype(BF16), mod3, norm_ffn_g[l],
                                 router_w[l], router_b[l])
        idx2 = idx.reshape(n_tok, TOP_K)
        rank, counts = _rank(idx2, ne)
        items, offs = _plan_items(counts[0], n_tok * TOP_K, tg)
        pos_flat = (offs[idx2] + rank).reshape(-1).astype(I32)
        xs = _dispatch(h2.reshape(n_tok, d), pos_flat)
        ys = _experts(xs, items, exp_w_gate[l], exp_b_gate[l], exp_w_up[l], exp_b_up[l],
                      exp_w_down[l], exp_b_down[l], tg)
        if l + 1 < depth:
            raise NotImplementedError("depth > 1 needs the non-final combine")
        x = _combine(pos_flat, x1, wt, mod3, norm_final_g, fmod3, ys)
    return x
```

```python
import functools
import math

import jax
import jax.numpy as jnp
from jax import lax
from jax.experimental import pallas as pl
from jax.experimental.pallas import tpu as pltpu

F32 = jnp.float32
BF16 = jnp.bfloat16
I32 = jnp.int32
HIGHEST = lax.Precision.HIGHEST

LANES = 128
HEAD_DIM = 64
MOBA_BLOCK = 256
MOBA_TOPK = 3
ROT_DIM = HEAD_DIM // 4
ROPE_THETA = 500000.0
SSM_GROUP = 16
SSM_CHUNK = 16
TOP_K = 4
SWIGLU_ALPHA = 1.702
SWIGLU_LIMIT = 7.0
NORM_EPS = 1e-5
NEG_INF = -1e30
KV_TILE = 2 * MOBA_BLOCK
ROW_UNROLL = 8
WT_ROWS = 8
VMEM_LIMIT = 56 * 1024 * 1024


def _cparams(*sem):
    return pltpu.CompilerParams(dimension_semantics=sem, vmem_limit_bytes=VMEM_LIMIT)


def _dot(a, b):
    return jnp.dot(a, b, preferred_element_type=F32)


def _dot_nt(a, b):
    return lax.dot_general(a, b, (((1,), (1,)), ((), ())), preferred_element_type=F32)


def _rms(x, g):
    return x * lax.rsqrt(jnp.mean(x * x, axis=-1, keepdims=True) + NORM_EPS) * g


def _adaln_body(c_ref, w_ref, b_ref, o_ref):
    c = c_ref[...]
    ca = c * jax.nn.sigmoid(c)
    o_ref[...] = jnp.dot(ca, w_ref[...], preferred_element_type=F32, precision=HIGHEST) + b_ref[...]


def _adaln(c, w, b, tn=512):
    bsz, d = c.shape
    n = w.shape[1]
    return pl.pallas_call(
        _adaln_body,
        grid=(n // tn,),
        in_specs=[pl.BlockSpec((bsz, d), lambda j: (0, 0)),
                  pl.BlockSpec((d, tn), lambda j: (0, j)),
                  pl.BlockSpec((1, tn), lambda j: (0, j))],
        out_specs=pl.BlockSpec((bsz, tn), lambda j: (0, j)),
        out_shape=jax.ShapeDtypeStruct((bsz, n), F32),
        compiler_params=_cparams("parallel"),
        name="adaln",
    )(c, w, b.reshape(1, n))


def _inproj_body(x_ref, g_ref, sh_ref, sc_ref, w_ref, tab_ref, exp_ref, u_ref, q_ref, k_ref, v_ref):
    x = x_ref[0]
    tm = x.shape[0]
    h = _rms(x, g_ref[...]) * (1.0 + sc_ref[0]) + sh_ref[0]
    proj = _dot(h.astype(BF16), w_ref[...])
    ssm_w = u_ref.shape[0] * LANES
    aw = q_ref.shape[2]
    for j in range(u_ref.shape[0]):
        u_ref[j, 0] = proj[:, LANES * j:LANES * (j + 1)].astype(BF16)
    tab = lax.dot_general(tab_ref[0], exp_ref[...], (((0,), (0,)), ((), ())),
                          preferred_element_type=F32, precision=HIGHEST)
    lane1 = lax.broadcasted_iota(I32, (tm, LANES), 1)
    cos1 = tab[:, :LANES] + jnp.where((lane1 & (HEAD_DIM - 1)) >= ROT_DIM, 1.0, 0.0)
    reps = aw // LANES
    cosf = jnp.concatenate([cos1] * reps, axis=1)
    sinf = jnp.concatenate([tab[:, LANES:]] * reps, axis=1)
    lane = lax.broadcasted_iota(I32, (tm, aw), 1)
    first_half = (lane & (HEAD_DIM - 1)) < (ROT_DIM // 2)

    def rope(t):
        partner = jnp.where(first_half, pltpu.roll(t, aw - ROT_DIM // 2, 1), pltpu.roll(t, ROT_DIM // 2, 1))
        return t * cosf + partner * sinf

    q = rope(proj[:, ssm_w:ssm_w + aw]) * (HEAD_DIM ** -0.5)
    k = rope(proj[:, ssm_w + aw:ssm_w + 2 * aw])
    q_ref[0] = q.astype(BF16)
    k_ref[0] = k.astype(BF16)
    v_ref[0] = proj[:, ssm_w + 2 * aw:ssm_w + 3 * aw].astype(BF16)


def _inproj(x, gain, mod3, w_in_bf, rope_tab, rope_exp, ssm_w, attn_w, tm=512):
    bsz, seq, d = x.shape
    n_u = ssm_w // LANES
    tm = min(tm, seq)
    return pl.pallas_call(
        _inproj_body,
        grid=(bsz, seq // tm),
        in_specs=[pl.BlockSpec((1, tm, d), lambda b, i: (b, i, 0)),
                  pl.BlockSpec((1, d), lambda b, i: (0, 0)),
                  pl.BlockSpec((1, 1, d), lambda b, i: (b, 0, 0)),
                  pl.BlockSpec((1, 1, d), lambda b, i: (b, 0, 1)),
                  pl.BlockSpec(w_in_bf.shape, lambda b, i: (0, 0)),
                  pl.BlockSpec((1, ROT_DIM, tm), lambda b, i: (b, 0, i)),
                  pl.BlockSpec(rope_exp.shape, lambda b, i: (0, 0))],
        out_specs=[pl.BlockSpec((n_u, 1, tm, LANES), lambda b, i: (0, b, i, 0)),
                   pl.BlockSpec((1, tm, attn_w), lambda b, i: (b, i, 0)),
                   pl.BlockSpec((1, tm, attn_w), lambda b, i: (b, i, 0)),
                   pl.BlockSpec((1, tm, attn_w), lambda b, i: (b, i, 0))],
        out_shape=[jax.ShapeDtypeStruct((n_u, bsz, seq, LANES), BF16),
                   jax.ShapeDtypeStruct((bsz, seq, attn_w), BF16),
                   jax.ShapeDtypeStruct((bsz, seq, attn_w), BF16),
                   jax.ShapeDtypeStruct((bsz, seq, attn_w), BF16)],
        compiler_params=_cparams("parallel", "parallel"),
        name="inproj",
    )(x, gain.reshape(1, d), mod3, mod3, w_in_bf, rope_tab, rope_exp)


def _rope_tables(positions):
    half = ROT_DIM // 2
    inv_freq = ROPE_THETA ** (-jnp.arange(0, ROT_DIM, 2, dtype=F32) / ROT_DIM)
    ang = positions.astype(F32)[:, None, :] * inv_freq[None, :, None]
    tab = jnp.concatenate([jnp.cos(ang), jnp.sin(ang)], axis=1)
    i = jnp.arange(ROT_DIM)[:, None]
    hl = jnp.arange(LANES)[None, :] % HEAD_DIM
    e_cos = jnp.where((i < half) & ((hl == i) | (hl == i + half)), 1.0, 0.0)
    e_sin = jnp.where(i >= half, jnp.where(hl == i - half, -1.0, jnp.where(hl == i, 1.0, 0.0)), 0.0)
    return tab, jnp.concatenate([e_cos, e_sin], axis=1).astype(F32)


def _s5_params(lam_re, lam_im, log_dt, b_re, b_im, c_re, c_im, d_skip, n_chunks):
    g_all, p = lam_re.shape
    hc = b_re.shape[-1]
    t = SSM_CHUNK
    gpb = LANES // hc
    nblk = g_all // gpb
    lr, li = lam_re.astype(F32), lam_im.astype(F32)
    dt = jnp.exp(log_dt.astype(F32))[:, None]
    ldr, ldi = lr * dt, li * dt

    def apow(k):
        k = jnp.asarray(k, F32)[..., None, None]
        mag = jnp.exp(ldr * k)
        return mag * jnp.cos(ldi * k), mag * jnp.sin(ldi * k)

    ar, ai = apow(1.0)
    zr, zi = ar - 1.0, ai
    den = lr * lr + li * li
    cr = (zr * lr + zi * li) / den
    ci = (zi * lr - zr * li) / den
    bre, bim = b_re.astype(F32), b_im.astype(F32)
    bbr = cr[..., None] * bre - ci[..., None] * bim
    bbi = cr[..., None] * bim + ci[..., None] * bre
    cre, cim = c_re.astype(F32), c_im.astype(F32)
    eye = jnp.eye(gpb, dtype=F32)

    pr, pi = apow(jnp.arange(t))
    abr = pr[..., None] * bbr - pi[..., None] * bbi
    abi = pr[..., None] * bbi + pi[..., None] * bbr
    kf = (jnp.einsum('gnp,kgph->kghn', cre, abr, precision=HIGHEST)
          - jnp.einsum('gnp,kgph->kghn', cim, abi, precision=HIGHEST))
    kblk = jnp.einsum('kjghn,gf->kjghfn', kf.reshape(t, nblk, gpb, hc, hc), eye)
    kblk = kblk.reshape(t, nblk, LANES, LANES)
    kpad = jnp.concatenate([jnp.zeros_like(kblk[:1]), kblk], axis=0)
    pm = jnp.concatenate([kpad[0:t], kpad[1:t + 1]], axis=-1)
    kstack = jnp.transpose(pm[::-1], (1, 0, 2, 3)).reshape(nblk, t * LANES, 2 * LANES)

    def inject(ab):
        ab = ab[::-1].reshape(t, nblk, gpb, p, hc)
        return jnp.einsum('sjgph,gf->jsghfp', ab, eye).reshape(nblk, t * LANES, gpb * p)

    bm = jnp.concatenate([inject(abr), inject(abi)], axis=-1)

    qr, qi = apow(jnp.arange(t) + 1.0)
    c_from_re = cre[None] * qr[:, :, None, :] - cim[None] * qi[:, :, None, :]
    c_from_im = -cre[None] * qi[:, :, None, :] - cim[None] * qr[:, :, None, :]

    def readout(cc):
        cc = cc.reshape(t, nblk, gpb, hc, p)
        return jnp.einsum('tjgnp,gf->jgptfn', cc, eye).reshape(nblk, gpb * p, t * LANES)

    cm = jnp.concatenate([readout(c_from_re), readout(c_from_im)], axis=1)

    n_steps = max(1, int(math.log2(n_chunks)))
    shifts = [float(t * (1 << s)) for s in range(n_steps)]
    sr, si = apow(jnp.asarray(shifts))
    ap = jnp.concatenate([sr.reshape(n_steps, nblk, gpb * p), si.reshape(n_steps, nblk, gpb * p)], axis=-1)
    ap = jnp.transpose(ap, (1, 0, 2))
    dvec = jnp.tile(d_skip.astype(F32).reshape(nblk, 1, LANES), (1, 1, t))
    return kstack.astype(BF16), bm.astype(BF16), cm.astype(BF16), ap, dvec


def _s5_body(x_ref, ks_ref, bm_ref, cm_ref, ap_ref, d_ref, y_ref, h_ref):
    nc = x_ref.shape[2]
    t = SSM_CHUNK
    pad = h_ref.shape[0] - nc
    half = h_ref.shape[1] // 2
    x = x_ref[0, 0]
    h_ref[0:pad] = jnp.zeros((pad, h_ref.shape[1]), F32)
    h_ref[pad:pad + nc] = _dot(x, bm_ref[0])
    for step in range(ap_ref.shape[1]):
        d = 1 << step
        cur = h_ref[pad:pad + nc]
        sft = h_ref[pad - d:pad + nc - d]
        ar = ap_ref[0, step:step + 1, 0:half]
        ai = ap_ref[0, step:step + 1, half:]
        cr, ci = cur[:, :half], cur[:, half:]
        sr, si = sft[:, :half], sft[:, half:]
        h_ref[pad:pad + nc, 0:half] = cr + ar * sr - ai * si
        h_ref[pad:pad + nc, half:] = ci + ar * si + ai * sr
    h_prev = h_ref[pad - 1:pad + nc - 1]
    ys = _dot(h_prev.astype(BF16), cm_ref[0])
    for t0 in range(0, t, 2):
        lo, hi = t0 * LANES, (t0 + 2) * LANES
        conv = _dot(x[:, :hi], ks_ref[0, (t - 2 - t0) * LANES:, :])
        y = conv + ys[:, lo:hi] + d_ref[0][:, lo:hi] * x[:, lo:hi].astype(F32)
        y_ref[0, 0, :, lo:hi] = y.astype(y_ref.dtype)


def _s5(u4, kstack, bm, cm, ap, dvec):
    nblk, bsz, nc, w = u4.shape
    sw = bm.shape[-1]
    pad = max(nc // 2, 8)
    return pl.pallas_call(
        _s5_body,
        grid=(nblk, bsz),
        in_specs=[pl.BlockSpec((1, 1, nc, w), lambda j, b: (j, b, 0, 0)),
                  pl.BlockSpec((1,) + kstack.shape[1:], lambda j, b: (j, 0, 0)),
                  pl.BlockSpec((1,) + bm.shape[1:], lambda j, b: (j, 0, 0)),
                  pl.BlockSpec((1,) + cm.shape[1:], lambda j, b: (j, 0, 0)),
                  pl.BlockSpec((1,) + ap.shape[1:], lambda j, b: (j, 0, 0)),
                  pl.BlockSpec((1, 1, w), lambda j, b: (j, 0, 0))],
        out_specs=pl.BlockSpec((1, 1, nc, w), lambda j, b: (j, b, 0, 0)),
        out_shape=jax.ShapeDtypeStruct((nblk, bsz, nc, w), BF16),
        scratch_shapes=[pltpu.VMEM((pad + nc, sw), F32)],
        compiler_params=_cparams("parallel", "parallel"),
        name="s5",
    )(u4, kstack, bm, cm, ap, dvec)


def _attn_body(q_ref, k_ref, v_ref, o_ref, ka_ref, va_ref, km_ref):
    qi = pl.program_id(2)
    seq = k_ref.shape[1]
    tq = q_ref.shape[1]
    nb = seq // MOBA_BLOCK
    blk_shift = MOBA_BLOCK.bit_length() - 1

    @pl.when(qi == 0)
    def _build():
        k = k_ref[0].astype(F32)
        v = v_ref[0].astype(F32)
        lane = lax.broadcasted_iota(I32, (seq, LANES), 1)
        blk = lax.shift_right_logical(lax.broadcasted_iota(I32, (seq, LANES), 0), blk_shift)
        ka_ref[0] = jnp.where(lane < HEAD_DIM, k, jnp.where(lane - HEAD_DIM == blk, 1.0, 0.0)).astype(BF16)
        ka_ref[1] = jnp.where(lane >= HEAD_DIM, k, jnp.where(lane == blk, 1.0, 0.0)).astype(BF16)
        va_ref[0] = jnp.where(lane < HEAD_DIM, v, jnp.where(lane == HEAD_DIM, 1.0, 0.0)).astype(BF16)
        va_ref[1] = jnp.where(lane >= HEAD_DIM, v, jnp.where(lane == 0, 1.0, 0.0)).astype(BF16)
        km = jnp.mean(k.reshape(nb, MOBA_BLOCK, LANES), axis=1)
        lane_b = lax.broadcasted_iota(I32, (nb, LANES), 1)
        km_ref[...] = jnp.zeros(km_ref.shape, F32)
        km_ref[0, HEAD_DIM:HEAD_DIM + nb, :] = jnp.where(lane_b < HEAD_DIM, km, 0.0)
        km_ref[1, 0:nb, :] = jnp.where(lane_b >= HEAD_DIM, km, 0.0)

    q_f32 = q_ref[0].astype(F32)
    lane_q = lax.broadcasted_iota(I32, (tq, LANES), 1)

    def augmented_q(hx):
        r0 = HEAD_DIM if hx == 0 else 0
        gate_t = _dot_nt(km_ref[hx], q_f32)
        g = gate_t[r0:r0 + nb, :]
        jidx = lax.broadcasted_iota(I32, (nb, tq), 0)
        qblk = qi * (tq // MOBA_BLOCK) + lax.shift_right_logical(
            lax.broadcasted_iota(I32, (nb, tq), 1), blk_shift)
        cnt = jnp.zeros((nb, tq), F32)
        for jp in range(nb):
            row = g[jp:jp + 1, :]
            beats = jnp.where(row > g, 1.0, jnp.where(row == g, jnp.where(jp < jidx, 1.0, 0.0), 0.0))
            cnt = cnt + jnp.where(jp < qblk, beats, 0.0)
        past_sel = jnp.where(jidx < qblk, jnp.where(cnt < MOBA_TOPK, 1.0, 0.0), 0.0)
        sel = jnp.where(jidx == qblk, 1.0, past_sel)
        bias = jnp.where(sel > 0.5, 0.0, NEG_INF)
        parts = []
        if r0 > 0:
            parts.append(jnp.zeros((r0, tq), F32))
        parts.append(bias)
        if LANES - r0 - nb > 0:
            parts.append(jnp.zeros((LANES - r0 - nb, tq), F32))
        bias_q = jnp.concatenate(parts, axis=0).T
        data = (lane_q < HEAD_DIM) if hx == 0 else (lane_q >= HEAD_DIM)
        return jnp.where(data, q_f32, bias_q).astype(BF16)

    q_aug = [augmented_q(0), augmented_q(1)]

    own_off = pl.multiple_of(qi * KV_TILE, KV_TILE)
    causal = (lax.broadcasted_iota(I32, (tq, KV_TILE), 1) <= lax.broadcasted_iota(I32, (tq, KV_TILE), 0))
    state = []
    for hx in range(2):
        s = _dot_nt(q_aug[hx], ka_ref[hx, pl.ds(own_off, KV_TILE), :])
        s = jnp.where(causal, s, NEG_INF)
        m = jnp.max(s, axis=1, keepdims=True)
        p = jnp.exp(s - m)
        state += [m, _dot(p.astype(BF16), va_ref[hx, pl.ds(own_off, KV_TILE), :])]

    def body(jj, carry):
        off = pl.multiple_of(jj * KV_TILE, KV_TILE)
        out = []
        for hx in range(2):
            m_c, acc_c = carry[2 * hx], carry[2 * hx + 1]
            s_j = _dot_nt(q_aug[hx], ka_ref[hx, pl.ds(off, KV_TILE), :])
            m_n = jnp.maximum(m_c, jnp.max(s_j, axis=1, keepdims=True))
            alpha = jnp.exp(m_c - m_n)
            p_j = jnp.exp(s_j - m_n)
            out += [m_n, acc_c * alpha + _dot(p_j.astype(BF16), va_ref[hx, pl.ds(off, KV_TILE), :])]
        return tuple(out)

    _, acc_a, _, acc_b = lax.fori_loop(0, qi, body, tuple(state))
    l_a = jnp.sum(jnp.where(lane_q == HEAD_DIM, acc_a, 0.0), axis=1, keepdims=True)
    l_b = jnp.sum(jnp.where(lane_q == 0, acc_b, 0.0), axis=1, keepdims=True)
    o_ref[0] = jnp.where(lane_q < HEAD_DIM, acc_a / l_a, acc_b / l_b).astype(o_ref.dtype)


def _attention(q, k, v):
    bsz, seq, aw = q.shape
    npair = aw // LANES
    tq = KV_TILE
    nb = seq // MOBA_BLOCK
    assert seq % KV_TILE == 0 and nb % 8 == 0 and nb <= HEAD_DIM
    return pl.pallas_call(
        _attn_body,
        grid=(bsz, npair, seq // tq),
        in_specs=[pl.BlockSpec((1, tq, LANES), lambda b, p, i: (b, i, p)),
                  pl.BlockSpec((1, seq, LANES), lambda b, p, i: (b, 0, p)),
                  pl.BlockSpec((1, seq, LANES), lambda b, p, i: (b, 0, p))],
        out_specs=pl.BlockSpec((1, tq, LANES), lambda b, p, i: (b, i, p)),
        out_shape=jax.ShapeDtypeStruct((bsz, seq, aw), BF16),
        scratch_shapes=[pltpu.VMEM((2, seq, LANES), BF16),
                        pltpu.VMEM((2, seq, LANES), BF16),
                        pltpu.VMEM((2, LANES, LANES), F32)],
        compiler_params=_cparams("parallel", "parallel", "arbitrary"),
        name="attn",
    )(q, k, v)


def _merge_body(y_ref, a_ref, x_ref, wglu_ref, bglu_ref, gs_ref, ga_ref, wout_ref, gm_ref,
                gf_ref, shf_ref, scf_ref, rw_ref, rb_ref, x1_ref, h2_ref, idx_ref, wt_ref):
    nblk = y_ref.shape[0]
    y = jnp.concatenate([y_ref[j, 0] for j in range(nblk)], axis=1).astype(F32)
    tm = y.shape[0]
    g = 0.5 * y * (1.0 + lax.erf(y * (2.0 ** -0.5)))
    glu = g * jax.nn.sigmoid(_dot(g.astype(BF16), wglu_ref[...]) + bglu_ref[...])
    ssm_n = _rms(glu, gs_ref[...])
    att_n = _rms(a_ref[0].astype(F32), ga_ref[...])
    merged = jnp.concatenate([ssm_n, att_n], axis=1).astype(BF16)
    mix = _dot(merged, wout_ref[...])
    x1 = x_ref[0] + gm_ref[0] * mix
    x1_ref[0] = x1
    h2 = _rms(x1, gf_ref[...]) * (1.0 + scf_ref[0]) + shf_ref[0]
    h2_ref[0] = h2
    logits = lax.dot_general(rw_ref[...], h2, (((1,), (1,)), ((), ())),
                             preferred_element_type=F32, precision=HIGHEST) + rb_ref[...]
    ne = logits.shape[0]
    sub = lax.broadcasted_iota(I32, (ne, tm), 0)
    vals, idxs = [], []
    for _ in range(TOP_K):
        mx = jnp.max(logits, axis=0, keepdims=True)
        ix = jnp.min(jnp.where(logits == mx, sub, ne), axis=0, keepdims=True)
        vals.append(mx)
        idxs.append(ix)
        logits = jnp.where(sub == ix, -jnp.inf, logits)
    exps = [jnp.exp(vv - vals[0]) for vv in vals]
    tot = exps[0]
    for e in exps[1:]:
        tot = tot + e
    row_i = lax.broadcasted_iota(I32, idx_ref.shape, 0)
    row_w = lax.broadcasted_iota(I32, wt_ref.shape, 0)
    idx_out = jnp.zeros(idx_ref.shape, I32)
    wt_out = jnp.zeros(wt_ref.shape, F32)
    for kk in range(TOP_K):
        idx_out = jnp.where(row_i == kk, idxs[kk], idx_out)
        wt_out = jnp.where(row_w == kk, exps[kk] / tot, wt_out)
    idx_ref[...] = idx_out
    wt_ref[...] = wt_out


def _merge(y4, attn, x, w_glu_bf, b_glu, g_ssm, g_attn, w_out_bf, mod3, g_ffn, router_w, router_b, tm=512):
    bsz, seq, d = x.shape
    nblk = y4.shape[0]
    ssm_w = nblk * LANES
    aw = attn.shape[-1]
    ne = router_w.shape[1]
    tm = min(tm, seq)
    spb = seq // tm
    n_tok = bsz * seq
    rw_t = router_w.T
    row = lambda n: pl.BlockSpec((1, n), lambda b, i: (0, 0))
    full = lambda a: pl.BlockSpec(a.shape, lambda b, i: (0,) * a.ndim)
    modv = lambda j: pl.BlockSpec((1, 1, d), lambda b, i: (b, 0, j))
    return pl.pallas_call(
        _merge_body,
        grid=(bsz, seq // tm),
        in_specs=[pl.BlockSpec((nblk, 1, tm, LANES), lambda b, i: (0, b, i, 0)),
                  pl.BlockSpec((1, tm, aw), lambda b, i: (b, i, 0)),
                  pl.BlockSpec((1, tm, d), lambda b, i: (b, i, 0)),
                  full(w_glu_bf), row(ssm_w), row(ssm_w), row(aw), full(w_out_bf),
                  modv(2), row(d), modv(3), modv(4), full(rw_t),
                  pl.BlockSpec((ne, 1), lambda b, i: (0, 0))],
        out_specs=[pl.BlockSpec((1, tm, d), lambda b, i: (b, i, 0)),
                   pl.BlockSpec((1, tm, d), lambda b, i: (b, i, 0)),
                   pl.BlockSpec((TOP_K, tm), lambda b, i: (0, b * spb + i)),
                   pl.BlockSpec((WT_ROWS, tm), lambda b, i: (0, b * spb + i))],
        out_shape=[jax.ShapeDtypeStruct((bsz, seq, d), F32),
                   jax.ShapeDtypeStruct((bsz, seq, d), F32),
                   jax.ShapeDtypeStruct((TOP_K, n_tok), I32),
                   jax.ShapeDtypeStruct((WT_ROWS, n_tok), F32)],
        compiler_params=_cparams("parallel", "parallel"),
        name="merge",
    )(y4, attn, x, w_glu_bf, b_glu.reshape(1, ssm_w), g_ssm.reshape(1, ssm_w), g_attn.reshape(1, aw),
      w_out_bf, mod3, g_ffn.reshape(1, d), mod3, mod3, rw_t, router_b.reshape(ne, 1))


def _rank_body(idx_ref, rank_ref, cnt_ref, carry_ref, *, ne):
    i = pl.program_id(0)
    tm = idx_ref.shape[1]

    @pl.when(i == 0)
    def _init():
        carry_ref[...] = jnp.zeros(carry_ref.shape, F32)

    idx = idx_ref[...]
    sub = lax.broadcasted_iota(I32, (ne, tm), 0)
    onehot = jnp.zeros((ne, tm), F32)
    for kk in range(TOP_K):
        onehot = onehot + jnp.where(idx[kk:kk + 1, :] == sub, 1.0, 0.0)
    r = lax.broadcasted_iota(I32, (tm, tm), 0)
    c = lax.broadcasted_iota(I32, (tm, tm), 1)
    tri = jnp.where(r < c, 1.0, 0.0).astype(BF16)
    before = _dot(onehot.astype(BF16), tri) + carry_ref[...]
    row_k = lax.broadcasted_iota(I32, (TOP_K, tm), 0)
    rank = jnp.zeros((TOP_K, tm), F32)
    for kk in range(TOP_K):
        rk = jnp.sum(jnp.where(idx[kk:kk + 1, :] == sub, before, 0.0), axis=0, keepdims=True)
        rank = jnp.where(row_k == kk, rk, rank)
    rank_ref[...] = rank.astype(I32)
    carry_ref[...] = carry_ref[...] + jnp.sum(onehot, axis=1, keepdims=True)
    cnt_ref[...] = carry_ref[...].astype(I32)


def _rank(idx_t, ne, tm=512):
    n = idx_t.shape[1]
    tm = min(tm, n)
    return pl.pallas_call(
        functools.partial(_rank_body, ne=ne),
        grid=(n // tm,),
        in_specs=[pl.BlockSpec((TOP_K, tm), lambda i: (0, i))],
        out_specs=[pl.BlockSpec((TOP_K, tm), lambda i: (0, i)),
                   pl.BlockSpec((ne, 1), lambda i: (0, 0))],
        out_shape=[jax.ShapeDtypeStruct((TOP_K, n), I32), jax.ShapeDtypeStruct((ne, 1), I32)],
        scratch_shapes=[pltpu.VMEM((ne, 1), F32)],
        compiler_params=_cparams("arbitrary"),
        name="rank",
    )(idx_t)


def _dispatch_body(pos_ref, h_ref, xs_ref, sem):
    tm = h_ref.shape[0]

    def issue(g, _):
        for rr in range(ROW_UNROLL):
            r = g * ROW_UNROLL + rr
            for kk in range(TOP_K):
                dst = pos_ref[kk, r]
                pltpu.make_async_copy(h_ref.at[pl.ds(r, 1), :], xs_ref.at[pl.ds(dst, 1), :], sem).start()
        return 0

    lax.fori_loop(0, tm // ROW_UNROLL, issue, 0)
    done = xs_ref.at[pl.ds(0, tm * TOP_K), :]
    pltpu.make_async_copy(done, done, sem).wait()


def _dispatch(h2, pos_t, tm=512):
    n, d = h2.shape
    tm = min(tm, n)
    return pl.pallas_call(
        _dispatch_body,
        grid=(n // tm,),
        in_specs=[pl.BlockSpec((TOP_K, tm), lambda i: (0, i), memory_space=pltpu.SMEM),
                  pl.BlockSpec((tm, d), lambda i: (i, 0))],
        out_specs=pl.BlockSpec(memory_space=pl.ANY),
        out_shape=jax.ShapeDtypeStruct((n * TOP_K, d), h2.dtype),
        scratch_shapes=[pltpu.SemaphoreType.DMA(())],
        compiler_params=_cparams("arbitrary"),
        name="dispatch",
    )(pos_t, h2)


def _experts_body(tile_ref, exp_ref, lo_ref, hi_ref, first_ref, newe_ref,
                  xs_ref, wg_ref, bg_ref, wu_ref, bu_ref, wd_ref, bd_ref, ys_ref,
                  wg_bf, wu_bf, wd_bf):
    i = pl.program_id(0)
    lo = lo_ref[i]
    hi = hi_ref[i]

    @pl.when(newe_ref[i] == 1)
    def _cast():
        wg_bf[...] = wg_ref[0].astype(BF16)
        wu_bf[...] = wu_ref[0].astype(BF16)
        wd_bf[...] = wd_ref[0].astype(BF16)

    @pl.when(hi > lo)
    def _compute():
        x = xs_ref[...].astype(BF16)
        gate = jnp.minimum(_dot(x, wg_bf[...]) + bg_ref[0], SWIGLU_LIMIT)
        lin = jnp.clip(_dot(x, wu_bf[...]) + bu_ref[0], -SWIGLU_LIMIT, SWIGLU_LIMIT)
        act = gate * jax.nn.sigmoid(SWIGLU_ALPHA * gate) * (lin + 1.0)
        y = _dot(act.astype(BF16), wd_bf[...]) + bd_ref[0]
        row = lax.broadcasted_iota(I32, y.shape, 0)
        mine = (row >= lo) & (row < hi)

        @pl.when(first_ref[i] == 1)
        def _():
            ys_ref[...] = jnp.where(mine, y, 0.0)

        @pl.when(first_ref[i] == 0)
        def _():
            ys_ref[...] = jnp.where(mine, y, ys_ref[...])


def _experts(xs, items, w_gate, b_gate, w_up, b_up, w_down, b_down, tg):
    nk, d = xs.shape
    ne, _, dff = w_gate.shape
    n_items = items[0].shape[0]
    wspec = lambda shp: pl.BlockSpec((1,) + shp, lambda i, t, e, lo, hi, f, nw: (e[i], 0, 0))
    return pl.pallas_call(
        _experts_body,
        grid_spec=pltpu.PrefetchScalarGridSpec(
            num_scalar_prefetch=6,
            grid=(n_items,),
            in_specs=[pl.BlockSpec((tg, d), lambda i, t, e, lo, hi, f, nw: (t[i], 0)),
                      wspec((d, dff)), wspec((1, dff)),
                      wspec((d, dff)), wspec((1, dff)),
                      wspec((dff, d)), wspec((1, d))],
            out_specs=pl.BlockSpec((tg, d), lambda i, t, e, lo, hi, f, nw: (t[i], 0)),
            scratch_shapes=[pltpu.VMEM((d, dff), BF16), pltpu.VMEM((d, dff), BF16), pltpu.VMEM((dff, d), BF16)],
        ),
        out_shape=jax.ShapeDtypeStruct((nk, d), F32),
        compiler_params=_cparams("arbitrary"),
        name="experts",
    )(*items, xs, w_gate, b_gate.reshape(ne, 1, dff), w_up, b_up.reshape(ne, 1, dff),
      w_down, b_down.reshape(ne, 1, d))


def _plan_items(counts, nk, tg):
    ne = counts.shape[0]
    n_tiles = nk // tg
    n_items = n_tiles + ne
    ends = jnp.cumsum(counts)
    offs = ends - counts
    first_tile = offs // tg
    last_tile = jnp.maximum(ends - 1, 0) // tg
    per_e = jnp.where(counts > 0, last_tile - first_tile + 1, 0)
    item_end = jnp.cumsum(per_e)
    item_start = item_end - per_e
    total = item_end[-1]
    ii = jnp.arange(n_items, dtype=I32)
    first_after = lambda i: jnp.minimum(jnp.sum((item_end[None, :] <= i[:, None]).astype(I32), axis=1), ne - 1)
    e_of = first_after(ii)
    valid = ii < total
    last_e = first_after(total[None] - 1)[0]
    e_of = jnp.where(valid, e_of, last_e)
    tile = jnp.where(valid, first_tile[e_of] + (ii - item_start[e_of]), n_tiles - 1).astype(I32)
    lo = jnp.clip(offs[e_of] - tile * tg, 0, tg)
    hi = jnp.clip(ends[e_of] - tile * tg, 0, tg)
    lo = jnp.where(valid, lo, 0).astype(I32)
    hi = jnp.where(valid, hi, 0).astype(I32)
    prev_tile = jnp.concatenate([jnp.full((1,), -1, I32), tile[:-1]])
    prev_e = jnp.concatenate([jnp.full((1,), -1, I32), e_of[:-1]])
    first = (tile != prev_tile).astype(I32)
    newe = (e_of != prev_e).astype(I32)
    return (tile, e_of, lo, hi, first, newe), offs


def _combine_body(pos_ref, x1_ref, wt_ref, gf_ref, gfin_ref, sho_ref, sco_ref, ys_ref, o_ref, buf, sem):
    tm = x1_ref.shape[1]
    n_half = buf.shape[0]
    th = tm // n_half

    def issue_half(hf):
        def issue(g, _):
            for rr in range(ROW_UNROLL):
                r = g * ROW_UNROLL + rr
                for kk in range(TOP_K):
                    src = pos_ref[kk, hf * th + r]
                    pltpu.make_async_copy(ys_ref.at[pl.ds(src, 1), :], buf.at[hf, kk, pl.ds(r, 1), :],
                                          sem.at[hf]).start()
            return 0
        lax.fori_loop(0, th // ROW_UNROLL, issue, 0)

    for hf in range(n_half):
        issue_half(hf)
    wt_all = wt_ref[...].T
    for hf in range(n_half):
        pltpu.make_async_copy(buf.at[hf], buf.at[hf], sem.at[hf]).wait()
        rows = pl.ds(hf * th, th)
        wt = wt_all[hf * th:(hf + 1) * th, :]
        ffn = wt[:, 0:1] * buf[hf, 0]
        for kk in range(1, TOP_K):
            ffn = ffn + wt[:, kk:kk + 1] * buf[hf, kk]
        x2 = x1_ref[0, rows, :] + gf_ref[0] * ffn
        o_ref[0, rows, :] = _rms(x2, gfin_ref[...]) * (1.0 + sco_ref[0]) + sho_ref[0]


def _combine(pos_t, x1, wt_t, mod3, g_final, fmod3, ys, tm=512):
    bsz, seq, d = x1.shape
    tm = min(tm, seq)
    spb = seq // tm
    return pl.pallas_call(
        _combine_body,
        grid=(bsz, spb),
        in_specs=[pl.BlockSpec((TOP_K, tm), lambda b, i: (0, b * spb + i), memory_space=pltpu.SMEM),
                  pl.BlockSpec((1, tm, d), lambda b, i: (b, i, 0)),
                  pl.BlockSpec((WT_ROWS, tm), lambda b, i: (0, b * spb + i)),
                  pl.BlockSpec((1, 1, d), lambda b, i: (b, 0, 5)),
                  pl.BlockSpec((1, d), lambda b, i: (0, 0)),
                  pl.BlockSpec((1, 1, d), lambda b, i: (b, 0, 0)),
                  pl.BlockSpec((1, 1, d), lambda b, i: (b, 0, 1)),
                  pl.BlockSpec(memory_space=pl.ANY)],
        out_specs=pl.BlockSpec((1, tm, d), lambda b, i: (b, i, 0)),
        out_shape=jax.ShapeDtypeStruct((bsz, seq, d), F32),
        scratch_shapes=[pltpu.VMEM((2, TOP_K, tm // 2, d), F32), pltpu.SemaphoreType.DMA((2,))],
        compiler_params=_cparams("arbitrary", "arbitrary"),
        name="combine",
    )(pos_t, x1, wt_t, mod3, g_final.reshape(1, d), fmod3, fmod3, ys)


def kernel(x, c, positions, ada_w, ada_b, final_ada_w, final_ada_b, norm_mix_g, norm_ffn_g, norm_final_g, w_in, ssm_lambda_re, ssm_lambda_im, ssm_log_dt, ssm_b_re, ssm_b_im, ssm_c_re, ssm_c_im, ssm_d, ssm_w_glu, ssm_b_glu, out_norm_ssm_g, out_norm_attn_g, w_out, router_w, router_b, exp_w_gate, exp_b_gate, exp_w_up, exp_b_up, exp_w_down, exp_b_down):
    bsz, seq, d = x.shape
    depth = ada_w.shape[0]
    ssm_w = ssm_d.shape[-1]
    attn_w = (w_in.shape[-1] - ssm_w) // 3
    ne = router_w.shape[-1]
    n_tok = bsz * seq
    nc = seq // SSM_CHUNK
    tg = 256

    rope_tab, rope_exp = _rope_tables(positions)
    fmod3 = _adaln(c, final_ada_w, final_ada_b).reshape(bsz, 1, 2 * d)
    for l in range(depth):
        mod3 = _adaln(c, ada_w[l], ada_b[l]).reshape(bsz, 1, -1)
        u4, q, k, v = _inproj(x, norm_mix_g[l], mod3, w_in[l].astype(BF16), rope_tab, rope_exp, ssm_w, attn_w)
        tables = _s5_params(ssm_lambda_re[l], ssm_lambda_im[l], ssm_log_dt[l], ssm_b_re[l], ssm_b_im[l],
                            ssm_c_re[l], ssm_c_im[l], ssm_d[l], nc)
        nblk = ssm_w // LANES
        y4 = _s5(u4.reshape(nblk, bsz, nc, SSM_CHUNK * LANES), *tables).reshape(nblk, bsz, seq, LANES)
        attn = _attention(q, k, v)
        x1, h2, idx_t, wt_t = _merge(y4, attn, x, ssm_w_glu[l].astype(BF16), ssm_b_glu[l], out_norm_ssm_g[l],
                                     out_norm_attn_g[l], w_out[l].astype(BF16), mod3, norm_ffn_g[l],
                                     router_w[l], router_b[l])
        rank_t, counts = _rank(idx_t, ne)
        items, offs = _plan_items(counts[:, 0], n_tok * TOP_K, tg)
        pos_t = (offs[idx_t] + rank_t).astype(I32)
        xs = _dispatch(h2.reshape(n_tok, d), pos_t)
        ys = _experts(xs, items, exp_w_gate[l], exp_b_gate[l], exp_w_up[l], exp_b_up[l],
                      exp_w_down[l], exp_b_down[l], tg)
        if l + 1 < depth:
            raise NotImplementedError("depth > 1 needs the non-final combine")
        x = _combine(pos_t, x1, wt_t, mod3, norm_final_g, fmod3, ys)
    return x
```

```python
import functools
import math

import jax
import jax.numpy as jnp
from jax import lax
from jax.experimental import pallas as pl
from jax.experimental.pallas import tpu as pltpu

F32 = jnp.float32
BF16 = jnp.bfloat16
I32 = jnp.int32
HIGHEST = lax.Precision.HIGHEST

LANES = 128
HEAD_DIM = 64
MOBA_BLOCK = 256
MOBA_TOPK = 3
ROT_DIM = HEAD_DIM // 4
ROPE_THETA = 500000.0
SSM_GROUP = 16
SSM_CHUNK = 16
TOP_K = 4
SWIGLU_ALPHA = 1.702
SWIGLU_LIMIT = 7.0
NORM_EPS = 1e-5
NEG_INF = -1e30
KV_TILE = 2 * MOBA_BLOCK
ROW_UNROLL = 8
WT_ROWS = 8
VMEM_LIMIT = 56 * 1024 * 1024


def _cparams(*sem):
    return pltpu.CompilerParams(dimension_semantics=sem, vmem_limit_bytes=VMEM_LIMIT)


def _dot(a, b):
    return jnp.dot(a, b, preferred_element_type=F32)


def _dot_nt(a, b):
    return lax.dot_general(a, b, (((1,), (1,)), ((), ())), preferred_element_type=F32)


def _rms(x, g):
    return x * lax.rsqrt(jnp.mean(x * x, axis=-1, keepdims=True) + NORM_EPS) * g


def _adaln_body(c_ref, w_ref, b_ref, o_ref):
    c = c_ref[...]
    ca = c * jax.nn.sigmoid(c)
    o_ref[...] = jnp.dot(ca, w_ref[...], preferred_element_type=F32, precision=HIGHEST) + b_ref[...]


def _adaln(c, w, b, tn=512):
    bsz, d = c.shape
    n = w.shape[1]
    return pl.pallas_call(
        _adaln_body,
        grid=(n // tn,),
        in_specs=[pl.BlockSpec((bsz, d), lambda j: (0, 0)),
                  pl.BlockSpec((d, tn), lambda j: (0, j)),
                  pl.BlockSpec((1, tn), lambda j: (0, j))],
        out_specs=pl.BlockSpec((bsz, tn), lambda j: (0, j)),
        out_shape=jax.ShapeDtypeStruct((bsz, n), F32),
        compiler_params=_cparams("parallel"),
        name="adaln",
    )(c, w, b.reshape(1, n))


def _inproj_body(x_ref, g_ref, sh_ref, sc_ref, w_ref, tab_ref, exp_ref, u_ref, q_ref, k_ref, v_ref):
    x = x_ref[0]
    tm = x.shape[0]
    h = _rms(x, g_ref[...]) * (1.0 + sc_ref[0]) + sh_ref[0]
    proj = _dot(h.astype(BF16), w_ref[...])
    ssm_w = u_ref.shape[0] * LANES
    aw = q_ref.shape[2]
    for j in range(u_ref.shape[0]):
        u_ref[j, 0] = proj[:, LANES * j:LANES * (j + 1)].astype(BF16)
    tab = lax.dot_general(tab_ref[0], exp_ref[...], (((0,), (0,)), ((), ())),
                          preferred_element_type=F32, precision=HIGHEST)
    lane1 = lax.broadcasted_iota(I32, (tm, LANES), 1)
    cos1 = tab[:, :LANES] + jnp.where((lane1 & (HEAD_DIM - 1)) >= ROT_DIM, 1.0, 0.0)
    reps = aw // LANES
    cosf = jnp.concatenate([cos1] * reps, axis=1)
    sinf = jnp.concatenate([tab[:, LANES:]] * reps, axis=1)
    lane = lax.broadcasted_iota(I32, (tm, aw), 1)
    first_half = (lane & (HEAD_DIM - 1)) < (ROT_DIM // 2)

    def rope(t):
        partner = jnp.where(first_half, pltpu.roll(t, aw - ROT_DIM // 2, 1), pltpu.roll(t, ROT_DIM // 2, 1))
        return t * cosf + partner * sinf

    q = rope(proj[:, ssm_w:ssm_w + aw]) * (HEAD_DIM ** -0.5)
    k = rope(proj[:, ssm_w + aw:ssm_w + 2 * aw])
    q_ref[0] = q.astype(BF16)
    k_ref[0] = k.astype(BF16)
    v_ref[0] = proj[:, ssm_w + 2 * aw:ssm_w + 3 * aw].astype(BF16)


def _inproj(x, gain, mod3, w_in_bf, rope_tab, rope_exp, ssm_w, attn_w, tm=512):
    bsz, seq, d = x.shape
    n_u = ssm_w // LANES
    tm = min(tm, seq)
    return pl.pallas_call(
        _inproj_body,
        grid=(bsz, seq // tm),
        in_specs=[pl.BlockSpec((1, tm, d), lambda b, i: (b, i, 0)),
                  pl.BlockSpec((1, d), lambda b, i: (0, 0)),
                  pl.BlockSpec((1, 1, d), lambda b, i: (b, 0, 0)),
                  pl.BlockSpec((1, 1, d), lambda b, i: (b, 0, 1)),
                  pl.BlockSpec(w_in_bf.shape, lambda b, i: (0, 0)),
                  pl.BlockSpec((1, ROT_DIM, tm), lambda b, i: (b, 0, i)),
                  pl.BlockSpec(rope_exp.shape, lambda b, i: (0, 0))],
        out_specs=[pl.BlockSpec((n_u, 1, tm, LANES), lambda b, i: (0, b, i, 0)),
                   pl.BlockSpec((1, tm, attn_w), lambda b, i: (b, i, 0)),
                   pl.BlockSpec((1, tm, attn_w), lambda b, i: (b, i, 0)),
                   pl.BlockSpec((1, tm, attn_w), lambda b, i: (b, i, 0))],
        out_shape=[jax.ShapeDtypeStruct((n_u, bsz, seq, LANES), BF16),
                   jax.ShapeDtypeStruct((bsz, seq, attn_w), BF16),
                   jax.ShapeDtypeStruct((bsz, seq, attn_w), BF16),
                   jax.ShapeDtypeStruct((bsz, seq, attn_w), BF16)],
        compiler_params=_cparams("parallel", "parallel"),
        name="inproj",
    )(x, gain.reshape(1, d), mod3, mod3, w_in_bf, rope_tab, rope_exp)


def _rope_tables(positions):
    half = ROT_DIM // 2
    inv_freq = ROPE_THETA ** (-jnp.arange(0, ROT_DIM, 2, dtype=F32) / ROT_DIM)
    ang = positions.astype(F32)[:, None, :] * inv_freq[None, :, None]
    tab = jnp.concatenate([jnp.cos(ang), jnp.sin(ang)], axis=1)
    i = jnp.arange(ROT_DIM)[:, None]
    hl = jnp.arange(LANES)[None, :] % HEAD_DIM
    e_cos = jnp.where((i < half) & ((hl == i) | (hl == i + half)), 1.0, 0.0)
    e_sin = jnp.where(i >= half, jnp.where(hl == i - half, -1.0, jnp.where(hl == i, 1.0, 0.0)), 0.0)
    return tab, jnp.concatenate([e_cos, e_sin], axis=1).astype(F32)


def _s5_params(lam_re, lam_im, log_dt, b_re, b_im, c_re, c_im, d_skip, n_chunks):
    g_all, p = lam_re.shape
    hc = b_re.shape[-1]
    t = SSM_CHUNK
    gpb = LANES // hc
    nblk = g_all // gpb
    lr, li = lam_re.astype(F32), lam_im.astype(F32)
    dt = jnp.exp(log_dt.astype(F32))[:, None]
    ldr, ldi = lr * dt, li * dt

    def apow(k):
        k = jnp.asarray(k, F32)[..., None, None]
        mag = jnp.exp(ldr * k)
        return mag * jnp.cos(ldi * k), mag * jnp.sin(ldi * k)

    ar, ai = apow(1.0)
    zr, zi = ar - 1.0, ai
    den = lr * lr + li * li
    cr = (zr * lr + zi * li) / den
    ci = (zi * lr - zr * li) / den
    bre, bim = b_re.astype(F32), b_im.astype(F32)
    bbr = cr[..., None] * bre - ci[..., None] * bim
    bbi = cr[..., None] * bim + ci[..., None] * bre
    cre, cim = c_re.astype(F32), c_im.astype(F32)
    eye = jnp.eye(gpb, dtype=F32)

    pr, pi = apow(jnp.arange(t))
    abr = pr[..., None] * bbr - pi[..., None] * bbi
    abi = pr[..., None] * bbi + pi[..., None] * bbr
    kf = (jnp.einsum('gnp,kgph->kghn', cre, abr, precision=HIGHEST)
          - jnp.einsum('gnp,kgph->kghn', cim, abi, precision=HIGHEST))
    kblk = jnp.einsum('kjghn,gf->kjghfn', kf.reshape(t, nblk, gpb, hc, hc), eye)
    kblk = kblk.reshape(t, nblk, LANES, LANES)
    kpad = jnp.concatenate([jnp.zeros_like(kblk[:1]), kblk], axis=0)
    pm = jnp.concatenate([kpad[0:t], kpad[1:t + 1]], axis=-1)
    kstack = jnp.transpose(pm[::-1], (1, 0, 2, 3)).reshape(nblk, t * LANES, 2 * LANES)

    def inject(ab):
        ab = ab[::-1].reshape(t, nblk, gpb, p, hc)
        return jnp.einsum('sjgph,gf->jsghfp', ab, eye).reshape(nblk, t * LANES, gpb * p)

    bm = jnp.concatenate([inject(abr), inject(abi)], axis=-1)

    qr, qi = apow(jnp.arange(t) + 1.0)
    c_from_re = cre[None] * qr[:, :, None, :] - cim[None] * qi[:, :, None, :]
    c_from_im = -cre[None] * qi[:, :, None, :] - cim[None] * qr[:, :, None, :]

    def readout(cc):
        cc = cc.reshape(t, nblk, gpb, hc, p)
        return jnp.einsum('tjgnp,gf->jgptfn', cc, eye).reshape(nblk, gpb * p, t * LANES)

    cm = jnp.concatenate([readout(c_from_re), readout(c_from_im)], axis=1)

    n_steps = max(1, int(math.log2(n_chunks)))
    shifts = [float(t * (1 << s)) for s in range(n_steps)]
    sr, si = apow(jnp.asarray(shifts))
    ap = jnp.concatenate([sr.reshape(n_steps, nblk, gpb * p), si.reshape(n_steps, nblk, gpb * p)], axis=-1)
    ap = jnp.transpose(ap, (1, 0, 2))
    dvec = jnp.tile(d_skip.astype(F32).reshape(nblk, 1, LANES), (1, 1, t))
    return kstack.astype(BF16), bm.astype(BF16), cm.astype(BF16), ap, dvec


def _s5_body(x_ref, ks_ref, bm_ref, cm_ref, ap_ref, d_ref, y_ref, h_ref):
    nc = x_ref.shape[2]
    t = SSM_CHUNK
    pad = h_ref.shape[0] - nc
    half = h_ref.shape[1] // 2
    x = x_ref[0, 0]
    h_ref[0:pad] = jnp.zeros((pad, h_ref.shape[1]), F32)
    h_ref[pad:pad + nc] = _dot(x, bm_ref[0])
    for step in range(ap_ref.shape[1]):
        d = 1 << step
        cur = h_ref[pad:pad + nc]
        sft = h_ref[pad - d:pad + nc - d]
        ar = ap_ref[0, step:step + 1, 0:half]
        ai = ap_ref[0, step:step + 1, half:]
        cr, ci = cur[:, :half], cur[:, half:]
        sr, si = sft[:, :half], sft[:, half:]
        h_ref[pad:pad + nc, 0:half] = cr + ar * sr - ai * si
        h_ref[pad:pad + nc, half:] = ci + ar * si + ai * sr
    h_prev = h_ref[pad - 1:pad + nc - 1]
    ys = _dot(h_prev.astype(BF16), cm_ref[0])
    for t0 in range(0, t, 2):
        lo, hi = t0 * LANES, (t0 + 2) * LANES
        conv = _dot(x[:, :hi], ks_ref[0, (t - 2 - t0) * LANES:, :])
        y = conv + ys[:, lo:hi] + d_ref[0][:, lo:hi] * x[:, lo:hi].astype(F32)
        y_ref[0, 0, :, lo:hi] = y.astype(y_ref.dtype)


def _s5(u4, kstack, bm, cm, ap, dvec):
    nblk, bsz, nc, w = u4.shape
    sw = bm.shape[-1]
    pad = max(nc // 2, 8)
    return pl.pallas_call(
        _s5_body,
        grid=(nblk, bsz),
        in_specs=[pl.BlockSpec((1, 1, nc, w), lambda j, b: (j, b, 0, 0)),
                  pl.BlockSpec((1,) + kstack.shape[1:], lambda j, b: (j, 0, 0)),
                  pl.BlockSpec((1,) + bm.shape[1:], lambda j, b: (j, 0, 0)),
                  pl.BlockSpec((1,) + cm.shape[1:], lambda j, b: (j, 0, 0)),
                  pl.BlockSpec((1,) + ap.shape[1:], lambda j, b: (j, 0, 0)),
                  pl.BlockSpec((1, 1, w), lambda j, b: (j, 0, 0))],
        out_specs=pl.BlockSpec((1, 1, nc, w), lambda j, b: (j, b, 0, 0)),
        out_shape=jax.ShapeDtypeStruct((nblk, bsz, nc, w), BF16),
        scratch_shapes=[pltpu.VMEM((pad + nc, sw), F32)],
        compiler_params=_cparams("parallel", "parallel"),
        name="s5",
    )(u4, kstack, bm, cm, ap, dvec)


def _attn_body(q_ref, k_ref, v_ref, o_ref, ka_ref, va_ref, km_ref):
    qi = pl.program_id(2)
    seq = k_ref.shape[1]
    tq = q_ref.shape[1]
    nb = seq // MOBA_BLOCK
    blk_shift = MOBA_BLOCK.bit_length() - 1

    @pl.when(qi == 0)
    def _build():
        k = k_ref[0].astype(F32)
        v = v_ref[0].astype(F32)
        lane = lax.broadcasted_iota(I32, (seq, LANES), 1)
        blk = lax.shift_right_logical(lax.broadcasted_iota(I32, (seq, LANES), 0), blk_shift)
        ka_ref[0] = jnp.where(lane < HEAD_DIM, k, jnp.where(lane - HEAD_DIM == blk, 1.0, 0.0)).astype(BF16)
        ka_ref[1] = jnp.where(lane >= HEAD_DIM, k, jnp.where(lane == blk, 1.0, 0.0)).astype(BF16)
        va_ref[0] = jnp.where(lane < HEAD_DIM, v, jnp.where(lane == HEAD_DIM, 1.0, 0.0)).astype(BF16)
        va_ref[1] = jnp.where(lane >= HEAD_DIM, v, jnp.where(lane == 0, 1.0, 0.0)).astype(BF16)
        km = jnp.mean(k.reshape(nb, MOBA_BLOCK, LANES), axis=1)
        lane_b = lax.broadcasted_iota(I32, (nb, LANES), 1)
        km_ref[...] = jnp.zeros(km_ref.shape, F32)
        km_ref[0, HEAD_DIM:HEAD_DIM + nb, :] = jnp.where(lane_b < HEAD_DIM, km, 0.0)
        km_ref[1, 0:nb, :] = jnp.where(lane_b >= HEAD_DIM, km, 0.0)

    q_f32 = q_ref[0].astype(F32)
    lane_q = lax.broadcasted_iota(I32, (tq, LANES), 1)

    def augmented_q(hx):
        r0 = HEAD_DIM if hx == 0 else 0
        gate_t = _dot_nt(km_ref[hx], q_f32)
        g = gate_t[r0:r0 + nb, :]
        jidx = lax.broadcasted_iota(I32, (nb, tq), 0)
        qblk = qi * (tq // MOBA_BLOCK) + lax.shift_right_logical(
            lax.broadcasted_iota(I32, (nb, tq), 1), blk_shift)
        cnt = jnp.zeros((nb, tq), F32)
        for jp in range(nb):
            row = g[jp:jp + 1, :]
            beats = jnp.where(row > g, 1.0, jnp.where(row == g, jnp.where(jp < jidx, 1.0, 0.0), 0.0))
            cnt = cnt + jnp.where(jp < qblk, beats, 0.0)
        past_sel = jnp.where(jidx < qblk, jnp.where(cnt < MOBA_TOPK, 1.0, 0.0), 0.0)
        sel = jnp.where(jidx == qblk, 1.0, past_sel)
        bias = jnp.where(sel > 0.5, 0.0, NEG_INF)
        parts = []
        if r0 > 0:
            parts.append(jnp.zeros((r0, tq), F32))
        parts.append(bias)
        if LANES - r0 - nb > 0:
            parts.append(jnp.zeros((LANES - r0 - nb, tq), F32))
        bias_q = jnp.concatenate(parts, axis=0).T
        data = (lane_q < HEAD_DIM) if hx == 0 else (lane_q >= HEAD_DIM)
        return jnp.where(data, q_f32, bias_q).astype(BF16)

    q_aug = [augmented_q(0), augmented_q(1)]

    own_off = pl.multiple_of(qi * KV_TILE, KV_TILE)
    causal = (lax.broadcasted_iota(I32, (tq, KV_TILE), 1) <= lax.broadcasted_iota(I32, (tq, KV_TILE), 0))
    state = []
    for hx in range(2):
        s = _dot_nt(q_aug[hx], ka_ref[hx, pl.ds(own_off, KV_TILE), :])
        s = jnp.where(causal, s, NEG_INF)
        m = jnp.max(s, axis=1, keepdims=True)
        p = jnp.exp(s - m)
        state += [m, _dot(p.astype(BF16), va_ref[hx, pl.ds(own_off, KV_TILE), :])]

    def body(jj, carry):
        off = pl.multiple_of(jj * KV_TILE, KV_TILE)
        out = []
        for hx in range(2):
            m_c, acc_c = carry[2 * hx], carry[2 * hx + 1]
            s_j = _dot_nt(q_aug[hx], ka_ref[hx, pl.ds(off, KV_TILE), :])
            m_n = jnp.maximum(m_c, jnp.max(s_j, axis=1, keepdims=True))
            alpha = jnp.exp(m_c - m_n)
            p_j = jnp.exp(s_j - m_n)
            out += [m_n, acc_c * alpha + _dot(p_j.astype(BF16), va_ref[hx, pl.ds(off, KV_TILE), :])]
        return tuple(out)

    _, acc_a, _, acc_b = lax.fori_loop(0, qi, body, tuple(state))
    l_a = jnp.sum(jnp.where(lane_q == HEAD_DIM, acc_a, 0.0), axis=1, keepdims=True)
    l_b = jnp.sum(jnp.where(lane_q == 0, acc_b, 0.0), axis=1, keepdims=True)
    o_ref[0] = jnp.where(lane_q < HEAD_DIM, acc_a / l_a, acc_b / l_b).astype(o_ref.dtype)


def _attention(q, k, v):
    bsz, seq, aw = q.shape
    npair = aw // LANES
    tq = KV_TILE
    nb = seq // MOBA_BLOCK
    assert seq % KV_TILE == 0 and nb % 8 == 0 and nb <= HEAD_DIM
    return pl.pallas_call(
        _attn_body,
        grid=(bsz, npair, seq // tq),
        in_specs=[pl.BlockSpec((1, tq, LANES), lambda b, p, i: (b, i, p)),
                  pl.BlockSpec((1, seq, LANES), lambda b, p, i: (b, 0, p)),
                  pl.BlockSpec((1, seq, LANES), lambda b, p, i: (b, 0, p))],
        out_specs=pl.BlockSpec((1, tq, LANES), lambda b, p, i: (b, i, p)),
        out_shape=jax.ShapeDtypeStruct((bsz, seq, aw), BF16),
        scratch_shapes=[pltpu.VMEM((2, seq, LANES), BF16),
                        pltpu.VMEM((2, seq, LANES), BF16),
                        pltpu.VMEM((2, LANES, LANES), F32)],
        compiler_params=_cparams("parallel", "parallel", "arbitrary"),
        name="attn",
    )(q, k, v)


def _merge_body(y_ref, a_ref, x_ref, wglu_ref, bglu_ref, gs_ref, ga_ref, wout_ref, gm_ref,
                gf_ref, shf_ref, scf_ref, rw_ref, rb_ref, x1_ref, h2_ref, idx_ref, wt_ref, cnt_ref):
    nblk = y_ref.shape[0]
    y = jnp.concatenate([y_ref[j, 0] for j in range(nblk)], axis=1).astype(F32)
    tm = y.shape[0]
    g = 0.5 * y * (1.0 + lax.erf(y * (2.0 ** -0.5)))
    glu = g * jax.nn.sigmoid(_dot(g.astype(BF16), wglu_ref[...]) + bglu_ref[...])
    ssm_n = _rms(glu, gs_ref[...])
    att_n = _rms(a_ref[0].astype(F32), ga_ref[...])
    merged = jnp.concatenate([ssm_n, att_n], axis=1).astype(BF16)
    mix = _dot(merged, wout_ref[...])
    x1 = x_ref[0] + gm_ref[0] * mix
    x1_ref[0] = x1
    h2 = _rms(x1, gf_ref[...]) * (1.0 + scf_ref[0]) + shf_ref[0]
    h2_ref[0] = h2
    logits = lax.dot_general(rw_ref[...], h2, (((1,), (1,)), ((), ())),
                             preferred_element_type=F32, precision=HIGHEST) + rb_ref[...]
    ne = logits.shape[0]
    sub = lax.broadcasted_iota(I32, (ne, tm), 0)
    vals, idxs = [], []
    for _ in range(TOP_K):
        mx = jnp.max(logits, axis=0, keepdims=True)
        ix = jnp.min(jnp.where(logits == mx, sub, ne), axis=0, keepdims=True)
        vals.append(mx)
        idxs.append(ix)
        logits = jnp.where(sub == ix, -jnp.inf, logits)
    exps = [jnp.exp(vv - vals[0]) for vv in vals]
    tot = exps[0]
    for e in exps[1:]:
        tot = tot + e
    row_i = lax.broadcasted_iota(I32, idx_ref.shape, 0)
    row_w = lax.broadcasted_iota(I32, wt_ref.shape, 0)
    idx_out = jnp.zeros(idx_ref.shape, I32)
    wt_out = jnp.zeros(wt_ref.shape, F32)
    for kk in range(TOP_K):
        idx_out = jnp.where(row_i == kk, idxs[kk], idx_out)
        wt_out = jnp.where(row_w == kk, exps[kk] / tot, wt_out)
    idx_ref[...] = idx_out
    wt_ref[...] = wt_out
    chosen = jnp.zeros((ne, tm), F32)
    for kk in range(TOP_K):
        chosen = chosen + jnp.where(sub == idxs[kk], 1.0, 0.0)
    cnt_ref[0] = jnp.sum(chosen, axis=1, keepdims=True).astype(I32)


def _merge(y4, attn, x, w_glu_bf, b_glu, g_ssm, g_attn, w_out_bf, mod3, g_ffn, router_w, router_b, tm=512):
    bsz, seq, d = x.shape
    nblk = y4.shape[0]
    ssm_w = nblk * LANES
    aw = attn.shape[-1]
    ne = router_w.shape[1]
    tm = min(tm, seq)
    spb = seq // tm
    n_tok = bsz * seq
    rw_t = router_w.T
    row = lambda n: pl.BlockSpec((1, n), lambda b, i: (0, 0))
    full = lambda a: pl.BlockSpec(a.shape, lambda b, i: (0,) * a.ndim)
    modv = lambda j: pl.BlockSpec((1, 1, d), lambda b, i: (b, 0, j))
    return pl.pallas_call(
        _merge_body,
        grid=(bsz, seq // tm),
        in_specs=[pl.BlockSpec((nblk, 1, tm, LANES), lambda b, i: (0, b, i, 0)),
                  pl.BlockSpec((1, tm, aw), lambda b, i: (b, i, 0)),
                  pl.BlockSpec((1, tm, d), lambda b, i: (b, i, 0)),
                  full(w_glu_bf), row(ssm_w), row(ssm_w), row(aw), full(w_out_bf),
                  modv(2), row(d), modv(3), modv(4), full(rw_t),
                  pl.BlockSpec((ne, 1), lambda b, i: (0, 0))],
        out_specs=[pl.BlockSpec((1, tm, d), lambda b, i: (b, i, 0)),
                   pl.BlockSpec((1, tm, d), lambda b, i: (b, i, 0)),
                   pl.BlockSpec((TOP_K, tm), lambda b, i: (0, b * spb + i)),
                   pl.BlockSpec((WT_ROWS, tm), lambda b, i: (0, b * spb + i)),
                   pl.BlockSpec((1, ne, 1), lambda b, i: (b * spb + i, 0, 0))],
        out_shape=[jax.ShapeDtypeStruct((bsz, seq, d), F32),
                   jax.ShapeDtypeStruct((bsz, seq, d), F32),
                   jax.ShapeDtypeStruct((TOP_K, n_tok), I32),
                   jax.ShapeDtypeStruct((WT_ROWS, n_tok), F32),
                   jax.ShapeDtypeStruct((bsz * spb, ne, 1), I32)],
        compiler_params=_cparams("parallel", "parallel"),
        name="merge",
    )(y4, attn, x, w_glu_bf, b_glu.reshape(1, ssm_w), g_ssm.reshape(1, ssm_w), g_attn.reshape(1, aw),
      w_out_bf, mod3, g_ffn.reshape(1, d), mod3, mod3, rw_t, router_b.reshape(ne, 1))


def _rank_body(idx_ref, offs_ref, pos_ref, carry_ref, *, ne):
    i = pl.program_id(0)
    tm = idx_ref.shape[1]

    @pl.when(i == 0)
    def _init():
        carry_ref[...] = offs_ref[...].astype(F32)

    idx = idx_ref[...]
    sub = lax.broadcasted_iota(I32, (ne, tm), 0)
    onehot = jnp.zeros((ne, tm), F32)
    for kk in range(TOP_K):
        onehot = onehot + jnp.where(idx[kk:kk + 1, :] == sub, 1.0, 0.0)
    r = lax.broadcasted_iota(I32, (tm, tm), 0)
    c = lax.broadcasted_iota(I32, (tm, tm), 1)
    tri = jnp.where(r < c, 1.0, 0.0).astype(BF16)
    slot = _dot(onehot.astype(BF16), tri) + carry_ref[...]
    row_k = lax.broadcasted_iota(I32, (TOP_K, tm), 0)
    pos = jnp.zeros((TOP_K, tm), F32)
    for kk in range(TOP_K):
        pk = jnp.sum(jnp.where(idx[kk:kk + 1, :] == sub, slot, 0.0), axis=0, keepdims=True)
        pos = jnp.where(row_k == kk, pk, pos)
    pos_ref[...] = pos.astype(I32)
    carry_ref[...] = carry_ref[...] + jnp.sum(onehot, axis=1, keepdims=True)


def _rank(idx_t, offs, tm=512):
    n = idx_t.shape[1]
    ne = offs.shape[0]
    tm = min(tm, n)
    return pl.pallas_call(
        functools.partial(_rank_body, ne=ne),
        grid=(n // tm,),
        in_specs=[pl.BlockSpec((TOP_K, tm), lambda i: (0, i)),
                  pl.BlockSpec((ne, 1), lambda i: (0, 0))],
        out_specs=pl.BlockSpec((TOP_K, tm), lambda i: (0, i)),
        out_shape=jax.ShapeDtypeStruct((TOP_K, n), I32),
        scratch_shapes=[pltpu.VMEM((ne, 1), F32)],
        compiler_params=_cparams("arbitrary"),
        name="rank",
    )(idx_t, offs.reshape(ne, 1))


def _dispatch_body(pos_ref, h_ref, xs_ref, sem):
    tm = h_ref.shape[0]

    def issue(g, _):
        for rr in range(ROW_UNROLL):
            r = g * ROW_UNROLL + rr
            for kk in range(TOP_K):
                dst = pos_ref[kk, r]
                pltpu.make_async_copy(h_ref.at[pl.ds(r, 1), :], xs_ref.at[pl.ds(dst, 1), :], sem).start()
        return 0

    lax.fori_loop(0, tm // ROW_UNROLL, issue, 0)
    done = xs_ref.at[pl.ds(0, tm * TOP_K), :]
    pltpu.make_async_copy(done, done, sem).wait()


def _dispatch(h2, pos_t, tm=512):
    n, d = h2.shape
    tm = min(tm, n)
    return pl.pallas_call(
        _dispatch_body,
        grid=(n // tm,),
        in_specs=[pl.BlockSpec((TOP_K, tm), lambda i: (0, i), memory_space=pltpu.SMEM),
                  pl.BlockSpec((tm, d), lambda i: (i, 0))],
        out_specs=pl.BlockSpec(memory_space=pl.ANY),
        out_shape=jax.ShapeDtypeStruct((n * TOP_K, d), h2.dtype),
        scratch_shapes=[pltpu.SemaphoreType.DMA(())],
        compiler_params=_cparams("arbitrary"),
        name="dispatch",
    )(pos_t, h2)


def _experts_body(tile_ref, exp_ref, lo_ref, hi_ref, first_ref, newe_ref,
                  xs_ref, wg_ref, bg_ref, wu_ref, bu_ref, wd_ref, bd_ref, ys_ref,
                  wg_bf, wu_bf, wd_bf):
    i = pl.program_id(0)
    lo = lo_ref[i]
    hi = hi_ref[i]

    @pl.when(newe_ref[i] == 1)
    def _cast():
        wg_bf[...] = wg_ref[0].astype(BF16)
        wu_bf[...] = wu_ref[0].astype(BF16)
        wd_bf[...] = wd_ref[0].astype(BF16)

    @pl.when(hi > lo)
    def _compute():
        x = xs_ref[...].astype(BF16)
        gate = jnp.minimum(_dot(x, wg_bf[...]) + bg_ref[0], SWIGLU_LIMIT)
        lin = jnp.clip(_dot(x, wu_bf[...]) + bu_ref[0], -SWIGLU_LIMIT, SWIGLU_LIMIT)
        act = gate * jax.nn.sigmoid(SWIGLU_ALPHA * gate) * (lin + 1.0)
        y = _dot(act.astype(BF16), wd_bf[...]) + bd_ref[0]
        row = lax.broadcasted_iota(I32, y.shape, 0)
        mine = (row >= lo) & (row < hi)

        @pl.when(first_ref[i] == 1)
        def _():
            ys_ref[...] = jnp.where(mine, y, 0.0)

        @pl.when(first_ref[i] == 0)
        def _():
            ys_ref[...] = jnp.where(mine, y, ys_ref[...])


def _experts(xs, items, w_gate, b_gate, w_up, b_up, w_down, b_down, tg):
    nk, d = xs.shape
    ne, _, dff = w_gate.shape
    n_items = items[0].shape[0]
    wspec = lambda shp: pl.BlockSpec((1,) + shp, lambda i, t, e, lo, hi, f, nw: (e[i], 0, 0))
    return pl.pallas_call(
        _experts_body,
        grid_spec=pltpu.PrefetchScalarGridSpec(
            num_scalar_prefetch=6,
            grid=(n_items,),
            in_specs=[pl.BlockSpec((tg, d), lambda i, t, e, lo, hi, f, nw: (t[i], 0)),
                      wspec((d, dff)), wspec((1, dff)),
                      wspec((d, dff)), wspec((1, dff)),
                      wspec((dff, d)), wspec((1, d))],
            out_specs=pl.BlockSpec((tg, d), lambda i, t, e, lo, hi, f, nw: (t[i], 0)),
            scratch_shapes=[pltpu.VMEM((d, dff), BF16), pltpu.VMEM((d, dff), BF16), pltpu.VMEM((dff, d), BF16)],
        ),
        out_shape=jax.ShapeDtypeStruct((nk, d), F32),
        compiler_params=_cparams("arbitrary"),
        name="experts",
    )(*items, xs, w_gate, b_gate.reshape(ne, 1, dff), w_up, b_up.reshape(ne, 1, dff),
      w_down, b_down.reshape(ne, 1, d))


def _plan_items(counts, nk, tg):
    ne = counts.shape[0]
    n_tiles = nk // tg
    n_items = n_tiles + ne
    ends = jnp.cumsum(counts)
    offs = ends - counts
    first_tile = offs // tg
    last_tile = jnp.maximum(ends - 1, 0) // tg
    per_e = jnp.where(counts > 0, last_tile - first_tile + 1, 0)
    item_end = jnp.cumsum(per_e)
    item_start = item_end - per_e
    total = item_end[-1]
    ii = jnp.arange(n_items, dtype=I32)
    first_after = lambda i: jnp.minimum(jnp.sum((item_end[None, :] <= i[:, None]).astype(I32), axis=1), ne - 1)
    e_of = first_after(ii)
    valid = ii < total
    last_e = first_after(total[None] - 1)[0]
    e_of = jnp.where(valid, e_of, last_e)
    tile = jnp.where(valid, first_tile[e_of] + (ii - item_start[e_of]), n_tiles - 1).astype(I32)
    lo = jnp.clip(offs[e_of] - tile * tg, 0, tg)
    hi = jnp.clip(ends[e_of] - tile * tg, 0, tg)
    lo = jnp.where(valid, lo, 0).astype(I32)
    hi = jnp.where(valid, hi, 0).astype(I32)
    prev_tile = jnp.concatenate([jnp.full((1,), -1, I32), tile[:-1]])
    prev_e = jnp.concatenate([jnp.full((1,), -1, I32), e_of[:-1]])
    first = (tile != prev_tile).astype(I32)
    newe = (e_of != prev_e).astype(I32)
    return (tile, e_of, lo, hi, first, newe), offs


def _combine_body(pos_ref, x1_ref, wt_ref, gf_ref, gfin_ref, sho_ref, sco_ref, ys_ref, o_ref, buf, sem):
    tm = x1_ref.shape[1]
    n_half = buf.shape[0]
    th = tm // n_half

    def issue_half(hf):
        def issue(g, _):
            for rr in range(ROW_UNROLL):
                r = g * ROW_UNROLL + rr
                for kk in range(TOP_K):
                    src = pos_ref[kk, hf * th + r]
                    pltpu.make_async_copy(ys_ref.at[pl.ds(src, 1), :], buf.at[hf, kk, pl.ds(r, 1), :],
                                          sem.at[hf]).start()
            return 0
        lax.fori_loop(0, th // ROW_UNROLL, issue, 0)

    for hf in range(n_half):
        issue_half(hf)
    wt_all = wt_ref[...].T
    for hf in range(n_half):
        pltpu.make_async_copy(buf.at[hf], buf.at[hf], sem.at[hf]).wait()
        rows = pl.ds(hf * th, th)
        wt = wt_all[hf * th:(hf + 1) * th, :]
        ffn = wt[:, 0:1] * buf[hf, 0]
        for kk in range(1, TOP_K):
            ffn = ffn + wt[:, kk:kk + 1] * buf[hf, kk]
        x2 = x1_ref[0, rows, :] + gf_ref[0] * ffn
        o_ref[0, rows, :] = _rms(x2, gfin_ref[...]) * (1.0 + sco_ref[0]) + sho_ref[0]


def _combine(pos_t, x1, wt_t, mod3, g_final, fmod3, ys, tm=512):
    bsz, seq, d = x1.shape
    tm = min(tm, seq)
    spb = seq // tm
    return pl.pallas_call(
        _combine_body,
        grid=(bsz, spb),
        in_specs=[pl.BlockSpec((TOP_K, tm), lambda b, i: (0, b * spb + i), memory_space=pltpu.SMEM),
                  pl.BlockSpec((1, tm, d), lambda b, i: (b, i, 0)),
                  pl.BlockSpec((WT_ROWS, tm), lambda b, i: (0, b * spb + i)),
                  pl.BlockSpec((1, 1, d), lambda b, i: (b, 0, 5)),
                  pl.BlockSpec((1, d), lambda b, i: (0, 0)),
                  pl.BlockSpec((1, 1, d), lambda b, i: (b, 0, 0)),
                  pl.BlockSpec((1, 1, d), lambda b, i: (b, 0, 1)),
                  pl.BlockSpec(memory_space=pl.ANY)],
        out_specs=pl.BlockSpec((1, tm, d), lambda b, i: (b, i, 0)),
        out_shape=jax.ShapeDtypeStruct((bsz, seq, d), F32),
        scratch_shapes=[pltpu.VMEM((2, TOP_K, tm // 2, d), F32), pltpu.SemaphoreType.DMA((2,))],
        compiler_params=_cparams("arbitrary", "arbitrary"),
        name="combine",
    )(pos_t, x1, wt_t, mod3, g_final.reshape(1, d), fmod3, fmod3, ys)


def kernel(x, c, positions, ada_w, ada_b, final_ada_w, final_ada_b, norm_mix_g, norm_ffn_g, norm_final_g, w_in, ssm_lambda_re, ssm_lambda_im, ssm_log_dt, ssm_b_re, ssm_b_im, ssm_c_re, ssm_c_im, ssm_d, ssm_w_glu, ssm_b_glu, out_norm_ssm_g, out_norm_attn_g, w_out, router_w, router_b, exp_w_gate, exp_b_gate, exp_w_up, exp_b_up, exp_w_down, exp_b_down):
    bsz, seq, d = x.shape
    depth = ada_w.shape[0]
    ssm_w = ssm_d.shape[-1]
    attn_w = (w_in.shape[-1] - ssm_w) // 3
    ne = router_w.shape[-1]
    n_tok = bsz * seq
    nc = seq // SSM_CHUNK
    tg = 256

    rope_tab, rope_exp = _rope_tables(positions)
    fmod3 = _adaln(c, final_ada_w, final_ada_b).reshape(bsz, 1, 2 * d)
    for l in range(depth):
        mod3 = _adaln(c, ada_w[l], ada_b[l]).reshape(bsz, 1, -1)
        u4, q, k, v = _inproj(x, norm_mix_g[l], mod3, w_in[l].astype(BF16), rope_tab, rope_exp, ssm_w, attn_w)
        tables = _s5_params(ssm_lambda_re[l], ssm_lambda_im[l], ssm_log_dt[l], ssm_b_re[l], ssm_b_im[l],
                            ssm_c_re[l], ssm_c_im[l], ssm_d[l], nc)
        nblk = ssm_w // LANES
        y4 = _s5(u4.reshape(nblk, bsz, nc, SSM_CHUNK * LANES), *tables).reshape(nblk, bsz, seq, LANES)
        attn = _attention(q, k, v)
        x1, h2, idx_t, wt_t, tile_cnt = _merge(y4, attn, x, ssm_w_glu[l].astype(BF16), ssm_b_glu[l], out_norm_ssm_g[l],
                                     out_norm_attn_g[l], w_out[l].astype(BF16), mod3, norm_ffn_g[l],
                                     router_w[l], router_b[l])
        items, offs = _plan_items(jnp.sum(tile_cnt[:, :, 0], axis=0), n_tok * TOP_K, tg)
        pos_t = _rank(idx_t, offs)
        xs = _dispatch(h2.reshape(n_tok, d), pos_t)
        ys = _experts(xs, items, exp_w_gate[l], exp_b_gate[l], exp_w_up[l], exp_b_up[l],
                      exp_w_down[l], exp_b_down[l], tg)
        if l + 1 < depth:
            raise NotImplementedError("depth > 1 needs the non-final combine")
        x = _combine(pos_t, x1, wt_t, mod3, norm_final_g, fmod3, ys)
    return x
```

```python
import functools
import math

import jax
import jax.numpy as jnp
from jax import lax
from jax.experimental import pallas as pl
from jax.experimental.pallas import tpu as pltpu

F32 = jnp.float32
BF16 = jnp.bfloat16
I32 = jnp.int32
HIGHEST = lax.Precision.HIGHEST

LANES = 128
HEAD_DIM = 64
MOBA_BLOCK = 256
MOBA_TOPK = 3
ROT_DIM = HEAD_DIM // 4
ROPE_THETA = 500000.0
SSM_GROUP = 16
SSM_CHUNK = 16
TOP_K = 4
SWIGLU_ALPHA = 1.702
SWIGLU_LIMIT = 7.0
NORM_EPS = 1e-5
NEG_INF = -1e30
LOG2_E = math.log2(math.e)
KV_TILE = 2 * MOBA_BLOCK
ROW_UNROLL = 8
WT_ROWS = 8
VMEM_LIMIT = 56 * 1024 * 1024


def _cparams(*sem):
    return pltpu.CompilerParams(dimension_semantics=sem, vmem_limit_bytes=VMEM_LIMIT)


def _dot(a, b):
    return jnp.dot(a, b, preferred_element_type=F32)


def _dot_nt(a, b):
    return lax.dot_general(a, b, (((1,), (1,)), ((), ())), preferred_element_type=F32)


def _rms(x, g):
    return x * lax.rsqrt(jnp.mean(x * x, axis=-1, keepdims=True) + NORM_EPS) * g


def _adaln_body(c_ref, w_ref, b_ref, o_ref):
    c = c_ref[...]
    ca = c * jax.nn.sigmoid(c)
    o_ref[...] = jnp.dot(ca, w_ref[...], preferred_element_type=F32, precision=HIGHEST) + b_ref[...]


def _adaln(c, w, b, tn=512):
    bsz, d = c.shape
    n = w.shape[1]
    return pl.pallas_call(
        _adaln_body,
        grid=(n // tn,),
        in_specs=[pl.BlockSpec((bsz, d), lambda j: (0, 0)),
                  pl.BlockSpec((d, tn), lambda j: (0, j)),
                  pl.BlockSpec((1, tn), lambda j: (0, j))],
        out_specs=pl.BlockSpec((bsz, tn), lambda j: (0, j)),
        out_shape=jax.ShapeDtypeStruct((bsz, n), F32),
        compiler_params=_cparams("parallel"),
        name="adaln",
    )(c, w, b.reshape(1, n))


def _inproj_body(x_ref, g_ref, sh_ref, sc_ref, w_ref, tab_ref, exp_ref, u_ref, q_ref, k_ref, v_ref):
    x = x_ref[0]
    tm = x.shape[0]
    h = _rms(x, g_ref[...]) * (1.0 + sc_ref[0]) + sh_ref[0]
    proj = _dot(h.astype(BF16), w_ref[...])
    ssm_w = u_ref.shape[0] * LANES
    aw = q_ref.shape[2]
    for j in range(u_ref.shape[0]):
        u_ref[j, 0] = proj[:, LANES * j:LANES * (j + 1)].astype(BF16)
    tab = lax.dot_general(tab_ref[0], exp_ref[...], (((0,), (0,)), ((), ())),
                          preferred_element_type=F32, precision=HIGHEST)
    lane1 = lax.broadcasted_iota(I32, (tm, LANES), 1)
    cos1 = tab[:, :LANES] + jnp.where((lane1 & (HEAD_DIM - 1)) >= ROT_DIM, 1.0, 0.0)
    reps = aw // LANES
    cosf = jnp.concatenate([cos1] * reps, axis=1)
    sinf = jnp.concatenate([tab[:, LANES:]] * reps, axis=1)
    lane = lax.broadcasted_iota(I32, (tm, aw), 1)
    first_half = (lane & (HEAD_DIM - 1)) < (ROT_DIM // 2)

    def rope(t):
        partner = jnp.where(first_half, pltpu.roll(t, aw - ROT_DIM // 2, 1), pltpu.roll(t, ROT_DIM // 2, 1))
        return t * cosf + partner * sinf

    q = rope(proj[:, ssm_w:ssm_w + aw]) * (HEAD_DIM ** -0.5 * LOG2_E)
    k = rope(proj[:, ssm_w + aw:ssm_w + 2 * aw])
    q_ref[0] = q.astype(BF16)
    k_ref[0] = k.astype(BF16)
    v_ref[0] = proj[:, ssm_w + 2 * aw:ssm_w + 3 * aw].astype(BF16)


def _inproj(x, gain, mod3, w_in_bf, rope_tab, rope_exp, ssm_w, attn_w, tm=512):
    bsz, seq, d = x.shape
    n_u = ssm_w // LANES
    tm = min(tm, seq)
    return pl.pallas_call(
        _inproj_body,
        grid=(bsz, seq // tm),
        in_specs=[pl.BlockSpec((1, tm, d), lambda b, i: (b, i, 0)),
                  pl.BlockSpec((1, d), lambda b, i: (0, 0)),
                  pl.BlockSpec((1, 1, d), lambda b, i: (b, 0, 0)),
                  pl.BlockSpec((1, 1, d), lambda b, i: (b, 0, 1)),
                  pl.BlockSpec(w_in_bf.shape, lambda b, i: (0, 0)),
                  pl.BlockSpec((1, ROT_DIM, tm), lambda b, i: (b, 0, i)),
                  pl.BlockSpec(rope_exp.shape, lambda b, i: (0, 0))],
        out_specs=[pl.BlockSpec((n_u, 1, tm, LANES), lambda b, i: (0, b, i, 0)),
                   pl.BlockSpec((1, tm, attn_w), lambda b, i: (b, i, 0)),
                   pl.BlockSpec((1, tm, attn_w), lambda b, i: (b, i, 0)),
                   pl.BlockSpec((1, tm, attn_w), lambda b, i: (b, i, 0))],
        out_shape=[jax.ShapeDtypeStruct((n_u, bsz, seq, LANES), BF16),
                   jax.ShapeDtypeStruct((bsz, seq, attn_w), BF16),
                   jax.ShapeDtypeStruct((bsz, seq, attn_w), BF16),
                   jax.ShapeDtypeStruct((bsz, seq, attn_w), BF16)],
        compiler_params=_cparams("parallel", "parallel"),
        name="inproj",
    )(x, gain.reshape(1, d), mod3, mod3, w_in_bf, rope_tab, rope_exp)


def _rope_tables(positions):
    half = ROT_DIM // 2
    inv_freq = ROPE_THETA ** (-jnp.arange(0, ROT_DIM, 2, dtype=F32) / ROT_DIM)
    ang = positions.astype(F32)[:, None, :] * inv_freq[None, :, None]
    tab = jnp.concatenate([jnp.cos(ang), jnp.sin(ang)], axis=1)
    i = jnp.arange(ROT_DIM)[:, None]
    hl = jnp.arange(LANES)[None, :] % HEAD_DIM
    e_cos = jnp.where((i < half) & ((hl == i) | (hl == i + half)), 1.0, 0.0)
    e_sin = jnp.where(i >= half, jnp.where(hl == i - half, -1.0, jnp.where(hl == i, 1.0, 0.0)), 0.0)
    return tab, jnp.concatenate([e_cos, e_sin], axis=1).astype(F32)


def _s5_params(lam_re, lam_im, log_dt, b_re, b_im, c_re, c_im, d_skip, n_chunks):
    g_all, p = lam_re.shape
    hc = b_re.shape[-1]
    t = SSM_CHUNK
    gpb = LANES // hc
    nblk = g_all // gpb
    lr, li = lam_re.astype(F32), lam_im.astype(F32)
    dt = jnp.exp(log_dt.astype(F32))[:, None]
    ldr, ldi = lr * dt, li * dt

    def apow(k):
        k = jnp.asarray(k, F32)[..., None, None]
        mag = jnp.exp(ldr * k)
        return mag * jnp.cos(ldi * k), mag * jnp.sin(ldi * k)

    ar, ai = apow(1.0)
    zr, zi = ar - 1.0, ai
    den = lr * lr + li * li
    cr = (zr * lr + zi * li) / den
    ci = (zi * lr - zr * li) / den
    bre, bim = b_re.astype(F32), b_im.astype(F32)
    bbr = cr[..., None] * bre - ci[..., None] * bim
    bbi = cr[..., None] * bim + ci[..., None] * bre
    cre, cim = c_re.astype(F32), c_im.astype(F32)
    eye = jnp.eye(gpb, dtype=F32)

    pr, pi = apow(jnp.arange(t))
    abr = pr[..., None] * bbr - pi[..., None] * bbi
    abi = pr[..., None] * bbi + pi[..., None] * bbr
    kf = (jnp.einsum('gnp,kgph->kghn', cre, abr, precision=HIGHEST)
          - jnp.einsum('gnp,kgph->kghn', cim, abi, precision=HIGHEST))
    kblk = jnp.einsum('kjghn,gf->kjghfn', kf.reshape(t, nblk, gpb, hc, hc), eye)
    kblk = kblk.reshape(t, nblk, LANES, LANES)
    kpad = jnp.concatenate([jnp.zeros_like(kblk[:1]), kblk], axis=0)
    pm = jnp.concatenate([kpad[0:t], kpad[1:t + 1]], axis=-1)
    kstack = jnp.transpose(pm[::-1], (1, 0, 2, 3)).reshape(nblk, t * LANES, 2 * LANES)

    def inject(ab):
        ab = ab[::-1].reshape(t, nblk, gpb, p, hc)
        return jnp.einsum('sjgph,gf->jsghfp', ab, eye).reshape(nblk, t * LANES, gpb * p)

    bm = jnp.concatenate([inject(abr), inject(abi)], axis=-1)

    qr, qi = apow(jnp.arange(t) + 1.0)
    c_from_re = cre[None] * qr[:, :, None, :] - cim[None] * qi[:, :, None, :]
    c_from_im = -cre[None] * qi[:, :, None, :] - cim[None] * qr[:, :, None, :]

    def readout(cc):
        cc = cc.reshape(t, nblk, gpb, hc, p)
        return jnp.einsum('tjgnp,gf->jgptfn', cc, eye).reshape(nblk, gpb * p, t * LANES)

    cm = jnp.concatenate([readout(c_from_re), readout(c_from_im)], axis=1)

    n_steps = max(1, int(math.log2(n_chunks)))
    shifts = [float(t * (1 << s)) for s in range(n_steps)]
    sr, si = apow(jnp.asarray(shifts))
    ap = jnp.concatenate([sr.reshape(n_steps, nblk, gpb * p), si.reshape(n_steps, nblk, gpb * p)], axis=-1)
    ap = jnp.transpose(ap, (1, 0, 2))
    dvec = jnp.tile(d_skip.astype(F32).reshape(nblk, 1, LANES), (1, 1, t))
    return kstack.astype(BF16), bm.astype(BF16), cm.astype(BF16), ap, dvec


def _s5_body(x_ref, ks_ref, bm_ref, cm_ref, ap_ref, d_ref, y_ref, h_ref):
    nc = x_ref.shape[2]
    t = SSM_CHUNK
    pad = h_ref.shape[0] - nc
    half = h_ref.shape[1] // 2
    x = x_ref[0, 0]
    h_ref[0:pad] = jnp.zeros((pad, h_ref.shape[1]), F32)
    h_ref[pad:pad + nc] = _dot(x, bm_ref[0])
    for step in range(ap_ref.shape[1]):
        d = 1 << step
        cur = h_ref[pad:pad + nc]
        sft = h_ref[pad - d:pad + nc - d]
        ar = ap_ref[0, step:step + 1, 0:half]
        ai = ap_ref[0, step:step + 1, half:]
        cr, ci = cur[:, :half], cur[:, half:]
        sr, si = sft[:, :half], sft[:, half:]
        h_ref[pad:pad + nc, 0:half] = cr + ar * sr - ai * si
        h_ref[pad:pad + nc, half:] = ci + ar * si + ai * sr
    h_prev = h_ref[pad - 1:pad + nc - 1]
    ys = _dot(h_prev.astype(BF16), cm_ref[0])
    for t0 in range(0, t, 2):
        lo, hi = t0 * LANES, (t0 + 2) * LANES
        conv = _dot(x[:, :hi], ks_ref[0, (t - 2 - t0) * LANES:, :])
        y = conv + ys[:, lo:hi] + d_ref[0][:, lo:hi] * x[:, lo:hi].astype(F32)
        y_ref[0, 0, :, lo:hi] = y.astype(y_ref.dtype)


def _s5(u4, kstack, bm, cm, ap, dvec):
    nblk, bsz, nc, w = u4.shape
    sw = bm.shape[-1]
    pad = max(nc // 2, 8)
    return pl.pallas_call(
        _s5_body,
        grid=(nblk, bsz),
        in_specs=[pl.BlockSpec((1, 1, nc, w), lambda j, b: (j, b, 0, 0)),
                  pl.BlockSpec((1,) + kstack.shape[1:], lambda j, b: (j, 0, 0)),
                  pl.BlockSpec((1,) + bm.shape[1:], lambda j, b: (j, 0, 0)),
                  pl.BlockSpec((1,) + cm.shape[1:], lambda j, b: (j, 0, 0)),
                  pl.BlockSpec((1,) + ap.shape[1:], lambda j, b: (j, 0, 0)),
                  pl.BlockSpec((1, 1, w), lambda j, b: (j, 0, 0))],
        out_specs=pl.BlockSpec((1, 1, nc, w), lambda j, b: (j, b, 0, 0)),
        out_shape=jax.ShapeDtypeStruct((nblk, bsz, nc, w), BF16),
        scratch_shapes=[pltpu.VMEM((pad + nc, sw), F32)],
        compiler_params=_cparams("parallel", "parallel"),
        name="s5",
    )(u4, kstack, bm, cm, ap, dvec)


def _attn_body(q_ref, k_ref, v_ref, o_ref, vt_ref, km_ref, bias_ref, sa_ref, sb_ref):
    qi = pl.program_id(2)
    seq = k_ref.shape[1]
    tq = q_ref.shape[1]
    nb = seq // MOBA_BLOCK
    n_kv = seq // KV_TILE
    bpt = KV_TILE // MOBA_BLOCK
    blk_shift = MOBA_BLOCK.bit_length() - 1

    @pl.when(qi == 0)
    def _build():
        k = k_ref[0].astype(F32)
        v = v_ref[0].astype(F32)
        lane = lax.broadcasted_iota(I32, (seq, LANES), 1)
        va = jnp.where(lane < HEAD_DIM, v, jnp.where(lane == HEAD_DIM, 1.0, 0.0))
        vb = jnp.where(lane >= HEAD_DIM, v, jnp.where(lane == 0, 1.0, 0.0))
        for t in range(n_kv):
            rows = slice(t * KV_TILE, (t + 1) * KV_TILE)
            vt_ref[0, t] = va[rows].T.astype(BF16)
            vt_ref[1, t] = vb[rows].T.astype(BF16)
        km = jnp.mean(k.reshape(nb, MOBA_BLOCK, LANES), axis=1)
        lane_b = lax.broadcasted_iota(I32, (nb, LANES), 1)
        km_ref[0] = jnp.where(lane_b < HEAD_DIM, km, 0.0)
        km_ref[1] = jnp.where(lane_b >= HEAD_DIM, km, 0.0)

    q_t = q_ref[0].astype(F32).T
    feat = lax.broadcasted_iota(I32, (LANES, tq), 0)
    q_h = [jnp.where(feat < HEAD_DIM, q_t, 0.0).astype(BF16),
           jnp.where(feat >= HEAD_DIM, q_t, 0.0).astype(BF16)]

    jidx = lax.broadcasted_iota(I32, (nb, tq), 0)
    qblk = qi * (tq // MOBA_BLOCK) + lax.shift_right_logical(
        lax.broadcasted_iota(I32, (nb, tq), 1), blk_shift)
    for hx in range(2):
        g = _dot(km_ref[hx], q_t)
        cnt = jnp.zeros((nb, tq), F32)
        for jp in range(nb):
            row = g[jp:jp + 1, :]
            beats = jnp.where(row > g, 1.0, jnp.where(row == g, jnp.where(jp < jidx, 1.0, 0.0), 0.0))
            cnt = cnt + jnp.where(jp < qblk, beats, 0.0)
        past_sel = jnp.where(jidx < qblk, jnp.where(cnt < MOBA_TOPK, 1.0, 0.0), 0.0)
        sel = jnp.where(jidx == qblk, 1.0, past_sel)
        bias_ref[hx] = jnp.where(sel > 0.5, 0.0, NEG_INF)

    def raw_scores(dst_ref, tile):
        off = pl.multiple_of(tile * KV_TILE, KV_TILE)
        k_t = k_ref[0, pl.ds(off, KV_TILE), :]
        for hx in range(2):
            dst_ref[hx] = _dot(k_t, q_h[hx])

    def fold(carry, src_ref, tile, live=None, causal=None):
        out = []
        for hx in range(2):
            parts, biases = [], []
            for j in range(bpt):
                s_j = src_ref[hx, j * MOBA_BLOCK:(j + 1) * MOBA_BLOCK, :]
                if causal is not None:
                    s_j = jnp.where(causal[j], s_j, NEG_INF)
                b_j = bias_ref[hx, pl.ds(tile * bpt + j, 1), :]
                if live is not None:
                    b_j = jnp.where(live, b_j, NEG_INF)
                parts.append(s_j)
                biases.append(b_j)
            m_n = jnp.max(parts[0], axis=0, keepdims=True) + biases[0]
            for s_j, b_j in zip(parts[1:], biases[1:]):
                m_n = jnp.maximum(m_n, jnp.max(s_j, axis=0, keepdims=True) + b_j)
            if carry is None:
                p_t = jnp.concatenate([jnp.exp2(s_j + (b_j - m_n)) for s_j, b_j in zip(parts, biases)], axis=0)
                out += [m_n, _dot(vt_ref[hx, tile], p_t.astype(BF16))]
            else:
                m_c, acc_c = carry[2 * hx], carry[2 * hx + 1]
                m_n = jnp.maximum(m_c, m_n)
                p_t = jnp.concatenate([jnp.exp2(s_j + (b_j - m_n)) for s_j, b_j in zip(parts, biases)], axis=0)
                out += [m_n, acc_c * jnp.exp2(m_c - m_n) + _dot(vt_ref[hx, tile], p_t.astype(BF16))]
        return tuple(out)

    raw_scores(sa_ref, qi)
    raw_scores(sb_ref, 0)
    krow = lax.broadcasted_iota(I32, (MOBA_BLOCK, tq), 0)
    qcol = lax.broadcasted_iota(I32, (MOBA_BLOCK, tq), 1)
    state = fold(None, sa_ref, qi, causal=[krow + j * MOBA_BLOCK <= qcol for j in range(bpt)])

    def body(i, carry):
        t0 = 2 * i
        raw_scores(sa_ref, jnp.minimum(t0 + 1, n_kv - 1))
        carry = fold(carry, sb_ref, t0)
        raw_scores(sb_ref, jnp.minimum(t0 + 2, n_kv - 1))
        return fold(carry, sa_ref, t0 + 1, live=t0 + 1 < qi)

    _, acc_a, _, acc_b = lax.fori_loop(0, (qi + 1) // 2, body, state)
    o_t = jnp.where(feat < HEAD_DIM, acc_a / acc_a[HEAD_DIM:HEAD_DIM + 1, :], acc_b / acc_b[0:1, :])
    o_ref[0] = o_t.T.astype(o_ref.dtype)


def _attention(q, k, v):
    bsz, seq, aw = q.shape
    npair = aw // LANES
    tq = KV_TILE
    nb = seq // MOBA_BLOCK
    assert seq % KV_TILE == 0 and nb % 8 == 0
    return pl.pallas_call(
        _attn_body,
        grid=(bsz, npair, seq // tq),
        in_specs=[pl.BlockSpec((1, tq, LANES), lambda b, p, i: (b, i, p)),
                  pl.BlockSpec((1, seq, LANES), lambda b, p, i: (b, 0, p)),
                  pl.BlockSpec((1, seq, LANES), lambda b, p, i: (b, 0, p))],
        out_specs=pl.BlockSpec((1, tq, LANES), lambda b, p, i: (b, i, p)),
        out_shape=jax.ShapeDtypeStruct((bsz, seq, aw), BF16),
        scratch_shapes=[pltpu.VMEM((2, seq // KV_TILE, LANES, KV_TILE), BF16),
                        pltpu.VMEM((2, nb, LANES), F32),
                        pltpu.VMEM((2, nb, tq), F32),
                        pltpu.VMEM((2, KV_TILE, tq), F32),
                        pltpu.VMEM((2, KV_TILE, tq), F32)],
        compiler_params=_cparams("parallel", "parallel", "arbitrary"),
        name="attn",
    )(q, k, v)


def _merge_body(y_ref, a_ref, x_ref, wglu_ref, bglu_ref, gs_ref, ga_ref, wout_ref, gm_ref,
                gf_ref, shf_ref, scf_ref, rw_ref, rb_ref, x1_ref, h2_ref, idx_ref, wt_ref, cnt_ref):
    nblk = y_ref.shape[0]
    y = jnp.concatenate([y_ref[j, 0] for j in range(nblk)], axis=1).astype(F32)
    tm = y.shape[0]
    g = 0.5 * y * (1.0 + lax.erf(y * (2.0 ** -0.5)))
    glu = g * jax.nn.sigmoid(_dot(g.astype(BF16), wglu_ref[...]) + bglu_ref[...])
    ssm_n = _rms(glu, gs_ref[...])
    att_n = _rms(a_ref[0].astype(F32), ga_ref[...])
    merged = jnp.concatenate([ssm_n, att_n], axis=1).astype(BF16)
    mix = _dot(merged, wout_ref[...])
    x1 = x_ref[0] + gm_ref[0] * mix
    x1_ref[0] = x1
    h2 = _rms(x1, gf_ref[...]) * (1.0 + scf_ref[0]) + shf_ref[0]
    h2_ref[0] = h2
    logits = lax.dot_general(rw_ref[...], h2, (((1,), (1,)), ((), ())),
                             preferred_element_type=F32, precision=HIGHEST) + rb_ref[...]
    ne = logits.shape[0]
    sub = lax.broadcasted_iota(I32, (ne, tm), 0)
    vals, idxs = [], []
    for _ in range(TOP_K):
        mx = jnp.max(logits, axis=0, keepdims=True)
        ix = jnp.min(jnp.where(logits == mx, sub, ne), axis=0, keepdims=True)
        vals.append(mx)
        idxs.append(ix)
        logits = jnp.where(sub == ix, -jnp.inf, logits)
    exps = [jnp.exp(vv - vals[0]) for vv in vals]
    tot = exps[0]
    for e in exps[1:]:
        tot = tot + e
    row_i = lax.broadcasted_iota(I32, idx_ref.shape, 0)
    row_w = lax.broadcasted_iota(I32, wt_ref.shape, 0)
    idx_out = jnp.zeros(idx_ref.shape, I32)
    wt_out = jnp.zeros(wt_ref.shape, F32)
    for kk in range(TOP_K):
        idx_out = jnp.where(row_i == kk, idxs[kk], idx_out)
        wt_out = jnp.where(row_w == kk, exps[kk] / tot, wt_out)
    idx_ref[...] = idx_out
    wt_ref[...] = wt_out
    chosen = jnp.zeros((ne, tm), F32)
    for kk in range(TOP_K):
        chosen = chosen + jnp.where(sub == idxs[kk], 1.0, 0.0)
    cnt_ref[0] = jnp.sum(chosen, axis=1, keepdims=True).astype(I32)


def _merge(y4, attn, x, w_glu_bf, b_glu, g_ssm, g_attn, w_out_bf, mod3, g_ffn, router_w, router_b, tm=512):
    bsz, seq, d = x.shape
    nblk = y4.shape[0]
    ssm_w = nblk * LANES
    aw = attn.shape[-1]
    ne = router_w.shape[1]
    tm = min(tm, seq)
    spb = seq // tm
    n_tok = bsz * seq
    rw_t = router_w.T
    row = lambda n: pl.BlockSpec((1, n), lambda b, i: (0, 0))
    full = lambda a: pl.BlockSpec(a.shape, lambda b, i: (0,) * a.ndim)
    modv = lambda j: pl.BlockSpec((1, 1, d), lambda b, i: (b, 0, j))
    return pl.pallas_call(
        _merge_body,
        grid=(bsz, seq // tm),
        in_specs=[pl.BlockSpec((nblk, 1, tm, LANES), lambda b, i: (0, b, i, 0)),
                  pl.BlockSpec((1, tm, aw), lambda b, i: (b, i, 0)),
                  pl.BlockSpec((1, tm, d), lambda b, i: (b, i, 0)),
                  full(w_glu_bf), row(ssm_w), row(ssm_w), row(aw), full(w_out_bf),
                  modv(2), row(d), modv(3), modv(4), full(rw_t),
                  pl.BlockSpec((ne, 1), lambda b, i: (0, 0))],
        out_specs=[pl.BlockSpec((1, tm, d), lambda b, i: (b, i, 0)),
                   pl.BlockSpec((1, tm, d), lambda b, i: (b, i, 0)),
                   pl.BlockSpec((TOP_K, tm), lambda b, i: (0, b * spb + i)),
                   pl.BlockSpec((WT_ROWS, tm), lambda b, i: (0, b * spb + i)),
                   pl.BlockSpec((1, ne, 1), lambda b, i: (b * spb + i, 0, 0))],
        out_shape=[jax.ShapeDtypeStruct((bsz, seq, d), F32),
                   jax.ShapeDtypeStruct((bsz, seq, d), F32),
                   jax.ShapeDtypeStruct((TOP_K, n_tok), I32),
                   jax.ShapeDtypeStruct((WT_ROWS, n_tok), F32),
                   jax.ShapeDtypeStruct((bsz * spb, ne, 1), I32)],
        compiler_params=_cparams("parallel", "parallel"),
        name="merge",
    )(y4, attn, x, w_glu_bf, b_glu.reshape(1, ssm_w), g_ssm.reshape(1, ssm_w), g_attn.reshape(1, aw),
      w_out_bf, mod3, g_ffn.reshape(1, d), mod3, mod3, rw_t, router_b.reshape(ne, 1))


def _rank_body(idx_ref, offs_ref, pos_ref, carry_ref, *, ne):
    i = pl.program_id(0)
    tm = idx_ref.shape[1]

    @pl.when(i == 0)
    def _init():
        carry_ref[...] = offs_ref[...].astype(F32)

    idx = idx_ref[...]
    sub = lax.broadcasted_iota(I32, (ne, tm), 0)
    onehot = jnp.zeros((ne, tm), F32)
    for kk in range(TOP_K):
        onehot = onehot + jnp.where(idx[kk:kk + 1, :] == sub, 1.0, 0.0)
    r = lax.broadcasted_iota(I32, (tm, tm), 0)
    c = lax.broadcasted_iota(I32, (tm, tm), 1)
    tri = jnp.where(r < c, 1.0, 0.0).astype(BF16)
    slot = _dot(onehot.astype(BF16), tri) + carry_ref[...]
    row_k = lax.broadcasted_iota(I32, (TOP_K, tm), 0)
    pos = jnp.zeros((TOP_K, tm), F32)
    for kk in range(TOP_K):
        pk = jnp.sum(jnp.where(idx[kk:kk + 1, :] == sub, slot, 0.0), axis=0, keepdims=True)
        pos = jnp.where(row_k == kk, pk, pos)
    pos_ref[...] = pos.astype(I32)
    carry_ref[...] = carry_ref[...] + jnp.sum(onehot, axis=1, keepdims=True)


def _rank(idx_t, offs, tm=512):
    n = idx_t.shape[1]
    ne = offs.shape[0]
    tm = min(tm, n)
    return pl.pallas_call(
        functools.partial(_rank_body, ne=ne),
        grid=(n // tm,),
        in_specs=[pl.BlockSpec((TOP_K, tm), lambda i: (0, i)),
                  pl.BlockSpec((ne, 1), lambda i: (0, 0))],
        out_specs=pl.BlockSpec((TOP_K, tm), lambda i: (0, i)),
        out_shape=jax.ShapeDtypeStruct((TOP_K, n), I32),
        scratch_shapes=[pltpu.VMEM((ne, 1), F32)],
        compiler_params=_cparams("arbitrary"),
        name="rank",
    )(idx_t, offs.reshape(ne, 1))


def _dispatch_body(pos_ref, h_ref, xs_ref, sem):
    tm = h_ref.shape[0]

    def issue(g, _):
        for rr in range(ROW_UNROLL):
            r = g * ROW_UNROLL + rr
            for kk in range(TOP_K):
                dst = pos_ref[kk, r]
                pltpu.make_async_copy(h_ref.at[pl.ds(r, 1), :], xs_ref.at[pl.ds(dst, 1), :], sem).start()
        return 0

    lax.fori_loop(0, tm // ROW_UNROLL, issue, 0)
    done = xs_ref.at[pl.ds(0, tm * TOP_K), :]
    pltpu.make_async_copy(done, done, sem).wait()


def _dispatch(h2, pos_t, tm=512):
    n, d = h2.shape
    tm = min(tm, n)
    return pl.pallas_call(
        _dispatch_body,
        grid=(n // tm,),
        in_specs=[pl.BlockSpec((TOP_K, tm), lambda i: (0, i), memory_space=pltpu.SMEM),
                  pl.BlockSpec((tm, d), lambda i: (i, 0))],
        out_specs=pl.BlockSpec(memory_space=pl.ANY),
        out_shape=jax.ShapeDtypeStruct((n * TOP_K, d), h2.dtype),
        scratch_shapes=[pltpu.SemaphoreType.DMA(())],
        compiler_params=_cparams("arbitrary"),
        name="dispatch",
    )(pos_t, h2)


def _experts_body(tile_ref, exp_ref, lo_ref, hi_ref, first_ref, newe_ref,
                  xs_ref, wg_ref, bg_ref, wu_ref, bu_ref, wd_ref, bd_ref, ys_ref,
                  wg_bf, wu_bf, wd_bf):
    i = pl.program_id(0)
    lo = lo_ref[i]
    hi = hi_ref[i]

    @pl.when(newe_ref[i] == 1)
    def _cast():
        wg_bf[...] = wg_ref[0].astype(BF16)
        wu_bf[...] = wu_ref[0].astype(BF16)
        wd_bf[...] = wd_ref[0].astype(BF16)

    @pl.when(hi > lo)
    def _compute():
        x = xs_ref[...].astype(BF16)
        gate = jnp.minimum(_dot(x, wg_bf[...]) + bg_ref[0], SWIGLU_LIMIT)
        lin = jnp.clip(_dot(x, wu_bf[...]) + bu_ref[0], -SWIGLU_LIMIT, SWIGLU_LIMIT)
        act = gate * jax.nn.sigmoid(SWIGLU_ALPHA * gate) * (lin + 1.0)
        y = _dot(act.astype(BF16), wd_bf[...]) + bd_ref[0]
        row = lax.broadcasted_iota(I32, y.shape, 0)
        mine = (row >= lo) & (row < hi)

        @pl.when(first_ref[i] == 1)
        def _():
            ys_ref[...] = jnp.where(mine, y, 0.0)

        @pl.when(first_ref[i] == 0)
        def _():
            ys_ref[...] = jnp.where(mine, y, ys_ref[...])


def _experts(xs, items, w_gate, b_gate, w_up, b_up, w_down, b_down, tg):
    nk, d = xs.shape
    ne, _, dff = w_gate.shape
    n_items = items[0].shape[0]
    wspec = lambda shp: pl.BlockSpec((1,) + shp, lambda i, t, e, lo, hi, f, nw: (e[i], 0, 0))
    return pl.pallas_call(
        _experts_body,
        grid_spec=pltpu.PrefetchScalarGridSpec(
            num_scalar_prefetch=6,
            grid=(n_items,),
            in_specs=[pl.BlockSpec((tg, d), lambda i, t, e, lo, hi, f, nw: (t[i], 0)),
                      wspec((d, dff)), wspec((1, dff)),
                      wspec((d, dff)), wspec((1, dff)),
                      wspec((dff, d)), wspec((1, d))],
            out_specs=pl.BlockSpec((tg, d), lambda i, t, e, lo, hi, f, nw: (t[i], 0)),
            scratch_shapes=[pltpu.VMEM((d, dff), BF16), pltpu.VMEM((d, dff), BF16), pltpu.VMEM((dff, d), BF16)],
        ),
        out_shape=jax.ShapeDtypeStruct((nk, d), F32),
        compiler_params=_cparams("arbitrary"),
        name="experts",
    )(*items, xs, w_gate, b_gate.reshape(ne, 1, dff), w_up, b_up.reshape(ne, 1, dff),
      w_down, b_down.reshape(ne, 1, d))


def _plan_items(counts, nk, tg):
    ne = counts.shape[0]
    n_tiles = nk // tg
    n_items = n_tiles + ne
    ends = jnp.cumsum(counts)
    offs = ends - counts
    first_tile = offs // tg
    last_tile = jnp.maximum(ends - 1, 0) // tg
    per_e = jnp.where(counts > 0, last_tile - first_tile + 1, 0)
    item_end = jnp.cumsum(per_e)
    item_start = item_end - per_e
    total = item_end[-1]
    ii = jnp.arange(n_items, dtype=I32)
    first_after = lambda i: jnp.minimum(jnp.sum((item_end[None, :] <= i[:, None]).astype(I32), axis=1), ne - 1)
    e_of = first_after(ii)
    valid = ii < total
    last_e = first_after(total[None] - 1)[0]
    e_of = jnp.where(valid, e_of, last_e)
    tile = jnp.where(valid, first_tile[e_of] + (ii - item_start[e_of]), n_tiles - 1).astype(I32)
    lo = jnp.clip(offs[e_of] - tile * tg, 0, tg)
    hi = jnp.clip(ends[e_of] - tile * tg, 0, tg)
    lo = jnp.where(valid, lo, 0).astype(I32)
    hi = jnp.where(valid, hi, 0).astype(I32)
    prev_tile = jnp.concatenate([jnp.full((1,), -1, I32), tile[:-1]])
    prev_e = jnp.concatenate([jnp.full((1,), -1, I32), e_of[:-1]])
    first = (tile != prev_tile).astype(I32)
    newe = (e_of != prev_e).astype(I32)
    return (tile, e_of, lo, hi, first, newe), offs


def _combine_body(pos_ref, x1_ref, wt_ref, gf_ref, gfin_ref, sho_ref, sco_ref, ys_ref, o_ref, buf, sem):
    tm = x1_ref.shape[1]
    n_half = buf.shape[0]
    th = tm // n_half

    def issue_half(hf):
        def issue(g, _):
            for rr in range(ROW_UNROLL):
                r = g * ROW_UNROLL + rr
                for kk in range(TOP_K):
                    src = pos_ref[kk, hf * th + r]
                    pltpu.make_async_copy(ys_ref.at[pl.ds(src, 1), :], buf.at[hf, kk, pl.ds(r, 1), :],
                                          sem.at[hf]).start()
            return 0
        lax.fori_loop(0, th // ROW_UNROLL, issue, 0)

    for hf in range(n_half):
        issue_half(hf)
    wt_all = wt_ref[...].T
    for hf in range(n_half):
        pltpu.make_async_copy(buf.at[hf], buf.at[hf], sem.at[hf]).wait()
        rows = pl.ds(hf * th, th)
        wt = wt_all[hf * th:(hf + 1) * th, :]
        ffn = wt[:, 0:1] * buf[hf, 0]
        for kk in range(1, TOP_K):
            ffn = ffn + wt[:, kk:kk + 1] * buf[hf, kk]
        x2 = x1_ref[0, rows, :] + gf_ref[0] * ffn
        o_ref[0, rows, :] = _rms(x2, gfin_ref[...]) * (1.0 + sco_ref[0]) + sho_ref[0]


def _combine(pos_t, x1, wt_t, mod3, g_final, fmod3, ys, tm=512):
    bsz, seq, d = x1.shape
    tm = min(tm, seq)
    spb = seq // tm
    return pl.pallas_call(
        _combine_body,
        grid=(bsz, spb),
        in_specs=[pl.BlockSpec((TOP_K, tm), lambda b, i: (0, b * spb + i), memory_space=pltpu.SMEM),
                  pl.BlockSpec((1, tm, d), lambda b, i: (b, i, 0)),
                  pl.BlockSpec((WT_ROWS, tm), lambda b, i: (0, b * spb + i)),
                  pl.BlockSpec((1, 1, d), lambda b, i: (b, 0, 5)),
                  pl.BlockSpec((1, d), lambda b, i: (0, 0)),
                  pl.BlockSpec((1, 1, d), lambda b, i: (b, 0, 0)),
                  pl.BlockSpec((1, 1, d), lambda b, i: (b, 0, 1)),
                  pl.BlockSpec(memory_space=pl.ANY)],
        out_specs=pl.BlockSpec((1, tm, d), lambda b, i: (b, i, 0)),
        out_shape=jax.ShapeDtypeStruct((bsz, seq, d), F32),
        scratch_shapes=[pltpu.VMEM((2, TOP_K, tm // 2, d), F32), pltpu.SemaphoreType.DMA((2,))],
        compiler_params=_cparams("arbitrary", "arbitrary"),
        name="combine",
    )(pos_t, x1, wt_t, mod3, g_final.reshape(1, d), fmod3, fmod3, ys)


def kernel(x, c, positions, ada_w, ada_b, final_ada_w, final_ada_b, norm_mix_g, norm_ffn_g, norm_final_g, w_in, ssm_lambda_re, ssm_lambda_im, ssm_log_dt, ssm_b_re, ssm_b_im, ssm_c_re, ssm_c_im, ssm_d, ssm_w_glu, ssm_b_glu, out_norm_ssm_g, out_norm_attn_g, w_out, router_w, router_b, exp_w_gate, exp_b_gate, exp_w_up, exp_b_up, exp_w_down, exp_b_down):
    bsz, seq, d = x.shape
    depth = ada_w.shape[0]
    ssm_w = ssm_d.shape[-1]
    attn_w = (w_in.shape[-1] - ssm_w) // 3
    ne = router_w.shape[-1]
    n_tok = bsz * seq
    nc = seq // SSM_CHUNK
    tg = 256

    rope_tab, rope_exp = _rope_tables(positions)
    fmod3 = _adaln(c, final_ada_w, final_ada_b).reshape(bsz, 1, 2 * d)
    for l in range(depth):
        mod3 = _adaln(c, ada_w[l], ada_b[l]).reshape(bsz, 1, -1)
        u4, q, k, v = _inproj(x, norm_mix_g[l], mod3, w_in[l].astype(BF16), rope_tab, rope_exp, ssm_w, attn_w)
        tables = _s5_params(ssm_lambda_re[l], ssm_lambda_im[l], ssm_log_dt[l], ssm_b_re[l], ssm_b_im[l],
                            ssm_c_re[l], ssm_c_im[l], ssm_d[l], nc)
        nblk = ssm_w // LANES
        y4 = _s5(u4.reshape(nblk, bsz, nc, SSM_CHUNK * LANES), *tables).reshape(nblk, bsz, seq, LANES)
        attn = _attention(q, k, v)
        x1, h2, idx_t, wt_t, tile_cnt = _merge(y4, attn, x, ssm_w_glu[l].astype(BF16), ssm_b_glu[l], out_norm_ssm_g[l],
                                     out_norm_attn_g[l], w_out[l].astype(BF16), mod3, norm_ffn_g[l],
                                     router_w[l], router_b[l])
        items, offs = _plan_items(jnp.sum(tile_cnt[:, :, 0], axis=0), n_tok * TOP_K, tg)
        pos_t = _rank(idx_t, offs)
        xs = _dispatch(h2.reshape(n_tok, d), pos_t)
        ys = _experts(xs, items, exp_w_gate[l], exp_b_gate[l], exp_w_up[l], exp_b_up[l],
                      exp_w_down[l], exp_b_down[l], tg)
        if l + 1 < depth:
            raise NotImplementedError("depth > 1 needs the non-final combine")
        x = _combine(pos_t, x1, wt_t, mod3, norm_final_g, fmod3, ys)
    return x
```

```python
import functools
import math

import jax
import jax.numpy as jnp
from jax import lax
from jax.experimental import pallas as pl
from jax.experimental.pallas import tpu as pltpu

F32 = jnp.float32
BF16 = jnp.bfloat16
I32 = jnp.int32
HIGHEST = lax.Precision.HIGHEST

LANES = 128
HEAD_DIM = 64
MOBA_BLOCK = 256
MOBA_TOPK = 3
ROT_DIM = HEAD_DIM // 4
ROPE_THETA = 500000.0
SSM_GROUP = 16
SSM_CHUNK = 16
TOP_K = 4
SWIGLU_ALPHA = 1.702
SWIGLU_LIMIT = 7.0
NORM_EPS = 1e-5
NEG_INF = -1e30
LOG2_E = math.log2(math.e)
KV_TILE = 2 * MOBA_BLOCK
WT_ROWS = 8
SEG_ALIGN = 8
SEG_UNITS = (64, 32, 16, 8, 4, 2, 1)
VMEM_LIMIT = 56 * 1024 * 1024


def _cparams(*sem):
    return pltpu.CompilerParams(dimension_semantics=sem, vmem_limit_bytes=VMEM_LIMIT)


def _dot(a, b):
    return jnp.dot(a, b, preferred_element_type=F32)


def _rms(x, g):
    return x * lax.rsqrt(jnp.mean(x * x, axis=-1, keepdims=True) + NORM_EPS) * g


def _adaln_body(c_ref, w_ref, b_ref, o_ref):
    c = c_ref[...]
    ca = c * jax.nn.sigmoid(c)
    o_ref[...] = jnp.dot(ca, w_ref[...], preferred_element_type=F32, precision=HIGHEST) + b_ref[...]


def _adaln(c, w, b, tn=512):
    bsz, d = c.shape
    n = w.shape[1]
    return pl.pallas_call(
        _adaln_body,
        grid=(n // tn,),
        in_specs=[pl.BlockSpec((bsz, d), lambda j: (0, 0)),
                  pl.BlockSpec((d, tn), lambda j: (0, j)),
                  pl.BlockSpec((1, tn), lambda j: (0, j))],
        out_specs=pl.BlockSpec((bsz, tn), lambda j: (0, j)),
        out_shape=jax.ShapeDtypeStruct((bsz, n), F32),
        compiler_params=_cparams("parallel"),
        name="adaln",
    )(c, w, b.reshape(1, n))


def _inproj_body(x_ref, g_ref, sh_ref, sc_ref, w_ref, tab_ref, exp_ref, u_ref, q_ref, k_ref, v_ref):
    x = x_ref[0]
    tm = x.shape[0]
    h = _rms(x, g_ref[...]) * (1.0 + sc_ref[0]) + sh_ref[0]
    proj = _dot(h.astype(BF16), w_ref[...])
    ssm_w = u_ref.shape[0] * LANES
    aw = q_ref.shape[2]
    for j in range(u_ref.shape[0]):
        u_ref[j, 0] = proj[:, LANES * j:LANES * (j + 1)].astype(BF16)
    tab = lax.dot_general(tab_ref[0], exp_ref[...], (((0,), (0,)), ((), ())),
                          preferred_element_type=F32, precision=HIGHEST)
    lane1 = lax.broadcasted_iota(I32, (tm, LANES), 1)
    cos1 = tab[:, :LANES] + jnp.where((lane1 & (HEAD_DIM - 1)) >= ROT_DIM, 1.0, 0.0)
    reps = aw // LANES
    cosf = jnp.concatenate([cos1] * reps, axis=1)
    sinf = jnp.concatenate([tab[:, LANES:]] * reps, axis=1)
    lane = lax.broadcasted_iota(I32, (tm, aw), 1)
    first_half = (lane & (HEAD_DIM - 1)) < (ROT_DIM // 2)

    def rope(t):
        partner = jnp.where(first_half, pltpu.roll(t, aw - ROT_DIM // 2, 1), pltpu.roll(t, ROT_DIM // 2, 1))
        return t * cosf + partner * sinf

    q = rope(proj[:, ssm_w:ssm_w + aw]) * (HEAD_DIM ** -0.5 * LOG2_E)
    k = rope(proj[:, ssm_w + aw:ssm_w + 2 * aw])
    q_ref[0] = q.astype(BF16)
    k_ref[0] = k.astype(BF16)
    v_ref[0] = proj[:, ssm_w + 2 * aw:ssm_w + 3 * aw].astype(BF16)


def _inproj(x, gain, mod3, w_in_bf, rope_tab, rope_exp, ssm_w, attn_w, tm=512):
    bsz, seq, d = x.shape
    n_u = ssm_w // LANES
    tm = min(tm, seq)
    return pl.pallas_call(
        _inproj_body,
        grid=(bsz, seq // tm),
        in_specs=[pl.BlockSpec((1, tm, d), lambda b, i: (b, i, 0)),
                  pl.BlockSpec((1, d), lambda b, i: (0, 0)),
                  pl.BlockSpec((1, 1, d), lambda b, i: (b, 0, 0)),
                  pl.BlockSpec((1, 1, d), lambda b, i: (b, 0, 1)),
                  pl.BlockSpec(w_in_bf.shape, lambda b, i: (0, 0)),
                  pl.BlockSpec((1, ROT_DIM, tm), lambda b, i: (b, 0, i)),
                  pl.BlockSpec(rope_exp.shape, lambda b, i: (0, 0))],
        out_specs=[pl.BlockSpec((n_u, 1, tm, LANES), lambda b, i: (0, b, i, 0)),
                   pl.BlockSpec((1, tm, attn_w), lambda b, i: (b, i, 0)),
                   pl.BlockSpec((1, tm, attn_w), lambda b, i: (b, i, 0)),
                   pl.BlockSpec((1, tm, attn_w), lambda b, i: (b, i, 0))],
        out_shape=[jax.ShapeDtypeStruct((n_u, bsz, seq, LANES), BF16),
                   jax.ShapeDtypeStruct((bsz, seq, attn_w), BF16),
                   jax.ShapeDtypeStruct((bsz, seq, attn_w), BF16),
                   jax.ShapeDtypeStruct((bsz, seq, attn_w), BF16)],
        compiler_params=_cparams("parallel", "parallel"),
        name="inproj",
    )(x, gain.reshape(1, d), mod3, mod3, w_in_bf, rope_tab, rope_exp)


def _rope_tables(positions):
    half = ROT_DIM // 2
    inv_freq = ROPE_THETA ** (-jnp.arange(0, ROT_DIM, 2, dtype=F32) / ROT_DIM)
    ang = positions.astype(F32)[:, None, :] * inv_freq[None, :, None]
    tab = jnp.concatenate([jnp.cos(ang), jnp.sin(ang)], axis=1)
    i = jnp.arange(ROT_DIM)[:, None]
    hl = jnp.arange(LANES)[None, :] % HEAD_DIM
    e_cos = jnp.where((i < half) & ((hl == i) | (hl == i + half)), 1.0, 0.0)
    e_sin = jnp.where(i >= half, jnp.where(hl == i - half, -1.0, jnp.where(hl == i, 1.0, 0.0)), 0.0)
    return tab, jnp.concatenate([e_cos, e_sin], axis=1).astype(F32)


def _s5_params(lam_re, lam_im, log_dt, b_re, b_im, c_re, c_im, d_skip, n_chunks):
    g_all, p = lam_re.shape
    hc = b_re.shape[-1]
    t = SSM_CHUNK
    gpb = LANES // hc
    nblk = g_all // gpb
    lr, li = lam_re.astype(F32), lam_im.astype(F32)
    dt = jnp.exp(log_dt.astype(F32))[:, None]
    ldr, ldi = lr * dt, li * dt

    def apow(k):
        k = jnp.asarray(k, F32)[..., None, None]
        mag = jnp.exp(ldr * k)
        return mag * jnp.cos(ldi * k), mag * jnp.sin(ldi * k)

    ar, ai = apow(1.0)
    zr, zi = ar - 1.0, ai
    den = lr * lr + li * li
    cr = (zr * lr + zi * li) / den
    ci = (zi * lr - zr * li) / den
    bre, bim = b_re.astype(F32), b_im.astype(F32)
    bbr = cr[..., None] * bre - ci[..., None] * bim
    bbi = cr[..., None] * bim + ci[..., None] * bre
    cre, cim = c_re.astype(F32), c_im.astype(F32)
    eye = jnp.eye(gpb, dtype=F32)

    pr, pi = apow(jnp.arange(t))
    abr = pr[..., None] * bbr - pi[..., None] * bbi
    abi = pr[..., None] * bbi + pi[..., None] * bbr
    kf = (jnp.einsum('gnp,kgph->kghn', cre, abr, precision=HIGHEST)
          - jnp.einsum('gnp,kgph->kghn', cim, abi, precision=HIGHEST))
    kblk = jnp.einsum('kjghn,gf->kjghfn', kf.reshape(t, nblk, gpb, hc, hc), eye)
    kblk = kblk.reshape(t, nblk, LANES, LANES)
    kpad = jnp.concatenate([jnp.zeros_like(kblk[:1]), kblk], axis=0)
    pm = jnp.concatenate([kpad[0:t], kpad[1:t + 1]], axis=-1)
    kstack = jnp.transpose(pm[::-1], (1, 0, 2, 3)).reshape(nblk, t * LANES, 2 * LANES)

    def inject(ab):
        ab = ab[::-1].reshape(t, nblk, gpb, p, hc)
        return jnp.einsum('sjgph,gf->jsghfp', ab, eye).reshape(nblk, t * LANES, gpb * p)

    bm = jnp.concatenate([inject(abr), inject(abi)], axis=-1)

    qr, qi = apow(jnp.arange(t) + 1.0)
    c_from_re = cre[None] * qr[:, :, None, :] - cim[None] * qi[:, :, None, :]
    c_from_im = -cre[None] * qi[:, :, None, :] - cim[None] * qr[:, :, None, :]

    def readout(cc):
        cc = cc.reshape(t, nblk, gpb, hc, p)
        return jnp.einsum('tjgnp,gf->jgptfn', cc, eye).reshape(nblk, gpb * p, t * LANES)

    cm = jnp.concatenate([readout(c_from_re), readout(c_from_im)], axis=1)

    n_steps = max(1, int(math.log2(n_chunks)))
    shifts = [float(t * (1 << s)) for s in range(n_steps)]
    sr, si = apow(jnp.asarray(shifts))
    ap = jnp.concatenate([sr.reshape(n_steps, nblk, gpb * p), si.reshape(n_steps, nblk, gpb * p)], axis=-1)
    ap = jnp.transpose(ap, (1, 0, 2))
    dvec = jnp.tile(d_skip.astype(F32).reshape(nblk, 1, LANES), (1, 1, t))
    return kstack.astype(BF16), bm.astype(BF16), cm.astype(BF16), ap, dvec


def _s5_body(x_ref, ks_ref, bm_ref, cm_ref, ap_ref, d_ref, y_ref, h_ref):
    nc = x_ref.shape[2]
    t = SSM_CHUNK
    pad = h_ref.shape[0] - nc
    half = h_ref.shape[1] // 2
    x = x_ref[0, 0]
    h_ref[0:pad] = jnp.zeros((pad, h_ref.shape[1]), F32)
    h_ref[pad:pad + nc] = _dot(x, bm_ref[0])
    for step in range(ap_ref.shape[1]):
        d = 1 << step
        cur = h_ref[pad:pad + nc]
        sft = h_ref[pad - d:pad + nc - d]
        ar = ap_ref[0, step:step + 1, 0:half]
        ai = ap_ref[0, step:step + 1, half:]
        cr, ci = cur[:, :half], cur[:, half:]
        sr, si = sft[:, :half], sft[:, half:]
        h_ref[pad:pad + nc, 0:half] = cr + ar * sr - ai * si
        h_ref[pad:pad + nc, half:] = ci + ar * si + ai * sr
    h_prev = h_ref[pad - 1:pad + nc - 1]
    ys = _dot(h_prev.astype(BF16), cm_ref[0])
    for t0 in range(0, t, 2):
        lo, hi = t0 * LANES, (t0 + 2) * LANES
        conv = _dot(x[:, :hi], ks_ref[0, (t - 2 - t0) * LANES:, :])
        y = conv + ys[:, lo:hi] + d_ref[0][:, lo:hi] * x[:, lo:hi].astype(F32)
        y_ref[0, 0, :, lo:hi] = y.astype(y_ref.dtype)


def _s5(u4, kstack, bm, cm, ap, dvec):
    nblk, bsz, nc, w = u4.shape
    sw = bm.shape[-1]
    pad = max(nc // 2, 8)
    return pl.pallas_call(
        _s5_body,
        grid=(nblk, bsz),
        in_specs=[pl.BlockSpec((1, 1, nc, w), lambda j, b: (j, b, 0, 0)),
                  pl.BlockSpec((1,) + kstack.shape[1:], lambda j, b: (j, 0, 0)),
                  pl.BlockSpec((1,) + bm.shape[1:], lambda j, b: (j, 0, 0)),
                  pl.BlockSpec((1,) + cm.shape[1:], lambda j, b: (j, 0, 0)),
                  pl.BlockSpec((1,) + ap.shape[1:], lambda j, b: (j, 0, 0)),
                  pl.BlockSpec((1, 1, w), lambda j, b: (j, 0, 0))],
        out_specs=pl.BlockSpec((1, 1, nc, w), lambda j, b: (j, b, 0, 0)),
        out_shape=jax.ShapeDtypeStruct((nblk, bsz, nc, w), BF16),
        scratch_shapes=[pltpu.VMEM((pad + nc, sw), F32)],
        compiler_params=_cparams("parallel", "parallel"),
        name="s5",
    )(u4, kstack, bm, cm, ap, dvec)


def _attn_body(q_ref, k_ref, v_ref, o_ref, vt_ref, km_ref, bias_ref, sa_ref, sb_ref):
    qi = pl.program_id(2)
    seq = k_ref.shape[1]
    tq = q_ref.shape[1]
    nb = seq // MOBA_BLOCK
    n_kv = seq // KV_TILE
    bpt = KV_TILE // MOBA_BLOCK
    blk_shift = MOBA_BLOCK.bit_length() - 1

    @pl.when(qi == 0)
    def _build():
        k = k_ref[0].astype(F32)
        v = v_ref[0].astype(F32)
        lane = lax.broadcasted_iota(I32, (seq, LANES), 1)
        va = jnp.where(lane < HEAD_DIM, v, jnp.where(lane == HEAD_DIM, 1.0, 0.0))
        vb = jnp.where(lane >= HEAD_DIM, v, jnp.where(lane == 0, 1.0, 0.0))
        for t in range(n_kv):
            rows = slice(t * KV_TILE, (t + 1) * KV_TILE)
            vt_ref[0, t] = va[rows].T.astype(BF16)
            vt_ref[1, t] = vb[rows].T.astype(BF16)
        km = jnp.mean(k.reshape(nb, MOBA_BLOCK, LANES), axis=1)
        lane_b = lax.broadcasted_iota(I32, (nb, LANES), 1)
        km_ref[0] = jnp.where(lane_b < HEAD_DIM, km, 0.0)
        km_ref[1] = jnp.where(lane_b >= HEAD_DIM, km, 0.0)

    q_t = q_ref[0].astype(F32).T
    feat = lax.broadcasted_iota(I32, (LANES, tq), 0)
    q_h = [jnp.where(feat < HEAD_DIM, q_t, 0.0).astype(BF16),
           jnp.where(feat >= HEAD_DIM, q_t, 0.0).astype(BF16)]

    jidx = lax.broadcasted_iota(I32, (nb, tq), 0)
    qblk = qi * (tq // MOBA_BLOCK) + lax.shift_right_logical(
        lax.broadcasted_iota(I32, (nb, tq), 1), blk_shift)
    for hx in range(2):
        g = _dot(km_ref[hx], q_t)
        cnt = jnp.zeros((nb, tq), F32)
        for jp in range(nb):
            row = g[jp:jp + 1, :]
            beats = jnp.where(row > g, 1.0, jnp.where(row == g, jnp.where(jp < jidx, 1.0, 0.0), 0.0))
            cnt = cnt + jnp.where(jp < qblk, beats, 0.0)
        past_sel = jnp.where(jidx < qblk, jnp.where(cnt < MOBA_TOPK, 1.0, 0.0), 0.0)
        sel = jnp.where(jidx == qblk, 1.0, past_sel)
        bias_ref[hx] = jnp.where(sel > 0.5, 0.0, NEG_INF)

    def raw_scores(dst_ref, tile):
        off = pl.multiple_of(tile * KV_TILE, KV_TILE)
        k_t = k_ref[0, pl.ds(off, KV_TILE), :]
        for hx in range(2):
            dst_ref[hx] = _dot(k_t, q_h[hx])

    def fold(carry, src_ref, tile, live=None, causal=None):
        out = []
        for hx in range(2):
            parts, biases = [], []
            for j in range(bpt):
                s_j = src_ref[hx, j * MOBA_BLOCK:(j + 1) * MOBA_BLOCK, :]
                if causal is not None:
                    s_j = jnp.where(causal[j], s_j, NEG_INF)
                b_j = bias_ref[hx, pl.ds(tile * bpt + j, 1), :]
                if live is not None:
                    b_j = jnp.where(live, b_j, NEG_INF)
                parts.append(s_j)
                biases.append(b_j)
            m_n = jnp.max(parts[0], axis=0, keepdims=True) + biases[0]
            for s_j, b_j in zip(parts[1:], biases[1:]):
                m_n = jnp.maximum(m_n, jnp.max(s_j, axis=0, keepdims=True) + b_j)
            if carry is None:
                p_t = jnp.concatenate([jnp.exp2(s_j + (b_j - m_n)) for s_j, b_j in zip(parts, biases)], axis=0)
                out += [m_n, _dot(vt_ref[hx, tile], p_t.astype(BF16))]
            else:
                m_c, acc_c = carry[2 * hx], carry[2 * hx + 1]
                m_n = jnp.maximum(m_c, m_n)
                p_t = jnp.concatenate([jnp.exp2(s_j + (b_j - m_n)) for s_j, b_j in zip(parts, biases)], axis=0)
                out += [m_n, acc_c * jnp.exp2(m_c - m_n) + _dot(vt_ref[hx, tile], p_t.astype(BF16))]
        return tuple(out)

    raw_scores(sa_ref, qi)
    raw_scores(sb_ref, 0)
    krow = lax.broadcasted_iota(I32, (MOBA_BLOCK, tq), 0)
    qcol = lax.broadcasted_iota(I32, (MOBA_BLOCK, tq), 1)
    state = fold(None, sa_ref, qi, causal=[krow + j * MOBA_BLOCK <= qcol for j in range(bpt)])

    def body(i, carry):
        t0 = 2 * i
        raw_scores(sa_ref, jnp.minimum(t0 + 1, n_kv - 1))
        carry = fold(carry, sb_ref, t0)
        raw_scores(sb_ref, jnp.minimum(t0 + 2, n_kv - 1))
        return fold(carry, sa_ref, t0 + 1, live=t0 + 1 < qi)

    _, acc_a, _, acc_b = lax.fori_loop(0, (qi + 1) // 2, body, state)
    o_t = jnp.where(feat < HEAD_DIM, acc_a / acc_a[HEAD_DIM:HEAD_DIM + 1, :], acc_b / acc_b[0:1, :])
    o_ref[0] = o_t.T.astype(o_ref.dtype)


def _attention(q, k, v):
    bsz, seq, aw = q.shape
    npair = aw // LANES
    tq = KV_TILE
    nb = seq // MOBA_BLOCK
    assert seq % KV_TILE == 0 and nb % 8 == 0
    return pl.pallas_call(
        _attn_body,
        grid=(bsz, npair, seq // tq),
        in_specs=[pl.BlockSpec((1, tq, LANES), lambda b, p, i: (b, i, p)),
                  pl.BlockSpec((1, seq, LANES), lambda b, p, i: (b, 0, p)),
                  pl.BlockSpec((1, seq, LANES), lambda b, p, i: (b, 0, p))],
        out_specs=pl.BlockSpec((1, tq, LANES), lambda b, p, i: (b, i, p)),
        out_shape=jax.ShapeDtypeStruct((bsz, seq, aw), BF16),
        scratch_shapes=[pltpu.VMEM((2, seq // KV_TILE, LANES, KV_TILE), BF16),
                        pltpu.VMEM((2, nb, LANES), F32),
                        pltpu.VMEM((2, nb, tq), F32),
                        pltpu.VMEM((2, KV_TILE, tq), F32),
                        pltpu.VMEM((2, KV_TILE, tq), F32)],
        compiler_params=_cparams("parallel", "parallel", "arbitrary"),
        name="attn",
    )(q, k, v)


def _merge_body(y_ref, a_ref, x_ref, wglu_ref, bglu_ref, gs_ref, ga_ref, wout_ref, gm_ref,
                gf_ref, shf_ref, scf_ref, rw_ref, rb_ref, x1_ref, h2_ref, idx_ref, wt_ref, cnt_ref):
    nblk = y_ref.shape[0]
    y = jnp.concatenate([y_ref[j, 0] for j in range(nblk)], axis=1).astype(F32)
    tm = y.shape[0]
    g = 0.5 * y * (1.0 + lax.erf(y * (2.0 ** -0.5)))
    glu = g * jax.nn.sigmoid(_dot(g.astype(BF16), wglu_ref[...]) + bglu_ref[...])
    ssm_n = _rms(glu, gs_ref[...])
    att_n = _rms(a_ref[0].astype(F32), ga_ref[...])
    merged = jnp.concatenate([ssm_n, att_n], axis=1).astype(BF16)
    mix = _dot(merged, wout_ref[...])
    x1 = x_ref[0] + gm_ref[0] * mix
    x1_ref[0] = x1
    h2 = _rms(x1, gf_ref[...]) * (1.0 + scf_ref[0]) + shf_ref[0]
    h2_ref[0] = h2
    logits = lax.dot_general(rw_ref[...], h2, (((1,), (1,)), ((), ())),
                             preferred_element_type=F32, precision=HIGHEST) + rb_ref[...]
    ne = logits.shape[0]
    sub = lax.broadcasted_iota(I32, (ne, tm), 0)
    vals, idxs = [], []
    for _ in range(TOP_K):
        mx = jnp.max(logits, axis=0, keepdims=True)
        ix = jnp.min(jnp.where(logits == mx, sub, ne), axis=0, keepdims=True)
        vals.append(mx)
        idxs.append(ix)
        logits = jnp.where(sub == ix, -jnp.inf, logits)
    exps = [jnp.exp(vv - vals[0]) for vv in vals]
    tot = exps[0]
    for e in exps[1:]:
        tot = tot + e
    row_i = lax.broadcasted_iota(I32, idx_ref.shape, 0)
    row_w = lax.broadcasted_iota(I32, wt_ref.shape, 0)
    idx_out = jnp.zeros(idx_ref.shape, I32)
    wt_out = jnp.zeros(wt_ref.shape, F32)
    for kk in range(TOP_K):
        idx_out = jnp.where(row_i == kk, idxs[kk], idx_out)
        wt_out = jnp.where(row_w == kk, exps[kk] / tot, wt_out)
    idx_ref[...] = idx_out
    wt_ref[...] = wt_out
    chosen = jnp.zeros((ne, tm), F32)
    for kk in range(TOP_K):
        chosen = chosen + jnp.where(sub == idxs[kk], 1.0, 0.0)
    cnt_ref[0] = jnp.sum(chosen, axis=1, keepdims=True).astype(I32)


def _merge(y4, attn, x, w_glu_bf, b_glu, g_ssm, g_attn, w_out_bf, mod3, g_ffn, router_w, router_b, tm=512):
    bsz, seq, d = x.shape
    nblk = y4.shape[0]
    ssm_w = nblk * LANES
    aw = attn.shape[-1]
    ne = router_w.shape[1]
    tm = min(tm, seq)
    spb = seq // tm
    n_tok = bsz * seq
    rw_t = router_w.T
    row = lambda n: pl.BlockSpec((1, n), lambda b, i: (0, 0))
    full = lambda a: pl.BlockSpec(a.shape, lambda b, i: (0,) * a.ndim)
    modv = lambda j: pl.BlockSpec((1, 1, d), lambda b, i: (b, 0, j))
    return pl.pallas_call(
        _merge_body,
        grid=(bsz, seq // tm),
        in_specs=[pl.BlockSpec((nblk, 1, tm, LANES), lambda b, i: (0, b, i, 0)),
                  pl.BlockSpec((1, tm, aw), lambda b, i: (b, i, 0)),
                  pl.BlockSpec((1, tm, d), lambda b, i: (b, i, 0)),
                  full(w_glu_bf), row(ssm_w), row(ssm_w), row(aw), full(w_out_bf),
                  modv(2), row(d), modv(3), modv(4), full(rw_t),
                  pl.BlockSpec((ne, 1), lambda b, i: (0, 0))],
        out_specs=[pl.BlockSpec((1, tm, d), lambda b, i: (b, i, 0)),
                   pl.BlockSpec((1, tm, d), lambda b, i: (b, i, 0)),
                   pl.BlockSpec((TOP_K, tm), lambda b, i: (0, b * spb + i)),
                   pl.BlockSpec((WT_ROWS, tm), lambda b, i: (0, b * spb + i)),
                   pl.BlockSpec((1, ne, 1), lambda b, i: (b * spb + i, 0, 0))],
        out_shape=[jax.ShapeDtypeStruct((bsz, seq, d), F32),
                   jax.ShapeDtypeStruct((bsz, seq, d), F32),
                   jax.ShapeDtypeStruct((TOP_K, n_tok), I32),
                   jax.ShapeDtypeStruct((WT_ROWS, n_tok), F32),
                   jax.ShapeDtypeStruct((bsz * spb, ne, 1), I32)],
        compiler_params=_cparams("parallel", "parallel"),
        name="merge",
    )(y4, attn, x, w_glu_bf, b_glu.reshape(1, ssm_w), g_ssm.reshape(1, ssm_w), g_attn.reshape(1, aw),
      w_out_bf, mod3, g_ffn.reshape(1, d), mod3, mod3, rw_t, router_b.reshape(ne, 1))


def _segment_copies(n_seg, meta, make_copy):
    def body(e, _):
        src0, dst0, units = meta(e)
        off = jnp.int32(0)
        for sz in SEG_UNITS:
            take = (units & sz) != 0
            src = pl.multiple_of(src0 + off, SEG_ALIGN)
            dst = pl.multiple_of(dst0 + off, SEG_ALIGN)

            @pl.when(take)
            def _(src=src, dst=dst, sz=sz):
                make_copy(src, dst, sz * SEG_ALIGN).start()

            off = off + jnp.where(take, sz * SEG_ALIGN, 0)
        return 0

    lax.fori_loop(0, n_seg, body, 0)


def _local_slots(idx, lstart_col, ne):
    tb = idx.shape[1]
    sub = lax.broadcasted_iota(I32, (ne, tb), 0)
    hits = [idx[kk:kk + 1, :] == sub for kk in range(TOP_K)]
    onehot = jnp.zeros((ne, tb), F32)
    for h in hits:
        onehot = onehot + jnp.where(h, 1.0, 0.0)
    r = lax.broadcasted_iota(I32, (tb, tb), 0)
    c = lax.broadcasted_iota(I32, (tb, tb), 1)
    tri = jnp.where(r < c, 1.0, 0.0).astype(BF16)
    slot = _dot(onehot.astype(BF16), tri) + lstart_col
    return [jnp.sum(jnp.where(h, slot, 0.0), axis=0, keepdims=True) for h in hits]


def _dispatch_body(lstart_s, pcnt_s, base_s, idx_ref, lcol_ref, h_ref, xs_ref, ls_ref, sorted_ref, sem, *, ne):
    i = pl.program_id(0)
    tb = h_ref.shape[0]
    lb = sorted_ref.shape[0]
    slots = _local_slots(idx_ref[...], lcol_ref[0].astype(F32), ne)
    row_o = lax.broadcasted_iota(I32, ls_ref.shape, 0)
    ls_out = jnp.full(ls_ref.shape, -1.0, F32)
    for kk in range(TOP_K):
        ls_out = jnp.where(row_o == kk, slots[kk], ls_out)
    ls_ref[...] = ls_out.astype(I32)
    j = lax.broadcasted_iota(I32, (lb, tb), 0).astype(F32)
    pm = jnp.zeros((lb, tb), F32)
    for kk in range(TOP_K):
        pm = jnp.where(j == slots[kk], 1.0, pm)
    sorted_ref[...] = _dot(pm.astype(BF16), h_ref[...].astype(BF16))

    def meta(e):
        return lstart_s[i * ne + e], base_s[i * ne + e], pcnt_s[i * ne + e] // SEG_ALIGN

    def make_copy(src, dst, rows):
        return pltpu.make_async_copy(sorted_ref.at[pl.ds(src, rows), :], xs_ref.at[pl.ds(dst, rows), :], sem)

    _segment_copies(ne, meta, make_copy)
    total = pl.multiple_of(lstart_s[i * ne + ne - 1] + pcnt_s[i * ne + ne - 1], SEG_ALIGN)
    done = xs_ref.at[pl.ds(0, total), :]
    pltpu.make_async_copy(done, done, sem).wait()


def _dispatch(h2, idx_t, plan, cap, tb):
    n, d = h2.shape
    lstart, pcnt, base = plan
    nt, ne = lstart.shape
    lb = tb * TOP_K + ne * SEG_ALIGN
    return pl.pallas_call(
        functools.partial(_dispatch_body, ne=ne),
        grid_spec=pltpu.PrefetchScalarGridSpec(
            num_scalar_prefetch=3,
            grid=(nt,),
            in_specs=[pl.BlockSpec((TOP_K, tb), lambda i, *_: (0, i)),
                      pl.BlockSpec((1, ne, 1), lambda i, *_: (i, 0, 0)),
                      pl.BlockSpec((tb, d), lambda i, *_: (i, 0))],
            out_specs=[pl.BlockSpec(memory_space=pl.ANY),
                       pl.BlockSpec((WT_ROWS, tb), lambda i, *_: (0, i))],
            scratch_shapes=[pltpu.VMEM((lb, d), F32), pltpu.SemaphoreType.DMA(())],
        ),
        out_shape=[jax.ShapeDtypeStruct((cap, d), F32),
                   jax.ShapeDtypeStruct((WT_ROWS, n), I32)],
        compiler_params=_cparams("arbitrary"),
        name="dispatch",
    )(lstart.reshape(-1), pcnt.reshape(-1), base.reshape(-1), idx_t, lstart.reshape(nt, ne, 1), h2)


def _route_plan(tile_cnt, tg, cap):
    cnt = tile_cnt[:, :, 0]
    pcnt = (cnt + SEG_ALIGN - 1) // SEG_ALIGN * SEG_ALIGN
    lstart = jnp.cumsum(pcnt, axis=1) - pcnt
    etot = jnp.sum(pcnt, axis=0)
    eoff = jnp.cumsum(etot) - etot
    base = eoff[None, :] + jnp.cumsum(pcnt, axis=0) - pcnt
    items = _plan_items(etot, cap, tg)
    return (lstart.astype(I32), pcnt.astype(I32), base.astype(I32)), items


def _experts_body(tile_ref, exp_ref, lo_ref, hi_ref, first_ref, newe_ref,
                  xs_ref, wg_ref, bg_ref, wu_ref, bu_ref, wd_ref, bd_ref, ys_ref,
                  wg_bf, wu_bf, wd_bf):
    i = pl.program_id(0)
    lo = lo_ref[i]
    hi = hi_ref[i]

    @pl.when(newe_ref[i] == 1)
    def _cast():
        wg_bf[...] = wg_ref[0].astype(BF16)
        wu_bf[...] = wu_ref[0].astype(BF16)
        wd_bf[...] = wd_ref[0].astype(BF16)

    @pl.when(hi > lo)
    def _compute():
        x = xs_ref[...].astype(BF16)
        gate = jnp.minimum(_dot(x, wg_bf[...]) + bg_ref[0], SWIGLU_LIMIT)
        lin = jnp.clip(_dot(x, wu_bf[...]) + bu_ref[0], -SWIGLU_LIMIT, SWIGLU_LIMIT)
        act = gate * jax.nn.sigmoid(SWIGLU_ALPHA * gate) * (lin + 1.0)
        y = _dot(act.astype(BF16), wd_bf[...]) + bd_ref[0]
        row = lax.broadcasted_iota(I32, y.shape, 0)
        mine = (row >= lo) & (row < hi)

        @pl.when(first_ref[i] == 1)
        def _():
            ys_ref[...] = jnp.where(mine, y, 0.0)

        @pl.when(first_ref[i] == 0)
        def _():
            ys_ref[...] = jnp.where(mine, y, ys_ref[...])


def _experts(xs, items, w_gate, b_gate, w_up, b_up, w_down, b_down, tg):
    nk, d = xs.shape
    ne, _, dff = w_gate.shape
    n_items = items[0].shape[0]
    wspec = lambda shp: pl.BlockSpec((1,) + shp, lambda i, t, e, lo, hi, f, nw: (e[i], 0, 0))
    return pl.pallas_call(
        _experts_body,
        grid_spec=pltpu.PrefetchScalarGridSpec(
            num_scalar_prefetch=6,
            grid=(n_items,),
            in_specs=[pl.BlockSpec((tg, d), lambda i, t, e, lo, hi, f, nw: (t[i], 0)),
                      wspec((d, dff)), wspec((1, dff)),
                      wspec((d, dff)), wspec((1, dff)),
                      wspec((dff, d)), wspec((1, d))],
            out_specs=pl.BlockSpec((tg, d), lambda i, t, e, lo, hi, f, nw: (t[i], 0)),
            scratch_shapes=[pltpu.VMEM((d, dff), BF16), pltpu.VMEM((d, dff), BF16), pltpu.VMEM((dff, d), BF16)],
        ),
        out_shape=jax.ShapeDtypeStruct((nk, d), F32),
        compiler_params=_cparams("arbitrary"),
        name="experts",
    )(*items, xs, w_gate, b_gate.reshape(ne, 1, dff), w_up, b_up.reshape(ne, 1, dff),
      w_down, b_down.reshape(ne, 1, d))


def _plan_items(counts, nk, tg):
    ne = counts.shape[0]
    n_tiles = nk // tg
    n_items = n_tiles + ne
    ends = jnp.cumsum(counts)
    offs = ends - counts
    first_tile = offs // tg
    last_tile = jnp.maximum(ends - 1, 0) // tg
    per_e = jnp.where(counts > 0, last_tile - first_tile + 1, 0)
    item_end = jnp.cumsum(per_e)
    item_start = item_end - per_e
    total = item_end[-1]
    ii = jnp.arange(n_items, dtype=I32)
    first_after = lambda i: jnp.minimum(jnp.sum((item_end[None, :] <= i[:, None]).astype(I32), axis=1), ne - 1)
    e_of = first_after(ii)
    valid = ii < total
    last_e = first_after(total[None] - 1)[0]
    e_of = jnp.where(valid, e_of, last_e)
    last_tile = jnp.maximum(ends[-1] - 1, 0) // tg
    tile = jnp.where(valid, first_tile[e_of] + (ii - item_start[e_of]), last_tile).astype(I32)
    lo = jnp.clip(offs[e_of] - tile * tg, 0, tg)
    hi = jnp.clip(ends[e_of] - tile * tg, 0, tg)
    lo = jnp.where(valid, lo, 0).astype(I32)
    hi = jnp.where(valid, hi, 0).astype(I32)
    prev_tile = jnp.concatenate([jnp.full((1,), -1, I32), tile[:-1]])
    prev_e = jnp.concatenate([jnp.full((1,), -1, I32), e_of[:-1]])
    first = (tile != prev_tile).astype(I32)
    newe = (e_of != prev_e).astype(I32)
    return (tile, e_of, lo, hi, first, newe)


def _combine_body(lstart_s, pcnt_s, base_s, ls_ref, wt_ref, x1_ref, gf_ref, gfin_ref, sho_ref, sco_ref,
                  ys_ref, o_ref, buf, sem, *, ne, spb):
    i = pl.program_id(0) * spb + pl.program_id(1)
    tb = x1_ref.shape[1]
    lb = buf.shape[0]

    @pl.when(i == 0)
    def _init():
        buf[...] = jnp.zeros(buf.shape, F32)

    def meta(e):
        return base_s[i * ne + e], lstart_s[i * ne + e], pcnt_s[i * ne + e] // SEG_ALIGN

    def make_copy(src, dst, rows):
        return pltpu.make_async_copy(ys_ref.at[pl.ds(src, rows), :], buf.at[pl.ds(dst, rows), :], sem)

    _segment_copies(ne, meta, make_copy)
    ls_col = ls_ref[...].astype(F32).T
    wt_col = wt_ref[...].T
    j = lax.broadcasted_iota(I32, (tb, lb), 1).astype(F32)
    wm = jnp.zeros((tb, lb), F32)
    for kk in range(TOP_K):
        wm = jnp.where(j == ls_col[:, kk:kk + 1], wt_col[:, kk:kk + 1], wm)
    total = pl.multiple_of(lstart_s[i * ne + ne - 1] + pcnt_s[i * ne + ne - 1], SEG_ALIGN)
    done = buf.at[pl.ds(0, total), :]
    pltpu.make_async_copy(done, done, sem).wait()
    ffn = _dot(wm.astype(BF16), buf[...].astype(BF16))
    x2 = x1_ref[0] + gf_ref[0] * ffn
    o_ref[0] = _rms(x2, gfin_ref[...]) * (1.0 + sco_ref[0]) + sho_ref[0]


def _combine(plan, ls_t, wt_t, x1, mod3, g_final, fmod3, ys, tb):
    bsz, seq, d = x1.shape
    lstart, pcnt, base = plan
    nt, ne = lstart.shape
    spb = seq // tb
    lb = tb * TOP_K + ne * SEG_ALIGN
    return pl.pallas_call(
        functools.partial(_combine_body, ne=ne, spb=spb),
        grid_spec=pltpu.PrefetchScalarGridSpec(
            num_scalar_prefetch=3,
            grid=(bsz, spb),
            in_specs=[pl.BlockSpec((WT_ROWS, tb), lambda b, i, *_: (0, b * spb + i)),
                      pl.BlockSpec((WT_ROWS, tb), lambda b, i, *_: (0, b * spb + i)),
                      pl.BlockSpec((1, tb, d), lambda b, i, *_: (b, i, 0)),
                      pl.BlockSpec((1, 1, d), lambda b, i, *_: (b, 0, 5)),
                      pl.BlockSpec((1, d), lambda b, i, *_: (0, 0)),
                      pl.BlockSpec((1, 1, d), lambda b, i, *_: (b, 0, 0)),
                      pl.BlockSpec((1, 1, d), lambda b, i, *_: (b, 0, 1)),
                      pl.BlockSpec(memory_space=pl.ANY)],
            out_specs=pl.BlockSpec((1, tb, d), lambda b, i, *_: (b, i, 0)),
            scratch_shapes=[pltpu.VMEM((lb, d), F32), pltpu.SemaphoreType.DMA(())],
        ),
        out_shape=jax.ShapeDtypeStruct((bsz, seq, d), F32),
        compiler_params=_cparams("arbitrary", "arbitrary"),
        name="combine",
    )(lstart.reshape(-1), pcnt.reshape(-1), base.reshape(-1), ls_t, wt_t, x1, mod3, g_final.reshape(1, d),
      fmod3, fmod3, ys)


def kernel(x, c, positions, ada_w, ada_b, final_ada_w, final_ada_b, norm_mix_g, norm_ffn_g, norm_final_g, w_in, ssm_lambda_re, ssm_lambda_im, ssm_log_dt, ssm_b_re, ssm_b_im, ssm_c_re, ssm_c_im, ssm_d, ssm_w_glu, ssm_b_glu, out_norm_ssm_g, out_norm_attn_g, w_out, router_w, router_b, exp_w_gate, exp_b_gate, exp_w_up, exp_b_up, exp_w_down, exp_b_down):
    bsz, seq, d = x.shape
    depth = ada_w.shape[0]
    ssm_w = ssm_d.shape[-1]
    attn_w = (w_in.shape[-1] - ssm_w) // 3
    ne = router_w.shape[-1]
    n_tok = bsz * seq
    nc = seq // SSM_CHUNK
    tg = 256

    rope_tab, rope_exp = _rope_tables(positions)
    fmod3 = _adaln(c, final_ada_w, final_ada_b).reshape(bsz, 1, 2 * d)
    for l in range(depth):
        mod3 = _adaln(c, ada_w[l], ada_b[l]).reshape(bsz, 1, -1)
        u4, q, k, v = _inproj(x, norm_mix_g[l], mod3, w_in[l].astype(BF16), rope_tab, rope_exp, ssm_w, attn_w)
        tables = _s5_params(ssm_lambda_re[l], ssm_lambda_im[l], ssm_log_dt[l], ssm_b_re[l], ssm_b_im[l],
                            ssm_c_re[l], ssm_c_im[l], ssm_d[l], nc)
        nblk = ssm_w // LANES
        y4 = _s5(u4.reshape(nblk, bsz, nc, SSM_CHUNK * LANES), *tables).reshape(nblk, bsz, seq, LANES)
        attn = _attention(q, k, v)
        x1, h2, idx_t, wt_t, tile_cnt = _merge(y4, attn, x, ssm_w_glu[l].astype(BF16), ssm_b_glu[l], out_norm_ssm_g[l],
                                     out_norm_attn_g[l], w_out[l].astype(BF16), mod3, norm_ffn_g[l],
                                     router_w[l], router_b[l])
        tb = seq // (tile_cnt.shape[0] // bsz)
        cap = n_tok * TOP_K + tile_cnt.shape[0] * ne * SEG_ALIGN
        assert cap % tg == 0 and tb // SEG_ALIGN < 2 * SEG_UNITS[0]
        plan, items = _route_plan(tile_cnt, tg, cap)
        xs, ls_t = _dispatch(h2.reshape(n_tok, d), idx_t, plan, cap, tb)
        ys = _experts(xs, items, exp_w_gate[l], exp_b_gate[l], exp_w_up[l], exp_b_up[l],
                      exp_w_down[l], exp_b_down[l], tg)
        if l + 1 < depth:
            raise NotImplementedError("depth > 1 needs the non-final combine")
        x = _combine(plan, ls_t, wt_t, x1, mod3, norm_final_g, fmod3, ys, tb)
    return x
```

```python
import functools
import math

import jax
import jax.numpy as jnp
from jax import lax
from jax.experimental import pallas as pl
from jax.experimental.pallas import tpu as pltpu

F32 = jnp.float32
BF16 = jnp.bfloat16
I32 = jnp.int32
HIGHEST = lax.Precision.HIGHEST

LANES = 128
HEAD_DIM = 64
MOBA_BLOCK = 256
MOBA_TOPK = 3
ROT_DIM = HEAD_DIM // 4
ROPE_THETA = 500000.0
SSM_GROUP = 16
SSM_CHUNK = 16
TOP_K = 4
SWIGLU_ALPHA = 1.702
SWIGLU_LIMIT = 7.0
NORM_EPS = 1e-5
NEG_INF = -1e30
LOG2_E = math.log2(math.e)
KV_TILE = 2 * MOBA_BLOCK
WT_ROWS = 8
SEG_ALIGN = 8
SEG_UNITS = (64, 32, 16, 8, 4, 2, 1)
SORT_CHUNK = 256
VMEM_LIMIT = 56 * 1024 * 1024


def _cparams(*sem):
    return pltpu.CompilerParams(dimension_semantics=sem, vmem_limit_bytes=VMEM_LIMIT)


def _dot(a, b):
    return jnp.dot(a, b, preferred_element_type=F32)


def _rms(x, g):
    return x * lax.rsqrt(jnp.mean(x * x, axis=-1, keepdims=True) + NORM_EPS) * g


def _adaln_body(c_ref, w_ref, b_ref, o_ref):
    c = c_ref[...]
    ca = c * jax.nn.sigmoid(c)
    o_ref[...] = jnp.dot(ca, w_ref[...], preferred_element_type=F32, precision=HIGHEST) + b_ref[...]


def _adaln(c, w, b, tn=512):
    bsz, d = c.shape
    n = w.shape[1]
    return pl.pallas_call(
        _adaln_body,
        grid=(n // tn,),
        in_specs=[pl.BlockSpec((bsz, d), lambda j: (0, 0)),
                  pl.BlockSpec((d, tn), lambda j: (0, j)),
                  pl.BlockSpec((1, tn), lambda j: (0, j))],
        out_specs=pl.BlockSpec((bsz, tn), lambda j: (0, j)),
        out_shape=jax.ShapeDtypeStruct((bsz, n), F32),
        compiler_params=_cparams("parallel"),
        name="adaln",
    )(c, w, b.reshape(1, n))


def _inproj_body(x_ref, g_ref, sh_ref, sc_ref, w_ref, tab_ref, exp_ref, u_ref, q_ref, k_ref, v_ref):
    x = x_ref[0]
    tm = x.shape[0]
    h = _rms(x, g_ref[...]) * (1.0 + sc_ref[0]) + sh_ref[0]
    proj = _dot(h.astype(BF16), w_ref[...])
    ssm_w = u_ref.shape[0] * LANES
    aw = q_ref.shape[2]
    for j in range(u_ref.shape[0]):
        u_ref[j, 0] = proj[:, LANES * j:LANES * (j + 1)].astype(BF16)
    tab = lax.dot_general(tab_ref[0], exp_ref[...], (((0,), (0,)), ((), ())),
                          preferred_element_type=F32, precision=HIGHEST)
    lane1 = lax.broadcasted_iota(I32, (tm, LANES), 1)
    cos1 = tab[:, :LANES] + jnp.where((lane1 & (HEAD_DIM - 1)) >= ROT_DIM, 1.0, 0.0)
    reps = aw // LANES
    cosf = jnp.concatenate([cos1] * reps, axis=1)
    sinf = jnp.concatenate([tab[:, LANES:]] * reps, axis=1)
    lane = lax.broadcasted_iota(I32, (tm, aw), 1)
    first_half = (lane & (HEAD_DIM - 1)) < (ROT_DIM // 2)

    def rope(t):
        partner = jnp.where(first_half, pltpu.roll(t, aw - ROT_DIM // 2, 1), pltpu.roll(t, ROT_DIM // 2, 1))
        return t * cosf + partner * sinf

    q = rope(proj[:, ssm_w:ssm_w + aw]) * (HEAD_DIM ** -0.5 * LOG2_E)
    k = rope(proj[:, ssm_w + aw:ssm_w + 2 * aw])
    q_ref[0] = q.astype(BF16)
    k_ref[0] = k.astype(BF16)
    v_ref[0] = proj[:, ssm_w + 2 * aw:ssm_w + 3 * aw].astype(BF16)


def _inproj(x, gain, mod3, w_in_bf, rope_tab, rope_exp, ssm_w, attn_w, tm=512):
    bsz, seq, d = x.shape
    n_u = ssm_w // LANES
    tm = min(tm, seq)
    return pl.pallas_call(
        _inproj_body,
        grid=(bsz, seq // tm),
        in_specs=[pl.BlockSpec((1, tm, d), lambda b, i: (b, i, 0)),
                  pl.BlockSpec((1, d), lambda b, i: (0, 0)),
                  pl.BlockSpec((1, 1, d), lambda b, i: (b, 0, 0)),
                  pl.BlockSpec((1, 1, d), lambda b, i: (b, 0, 1)),
                  pl.BlockSpec(w_in_bf.shape, lambda b, i: (0, 0)),
                  pl.BlockSpec((1, ROT_DIM, tm), lambda b, i: (b, 0, i)),
                  pl.BlockSpec(rope_exp.shape, lambda b, i: (0, 0))],
        out_specs=[pl.BlockSpec((n_u, 1, tm, LANES), lambda b, i: (0, b, i, 0)),
                   pl.BlockSpec((1, tm, attn_w), lambda b, i: (b, i, 0)),
                   pl.BlockSpec((1, tm, attn_w), lambda b, i: (b, i, 0)),
                   pl.BlockSpec((1, tm, attn_w), lambda b, i: (b, i, 0))],
        out_shape=[jax.ShapeDtypeStruct((n_u, bsz, seq, LANES), BF16),
                   jax.ShapeDtypeStruct((bsz, seq, attn_w), BF16),
                   jax.ShapeDtypeStruct((bsz, seq, attn_w), BF16),
                   jax.ShapeDtypeStruct((bsz, seq, attn_w), BF16)],
        compiler_params=_cparams("parallel", "parallel"),
        name="inproj",
    )(x, gain.reshape(1, d), mod3, mod3, w_in_bf, rope_tab, rope_exp)


def _rope_tables(positions):
    half = ROT_DIM // 2
    inv_freq = ROPE_THETA ** (-jnp.arange(0, ROT_DIM, 2, dtype=F32) / ROT_DIM)
    ang = positions.astype(F32)[:, None, :] * inv_freq[None, :, None]
    tab = jnp.concatenate([jnp.cos(ang), jnp.sin(ang)], axis=1)
    i = jnp.arange(ROT_DIM)[:, None]
    hl = jnp.arange(LANES)[None, :] % HEAD_DIM
    e_cos = jnp.where((i < half) & ((hl == i) | (hl == i + half)), 1.0, 0.0)
    e_sin = jnp.where(i >= half, jnp.where(hl == i - half, -1.0, jnp.where(hl == i, 1.0, 0.0)), 0.0)
    return tab, jnp.concatenate([e_cos, e_sin], axis=1).astype(F32)


def _s5_params(lam_re, lam_im, log_dt, b_re, b_im, c_re, c_im, d_skip, n_chunks):
    g_all, p = lam_re.shape
    hc = b_re.shape[-1]
    t = SSM_CHUNK
    gpb = LANES // hc
    nblk = g_all // gpb
    lr, li = lam_re.astype(F32), lam_im.astype(F32)
    dt = jnp.exp(log_dt.astype(F32))[:, None]
    ldr, ldi = lr * dt, li * dt

    def apow(k):
        k = jnp.asarray(k, F32)[..., None, None]
        mag = jnp.exp(ldr * k)
        return mag * jnp.cos(ldi * k), mag * jnp.sin(ldi * k)

    ar, ai = apow(1.0)
    zr, zi = ar - 1.0, ai
    den = lr * lr + li * li
    cr = (zr * lr + zi * li) / den
    ci = (zi * lr - zr * li) / den
    bre, bim = b_re.astype(F32), b_im.astype(F32)
    bbr = cr[..., None] * bre - ci[..., None] * bim
    bbi = cr[..., None] * bim + ci[..., None] * bre
    cre, cim = c_re.astype(F32), c_im.astype(F32)
    eye = jnp.eye(gpb, dtype=F32)

    pr, pi = apow(jnp.arange(t))
    abr = pr[..., None] * bbr - pi[..., None] * bbi
    abi = pr[..., None] * bbi + pi[..., None] * bbr
    kf = (jnp.einsum('gnp,kgph->kghn', cre, abr, precision=HIGHEST)
          - jnp.einsum('gnp,kgph->kghn', cim, abi, precision=HIGHEST))
    kblk = jnp.einsum('kjghn,gf->kjghfn', kf.reshape(t, nblk, gpb, hc, hc), eye)
    kblk = kblk.reshape(t, nblk, LANES, LANES)
    kpad = jnp.concatenate([jnp.zeros_like(kblk[:1]), kblk], axis=0)
    pm = jnp.concatenate([kpad[0:t], kpad[1:t + 1]], axis=-1)
    kstack = jnp.transpose(pm[::-1], (1, 0, 2, 3)).reshape(nblk, t * LANES, 2 * LANES)

    def inject(ab):
        ab = ab[::-1].reshape(t, nblk, gpb, p, hc)
        return jnp.einsum('sjgph,gf->jsghfp', ab, eye).reshape(nblk, t * LANES, gpb * p)

    bm = jnp.concatenate([inject(abr), inject(abi)], axis=-1)

    qr, qi = apow(jnp.arange(t) + 1.0)
    c_from_re = cre[None] * qr[:, :, None, :] - cim[None] * qi[:, :, None, :]
    c_from_im = -cre[None] * qi[:, :, None, :] - cim[None] * qr[:, :, None, :]

    def readout(cc):
        cc = cc.reshape(t, nblk, gpb, hc, p)
        return jnp.einsum('tjgnp,gf->jgptfn', cc, eye).reshape(nblk, gpb * p, t * LANES)

    cm = jnp.concatenate([readout(c_from_re), readout(c_from_im)], axis=1)

    n_steps = max(1, int(math.log2(n_chunks)))
    shifts = [float(t * (1 << s)) for s in range(n_steps)]
    sr, si = apow(jnp.asarray(shifts))
    ap = jnp.concatenate([sr.reshape(n_steps, nblk, gpb * p), si.reshape(n_steps, nblk, gpb * p)], axis=-1)
    ap = jnp.transpose(ap, (1, 0, 2))
    dvec = jnp.tile(d_skip.astype(F32).reshape(nblk, 1, LANES), (1, 1, t))
    return kstack.astype(BF16), bm.astype(BF16), cm.astype(BF16), ap, dvec


def _s5_body(x_ref, ks_ref, bm_ref, cm_ref, ap_ref, d_ref, y_ref, h_ref):
    nc = x_ref.shape[2]
    t = SSM_CHUNK
    pad = h_ref.shape[0] - nc
    half = h_ref.shape[1] // 2
    x = x_ref[0, 0]
    h_ref[0:pad] = jnp.zeros((pad, h_ref.shape[1]), F32)
    h_ref[pad:pad + nc] = _dot(x, bm_ref[0])
    for step in range(ap_ref.shape[1]):
        d = 1 << step
        cur = h_ref[pad:pad + nc]
        sft = h_ref[pad - d:pad + nc - d]
        ar = ap_ref[0, step:step + 1, 0:half]
        ai = ap_ref[0, step:step + 1, half:]
        cr, ci = cur[:, :half], cur[:, half:]
        sr, si = sft[:, :half], sft[:, half:]
        h_ref[pad:pad + nc, 0:half] = cr + ar * sr - ai * si
        h_ref[pad:pad + nc, half:] = ci + ar * si + ai * sr
    h_prev = h_ref[pad - 1:pad + nc - 1]
    ys = _dot(h_prev.astype(BF16), cm_ref[0])
    for t0 in range(0, t, 2):
        lo, hi = t0 * LANES, (t0 + 2) * LANES
        conv = _dot(x[:, :hi], ks_ref[0, (t - 2 - t0) * LANES:, :])
        y = conv + ys[:, lo:hi] + d_ref[0][:, lo:hi] * x[:, lo:hi].astype(F32)
        y_ref[0, 0, :, lo:hi] = y.astype(y_ref.dtype)


def _s5(u4, kstack, bm, cm, ap, dvec):
    nblk, bsz, nc, w = u4.shape
    sw = bm.shape[-1]
    pad = max(nc // 2, 8)
    return pl.pallas_call(
        _s5_body,
        grid=(nblk, bsz),
        in_specs=[pl.BlockSpec((1, 1, nc, w), lambda j, b: (j, b, 0, 0)),
                  pl.BlockSpec((1,) + kstack.shape[1:], lambda j, b: (j, 0, 0)),
                  pl.BlockSpec((1,) + bm.shape[1:], lambda j, b: (j, 0, 0)),
                  pl.BlockSpec((1,) + cm.shape[1:], lambda j, b: (j, 0, 0)),
                  pl.BlockSpec((1,) + ap.shape[1:], lambda j, b: (j, 0, 0)),
                  pl.BlockSpec((1, 1, w), lambda j, b: (j, 0, 0))],
        out_specs=pl.BlockSpec((1, 1, nc, w), lambda j, b: (j, b, 0, 0)),
        out_shape=jax.ShapeDtypeStruct((nblk, bsz, nc, w), BF16),
        scratch_shapes=[pltpu.VMEM((pad + nc, sw), F32)],
        compiler_params=_cparams("parallel", "parallel"),
        name="s5",
    )(u4, kstack, bm, cm, ap, dvec)


def _attn_body(q_ref, k_ref, v_ref, o_ref, vt_ref, km_ref, bias_ref, sa_ref, sb_ref):
    qi = pl.program_id(2)
    seq = k_ref.shape[1]
    tq = q_ref.shape[1]
    nb = seq // MOBA_BLOCK
    n_kv = seq // KV_TILE
    bpt = KV_TILE // MOBA_BLOCK
    blk_shift = MOBA_BLOCK.bit_length() - 1

    @pl.when(qi == 0)
    def _build():
        k = k_ref[0].astype(F32)
        v = v_ref[0].astype(F32)
        lane = lax.broadcasted_iota(I32, (seq, LANES), 1)
        va = jnp.where(lane < HEAD_DIM, v, jnp.where(lane == HEAD_DIM, 1.0, 0.0))
        vb = jnp.where(lane >= HEAD_DIM, v, jnp.where(lane == 0, 1.0, 0.0))
        for t in range(n_kv):
            rows = slice(t * KV_TILE, (t + 1) * KV_TILE)
            vt_ref[0, t] = va[rows].T.astype(BF16)
            vt_ref[1, t] = vb[rows].T.astype(BF16)
        km = jnp.mean(k.reshape(nb, MOBA_BLOCK, LANES), axis=1)
        lane_b = lax.broadcasted_iota(I32, (nb, LANES), 1)
        km_ref[0] = jnp.where(lane_b < HEAD_DIM, km, 0.0)
        km_ref[1] = jnp.where(lane_b >= HEAD_DIM, km, 0.0)

    q_t = q_ref[0].astype(F32).T
    feat = lax.broadcasted_iota(I32, (LANES, tq), 0)
    q_h = [jnp.where(feat < HEAD_DIM, q_t, 0.0).astype(BF16),
           jnp.where(feat >= HEAD_DIM, q_t, 0.0).astype(BF16)]

    jidx = lax.broadcasted_iota(I32, (nb, tq), 0)
    qblk = qi * (tq // MOBA_BLOCK) + lax.shift_right_logical(
        lax.broadcasted_iota(I32, (nb, tq), 1), blk_shift)
    for hx in range(2):
        g = _dot(km_ref[hx], q_t)
        cnt = jnp.zeros((nb, tq), F32)
        for jp in range(nb):
            row = g[jp:jp + 1, :]
            beats = jnp.where(row > g, 1.0, jnp.where(row == g, jnp.where(jp < jidx, 1.0, 0.0), 0.0))
            cnt = cnt + jnp.where(jp < qblk, beats, 0.0)
        past_sel = jnp.where(jidx < qblk, jnp.where(cnt < MOBA_TOPK, 1.0, 0.0), 0.0)
        sel = jnp.where(jidx == qblk, 1.0, past_sel)
        bias_ref[hx] = jnp.where(sel > 0.5, 0.0, NEG_INF)

    def raw_scores(dst_ref, tile):
        off = pl.multiple_of(tile * KV_TILE, KV_TILE)
        k_t = k_ref[0, pl.ds(off, KV_TILE), :]
        for hx in range(2):
            dst_ref[hx] = _dot(k_t, q_h[hx])

    def fold(carry, src_ref, tile, live=None, causal=None):
        out = []
        for hx in range(2):
            parts, biases = [], []
            for j in range(bpt):
                s_j = src_ref[hx, j * MOBA_BLOCK:(j + 1) * MOBA_BLOCK, :]
                if causal is not None:
                    s_j = jnp.where(causal[j], s_j, NEG_INF)
                b_j = bias_ref[hx, pl.ds(tile * bpt + j, 1), :]
                if live is not None:
                    b_j = jnp.where(live, b_j, NEG_INF)
                parts.append(s_j)
                biases.append(b_j)
            m_n = jnp.max(parts[0], axis=0, keepdims=True) + biases[0]
            for s_j, b_j in zip(parts[1:], biases[1:]):
                m_n = jnp.maximum(m_n, jnp.max(s_j, axis=0, keepdims=True) + b_j)
            if carry is None:
                p_t = jnp.concatenate([jnp.exp2(s_j + (b_j - m_n)) for s_j, b_j in zip(parts, biases)], axis=0)
                out += [m_n, _dot(vt_ref[hx, tile], p_t.astype(BF16))]
            else:
                m_c, acc_c = carry[2 * hx], carry[2 * hx + 1]
                m_n = jnp.maximum(m_c, m_n)
                p_t = jnp.concatenate([jnp.exp2(s_j + (b_j - m_n)) for s_j, b_j in zip(parts, biases)], axis=0)
                out += [m_n, acc_c * jnp.exp2(m_c - m_n) + _dot(vt_ref[hx, tile], p_t.astype(BF16))]
        return tuple(out)

    raw_scores(sa_ref, qi)
    raw_scores(sb_ref, 0)
    krow = lax.broadcasted_iota(I32, (MOBA_BLOCK, tq), 0)
    qcol = lax.broadcasted_iota(I32, (MOBA_BLOCK, tq), 1)
    state = fold(None, sa_ref, qi, causal=[krow + j * MOBA_BLOCK <= qcol for j in range(bpt)])

    def body(i, carry):
        t0 = 2 * i
        raw_scores(sa_ref, jnp.minimum(t0 + 1, n_kv - 1))
        carry = fold(carry, sb_ref, t0)
        raw_scores(sb_ref, jnp.minimum(t0 + 2, n_kv - 1))
        return fold(carry, sa_ref, t0 + 1, live=t0 + 1 < qi)

    _, acc_a, _, acc_b = lax.fori_loop(0, (qi + 1) // 2, body, state)
    o_t = jnp.where(feat < HEAD_DIM, acc_a / acc_a[HEAD_DIM:HEAD_DIM + 1, :], acc_b / acc_b[0:1, :])
    o_ref[0] = o_t.T.astype(o_ref.dtype)


def _attention(q, k, v):
    bsz, seq, aw = q.shape
    npair = aw // LANES
    tq = KV_TILE
    nb = seq // MOBA_BLOCK
    assert seq % KV_TILE == 0 and nb % 8 == 0
    return pl.pallas_call(
        _attn_body,
        grid=(bsz, npair, seq // tq),
        in_specs=[pl.BlockSpec((1, tq, LANES), lambda b, p, i: (b, i, p)),
                  pl.BlockSpec((1, seq, LANES), lambda b, p, i: (b, 0, p)),
                  pl.BlockSpec((1, seq, LANES), lambda b, p, i: (b, 0, p))],
        out_specs=pl.BlockSpec((1, tq, LANES), lambda b, p, i: (b, i, p)),
        out_shape=jax.ShapeDtypeStruct((bsz, seq, aw), BF16),
        scratch_shapes=[pltpu.VMEM((2, seq // KV_TILE, LANES, KV_TILE), BF16),
                        pltpu.VMEM((2, nb, LANES), F32),
                        pltpu.VMEM((2, nb, tq), F32),
                        pltpu.VMEM((2, KV_TILE, tq), F32),
                        pltpu.VMEM((2, KV_TILE, tq), F32)],
        compiler_params=_cparams("parallel", "parallel", "arbitrary"),
        name="attn",
    )(q, k, v)


def _merge_body(y_ref, a_ref, x_ref, wglu_ref, bglu_ref, gs_ref, ga_ref, wout_ref, gm_ref,
                gf_ref, shf_ref, scf_ref, rw_ref, rb_ref, x1_ref, h2_ref, idx_ref, wt_ref, cnt_ref):
    nblk = y_ref.shape[0]
    y = jnp.concatenate([y_ref[j, 0] for j in range(nblk)], axis=1).astype(F32)
    tm = y.shape[0]
    g = 0.5 * y * (1.0 + lax.erf(y * (2.0 ** -0.5)))
    glu = g * jax.nn.sigmoid(_dot(g.astype(BF16), wglu_ref[...]) + bglu_ref[...])
    ssm_n = _rms(glu, gs_ref[...])
    att_n = _rms(a_ref[0].astype(F32), ga_ref[...])
    merged = jnp.concatenate([ssm_n, att_n], axis=1).astype(BF16)
    mix = _dot(merged, wout_ref[...])
    x1 = x_ref[0] + gm_ref[0] * mix
    x1_ref[0] = x1
    h2 = _rms(x1, gf_ref[...]) * (1.0 + scf_ref[0]) + shf_ref[0]
    h2_ref[0] = h2
    logits = lax.dot_general(rw_ref[...], h2, (((1,), (1,)), ((), ())),
                             preferred_element_type=F32, precision=HIGHEST) + rb_ref[...]
    ne = logits.shape[0]
    sub = lax.broadcasted_iota(I32, (ne, tm), 0)
    vals, idxs = [], []
    for _ in range(TOP_K):
        mx = jnp.max(logits, axis=0, keepdims=True)
        ix = jnp.min(jnp.where(logits == mx, sub, ne), axis=0, keepdims=True)
        vals.append(mx)
        idxs.append(ix)
        logits = jnp.where(sub == ix, -jnp.inf, logits)
    exps = [jnp.exp(vv - vals[0]) for vv in vals]
    tot = exps[0]
    for e in exps[1:]:
        tot = tot + e
    row_i = lax.broadcasted_iota(I32, idx_ref.shape, 0)
    row_w = lax.broadcasted_iota(I32, wt_ref.shape, 0)
    idx_out = jnp.zeros(idx_ref.shape, I32)
    wt_out = jnp.zeros(wt_ref.shape, F32)
    for kk in range(TOP_K):
        idx_out = jnp.where(row_i == kk, idxs[kk], idx_out)
        wt_out = jnp.where(row_w == kk, exps[kk] / tot, wt_out)
    idx_ref[...] = idx_out
    wt_ref[...] = wt_out
    chosen = jnp.zeros((ne, tm), F32)
    for kk in range(TOP_K):
        chosen = chosen + jnp.where(sub == idxs[kk], 1.0, 0.0)
    cnt_ref[0] = jnp.sum(chosen, axis=1, keepdims=True).astype(I32)


def _merge(y4, attn, x, w_glu_bf, b_glu, g_ssm, g_attn, w_out_bf, mod3, g_ffn, router_w, router_b, tm=512):
    bsz, seq, d = x.shape
    nblk = y4.shape[0]
    ssm_w = nblk * LANES
    aw = attn.shape[-1]
    ne = router_w.shape[1]
    tm = min(tm, seq)
    spb = seq // tm
    n_tok = bsz * seq
    rw_t = router_w.T
    row = lambda n: pl.BlockSpec((1, n), lambda b, i: (0, 0))
    full = lambda a: pl.BlockSpec(a.shape, lambda b, i: (0,) * a.ndim)
    modv = lambda j: pl.BlockSpec((1, 1, d), lambda b, i: (b, 0, j))
    return pl.pallas_call(
        _merge_body,
        grid=(bsz, seq // tm),
        in_specs=[pl.BlockSpec((nblk, 1, tm, LANES), lambda b, i: (0, b, i, 0)),
                  pl.BlockSpec((1, tm, aw), lambda b, i: (b, i, 0)),
                  pl.BlockSpec((1, tm, d), lambda b, i: (b, i, 0)),
                  full(w_glu_bf), row(ssm_w), row(ssm_w), row(aw), full(w_out_bf),
                  modv(2), row(d), modv(3), modv(4), full(rw_t),
                  pl.BlockSpec((ne, 1), lambda b, i: (0, 0))],
        out_specs=[pl.BlockSpec((1, tm, d), lambda b, i: (b, i, 0)),
                   pl.BlockSpec((1, tm, d), lambda b, i: (b, i, 0)),
                   pl.BlockSpec((TOP_K, tm), lambda b, i: (0, b * spb + i)),
                   pl.BlockSpec((WT_ROWS, tm), lambda b, i: (0, b * spb + i)),
                   pl.BlockSpec((1, ne, 1), lambda b, i: (b * spb + i, 0, 0))],
        out_shape=[jax.ShapeDtypeStruct((bsz, seq, d), F32),
                   jax.ShapeDtypeStruct((bsz, seq, d), F32),
                   jax.ShapeDtypeStruct((TOP_K, n_tok), I32),
                   jax.ShapeDtypeStruct((WT_ROWS, n_tok), F32),
                   jax.ShapeDtypeStruct((bsz * spb, ne, 1), I32)],
        compiler_params=_cparams("parallel", "parallel"),
        name="merge",
    )(y4, attn, x, w_glu_bf, b_glu.reshape(1, ssm_w), g_ssm.reshape(1, ssm_w), g_attn.reshape(1, aw),
      w_out_bf, mod3, g_ffn.reshape(1, d), mod3, mod3, rw_t, router_b.reshape(ne, 1))


def _segment_copies(n_seg, meta, make_copy):
    def body(e, _):
        src0, dst0, units = meta(e)
        off = jnp.int32(0)
        for sz in SEG_UNITS:
            take = (units & sz) != 0
            src = pl.multiple_of(src0 + off, SEG_ALIGN)
            dst = pl.multiple_of(dst0 + off, SEG_ALIGN)

            @pl.when(take)
            def _(src=src, dst=dst, sz=sz):
                make_copy(src, dst, sz * SEG_ALIGN).start()

            off = off + jnp.where(take, sz * SEG_ALIGN, 0)
        return 0

    lax.fori_loop(0, n_seg, body, 0)


def _local_slots(idx, lstart_col, ne):
    tb = idx.shape[1]
    sub = lax.broadcasted_iota(I32, (ne, tb), 0)
    hits = [idx[kk:kk + 1, :] == sub for kk in range(TOP_K)]
    onehot = jnp.zeros((ne, tb), F32)
    for h in hits:
        onehot = onehot + jnp.where(h, 1.0, 0.0)
    r = lax.broadcasted_iota(I32, (tb, tb), 0)
    c = lax.broadcasted_iota(I32, (tb, tb), 1)
    tri = jnp.where(r < c, 1.0, 0.0).astype(BF16)
    slot = _dot(onehot.astype(BF16), tri) + lstart_col
    return [jnp.sum(jnp.where(h, slot, 0.0), axis=0, keepdims=True) for h in hits]


def _dispatch_body(lstart_s, pcnt_s, base_s, idx_ref, lcol_ref, h_ref, xs_ref, ls_ref, sorted_ref, sem, *, ne):
    i = pl.program_id(0)
    nt = pl.num_programs(0)
    tb = h_ref.shape[0]
    lb = sorted_ref.shape[1]
    slot = lax.rem(i, 2)

    def tile_rows(t):
        return pl.multiple_of(lstart_s[t * ne + ne - 1] + pcnt_s[t * ne + ne - 1], SEG_ALIGN)

    def wait_tile(t, sl):
        done = xs_ref.at[pl.ds(0, tile_rows(t)), :]
        pltpu.make_async_copy(done, done, sem.at[sl]).wait()

    @pl.when(i >= 2)
    def _():
        wait_tile(i - 2, slot)

    slots = _local_slots(idx_ref[...], lcol_ref[0].astype(F32), ne)
    row_o = lax.broadcasted_iota(I32, ls_ref.shape, 0)
    ls_out = jnp.full(ls_ref.shape, -1.0, F32)
    for kk in range(TOP_K):
        ls_out = jnp.where(row_o == kk, slots[kk], ls_out)
    ls_ref[...] = ls_out.astype(I32)
    h = h_ref[...].astype(BF16)
    for r0 in range(0, lb, SORT_CHUNK):
        j = (lax.broadcasted_iota(I32, (SORT_CHUNK, tb), 0) + r0).astype(F32)
        pm = jnp.zeros((SORT_CHUNK, tb), F32)
        for kk in range(TOP_K):
            pm = jnp.where(j == slots[kk], 1.0, pm)
        sorted_ref[slot, r0:r0 + SORT_CHUNK, :] = _dot(pm.astype(BF16), h)

    def meta(e):
        return lstart_s[i * ne + e], base_s[i * ne + e], pcnt_s[i * ne + e] // SEG_ALIGN

    def make_copy(src, dst, rows):
        return pltpu.make_async_copy(sorted_ref.at[slot, pl.ds(src, rows), :], xs_ref.at[pl.ds(dst, rows), :],
                                     sem.at[slot])

    _segment_copies(ne, meta, make_copy)

    @pl.when(i == nt - 1)
    def _():
        @pl.when(i >= 1)
        def _():
            wait_tile(i - 1, 1 - slot)
        wait_tile(i, slot)


def _dispatch(h2, idx_t, plan, cap, tb):
    n, d = h2.shape
    lstart, pcnt, base = plan
    nt, ne = lstart.shape
    lb = tb * TOP_K + ne * SEG_ALIGN
    assert lb % SORT_CHUNK == 0
    return pl.pallas_call(
        functools.partial(_dispatch_body, ne=ne),
        grid_spec=pltpu.PrefetchScalarGridSpec(
            num_scalar_prefetch=3,
            grid=(nt,),
            in_specs=[pl.BlockSpec((TOP_K, tb), lambda i, *_: (0, i)),
                      pl.BlockSpec((1, ne, 1), lambda i, *_: (i, 0, 0)),
                      pl.BlockSpec((tb, d), lambda i, *_: (i, 0))],
            out_specs=[pl.BlockSpec(memory_space=pl.ANY),
                       pl.BlockSpec((WT_ROWS, tb), lambda i, *_: (0, i))],
            scratch_shapes=[pltpu.VMEM((2, lb, d), F32), pltpu.SemaphoreType.DMA((2,))],
        ),
        out_shape=[jax.ShapeDtypeStruct((cap, d), F32),
                   jax.ShapeDtypeStruct((WT_ROWS, n), I32)],
        compiler_params=_cparams("arbitrary"),
        name="dispatch",
    )(lstart.reshape(-1), pcnt.reshape(-1), base.reshape(-1), idx_t, lstart.reshape(nt, ne, 1), h2)


def _route_plan(tile_cnt, tg, cap):
    cnt = tile_cnt[:, :, 0]
    pcnt = (cnt + SEG_ALIGN - 1) // SEG_ALIGN * SEG_ALIGN
    lstart = jnp.cumsum(pcnt, axis=1) - pcnt
    etot = jnp.sum(pcnt, axis=0)
    eoff = jnp.cumsum(etot) - etot
    base = eoff[None, :] + jnp.cumsum(pcnt, axis=0) - pcnt
    items = _plan_items(etot, cap, tg)
    return (lstart.astype(I32), pcnt.astype(I32), base.astype(I32)), items


def _experts_body(tile_ref, exp_ref, lo_ref, hi_ref, first_ref, newe_ref,
                  xs_ref, wg_ref, bg_ref, wu_ref, bu_ref, wd_ref, bd_ref, ys_ref,
                  wg_bf, wu_bf, wd_bf):
    i = pl.program_id(0)
    lo = lo_ref[i]
    hi = hi_ref[i]

    @pl.when(newe_ref[i] == 1)
    def _cast():
        wg_bf[...] = wg_ref[0].astype(BF16)
        wu_bf[...] = wu_ref[0].astype(BF16)
        wd_bf[...] = wd_ref[0].astype(BF16)

    @pl.when(hi > lo)
    def _compute():
        x = xs_ref[...].astype(BF16)
        gate = jnp.minimum(_dot(x, wg_bf[...]) + bg_ref[0], SWIGLU_LIMIT)
        lin = jnp.clip(_dot(x, wu_bf[...]) + bu_ref[0], -SWIGLU_LIMIT, SWIGLU_LIMIT)
        act = gate * jax.nn.sigmoid(SWIGLU_ALPHA * gate) * (lin + 1.0)
        y = _dot(act.astype(BF16), wd_bf[...]) + bd_ref[0]
        row = lax.broadcasted_iota(I32, y.shape, 0)
        mine = (row >= lo) & (row < hi)

        @pl.when(first_ref[i] == 1)
        def _():
            ys_ref[...] = jnp.where(mine, y, 0.0)

        @pl.when(first_ref[i] == 0)
        def _():
            ys_ref[...] = jnp.where(mine, y, ys_ref[...])


def _experts(xs, items, w_gate, b_gate, w_up, b_up, w_down, b_down, tg):
    nk, d = xs.shape
    ne, _, dff = w_gate.shape
    n_items = items[0].shape[0]
    wspec = lambda shp: pl.BlockSpec((1,) + shp, lambda i, t, e, lo, hi, f, nw: (e[i], 0, 0))
    return pl.pallas_call(
        _experts_body,
        grid_spec=pltpu.PrefetchScalarGridSpec(
            num_scalar_prefetch=6,
            grid=(n_items,),
            in_specs=[pl.BlockSpec((tg, d), lambda i, t, e, lo, hi, f, nw: (t[i], 0)),
                      wspec((d, dff)), wspec((1, dff)),
                      wspec((d, dff)), wspec((1, dff)),
                      wspec((dff, d)), wspec((1, d))],
            out_specs=pl.BlockSpec((tg, d), lambda i, t, e, lo, hi, f, nw: (t[i], 0)),
            scratch_shapes=[pltpu.VMEM((d, dff), BF16), pltpu.VMEM((d, dff), BF16), pltpu.VMEM((dff, d), BF16)],
        ),
        out_shape=jax.ShapeDtypeStruct((nk, d), F32),
        compiler_params=_cparams("arbitrary"),
        name="experts",
    )(*items, xs, w_gate, b_gate.reshape(ne, 1, dff), w_up, b_up.reshape(ne, 1, dff),
      w_down, b_down.reshape(ne, 1, d))


def _plan_items(counts, nk, tg):
    ne = counts.shape[0]
    n_tiles = nk // tg
    n_items = n_tiles + ne
    ends = jnp.cumsum(counts)
    offs = ends - counts
    first_tile = offs // tg
    last_tile = jnp.maximum(ends - 1, 0) // tg
    per_e = jnp.where(counts > 0, last_tile - first_tile + 1, 0)
    item_end = jnp.cumsum(per_e)
    item_start = item_end - per_e
    total = item_end[-1]
    ii = jnp.arange(n_items, dtype=I32)
    first_after = lambda i: jnp.minimum(jnp.sum((item_end[None, :] <= i[:, None]).astype(I32), axis=1), ne - 1)
    e_of = first_after(ii)
    valid = ii < total
    last_e = first_after(total[None] - 1)[0]
    e_of = jnp.where(valid, e_of, last_e)
    last_tile = jnp.maximum(ends[-1] - 1, 0) // tg
    tile = jnp.where(valid, first_tile[e_of] + (ii - item_start[e_of]), last_tile).astype(I32)
    lo = jnp.clip(offs[e_of] - tile * tg, 0, tg)
    hi = jnp.clip(ends[e_of] - tile * tg, 0, tg)
    lo = jnp.where(valid, lo, 0).astype(I32)
    hi = jnp.where(valid, hi, 0).astype(I32)
    prev_tile = jnp.concatenate([jnp.full((1,), -1, I32), tile[:-1]])
    prev_e = jnp.concatenate([jnp.full((1,), -1, I32), e_of[:-1]])
    first = (tile != prev_tile).astype(I32)
    newe = (e_of != prev_e).astype(I32)
    return (tile, e_of, lo, hi, first, newe)


def _combine_body(lstart_s, pcnt_s, base_s, ls_ref, wt_ref, x1_ref, gf_ref, gfin_ref, sho_ref, sco_ref,
                  ys_ref, o_ref, buf, sem, *, ne, spb):
    i = pl.program_id(0) * spb + pl.program_id(1)
    nt = pl.num_programs(0) * spb
    tb = x1_ref.shape[1]
    lb = buf.shape[1]
    slot = lax.rem(i, 2)

    def fetch(t, sl):
        def meta(e):
            return base_s[t * ne + e], lstart_s[t * ne + e], pcnt_s[t * ne + e] // SEG_ALIGN

        def make_copy(src, dst, rows):
            return pltpu.make_async_copy(ys_ref.at[pl.ds(src, rows), :], buf.at[sl, pl.ds(dst, rows), :],
                                         sem.at[sl])

        _segment_copies(ne, meta, make_copy)

    @pl.when(i == 0)
    def _():
        buf[...] = jnp.zeros(buf.shape, F32)
        fetch(0, 0)

    @pl.when(i + 1 < nt)
    def _():
        fetch(i + 1, 1 - slot)

    total = pl.multiple_of(lstart_s[i * ne + ne - 1] + pcnt_s[i * ne + ne - 1], SEG_ALIGN)
    done = buf.at[slot, pl.ds(0, total), :]
    pltpu.make_async_copy(done, done, sem.at[slot]).wait()
    ls_col = ls_ref[...].astype(F32).T
    wt_col = wt_ref[...].T
    ffn = jnp.zeros((tb, x1_ref.shape[2]), F32)
    for r0 in range(0, lb, SORT_CHUNK):
        j = (lax.broadcasted_iota(I32, (tb, SORT_CHUNK), 1) + r0).astype(F32)
        wm = jnp.zeros((tb, SORT_CHUNK), F32)
        for kk in range(TOP_K):
            wm = jnp.where(j == ls_col[:, kk:kk + 1], wt_col[:, kk:kk + 1], wm)
        ffn = ffn + _dot(wm.astype(BF16), buf[slot, r0:r0 + SORT_CHUNK, :].astype(BF16))
    x2 = x1_ref[0] + gf_ref[0] * ffn
    o_ref[0] = _rms(x2, gfin_ref[...]) * (1.0 + sco_ref[0]) + sho_ref[0]


def _combine(plan, ls_t, wt_t, x1, mod3, g_final, fmod3, ys, tb):
    bsz, seq, d = x1.shape
    lstart, pcnt, base = plan
    nt, ne = lstart.shape
    spb = seq // tb
    lb = tb * TOP_K + ne * SEG_ALIGN
    return pl.pallas_call(
        functools.partial(_combine_body, ne=ne, spb=spb),
        grid_spec=pltpu.PrefetchScalarGridSpec(
            num_scalar_prefetch=3,
            grid=(bsz, spb),
            in_specs=[pl.BlockSpec((WT_ROWS, tb), lambda b, i, *_: (0, b * spb + i)),
                      pl.BlockSpec((WT_ROWS, tb), lambda b, i, *_: (0, b * spb + i)),
                      pl.BlockSpec((1, tb, d), lambda b, i, *_: (b, i, 0)),
                      pl.BlockSpec((1, 1, d), lambda b, i, *_: (b, 0, 5)),
                      pl.BlockSpec((1, d), lambda b, i, *_: (0, 0)),
                      pl.BlockSpec((1, 1, d), lambda b, i, *_: (b, 0, 0)),
                      pl.BlockSpec((1, 1, d), lambda b, i, *_: (b, 0, 1)),
                      pl.BlockSpec(memory_space=pl.ANY)],
            out_specs=pl.BlockSpec((1, tb, d), lambda b, i, *_: (b, i, 0)),
            scratch_shapes=[pltpu.VMEM((2, lb, d), F32), pltpu.SemaphoreType.DMA((2,))],
        ),
        out_shape=jax.ShapeDtypeStruct((bsz, seq, d), F32),
        compiler_params=_cparams("arbitrary", "arbitrary"),
        name="combine",
    )(lstart.reshape(-1), pcnt.reshape(-1), base.reshape(-1), ls_t, wt_t, x1, mod3, g_final.reshape(1, d),
      fmod3, fmod3, ys)


def kernel(x, c, positions, ada_w, ada_b, final_ada_w, final_ada_b, norm_mix_g, norm_ffn_g, norm_final_g, w_in, ssm_lambda_re, ssm_lambda_im, ssm_log_dt, ssm_b_re, ssm_b_im, ssm_c_re, ssm_c_im, ssm_d, ssm_w_glu, ssm_b_glu, out_norm_ssm_g, out_norm_attn_g, w_out, router_w, router_b, exp_w_gate, exp_b_gate, exp_w_up, exp_b_up, exp_w_down, exp_b_down):
    bsz, seq, d = x.shape
    depth = ada_w.shape[0]
    ssm_w = ssm_d.shape[-1]
    attn_w = (w_in.shape[-1] - ssm_w) // 3
    ne = router_w.shape[-1]
    n_tok = bsz * seq
    nc = seq // SSM_CHUNK
    tg = 256

    rope_tab, rope_exp = _rope_tables(positions)
    fmod3 = _adaln(c, final_ada_w, final_ada_b).reshape(bsz, 1, 2 * d)
    for l in range(depth):
        mod3 = _adaln(c, ada_w[l], ada_b[l]).reshape(bsz, 1, -1)
        u4, q, k, v = _inproj(x, norm_mix_g[l], mod3, w_in[l].astype(BF16), rope_tab, rope_exp, ssm_w, attn_w)
        tables = _s5_params(ssm_lambda_re[l], ssm_lambda_im[l], ssm_log_dt[l], ssm_b_re[l], ssm_b_im[l],
                            ssm_c_re[l], ssm_c_im[l], ssm_d[l], nc)
        nblk = ssm_w // LANES
        y4 = _s5(u4.reshape(nblk, bsz, nc, SSM_CHUNK * LANES), *tables).reshape(nblk, bsz, seq, LANES)
        attn = _attention(q, k, v)
        x1, h2, idx_t, wt_t, tile_cnt = _merge(y4, attn, x, ssm_w_glu[l].astype(BF16), ssm_b_glu[l], out_norm_ssm_g[l],
                                     out_norm_attn_g[l], w_out[l].astype(BF16), mod3, norm_ffn_g[l],
                                     router_w[l], router_b[l])
        tb = seq // (tile_cnt.shape[0] // bsz)
        cap = n_tok * TOP_K + tile_cnt.shape[0] * ne * SEG_ALIGN
        assert cap % tg == 0 and tb // SEG_ALIGN < 2 * SEG_UNITS[0]
        plan, items = _route_plan(tile_cnt, tg, cap)
        xs, ls_t = _dispatch(h2.reshape(n_tok, d), idx_t, plan, cap, tb)
        ys = _experts(xs, items, exp_w_gate[l], exp_b_gate[l], exp_w_up[l], exp_b_up[l],
                      exp_w_down[l], exp_b_down[l], tg)
        if l + 1 < depth:
            raise NotImplementedError("depth > 1 needs the non-final combine")
        x = _combine(plan, ls_t, wt_t, x1, mod3, norm_final_g, fmod3, ys, tb)
    return x
```

```python
import functools
import math

import jax
import jax.numpy as jnp
from jax import lax
from jax.experimental import pallas as pl
from jax.experimental.pallas import tpu as pltpu

F32 = jnp.float32
BF16 = jnp.bfloat16
I32 = jnp.int32
HIGHEST = lax.Precision.HIGHEST

LANES = 128
HEAD_DIM = 64
MOBA_BLOCK = 256
MOBA_TOPK = 3
ROT_DIM = HEAD_DIM // 4
ROPE_THETA = 500000.0
SSM_GROUP = 16
SSM_CHUNK = 16
TOP_K = 4
SWIGLU_ALPHA = 1.702
SWIGLU_LIMIT = 7.0
NORM_EPS = 1e-5
NEG_INF = -1e30
LOG2_E = math.log2(math.e)
KV_TILE = 2 * MOBA_BLOCK
PV_ROWS = HEAD_DIM + 8
WT_ROWS = 8
SEG_ALIGN = 8
SEG_UNITS = (64, 32, 16, 8, 4, 2, 1)
SORT_CHUNK = 256
VMEM_LIMIT = 56 * 1024 * 1024


def _cparams(*sem):
    return pltpu.CompilerParams(dimension_semantics=sem, vmem_limit_bytes=VMEM_LIMIT)


def _dot(a, b):
    return jnp.dot(a, b, preferred_element_type=F32)


def _rms(x, g):
    return x * lax.rsqrt(jnp.mean(x * x, axis=-1, keepdims=True) + NORM_EPS) * g


def _adaln_body(c_ref, w_ref, b_ref, o_ref):
    c = c_ref[...]
    ca = c * jax.nn.sigmoid(c)
    o_ref[...] = jnp.dot(ca, w_ref[...], preferred_element_type=F32, precision=HIGHEST) + b_ref[...]


def _adaln(c, w, b, tn=512):
    bsz, d = c.shape
    n = w.shape[1]
    return pl.pallas_call(
        _adaln_body,
        grid=(n // tn,),
        in_specs=[pl.BlockSpec((bsz, d), lambda j: (0, 0)),
                  pl.BlockSpec((d, tn), lambda j: (0, j)),
                  pl.BlockSpec((1, tn), lambda j: (0, j))],
        out_specs=pl.BlockSpec((bsz, tn), lambda j: (0, j)),
        out_shape=jax.ShapeDtypeStruct((bsz, n), F32),
        compiler_params=_cparams("parallel"),
        name="adaln",
    )(c, w, b.reshape(1, n))


def _inproj_body(x_ref, g_ref, sh_ref, sc_ref, w_ref, tab_ref, exp_ref, u_ref, q_ref, k_ref, v_ref):
    x = x_ref[0]
    tm = x.shape[0]
    h = _rms(x, g_ref[...]) * (1.0 + sc_ref[0]) + sh_ref[0]
    proj = _dot(h.astype(BF16), w_ref[...])
    ssm_w = u_ref.shape[0] * LANES
    aw = q_ref.shape[2]
    for j in range(u_ref.shape[0]):
        u_ref[j, 0] = proj[:, LANES * j:LANES * (j + 1)].astype(BF16)
    t_hi = tab_ref[0].astype(BF16)
    t_lo = (tab_ref[0] - t_hi.astype(F32)).astype(BF16)
    tn = (((0,), (0,)), ((), ()))
    tab = (lax.dot_general(t_hi, exp_ref[...], tn, preferred_element_type=F32)
           + lax.dot_general(t_lo, exp_ref[...], tn, preferred_element_type=F32))
    lane1 = lax.broadcasted_iota(I32, (tm, LANES), 1)
    cos1 = tab[:, :LANES] + jnp.where((lane1 & (HEAD_DIM - 1)) >= ROT_DIM, 1.0, 0.0)
    reps = aw // LANES
    cosf = jnp.concatenate([cos1] * reps, axis=1)
    sinf = jnp.concatenate([tab[:, LANES:]] * reps, axis=1)
    lane = lax.broadcasted_iota(I32, (tm, aw), 1)
    first_half = (lane & (HEAD_DIM - 1)) < (ROT_DIM // 2)

    def rope(t):
        partner = jnp.where(first_half, pltpu.roll(t, aw - ROT_DIM // 2, 1), pltpu.roll(t, ROT_DIM // 2, 1))
        return t * cosf + partner * sinf

    q = rope(proj[:, ssm_w:ssm_w + aw]) * (HEAD_DIM ** -0.5 * LOG2_E)
    k = rope(proj[:, ssm_w + aw:ssm_w + 2 * aw])
    q_ref[0] = q.astype(BF16)
    k_ref[0] = k.astype(BF16)
    v_ref[0] = proj[:, ssm_w + 2 * aw:ssm_w + 3 * aw].astype(BF16)


def _inproj(x, gain, mod3, w_in_bf, rope_tab, rope_exp, ssm_w, attn_w, tm=512):
    bsz, seq, d = x.shape
    n_u = ssm_w // LANES
    tm = min(tm, seq)
    return pl.pallas_call(
        _inproj_body,
        grid=(bsz, seq // tm),
        in_specs=[pl.BlockSpec((1, tm, d), lambda b, i: (b, i, 0)),
                  pl.BlockSpec((1, d), lambda b, i: (0, 0)),
                  pl.BlockSpec((1, 1, d), lambda b, i: (b, 0, 0)),
                  pl.BlockSpec((1, 1, d), lambda b, i: (b, 0, 1)),
                  pl.BlockSpec(w_in_bf.shape, lambda b, i: (0, 0)),
                  pl.BlockSpec((1, ROT_DIM, tm), lambda b, i: (b, 0, i)),
                  pl.BlockSpec(rope_exp.shape, lambda b, i: (0, 0))],
        out_specs=[pl.BlockSpec((n_u, 1, tm, LANES), lambda b, i: (0, b, i, 0)),
                   pl.BlockSpec((1, tm, attn_w), lambda b, i: (b, i, 0)),
                   pl.BlockSpec((1, tm, attn_w), lambda b, i: (b, i, 0)),
                   pl.BlockSpec((1, tm, attn_w), lambda b, i: (b, i, 0))],
        out_shape=[jax.ShapeDtypeStruct((n_u, bsz, seq, LANES), BF16),
                   jax.ShapeDtypeStruct((bsz, seq, attn_w), BF16),
                   jax.ShapeDtypeStruct((bsz, seq, attn_w), BF16),
                   jax.ShapeDtypeStruct((bsz, seq, attn_w), BF16)],
        compiler_params=_cparams("parallel", "parallel"),
        name="inproj",
    )(x, gain.reshape(1, d), mod3, mod3, w_in_bf, rope_tab, rope_exp)


def _rope_tables(positions):
    half = ROT_DIM // 2
    inv_freq = ROPE_THETA ** (-jnp.arange(0, ROT_DIM, 2, dtype=F32) / ROT_DIM)
    ang = positions.astype(F32)[:, None, :] * inv_freq[None, :, None]
    tab = jnp.concatenate([jnp.cos(ang), jnp.sin(ang)], axis=1)
    i = jnp.arange(ROT_DIM)[:, None]
    hl = jnp.arange(LANES)[None, :] % HEAD_DIM
    e_cos = jnp.where((i < half) & ((hl == i) | (hl == i + half)), 1.0, 0.0)
    e_sin = jnp.where(i >= half, jnp.where(hl == i - half, -1.0, jnp.where(hl == i, 1.0, 0.0)), 0.0)
    return tab, jnp.concatenate([e_cos, e_sin], axis=1).astype(BF16)


def _s5_params(lam_re, lam_im, log_dt, b_re, b_im, c_re, c_im, d_skip, n_chunks):
    g_all, p = lam_re.shape
    hc = b_re.shape[-1]
    t = SSM_CHUNK
    gpb = LANES // hc
    nblk = g_all // gpb
    lr, li = lam_re.astype(F32), lam_im.astype(F32)
    dt = jnp.exp(log_dt.astype(F32))[:, None]
    ldr, ldi = lr * dt, li * dt

    def apow(k):
        k = jnp.asarray(k, F32)[..., None, None]
        mag = jnp.exp(ldr * k)
        return mag * jnp.cos(ldi * k), mag * jnp.sin(ldi * k)

    ar, ai = apow(1.0)
    zr, zi = ar - 1.0, ai
    den = lr * lr + li * li
    cr = (zr * lr + zi * li) / den
    ci = (zi * lr - zr * li) / den
    bre, bim = b_re.astype(F32), b_im.astype(F32)
    bbr = cr[..., None] * bre - ci[..., None] * bim
    bbi = cr[..., None] * bim + ci[..., None] * bre
    cre, cim = c_re.astype(F32), c_im.astype(F32)
    eye = jnp.eye(gpb, dtype=F32)

    pr, pi = apow(jnp.arange(t))
    abr = pr[..., None] * bbr - pi[..., None] * bbi
    abi = pr[..., None] * bbi + pi[..., None] * bbr
    kf = (jnp.einsum('gnp,kgph->kghn', cre, abr, precision=HIGHEST)
          - jnp.einsum('gnp,kgph->kghn', cim, abi, precision=HIGHEST))
    kblk = jnp.einsum('kjghn,gf->kjghfn', kf.reshape(t, nblk, gpb, hc, hc), eye)
    kblk = kblk.reshape(t, nblk, LANES, LANES)
    kpad = jnp.concatenate([jnp.zeros_like(kblk[:1]), kblk], axis=0)
    pm = jnp.concatenate([kpad[0:t], kpad[1:t + 1]], axis=-1)
    kstack = jnp.transpose(pm[::-1], (1, 0, 2, 3)).reshape(nblk, t * LANES, 2 * LANES)

    def inject(ab):
        ab = ab[::-1].reshape(t, nblk, gpb, p, hc)
        return jnp.einsum('sjgph,gf->jsghfp', ab, eye).reshape(nblk, t * LANES, gpb * p)

    bm = jnp.concatenate([inject(abr), inject(abi)], axis=-1)

    qr, qi = apow(jnp.arange(t) + 1.0)
    c_from_re = cre[None] * qr[:, :, None, :] - cim[None] * qi[:, :, None, :]
    c_from_im = -cre[None] * qi[:, :, None, :] - cim[None] * qr[:, :, None, :]

    def readout(cc):
        cc = cc.reshape(t, nblk, gpb, hc, p)
        return jnp.einsum('tjgnp,gf->jgptfn', cc, eye).reshape(nblk, gpb * p, t * LANES)

    cm = jnp.concatenate([readout(c_from_re), readout(c_from_im)], axis=1)

    n_steps = max(1, int(math.log2(n_chunks)))
    shifts = [float(t * (1 << s)) for s in range(n_steps)]
    sr, si = apow(jnp.asarray(shifts))
    ap = jnp.concatenate([sr.reshape(n_steps, nblk, gpb * p), si.reshape(n_steps, nblk, gpb * p)], axis=-1)
    ap = jnp.transpose(ap, (1, 0, 2))
    dvec = jnp.tile(d_skip.astype(F32).reshape(nblk, 1, LANES), (1, 1, t))
    return kstack.astype(BF16), bm.astype(BF16), cm.astype(BF16), ap, dvec


def _s5_body(x_ref, ks_ref, bm_ref, cm_ref, ap_ref, d_ref, y_ref, h_ref):
    nc = x_ref.shape[2]
    t = SSM_CHUNK
    pad = h_ref.shape[0] - nc
    half = h_ref.shape[1] // 2
    x = x_ref[0, 0]
    h_ref[0:pad] = jnp.zeros((pad, h_ref.shape[1]), F32)
    h_ref[pad:pad + nc] = _dot(x, bm_ref[0])
    for step in range(ap_ref.shape[1]):
        d = 1 << step
        cur = h_ref[pad:pad + nc]
        sft = h_ref[pad - d:pad + nc - d]
        ar = ap_ref[0, step:step + 1, 0:half]
        ai = ap_ref[0, step:step + 1, half:]
        cr, ci = cur[:, :half], cur[:, half:]
        sr, si = sft[:, :half], sft[:, half:]
        h_ref[pad:pad + nc, 0:half] = cr + ar * sr - ai * si
        h_ref[pad:pad + nc, half:] = ci + ar * si + ai * sr
    h_prev = h_ref[pad - 1:pad + nc - 1]
    ys = _dot(h_prev.astype(BF16), cm_ref[0])
    for t0 in range(0, t, 2):
        lo, hi = t0 * LANES, (t0 + 2) * LANES
        conv = _dot(x[:, :hi], ks_ref[0, (t - 2 - t0) * LANES:, :])
        y = conv + ys[:, lo:hi] + d_ref[0][:, lo:hi] * x[:, lo:hi].astype(F32)
        y_ref[0, 0, :, lo:hi] = y.astype(y_ref.dtype)


def _s5(u4, kstack, bm, cm, ap, dvec):
    nblk, bsz, nc, w = u4.shape
    sw = bm.shape[-1]
    pad = max(nc // 2, 8)
    return pl.pallas_call(
        _s5_body,
        grid=(nblk, bsz),
        in_specs=[pl.BlockSpec((1, 1, nc, w), lambda j, b: (j, b, 0, 0)),
                  pl.BlockSpec((1,) + kstack.shape[1:], lambda j, b: (j, 0, 0)),
                  pl.BlockSpec((1,) + bm.shape[1:], lambda j, b: (j, 0, 0)),
                  pl.BlockSpec((1,) + cm.shape[1:], lambda j, b: (j, 0, 0)),
                  pl.BlockSpec((1,) + ap.shape[1:], lambda j, b: (j, 0, 0)),
                  pl.BlockSpec((1, 1, w), lambda j, b: (j, 0, 0))],
        out_specs=pl.BlockSpec((1, 1, nc, w), lambda j, b: (j, b, 0, 0)),
        out_shape=jax.ShapeDtypeStruct((nblk, bsz, nc, w), BF16),
        scratch_shapes=[pltpu.VMEM((pad + nc, sw), F32)],
        compiler_params=_cparams("parallel", "parallel"),
        name="s5",
    )(u4, kstack, bm, cm, ap, dvec)


def _attn_body(q_ref, k_ref, v_ref, o_ref, vt_ref, km_ref, bias_ref, sa_ref, sb_ref):
    qi = pl.program_id(2)
    seq = k_ref.shape[1]
    tq = q_ref.shape[1]
    nb = seq // MOBA_BLOCK
    n_kv = seq // KV_TILE
    bpt = KV_TILE // MOBA_BLOCK
    blk_shift = MOBA_BLOCK.bit_length() - 1

    @pl.when(qi == 0)
    def _build():
        k = k_ref[0].astype(F32)
        v = v_ref[0].astype(F32)
        ones = jnp.where(lax.broadcasted_iota(I32, (PV_ROWS - HEAD_DIM, KV_TILE), 0) == 0, 1.0, 0.0)
        for t in range(n_kv):
            v_t = v[t * KV_TILE:(t + 1) * KV_TILE].T
            vt_ref[0, t] = jnp.concatenate([v_t[:HEAD_DIM], ones], axis=0).astype(BF16)
            vt_ref[1, t] = jnp.concatenate([v_t[HEAD_DIM:], ones], axis=0).astype(BF16)
        km = jnp.mean(k.reshape(nb, MOBA_BLOCK, LANES), axis=1)
        lane_b = lax.broadcasted_iota(I32, (nb, LANES), 1)
        km_ref[0] = jnp.where(lane_b < HEAD_DIM, km, 0.0)
        km_ref[1] = jnp.where(lane_b >= HEAD_DIM, km, 0.0)

    q_t = q_ref[0].astype(F32).T
    feat = lax.broadcasted_iota(I32, (LANES, tq), 0)
    q_h = [jnp.where(feat < HEAD_DIM, q_t, 0.0).astype(BF16),
           jnp.where(feat >= HEAD_DIM, q_t, 0.0).astype(BF16)]

    jidx = lax.broadcasted_iota(I32, (nb, tq), 0)
    qblk = qi * (tq // MOBA_BLOCK) + lax.shift_right_logical(
        lax.broadcasted_iota(I32, (nb, tq), 1), blk_shift)
    for hx in range(2):
        g = _dot(km_ref[hx], q_t)
        cnt = jnp.zeros((nb, tq), F32)
        for jp in range(nb):
            row = g[jp:jp + 1, :]
            beats = jnp.where(row > g, 1.0, jnp.where(row == g, jnp.where(jp < jidx, 1.0, 0.0), 0.0))
            cnt = cnt + jnp.where(jp < qblk, beats, 0.0)
        past_sel = jnp.where(jidx < qblk, jnp.where(cnt < MOBA_TOPK, 1.0, 0.0), 0.0)
        sel = jnp.where(jidx == qblk, 1.0, past_sel)
        bias_ref[hx] = jnp.where(sel > 0.5, 0.0, NEG_INF)

    def raw_scores(dst_ref, tile):
        off = pl.multiple_of(tile * KV_TILE, KV_TILE)
        k_t = k_ref[0, pl.ds(off, KV_TILE), :]
        for hx in range(2):
            dst_ref[hx] = _dot(k_t, q_h[hx])

    def fold(carry, src_ref, tile, live=None, causal=None):
        out = []
        for hx in range(2):
            parts, biases = [], []
            for j in range(bpt):
                s_j = src_ref[hx, j * MOBA_BLOCK:(j + 1) * MOBA_BLOCK, :]
                if causal is not None:
                    s_j = jnp.where(causal[j], s_j, NEG_INF)
                b_j = bias_ref[hx, pl.ds(tile * bpt + j, 1), :]
                if live is not None:
                    b_j = jnp.where(live, b_j, NEG_INF)
                parts.append(s_j)
                biases.append(b_j)
            m_n = jnp.max(parts[0], axis=0, keepdims=True) + biases[0]
            for s_j, b_j in zip(parts[1:], biases[1:]):
                m_n = jnp.maximum(m_n, jnp.max(s_j, axis=0, keepdims=True) + b_j)
            if carry is None:
                p_t = jnp.concatenate([jnp.exp2(s_j + (b_j - m_n)) for s_j, b_j in zip(parts, biases)], axis=0)
                out += [m_n, _dot(vt_ref[hx, tile], p_t.astype(BF16))]
            else:
                m_c, acc_c = carry[2 * hx], carry[2 * hx + 1]
                m_n = jnp.maximum(m_c, m_n)
                p_t = jnp.concatenate([jnp.exp2(s_j + (b_j - m_n)) for s_j, b_j in zip(parts, biases)], axis=0)
                out += [m_n, acc_c * jnp.exp2(m_c - m_n) + _dot(vt_ref[hx, tile], p_t.astype(BF16))]
        return tuple(out)

    raw_scores(sa_ref, qi)
    raw_scores(sb_ref, 0)
    krow = lax.broadcasted_iota(I32, (MOBA_BLOCK, tq), 0)
    qcol = lax.broadcasted_iota(I32, (MOBA_BLOCK, tq), 1)
    state = fold(None, sa_ref, qi, causal=[krow + j * MOBA_BLOCK <= qcol for j in range(bpt)])

    def body(i, carry):
        t0 = 2 * i
        raw_scores(sa_ref, jnp.minimum(t0 + 1, n_kv - 1))
        carry = fold(carry, sb_ref, t0)
        raw_scores(sb_ref, jnp.minimum(t0 + 2, n_kv - 1))
        return fold(carry, sa_ref, t0 + 1, live=t0 + 1 < qi)

    _, acc_a, _, acc_b = lax.fori_loop(0, (qi + 1) // 2, body, state)
    o_t = jnp.concatenate([acc[:HEAD_DIM] / acc[HEAD_DIM:HEAD_DIM + 1] for acc in (acc_a, acc_b)], axis=0)
    o_ref[0] = o_t.T.astype(o_ref.dtype)


def _attention(q, k, v):
    bsz, seq, aw = q.shape
    npair = aw // LANES
    tq = KV_TILE
    nb = seq // MOBA_BLOCK
    assert seq % KV_TILE == 0 and nb % 8 == 0
    return pl.pallas_call(
        _attn_body,
        grid=(bsz, npair, seq // tq),
        in_specs=[pl.BlockSpec((1, tq, LANES), lambda b, p, i: (b, i, p)),
                  pl.BlockSpec((1, seq, LANES), lambda b, p, i: (b, 0, p)),
                  pl.BlockSpec((1, seq, LANES), lambda b, p, i: (b, 0, p))],
        out_specs=pl.BlockSpec((1, tq, LANES), lambda b, p, i: (b, i, p)),
        out_shape=jax.ShapeDtypeStruct((bsz, seq, aw), BF16),
        scratch_shapes=[pltpu.VMEM((2, seq // KV_TILE, PV_ROWS, KV_TILE), BF16),
                        pltpu.VMEM((2, nb, LANES), F32),
                        pltpu.VMEM((2, nb, tq), F32),
                        pltpu.VMEM((2, KV_TILE, tq), F32),
                        pltpu.VMEM((2, KV_TILE, tq), F32)],
        compiler_params=_cparams("parallel", "parallel", "arbitrary"),
        name="attn",
    )(q, k, v)


def _merge_body(y_ref, a_ref, x_ref, wglu_ref, bglu_ref, gs_ref, ga_ref, wout_ref, gm_ref,
                gf_ref, shf_ref, scf_ref, rw_ref, rb_ref, x1_ref, h2_ref, idx_ref, wt_ref, cnt_ref):
    nblk = y_ref.shape[0]
    y = jnp.concatenate([y_ref[j, 0] for j in range(nblk)], axis=1).astype(F32)
    tm = y.shape[0]
    g = 0.5 * y * (1.0 + lax.erf(y * (2.0 ** -0.5)))
    glu = g * jax.nn.sigmoid(_dot(g.astype(BF16), wglu_ref[...]) + bglu_ref[...])
    ssm_n = _rms(glu, gs_ref[...])
    att_n = _rms(a_ref[0].astype(F32), ga_ref[...])
    merged = jnp.concatenate([ssm_n, att_n], axis=1).astype(BF16)
    mix = _dot(merged, wout_ref[...])
    x1 = x_ref[0] + gm_ref[0] * mix
    x1_ref[0] = x1
    h2 = _rms(x1, gf_ref[...]) * (1.0 + scf_ref[0]) + shf_ref[0]
    h2_ref[0] = h2
    logits = lax.dot_general(rw_ref[...], h2, (((1,), (1,)), ((), ())),
                             preferred_element_type=F32, precision=HIGHEST) + rb_ref[...]
    ne = logits.shape[0]
    sub = lax.broadcasted_iota(I32, (ne, tm), 0)
    vals, idxs = [], []
    for _ in range(TOP_K):
        mx = jnp.max(logits, axis=0, keepdims=True)
        ix = jnp.min(jnp.where(logits == mx, sub, ne), axis=0, keepdims=True)
        vals.append(mx)
        idxs.append(ix)
        logits = jnp.where(sub == ix, -jnp.inf, logits)
    exps = [jnp.exp(vv - vals[0]) for vv in vals]
    tot = exps[0]
    for e in exps[1:]:
        tot = tot + e
    row_i = lax.broadcasted_iota(I32, idx_ref.shape, 0)
    row_w = lax.broadcasted_iota(I32, wt_ref.shape, 0)
    idx_out = jnp.zeros(idx_ref.shape, I32)
    wt_out = jnp.zeros(wt_ref.shape, F32)
    for kk in range(TOP_K):
        idx_out = jnp.where(row_i == kk, idxs[kk], idx_out)
        wt_out = jnp.where(row_w == kk, exps[kk] / tot, wt_out)
    idx_ref[...] = idx_out
    wt_ref[...] = wt_out
    chosen = jnp.zeros((ne, tm), F32)
    for kk in range(TOP_K):
        chosen = chosen + jnp.where(sub == idxs[kk], 1.0, 0.0)
    cnt_ref[0] = jnp.sum(chosen, axis=1, keepdims=True).astype(I32)


def _merge(y4, attn, x, w_glu_bf, b_glu, g_ssm, g_attn, w_out_bf, mod3, g_ffn, router_w, router_b, tm=512):
    bsz, seq, d = x.shape
    nblk = y4.shape[0]
    ssm_w = nblk * LANES
    aw = attn.shape[-1]
    ne = router_w.shape[1]
    tm = min(tm, seq)
    spb = seq // tm
    n_tok = bsz * seq
    rw_t = router_w.T
    row = lambda n: pl.BlockSpec((1, n), lambda b, i: (0, 0))
    full = lambda a: pl.BlockSpec(a.shape, lambda b, i: (0,) * a.ndim)
    modv = lambda j: pl.BlockSpec((1, 1, d), lambda b, i: (b, 0, j))
    return pl.pallas_call(
        _merge_body,
        grid=(bsz, seq // tm),
        in_specs=[pl.BlockSpec((nblk, 1, tm, LANES), lambda b, i: (0, b, i, 0)),
                  pl.BlockSpec((1, tm, aw), lambda b, i: (b, i, 0)),
                  pl.BlockSpec((1, tm, d), lambda b, i: (b, i, 0)),
                  full(w_glu_bf), row(ssm_w), row(ssm_w), row(aw), full(w_out_bf),
                  modv(2), row(d), modv(3), modv(4), full(rw_t),
                  pl.BlockSpec((ne, 1), lambda b, i: (0, 0))],
        out_specs=[pl.BlockSpec((1, tm, d), lambda b, i: (b, i, 0)),
                   pl.BlockSpec((1, tm, d), lambda b, i: (b, i, 0)),
                   pl.BlockSpec((TOP_K, tm), lambda b, i: (0, b * spb + i)),
                   pl.BlockSpec((WT_ROWS, tm), lambda b, i: (0, b * spb + i)),
                   pl.BlockSpec((1, ne, 1), lambda b, i: (b * spb + i, 0, 0))],
        out_shape=[jax.ShapeDtypeStruct((bsz, seq, d), F32),
                   jax.ShapeDtypeStruct((bsz, seq, d), F32),
                   jax.ShapeDtypeStruct((TOP_K, n_tok), I32),
                   jax.ShapeDtypeStruct((WT_ROWS, n_tok), F32),
                   jax.ShapeDtypeStruct((bsz * spb, ne, 1), I32)],
        compiler_params=_cparams("parallel", "parallel"),
        name="merge",
    )(y4, attn, x, w_glu_bf, b_glu.reshape(1, ssm_w), g_ssm.reshape(1, ssm_w), g_attn.reshape(1, aw),
      w_out_bf, mod3, g_ffn.reshape(1, d), mod3, mod3, rw_t, router_b.reshape(ne, 1))


def _segment_copies(n_seg, meta, make_copy):
    def body(e, _):
        src0, dst0, units = meta(e)
        off = jnp.int32(0)
        for sz in SEG_UNITS:
            take = (units & sz) != 0
            src = pl.multiple_of(src0 + off, SEG_ALIGN)
            dst = pl.multiple_of(dst0 + off, SEG_ALIGN)

            @pl.when(take)
            def _(src=src, dst=dst, sz=sz):
                make_copy(src, dst, sz * SEG_ALIGN).start()

            off = off + jnp.where(take, sz * SEG_ALIGN, 0)
        return 0

    lax.fori_loop(0, n_seg, body, 0)


def _local_slots(idx, lstart_col, ne):
    tb = idx.shape[1]
    sub = lax.broadcasted_iota(I32, (ne, tb), 0)
    hits = [idx[kk:kk + 1, :] == sub for kk in range(TOP_K)]
    onehot = jnp.zeros((ne, tb), F32)
    for h in hits:
        onehot = onehot + jnp.where(h, 1.0, 0.0)
    r = lax.broadcasted_iota(I32, (tb, tb), 0)
    c = lax.broadcasted_iota(I32, (tb, tb), 1)
    tri = jnp.where(r < c, 1.0, 0.0).astype(BF16)
    slot = _dot(onehot.astype(BF16), tri) + lstart_col
    return [jnp.sum(jnp.where(h, slot, 0.0), axis=0, keepdims=True) for h in hits]


def _dispatch_body(lstart_s, pcnt_s, base_s, idx_ref, lcol_ref, h_ref, xs_ref, ls_ref, sorted_ref, sem, *, ne):
    i = pl.program_id(0)
    nt = pl.num_programs(0)
    tb = h_ref.shape[0]
    lb = sorted_ref.shape[1]
    slot = lax.rem(i, 2)

    def tile_rows(t):
        return pl.multiple_of(lstart_s[t * ne + ne - 1] + pcnt_s[t * ne + ne - 1], SEG_ALIGN)

    def wait_tile(t, sl):
        done = xs_ref.at[pl.ds(0, tile_rows(t)), :]
        pltpu.make_async_copy(done, done, sem.at[sl]).wait()

    @pl.when(i >= 2)
    def _():
        wait_tile(i - 2, slot)

    slots = _local_slots(idx_ref[...], lcol_ref[0].astype(F32), ne)
    row_o = lax.broadcasted_iota(I32, ls_ref.shape, 0)
    ls_out = jnp.full(ls_ref.shape, -1.0, F32)
    for kk in range(TOP_K):
        ls_out = jnp.where(row_o == kk, slots[kk], ls_out)
    ls_ref[...] = ls_out.astype(I32)
    h = h_ref[...].astype(BF16)
    for r0 in range(0, lb, SORT_CHUNK):
        j = (lax.broadcasted_iota(I32, (SORT_CHUNK, tb), 0) + r0).astype(F32)
        pm = jnp.zeros((SORT_CHUNK, tb), F32)
        for kk in range(TOP_K):
            pm = jnp.where(j == slots[kk], 1.0, pm)
        sorted_ref[slot, r0:r0 + SORT_CHUNK, :] = _dot(pm.astype(BF16), h)

    def meta(e):
        return lstart_s[i * ne + e], base_s[i * ne + e], pcnt_s[i * ne + e] // SEG_ALIGN

    def make_copy(src, dst, rows):
        return pltpu.make_async_copy(sorted_ref.at[slot, pl.ds(src, rows), :], xs_ref.at[pl.ds(dst, rows), :],
                                     sem.at[slot])

    _segment_copies(ne, meta, make_copy)

    @pl.when(i == nt - 1)
    def _():
        @pl.when(i >= 1)
        def _():
            wait_tile(i - 1, 1 - slot)
        wait_tile(i, slot)


def _dispatch(h2, idx_t, plan, cap, tb):
    n, d = h2.shape
    lstart, pcnt, base = plan
    nt, ne = lstart.shape
    lb = tb * TOP_K + ne * SEG_ALIGN
    assert lb % SORT_CHUNK == 0
    return pl.pallas_call(
        functools.partial(_dispatch_body, ne=ne),
        grid_spec=pltpu.PrefetchScalarGridSpec(
            num_scalar_prefetch=3,
            grid=(nt,),
            in_specs=[pl.BlockSpec((TOP_K, tb), lambda i, *_: (0, i)),
                      pl.BlockSpec((1, ne, 1), lambda i, *_: (i, 0, 0)),
                      pl.BlockSpec((tb, d), lambda i, *_: (i, 0))],
            out_specs=[pl.BlockSpec(memory_space=pl.ANY),
                       pl.BlockSpec((WT_ROWS, tb), lambda i, *_: (0, i))],
            scratch_shapes=[pltpu.VMEM((2, lb, d), F32), pltpu.SemaphoreType.DMA((2,))],
        ),
        out_shape=[jax.ShapeDtypeStruct((cap, d), F32),
                   jax.ShapeDtypeStruct((WT_ROWS, n), I32)],
        compiler_params=_cparams("arbitrary"),
        name="dispatch",
    )(lstart.reshape(-1), pcnt.reshape(-1), base.reshape(-1), idx_t, lstart.reshape(nt, ne, 1), h2)


def _route_plan(tile_cnt, tg, cap):
    cnt = tile_cnt[:, :, 0]
    pcnt = (cnt + SEG_ALIGN - 1) // SEG_ALIGN * SEG_ALIGN
    lstart = jnp.cumsum(pcnt, axis=1) - pcnt
    etot = jnp.sum(pcnt, axis=0)
    eoff = jnp.cumsum(etot) - etot
    base = eoff[None, :] + jnp.cumsum(pcnt, axis=0) - pcnt
    items = _plan_items(etot, cap, tg)
    return (lstart.astype(I32), pcnt.astype(I32), base.astype(I32)), items


def _experts_body(tile_ref, exp_ref, lo_ref, hi_ref, first_ref, newe_ref,
                  xs_ref, wg_ref, bg_ref, wu_ref, bu_ref, wd_ref, bd_ref, ys_ref,
                  wg_bf, wu_bf, wd_bf):
    i = pl.program_id(0)
    lo = lo_ref[i]
    hi = hi_ref[i]

    @pl.when(newe_ref[i] == 1)
    def _cast():
        wg_bf[...] = wg_ref[0].astype(BF16)
        wu_bf[...] = wu_ref[0].astype(BF16)
        wd_bf[...] = wd_ref[0].astype(BF16)

    @pl.when(hi > lo)
    def _compute():
        x = xs_ref[...].astype(BF16)
        gate = jnp.minimum(_dot(x, wg_bf[...]) + bg_ref[0], SWIGLU_LIMIT)
        lin = jnp.clip(_dot(x, wu_bf[...]) + bu_ref[0], -SWIGLU_LIMIT, SWIGLU_LIMIT)
        act = gate * jax.nn.sigmoid(SWIGLU_ALPHA * gate) * (lin + 1.0)
        y = _dot(act.astype(BF16), wd_bf[...]) + bd_ref[0]
        row = lax.broadcasted_iota(I32, y.shape, 0)
        mine = (row >= lo) & (row < hi)

        @pl.when(first_ref[i] == 1)
        def _():
            ys_ref[...] = jnp.where(mine, y, 0.0)

        @pl.when(first_ref[i] == 0)
        def _():
            ys_ref[...] = jnp.where(mine, y, ys_ref[...])


def _experts(xs, items, w_gate, b_gate, w_up, b_up, w_down, b_down, tg):
    nk, d = xs.shape
    ne, _, dff = w_gate.shape
    n_items = items[0].shape[0]
    wspec = lambda shp: pl.BlockSpec((1,) + shp, lambda i, t, e, lo, hi, f, nw: (e[i], 0, 0))
    return pl.pallas_call(
        _experts_body,
        grid_spec=pltpu.PrefetchScalarGridSpec(
            num_scalar_prefetch=6,
            grid=(n_items,),
            in_specs=[pl.BlockSpec((tg, d), lambda i, t, e, lo, hi, f, nw: (t[i], 0)),
                      wspec((d, dff)), wspec((1, dff)),
                      wspec((d, dff)), wspec((1, dff)),
                      wspec((dff, d)), wspec((1, d))],
            out_specs=pl.BlockSpec((tg, d), lambda i, t, e, lo, hi, f, nw: (t[i], 0)),
            scratch_shapes=[pltpu.VMEM((d, dff), BF16), pltpu.VMEM((d, dff), BF16), pltpu.VMEM((dff, d), BF16)],
        ),
        out_shape=jax.ShapeDtypeStruct((nk, d), F32),
        compiler_params=_cparams("arbitrary"),
        name="experts",
    )(*items, xs, w_gate, b_gate.reshape(ne, 1, dff), w_up, b_up.reshape(ne, 1, dff),
      w_down, b_down.reshape(ne, 1, d))


def _plan_items(counts, nk, tg):
    ne = counts.shape[0]
    n_tiles = nk // tg
    n_items = n_tiles + ne
    ends = jnp.cumsum(counts)
    offs = ends - counts
    first_tile = offs // tg
    last_tile = jnp.maximum(ends - 1, 0) // tg
    per_e = jnp.where(counts > 0, last_tile - first_tile + 1, 0)
    item_end = jnp.cumsum(per_e)
    item_start = item_end - per_e
    total = item_end[-1]
    ii = jnp.arange(n_items, dtype=I32)
    first_after = lambda i: jnp.minimum(jnp.sum((item_end[None, :] <= i[:, None]).astype(I32), axis=1), ne - 1)
    e_of = first_after(ii)
    valid = ii < total
    last_e = first_after(total[None] - 1)[0]
    e_of = jnp.where(valid, e_of, last_e)
    last_tile = jnp.maximum(ends[-1] - 1, 0) // tg
    tile = jnp.where(valid, first_tile[e_of] + (ii - item_start[e_of]), last_tile).astype(I32)
    lo = jnp.clip(offs[e_of] - tile * tg, 0, tg)
    hi = jnp.clip(ends[e_of] - tile * tg, 0, tg)
    lo = jnp.where(valid, lo, 0).astype(I32)
    hi = jnp.where(valid, hi, 0).astype(I32)
    prev_tile = jnp.concatenate([jnp.full((1,), -1, I32), tile[:-1]])
    prev_e = jnp.concatenate([jnp.full((1,), -1, I32), e_of[:-1]])
    first = (tile != prev_tile).astype(I32)
    newe = (e_of != prev_e).astype(I32)
    return (tile, e_of, lo, hi, first, newe)


def _combine_body(lstart_s, pcnt_s, base_s, ls_ref, wt_ref, x1_ref, gf_ref, gfin_ref, sho_ref, sco_ref,
                  ys_ref, o_ref, buf, sem, *, ne, spb):
    i = pl.program_id(0) * spb + pl.program_id(1)
    nt = pl.num_programs(0) * spb
    tb = x1_ref.shape[1]
    lb = buf.shape[1]
    slot = lax.rem(i, 2)

    def fetch(t, sl):
        def meta(e):
            return base_s[t * ne + e], lstart_s[t * ne + e], pcnt_s[t * ne + e] // SEG_ALIGN

        def make_copy(src, dst, rows):
            return pltpu.make_async_copy(ys_ref.at[pl.ds(src, rows), :], buf.at[sl, pl.ds(dst, rows), :],
                                         sem.at[sl])

        _segment_copies(ne, meta, make_copy)

    @pl.when(i == 0)
    def _():
        buf[...] = jnp.zeros(buf.shape, F32)
        fetch(0, 0)

    @pl.when(i + 1 < nt)
    def _():
        fetch(i + 1, 1 - slot)

    total = pl.multiple_of(lstart_s[i * ne + ne - 1] + pcnt_s[i * ne + ne - 1], SEG_ALIGN)
    done = buf.at[slot, pl.ds(0, total), :]
    pltpu.make_async_copy(done, done, sem.at[slot]).wait()
    ls_col = ls_ref[...].astype(F32).T
    wt_col = wt_ref[...].T
    ffn = jnp.zeros((tb, x1_ref.shape[2]), F32)
    for r0 in range(0, lb, SORT_CHUNK):
        j = (lax.broadcasted_iota(I32, (tb, SORT_CHUNK), 1) + r0).astype(F32)
        wm = jnp.zeros((tb, SORT_CHUNK), F32)
        for kk in range(TOP_K):
            wm = jnp.where(j == ls_col[:, kk:kk + 1], wt_col[:, kk:kk + 1], wm)
        ffn = ffn + _dot(wm.astype(BF16), buf[slot, r0:r0 + SORT_CHUNK, :].astype(BF16))
    x2 = x1_ref[0] + gf_ref[0] * ffn
    o_ref[0] = _rms(x2, gfin_ref[...]) * (1.0 + sco_ref[0]) + sho_ref[0]


def _combine(plan, ls_t, wt_t, x1, mod3, g_final, fmod3, ys, tb):
    bsz, seq, d = x1.shape
    lstart, pcnt, base = plan
    nt, ne = lstart.shape
    spb = seq // tb
    lb = tb * TOP_K + ne * SEG_ALIGN
    return pl.pallas_call(
        functools.partial(_combine_body, ne=ne, spb=spb),
        grid_spec=pltpu.PrefetchScalarGridSpec(
            num_scalar_prefetch=3,
            grid=(bsz, spb),
            in_specs=[pl.BlockSpec((WT_ROWS, tb), lambda b, i, *_: (0, b * spb + i)),
                      pl.BlockSpec((WT_ROWS, tb), lambda b, i, *_: (0, b * spb + i)),
                      pl.BlockSpec((1, tb, d), lambda b, i, *_: (b, i, 0)),
                      pl.BlockSpec((1, 1, d), lambda b, i, *_: (b, 0, 5)),
                      pl.BlockSpec((1, d), lambda b, i, *_: (0, 0)),
                      pl.BlockSpec((1, 1, d), lambda b, i, *_: (b, 0, 0)),
                      pl.BlockSpec((1, 1, d), lambda b, i, *_: (b, 0, 1)),
                      pl.BlockSpec(memory_space=pl.ANY)],
            out_specs=pl.BlockSpec((1, tb, d), lambda b, i, *_: (b, i, 0)),
            scratch_shapes=[pltpu.VMEM((2, lb, d), F32), pltpu.SemaphoreType.DMA((2,))],
        ),
        out_shape=jax.ShapeDtypeStruct((bsz, seq, d), F32),
        compiler_params=_cparams("arbitrary", "arbitrary"),
        name="combine",
    )(lstart.reshape(-1), pcnt.reshape(-1), base.reshape(-1), ls_t, wt_t, x1, mod3, g_final.reshape(1, d),
      fmod3, fmod3, ys)


def kernel(x, c, positions, ada_w, ada_b, final_ada_w, final_ada_b, norm_mix_g, norm_ffn_g, norm_final_g, w_in, ssm_lambda_re, ssm_lambda_im, ssm_log_dt, ssm_b_re, ssm_b_im, ssm_c_re, ssm_c_im, ssm_d, ssm_w_glu, ssm_b_glu, out_norm_ssm_g, out_norm_attn_g, w_out, router_w, router_b, exp_w_gate, exp_b_gate, exp_w_up, exp_b_up, exp_w_down, exp_b_down):
    bsz, seq, d = x.shape
    depth = ada_w.shape[0]
    ssm_w = ssm_d.shape[-1]
    attn_w = (w_in.shape[-1] - ssm_w) // 3
    ne = router_w.shape[-1]
    n_tok = bsz * seq
    nc = seq // SSM_CHUNK
    tg = 512

    rope_tab, rope_exp = _rope_tables(positions)
    fmod3 = _adaln(c, final_ada_w, final_ada_b).reshape(bsz, 1, 2 * d)
    for l in range(depth):
        mod3 = _adaln(c, ada_w[l], ada_b[l]).reshape(bsz, 1, -1)
        u4, q, k, v = _inproj(x, norm_mix_g[l], mod3, w_in[l].astype(BF16), rope_tab, rope_exp, ssm_w, attn_w)
        tables = _s5_params(ssm_lambda_re[l], ssm_lambda_im[l], ssm_log_dt[l], ssm_b_re[l], ssm_b_im[l],
                            ssm_c_re[l], ssm_c_im[l], ssm_d[l], nc)
        nblk = ssm_w // LANES
        y4 = _s5(u4.reshape(nblk, bsz, nc, SSM_CHUNK * LANES), *tables).reshape(nblk, bsz, seq, LANES)
        attn = _attention(q, k, v)
        x1, h2, idx_t, wt_t, tile_cnt = _merge(y4, attn, x, ssm_w_glu[l].astype(BF16), ssm_b_glu[l], out_norm_ssm_g[l],
                                     out_norm_attn_g[l], w_out[l].astype(BF16), mod3, norm_ffn_g[l],
                                     router_w[l], router_b[l])
        tb = seq // (tile_cnt.shape[0] // bsz)
        cap = n_tok * TOP_K + tile_cnt.shape[0] * ne * SEG_ALIGN
        assert cap % tg == 0 and tb // SEG_ALIGN < 2 * SEG_UNITS[0]
        plan, items = _route_plan(tile_cnt, tg, cap)
        xs, ls_t = _dispatch(h2.reshape(n_tok, d), idx_t, plan, cap, tb)
        ys = _experts(xs, items, exp_w_gate[l], exp_b_gate[l], exp_w_up[l], exp_b_up[l],
                      exp_w_down[l], exp_b_down[l], tg)
        if l + 1 < depth:
            raise NotImplementedError("depth > 1 needs the non-final combine")
        x = _combine(plan, ls_t, wt_t, x1, mod3, norm_final_g, fmod3, ys, tb)
    return x
```

```python
import functools
import math

import jax
import jax.numpy as jnp
from jax import lax
from jax.experimental import pallas as pl
from jax.experimental.pallas import tpu as pltpu

F32 = jnp.float32
BF16 = jnp.bfloat16
I32 = jnp.int32
HIGHEST = lax.Precision.HIGHEST

LANES = 128
HEAD_DIM = 64
MOBA_BLOCK = 256
MOBA_TOPK = 3
ROT_DIM = HEAD_DIM // 4
ROPE_THETA = 500000.0
SSM_GROUP = 16
SSM_CHUNK = 16
TOP_K = 4
SWIGLU_ALPHA = 1.702
SWIGLU_LIMIT = 7.0
NORM_EPS = 1e-5
NEG_INF = -1e30
LOG2_E = math.log2(math.e)
KV_TILE = 2 * MOBA_BLOCK
PV_ROWS = HEAD_DIM + 8
WT_ROWS = 8
SEG_ALIGN = 8
SEG_UNITS = (64, 32, 16, 8, 4, 2, 1)
SORT_CHUNK = 256
VMEM_LIMIT = 56 * 1024 * 1024


def _cparams(*sem):
    return pltpu.CompilerParams(dimension_semantics=sem, vmem_limit_bytes=VMEM_LIMIT)


def _dot(a, b):
    return jnp.dot(a, b, preferred_element_type=F32)


def _rms(x, g):
    return x * lax.rsqrt(jnp.mean(x * x, axis=-1, keepdims=True) + NORM_EPS) * g


def _adaln_body(c_ref, w_ref, b_ref, o_ref):
    c = c_ref[...]
    ca = c * jax.nn.sigmoid(c)
    o_ref[...] = jnp.dot(ca, w_ref[...], preferred_element_type=F32, precision=HIGHEST) + b_ref[...]


def _adaln(c, w, b, tn=512):
    bsz, d = c.shape
    n = w.shape[1]
    return pl.pallas_call(
        _adaln_body,
        grid=(n // tn,),
        in_specs=[pl.BlockSpec((bsz, d), lambda j: (0, 0)),
                  pl.BlockSpec((d, tn), lambda j: (0, j)),
                  pl.BlockSpec((1, tn), lambda j: (0, j))],
        out_specs=pl.BlockSpec((bsz, tn), lambda j: (0, j)),
        out_shape=jax.ShapeDtypeStruct((bsz, n), F32),
        compiler_params=_cparams("parallel"),
        name="adaln",
    )(c, w, b.reshape(1, n))


def _inproj_body(x_ref, g_ref, sh_ref, sc_ref, w_ref, tab_ref, exp_ref, u_ref, q_ref, k_ref, v_ref):
    x = x_ref[0]
    tm = x.shape[0]
    h = _rms(x, g_ref[...]) * (1.0 + sc_ref[0]) + sh_ref[0]
    proj = _dot(h.astype(BF16), w_ref[...])
    ssm_w = u_ref.shape[0] * LANES
    aw = q_ref.shape[2]
    for j in range(u_ref.shape[0]):
        u_ref[j, 0] = proj[:, LANES * j:LANES * (j + 1)].astype(BF16)
    t_hi = tab_ref[0].astype(BF16)
    t_lo = (tab_ref[0] - t_hi.astype(F32)).astype(BF16)
    tn = (((0,), (0,)), ((), ()))
    tab = (lax.dot_general(t_hi, exp_ref[...], tn, preferred_element_type=F32)
           + lax.dot_general(t_lo, exp_ref[...], tn, preferred_element_type=F32))
    lane1 = lax.broadcasted_iota(I32, (tm, LANES), 1)
    cos1 = tab[:, :LANES] + jnp.where((lane1 & (HEAD_DIM - 1)) >= ROT_DIM, 1.0, 0.0)
    reps = aw // LANES
    cosf = jnp.concatenate([cos1] * reps, axis=1)
    sinf = jnp.concatenate([tab[:, LANES:]] * reps, axis=1)
    lane = lax.broadcasted_iota(I32, (tm, aw), 1)
    first_half = (lane & (HEAD_DIM - 1)) < (ROT_DIM // 2)

    def rope(t):
        partner = jnp.where(first_half, pltpu.roll(t, aw - ROT_DIM // 2, 1), pltpu.roll(t, ROT_DIM // 2, 1))
        return t * cosf + partner * sinf

    q = rope(proj[:, ssm_w:ssm_w + aw]) * (HEAD_DIM ** -0.5 * LOG2_E)
    k = rope(proj[:, ssm_w + aw:ssm_w + 2 * aw])
    q_ref[0] = q.astype(BF16)
    k_ref[0] = k.astype(BF16)
    v_ref[0] = proj[:, ssm_w + 2 * aw:ssm_w + 3 * aw].astype(BF16)


def _inproj(x, gain, mod3, w_in_bf, rope_tab, rope_exp, ssm_w, attn_w, tm=512):
    bsz, seq, d = x.shape
    n_u = ssm_w // LANES
    tm = min(tm, seq)
    return pl.pallas_call(
        _inproj_body,
        grid=(bsz, seq // tm),
        in_specs=[pl.BlockSpec((1, tm, d), lambda b, i: (b, i, 0)),
                  pl.BlockSpec((1, d), lambda b, i: (0, 0)),
                  pl.BlockSpec((1, 1, d), lambda b, i: (b, 0, 0)),
                  pl.BlockSpec((1, 1, d), lambda b, i: (b, 0, 1)),
                  pl.BlockSpec(w_in_bf.shape, lambda b, i: (0, 0)),
                  pl.BlockSpec((1, ROT_DIM, tm), lambda b, i: (b, 0, i)),
                  pl.BlockSpec(rope_exp.shape, lambda b, i: (0, 0))],
        out_specs=[pl.BlockSpec((n_u, 1, tm, LANES), lambda b, i: (0, b, i, 0)),
                   pl.BlockSpec((1, tm, attn_w), lambda b, i: (b, i, 0)),
                   pl.BlockSpec((1, tm, attn_w), lambda b, i: (b, i, 0)),
                   pl.BlockSpec((1, tm, attn_w), lambda b, i: (b, i, 0))],
        out_shape=[jax.ShapeDtypeStruct((n_u, bsz, seq, LANES), BF16),
                   jax.ShapeDtypeStruct((bsz, seq, attn_w), BF16),
                   jax.ShapeDtypeStruct((bsz, seq, attn_w), BF16),
                   jax.ShapeDtypeStruct((bsz, seq, attn_w), BF16)],
        compiler_params=_cparams("parallel", "parallel"),
        name="inproj",
    )(x, gain.reshape(1, d), mod3, mod3, w_in_bf, rope_tab, rope_exp)


def _rope_tables(positions):
    half = ROT_DIM // 2
    inv_freq = ROPE_THETA ** (-jnp.arange(0, ROT_DIM, 2, dtype=F32) / ROT_DIM)
    ang = positions.astype(F32)[:, None, :] * inv_freq[None, :, None]
    tab = jnp.concatenate([jnp.cos(ang), jnp.sin(ang)], axis=1)
    i = jnp.arange(ROT_DIM)[:, None]
    hl = jnp.arange(LANES)[None, :] % HEAD_DIM
    e_cos = jnp.where((i < half) & ((hl == i) | (hl == i + half)), 1.0, 0.0)
    e_sin = jnp.where(i >= half, jnp.where(hl == i - half, -1.0, jnp.where(hl == i, 1.0, 0.0)), 0.0)
    return tab, jnp.concatenate([e_cos, e_sin], axis=1).astype(BF16)


def _s5_params(lam_re, lam_im, log_dt, b_re, b_im, c_re, c_im, d_skip, n_chunks):
    g_all, p = lam_re.shape
    hc = b_re.shape[-1]
    t = SSM_CHUNK
    gpb = LANES // hc
    nblk = g_all // gpb
    lr, li = lam_re.astype(F32), lam_im.astype(F32)
    dt = jnp.exp(log_dt.astype(F32))[:, None]
    ldr, ldi = lr * dt, li * dt

    def apow(k):
        k = jnp.asarray(k, F32)[..., None, None]
        mag = jnp.exp(ldr * k)
        return mag * jnp.cos(ldi * k), mag * jnp.sin(ldi * k)

    ar, ai = apow(1.0)
    zr, zi = ar - 1.0, ai
    den = lr * lr + li * li
    cr = (zr * lr + zi * li) / den
    ci = (zi * lr - zr * li) / den
    bre, bim = b_re.astype(F32), b_im.astype(F32)
    bbr = cr[..., None] * bre - ci[..., None] * bim
    bbi = cr[..., None] * bim + ci[..., None] * bre
    cre, cim = c_re.astype(F32), c_im.astype(F32)
    def block_diag(a):
        w = a.shape[-1]
        wide = jnp.concatenate([a] * gpb, axis=-1)
        grp = jnp.arange(gpb)[:, None, None]
        lane_grp = (jnp.arange(gpb * w) // w)[None, None, :]
        wide = jnp.where(lane_grp == grp, wide, 0.0)
        return wide.reshape(a.shape[:-3] + (gpb * a.shape[-2], gpb * w))

    pr, pi = apow(jnp.arange(t))
    abr = pr[..., None] * bbr - pi[..., None] * bbi
    abi = pr[..., None] * bbi + pi[..., None] * bbr
    kf = (jnp.einsum('gnp,kgph->kghn', cre, abr, precision=HIGHEST)
          - jnp.einsum('gnp,kgph->kghn', cim, abi, precision=HIGHEST))
    kblk = block_diag(kf.reshape(t, nblk, gpb, hc, hc))
    kpad = jnp.concatenate([jnp.zeros_like(kblk[:1]), kblk], axis=0)
    pm = jnp.concatenate([kpad[0:t], kpad[1:t + 1]], axis=-1)
    kstack = jnp.transpose(pm[::-1], (1, 0, 2, 3)).reshape(nblk, t * LANES, 2 * LANES)

    def inject(ab):
        ab = jnp.transpose(ab[::-1].reshape(t, nblk, gpb, p, hc), (1, 0, 2, 4, 3))
        return block_diag(ab).reshape(nblk, t * LANES, gpb * p)

    bm = jnp.concatenate([inject(abr), inject(abi)], axis=-1)

    qr, qi = apow(jnp.arange(t) + 1.0)
    c_from_re = cre[None] * qr[:, :, None, :] - cim[None] * qi[:, :, None, :]
    c_from_im = -cre[None] * qi[:, :, None, :] - cim[None] * qr[:, :, None, :]

    def readout_t(cc):
        cc = jnp.transpose(cc.reshape(t, nblk, gpb, hc, p), (1, 0, 2, 3, 4))
        return block_diag(cc).reshape(nblk, t * LANES, gpb * p)

    cm = jnp.swapaxes(jnp.concatenate([readout_t(c_from_re), readout_t(c_from_im)], axis=-1), 1, 2)

    n_steps = max(1, int(math.log2(n_chunks)))
    shifts = [float(t * (1 << s)) for s in range(n_steps)]
    sr, si = apow(jnp.asarray(shifts))
    ap = jnp.concatenate([sr.reshape(n_steps, nblk, gpb * p), si.reshape(n_steps, nblk, gpb * p)], axis=-1)
    ap = jnp.transpose(ap, (1, 0, 2))
    dvec = jnp.tile(d_skip.astype(F32).reshape(nblk, 1, LANES), (1, 1, t))
    return kstack.astype(BF16), bm.astype(BF16), cm.astype(BF16), ap, dvec


def _s5_body(x_ref, ks_ref, bm_ref, cm_ref, ap_ref, d_ref, y_ref, h_ref):
    nc = x_ref.shape[2]
    t = SSM_CHUNK
    pad = h_ref.shape[0] - nc
    half = h_ref.shape[1] // 2
    x = x_ref[0, 0]
    h_ref[0:pad] = jnp.zeros((pad, h_ref.shape[1]), F32)
    h_ref[pad:pad + nc] = _dot(x, bm_ref[0])
    for step in range(ap_ref.shape[1]):
        d = 1 << step
        cur = h_ref[pad:pad + nc]
        sft = h_ref[pad - d:pad + nc - d]
        ar = ap_ref[0, step:step + 1, 0:half]
        ai = ap_ref[0, step:step + 1, half:]
        cr, ci = cur[:, :half], cur[:, half:]
        sr, si = sft[:, :half], sft[:, half:]
        h_ref[pad:pad + nc, 0:half] = cr + ar * sr - ai * si
        h_ref[pad:pad + nc, half:] = ci + ar * si + ai * sr
    h_prev = h_ref[pad - 1:pad + nc - 1]
    ys = _dot(h_prev.astype(BF16), cm_ref[0])
    for t0 in range(0, t, 2):
        lo, hi = t0 * LANES, (t0 + 2) * LANES
        conv = _dot(x[:, :hi], ks_ref[0, (t - 2 - t0) * LANES:, :])
        y = conv + ys[:, lo:hi] + d_ref[0][:, lo:hi] * x[:, lo:hi].astype(F32)
        y_ref[0, 0, :, lo:hi] = y.astype(y_ref.dtype)


def _s5(u4, kstack, bm, cm, ap, dvec):
    nblk, bsz, nc, w = u4.shape
    sw = bm.shape[-1]
    pad = max(nc // 2, 8)
    return pl.pallas_call(
        _s5_body,
        grid=(nblk, bsz),
        in_specs=[pl.BlockSpec((1, 1, nc, w), lambda j, b: (j, b, 0, 0)),
                  pl.BlockSpec((1,) + kstack.shape[1:], lambda j, b: (j, 0, 0)),
                  pl.BlockSpec((1,) + bm.shape[1:], lambda j, b: (j, 0, 0)),
                  pl.BlockSpec((1,) + cm.shape[1:], lambda j, b: (j, 0, 0)),
                  pl.BlockSpec((1,) + ap.shape[1:], lambda j, b: (j, 0, 0)),
                  pl.BlockSpec((1, 1, w), lambda j, b: (j, 0, 0))],
        out_specs=pl.BlockSpec((1, 1, nc, w), lambda j, b: (j, b, 0, 0)),
        out_shape=jax.ShapeDtypeStruct((nblk, bsz, nc, w), BF16),
        scratch_shapes=[pltpu.VMEM((pad + nc, sw), F32)],
        compiler_params=_cparams("parallel", "parallel"),
        name="s5",
    )(u4, kstack, bm, cm, ap, dvec)


def _attn_body(q_ref, k_ref, v_ref, o_ref, vt_ref, km_ref, bias_ref, sa_ref, sb_ref):
    qi = pl.program_id(2)
    seq = k_ref.shape[1]
    tq = q_ref.shape[1]
    nb = seq // MOBA_BLOCK
    n_kv = seq // KV_TILE
    bpt = KV_TILE // MOBA_BLOCK
    blk_shift = MOBA_BLOCK.bit_length() - 1

    @pl.when(qi == 0)
    def _build():
        k = k_ref[0].astype(F32)
        v = v_ref[0].astype(F32)
        ones = jnp.where(lax.broadcasted_iota(I32, (PV_ROWS - HEAD_DIM, KV_TILE), 0) == 0, 1.0, 0.0)
        for t in range(n_kv):
            v_t = v[t * KV_TILE:(t + 1) * KV_TILE].T
            vt_ref[0, t] = jnp.concatenate([v_t[:HEAD_DIM], ones], axis=0).astype(BF16)
            vt_ref[1, t] = jnp.concatenate([v_t[HEAD_DIM:], ones], axis=0).astype(BF16)
        km = jnp.mean(k.reshape(nb, MOBA_BLOCK, LANES), axis=1)
        lane_b = lax.broadcasted_iota(I32, (nb, LANES), 1)
        km_ref[0] = jnp.where(lane_b < HEAD_DIM, km, 0.0)
        km_ref[1] = jnp.where(lane_b >= HEAD_DIM, km, 0.0)

    q_t = q_ref[0].astype(F32).T
    feat = lax.broadcasted_iota(I32, (LANES, tq), 0)
    q_h = [jnp.where(feat < HEAD_DIM, q_t, 0.0).astype(BF16),
           jnp.where(feat >= HEAD_DIM, q_t, 0.0).astype(BF16)]

    jidx = lax.broadcasted_iota(I32, (nb, tq), 0)
    qblk = qi * (tq // MOBA_BLOCK) + lax.shift_right_logical(
        lax.broadcasted_iota(I32, (nb, tq), 1), blk_shift)
    for hx in range(2):
        g = _dot(km_ref[hx], q_t)
        cnt = jnp.zeros((nb, tq), F32)
        for jp in range(nb):
            row = g[jp:jp + 1, :]
            beats = jnp.where(row > g, 1.0, jnp.where(row == g, jnp.where(jp < jidx, 1.0, 0.0), 0.0))
            cnt = cnt + jnp.where(jp < qblk, beats, 0.0)
        past_sel = jnp.where(jidx < qblk, jnp.where(cnt < MOBA_TOPK, 1.0, 0.0), 0.0)
        sel = jnp.where(jidx == qblk, 1.0, past_sel)
        bias_ref[hx] = jnp.where(sel > 0.5, 0.0, NEG_INF)

    def raw_scores(dst_ref, tile):
        off = pl.multiple_of(tile * KV_TILE, KV_TILE)
        k_t = k_ref[0, pl.ds(off, KV_TILE), :]
        for hx in range(2):
            dst_ref[hx] = _dot(k_t, q_h[hx])

    def fold(carry, src_ref, tile, causal=None):
        out = []
        for hx in range(2):
            parts, biases = [], []
            for j in range(bpt):
                s_j = src_ref[hx, j * MOBA_BLOCK:(j + 1) * MOBA_BLOCK, :]
                if causal is not None:
                    s_j = jnp.where(causal[j], s_j, NEG_INF)
                b_j = bias_ref[hx, pl.ds(tile * bpt + j, 1), :]
                parts.append(s_j)
                biases.append(b_j)
            m_n = jnp.max(parts[0], axis=0, keepdims=True) + biases[0]
            for s_j, b_j in zip(parts[1:], biases[1:]):
                m_n = jnp.maximum(m_n, jnp.max(s_j, axis=0, keepdims=True) + b_j)
            if carry is None:
                p_t = jnp.concatenate([jnp.exp2(s_j + (b_j - m_n)) for s_j, b_j in zip(parts, biases)], axis=0)
                out += [m_n, _dot(vt_ref[hx, tile], p_t.astype(BF16))]
            else:
                m_c, acc_c = carry[2 * hx], carry[2 * hx + 1]
                m_n = jnp.maximum(m_c, m_n)
                p_t = jnp.concatenate([jnp.exp2(s_j + (b_j - m_n)) for s_j, b_j in zip(parts, biases)], axis=0)
                out += [m_n, acc_c * jnp.exp2(m_c - m_n) + _dot(vt_ref[hx, tile], p_t.astype(BF16))]
        return tuple(out)

    raw_scores(sa_ref, qi)
    raw_scores(sb_ref, 0)
    krow = lax.broadcasted_iota(I32, (MOBA_BLOCK, tq), 0)
    qcol = lax.broadcasted_iota(I32, (MOBA_BLOCK, tq), 1)
    state = fold(None, sa_ref, qi, causal=[krow + j * MOBA_BLOCK <= qcol for j in range(bpt)])

    def body(i, carry):
        t0 = 2 * i
        raw_scores(sa_ref, t0 + 1)
        carry = fold(carry, sb_ref, t0)
        raw_scores(sb_ref, jnp.minimum(t0 + 2, n_kv - 1))
        return fold(carry, sa_ref, t0 + 1)

    state = lax.fori_loop(0, qi // 2, body, state)
    _, acc_a, _, acc_b = lax.cond(lax.rem(qi, 2) == 1, lambda c: fold(c, sb_ref, qi - 1), lambda c: c, state)
    o_t = jnp.concatenate([acc[:HEAD_DIM] / acc[HEAD_DIM:HEAD_DIM + 1] for acc in (acc_a, acc_b)], axis=0)
    o_ref[0] = o_t.T.astype(o_ref.dtype)


def _attention(q, k, v):
    bsz, seq, aw = q.shape
    npair = aw // LANES
    tq = KV_TILE
    nb = seq // MOBA_BLOCK
    assert seq % KV_TILE == 0 and nb % 8 == 0
    return pl.pallas_call(
        _attn_body,
        grid=(bsz, npair, seq // tq),
        in_specs=[pl.BlockSpec((1, tq, LANES), lambda b, p, i: (b, i, p)),
                  pl.BlockSpec((1, seq, LANES), lambda b, p, i: (b, 0, p)),
                  pl.BlockSpec((1, seq, LANES), lambda b, p, i: (b, 0, p))],
        out_specs=pl.BlockSpec((1, tq, LANES), lambda b, p, i: (b, i, p)),
        out_shape=jax.ShapeDtypeStruct((bsz, seq, aw), BF16),
        scratch_shapes=[pltpu.VMEM((2, seq // KV_TILE, PV_ROWS, KV_TILE), BF16),
                        pltpu.VMEM((2, nb, LANES), F32),
                        pltpu.VMEM((2, nb, tq), F32),
                        pltpu.VMEM((2, KV_TILE, tq), F32),
                        pltpu.VMEM((2, KV_TILE, tq), F32)],
        compiler_params=_cparams("parallel", "parallel", "arbitrary"),
        name="attn",
    )(q, k, v)


def _merge_body(y_ref, a_ref, x_ref, wglu_ref, bglu_ref, gs_ref, ga_ref, wout_ref, gm_ref,
                gf_ref, shf_ref, scf_ref, rw_ref, rb_ref, x1_ref, h2_ref, idx_ref, wt_ref, cnt_ref):
    nblk = y_ref.shape[0]
    y = jnp.concatenate([y_ref[j, 0] for j in range(nblk)], axis=1).astype(F32)
    tm = y.shape[0]
    g = 0.5 * y * (1.0 + lax.erf(y * (2.0 ** -0.5)))
    glu = g * jax.nn.sigmoid(_dot(g.astype(BF16), wglu_ref[...]) + bglu_ref[...])
    ssm_n = _rms(glu, gs_ref[...])
    att_n = _rms(a_ref[0].astype(F32), ga_ref[...])
    merged = jnp.concatenate([ssm_n, att_n], axis=1).astype(BF16)
    mix = _dot(merged, wout_ref[...])
    x1 = x_ref[0] + gm_ref[0] * mix
    x1_ref[0] = x1
    h2 = _rms(x1, gf_ref[...]) * (1.0 + scf_ref[0]) + shf_ref[0]
    h2_ref[0] = h2
    logits = lax.dot_general(rw_ref[...], h2, (((1,), (1,)), ((), ())),
                             preferred_element_type=F32, precision=HIGHEST) + rb_ref[...]
    ne = logits.shape[0]
    sub = lax.broadcasted_iota(I32, (ne, tm), 0)
    vals, idxs = [], []
    for _ in range(TOP_K):
        mx = jnp.max(logits, axis=0, keepdims=True)
        ix = jnp.min(jnp.where(logits == mx, sub, ne), axis=0, keepdims=True)
        vals.append(mx)
        idxs.append(ix)
        logits = jnp.where(sub == ix, -jnp.inf, logits)
    exps = [jnp.exp(vv - vals[0]) for vv in vals]
    tot = exps[0]
    for e in exps[1:]:
        tot = tot + e
    row_i = lax.broadcasted_iota(I32, idx_ref.shape, 0)
    row_w = lax.broadcasted_iota(I32, wt_ref.shape, 0)
    idx_out = jnp.zeros(idx_ref.shape, I32)
    wt_out = jnp.zeros(wt_ref.shape, F32)
    for kk in range(TOP_K):
        idx_out = jnp.where(row_i == kk, idxs[kk], idx_out)
        wt_out = jnp.where(row_w == kk, exps[kk] / tot, wt_out)
    idx_ref[...] = idx_out
    wt_ref[...] = wt_out
    chosen = jnp.zeros((ne, tm), F32)
    for kk in range(TOP_K):
        chosen = chosen + jnp.where(sub == idxs[kk], 1.0, 0.0)
    cnt_ref[0] = jnp.sum(chosen, axis=1, keepdims=True).astype(I32)


def _merge(y4, attn, x, w_glu_bf, b_glu, g_ssm, g_attn, w_out_bf, mod3, g_ffn, router_w, router_b, tm=512):
    bsz, seq, d = x.shape
    nblk = y4.shape[0]
    ssm_w = nblk * LANES
    aw = attn.shape[-1]
    ne = router_w.shape[1]
    tm = min(tm, seq)
    spb = seq // tm
    n_tok = bsz * seq
    rw_t = router_w.T
    row = lambda n: pl.BlockSpec((1, n), lambda b, i: (0, 0))
    full = lambda a: pl.BlockSpec(a.shape, lambda b, i: (0,) * a.ndim)
    modv = lambda j: pl.BlockSpec((1, 1, d), lambda b, i: (b, 0, j))
    return pl.pallas_call(
        _merge_body,
        grid=(bsz, seq // tm),
        in_specs=[pl.BlockSpec((nblk, 1, tm, LANES), lambda b, i: (0, b, i, 0)),
                  pl.BlockSpec((1, tm, aw), lambda b, i: (b, i, 0)),
                  pl.BlockSpec((1, tm, d), lambda b, i: (b, i, 0)),
                  full(w_glu_bf), row(ssm_w), row(ssm_w), row(aw), full(w_out_bf),
                  modv(2), row(d), modv(3), modv(4), full(rw_t),
                  pl.BlockSpec((ne, 1), lambda b, i: (0, 0))],
        out_specs=[pl.BlockSpec((1, tm, d), lambda b, i: (b, i, 0)),
                   pl.BlockSpec((1, tm, d), lambda b, i: (b, i, 0)),
                   pl.BlockSpec((TOP_K, tm), lambda b, i: (0, b * spb + i)),
                   pl.BlockSpec((WT_ROWS, tm), lambda b, i: (0, b * spb + i)),
                   pl.BlockSpec((1, ne, 1), lambda b, i: (b * spb + i, 0, 0))],
        out_shape=[jax.ShapeDtypeStruct((bsz, seq, d), F32),
                   jax.ShapeDtypeStruct((bsz, seq, d), F32),
                   jax.ShapeDtypeStruct((TOP_K, n_tok), I32),
                   jax.ShapeDtypeStruct((WT_ROWS, n_tok), F32),
                   jax.ShapeDtypeStruct((bsz * spb, ne, 1), I32)],
        compiler_params=_cparams("parallel", "parallel"),
        name="merge",
    )(y4, attn, x, w_glu_bf, b_glu.reshape(1, ssm_w), g_ssm.reshape(1, ssm_w), g_attn.reshape(1, aw),
      w_out_bf, mod3, g_ffn.reshape(1, d), mod3, mod3, rw_t, router_b.reshape(ne, 1))


def _segment_copies(n_seg, meta, make_copy):
    def body(e, _):
        src0, dst0, units = meta(e)
        off = jnp.int32(0)
        for sz in SEG_UNITS:
            take = (units & sz) != 0
            src = pl.multiple_of(src0 + off, SEG_ALIGN)
            dst = pl.multiple_of(dst0 + off, SEG_ALIGN)

            @pl.when(take)
            def _(src=src, dst=dst, sz=sz):
                make_copy(src, dst, sz * SEG_ALIGN).start()

            off = off + jnp.where(take, sz * SEG_ALIGN, 0)
        return 0

    lax.fori_loop(0, n_seg, body, 0)


def _local_slots(idx, lstart_col, ne):
    tb = idx.shape[1]
    sub = lax.broadcasted_iota(I32, (ne, tb), 0)
    hits = [idx[kk:kk + 1, :] == sub for kk in range(TOP_K)]
    onehot = jnp.zeros((ne, tb), F32)
    for h in hits:
        onehot = onehot + jnp.where(h, 1.0, 0.0)
    r = lax.broadcasted_iota(I32, (tb, tb), 0)
    c = lax.broadcasted_iota(I32, (tb, tb), 1)
    tri = jnp.where(r < c, 1.0, 0.0).astype(BF16)
    slot = _dot(onehot.astype(BF16), tri) + lstart_col
    return [jnp.sum(jnp.where(h, slot, 0.0), axis=0, keepdims=True) for h in hits]


def _dispatch_body(lstart_s, pcnt_s, base_s, idx_ref, lcol_ref, h_ref, xs_ref, ls_ref, sorted_ref, sem, *, ne):
    i = pl.program_id(0)
    nt = pl.num_programs(0)
    tb = h_ref.shape[0]
    lb = sorted_ref.shape[1]
    slot = lax.rem(i, 2)

    def tile_rows(t):
        return pl.multiple_of(lstart_s[t * ne + ne - 1] + pcnt_s[t * ne + ne - 1], SEG_ALIGN)

    def wait_tile(t, sl):
        done = xs_ref.at[pl.ds(0, tile_rows(t)), :]
        pltpu.make_async_copy(done, done, sem.at[sl]).wait()

    @pl.when(i >= 2)
    def _():
        wait_tile(i - 2, slot)

    slots = _local_slots(idx_ref[...], lcol_ref[0].astype(F32), ne)
    row_o = lax.broadcasted_iota(I32, ls_ref.shape, 0)
    ls_out = jnp.full(ls_ref.shape, -1.0, F32)
    for kk in range(TOP_K):
        ls_out = jnp.where(row_o == kk, slots[kk], ls_out)
    ls_ref[...] = ls_out.astype(I32)
    h = h_ref[...].astype(BF16)
    for r0 in range(0, lb, SORT_CHUNK):
        j = (lax.broadcasted_iota(I32, (SORT_CHUNK, tb), 0) + r0).astype(F32)
        pm = jnp.zeros((SORT_CHUNK, tb), F32)
        for kk in range(TOP_K):
            pm = jnp.where(j == slots[kk], 1.0, pm)
        sorted_ref[slot, r0:r0 + SORT_CHUNK, :] = _dot(pm.astype(BF16), h)

    def meta(e):
        return lstart_s[i * ne + e], base_s[i * ne + e], pcnt_s[i * ne + e] // SEG_ALIGN

    def make_copy(src, dst, rows):
        return pltpu.make_async_copy(sorted_ref.at[slot, pl.ds(src, rows), :], xs_ref.at[pl.ds(dst, rows), :],
                                     sem.at[slot])

    _segment_copies(ne, meta, make_copy)

    @pl.when(i == nt - 1)
    def _():
        @pl.when(i >= 1)
        def _():
            wait_tile(i - 1, 1 - slot)
        wait_tile(i, slot)


def _dispatch(h2, idx_t, plan, cap, tb):
    n, d = h2.shape
    lstart, pcnt, base = plan
    nt, ne = lstart.shape
    lb = tb * TOP_K + ne * SEG_ALIGN
    assert lb % SORT_CHUNK == 0
    return pl.pallas_call(
        functools.partial(_dispatch_body, ne=ne),
        grid_spec=pltpu.PrefetchScalarGridSpec(
            num_scalar_prefetch=3,
            grid=(nt,),
            in_specs=[pl.BlockSpec((TOP_K, tb), lambda i, *_: (0, i)),
                      pl.BlockSpec((1, ne, 1), lambda i, *_: (i, 0, 0)),
                      pl.BlockSpec((tb, d), lambda i, *_: (i, 0))],
            out_specs=[pl.BlockSpec(memory_space=pl.ANY),
                       pl.BlockSpec((WT_ROWS, tb), lambda i, *_: (0, i))],
            scratch_shapes=[pltpu.VMEM((2, lb, d), F32), pltpu.SemaphoreType.DMA((2,))],
        ),
        out_shape=[jax.ShapeDtypeStruct((cap, d), F32),
                   jax.ShapeDtypeStruct((WT_ROWS, n), I32)],
        compiler_params=_cparams("arbitrary"),
        name="dispatch",
    )(lstart.reshape(-1), pcnt.reshape(-1), base.reshape(-1), idx_t, lstart.reshape(nt, ne, 1), h2)


def _route_plan(tile_cnt, tg, cap):
    cnt = tile_cnt[:, :, 0]
    pcnt = (cnt + SEG_ALIGN - 1) // SEG_ALIGN * SEG_ALIGN
    lstart = jnp.cumsum(pcnt, axis=1) - pcnt
    etot = jnp.sum(pcnt, axis=0)
    eoff = jnp.cumsum(etot) - etot
    base = eoff[None, :] + jnp.cumsum(pcnt, axis=0) - pcnt
    items = _plan_items(etot, cap, tg)
    return (lstart.astype(I32), pcnt.astype(I32), base.astype(I32)), items


def _experts_body(tile_ref, exp_ref, lo_ref, hi_ref, first_ref, newe_ref,
                  xs_ref, wg_ref, bg_ref, wu_ref, bu_ref, wd_ref, bd_ref, ys_ref,
                  wg_bf, wu_bf, wd_bf):
    i = pl.program_id(0)
    lo = lo_ref[i]
    hi = hi_ref[i]

    @pl.when(newe_ref[i] == 1)
    def _cast():
        wg_bf[...] = wg_ref[0].astype(BF16)
        wu_bf[...] = wu_ref[0].astype(BF16)
        wd_bf[...] = wd_ref[0].astype(BF16)

    @pl.when(hi > lo)
    def _compute():
        x = xs_ref[...].astype(BF16)
        gate = jnp.minimum(_dot(x, wg_bf[...]) + bg_ref[0], SWIGLU_LIMIT)
        lin = jnp.clip(_dot(x, wu_bf[...]) + bu_ref[0], -SWIGLU_LIMIT, SWIGLU_LIMIT)
        act = gate * jax.nn.sigmoid(SWIGLU_ALPHA * gate) * (lin + 1.0)
        y = _dot(act.astype(BF16), wd_bf[...]) + bd_ref[0]
        row = lax.broadcasted_iota(I32, y.shape, 0)
        mine = (row >= lo) & (row < hi)

        @pl.when(first_ref[i] == 1)
        def _():
            ys_ref[...] = jnp.where(mine, y, 0.0)

        @pl.when(first_ref[i] == 0)
        def _():
            ys_ref[...] = jnp.where(mine, y, ys_ref[...])


def _experts(xs, items, w_gate, b_gate, w_up, b_up, w_down, b_down, tg):
    nk, d = xs.shape
    ne, _, dff = w_gate.shape
    n_items = items[0].shape[0]
    wspec = lambda shp: pl.BlockSpec((1,) + shp, lambda i, t, e, lo, hi, f, nw: (e[i], 0, 0))
    return pl.pallas_call(
        _experts_body,
        grid_spec=pltpu.PrefetchScalarGridSpec(
            num_scalar_prefetch=6,
            grid=(n_items,),
            in_specs=[pl.BlockSpec((tg, d), lambda i, t, e, lo, hi, f, nw: (t[i], 0)),
                      wspec((d, dff)), wspec((1, dff)),
                      wspec((d, dff)), wspec((1, dff)),
                      wspec((dff, d)), wspec((1, d))],
            out_specs=pl.BlockSpec((tg, d), lambda i, t, e, lo, hi, f, nw: (t[i], 0)),
            scratch_shapes=[pltpu.VMEM((d, dff), BF16), pltpu.VMEM((d, dff), BF16), pltpu.VMEM((dff, d), BF16)],
        ),
        out_shape=jax.ShapeDtypeStruct((nk, d), F32),
        compiler_params=_cparams("arbitrary"),
        name="experts",
    )(*items, xs, w_gate, b_gate.reshape(ne, 1, dff), w_up, b_up.reshape(ne, 1, dff),
      w_down, b_down.reshape(ne, 1, d))


def _plan_items(counts, nk, tg):
    ne = counts.shape[0]
    n_tiles = nk // tg
    n_items = n_tiles + ne
    ends = jnp.cumsum(counts)
    offs = ends - counts
    first_tile = offs // tg
    last_tile = jnp.maximum(ends - 1, 0) // tg
    per_e = jnp.where(counts > 0, last_tile - first_tile + 1, 0)
    item_end = jnp.cumsum(per_e)
    item_start = item_end - per_e
    total = item_end[-1]
    ii = jnp.arange(n_items, dtype=I32)
    first_after = lambda i: jnp.minimum(jnp.sum((item_end[None, :] <= i[:, None]).astype(I32), axis=1), ne - 1)
    e_of = first_after(ii)
    valid = ii < total
    last_e = first_after(total[None] - 1)[0]
    e_of = jnp.where(valid, e_of, last_e)
    last_tile = jnp.maximum(ends[-1] - 1, 0) // tg
    tile = jnp.where(valid, first_tile[e_of] + (ii - item_start[e_of]), last_tile).astype(I32)
    lo = jnp.clip(offs[e_of] - tile * tg, 0, tg)
    hi = jnp.clip(ends[e_of] - tile * tg, 0, tg)
    lo = jnp.where(valid, lo, 0).astype(I32)
    hi = jnp.where(valid, hi, 0).astype(I32)
    prev_tile = jnp.concatenate([jnp.full((1,), -1, I32), tile[:-1]])
    prev_e = jnp.concatenate([jnp.full((1,), -1, I32), e_of[:-1]])
    first = (tile != prev_tile).astype(I32)
    newe = (e_of != prev_e).astype(I32)
    return (tile, e_of, lo, hi, first, newe)


def _combine_body(lstart_s, pcnt_s, base_s, ls_ref, wt_ref, x1_ref, gf_ref, gfin_ref, sho_ref, sco_ref,
                  ys_ref, o_ref, buf, sem, *, ne, spb):
    i = pl.program_id(0) * spb + pl.program_id(1)
    nt = pl.num_programs(0) * spb
    tb = x1_ref.shape[1]
    lb = buf.shape[1]
    slot = lax.rem(i, 2)

    def fetch(t, sl):
        def meta(e):
            return base_s[t * ne + e], lstart_s[t * ne + e], pcnt_s[t * ne + e] // SEG_ALIGN

        def make_copy(src, dst, rows):
            return pltpu.make_async_copy(ys_ref.at[pl.ds(src, rows), :], buf.at[sl, pl.ds(dst, rows), :],
                                         sem.at[sl])

        _segment_copies(ne, meta, make_copy)

    @pl.when(i == 0)
    def _():
        buf[...] = jnp.zeros(buf.shape, F32)
        fetch(0, 0)

    @pl.when(i + 1 < nt)
    def _():
        fetch(i + 1, 1 - slot)

    total = pl.multiple_of(lstart_s[i * ne + ne - 1] + pcnt_s[i * ne + ne - 1], SEG_ALIGN)
    done = buf.at[slot, pl.ds(0, total), :]
    pltpu.make_async_copy(done, done, sem.at[slot]).wait()
    ls_col = ls_ref[...].astype(F32).T
    wt_col = wt_ref[...].T
    ffn = jnp.zeros((tb, x1_ref.shape[2]), F32)
    for r0 in range(0, lb, SORT_CHUNK):
        j = (lax.broadcasted_iota(I32, (tb, SORT_CHUNK), 1) + r0).astype(F32)
        wm = jnp.zeros((tb, SORT_CHUNK), F32)
        for kk in range(TOP_K):
            wm = jnp.where(j == ls_col[:, kk:kk + 1], wt_col[:, kk:kk + 1], wm)
        ffn = ffn + _dot(wm.astype(BF16), buf[slot, r0:r0 + SORT_CHUNK, :].astype(BF16))
    x2 = x1_ref[0] + gf_ref[0] * ffn
    o_ref[0] = _rms(x2, gfin_ref[...]) * (1.0 + sco_ref[0]) + sho_ref[0]


def _combine(plan, ls_t, wt_t, x1, mod3, g_final, fmod3, ys, tb):
    bsz, seq, d = x1.shape
    lstart, pcnt, base = plan
    nt, ne = lstart.shape
    spb = seq // tb
    lb = tb * TOP_K + ne * SEG_ALIGN
    return pl.pallas_call(
        functools.partial(_combine_body, ne=ne, spb=spb),
        grid_spec=pltpu.PrefetchScalarGridSpec(
            num_scalar_prefetch=3,
            grid=(bsz, spb),
            in_specs=[pl.BlockSpec((WT_ROWS, tb), lambda b, i, *_: (0, b * spb + i)),
                      pl.BlockSpec((WT_ROWS, tb), lambda b, i, *_: (0, b * spb + i)),
                      pl.BlockSpec((1, tb, d), lambda b, i, *_: (b, i, 0)),
                      pl.BlockSpec((1, 1, d), lambda b, i, *_: (b, 0, 5)),
                      pl.BlockSpec((1, d), lambda b, i, *_: (0, 0)),
                      pl.BlockSpec((1, 1, d), lambda b, i, *_: (b, 0, 0)),
                      pl.BlockSpec((1, 1, d), lambda b, i, *_: (b, 0, 1)),
                      pl.BlockSpec(memory_space=pl.ANY)],
            out_specs=pl.BlockSpec((1, tb, d), lambda b, i, *_: (b, i, 0)),
            scratch_shapes=[pltpu.VMEM((2, lb, d), F32), pltpu.SemaphoreType.DMA((2,))],
        ),
        out_shape=jax.ShapeDtypeStruct((bsz, seq, d), F32),
        compiler_params=_cparams("arbitrary", "arbitrary"),
        name="combine",
    )(lstart.reshape(-1), pcnt.reshape(-1), base.reshape(-1), ls_t, wt_t, x1, mod3, g_final.reshape(1, d),
      fmod3, fmod3, ys)


def kernel(x, c, positions, ada_w, ada_b, final_ada_w, final_ada_b, norm_mix_g, norm_ffn_g, norm_final_g, w_in, ssm_lambda_re, ssm_lambda_im, ssm_log_dt, ssm_b_re, ssm_b_im, ssm_c_re, ssm_c_im, ssm_d, ssm_w_glu, ssm_b_glu, out_norm_ssm_g, out_norm_attn_g, w_out, router_w, router_b, exp_w_gate, exp_b_gate, exp_w_up, exp_b_up, exp_w_down, exp_b_down):
    bsz, seq, d = x.shape
    depth = ada_w.shape[0]
    ssm_w = ssm_d.shape[-1]
    attn_w = (w_in.shape[-1] - ssm_w) // 3
    ne = router_w.shape[-1]
    n_tok = bsz * seq
    nc = seq // SSM_CHUNK
    tg = 512

    rope_tab, rope_exp = _rope_tables(positions)
    fmod3 = _adaln(c, final_ada_w, final_ada_b).reshape(bsz, 1, 2 * d)
    for l in range(depth):
        mod3 = _adaln(c, ada_w[l], ada_b[l]).reshape(bsz, 1, -1)
        u4, q, k, v = _inproj(x, norm_mix_g[l], mod3, w_in[l].astype(BF16), rope_tab, rope_exp, ssm_w, attn_w)
        tables = _s5_params(ssm_lambda_re[l], ssm_lambda_im[l], ssm_log_dt[l], ssm_b_re[l], ssm_b_im[l],
                            ssm_c_re[l], ssm_c_im[l], ssm_d[l], nc)
        nblk = ssm_w // LANES
        y4 = _s5(u4.reshape(nblk, bsz, nc, SSM_CHUNK * LANES), *tables).reshape(nblk, bsz, seq, LANES)
        attn = _attention(q, k, v)
        x1, h2, idx_t, wt_t, tile_cnt = _merge(y4, attn, x, ssm_w_glu[l].astype(BF16), ssm_b_glu[l], out_norm_ssm_g[l],
                                     out_norm_attn_g[l], w_out[l].astype(BF16), mod3, norm_ffn_g[l],
                                     router_w[l], router_b[l])
        tb = seq // (tile_cnt.shape[0] // bsz)
        cap = n_tok * TOP_K + tile_cnt.shape[0] * ne * SEG_ALIGN
        assert cap % tg == 0 and tb // SEG_ALIGN < 2 * SEG_UNITS[0]
        plan, items = _route_plan(tile_cnt, tg, cap)
        xs, ls_t = _dispatch(h2.reshape(n_tok, d), idx_t, plan, cap, tb)
        ys = _experts(xs, items, exp_w_gate[l], exp_b_gate[l], exp_w_up[l], exp_b_up[l],
                      exp_w_down[l], exp_b_down[l], tg)
        if l + 1 < depth:
            raise NotImplementedError("depth > 1 needs the non-final combine")
        x = _combine(plan, ls_t, wt_t, x1, mod3, norm_final_g, fmod3, ys, tb)
    return x
```

```python
import functools
import math

import jax
import jax.numpy as jnp
from jax import lax
from jax.experimental import pallas as pl
from jax.experimental.pallas import tpu as pltpu

F32 = jnp.float32
BF16 = jnp.bfloat16
I32 = jnp.int32
HIGHEST = lax.Precision.HIGHEST

LANES = 128
HEAD_DIM = 64
MOBA_BLOCK = 256
MOBA_TOPK = 3
ROT_DIM = HEAD_DIM // 4
ROPE_THETA = 500000.0
SSM_GROUP = 16
SSM_CHUNK = 16
TOP_K = 4
SWIGLU_ALPHA = 1.702
SWIGLU_LIMIT = 7.0
NORM_EPS = 1e-5
NEG_INF = -1e30
LOG2_E = math.log2(math.e)
KV_TILE = 2 * MOBA_BLOCK
PV_ROWS = HEAD_DIM + 8
WT_ROWS = 8
SEG_ALIGN = 8
SEG_UNITS = (64, 32, 16, 8, 4, 2, 1)
SORT_CHUNK = 256
VMEM_LIMIT = 56 * 1024 * 1024


def _cparams(*sem):
    return pltpu.CompilerParams(dimension_semantics=sem, vmem_limit_bytes=VMEM_LIMIT)


def _dot(a, b):
    return jnp.dot(a, b, preferred_element_type=F32)


def _rms(x, g):
    return x * lax.rsqrt(jnp.mean(x * x, axis=-1, keepdims=True) + NORM_EPS) * g


def _adaln_body(c_ref, w_ref, b_ref, o_ref):
    c = c_ref[...]
    ca = c * jax.nn.sigmoid(c)
    o_ref[...] = jnp.dot(ca, w_ref[...], preferred_element_type=F32, precision=HIGHEST) + b_ref[...]


def _adaln(c, w, b, tn=512):
    bsz, d = c.shape
    n = w.shape[1]
    return pl.pallas_call(
        _adaln_body,
        grid=(n // tn,),
        in_specs=[pl.BlockSpec((bsz, d), lambda j: (0, 0)),
                  pl.BlockSpec((d, tn), lambda j: (0, j)),
                  pl.BlockSpec((1, tn), lambda j: (0, j))],
        out_specs=pl.BlockSpec((bsz, tn), lambda j: (0, j)),
        out_shape=jax.ShapeDtypeStruct((bsz, n), F32),
        compiler_params=_cparams("parallel"),
        name="adaln",
    )(c, w, b.reshape(1, n))


def _inproj_body(x_ref, g_ref, sh_ref, sc_ref, w_ref, tab_ref, exp_ref, u_ref, q_ref, k_ref, v_ref, us_ref):
    x = x_ref[0]
    tm = x.shape[0]
    h = _rms(x, g_ref[...]) * (1.0 + sc_ref[0]) + sh_ref[0]
    proj = _dot(h.astype(BF16), w_ref[...])
    ssm_w = u_ref.shape[0] * LANES
    aw = q_ref.shape[2]
    for j in range(u_ref.shape[0]):
        us_ref[j] = proj[:, j * LANES:(j + 1) * LANES]
        for t in range(SSM_CHUNK):
            u_ref[j, 0, :, t * LANES:(t + 1) * LANES] = us_ref[
                j, pl.ds(t, tm // SSM_CHUNK, stride=SSM_CHUNK), :].astype(BF16)
    t_hi = tab_ref[0].astype(BF16)
    t_lo = (tab_ref[0] - t_hi.astype(F32)).astype(BF16)
    tn = (((0,), (0,)), ((), ()))
    tab = (lax.dot_general(t_hi, exp_ref[...], tn, preferred_element_type=F32)
           + lax.dot_general(t_lo, exp_ref[...], tn, preferred_element_type=F32))
    lane1 = lax.broadcasted_iota(I32, (tm, LANES), 1)
    cos1 = tab[:, :LANES] + jnp.where((lane1 & (HEAD_DIM - 1)) >= ROT_DIM, 1.0, 0.0)
    reps = aw // LANES
    cosf = jnp.concatenate([cos1] * reps, axis=1)
    sinf = jnp.concatenate([tab[:, LANES:]] * reps, axis=1)
    lane = lax.broadcasted_iota(I32, (tm, aw), 1)
    first_half = (lane & (HEAD_DIM - 1)) < (ROT_DIM // 2)

    def rope(t):
        partner = jnp.where(first_half, pltpu.roll(t, aw - ROT_DIM // 2, 1), pltpu.roll(t, ROT_DIM // 2, 1))
        return t * cosf + partner * sinf

    q = rope(proj[:, ssm_w:ssm_w + aw]) * (HEAD_DIM ** -0.5 * LOG2_E)
    k = rope(proj[:, ssm_w + aw:ssm_w + 2 * aw])
    q_ref[0] = q.astype(BF16)
    k_ref[0] = k.astype(BF16)
    v_ref[0] = proj[:, ssm_w + 2 * aw:ssm_w + 3 * aw].astype(BF16)


def _inproj(x, gain, mod3, w_in_bf, rope_tab, rope_exp, ssm_w, attn_w, tm=512):
    bsz, seq, d = x.shape
    n_u = ssm_w // LANES
    tm = min(tm, seq)
    return pl.pallas_call(
        _inproj_body,
        grid=(bsz, seq // tm),
        in_specs=[pl.BlockSpec((1, tm, d), lambda b, i: (b, i, 0)),
                  pl.BlockSpec((1, d), lambda b, i: (0, 0)),
                  pl.BlockSpec((1, 1, d), lambda b, i: (b, 0, 0)),
                  pl.BlockSpec((1, 1, d), lambda b, i: (b, 0, 1)),
                  pl.BlockSpec(w_in_bf.shape, lambda b, i: (0, 0)),
                  pl.BlockSpec((1, ROT_DIM, tm), lambda b, i: (b, 0, i)),
                  pl.BlockSpec(rope_exp.shape, lambda b, i: (0, 0))],
        out_specs=[pl.BlockSpec((n_u, 1, tm // SSM_CHUNK, SSM_CHUNK * LANES), lambda b, i: (0, b, i, 0)),
                   pl.BlockSpec((1, tm, attn_w), lambda b, i: (b, i, 0)),
                   pl.BlockSpec((1, tm, attn_w), lambda b, i: (b, i, 0)),
                   pl.BlockSpec((1, tm, attn_w), lambda b, i: (b, i, 0))],
        out_shape=[jax.ShapeDtypeStruct((n_u, bsz, seq // SSM_CHUNK, SSM_CHUNK * LANES), BF16),
                   jax.ShapeDtypeStruct((bsz, seq, attn_w), BF16),
                   jax.ShapeDtypeStruct((bsz, seq, attn_w), BF16),
                   jax.ShapeDtypeStruct((bsz, seq, attn_w), BF16)],
        scratch_shapes=[pltpu.VMEM((n_u, tm, LANES), F32)],
        compiler_params=_cparams("parallel", "parallel"),
        name="inproj",
    )(x, gain.reshape(1, d), mod3, mod3, w_in_bf, rope_tab, rope_exp)


def _rope_tables(positions):
    half = ROT_DIM // 2
    inv_freq = ROPE_THETA ** (-jnp.arange(0, ROT_DIM, 2, dtype=F32) / ROT_DIM)
    ang = positions.astype(F32)[:, None, :] * inv_freq[None, :, None]
    tab = jnp.concatenate([jnp.cos(ang), jnp.sin(ang)], axis=1)
    i = jnp.arange(ROT_DIM)[:, None]
    hl = jnp.arange(LANES)[None, :] % HEAD_DIM
    e_cos = jnp.where((i < half) & ((hl == i) | (hl == i + half)), 1.0, 0.0)
    e_sin = jnp.where(i >= half, jnp.where(hl == i - half, -1.0, jnp.where(hl == i, 1.0, 0.0)), 0.0)
    return tab, jnp.concatenate([e_cos, e_sin], axis=1).astype(BF16)


def _s5_params(lam_re, lam_im, log_dt, b_re, b_im, c_re, c_im, d_skip, n_chunks):
    g_all, p = lam_re.shape
    hc = b_re.shape[-1]
    t = SSM_CHUNK
    gpb = LANES // hc
    nblk = g_all // gpb
    lr, li = lam_re.astype(F32), lam_im.astype(F32)
    dt = jnp.exp(log_dt.astype(F32))[:, None]
    ldr, ldi = lr * dt, li * dt

    def apow(k):
        k = jnp.asarray(k, F32)[..., None, None]
        mag = jnp.exp(ldr * k)
        return mag * jnp.cos(ldi * k), mag * jnp.sin(ldi * k)

    ar, ai = apow(1.0)
    zr, zi = ar - 1.0, ai
    den = lr * lr + li * li
    cr = (zr * lr + zi * li) / den
    ci = (zi * lr - zr * li) / den
    bre, bim = b_re.astype(F32), b_im.astype(F32)
    bbr = cr[..., None] * bre - ci[..., None] * bim
    bbi = cr[..., None] * bim + ci[..., None] * bre
    cre, cim = c_re.astype(F32), c_im.astype(F32)
    def block_diag(a):
        w = a.shape[-1]
        wide = jnp.concatenate([a] * gpb, axis=-1)
        grp = jnp.arange(gpb)[:, None, None]
        lane_grp = (jnp.arange(gpb * w) // w)[None, None, :]
        wide = jnp.where(lane_grp == grp, wide, 0.0)
        return wide.reshape(a.shape[:-3] + (gpb * a.shape[-2], gpb * w))

    pr, pi = apow(jnp.arange(t))
    abr = pr[..., None] * bbr - pi[..., None] * bbi
    abi = pr[..., None] * bbi + pi[..., None] * bbr
    kf = (jnp.einsum('gnp,kgph->kghn', cre, abr, precision=HIGHEST)
          - jnp.einsum('gnp,kgph->kghn', cim, abi, precision=HIGHEST))
    kblk = block_diag(kf.reshape(t, nblk, gpb, hc, hc))
    kpad = jnp.concatenate([jnp.zeros_like(kblk[:1]), kblk], axis=0)
    pm = jnp.concatenate([kpad[0:t], kpad[1:t + 1]], axis=-1)
    kstack = jnp.transpose(pm[::-1], (1, 0, 2, 3)).reshape(nblk, t * LANES, 2 * LANES)

    def inject(ab):
        ab = jnp.transpose(ab[::-1].reshape(t, nblk, gpb, p, hc), (1, 0, 2, 4, 3))
        return block_diag(ab).reshape(nblk, t * LANES, gpb * p)

    bm = jnp.concatenate([inject(abr), inject(abi)], axis=-1)

    qr, qi = apow(jnp.arange(t) + 1.0)
    c_from_re = cre[None] * qr[:, :, None, :] - cim[None] * qi[:, :, None, :]
    c_from_im = -cre[None] * qi[:, :, None, :] - cim[None] * qr[:, :, None, :]

    def readout_t(cc):
        cc = jnp.transpose(cc.reshape(t, nblk, gpb, hc, p), (1, 0, 2, 3, 4))
        return block_diag(cc).reshape(nblk, t * LANES, gpb * p)

    cm = jnp.swapaxes(jnp.concatenate([readout_t(c_from_re), readout_t(c_from_im)], axis=-1), 1, 2)

    n_steps = max(1, int(math.log2(n_chunks)))
    shifts = [float(t * (1 << s)) for s in range(n_steps)]
    sr, si = apow(jnp.asarray(shifts))
    ap = jnp.concatenate([sr.reshape(n_steps, nblk, gpb * p), si.reshape(n_steps, nblk, gpb * p)], axis=-1)
    ap = jnp.transpose(ap, (1, 0, 2))
    dvec = jnp.tile(d_skip.astype(F32).reshape(nblk, 1, LANES), (1, 1, t))
    return kstack.astype(BF16), bm.astype(BF16), cm.astype(BF16), ap, dvec


def _s5_body(x_ref, ks_ref, bm_ref, cm_ref, ap_ref, d_ref, y_ref, h_ref):
    nc = x_ref.shape[2]
    t = SSM_CHUNK
    pad = h_ref.shape[0] - nc
    half = h_ref.shape[1] // 2
    x = x_ref[0, 0]
    h_ref[0:pad] = jnp.zeros((pad, h_ref.shape[1]), F32)
    h_ref[pad:pad + nc] = _dot(x, bm_ref[0])
    for step in range(ap_ref.shape[1]):
        d = 1 << step
        cur = h_ref[pad:pad + nc]
        sft = h_ref[pad - d:pad + nc - d]
        ar = ap_ref[0, step:step + 1, 0:half]
        ai = ap_ref[0, step:step + 1, half:]
        cr, ci = cur[:, :half], cur[:, half:]
        sr, si = sft[:, :half], sft[:, half:]
        h_ref[pad:pad + nc, 0:half] = cr + ar * sr - ai * si
        h_ref[pad:pad + nc, half:] = ci + ar * si + ai * sr
    h_prev = h_ref[pad - 1:pad + nc - 1]
    ys = _dot(h_prev.astype(BF16), cm_ref[0])
    for t0 in range(0, t, 2):
        lo, hi = t0 * LANES, (t0 + 2) * LANES
        conv = _dot(x[:, :hi], ks_ref[0, (t - 2 - t0) * LANES:, :])
        y = conv + ys[:, lo:hi] + d_ref[0][:, lo:hi] * x[:, lo:hi].astype(F32)
        y_ref[0, 0, :, lo:hi] = y.astype(y_ref.dtype)


def _s5(u4, kstack, bm, cm, ap, dvec):
    nblk, bsz, nc, w = u4.shape
    sw = bm.shape[-1]
    pad = max(nc // 2, 8)
    return pl.pallas_call(
        _s5_body,
        grid=(nblk, bsz),
        in_specs=[pl.BlockSpec((1, 1, nc, w), lambda j, b: (j, b, 0, 0)),
                  pl.BlockSpec((1,) + kstack.shape[1:], lambda j, b: (j, 0, 0)),
                  pl.BlockSpec((1,) + bm.shape[1:], lambda j, b: (j, 0, 0)),
                  pl.BlockSpec((1,) + cm.shape[1:], lambda j, b: (j, 0, 0)),
                  pl.BlockSpec((1,) + ap.shape[1:], lambda j, b: (j, 0, 0)),
                  pl.BlockSpec((1, 1, w), lambda j, b: (j, 0, 0))],
        out_specs=pl.BlockSpec((1, 1, nc, w), lambda j, b: (j, b, 0, 0)),
        out_shape=jax.ShapeDtypeStruct((nblk, bsz, nc, w), BF16),
        scratch_shapes=[pltpu.VMEM((pad + nc, sw), F32)],
        compiler_params=_cparams("parallel", "parallel"),
        name="s5",
    )(u4, kstack, bm, cm, ap, dvec)


def _attn_body(q_ref, k_ref, v_ref, o_ref, vt_ref, km_ref, bias_ref, sa_ref, sb_ref):
    qi = pl.program_id(2)
    seq = k_ref.shape[1]
    tq = q_ref.shape[1]
    nb = seq // MOBA_BLOCK
    n_kv = seq // KV_TILE
    bpt = KV_TILE // MOBA_BLOCK
    blk_shift = MOBA_BLOCK.bit_length() - 1

    @pl.when(qi == 0)
    def _build():
        k = k_ref[0].astype(F32)
        v = v_ref[0].astype(F32)
        ones = jnp.where(lax.broadcasted_iota(I32, (PV_ROWS - HEAD_DIM, KV_TILE), 0) == 0, 1.0, 0.0)
        for t in range(n_kv):
            v_t = v[t * KV_TILE:(t + 1) * KV_TILE].T
            vt_ref[0, t] = jnp.concatenate([v_t[:HEAD_DIM], ones], axis=0).astype(BF16)
            vt_ref[1, t] = jnp.concatenate([v_t[HEAD_DIM:], ones], axis=0).astype(BF16)
        km = jnp.mean(k.reshape(nb, MOBA_BLOCK, LANES), axis=1)
        lane_b = lax.broadcasted_iota(I32, (nb, LANES), 1)
        km_ref[0] = jnp.where(lane_b < HEAD_DIM, km, 0.0)
        km_ref[1] = jnp.where(lane_b >= HEAD_DIM, km, 0.0)

    q_t = q_ref[0].astype(F32).T
    feat = lax.broadcasted_iota(I32, (LANES, tq), 0)
    q_h = [jnp.where(feat < HEAD_DIM, q_t, 0.0).astype(BF16),
           jnp.where(feat >= HEAD_DIM, q_t, 0.0).astype(BF16)]

    jidx = lax.broadcasted_iota(I32, (nb, tq), 0)
    qblk = qi * (tq // MOBA_BLOCK) + lax.shift_right_logical(
        lax.broadcasted_iota(I32, (nb, tq), 1), blk_shift)
    for hx in range(2):
        g = _dot(km_ref[hx], q_t)
        cnt = jnp.zeros((nb, tq), F32)
        for jp in range(nb):
            row = g[jp:jp + 1, :]
            beats = jnp.where(row > g, 1.0, jnp.where(row == g, jnp.where(jp < jidx, 1.0, 0.0), 0.0))
            cnt = cnt + jnp.where(jp < qblk, beats, 0.0)
        past_sel = jnp.where(jidx < qblk, jnp.where(cnt < MOBA_TOPK, 1.0, 0.0), 0.0)
        sel = jnp.where(jidx == qblk, 1.0, past_sel)
        bias_ref[hx] = jnp.where(sel > 0.5, 0.0, NEG_INF)

    def raw_scores(dst_ref, tile):
        off = pl.multiple_of(tile * KV_TILE, KV_TILE)
        k_t = k_ref[0, pl.ds(off, KV_TILE), :]
        for hx in range(2):
            dst_ref[hx] = _dot(k_t, q_h[hx])

    def fold(carry, src_ref, tile, causal=None):
        out = []
        for hx in range(2):
            parts, biases = [], []
            for j in range(bpt):
                s_j = src_ref[hx, j * MOBA_BLOCK:(j + 1) * MOBA_BLOCK, :]
                if causal is not None:
                    s_j = jnp.where(causal[j], s_j, NEG_INF)
                b_j = bias_ref[hx, pl.ds(tile * bpt + j, 1), :]
                parts.append(s_j)
                biases.append(b_j)
            m_n = jnp.max(parts[0], axis=0, keepdims=True) + biases[0]
            for s_j, b_j in zip(parts[1:], biases[1:]):
                m_n = jnp.maximum(m_n, jnp.max(s_j, axis=0, keepdims=True) + b_j)
            if carry is None:
                p_t = jnp.concatenate([jnp.exp2(s_j + (b_j - m_n)) for s_j, b_j in zip(parts, biases)], axis=0)
                out += [m_n, _dot(vt_ref[hx, tile], p_t.astype(BF16))]
            else:
                m_c, acc_c = carry[2 * hx], carry[2 * hx + 1]
                m_n = jnp.maximum(m_c, m_n)
                p_t = jnp.concatenate([jnp.exp2(s_j + (b_j - m_n)) for s_j, b_j in zip(parts, biases)], axis=0)
                out += [m_n, acc_c * jnp.exp2(m_c - m_n) + _dot(vt_ref[hx, tile], p_t.astype(BF16))]
        return tuple(out)

    raw_scores(sa_ref, qi)
    raw_scores(sb_ref, 0)
    krow = lax.broadcasted_iota(I32, (MOBA_BLOCK, tq), 0)
    qcol = lax.broadcasted_iota(I32, (MOBA_BLOCK, tq), 1)
    state = fold(None, sa_ref, qi, causal=[krow + j * MOBA_BLOCK <= qcol for j in range(bpt)])

    def body(i, carry):
        t0 = 2 * i
        raw_scores(sa_ref, t0 + 1)
        carry = fold(carry, sb_ref, t0)
        raw_scores(sb_ref, jnp.minimum(t0 + 2, n_kv - 1))
        return fold(carry, sa_ref, t0 + 1)

    state = lax.fori_loop(0, qi // 2, body, state)
    _, acc_a, _, acc_b = lax.cond(lax.rem(qi, 2) == 1, lambda c: fold(c, sb_ref, qi - 1), lambda c: c, state)
    o_t = jnp.concatenate([acc[:HEAD_DIM] / acc[HEAD_DIM:HEAD_DIM + 1] for acc in (acc_a, acc_b)], axis=0)
    o_ref[0] = o_t.T.astype(o_ref.dtype)


def _attention(q, k, v):
    bsz, seq, aw = q.shape
    npair = aw // LANES
    tq = KV_TILE
    nb = seq // MOBA_BLOCK
    assert seq % KV_TILE == 0 and nb % 8 == 0
    return pl.pallas_call(
        _attn_body,
        grid=(bsz, npair, seq // tq),
        in_specs=[pl.BlockSpec((1, tq, LANES), lambda b, p, i: (b, i, p)),
                  pl.BlockSpec((1, seq, LANES), lambda b, p, i: (b, 0, p)),
                  pl.BlockSpec((1, seq, LANES), lambda b, p, i: (b, 0, p))],
        out_specs=pl.BlockSpec((1, tq, LANES), lambda b, p, i: (b, i, p)),
        out_shape=jax.ShapeDtypeStruct((bsz, seq, aw), BF16),
        scratch_shapes=[pltpu.VMEM((2, seq // KV_TILE, PV_ROWS, KV_TILE), BF16),
                        pltpu.VMEM((2, nb, LANES), F32),
                        pltpu.VMEM((2, nb, tq), F32),
                        pltpu.VMEM((2, KV_TILE, tq), F32),
                        pltpu.VMEM((2, KV_TILE, tq), F32)],
        compiler_params=_cparams("parallel", "parallel", "arbitrary"),
        name="attn",
    )(q, k, v)


def _merge_body(y_ref, a_ref, x_ref, wglu_ref, bglu_ref, gs_ref, ga_ref, wout_ref, gm_ref,
                gf_ref, shf_ref, scf_ref, rw_ref, rb_ref, x1_ref, h2_ref, idx_ref, wt_ref, cnt_ref, ys_ref):
    nblk = y_ref.shape[0]
    tm = x_ref.shape[1]
    for j in range(nblk):
        for t in range(SSM_CHUNK):
            ys_ref[j, pl.ds(t, tm // SSM_CHUNK, stride=SSM_CHUNK), :] = (
                y_ref[j, 0, :, t * LANES:(t + 1) * LANES].astype(F32))
    y = jnp.concatenate([ys_ref[j] for j in range(nblk)], axis=1)
    g = 0.5 * y * (1.0 + lax.erf(y * (2.0 ** -0.5)))
    glu = g * jax.nn.sigmoid(_dot(g.astype(BF16), wglu_ref[...]) + bglu_ref[...])
    ssm_n = _rms(glu, gs_ref[...])
    att_n = _rms(a_ref[0].astype(F32), ga_ref[...])
    merged = jnp.concatenate([ssm_n, att_n], axis=1).astype(BF16)
    mix = _dot(merged, wout_ref[...])
    x1 = x_ref[0] + gm_ref[0] * mix
    x1_ref[0] = x1
    h2 = _rms(x1, gf_ref[...]) * (1.0 + scf_ref[0]) + shf_ref[0]
    h2_ref[0] = h2
    logits = lax.dot_general(rw_ref[...], h2, (((1,), (1,)), ((), ())),
                             preferred_element_type=F32, precision=HIGHEST) + rb_ref[...]
    ne = logits.shape[0]
    sub = lax.broadcasted_iota(I32, (ne, tm), 0)
    vals, idxs = [], []
    for _ in range(TOP_K):
        mx = jnp.max(logits, axis=0, keepdims=True)
        ix = jnp.min(jnp.where(logits == mx, sub, ne), axis=0, keepdims=True)
        vals.append(mx)
        idxs.append(ix)
        logits = jnp.where(sub == ix, -jnp.inf, logits)
    exps = [jnp.exp(vv - vals[0]) for vv in vals]
    tot = exps[0]
    for e in exps[1:]:
        tot = tot + e
    row_i = lax.broadcasted_iota(I32, idx_ref.shape, 0)
    row_w = lax.broadcasted_iota(I32, wt_ref.shape, 0)
    idx_out = jnp.zeros(idx_ref.shape, I32)
    wt_out = jnp.zeros(wt_ref.shape, F32)
    for kk in range(TOP_K):
        idx_out = jnp.where(row_i == kk, idxs[kk], idx_out)
        wt_out = jnp.where(row_w == kk, exps[kk] / tot, wt_out)
    idx_ref[...] = idx_out
    wt_ref[...] = wt_out
    chosen = jnp.zeros((ne, tm), F32)
    for kk in range(TOP_K):
        chosen = chosen + jnp.where(sub == idxs[kk], 1.0, 0.0)
    cnt_ref[0] = jnp.sum(chosen, axis=1, keepdims=True).astype(I32)


def _merge(y4, attn, x, w_glu_bf, b_glu, g_ssm, g_attn, w_out_bf, mod3, g_ffn, router_w, router_b, tm=512):
    bsz, seq, d = x.shape
    nblk = y4.shape[0]
    ssm_w = nblk * LANES
    aw = attn.shape[-1]
    ne = router_w.shape[1]
    tm = min(tm, seq)
    spb = seq // tm
    n_tok = bsz * seq
    rw_t = router_w.T
    row = lambda n: pl.BlockSpec((1, n), lambda b, i: (0, 0))
    full = lambda a: pl.BlockSpec(a.shape, lambda b, i: (0,) * a.ndim)
    modv = lambda j: pl.BlockSpec((1, 1, d), lambda b, i: (b, 0, j))
    return pl.pallas_call(
        _merge_body,
        grid=(bsz, seq // tm),
        in_specs=[pl.BlockSpec((nblk, 1, tm // SSM_CHUNK, SSM_CHUNK * LANES), lambda b, i: (0, b, i, 0)),
                  pl.BlockSpec((1, tm, aw), lambda b, i: (b, i, 0)),
                  pl.BlockSpec((1, tm, d), lambda b, i: (b, i, 0)),
                  full(w_glu_bf), row(ssm_w), row(ssm_w), row(aw), full(w_out_bf),
                  modv(2), row(d), modv(3), modv(4), full(rw_t),
                  pl.BlockSpec((ne, 1), lambda b, i: (0, 0))],
        out_specs=[pl.BlockSpec((1, tm, d), lambda b, i: (b, i, 0)),
                   pl.BlockSpec((1, tm, d), lambda b, i: (b, i, 0)),
                   pl.BlockSpec((TOP_K, tm), lambda b, i: (0, b * spb + i)),
                   pl.BlockSpec((WT_ROWS, tm), lambda b, i: (0, b * spb + i)),
                   pl.BlockSpec((1, ne, 1), lambda b, i: (b * spb + i, 0, 0))],
        out_shape=[jax.ShapeDtypeStruct((bsz, seq, d), F32),
                   jax.ShapeDtypeStruct((bsz, seq, d), F32),
                   jax.ShapeDtypeStruct((TOP_K, n_tok), I32),
                   jax.ShapeDtypeStruct((WT_ROWS, n_tok), F32),
                   jax.ShapeDtypeStruct((bsz * spb, ne, 1), I32)],
        scratch_shapes=[pltpu.VMEM((nblk, tm, LANES), F32)],
        compiler_params=_cparams("parallel", "parallel"),
        name="merge",
    )(y4, attn, x, w_glu_bf, b_glu.reshape(1, ssm_w), g_ssm.reshape(1, ssm_w), g_attn.reshape(1, aw),
      w_out_bf, mod3, g_ffn.reshape(1, d), mod3, mod3, rw_t, router_b.reshape(ne, 1))


def _segment_copies(n_seg, meta, make_copy):
    def body(e, _):
        src0, dst0, units = meta(e)
        off = jnp.int32(0)
        for sz in SEG_UNITS:
            take = (units & sz) != 0
            src = pl.multiple_of(src0 + off, SEG_ALIGN)
            dst = pl.multiple_of(dst0 + off, SEG_ALIGN)

            @pl.when(take)
            def _(src=src, dst=dst, sz=sz):
                make_copy(src, dst, sz * SEG_ALIGN).start()

            off = off + jnp.where(take, sz * SEG_ALIGN, 0)
        return 0

    lax.fori_loop(0, n_seg, body, 0)


def _local_slots(idx, lstart_col, ne):
    tb = idx.shape[1]
    sub = lax.broadcasted_iota(I32, (ne, tb), 0)
    hits = [idx[kk:kk + 1, :] == sub for kk in range(TOP_K)]
    onehot = jnp.zeros((ne, tb), F32)
    for h in hits:
        onehot = onehot + jnp.where(h, 1.0, 0.0)
    r = lax.broadcasted_iota(I32, (tb, tb), 0)
    c = lax.broadcasted_iota(I32, (tb, tb), 1)
    tri = jnp.where(r < c, 1.0, 0.0).astype(BF16)
    slot = _dot(onehot.astype(BF16), tri) + lstart_col
    return [jnp.sum(jnp.where(h, slot, 0.0), axis=0, keepdims=True) for h in hits]


def _dispatch_body(lstart_s, pcnt_s, base_s, idx_ref, lcol_ref, h_ref, xs_ref, ls_ref, sorted_ref, sem, *, ne):
    i = pl.program_id(0)
    nt = pl.num_programs(0)
    tb = h_ref.shape[0]
    lb = sorted_ref.shape[1]
    slot = lax.rem(i, 2)

    def tile_rows(t):
        return pl.multiple_of(lstart_s[t * ne + ne - 1] + pcnt_s[t * ne + ne - 1], SEG_ALIGN)

    def wait_tile(t, sl):
        done = xs_ref.at[pl.ds(0, tile_rows(t)), :]
        pltpu.make_async_copy(done, done, sem.at[sl]).wait()

    @pl.when(i >= 2)
    def _():
        wait_tile(i - 2, slot)

    slots = _local_slots(idx_ref[...], lcol_ref[0].astype(F32), ne)
    row_o = lax.broadcasted_iota(I32, ls_ref.shape, 0)
    ls_out = jnp.full(ls_ref.shape, -1.0, F32)
    for kk in range(TOP_K):
        ls_out = jnp.where(row_o == kk, slots[kk], ls_out)
    ls_ref[...] = ls_out.astype(I32)
    h = h_ref[...].astype(BF16)
    for r0 in range(0, lb, SORT_CHUNK):
        j = (lax.broadcasted_iota(I32, (SORT_CHUNK, tb), 0) + r0).astype(F32)
        pm = jnp.zeros((SORT_CHUNK, tb), F32)
        for kk in range(TOP_K):
            pm = jnp.where(j == slots[kk], 1.0, pm)
        sorted_ref[slot, r0:r0 + SORT_CHUNK, :] = _dot(pm.astype(BF16), h)

    def meta(e):
        return lstart_s[i * ne + e], base_s[i * ne + e], pcnt_s[i * ne + e] // SEG_ALIGN

    def make_copy(src, dst, rows):
        return pltpu.make_async_copy(sorted_ref.at[slot, pl.ds(src, rows), :], xs_ref.at[pl.ds(dst, rows), :],
                                     sem.at[slot])

    _segment_copies(ne, meta, make_copy)

    @pl.when(i == nt - 1)
    def _():
        @pl.when(i >= 1)
        def _():
            wait_tile(i - 1, 1 - slot)
        wait_tile(i, slot)


def _dispatch(h2, idx_t, plan, cap, tb):
    n, d = h2.shape
    lstart, pcnt, base = plan
    nt, ne = lstart.shape
    lb = tb * TOP_K + ne * SEG_ALIGN
    assert lb % SORT_CHUNK == 0
    return pl.pallas_call(
        functools.partial(_dispatch_body, ne=ne),
        grid_spec=pltpu.PrefetchScalarGridSpec(
            num_scalar_prefetch=3,
            grid=(nt,),
            in_specs=[pl.BlockSpec((TOP_K, tb), lambda i, *_: (0, i)),
                      pl.BlockSpec((1, ne, 1), lambda i, *_: (i, 0, 0)),
                      pl.BlockSpec((tb, d), lambda i, *_: (i, 0))],
            out_specs=[pl.BlockSpec(memory_space=pl.ANY),
                       pl.BlockSpec((WT_ROWS, tb), lambda i, *_: (0, i))],
            scratch_shapes=[pltpu.VMEM((2, lb, d), F32), pltpu.SemaphoreType.DMA((2,))],
        ),
        out_shape=[jax.ShapeDtypeStruct((cap, d), F32),
                   jax.ShapeDtypeStruct((WT_ROWS, n), I32)],
        compiler_params=_cparams("arbitrary"),
        name="dispatch",
    )(lstart.reshape(-1), pcnt.reshape(-1), base.reshape(-1), idx_t, lstart.reshape(nt, ne, 1), h2)


def _route_plan(tile_cnt, tg, cap):
    cnt = tile_cnt[:, :, 0]
    ne = cnt.shape[1]
    pcnt = (cnt + SEG_ALIGN - 1) // SEG_ALIGN * SEG_ALIGN
    lstart = jnp.cumsum(pcnt, axis=1) - pcnt
    e_tiles = (jnp.sum(pcnt, axis=0) + tg - 1) // tg
    e_end = jnp.cumsum(e_tiles)
    base = ((e_end - e_tiles) * tg)[None, :] + jnp.cumsum(pcnt, axis=0) - pcnt
    ii = jnp.arange(cap // tg, dtype=I32)
    used = ii < e_end[-1]
    tile = jnp.minimum(ii, e_end[-1] - 1)
    e_of = jnp.minimum(jnp.sum((e_end[None, :] <= tile[:, None]).astype(I32), axis=1), ne - 1)
    newe = (e_of != jnp.concatenate([jnp.full((1,), -1, I32), e_of[:-1]])).astype(I32)
    items = (tile.astype(I32), e_of.astype(I32), used.astype(I32), newe)
    return (lstart.astype(I32), pcnt.astype(I32), base.astype(I32)), items


def _experts_body(tile_ref, exp_ref, used_ref, newe_ref,
                  xs_ref, wg_ref, bg_ref, wu_ref, bu_ref, wd_ref, bd_ref, ys_ref,
                  wg_bf, wu_bf, wd_bf):
    i = pl.program_id(0)

    @pl.when(newe_ref[i] == 1)
    def _cast():
        wg_bf[...] = wg_ref[0].astype(BF16)
        wu_bf[...] = wu_ref[0].astype(BF16)
        wd_bf[...] = wd_ref[0].astype(BF16)

    @pl.when(used_ref[i] == 1)
    def _compute():
        x = xs_ref[...].astype(BF16)
        gate = jnp.minimum(_dot(x, wg_bf[...]) + bg_ref[0], SWIGLU_LIMIT)
        lin = jnp.clip(_dot(x, wu_bf[...]) + bu_ref[0], -SWIGLU_LIMIT, SWIGLU_LIMIT)
        act = gate * jax.nn.sigmoid(SWIGLU_ALPHA * gate) * (lin + 1.0)
        ys_ref[...] = _dot(act.astype(BF16), wd_bf[...]) + bd_ref[0]


def _experts(xs, items, w_gate, b_gate, w_up, b_up, w_down, b_down, tg):
    nk, d = xs.shape
    ne, _, dff = w_gate.shape
    n_items = items[0].shape[0]
    wspec = lambda shp: pl.BlockSpec((1,) + shp, lambda i, t, e, u, nw: (e[i], 0, 0))
    return pl.pallas_call(
        _experts_body,
        grid_spec=pltpu.PrefetchScalarGridSpec(
            num_scalar_prefetch=4,
            grid=(n_items,),
            in_specs=[pl.BlockSpec((tg, d), lambda i, t, e, u, nw: (t[i], 0)),
                      wspec((d, dff)), wspec((1, dff)),
                      wspec((d, dff)), wspec((1, dff)),
                      wspec((dff, d)), wspec((1, d))],
            out_specs=pl.BlockSpec((tg, d), lambda i, t, e, u, nw: (t[i], 0)),
            scratch_shapes=[pltpu.VMEM((d, dff), BF16), pltpu.VMEM((d, dff), BF16), pltpu.VMEM((dff, d), BF16)],
        ),
        out_shape=jax.ShapeDtypeStruct((nk, d), F32),
        compiler_params=_cparams("arbitrary"),
        name="experts",
    )(*items, xs, w_gate, b_gate.reshape(ne, 1, dff), w_up, b_up.reshape(ne, 1, dff),
      w_down, b_down.reshape(ne, 1, d))


def _combine_body(lstart_s, pcnt_s, base_s, ls_ref, wt_ref, x1_ref, gf_ref, gfin_ref, sho_ref, sco_ref,
                  ys_ref, o_ref, buf, sem, *, ne, spb):
    i = pl.program_id(0) * spb + pl.program_id(1)
    nt = pl.num_programs(0) * spb
    tb = x1_ref.shape[1]
    lb = buf.shape[1]
    slot = lax.rem(i, 2)

    def fetch(t, sl):
        def meta(e):
            return base_s[t * ne + e], lstart_s[t * ne + e], pcnt_s[t * ne + e] // SEG_ALIGN

        def make_copy(src, dst, rows):
            return pltpu.make_async_copy(ys_ref.at[pl.ds(src, rows), :], buf.at[sl, pl.ds(dst, rows), :],
                                         sem.at[sl])

        _segment_copies(ne, meta, make_copy)

    @pl.when(i == 0)
    def _():
        buf[...] = jnp.zeros(buf.shape, F32)
        fetch(0, 0)

    @pl.when(i + 1 < nt)
    def _():
        fetch(i + 1, 1 - slot)

    total = pl.multiple_of(lstart_s[i * ne + ne - 1] + pcnt_s[i * ne + ne - 1], SEG_ALIGN)
    done = buf.at[slot, pl.ds(0, total), :]
    pltpu.make_async_copy(done, done, sem.at[slot]).wait()
    ls_col = ls_ref[...].astype(F32).T
    wt_col = wt_ref[...].T
    ffn = jnp.zeros((tb, x1_ref.shape[2]), F32)
    for r0 in range(0, lb, SORT_CHUNK):
        j = (lax.broadcasted_iota(I32, (tb, SORT_CHUNK), 1) + r0).astype(F32)
        wm = jnp.zeros((tb, SORT_CHUNK), F32)
        for kk in range(TOP_K):
            wm = jnp.where(j == ls_col[:, kk:kk + 1], wt_col[:, kk:kk + 1], wm)
        ffn = ffn + _dot(wm.astype(BF16), buf[slot, r0:r0 + SORT_CHUNK, :].astype(BF16))
    x2 = x1_ref[0] + gf_ref[0] * ffn
    o_ref[0] = _rms(x2, gfin_ref[...]) * (1.0 + sco_ref[0]) + sho_ref[0]


def _combine(plan, ls_t, wt_t, x1, mod3, g_final, fmod3, ys, tb):
    bsz, seq, d = x1.shape
    lstart, pcnt, base = plan
    nt, ne = lstart.shape
    spb = seq // tb
    lb = tb * TOP_K + ne * SEG_ALIGN
    return pl.pallas_call(
        functools.partial(_combine_body, ne=ne, spb=spb),
        grid_spec=pltpu.PrefetchScalarGridSpec(
            num_scalar_prefetch=3,
            grid=(bsz, spb),
            in_specs=[pl.BlockSpec((WT_ROWS, tb), lambda b, i, *_: (0, b * spb + i)),
                      pl.BlockSpec((WT_ROWS, tb), lambda b, i, *_: (0, b * spb + i)),
                      pl.BlockSpec((1, tb, d), lambda b, i, *_: (b, i, 0)),
                      pl.BlockSpec((1, 1, d), lambda b, i, *_: (b, 0, 5)),
                      pl.BlockSpec((1, d), lambda b, i, *_: (0, 0)),
                      pl.BlockSpec((1, 1, d), lambda b, i, *_: (b, 0, 0)),
                      pl.BlockSpec((1, 1, d), lambda b, i, *_: (b, 0, 1)),
                      pl.BlockSpec(memory_space=pl.ANY)],
            out_specs=pl.BlockSpec((1, tb, d), lambda b, i, *_: (b, i, 0)),
            scratch_shapes=[pltpu.VMEM((2, lb, d), F32), pltpu.SemaphoreType.DMA((2,))],
        ),
        out_shape=jax.ShapeDtypeStruct((bsz, seq, d), F32),
        compiler_params=_cparams("arbitrary", "arbitrary"),
        name="combine",
    )(lstart.reshape(-1), pcnt.reshape(-1), base.reshape(-1), ls_t, wt_t, x1, mod3, g_final.reshape(1, d),
      fmod3, fmod3, ys)


def kernel(x, c, positions, ada_w, ada_b, final_ada_w, final_ada_b, norm_mix_g, norm_ffn_g, norm_final_g, w_in, ssm_lambda_re, ssm_lambda_im, ssm_log_dt, ssm_b_re, ssm_b_im, ssm_c_re, ssm_c_im, ssm_d, ssm_w_glu, ssm_b_glu, out_norm_ssm_g, out_norm_attn_g, w_out, router_w, router_b, exp_w_gate, exp_b_gate, exp_w_up, exp_b_up, exp_w_down, exp_b_down):
    bsz, seq, d = x.shape
    depth = ada_w.shape[0]
    ssm_w = ssm_d.shape[-1]
    attn_w = (w_in.shape[-1] - ssm_w) // 3
    ne = router_w.shape[-1]
    n_tok = bsz * seq
    nc = seq // SSM_CHUNK
    tg = 512

    rope_tab, rope_exp = _rope_tables(positions)
    fmod3 = _adaln(c, final_ada_w, final_ada_b).reshape(bsz, 1, 2 * d)
    for l in range(depth):
        mod3 = _adaln(c, ada_w[l], ada_b[l]).reshape(bsz, 1, -1)
        u4, q, k, v = _inproj(x, norm_mix_g[l], mod3, w_in[l].astype(BF16), rope_tab, rope_exp, ssm_w, attn_w)
        tables = _s5_params(ssm_lambda_re[l], ssm_lambda_im[l], ssm_log_dt[l], ssm_b_re[l], ssm_b_im[l],
                            ssm_c_re[l], ssm_c_im[l], ssm_d[l], nc)
        y4 = _s5(u4, *tables)
        attn = _attention(q, k, v)
        x1, h2, idx_t, wt_t, tile_cnt = _merge(y4, attn, x, ssm_w_glu[l].astype(BF16), ssm_b_glu[l], out_norm_ssm_g[l],
                                     out_norm_attn_g[l], w_out[l].astype(BF16), mod3, norm_ffn_g[l],
                                     router_w[l], router_b[l])
        tb = seq // (tile_cnt.shape[0] // bsz)
        cap = n_tok * TOP_K + tile_cnt.shape[0] * ne * SEG_ALIGN + ne * tg
        assert cap % tg == 0 and tb // SEG_ALIGN < 2 * SEG_UNITS[0]
        plan, items = _route_plan(tile_cnt, tg, cap)
        xs, ls_t = _dispatch(h2.reshape(n_tok, d), idx_t, plan, cap, tb)
        ys = _experts(xs, items, exp_w_gate[l], exp_b_gate[l], exp_w_up[l], exp_b_up[l],
                      exp_w_down[l], exp_b_down[l], tg)
        if l + 1 < depth:
            raise NotImplementedError("depth > 1 needs the non-final combine")
        x = _combine(plan, ls_t, wt_t, x1, mod3, norm_final_g, fmod3, ys, tb)
    return x
```

```python
import functools
import math

import jax
import jax.numpy as jnp
from jax import lax
from jax.experimental import pallas as pl
from jax.experimental.pallas import tpu as pltpu

F32 = jnp.float32
BF16 = jnp.bfloat16
I32 = jnp.int32
HIGHEST = lax.Precision.HIGHEST

LANES = 128
HEAD_DIM = 64
MOBA_BLOCK = 256
MOBA_TOPK = 3
ROT_DIM = HEAD_DIM // 4
ROPE_THETA = 500000.0
SSM_GROUP = 16
SSM_CHUNK = 16
TOP_K = 4
SWIGLU_ALPHA = 1.702
SWIGLU_LIMIT = 7.0
NORM_EPS = 1e-5
NEG_INF = -1e30
LOG2_E = math.log2(math.e)
KV_TILE = 2 * MOBA_BLOCK
PV_ROWS = HEAD_DIM + 8
WT_ROWS = 8
SEG_ALIGN = 8
SEG_UNITS = (64, 32, 16, 8, 4, 2, 1)
SEG_RARE = 3
SORT_CHUNK = 256
VMEM_LIMIT = 56 * 1024 * 1024


def _cparams(*sem):
    return pltpu.CompilerParams(dimension_semantics=sem, vmem_limit_bytes=VMEM_LIMIT)


def _dot(a, b):
    return jnp.dot(a, b, preferred_element_type=F32)


def _rms(x, g):
    return x * lax.rsqrt(jnp.mean(x * x, axis=-1, keepdims=True) + NORM_EPS) * g


def _adaln_body(c_ref, w_ref, b_ref, o_ref):
    c = c_ref[...]
    ca = c * jax.nn.sigmoid(c)
    o_ref[...] = jnp.dot(ca, w_ref[...], preferred_element_type=F32, precision=HIGHEST) + b_ref[...]


def _adaln(c, w, b, tn=512):
    bsz, d = c.shape
    n = w.shape[1]
    return pl.pallas_call(
        _adaln_body,
        grid=(n // tn,),
        in_specs=[pl.BlockSpec((bsz, d), lambda j: (0, 0)),
                  pl.BlockSpec((d, tn), lambda j: (0, j)),
                  pl.BlockSpec((1, tn), lambda j: (0, j))],
        out_specs=pl.BlockSpec((bsz, tn), lambda j: (0, j)),
        out_shape=jax.ShapeDtypeStruct((bsz, n), F32),
        compiler_params=_cparams("parallel"),
        name="adaln",
    )(c, w, b.reshape(1, n))


def _inproj_body(x_ref, g_ref, sh_ref, sc_ref, w_ref, tab_ref, exp_ref, u_ref, q_ref, k_ref, v_ref, us_ref):
    x = x_ref[0]
    tm = x.shape[0]
    h = _rms(x, g_ref[...]) * (1.0 + sc_ref[0]) + sh_ref[0]
    proj = _dot(h.astype(BF16), w_ref[...])
    ssm_w = u_ref.shape[0] * LANES
    aw = q_ref.shape[2]
    for j in range(u_ref.shape[0]):
        us_ref[j] = proj[:, j * LANES:(j + 1) * LANES]
        for t in range(SSM_CHUNK):
            u_ref[j, 0, :, t * LANES:(t + 1) * LANES] = us_ref[
                j, pl.ds(t, tm // SSM_CHUNK, stride=SSM_CHUNK), :].astype(BF16)
    t_hi = tab_ref[0].astype(BF16)
    t_lo = (tab_ref[0] - t_hi.astype(F32)).astype(BF16)
    tn = (((0,), (0,)), ((), ()))
    tab = (lax.dot_general(t_hi, exp_ref[...], tn, preferred_element_type=F32)
           + lax.dot_general(t_lo, exp_ref[...], tn, preferred_element_type=F32))
    lane1 = lax.broadcasted_iota(I32, (tm, LANES), 1)
    cos1 = tab[:, :LANES] + jnp.where((lane1 & (HEAD_DIM - 1)) >= ROT_DIM, 1.0, 0.0)
    reps = aw // LANES
    cosf = jnp.concatenate([cos1] * reps, axis=1)
    sinf = jnp.concatenate([tab[:, LANES:]] * reps, axis=1)
    lane = lax.broadcasted_iota(I32, (tm, aw), 1)
    first_half = (lane & (HEAD_DIM - 1)) < (ROT_DIM // 2)

    def rope(t):
        partner = jnp.where(first_half, pltpu.roll(t, aw - ROT_DIM // 2, 1), pltpu.roll(t, ROT_DIM // 2, 1))
        return t * cosf + partner * sinf

    q = rope(proj[:, ssm_w:ssm_w + aw]) * (HEAD_DIM ** -0.5 * LOG2_E)
    k = rope(proj[:, ssm_w + aw:ssm_w + 2 * aw])
    q_ref[0] = q.astype(BF16)
    k_ref[0] = k.astype(BF16)
    v_ref[0] = proj[:, ssm_w + 2 * aw:ssm_w + 3 * aw].astype(BF16)


def _inproj(x, gain, mod3, w_in_bf, rope_tab, rope_exp, ssm_w, attn_w, tm=512):
    bsz, seq, d = x.shape
    n_u = ssm_w // LANES
    tm = min(tm, seq)
    return pl.pallas_call(
        _inproj_body,
        grid=(bsz, seq // tm),
        in_specs=[pl.BlockSpec((1, tm, d), lambda b, i: (b, i, 0)),
                  pl.BlockSpec((1, d), lambda b, i: (0, 0)),
                  pl.BlockSpec((1, 1, d), lambda b, i: (b, 0, 0)),
                  pl.BlockSpec((1, 1, d), lambda b, i: (b, 0, 1)),
                  pl.BlockSpec(w_in_bf.shape, lambda b, i: (0, 0)),
                  pl.BlockSpec((1, ROT_DIM, tm), lambda b, i: (b, 0, i)),
                  pl.BlockSpec(rope_exp.shape, lambda b, i: (0, 0))],
        out_specs=[pl.BlockSpec((n_u, 1, tm // SSM_CHUNK, SSM_CHUNK * LANES), lambda b, i: (0, b, i, 0)),
                   pl.BlockSpec((1, tm, attn_w), lambda b, i: (b, i, 0)),
                   pl.BlockSpec((1, tm, attn_w), lambda b, i: (b, i, 0)),
                   pl.BlockSpec((1, tm, attn_w), lambda b, i: (b, i, 0))],
        out_shape=[jax.ShapeDtypeStruct((n_u, bsz, seq // SSM_CHUNK, SSM_CHUNK * LANES), BF16),
                   jax.ShapeDtypeStruct((bsz, seq, attn_w), BF16),
                   jax.ShapeDtypeStruct((bsz, seq, attn_w), BF16),
                   jax.ShapeDtypeStruct((bsz, seq, attn_w), BF16)],
        scratch_shapes=[pltpu.VMEM((n_u, tm, LANES), F32)],
        compiler_params=_cparams("parallel", "parallel"),
        name="inproj",
    )(x, gain.reshape(1, d), mod3, mod3, w_in_bf, rope_tab, rope_exp)


def _rope_tables(positions):
    half = ROT_DIM // 2
    inv_freq = ROPE_THETA ** (-jnp.arange(0, ROT_DIM, 2, dtype=F32) / ROT_DIM)
    ang = positions.astype(F32)[:, None, :] * inv_freq[None, :, None]
    tab = jnp.concatenate([jnp.cos(ang), jnp.sin(ang)], axis=1)
    i = jnp.arange(ROT_DIM)[:, None]
    hl = jnp.arange(LANES)[None, :] % HEAD_DIM
    e_cos = jnp.where((i < half) & ((hl == i) | (hl == i + half)), 1.0, 0.0)
    e_sin = jnp.where(i >= half, jnp.where(hl == i - half, -1.0, jnp.where(hl == i, 1.0, 0.0)), 0.0)
    return tab, jnp.concatenate([e_cos, e_sin], axis=1).astype(BF16)


def _s5_params(lam_re, lam_im, log_dt, b_re, b_im, c_re, c_im, d_skip, n_chunks):
    g_all, p = lam_re.shape
    hc = b_re.shape[-1]
    t = SSM_CHUNK
    gpb = LANES // hc
    nblk = g_all // gpb
    lr, li = lam_re.astype(F32), lam_im.astype(F32)
    dt = jnp.exp(log_dt.astype(F32))[:, None]
    ldr, ldi = lr * dt, li * dt

    def apow(k):
        k = jnp.asarray(k, F32)[..., None, None]
        mag = jnp.exp(ldr * k)
        return mag * jnp.cos(ldi * k), mag * jnp.sin(ldi * k)

    ar, ai = apow(1.0)
    zr, zi = ar - 1.0, ai
    den = lr * lr + li * li
    cr = (zr * lr + zi * li) / den
    ci = (zi * lr - zr * li) / den
    bre, bim = b_re.astype(F32), b_im.astype(F32)
    bbr = cr[..., None] * bre - ci[..., None] * bim
    bbi = cr[..., None] * bim + ci[..., None] * bre
    cre, cim = c_re.astype(F32), c_im.astype(F32)
    def block_diag(a):
        w = a.shape[-1]
        wide = jnp.concatenate([a] * gpb, axis=-1)
        grp = jnp.arange(gpb)[:, None, None]
        lane_grp = (jnp.arange(gpb * w) // w)[None, None, :]
        wide = jnp.where(lane_grp == grp, wide, 0.0)
        return wide.reshape(a.shape[:-3] + (gpb * a.shape[-2], gpb * w))

    pr, pi = apow(jnp.arange(t))
    abr = pr[..., None] * bbr - pi[..., None] * bbi
    abi = pr[..., None] * bbi + pi[..., None] * bbr
    kf = (jnp.einsum('gnp,kgph->kghn', cre, abr, precision=HIGHEST)
          - jnp.einsum('gnp,kgph->kghn', cim, abi, precision=HIGHEST))
    kblk = block_diag(kf.reshape(t, nblk, gpb, hc, hc))
    kpad = jnp.concatenate([jnp.zeros_like(kblk[:1]), kblk], axis=0)
    pm = jnp.concatenate([kpad[0:t], kpad[1:t + 1]], axis=-1)
    kstack = jnp.transpose(pm[::-1], (1, 0, 2, 3)).reshape(nblk, t * LANES, 2 * LANES)

    def inject(ab):
        ab = jnp.transpose(ab[::-1].reshape(t, nblk, gpb, p, hc), (1, 0, 2, 4, 3))
        return block_diag(ab).reshape(nblk, t * LANES, gpb * p)

    bm = jnp.concatenate([inject(abr), inject(abi)], axis=-1)

    qr, qi = apow(jnp.arange(t) + 1.0)
    c_from_re = cre[None] * qr[:, :, None, :] - cim[None] * qi[:, :, None, :]
    c_from_im = -cre[None] * qi[:, :, None, :] - cim[None] * qr[:, :, None, :]

    def readout_t(cc):
        cc = jnp.transpose(cc.reshape(t, nblk, gpb, hc, p), (1, 0, 2, 3, 4))
        return block_diag(cc).reshape(nblk, t * LANES, gpb * p)

    cm = jnp.swapaxes(jnp.concatenate([readout_t(c_from_re), readout_t(c_from_im)], axis=-1), 1, 2)

    n_steps = max(1, int(math.log2(n_chunks)))
    shifts = [float(t * (1 << s)) for s in range(n_steps)]
    sr, si = apow(jnp.asarray(shifts))
    ap = jnp.concatenate([sr.reshape(n_steps, nblk, gpb * p), si.reshape(n_steps, nblk, gpb * p)], axis=-1)
    ap = jnp.transpose(ap, (1, 0, 2))
    dvec = jnp.tile(d_skip.astype(F32).reshape(nblk, 1, LANES), (1, 1, t))
    return kstack.astype(BF16), bm.astype(BF16), cm.astype(BF16), ap, dvec


def _s5_body(x_ref, ks_ref, bm_ref, cm_ref, ap_ref, d_ref, y_ref, h_ref):
    nc = x_ref.shape[2]
    t = SSM_CHUNK
    pad = h_ref.shape[0] - nc
    half = h_ref.shape[1] // 2
    x = x_ref[0, 0]
    h_ref[0:pad] = jnp.zeros((pad, h_ref.shape[1]), F32)
    h_ref[pad:pad + nc] = _dot(x, bm_ref[0])
    for step in range(ap_ref.shape[1]):
        d = 1 << step
        cur = h_ref[pad:pad + nc]
        sft = h_ref[pad - d:pad + nc - d]
        ar = ap_ref[0, step:step + 1, 0:half]
        ai = ap_ref[0, step:step + 1, half:]
        cr, ci = cur[:, :half], cur[:, half:]
        sr, si = sft[:, :half], sft[:, half:]
        h_ref[pad:pad + nc, 0:half] = cr + ar * sr - ai * si
        h_ref[pad:pad + nc, half:] = ci + ar * si + ai * sr
    h_prev = h_ref[pad - 1:pad + nc - 1]
    ys = _dot(h_prev.astype(BF16), cm_ref[0])
    for t0 in range(0, t, 2):
        lo, hi = t0 * LANES, (t0 + 2) * LANES
        conv = _dot(x[:, :hi], ks_ref[0, (t - 2 - t0) * LANES:, :])
        y = conv + ys[:, lo:hi] + d_ref[0][:, lo:hi] * x[:, lo:hi].astype(F32)
        y_ref[0, 0, :, lo:hi] = y.astype(y_ref.dtype)


def _s5(u4, kstack, bm, cm, ap, dvec):
    nblk, bsz, nc, w = u4.shape
    sw = bm.shape[-1]
    pad = max(nc // 2, 8)
    return pl.pallas_call(
        _s5_body,
        grid=(nblk, bsz),
        in_specs=[pl.BlockSpec((1, 1, nc, w), lambda j, b: (j, b, 0, 0)),
                  pl.BlockSpec((1,) + kstack.shape[1:], lambda j, b: (j, 0, 0)),
                  pl.BlockSpec((1,) + bm.shape[1:], lambda j, b: (j, 0, 0)),
                  pl.BlockSpec((1,) + cm.shape[1:], lambda j, b: (j, 0, 0)),
                  pl.BlockSpec((1,) + ap.shape[1:], lambda j, b: (j, 0, 0)),
                  pl.BlockSpec((1, 1, w), lambda j, b: (j, 0, 0))],
        out_specs=pl.BlockSpec((1, 1, nc, w), lambda j, b: (j, b, 0, 0)),
        out_shape=jax.ShapeDtypeStruct((nblk, bsz, nc, w), BF16),
        scratch_shapes=[pltpu.VMEM((pad + nc, sw), F32)],
        compiler_params=_cparams("parallel", "parallel"),
        name="s5",
    )(u4, kstack, bm, cm, ap, dvec)


def _attn_body(q_ref, k_ref, v_ref, o_ref, vt_ref, km_ref, bias_ref, sa_ref, sb_ref):
    qi = pl.program_id(2)
    seq = k_ref.shape[1]
    tq = q_ref.shape[1]
    nb = seq // MOBA_BLOCK
    n_kv = seq // KV_TILE
    bpt = KV_TILE // MOBA_BLOCK
    blk_shift = MOBA_BLOCK.bit_length() - 1

    @pl.when(qi == 0)
    def _build():
        k = k_ref[0].astype(F32)
        v = v_ref[0].astype(F32)
        ones = jnp.where(lax.broadcasted_iota(I32, (PV_ROWS - HEAD_DIM, KV_TILE), 0) == 0, 1.0, 0.0)
        for t in range(n_kv):
            v_t = v[t * KV_TILE:(t + 1) * KV_TILE].T
            vt_ref[0, t] = jnp.concatenate([v_t[:HEAD_DIM], ones], axis=0).astype(BF16)
            vt_ref[1, t] = jnp.concatenate([v_t[HEAD_DIM:], ones], axis=0).astype(BF16)
        km = jnp.mean(k.reshape(nb, MOBA_BLOCK, LANES), axis=1)
        lane_b = lax.broadcasted_iota(I32, (nb, LANES), 1)
        km_ref[0] = jnp.where(lane_b < HEAD_DIM, km, 0.0)
        km_ref[1] = jnp.where(lane_b >= HEAD_DIM, km, 0.0)

    q_t = q_ref[0].astype(F32).T
    feat = lax.broadcasted_iota(I32, (LANES, tq), 0)
    q_h = [jnp.where(feat < HEAD_DIM, q_t, 0.0).astype(BF16),
           jnp.where(feat >= HEAD_DIM, q_t, 0.0).astype(BF16)]

    jidx = lax.broadcasted_iota(I32, (nb, tq), 0)
    qblk = qi * (tq // MOBA_BLOCK) + lax.shift_right_logical(
        lax.broadcasted_iota(I32, (nb, tq), 1), blk_shift)
    for hx in range(2):
        g = _dot(km_ref[hx], q_t)
        cnt = jnp.zeros((nb, tq), F32)
        for jp in range(nb):
            row = g[jp:jp + 1, :]
            beats = jnp.where(row > g, 1.0, jnp.where(row == g, jnp.where(jp < jidx, 1.0, 0.0), 0.0))
            cnt = cnt + jnp.where(jp < qblk, beats, 0.0)
        past_sel = jnp.where(jidx < qblk, jnp.where(cnt < MOBA_TOPK, 1.0, 0.0), 0.0)
        sel = jnp.where(jidx == qblk, 1.0, past_sel)
        bias_ref[hx] = jnp.where(sel > 0.5, 0.0, NEG_INF)

    def raw_scores(dst_ref, tile):
        off = pl.multiple_of(tile * KV_TILE, KV_TILE)
        k_t = k_ref[0, pl.ds(off, KV_TILE), :]
        for hx in range(2):
            dst_ref[hx] = _dot(k_t, q_h[hx])

    def fold(carry, src_ref, tile, causal=None):
        out = []
        for hx in range(2):
            parts, biases = [], []
            for j in range(bpt):
                s_j = src_ref[hx, j * MOBA_BLOCK:(j + 1) * MOBA_BLOCK, :]
                if causal is not None:
                    s_j = jnp.where(causal[j], s_j, NEG_INF)
                b_j = bias_ref[hx, pl.ds(tile * bpt + j, 1), :]
                parts.append(s_j)
                biases.append(b_j)
            m_n = jnp.max(parts[0], axis=0, keepdims=True) + biases[0]
            for s_j, b_j in zip(parts[1:], biases[1:]):
                m_n = jnp.maximum(m_n, jnp.max(s_j, axis=0, keepdims=True) + b_j)
            if carry is None:
                p_t = jnp.concatenate([jnp.exp2(s_j + (b_j - m_n)) for s_j, b_j in zip(parts, biases)], axis=0)
                out += [m_n, _dot(vt_ref[hx, tile], p_t.astype(BF16))]
            else:
                m_c, acc_c = carry[2 * hx], carry[2 * hx + 1]
                m_n = jnp.maximum(m_c, m_n)
                p_t = jnp.concatenate([jnp.exp2(s_j + (b_j - m_n)) for s_j, b_j in zip(parts, biases)], axis=0)
                out += [m_n, acc_c * jnp.exp2(m_c - m_n) + _dot(vt_ref[hx, tile], p_t.astype(BF16))]
        return tuple(out)

    raw_scores(sa_ref, qi)
    raw_scores(sb_ref, 0)
    krow = lax.broadcasted_iota(I32, (MOBA_BLOCK, tq), 0)
    qcol = lax.broadcasted_iota(I32, (MOBA_BLOCK, tq), 1)
    state = fold(None, sa_ref, qi, causal=[krow + j * MOBA_BLOCK <= qcol for j in range(bpt)])

    def body(i, carry):
        t0 = 2 * i
        raw_scores(sa_ref, t0 + 1)
        carry = fold(carry, sb_ref, t0)
        raw_scores(sb_ref, jnp.minimum(t0 + 2, n_kv - 1))
        return fold(carry, sa_ref, t0 + 1)

    state = lax.fori_loop(0, qi // 2, body, state)
    _, acc_a, _, acc_b = lax.cond(lax.rem(qi, 2) == 1, lambda c: fold(c, sb_ref, qi - 1), lambda c: c, state)
    o_t = jnp.concatenate([acc[:HEAD_DIM] / acc[HEAD_DIM:HEAD_DIM + 1] for acc in (acc_a, acc_b)], axis=0)
    o_ref[0] = o_t.T.astype(o_ref.dtype)


def _attention(q, k, v):
    bsz, seq, aw = q.shape
    npair = aw // LANES
    tq = KV_TILE
    nb = seq // MOBA_BLOCK
    assert seq % KV_TILE == 0 and nb % 8 == 0
    return pl.pallas_call(
        _attn_body,
        grid=(bsz, npair, seq // tq),
        in_specs=[pl.BlockSpec((1, tq, LANES), lambda b, p, i: (b, i, p)),
                  pl.BlockSpec((1, seq, LANES), lambda b, p, i: (b, 0, p)),
                  pl.BlockSpec((1, seq, LANES), lambda b, p, i: (b, 0, p))],
        out_specs=pl.BlockSpec((1, tq, LANES), lambda b, p, i: (b, i, p)),
        out_shape=jax.ShapeDtypeStruct((bsz, seq, aw), BF16),
        scratch_shapes=[pltpu.VMEM((2, seq // KV_TILE, PV_ROWS, KV_TILE), BF16),
                        pltpu.VMEM((2, nb, LANES), F32),
                        pltpu.VMEM((2, nb, tq), F32),
                        pltpu.VMEM((2, KV_TILE, tq), F32),
                        pltpu.VMEM((2, KV_TILE, tq), F32)],
        compiler_params=_cparams("parallel", "parallel", "arbitrary"),
        name="attn",
    )(q, k, v)


def _merge_body(y_ref, a_ref, x_ref, wglu_ref, bglu_ref, gs_ref, ga_ref, wout_ref, gm_ref,
                gf_ref, shf_ref, scf_ref, rw_ref, rb_ref, x1_ref, h2_ref, idx_ref, wt_ref, cnt_ref, ys_ref):
    nblk = y_ref.shape[0]
    tm = x_ref.shape[1]
    for j in range(nblk):
        for t in range(SSM_CHUNK):
            ys_ref[j, pl.ds(t, tm // SSM_CHUNK, stride=SSM_CHUNK), :] = (
                y_ref[j, 0, :, t * LANES:(t + 1) * LANES].astype(F32))
    y = jnp.concatenate([ys_ref[j] for j in range(nblk)], axis=1)
    g = 0.5 * y * (1.0 + lax.erf(y * (2.0 ** -0.5)))
    glu = g * jax.nn.sigmoid(_dot(g.astype(BF16), wglu_ref[...]) + bglu_ref[...])
    ssm_n = _rms(glu, gs_ref[...])
    att_n = _rms(a_ref[0].astype(F32), ga_ref[...])
    merged = jnp.concatenate([ssm_n, att_n], axis=1).astype(BF16)
    mix = _dot(merged, wout_ref[...])
    x1 = x_ref[0] + gm_ref[0] * mix
    x1_ref[0] = x1
    h2 = _rms(x1, gf_ref[...]) * (1.0 + scf_ref[0]) + shf_ref[0]
    h2_ref[0] = h2
    logits = lax.dot_general(rw_ref[...], h2, (((1,), (1,)), ((), ())),
                             preferred_element_type=F32, precision=HIGHEST) + rb_ref[...]
    ne = logits.shape[0]
    sub = lax.broadcasted_iota(I32, (ne, tm), 0)
    vals, idxs = [], []
    for _ in range(TOP_K):
        mx = jnp.max(logits, axis=0, keepdims=True)
        ix = jnp.min(jnp.where(logits == mx, sub, ne), axis=0, keepdims=True)
        vals.append(mx)
        idxs.append(ix)
        logits = jnp.where(sub == ix, -jnp.inf, logits)
    exps = [jnp.exp(vv - vals[0]) for vv in vals]
    tot = exps[0]
    for e in exps[1:]:
        tot = tot + e
    row_i = lax.broadcasted_iota(I32, idx_ref.shape, 0)
    row_w = lax.broadcasted_iota(I32, wt_ref.shape, 0)
    idx_out = jnp.zeros(idx_ref.shape, I32)
    wt_out = jnp.zeros(wt_ref.shape, F32)
    for kk in range(TOP_K):
        idx_out = jnp.where(row_i == kk, idxs[kk], idx_out)
        wt_out = jnp.where(row_w == kk, exps[kk] / tot, wt_out)
    idx_ref[...] = idx_out
    wt_ref[...] = wt_out
    chosen = jnp.zeros((ne, tm), F32)
    for kk in range(TOP_K):
        chosen = chosen + jnp.where(sub == idxs[kk], 1.0, 0.0)
    cnt_ref[0] = jnp.sum(chosen, axis=1, keepdims=True).astype(I32)


def _merge(y4, attn, x, w_glu_bf, b_glu, g_ssm, g_attn, w_out_bf, mod3, g_ffn, router_w, router_b, tm=512):
    bsz, seq, d = x.shape
    nblk = y4.shape[0]
    ssm_w = nblk * LANES
    aw = attn.shape[-1]
    ne = router_w.shape[1]
    tm = min(tm, seq)
    spb = seq // tm
    n_tok = bsz * seq
    rw_t = router_w.T
    row = lambda n: pl.BlockSpec((1, n), lambda b, i: (0, 0))
    full = lambda a: pl.BlockSpec(a.shape, lambda b, i: (0,) * a.ndim)
    modv = lambda j: pl.BlockSpec((1, 1, d), lambda b, i: (b, 0, j))
    return pl.pallas_call(
        _merge_body,
        grid=(bsz, seq // tm),
        in_specs=[pl.BlockSpec((nblk, 1, tm // SSM_CHUNK, SSM_CHUNK * LANES), lambda b, i: (0, b, i, 0)),
                  pl.BlockSpec((1, tm, aw), lambda b, i: (b, i, 0)),
                  pl.BlockSpec((1, tm, d), lambda b, i: (b, i, 0)),
                  full(w_glu_bf), row(ssm_w), row(ssm_w), row(aw), full(w_out_bf),
                  modv(2), row(d), modv(3), modv(4), full(rw_t),
                  pl.BlockSpec((ne, 1), lambda b, i: (0, 0))],
        out_specs=[pl.BlockSpec((1, tm, d), lambda b, i: (b, i, 0)),
                   pl.BlockSpec((1, tm, d), lambda b, i: (b, i, 0)),
                   pl.BlockSpec((TOP_K, tm), lambda b, i: (0, b * spb + i)),
                   pl.BlockSpec((WT_ROWS, tm), lambda b, i: (0, b * spb + i)),
                   pl.BlockSpec((1, ne, 1), lambda b, i: (b * spb + i, 0, 0))],
        out_shape=[jax.ShapeDtypeStruct((bsz, seq, d), F32),
                   jax.ShapeDtypeStruct((bsz, seq, d), F32),
                   jax.ShapeDtypeStruct((TOP_K, n_tok), I32),
                   jax.ShapeDtypeStruct((WT_ROWS, n_tok), F32),
                   jax.ShapeDtypeStruct((bsz * spb, ne, 1), I32)],
        scratch_shapes=[pltpu.VMEM((nblk, tm, LANES), F32)],
        compiler_params=_cparams("parallel", "parallel"),
        name="merge",
    )(y4, attn, x, w_glu_bf, b_glu.reshape(1, ssm_w), g_ssm.reshape(1, ssm_w), g_attn.reshape(1, aw),
      w_out_bf, mod3, g_ffn.reshape(1, d), mod3, mod3, rw_t, router_b.reshape(ne, 1))


def _segment_copies(n_seg, meta, make_copy):
    def pieces(src0, dst0, units, sizes):
        for sz in sizes:
            off = (units & ~(2 * sz - 1)) * SEG_ALIGN
            src = pl.multiple_of(src0 + off, SEG_ALIGN)
            dst = pl.multiple_of(dst0 + off, SEG_ALIGN)

            @pl.when((units & sz) != 0)
            def _(src=src, dst=dst, sz=sz):
                make_copy(src, dst, sz * SEG_ALIGN).start()

    def body(e, _):
        src0, dst0, units = meta(e)

        @pl.when(units >= SEG_UNITS[SEG_RARE - 1])
        def _():
            pieces(src0, dst0, units, SEG_UNITS[:SEG_RARE])

        pieces(src0, dst0, units, SEG_UNITS[SEG_RARE:])
        return 0

    lax.fori_loop(0, n_seg, body, 0)


def _local_slots(idx, lstart_col, ne):
    tb = idx.shape[1]
    sub = lax.broadcasted_iota(I32, (ne, tb), 0)
    hits = [idx[kk:kk + 1, :] == sub for kk in range(TOP_K)]
    onehot = jnp.zeros((ne, tb), F32)
    for h in hits:
        onehot = onehot + jnp.where(h, 1.0, 0.0)
    r = lax.broadcasted_iota(I32, (tb, tb), 0)
    c = lax.broadcasted_iota(I32, (tb, tb), 1)
    tri = jnp.where(r < c, 1.0, 0.0).astype(BF16)
    slot = _dot(onehot.astype(BF16), tri) + lstart_col
    return [jnp.sum(jnp.where(h, slot, 0.0), axis=0, keepdims=True) for h in hits]


def _dispatch_body(lstart_s, pcnt_s, base_s, idx_ref, lcol_ref, h_ref, xs_ref, ls_ref, sorted_ref, sem, *, ne):
    i = pl.program_id(0)
    nt = pl.num_programs(0)
    tb = h_ref.shape[0]
    lb = sorted_ref.shape[1]
    slot = lax.rem(i, 2)

    def tile_rows(t):
        return pl.multiple_of(lstart_s[t * ne + ne - 1] + pcnt_s[t * ne + ne - 1], SEG_ALIGN)

    def wait_tile(t, sl):
        done = xs_ref.at[pl.ds(0, tile_rows(t)), :]
        pltpu.make_async_copy(done, done, sem.at[sl]).wait()

    @pl.when(i >= 2)
    def _():
        wait_tile(i - 2, slot)

    slots = _local_slots(idx_ref[...], lcol_ref[0].astype(F32), ne)
    row_o = lax.broadcasted_iota(I32, ls_ref.shape, 0)
    ls_out = jnp.full(ls_ref.shape, -1.0, F32)
    for kk in range(TOP_K):
        ls_out = jnp.where(row_o == kk, slots[kk], ls_out)
    ls_ref[...] = ls_out.astype(I32)
    h = h_ref[...].astype(BF16)
    for r0 in range(0, lb, SORT_CHUNK):
        j = (lax.broadcasted_iota(I32, (SORT_CHUNK, tb), 0) + r0).astype(F32)
        pm = jnp.zeros((SORT_CHUNK, tb), F32)
        for kk in range(TOP_K):
            pm = jnp.where(j == slots[kk], 1.0, pm)
        sorted_ref[slot, r0:r0 + SORT_CHUNK, :] = _dot(pm.astype(BF16), h)

    def meta(e):
        return lstart_s[i * ne + e], base_s[i * ne + e], pcnt_s[i * ne + e] // SEG_ALIGN

    def make_copy(src, dst, rows):
        return pltpu.make_async_copy(sorted_ref.at[slot, pl.ds(src, rows), :], xs_ref.at[pl.ds(dst, rows), :],
                                     sem.at[slot])

    _segment_copies(ne, meta, make_copy)

    @pl.when(i == nt - 1)
    def _():
        @pl.when(i >= 1)
        def _():
            wait_tile(i - 1, 1 - slot)
        wait_tile(i, slot)


def _dispatch(h2, idx_t, plan, cap, tb):
    n, d = h2.shape
    lstart, pcnt, base = plan
    nt, ne = lstart.shape
    lb = tb * TOP_K + ne * SEG_ALIGN
    assert lb % SORT_CHUNK == 0
    return pl.pallas_call(
        functools.partial(_dispatch_body, ne=ne),
        grid_spec=pltpu.PrefetchScalarGridSpec(
            num_scalar_prefetch=3,
            grid=(nt,),
            in_specs=[pl.BlockSpec((TOP_K, tb), lambda i, *_: (0, i)),
                      pl.BlockSpec((1, ne, 1), lambda i, *_: (i, 0, 0)),
                      pl.BlockSpec((tb, d), lambda i, *_: (i, 0))],
            out_specs=[pl.BlockSpec(memory_space=pl.ANY),
                       pl.BlockSpec((WT_ROWS, tb), lambda i, *_: (0, i))],
            scratch_shapes=[pltpu.VMEM((2, lb, d), F32), pltpu.SemaphoreType.DMA((2,))],
        ),
        out_shape=[jax.ShapeDtypeStruct((cap, d), F32),
                   jax.ShapeDtypeStruct((WT_ROWS, n), I32)],
        compiler_params=_cparams("arbitrary"),
        name="dispatch",
    )(lstart.reshape(-1), pcnt.reshape(-1), base.reshape(-1), idx_t, lstart.reshape(nt, ne, 1), h2)


def _route_plan(tile_cnt, tg, cap):
    cnt = tile_cnt[:, :, 0]
    ne = cnt.shape[1]
    pcnt = (cnt + SEG_ALIGN - 1) // SEG_ALIGN * SEG_ALIGN
    lstart = jnp.cumsum(pcnt, axis=1) - pcnt
    e_rows = jnp.sum(pcnt, axis=0)
    e_tiles = (e_rows + tg - 1) // tg
    e_end = jnp.cumsum(e_tiles)
    base = ((e_end - e_tiles) * tg)[None, :] + jnp.cumsum(pcnt, axis=0) - pcnt
    ii = jnp.arange(cap // tg, dtype=I32)
    used = ii < e_end[-1]
    tile = jnp.minimum(ii, e_end[-1] - 1)
    e_of = jnp.minimum(jnp.sum((e_end[None, :] <= tile[:, None]).astype(I32), axis=1), ne - 1)
    newe = (e_of != jnp.concatenate([jnp.full((1,), -1, I32), e_of[:-1]])).astype(I32)
    rows = jnp.clip(e_rows[e_of] - (tile - (e_end - e_tiles)[e_of]) * tg, 0, tg)
    rows = jnp.where(used, rows, 0)
    items = (tile.astype(I32), e_of.astype(I32), rows.astype(I32), newe)
    return (lstart.astype(I32), pcnt.astype(I32), base.astype(I32)), items


def _experts_body(tile_ref, exp_ref, rows_ref, newe_ref,
                  xs_ref, wg_ref, bg_ref, wu_ref, bu_ref, wd_ref, bd_ref, ys_ref,
                  wg_bf, wu_bf, wd_bf):
    i = pl.program_id(0)

    @pl.when(newe_ref[i] == 1)
    def _cast():
        wg_bf[...] = wg_ref[0].astype(BF16)
        wu_bf[...] = wu_ref[0].astype(BF16)
        wd_bf[...] = wd_ref[0].astype(BF16)

    @pl.when(rows_ref[i] > 0)
    def _compute():
        row = lax.broadcasted_iota(I32, xs_ref.shape, 0)
        x = jnp.where(row < rows_ref[i], xs_ref[...], 0.0).astype(BF16)
        gate = jnp.minimum(_dot(x, wg_bf[...]) + bg_ref[0], SWIGLU_LIMIT)
        lin = jnp.clip(_dot(x, wu_bf[...]) + bu_ref[0], -SWIGLU_LIMIT, SWIGLU_LIMIT)
        act = gate * jax.nn.sigmoid(SWIGLU_ALPHA * gate) * (lin + 1.0)
        ys_ref[...] = _dot(act.astype(BF16), wd_bf[...]) + bd_ref[0]


def _experts(xs, items, w_gate, b_gate, w_up, b_up, w_down, b_down, tg):
    nk, d = xs.shape
    ne, _, dff = w_gate.shape
    n_items = items[0].shape[0]
    wspec = lambda shp: pl.BlockSpec((1,) + shp, lambda i, t, e, u, nw: (e[i], 0, 0))
    return pl.pallas_call(
        _experts_body,
        grid_spec=pltpu.PrefetchScalarGridSpec(
            num_scalar_prefetch=4,
            grid=(n_items,),
            in_specs=[pl.BlockSpec((tg, d), lambda i, t, e, u, nw: (t[i], 0)),
                      wspec((d, dff)), wspec((1, dff)),
                      wspec((d, dff)), wspec((1, dff)),
                      wspec((dff, d)), wspec((1, d))],
            out_specs=pl.BlockSpec((tg, d), lambda i, t, e, u, nw: (t[i], 0)),
            scratch_shapes=[pltpu.VMEM((d, dff), BF16), pltpu.VMEM((d, dff), BF16), pltpu.VMEM((dff, d), BF16)],
        ),
        out_shape=jax.ShapeDtypeStruct((nk, d), F32),
        compiler_params=_cparams("arbitrary"),
        name="experts",
    )(*items, xs, w_gate, b_gate.reshape(ne, 1, dff), w_up, b_up.reshape(ne, 1, dff),
      w_down, b_down.reshape(ne, 1, d))


def _combine_body(lstart_s, pcnt_s, base_s, ls_ref, wt_ref, x1_ref, gf_ref, gfin_ref, sho_ref, sco_ref,
                  ys_ref, o_ref, buf, sem, *, ne, spb):
    i = pl.program_id(0) * spb + pl.program_id(1)
    nt = pl.num_programs(0) * spb
    tb = x1_ref.shape[1]
    lb = buf.shape[1]
    slot = lax.rem(i, 2)

    def fetch(t, sl):
        def meta(e):
            return base_s[t * ne + e], lstart_s[t * ne + e], pcnt_s[t * ne + e] // SEG_ALIGN

        def make_copy(src, dst, rows):
            return pltpu.make_async_copy(ys_ref.at[pl.ds(src, rows), :], buf.at[sl, pl.ds(dst, rows), :],
                                         sem.at[sl])

        _segment_copies(ne, meta, make_copy)

    @pl.when(i == 0)
    def _():
        buf[...] = jnp.zeros(buf.shape, F32)
        fetch(0, 0)

    @pl.when(i + 1 < nt)
    def _():
        fetch(i + 1, 1 - slot)

    total = pl.multiple_of(lstart_s[i * ne + ne - 1] + pcnt_s[i * ne + ne - 1], SEG_ALIGN)
    done = buf.at[slot, pl.ds(0, total), :]
    pltpu.make_async_copy(done, done, sem.at[slot]).wait()
    ls_col = ls_ref[...].astype(F32).T
    wt_col = wt_ref[...].T
    ffn = jnp.zeros((tb, x1_ref.shape[2]), F32)
    for r0 in range(0, lb, SORT_CHUNK):
        j = (lax.broadcasted_iota(I32, (tb, SORT_CHUNK), 1) + r0).astype(F32)
        wm = jnp.zeros((tb, SORT_CHUNK), F32)
        for kk in range(TOP_K):
            wm = jnp.where(j == ls_col[:, kk:kk + 1], wt_col[:, kk:kk + 1], wm)
        ffn = ffn + _dot(wm.astype(BF16), buf[slot, r0:r0 + SORT_CHUNK, :].astype(BF16))
    x2 = x1_ref[0] + gf_ref[0] * ffn
    o_ref[0] = _rms(x2, gfin_ref[...]) * (1.0 + sco_ref[0]) + sho_ref[0]


def _combine(plan, ls_t, wt_t, x1, mod3, g_final, fmod3, ys, tb):
    bsz, seq, d = x1.shape
    lstart, pcnt, base = plan
    nt, ne = lstart.shape
    spb = seq // tb
    lb = tb * TOP_K + ne * SEG_ALIGN
    return pl.pallas_call(
        functools.partial(_combine_body, ne=ne, spb=spb),
        grid_spec=pltpu.PrefetchScalarGridSpec(
            num_scalar_prefetch=3,
            grid=(bsz, spb),
            in_specs=[pl.BlockSpec((WT_ROWS, tb), lambda b, i, *_: (0, b * spb + i)),
                      pl.BlockSpec((WT_ROWS, tb), lambda b, i, *_: (0, b * spb + i)),
                      pl.BlockSpec((1, tb, d), lambda b, i, *_: (b, i, 0)),
                      pl.BlockSpec((1, 1, d), lambda b, i, *_: (b, 0, 5)),
                      pl.BlockSpec((1, d), lambda b, i, *_: (0, 0)),
                      pl.BlockSpec((1, 1, d), lambda b, i, *_: (b, 0, 0)),
                      pl.BlockSpec((1, 1, d), lambda b, i, *_: (b, 0, 1)),
                      pl.BlockSpec(memory_space=pl.ANY)],
            out_specs=pl.BlockSpec((1, tb, d), lambda b, i, *_: (b, i, 0)),
            scratch_shapes=[pltpu.VMEM((2, lb, d), F32), pltpu.SemaphoreType.DMA((2,))],
        ),
        out_shape=jax.ShapeDtypeStruct((bsz, seq, d), F32),
        compiler_params=_cparams("arbitrary", "arbitrary"),
        name="combine",
    )(lstart.reshape(-1), pcnt.reshape(-1), base.reshape(-1), ls_t, wt_t, x1, mod3, g_final.reshape(1, d),
      fmod3, fmod3, ys)


def kernel(x, c, positions, ada_w, ada_b, final_ada_w, final_ada_b, norm_mix_g, norm_ffn_g, norm_final_g, w_in, ssm_lambda_re, ssm_lambda_im, ssm_log_dt, ssm_b_re, ssm_b_im, ssm_c_re, ssm_c_im, ssm_d, ssm_w_glu, ssm_b_glu, out_norm_ssm_g, out_norm_attn_g, w_out, router_w, router_b, exp_w_gate, exp_b_gate, exp_w_up, exp_b_up, exp_w_down, exp_b_down):
    bsz, seq, d = x.shape
    depth = ada_w.shape[0]
    ssm_w = ssm_d.shape[-1]
    attn_w = (w_in.shape[-1] - ssm_w) // 3
    ne = router_w.shape[-1]
    n_tok = bsz * seq
    nc = seq // SSM_CHUNK
    tg = 512

    rope_tab, rope_exp = _rope_tables(positions)
    fmod3 = _adaln(c, final_ada_w, final_ada_b).reshape(bsz, 1, 2 * d)
    for l in range(depth):
        mod3 = _adaln(c, ada_w[l], ada_b[l]).reshape(bsz, 1, -1)
        u4, q, k, v = _inproj(x, norm_mix_g[l], mod3, w_in[l].astype(BF16), rope_tab, rope_exp, ssm_w, attn_w)
        tables = _s5_params(ssm_lambda_re[l], ssm_lambda_im[l], ssm_log_dt[l], ssm_b_re[l], ssm_b_im[l],
                            ssm_c_re[l], ssm_c_im[l], ssm_d[l], nc)
        y4 = _s5(u4, *tables)
        attn = _attention(q, k, v)
        x1, h2, idx_t, wt_t, tile_cnt = _merge(y4, attn, x, ssm_w_glu[l].astype(BF16), ssm_b_glu[l], out_norm_ssm_g[l],
                                     out_norm_attn_g[l], w_out[l].astype(BF16), mod3, norm_ffn_g[l],
                                     router_w[l], router_b[l])
        tb = seq // (tile_cnt.shape[0] // bsz)
        cap = n_tok * TOP_K + tile_cnt.shape[0] * ne * SEG_ALIGN + ne * tg
        assert cap % tg == 0 and tb // SEG_ALIGN < 2 * SEG_UNITS[0]
        plan, items = _route_plan(tile_cnt, tg, cap)
        xs, ls_t = _dispatch(h2.reshape(n_tok, d), idx_t, plan, cap, tb)
        ys = _experts(xs, items, exp_w_gate[l], exp_b_gate[l], exp_w_up[l], exp_b_up[l],
                      exp_w_down[l], exp_b_down[l], tg)
        if l + 1 < depth:
            raise NotImplementedError("depth > 1 needs the non-final combine")
        x = _combine(plan, ls_t, wt_t, x1, mod3, norm_final_g, fmod3, ys, tb)
    return x
```

```python
import functools
import math

import jax
import jax.numpy as jnp
from jax import lax
from jax.experimental import pallas as pl
from jax.experimental.pallas import tpu as pltpu

F32 = jnp.float32
BF16 = jnp.bfloat16
I32 = jnp.int32
HIGHEST = lax.Precision.HIGHEST

LANES = 128
HEAD_DIM = 64
MOBA_BLOCK = 256
MOBA_TOPK = 3
ROT_DIM = HEAD_DIM // 4
ROPE_THETA = 500000.0
SSM_GROUP = 16
SSM_CHUNK = 16
TOP_K = 4
SWIGLU_ALPHA = 1.702
SWIGLU_LIMIT = 7.0
NORM_EPS = 1e-5
NEG_INF = -1e30
LOG2_E = math.log2(math.e)
KV_TILE = 2 * MOBA_BLOCK
PV_ROWS = HEAD_DIM + 8
WT_ROWS = 8
SEG_ALIGN = 8
SEG_UNITS = (64, 32, 16, 8, 4, 2, 1)
SORT_CHUNK = 256
VMEM_LIMIT = 56 * 1024 * 1024


def _cparams(*sem):
    return pltpu.CompilerParams(dimension_semantics=sem, vmem_limit_bytes=VMEM_LIMIT)


def _dot(a, b):
    return jnp.dot(a, b, preferred_element_type=F32)


def _rms(x, g):
    return x * lax.rsqrt(jnp.mean(x * x, axis=-1, keepdims=True) + NORM_EPS) * g


def _adaln_body(c_ref, w_ref, b_ref, o_ref):
    c = c_ref[...]
    ca = c * jax.nn.sigmoid(c)
    o_ref[...] = jnp.dot(ca, w_ref[...], preferred_element_type=F32, precision=HIGHEST) + b_ref[...]


def _adaln(c, w, b, tn=512):
    bsz, d = c.shape
    n = w.shape[1]
    return pl.pallas_call(
        _adaln_body,
        grid=(n // tn,),
        in_specs=[pl.BlockSpec((bsz, d), lambda j: (0, 0)),
                  pl.BlockSpec((d, tn), lambda j: (0, j)),
                  pl.BlockSpec((1, tn), lambda j: (0, j))],
        out_specs=pl.BlockSpec((bsz, tn), lambda j: (0, j)),
        out_shape=jax.ShapeDtypeStruct((bsz, n), F32),
        compiler_params=_cparams("parallel"),
        name="adaln",
    )(c, w, b.reshape(1, n))


def _inproj_body(x_ref, g_ref, sh_ref, sc_ref, w_ref, tab_ref, exp_ref, u_ref, q_ref, k_ref, v_ref, us_ref):
    x = x_ref[0]
    tm = x.shape[0]
    h = _rms(x, g_ref[...]) * (1.0 + sc_ref[0]) + sh_ref[0]
    proj = _dot(h.astype(BF16), w_ref[...])
    ssm_w = u_ref.shape[0] * LANES
    aw = q_ref.shape[2]
    for j in range(u_ref.shape[0]):
        us_ref[j] = proj[:, j * LANES:(j + 1) * LANES]
        for t in range(SSM_CHUNK):
            u_ref[j, 0, :, t * LANES:(t + 1) * LANES] = us_ref[
                j, pl.ds(t, tm // SSM_CHUNK, stride=SSM_CHUNK), :].astype(BF16)
    t_hi = tab_ref[0].astype(BF16)
    t_lo = (tab_ref[0] - t_hi.astype(F32)).astype(BF16)
    tn = (((0,), (0,)), ((), ()))
    tab = (lax.dot_general(t_hi, exp_ref[...], tn, preferred_element_type=F32)
           + lax.dot_general(t_lo, exp_ref[...], tn, preferred_element_type=F32))
    lane1 = lax.broadcasted_iota(I32, (tm, LANES), 1)
    cos1 = tab[:, :LANES] + jnp.where((lane1 & (HEAD_DIM - 1)) >= ROT_DIM, 1.0, 0.0)
    reps = aw // LANES
    cosf = jnp.concatenate([cos1] * reps, axis=1)
    sinf = jnp.concatenate([tab[:, LANES:]] * reps, axis=1)
    lane = lax.broadcasted_iota(I32, (tm, aw), 1)
    first_half = (lane & (HEAD_DIM - 1)) < (ROT_DIM // 2)

    def rope(t):
        partner = jnp.where(first_half, pltpu.roll(t, aw - ROT_DIM // 2, 1), pltpu.roll(t, ROT_DIM // 2, 1))
        return t * cosf + partner * sinf

    q = rope(proj[:, ssm_w:ssm_w + aw]) * (HEAD_DIM ** -0.5 * LOG2_E)
    k = rope(proj[:, ssm_w + aw:ssm_w + 2 * aw])
    q_ref[0] = q.astype(BF16)
    k_ref[0] = k.astype(BF16)
    v_ref[0] = proj[:, ssm_w + 2 * aw:ssm_w + 3 * aw].astype(BF16)


def _inproj(x, gain, mod3, w_in_bf, rope_tab, rope_exp, ssm_w, attn_w, tm=512):
    bsz, seq, d = x.shape
    n_u = ssm_w // LANES
    tm = min(tm, seq)
    return pl.pallas_call(
        _inproj_body,
        grid=(bsz, seq // tm),
        in_specs=[pl.BlockSpec((1, tm, d), lambda b, i: (b, i, 0)),
                  pl.BlockSpec((1, d), lambda b, i: (0, 0)),
                  pl.BlockSpec((1, 1, d), lambda b, i: (b, 0, 0)),
                  pl.BlockSpec((1, 1, d), lambda b, i: (b, 0, 1)),
                  pl.BlockSpec(w_in_bf.shape, lambda b, i: (0, 0)),
                  pl.BlockSpec((1, ROT_DIM, tm), lambda b, i: (b, 0, i)),
                  pl.BlockSpec(rope_exp.shape, lambda b, i: (0, 0))],
        out_specs=[pl.BlockSpec((n_u, 1, tm // SSM_CHUNK, SSM_CHUNK * LANES), lambda b, i: (0, b, i, 0)),
                   pl.BlockSpec((1, tm, attn_w), lambda b, i: (b, i, 0)),
                   pl.BlockSpec((1, tm, attn_w), lambda b, i: (b, i, 0)),
                   pl.BlockSpec((1, tm, attn_w), lambda b, i: (b, i, 0))],
        out_shape=[jax.ShapeDtypeStruct((n_u, bsz, seq // SSM_CHUNK, SSM_CHUNK * LANES), BF16),
                   jax.ShapeDtypeStruct((bsz, seq, attn_w), BF16),
                   jax.ShapeDtypeStruct((bsz, seq, attn_w), BF16),
                   jax.ShapeDtypeStruct((bsz, seq, attn_w), BF16)],
        scratch_shapes=[pltpu.VMEM((n_u, tm, LANES), F32)],
        compiler_params=_cparams("parallel", "parallel"),
        name="inproj",
    )(x, gain.reshape(1, d), mod3, mod3, w_in_bf, rope_tab, rope_exp)


def _rope_tables(positions):
    half = ROT_DIM // 2
    inv_freq = ROPE_THETA ** (-jnp.arange(0, ROT_DIM, 2, dtype=F32) / ROT_DIM)
    ang = positions.astype(F32)[:, None, :] * inv_freq[None, :, None]
    tab = jnp.concatenate([jnp.cos(ang), jnp.sin(ang)], axis=1)
    i = jnp.arange(ROT_DIM)[:, None]
    hl = jnp.arange(LANES)[None, :] % HEAD_DIM
    e_cos = jnp.where((i < half) & ((hl == i) | (hl == i + half)), 1.0, 0.0)
    e_sin = jnp.where(i >= half, jnp.where(hl == i - half, -1.0, jnp.where(hl == i, 1.0, 0.0)), 0.0)
    return tab, jnp.concatenate([e_cos, e_sin], axis=1).astype(BF16)


def _s5_params(lam_re, lam_im, log_dt, b_re, b_im, c_re, c_im, d_skip, n_chunks):
    g_all, p = lam_re.shape
    hc = b_re.shape[-1]
    t = SSM_CHUNK
    gpb = LANES // hc
    nblk = g_all // gpb
    lr, li = lam_re.astype(F32), lam_im.astype(F32)
    dt = jnp.exp(log_dt.astype(F32))[:, None]
    ldr, ldi = lr * dt, li * dt

    def apow(k):
        k = jnp.asarray(k, F32)[..., None, None]
        mag = jnp.exp(ldr * k)
        return mag * jnp.cos(ldi * k), mag * jnp.sin(ldi * k)

    ar, ai = apow(1.0)
    zr, zi = ar - 1.0, ai
    den = lr * lr + li * li
    cr = (zr * lr + zi * li) / den
    ci = (zi * lr - zr * li) / den
    bre, bim = b_re.astype(F32), b_im.astype(F32)
    bbr = cr[..., None] * bre - ci[..., None] * bim
    bbi = cr[..., None] * bim + ci[..., None] * bre
    cre, cim = c_re.astype(F32), c_im.astype(F32)
    def block_diag(a):
        w = a.shape[-1]
        wide = jnp.concatenate([a] * gpb, axis=-1)
        grp = jnp.arange(gpb)[:, None, None]
        lane_grp = (jnp.arange(gpb * w) // w)[None, None, :]
        wide = jnp.where(lane_grp == grp, wide, 0.0)
        return wide.reshape(a.shape[:-3] + (gpb * a.shape[-2], gpb * w))

    pr, pi = apow(jnp.arange(t))
    abr = pr[..., None] * bbr - pi[..., None] * bbi
    abi = pr[..., None] * bbi + pi[..., None] * bbr
    kf = (jnp.einsum('gnp,kgph->kghn', cre, abr, precision=HIGHEST)
          - jnp.einsum('gnp,kgph->kghn', cim, abi, precision=HIGHEST))
    kblk = block_diag(kf.reshape(t, nblk, gpb, hc, hc))
    kpad = jnp.concatenate([jnp.zeros_like(kblk[:1]), kblk], axis=0)
    pm = jnp.concatenate([kpad[0:t], kpad[1:t + 1]], axis=-1)
    kstack = jnp.transpose(pm[::-1], (1, 0, 2, 3)).reshape(nblk, t * LANES, 2 * LANES)

    def inject(ab):
        ab = jnp.transpose(ab[::-1].reshape(t, nblk, gpb, p, hc), (1, 0, 2, 4, 3))
        return block_diag(ab).reshape(nblk, t * LANES, gpb * p)

    bm = jnp.concatenate([inject(abr), inject(abi)], axis=-1)

    qr, qi = apow(jnp.arange(t) + 1.0)
    c_from_re = cre[None] * qr[:, :, None, :] - cim[None] * qi[:, :, None, :]
    c_from_im = -cre[None] * qi[:, :, None, :] - cim[None] * qr[:, :, None, :]

    def readout_t(cc):
        cc = jnp.transpose(cc.reshape(t, nblk, gpb, hc, p), (1, 0, 2, 3, 4))
        return block_diag(cc).reshape(nblk, t * LANES, gpb * p)

    cm = jnp.swapaxes(jnp.concatenate([readout_t(c_from_re), readout_t(c_from_im)], axis=-1), 1, 2)

    n_steps = max(1, int(math.log2(n_chunks)))
    shifts = [float(t * (1 << s)) for s in range(n_steps)]
    sr, si = apow(jnp.asarray(shifts))
    ap = jnp.concatenate([sr.reshape(n_steps, nblk, gpb * p), si.reshape(n_steps, nblk, gpb * p)], axis=-1)
    ap = jnp.transpose(ap, (1, 0, 2))
    dvec = jnp.tile(d_skip.astype(F32).reshape(nblk, 1, LANES), (1, 1, t))
    return kstack.astype(BF16), bm.astype(BF16), cm.astype(BF16), ap, dvec


def _s5_body(x_ref, ks_ref, bm_ref, cm_ref, ap_ref, d_ref, y_ref, h_ref):
    nc = x_ref.shape[2]
    t = SSM_CHUNK
    pad = h_ref.shape[0] - nc
    half = h_ref.shape[1] // 2
    x = x_ref[0, 0]
    h_ref[0:pad] = jnp.zeros((pad, h_ref.shape[1]), F32)
    h_ref[pad:pad + nc] = _dot(x, bm_ref[0])
    for step in range(ap_ref.shape[1]):
        d = 1 << step
        cur = h_ref[pad:pad + nc]
        sft = h_ref[pad - d:pad + nc - d]
        ar = ap_ref[0, step:step + 1, 0:half]
        ai = ap_ref[0, step:step + 1, half:]
        cr, ci = cur[:, :half], cur[:, half:]
        sr, si = sft[:, :half], sft[:, half:]
        h_ref[pad:pad + nc, 0:half] = cr + ar * sr - ai * si
        h_ref[pad:pad + nc, half:] = ci + ar * si + ai * sr
    h_prev = h_ref[pad - 1:pad + nc - 1]
    ys = _dot(h_prev.astype(BF16), cm_ref[0])
    for t0 in range(0, t, 2):
        lo, hi = t0 * LANES, (t0 + 2) * LANES
        conv = _dot(x[:, :hi], ks_ref[0, (t - 2 - t0) * LANES:, :])
        y = conv + ys[:, lo:hi] + d_ref[0][:, lo:hi] * x[:, lo:hi].astype(F32)
        y_ref[0, 0, :, lo:hi] = y.astype(y_ref.dtype)


def _s5(u4, kstack, bm, cm, ap, dvec):
    nblk, bsz, nc, w = u4.shape
    sw = bm.shape[-1]
    pad = max(nc // 2, 8)
    return pl.pallas_call(
        _s5_body,
        grid=(nblk, bsz),
        in_specs=[pl.BlockSpec((1, 1, nc, w), lambda j, b: (j, b, 0, 0)),
                  pl.BlockSpec((1,) + kstack.shape[1:], lambda j, b: (j, 0, 0)),
                  pl.BlockSpec((1,) + bm.shape[1:], lambda j, b: (j, 0, 0)),
                  pl.BlockSpec((1,) + cm.shape[1:], lambda j, b: (j, 0, 0)),
                  pl.BlockSpec((1,) + ap.shape[1:], lambda j, b: (j, 0, 0)),
                  pl.BlockSpec((1, 1, w), lambda j, b: (j, 0, 0))],
        out_specs=pl.BlockSpec((1, 1, nc, w), lambda j, b: (j, b, 0, 0)),
        out_shape=jax.ShapeDtypeStruct((nblk, bsz, nc, w), BF16),
        scratch_shapes=[pltpu.VMEM((pad + nc, sw), F32)],
        compiler_params=_cparams("parallel", "parallel"),
        name="s5",
    )(u4, kstack, bm, cm, ap, dvec)


def _attn_body(q_ref, k_ref, v_ref, o_ref, vt_ref, km_ref, bias_ref, sa_ref, sb_ref):
    qi = pl.program_id(2)
    seq = k_ref.shape[1]
    tq = q_ref.shape[1]
    nb = seq // MOBA_BLOCK
    n_kv = seq // KV_TILE
    bpt = KV_TILE // MOBA_BLOCK
    blk_shift = MOBA_BLOCK.bit_length() - 1

    @pl.when(qi == 0)
    def _build():
        k = k_ref[0].astype(F32)
        v = v_ref[0].astype(F32)
        ones = jnp.where(lax.broadcasted_iota(I32, (PV_ROWS - HEAD_DIM, KV_TILE), 0) == 0, 1.0, 0.0)
        for t in range(n_kv):
            v_t = v[t * KV_TILE:(t + 1) * KV_TILE].T
            vt_ref[0, t] = jnp.concatenate([v_t[:HEAD_DIM], ones], axis=0).astype(BF16)
            vt_ref[1, t] = jnp.concatenate([v_t[HEAD_DIM:], ones], axis=0).astype(BF16)
        km = jnp.mean(k.reshape(nb, MOBA_BLOCK, LANES), axis=1)
        lane_b = lax.broadcasted_iota(I32, (nb, LANES), 1)
        km_ref[0] = jnp.where(lane_b < HEAD_DIM, km, 0.0)
        km_ref[1] = jnp.where(lane_b >= HEAD_DIM, km, 0.0)

    q_t = q_ref[0].astype(F32).T
    feat = lax.broadcasted_iota(I32, (LANES, tq), 0)
    q_h = [jnp.where(feat < HEAD_DIM, q_t, 0.0).astype(BF16),
           jnp.where(feat >= HEAD_DIM, q_t, 0.0).astype(BF16)]

    jidx = lax.broadcasted_iota(I32, (nb, tq), 0)
    qblk = qi * (tq // MOBA_BLOCK) + lax.shift_right_logical(
        lax.broadcasted_iota(I32, (nb, tq), 1), blk_shift)
    for hx in range(2):
        g = _dot(km_ref[hx], q_t)
        cnt = jnp.zeros((nb, tq), F32)
        for jp in range(nb):
            row = g[jp:jp + 1, :]
            beats = jnp.where(row > g, 1.0, jnp.where(row == g, jnp.where(jp < jidx, 1.0, 0.0), 0.0))
            cnt = cnt + jnp.where(jp < qblk, beats, 0.0)
        past_sel = jnp.where(jidx < qblk, jnp.where(cnt < MOBA_TOPK, 1.0, 0.0), 0.0)
        sel = jnp.where(jidx == qblk, 1.0, past_sel)
        bias_ref[hx] = jnp.where(sel > 0.5, 0.0, NEG_INF)

    def raw_scores(dst_ref, tile):
        off = pl.multiple_of(tile * KV_TILE, KV_TILE)
        k_t = k_ref[0, pl.ds(off, KV_TILE), :]
        for hx in range(2):
            dst_ref[hx] = _dot(k_t, q_h[hx])

    def fold(carry, src_ref, tile, causal=None):
        out = []
        for hx in range(2):
            parts, biases = [], []
            for j in range(bpt):
                s_j = src_ref[hx, j * MOBA_BLOCK:(j + 1) * MOBA_BLOCK, :]
                if causal is not None:
                    s_j = jnp.where(causal[j], s_j, NEG_INF)
                b_j = bias_ref[hx, pl.ds(tile * bpt + j, 1), :]
                parts.append(s_j)
                biases.append(b_j)
            m_n = jnp.max(parts[0], axis=0, keepdims=True) + biases[0]
            for s_j, b_j in zip(parts[1:], biases[1:]):
                m_n = jnp.maximum(m_n, jnp.max(s_j, axis=0, keepdims=True) + b_j)
            if carry is None:
                p_t = jnp.concatenate([jnp.exp2(s_j + (b_j - m_n)) for s_j, b_j in zip(parts, biases)], axis=0)
                out += [m_n, _dot(vt_ref[hx, tile], p_t.astype(BF16))]
            else:
                m_c, acc_c = carry[2 * hx], carry[2 * hx + 1]
                m_n = jnp.maximum(m_c, m_n)
                p_t = jnp.concatenate([jnp.exp2(s_j + (b_j - m_n)) for s_j, b_j in zip(parts, biases)], axis=0)
                out += [m_n, acc_c * jnp.exp2(m_c - m_n) + _dot(vt_ref[hx, tile], p_t.astype(BF16))]
        return tuple(out)

    raw_scores(sa_ref, qi)
    raw_scores(sb_ref, 0)
    krow = lax.broadcasted_iota(I32, (MOBA_BLOCK, tq), 0)
    qcol = lax.broadcasted_iota(I32, (MOBA_BLOCK, tq), 1)
    state = fold(None, sa_ref, qi, causal=[krow + j * MOBA_BLOCK <= qcol for j in range(bpt)])

    def body(i, carry):
        t0 = 2 * i
        raw_scores(sa_ref, t0 + 1)
        carry = fold(carry, sb_ref, t0)
        raw_scores(sb_ref, jnp.minimum(t0 + 2, n_kv - 1))
        return fold(carry, sa_ref, t0 + 1)

    state = lax.fori_loop(0, qi // 2, body, state)
    _, acc_a, _, acc_b = lax.cond(lax.rem(qi, 2) == 1, lambda c: fold(c, sb_ref, qi - 1), lambda c: c, state)
    o_t = jnp.concatenate([acc[:HEAD_DIM] / acc[HEAD_DIM:HEAD_DIM + 1] for acc in (acc_a, acc_b)], axis=0)
    o_ref[0] = o_t.T.astype(o_ref.dtype)


def _attention(q, k, v):
    bsz, seq, aw = q.shape
    npair = aw // LANES
    tq = KV_TILE
    nb = seq // MOBA_BLOCK
    assert seq % KV_TILE == 0 and nb % 8 == 0
    return pl.pallas_call(
        _attn_body,
        grid=(bsz, npair, seq // tq),
        in_specs=[pl.BlockSpec((1, tq, LANES), lambda b, p, i: (b, i, p)),
                  pl.BlockSpec((1, seq, LANES), lambda b, p, i: (b, 0, p)),
                  pl.BlockSpec((1, seq, LANES), lambda b, p, i: (b, 0, p))],
        out_specs=pl.BlockSpec((1, tq, LANES), lambda b, p, i: (b, i, p)),
        out_shape=jax.ShapeDtypeStruct((bsz, seq, aw), BF16),
        scratch_shapes=[pltpu.VMEM((2, seq // KV_TILE, PV_ROWS, KV_TILE), BF16),
                        pltpu.VMEM((2, nb, LANES), F32),
                        pltpu.VMEM((2, nb, tq), F32),
                        pltpu.VMEM((2, KV_TILE, tq), F32),
                        pltpu.VMEM((2, KV_TILE, tq), F32)],
        compiler_params=_cparams("parallel", "parallel", "arbitrary"),
        name="attn",
    )(q, k, v)


def _merge_body(y_ref, a_ref, x_ref, wglu_ref, bglu_ref, gs_ref, ga_ref, wout_ref, gm_ref,
                gf_ref, shf_ref, scf_ref, rw_ref, rb_ref, x1_ref, h2_ref, idx_ref, wt_ref, cnt_ref, ys_ref):
    nblk = y_ref.shape[0]
    tm = x_ref.shape[1]
    for j in range(nblk):
        for t in range(SSM_CHUNK):
            ys_ref[j, pl.ds(t, tm // SSM_CHUNK, stride=SSM_CHUNK), :] = (
                y_ref[j, 0, :, t * LANES:(t + 1) * LANES].astype(F32))
    y = jnp.concatenate([ys_ref[j] for j in range(nblk)], axis=1)
    g = 0.5 * y * (1.0 + lax.erf(y * (2.0 ** -0.5)))
    glu = g * jax.nn.sigmoid(_dot(g.astype(BF16), wglu_ref[...]) + bglu_ref[...])
    ssm_n = _rms(glu, gs_ref[...])
    att_n = _rms(a_ref[0].astype(F32), ga_ref[...])
    merged = jnp.concatenate([ssm_n, att_n], axis=1).astype(BF16)
    mix = _dot(merged, wout_ref[...])
    x1 = x_ref[0] + gm_ref[0] * mix
    x1_ref[0] = x1
    h2 = _rms(x1, gf_ref[...]) * (1.0 + scf_ref[0]) + shf_ref[0]
    h2_ref[0] = h2
    logits = lax.dot_general(rw_ref[...], h2, (((1,), (1,)), ((), ())),
                             preferred_element_type=F32, precision=HIGHEST) + rb_ref[...]
    ne = logits.shape[0]
    sub = lax.broadcasted_iota(I32, (ne, tm), 0)
    vals, idxs = [], []
    for _ in range(TOP_K):
        mx = jnp.max(logits, axis=0, keepdims=True)
        ix = jnp.min(jnp.where(logits == mx, sub, ne), axis=0, keepdims=True)
        vals.append(mx)
        idxs.append(ix)
        logits = jnp.where(sub == ix, -jnp.inf, logits)
    exps = [jnp.exp(vv - vals[0]) for vv in vals]
    tot = exps[0]
    for e in exps[1:]:
        tot = tot + e
    row_i = lax.broadcasted_iota(I32, idx_ref.shape, 0)
    row_w = lax.broadcasted_iota(I32, wt_ref.shape, 0)
    idx_out = jnp.zeros(idx_ref.shape, I32)
    wt_out = jnp.zeros(wt_ref.shape, F32)
    for kk in range(TOP_K):
        idx_out = jnp.where(row_i == kk, idxs[kk], idx_out)
        wt_out = jnp.where(row_w == kk, exps[kk] / tot, wt_out)
    idx_ref[...] = idx_out
    wt_ref[...] = wt_out
    chosen = jnp.zeros((ne, tm), F32)
    for kk in range(TOP_K):
        chosen = chosen + jnp.where(sub == idxs[kk], 1.0, 0.0)
    cnt_ref[0] = jnp.sum(chosen, axis=1, keepdims=True).astype(I32)


def _merge(y4, attn, x, w_glu_bf, b_glu, g_ssm, g_attn, w_out_bf, mod3, g_ffn, router_w, router_b, tm=512):
    bsz, seq, d = x.shape
    nblk = y4.shape[0]
    ssm_w = nblk * LANES
    aw = attn.shape[-1]
    ne = router_w.shape[1]
    tm = min(tm, seq)
    spb = seq // tm
    n_tok = bsz * seq
    rw_t = router_w.T
    row = lambda n: pl.BlockSpec((1, n), lambda b, i: (0, 0))
    full = lambda a: pl.BlockSpec(a.shape, lambda b, i: (0,) * a.ndim)
    modv = lambda j: pl.BlockSpec((1, 1, d), lambda b, i: (b, 0, j))
    return pl.pallas_call(
        _merge_body,
        grid=(bsz, seq // tm),
        in_specs=[pl.BlockSpec((nblk, 1, tm // SSM_CHUNK, SSM_CHUNK * LANES), lambda b, i: (0, b, i, 0)),
                  pl.BlockSpec((1, tm, aw), lambda b, i: (b, i, 0)),
                  pl.BlockSpec((1, tm, d), lambda b, i: (b, i, 0)),
                  full(w_glu_bf), row(ssm_w), row(ssm_w), row(aw), full(w_out_bf),
                  modv(2), row(d), modv(3), modv(4), full(rw_t),
                  pl.BlockSpec((ne, 1), lambda b, i: (0, 0))],
        out_specs=[pl.BlockSpec((1, tm, d), lambda b, i: (b, i, 0)),
                   pl.BlockSpec((1, tm, d), lambda b, i: (b, i, 0)),
                   pl.BlockSpec((TOP_K, tm), lambda b, i: (0, b * spb + i)),
                   pl.BlockSpec((WT_ROWS, tm), lambda b, i: (0, b * spb + i)),
                   pl.BlockSpec((1, ne, 1), lambda b, i: (b * spb + i, 0, 0))],
        out_shape=[jax.ShapeDtypeStruct((bsz, seq, d), F32),
                   jax.ShapeDtypeStruct((bsz, seq, d), F32),
                   jax.ShapeDtypeStruct((TOP_K, n_tok), I32),
                   jax.ShapeDtypeStruct((WT_ROWS, n_tok), F32),
                   jax.ShapeDtypeStruct((bsz * spb, ne, 1), I32)],
        scratch_shapes=[pltpu.VMEM((nblk, tm, LANES), F32)],
        compiler_params=_cparams("parallel", "parallel"),
        name="merge",
    )(y4, attn, x, w_glu_bf, b_glu.reshape(1, ssm_w), g_ssm.reshape(1, ssm_w), g_attn.reshape(1, aw),
      w_out_bf, mod3, g_ffn.reshape(1, d), mod3, mod3, rw_t, router_b.reshape(ne, 1))


def _segment_copies(n_seg, meta, make_copy):
    def body(e, _):
        src0, dst0, units = meta(e)
        for sz in SEG_UNITS:
            off = (units & ~(2 * sz - 1)) * SEG_ALIGN
            src = pl.multiple_of(src0 + off, SEG_ALIGN)
            dst = pl.multiple_of(dst0 + off, SEG_ALIGN)

            @pl.when((units & sz) != 0)
            def _(src=src, dst=dst, sz=sz):
                make_copy(src, dst, sz * SEG_ALIGN).start()
        return 0

    lax.fori_loop(0, n_seg, body, 0)


def _local_slots(idx, lstart_col, ne):
    tb = idx.shape[1]
    sub = lax.broadcasted_iota(I32, (ne, tb), 0)
    hits = [idx[kk:kk + 1, :] == sub for kk in range(TOP_K)]
    onehot = jnp.zeros((ne, tb), F32)
    for h in hits:
        onehot = onehot + jnp.where(h, 1.0, 0.0)
    r = lax.broadcasted_iota(I32, (tb, tb), 0)
    c = lax.broadcasted_iota(I32, (tb, tb), 1)
    tri = jnp.where(r < c, 1.0, 0.0).astype(BF16)
    slot = _dot(onehot.astype(BF16), tri) + lstart_col
    return [jnp.sum(jnp.where(h, slot, 0.0), axis=0, keepdims=True) for h in hits]


def _dispatch_body(lstart_s, pcnt_s, base_s, tail_dst_s, tail_units_s, tail_total_s,
                   idx_ref, lcol_ref, h_ref, xs_ref, ls_ref, sorted_ref, zero_ref, sem, *, ne):
    i = pl.program_id(0)
    nt = pl.num_programs(0)
    tb = h_ref.shape[0]
    lb = sorted_ref.shape[1]
    slot = lax.rem(i, 2)

    @pl.when(i == 0)
    def _():
        zero_ref[...] = jnp.zeros(zero_ref.shape, F32)
        _segment_copies(
            ne, lambda e: (0, tail_dst_s[e], tail_units_s[e]),
            lambda src, dst, rows: pltpu.make_async_copy(zero_ref.at[pl.ds(src, rows), :],
                                                         xs_ref.at[pl.ds(dst, rows), :], sem.at[2]))

    def tile_rows(t):
        return pl.multiple_of(lstart_s[t * ne + ne - 1] + pcnt_s[t * ne + ne - 1], SEG_ALIGN)

    def wait_tile(t, sl):
        done = xs_ref.at[pl.ds(0, tile_rows(t)), :]
        pltpu.make_async_copy(done, done, sem.at[sl]).wait()

    @pl.when(i >= 2)
    def _():
        wait_tile(i - 2, slot)

    slots = _local_slots(idx_ref[...], lcol_ref[0].astype(F32), ne)
    row_o = lax.broadcasted_iota(I32, ls_ref.shape, 0)
    ls_out = jnp.full(ls_ref.shape, -1.0, F32)
    for kk in range(TOP_K):
        ls_out = jnp.where(row_o == kk, slots[kk], ls_out)
    ls_ref[...] = ls_out.astype(I32)
    h = h_ref[...].astype(BF16)
    for r0 in range(0, lb, SORT_CHUNK):
        j = (lax.broadcasted_iota(I32, (SORT_CHUNK, tb), 0) + r0).astype(F32)
        pm = jnp.zeros((SORT_CHUNK, tb), F32)
        for kk in range(TOP_K):
            pm = jnp.where(j == slots[kk], 1.0, pm)
        sorted_ref[slot, r0:r0 + SORT_CHUNK, :] = _dot(pm.astype(BF16), h)

    def meta(e):
        return lstart_s[i * ne + e], base_s[i * ne + e], pcnt_s[i * ne + e] // SEG_ALIGN

    def make_copy(src, dst, rows):
        return pltpu.make_async_copy(sorted_ref.at[slot, pl.ds(src, rows), :], xs_ref.at[pl.ds(dst, rows), :],
                                     sem.at[slot])

    _segment_copies(ne, meta, make_copy)

    @pl.when(i == nt - 1)
    def _():
        @pl.when(i >= 1)
        def _():
            wait_tile(i - 1, 1 - slot)
        wait_tile(i, slot)
        filled = xs_ref.at[pl.ds(0, pl.multiple_of(tail_total_s[0] * SEG_ALIGN, SEG_ALIGN)), :]
        pltpu.make_async_copy(filled, filled, sem.at[2]).wait()


def _dispatch(h2, idx_t, plan, tail, cap, tb, tg):
    n, d = h2.shape
    lstart, pcnt, base = plan
    nt, ne = lstart.shape
    lb = tb * TOP_K + ne * SEG_ALIGN
    assert lb % SORT_CHUNK == 0
    return pl.pallas_call(
        functools.partial(_dispatch_body, ne=ne),
        grid_spec=pltpu.PrefetchScalarGridSpec(
            num_scalar_prefetch=6,
            grid=(nt,),
            in_specs=[pl.BlockSpec((TOP_K, tb), lambda i, *_: (0, i)),
                      pl.BlockSpec((1, ne, 1), lambda i, *_: (i, 0, 0)),
                      pl.BlockSpec((tb, d), lambda i, *_: (i, 0))],
            out_specs=[pl.BlockSpec(memory_space=pl.ANY),
                       pl.BlockSpec((WT_ROWS, tb), lambda i, *_: (0, i))],
            scratch_shapes=[pltpu.VMEM((2, lb, d), F32), pltpu.VMEM((tg, d), F32),
                            pltpu.SemaphoreType.DMA((3,))],
        ),
        out_shape=[jax.ShapeDtypeStruct((cap, d), F32),
                   jax.ShapeDtypeStruct((WT_ROWS, n), I32)],
        compiler_params=_cparams("arbitrary"),
        name="dispatch",
    )(lstart.reshape(-1), pcnt.reshape(-1), base.reshape(-1), *tail, idx_t, lstart.reshape(nt, ne, 1), h2)


def _route_plan(tile_cnt, tg, cap):
    cnt = tile_cnt[:, :, 0]
    ne = cnt.shape[1]
    pcnt = (cnt + SEG_ALIGN - 1) // SEG_ALIGN * SEG_ALIGN
    lstart = jnp.cumsum(pcnt, axis=1) - pcnt
    e_rows = jnp.sum(pcnt, axis=0)
    e_tiles = (e_rows + tg - 1) // tg
    e_end = jnp.cumsum(e_tiles)
    base = ((e_end - e_tiles) * tg)[None, :] + jnp.cumsum(pcnt, axis=0) - pcnt
    ii = jnp.arange(cap // tg, dtype=I32)
    used = ii < e_end[-1]
    tile = jnp.minimum(ii, e_end[-1] - 1)
    e_of = jnp.minimum(jnp.sum((e_end[None, :] <= tile[:, None]).astype(I32), axis=1), ne - 1)
    newe = (e_of != jnp.concatenate([jnp.full((1,), -1, I32), e_of[:-1]])).astype(I32)
    items = (tile.astype(I32), e_of.astype(I32), used.astype(I32), newe)
    tail = (e_end * tg - e_tiles * tg + e_rows, (e_tiles * tg - e_rows) // SEG_ALIGN)
    tail = (tail[0].astype(I32), tail[1].astype(I32), jnp.sum(tail[1]).astype(I32).reshape(1))
    return (lstart.astype(I32), pcnt.astype(I32), base.astype(I32)), tail, items


def _experts_body(tile_ref, exp_ref, used_ref, newe_ref,
                  xs_ref, wg_ref, bg_ref, wu_ref, bu_ref, wd_ref, bd_ref, ys_ref,
                  wg_bf, wu_bf, wd_bf):
    i = pl.program_id(0)

    @pl.when(newe_ref[i] == 1)
    def _cast():
        wg_bf[...] = wg_ref[0].astype(BF16)
        wu_bf[...] = wu_ref[0].astype(BF16)
        wd_bf[...] = wd_ref[0].astype(BF16)

    @pl.when(used_ref[i] == 1)
    def _compute():
        x = xs_ref[...].astype(BF16)
        gate = jnp.minimum(_dot(x, wg_bf[...]) + bg_ref[0], SWIGLU_LIMIT)
        lin = jnp.clip(_dot(x, wu_bf[...]) + bu_ref[0], -SWIGLU_LIMIT, SWIGLU_LIMIT)
        act = gate * jax.nn.sigmoid(SWIGLU_ALPHA * gate) * (lin + 1.0)
        ys_ref[...] = _dot(act.astype(BF16), wd_bf[...]) + bd_ref[0]


def _experts(xs, items, w_gate, b_gate, w_up, b_up, w_down, b_down, tg):
    nk, d = xs.shape
    ne, _, dff = w_gate.shape
    n_items = items[0].shape[0]
    wspec = lambda shp: pl.BlockSpec((1,) + shp, lambda i, t, e, u, nw: (e[i], 0, 0))
    return pl.pallas_call(
        _experts_body,
        grid_spec=pltpu.PrefetchScalarGridSpec(
            num_scalar_prefetch=4,
            grid=(n_items,),
            in_specs=[pl.BlockSpec((tg, d), lambda i, t, e, u, nw: (t[i], 0)),
                      wspec((d, dff)), wspec((1, dff)),
                      wspec((d, dff)), wspec((1, dff)),
                      wspec((dff, d)), wspec((1, d))],
            out_specs=pl.BlockSpec((tg, d), lambda i, t, e, u, nw: (t[i], 0)),
            scratch_shapes=[pltpu.VMEM((d, dff), BF16), pltpu.VMEM((d, dff), BF16), pltpu.VMEM((dff, d), BF16)],
        ),
        out_shape=jax.ShapeDtypeStruct((nk, d), F32),
        compiler_params=_cparams("arbitrary"),
        name="experts",
    )(*items, xs, w_gate, b_gate.reshape(ne, 1, dff), w_up, b_up.reshape(ne, 1, dff),
      w_down, b_down.reshape(ne, 1, d))


def _combine_body(lstart_s, pcnt_s, base_s, ls_ref, wt_ref, x1_ref, gf_ref, gfin_ref, sho_ref, sco_ref,
                  ys_ref, o_ref, buf, sem, *, ne, spb):
    i = pl.program_id(0) * spb + pl.program_id(1)
    nt = pl.num_programs(0) * spb
    tb = x1_ref.shape[1]
    lb = buf.shape[1]
    slot = lax.rem(i, 2)

    def fetch(t, sl):
        def meta(e):
            return base_s[t * ne + e], lstart_s[t * ne + e], pcnt_s[t * ne + e] // SEG_ALIGN

        def make_copy(src, dst, rows):
            return pltpu.make_async_copy(ys_ref.at[pl.ds(src, rows), :], buf.at[sl, pl.ds(dst, rows), :],
                                         sem.at[sl])

        _segment_copies(ne, meta, make_copy)

    @pl.when(i == 0)
    def _():
        buf[...] = jnp.zeros(buf.shape, F32)
        fetch(0, 0)

    @pl.when(i + 1 < nt)
    def _():
        fetch(i + 1, 1 - slot)

    total = pl.multiple_of(lstart_s[i * ne + ne - 1] + pcnt_s[i * ne + ne - 1], SEG_ALIGN)
    done = buf.at[slot, pl.ds(0, total), :]
    pltpu.make_async_copy(done, done, sem.at[slot]).wait()
    ls_col = ls_ref[...].astype(F32).T
    wt_col = wt_ref[...].T
    ffn = jnp.zeros((tb, x1_ref.shape[2]), F32)
    for r0 in range(0, lb, SORT_CHUNK):
        j = (lax.broadcasted_iota(I32, (tb, SORT_CHUNK), 1) + r0).astype(F32)
        wm = jnp.zeros((tb, SORT_CHUNK), F32)
        for kk in range(TOP_K):
            wm = jnp.where(j == ls_col[:, kk:kk + 1], wt_col[:, kk:kk + 1], wm)
        ffn = ffn + _dot(wm.astype(BF16), buf[slot, r0:r0 + SORT_CHUNK, :].astype(BF16))
    x2 = x1_ref[0] + gf_ref[0] * ffn
    o_ref[0] = _rms(x2, gfin_ref[...]) * (1.0 + sco_ref[0]) + sho_ref[0]


def _combine(plan, ls_t, wt_t, x1, mod3, g_final, fmod3, ys, tb):
    bsz, seq, d = x1.shape
    lstart, pcnt, base = plan
    nt, ne = lstart.shape
    spb = seq // tb
    lb = tb * TOP_K + ne * SEG_ALIGN
    return pl.pallas_call(
        functools.partial(_combine_body, ne=ne, spb=spb),
        grid_spec=pltpu.PrefetchScalarGridSpec(
            num_scalar_prefetch=3,
            grid=(bsz, spb),
            in_specs=[pl.BlockSpec((WT_ROWS, tb), lambda b, i, *_: (0, b * spb + i)),
                      pl.BlockSpec((WT_ROWS, tb), lambda b, i, *_: (0, b * spb + i)),
                      pl.BlockSpec((1, tb, d), lambda b, i, *_: (b, i, 0)),
                      pl.BlockSpec((1, 1, d), lambda b, i, *_: (b, 0, 5)),
                      pl.BlockSpec((1, d), lambda b, i, *_: (0, 0)),
                      pl.BlockSpec((1, 1, d), lambda b, i, *_: (b, 0, 0)),
                      pl.BlockSpec((1, 1, d), lambda b, i, *_: (b, 0, 1)),
                      pl.BlockSpec(memory_space=pl.ANY)],
            out_specs=pl.BlockSpec((1, tb, d), lambda b, i, *_: (b, i, 0)),
            scratch_shapes=[pltpu.VMEM((2, lb, d), F32), pltpu.SemaphoreType.DMA((2,))],
        ),
        out_shape=jax.ShapeDtypeStruct((bsz, seq, d), F32),
        compiler_params=_cparams("arbitrary", "arbitrary"),
        name="combine",
    )(lstart.reshape(-1), pcnt.reshape(-1), base.reshape(-1), ls_t, wt_t, x1, mod3, g_final.reshape(1, d),
      fmod3, fmod3, ys)


def kernel(x, c, positions, ada_w, ada_b, final_ada_w, final_ada_b, norm_mix_g, norm_ffn_g, norm_final_g, w_in, ssm_lambda_re, ssm_lambda_im, ssm_log_dt, ssm_b_re, ssm_b_im, ssm_c_re, ssm_c_im, ssm_d, ssm_w_glu, ssm_b_glu, out_norm_ssm_g, out_norm_attn_g, w_out, router_w, router_b, exp_w_gate, exp_b_gate, exp_w_up, exp_b_up, exp_w_down, exp_b_down):
    bsz, seq, d = x.shape
    depth = ada_w.shape[0]
    ssm_w = ssm_d.shape[-1]
    attn_w = (w_in.shape[-1] - ssm_w) // 3
    ne = router_w.shape[-1]
    n_tok = bsz * seq
    nc = seq // SSM_CHUNK
    tg = 512

    rope_tab, rope_exp = _rope_tables(positions)
    fmod3 = _adaln(c, final_ada_w, final_ada_b).reshape(bsz, 1, 2 * d)
    for l in range(depth):
        mod3 = _adaln(c, ada_w[l], ada_b[l]).reshape(bsz, 1, -1)
        u4, q, k, v = _inproj(x, norm_mix_g[l], mod3, w_in[l].astype(BF16), rope_tab, rope_exp, ssm_w, attn_w)
        tables = _s5_params(ssm_lambda_re[l], ssm_lambda_im[l], ssm_log_dt[l], ssm_b_re[l], ssm_b_im[l],
                            ssm_c_re[l], ssm_c_im[l], ssm_d[l], nc)
        y4 = _s5(u4, *tables)
        attn = _attention(q, k, v)
        x1, h2, idx_t, wt_t, tile_cnt = _merge(y4, attn, x, ssm_w_glu[l].astype(BF16), ssm_b_glu[l], out_norm_ssm_g[l],
                                     out_norm_attn_g[l], w_out[l].astype(BF16), mod3, norm_ffn_g[l],
                                     router_w[l], router_b[l])
        tb = seq // (tile_cnt.shape[0] // bsz)
        cap = n_tok * TOP_K + tile_cnt.shape[0] * ne * SEG_ALIGN + ne * tg
        assert cap % tg == 0 and max(tb, tg) // SEG_ALIGN < 2 * SEG_UNITS[0]
        plan, tail, items = _route_plan(tile_cnt, tg, cap)
        xs, ls_t = _dispatch(h2.reshape(n_tok, d), idx_t, plan, tail, cap, tb, tg)
        ys = _experts(xs, items, exp_w_gate[l], exp_b_gate[l], exp_w_up[l], exp_b_up[l],
                      exp_w_down[l], exp_b_down[l], tg)
        if l + 1 < depth:
            raise NotImplementedError("depth > 1 needs the non-final combine")
        x = _combine(plan, ls_t, wt_t, x1, mod3, norm_final_g, fmod3, ys, tb)
    return x
```

```python
import functools
import math

import jax
import jax.numpy as jnp
from jax import lax
from jax.experimental import pallas as pl
from jax.experimental.pallas import tpu as pltpu

F32 = jnp.float32
BF16 = jnp.bfloat16
I32 = jnp.int32
HIGHEST = lax.Precision.HIGHEST

LANES = 128
HEAD_DIM = 64
MOBA_BLOCK = 256
MOBA_TOPK = 3
ROT_DIM = HEAD_DIM // 4
ROPE_THETA = 500000.0
SSM_GROUP = 16
SSM_CHUNK = 16
TOP_K = 4
SWIGLU_ALPHA = 1.702
SWIGLU_LIMIT = 7.0
NORM_EPS = 1e-5
NEG_INF = -1e30
LOG2_E = math.log2(math.e)
KV_TILE = 2 * MOBA_BLOCK
PV_ROWS = HEAD_DIM + 8
WT_ROWS = 8
SEG_ALIGN = 8
SEG_UNITS = (64, 32, 16, 8, 4, 2, 1)
SORT_CHUNK = 256
VMEM_LIMIT = 56 * 1024 * 1024


def _cparams(*sem):
    return pltpu.CompilerParams(dimension_semantics=sem, vmem_limit_bytes=VMEM_LIMIT)


def _dot(a, b):
    return jnp.dot(a, b, preferred_element_type=F32)


def _rms(x, g):
    return x * lax.rsqrt(jnp.mean(x * x, axis=-1, keepdims=True) + NORM_EPS) * g


def _adaln_body(c_ref, w_ref, b_ref, o_ref):
    c = c_ref[...]
    ca = c * jax.nn.sigmoid(c)
    o_ref[...] = jnp.dot(ca, w_ref[...], preferred_element_type=F32, precision=HIGHEST) + b_ref[...]


def _adaln(c, w, b, tn=512):
    bsz, d = c.shape
    n = w.shape[1]
    return pl.pallas_call(
        _adaln_body,
        grid=(n // tn,),
        in_specs=[pl.BlockSpec((bsz, d), lambda j: (0, 0)),
                  pl.BlockSpec((d, tn), lambda j: (0, j)),
                  pl.BlockSpec((1, tn), lambda j: (0, j))],
        out_specs=pl.BlockSpec((bsz, tn), lambda j: (0, j)),
        out_shape=jax.ShapeDtypeStruct((bsz, n), F32),
        compiler_params=_cparams("parallel"),
        name="adaln",
    )(c, w, b.reshape(1, n))


def _inproj_body(x_ref, g_ref, sh_ref, sc_ref, w_ref, tab_ref, exp_ref, u_ref, q_ref, k_ref, v_ref, us_ref):
    x = x_ref[0]
    tm = x.shape[0]
    h = _rms(x, g_ref[...]) * (1.0 + sc_ref[0]) + sh_ref[0]
    proj = _dot(h.astype(BF16), w_ref[...])
    ssm_w = u_ref.shape[0] * LANES
    aw = q_ref.shape[2]
    for j in range(u_ref.shape[0]):
        us_ref[j] = proj[:, j * LANES:(j + 1) * LANES]
        for t in range(SSM_CHUNK):
            u_ref[j, 0, :, t * LANES:(t + 1) * LANES] = us_ref[
                j, pl.ds(t, tm // SSM_CHUNK, stride=SSM_CHUNK), :].astype(BF16)
    t_hi = tab_ref[0].astype(BF16)
    t_lo = (tab_ref[0] - t_hi.astype(F32)).astype(BF16)
    tn = (((0,), (0,)), ((), ()))
    tab = (lax.dot_general(t_hi, exp_ref[...], tn, preferred_element_type=F32)
           + lax.dot_general(t_lo, exp_ref[...], tn, preferred_element_type=F32))
    lane1 = lax.broadcasted_iota(I32, (tm, LANES), 1)
    cos1 = tab[:, :LANES] + jnp.where((lane1 & (HEAD_DIM - 1)) >= ROT_DIM, 1.0, 0.0)
    reps = aw // LANES
    cosf = jnp.concatenate([cos1] * reps, axis=1)
    sinf = jnp.concatenate([tab[:, LANES:]] * reps, axis=1)
    lane = lax.broadcasted_iota(I32, (tm, aw), 1)
    first_half = (lane & (HEAD_DIM - 1)) < (ROT_DIM // 2)

    def rope(t):
        partner = jnp.where(first_half, pltpu.roll(t, aw - ROT_DIM // 2, 1), pltpu.roll(t, ROT_DIM // 2, 1))
        return t * cosf + partner * sinf

    q = rope(proj[:, ssm_w:ssm_w + aw]) * (HEAD_DIM ** -0.5 * LOG2_E)
    k = rope(proj[:, ssm_w + aw:ssm_w + 2 * aw])
    q_ref[0] = q.astype(BF16)
    k_ref[0] = k.astype(BF16)
    v_ref[0] = proj[:, ssm_w + 2 * aw:ssm_w + 3 * aw].astype(BF16)


def _inproj(x, gain, mod3, w_in_bf, rope_tab, rope_exp, ssm_w, attn_w, tm=512):
    bsz, seq, d = x.shape
    n_u = ssm_w // LANES
    tm = min(tm, seq)
    return pl.pallas_call(
        _inproj_body,
        grid=(bsz, seq // tm),
        in_specs=[pl.BlockSpec((1, tm, d), lambda b, i: (b, i, 0)),
                  pl.BlockSpec((1, d), lambda b, i: (0, 0)),
                  pl.BlockSpec((1, 1, d), lambda b, i: (b, 0, 0)),
                  pl.BlockSpec((1, 1, d), lambda b, i: (b, 0, 1)),
                  pl.BlockSpec(w_in_bf.shape, lambda b, i: (0, 0)),
                  pl.BlockSpec((1, ROT_DIM, tm), lambda b, i: (b, 0, i)),
                  pl.BlockSpec(rope_exp.shape, lambda b, i: (0, 0))],
        out_specs=[pl.BlockSpec((n_u, 1, tm // SSM_CHUNK, SSM_CHUNK * LANES), lambda b, i: (0, b, i, 0)),
                   pl.BlockSpec((1, tm, attn_w), lambda b, i: (b, i, 0)),
                   pl.BlockSpec((1, tm, attn_w), lambda b, i: (b, i, 0)),
                   pl.BlockSpec((1, tm, attn_w), lambda b, i: (b, i, 0))],
        out_shape=[jax.ShapeDtypeStruct((n_u, bsz, seq // SSM_CHUNK, SSM_CHUNK * LANES), BF16),
                   jax.ShapeDtypeStruct((bsz, seq, attn_w), BF16),
                   jax.ShapeDtypeStruct((bsz, seq, attn_w), BF16),
                   jax.ShapeDtypeStruct((bsz, seq, attn_w), BF16)],
        scratch_shapes=[pltpu.VMEM((n_u, tm, LANES), F32)],
        compiler_params=_cparams("parallel", "parallel"),
        name="inproj",
    )(x, gain.reshape(1, d), mod3, mod3, w_in_bf, rope_tab, rope_exp)


def _rope_tables(positions):
    half = ROT_DIM // 2
    inv_freq = ROPE_THETA ** (-jnp.arange(0, ROT_DIM, 2, dtype=F32) / ROT_DIM)
    ang = positions.astype(F32)[:, None, :] * inv_freq[None, :, None]
    tab = jnp.concatenate([jnp.cos(ang), jnp.sin(ang)], axis=1)
    i = jnp.arange(ROT_DIM)[:, None]
    hl = jnp.arange(LANES)[None, :] % HEAD_DIM
    e_cos = jnp.where((i < half) & ((hl == i) | (hl == i + half)), 1.0, 0.0)
    e_sin = jnp.where(i >= half, jnp.where(hl == i - half, -1.0, jnp.where(hl == i, 1.0, 0.0)), 0.0)
    return tab, jnp.concatenate([e_cos, e_sin], axis=1).astype(BF16)


def _s5_params(lam_re, lam_im, log_dt, b_re, b_im, c_re, c_im, d_skip, n_chunks):
    g_all, p = lam_re.shape
    hc = b_re.shape[-1]
    t = SSM_CHUNK
    gpb = LANES // hc
    nblk = g_all // gpb
    lr, li = lam_re.astype(F32), lam_im.astype(F32)
    dt = jnp.exp(log_dt.astype(F32))[:, None]
    ldr, ldi = lr * dt, li * dt

    def apow(k):
        k = jnp.asarray(k, F32)[..., None, None]
        mag = jnp.exp(ldr * k)
        return mag * jnp.cos(ldi * k), mag * jnp.sin(ldi * k)

    ar, ai = apow(1.0)
    zr, zi = ar - 1.0, ai
    den = lr * lr + li * li
    cr = (zr * lr + zi * li) / den
    ci = (zi * lr - zr * li) / den
    bre, bim = b_re.astype(F32), b_im.astype(F32)
    bbr = cr[..., None] * bre - ci[..., None] * bim
    bbi = cr[..., None] * bim + ci[..., None] * bre
    cre, cim = c_re.astype(F32), c_im.astype(F32)
    def block_diag(a):
        w = a.shape[-1]
        wide = jnp.concatenate([a] * gpb, axis=-1)
        grp = jnp.arange(gpb)[:, None, None]
        lane_grp = (jnp.arange(gpb * w) // w)[None, None, :]
        wide = jnp.where(lane_grp == grp, wide, 0.0)
        return wide.reshape(a.shape[:-3] + (gpb * a.shape[-2], gpb * w))

    pr, pi = apow(jnp.arange(t))
    abr = pr[..., None] * bbr - pi[..., None] * bbi
    abi = pr[..., None] * bbi + pi[..., None] * bbr
    kf = (jnp.einsum('gnp,kgph->kghn', cre, abr, precision=HIGHEST)
          - jnp.einsum('gnp,kgph->kghn', cim, abi, precision=HIGHEST))
    kblk = block_diag(kf.reshape(t, nblk, gpb, hc, hc))
    kpad = jnp.concatenate([jnp.zeros_like(kblk[:1]), kblk], axis=0)
    pm = jnp.concatenate([kpad[0:t], kpad[1:t + 1]], axis=-1)
    kstack = jnp.transpose(pm[::-1], (1, 0, 2, 3)).reshape(nblk, t * LANES, 2 * LANES)

    def inject(ab):
        ab = jnp.transpose(ab[::-1].reshape(t, nblk, gpb, p, hc), (1, 0, 2, 4, 3))
        return block_diag(ab).reshape(nblk, t * LANES, gpb * p)

    bm = jnp.concatenate([inject(abr), inject(abi)], axis=-1)

    qr, qi = apow(jnp.arange(t) + 1.0)
    c_from_re = cre[None] * qr[:, :, None, :] - cim[None] * qi[:, :, None, :]
    c_from_im = -cre[None] * qi[:, :, None, :] - cim[None] * qr[:, :, None, :]

    def readout_t(cc):
        cc = jnp.transpose(cc.reshape(t, nblk, gpb, hc, p), (1, 0, 2, 3, 4))
        return block_diag(cc).reshape(nblk, t * LANES, gpb * p)

    cm = jnp.swapaxes(jnp.concatenate([readout_t(c_from_re), readout_t(c_from_im)], axis=-1), 1, 2)

    n_steps = max(1, int(math.log2(n_chunks)))
    shifts = [float(t * (1 << s)) for s in range(n_steps)]
    sr, si = apow(jnp.asarray(shifts))
    ap = jnp.concatenate([sr.reshape(n_steps, nblk, gpb * p), si.reshape(n_steps, nblk, gpb * p)], axis=-1)
    ap = jnp.transpose(ap, (1, 0, 2))
    dvec = jnp.tile(d_skip.astype(F32).reshape(nblk, 1, LANES), (1, 1, t))
    return kstack.astype(BF16), bm.astype(BF16), cm.astype(BF16), ap, dvec


def _s5_body(x_ref, ks_ref, bm_ref, cm_ref, ap_ref, d_ref, y_ref, h_ref):
    nc = x_ref.shape[2]
    t = SSM_CHUNK
    pad = h_ref.shape[0] - nc
    half = h_ref.shape[1] // 2
    x = x_ref[0, 0]
    h_ref[0:pad] = jnp.zeros((pad, h_ref.shape[1]), F32)
    h_ref[pad:pad + nc] = _dot(x, bm_ref[0])
    for step in range(ap_ref.shape[1]):
        d = 1 << step
        cur = h_ref[pad:pad + nc]
        sft = h_ref[pad - d:pad + nc - d]
        ar = ap_ref[0, step:step + 1, 0:half]
        ai = ap_ref[0, step:step + 1, half:]
        cr, ci = cur[:, :half], cur[:, half:]
        sr, si = sft[:, :half], sft[:, half:]
        h_ref[pad:pad + nc, 0:half] = cr + ar * sr - ai * si
        h_ref[pad:pad + nc, half:] = ci + ar * si + ai * sr
    h_prev = h_ref[pad - 1:pad + nc - 1]
    ys = _dot(h_prev.astype(BF16), cm_ref[0])
    for t0 in range(0, t, 2):
        lo, hi = t0 * LANES, (t0 + 2) * LANES
        conv = _dot(x[:, :hi], ks_ref[0, (t - 2 - t0) * LANES:, :])
        y = conv + ys[:, lo:hi] + d_ref[0][:, lo:hi] * x[:, lo:hi].astype(F32)
        y_ref[0, 0, :, lo:hi] = y.astype(y_ref.dtype)


def _s5(u4, kstack, bm, cm, ap, dvec):
    nblk, bsz, nc, w = u4.shape
    sw = bm.shape[-1]
    pad = max(nc // 2, 8)
    return pl.pallas_call(
        _s5_body,
        grid=(nblk, bsz),
        in_specs=[pl.BlockSpec((1, 1, nc, w), lambda j, b: (j, b, 0, 0)),
                  pl.BlockSpec((1,) + kstack.shape[1:], lambda j, b: (j, 0, 0)),
                  pl.BlockSpec((1,) + bm.shape[1:], lambda j, b: (j, 0, 0)),
                  pl.BlockSpec((1,) + cm.shape[1:], lambda j, b: (j, 0, 0)),
                  pl.BlockSpec((1,) + ap.shape[1:], lambda j, b: (j, 0, 0)),
                  pl.BlockSpec((1, 1, w), lambda j, b: (j, 0, 0))],
        out_specs=pl.BlockSpec((1, 1, nc, w), lambda j, b: (j, b, 0, 0)),
        out_shape=jax.ShapeDtypeStruct((nblk, bsz, nc, w), BF16),
        scratch_shapes=[pltpu.VMEM((pad + nc, sw), F32)],
        compiler_params=_cparams("parallel", "parallel"),
        name="s5",
    )(u4, kstack, bm, cm, ap, dvec)


def _attn_body(q_ref, k_ref, v_ref, o_ref, vt_ref, km_ref, bias_ref, sa_ref, sb_ref):
    qi = pl.program_id(2)
    seq = k_ref.shape[1]
    tq = q_ref.shape[1]
    nb = seq // MOBA_BLOCK
    n_kv = seq // KV_TILE
    bpt = KV_TILE // MOBA_BLOCK
    blk_shift = MOBA_BLOCK.bit_length() - 1

    @pl.when(qi == 0)
    def _build():
        k = k_ref[0].astype(F32)
        v = v_ref[0].astype(F32)
        ones = jnp.where(lax.broadcasted_iota(I32, (PV_ROWS - HEAD_DIM, KV_TILE), 0) == 0, 1.0, 0.0)
        for t in range(n_kv):
            v_t = v[t * KV_TILE:(t + 1) * KV_TILE].T
            vt_ref[0, t] = jnp.concatenate([v_t[:HEAD_DIM], ones], axis=0).astype(BF16)
            vt_ref[1, t] = jnp.concatenate([v_t[HEAD_DIM:], ones], axis=0).astype(BF16)
        km = jnp.mean(k.reshape(nb, MOBA_BLOCK, LANES), axis=1)
        lane_b = lax.broadcasted_iota(I32, (nb, LANES), 1)
        km_ref[0] = jnp.where(lane_b < HEAD_DIM, km, 0.0)
        km_ref[1] = jnp.where(lane_b >= HEAD_DIM, km, 0.0)

    q_t = q_ref[0].astype(F32).T
    feat = lax.broadcasted_iota(I32, (LANES, tq), 0)
    q_h = [jnp.where(feat < HEAD_DIM, q_t, 0.0).astype(BF16),
           jnp.where(feat >= HEAD_DIM, q_t, 0.0).astype(BF16)]

    jidx = lax.broadcasted_iota(I32, (nb, tq), 0)
    qblk = qi * (tq // MOBA_BLOCK) + lax.shift_right_logical(
        lax.broadcasted_iota(I32, (nb, tq), 1), blk_shift)
    for hx in range(2):
        g = _dot(km_ref[hx], q_t)
        cnt = jnp.zeros((nb, tq), F32)
        for jp in range(nb):
            row = g[jp:jp + 1, :]
            beats = jnp.where(row > g, 1.0, jnp.where(row == g, jnp.where(jp < jidx, 1.0, 0.0), 0.0))
            cnt = cnt + jnp.where(jp < qblk, beats, 0.0)
        past_sel = jnp.where(jidx < qblk, jnp.where(cnt < MOBA_TOPK, 1.0, 0.0), 0.0)
        sel = jnp.where(jidx == qblk, 1.0, past_sel)
        bias_ref[hx] = jnp.where(sel > 0.5, 0.0, NEG_INF)

    def raw_scores(dst_ref, tile):
        off = pl.multiple_of(tile * KV_TILE, KV_TILE)
        k_t = k_ref[0, pl.ds(off, KV_TILE), :]
        for hx in range(2):
            dst_ref[hx] = _dot(k_t, q_h[hx])

    def fold(carry, src_ref, tile, causal=None):
        out = []
        for hx in range(2):
            parts, biases = [], []
            for j in range(bpt):
                s_j = src_ref[hx, j * MOBA_BLOCK:(j + 1) * MOBA_BLOCK, :]
                if causal is not None:
                    s_j = jnp.where(causal[j], s_j, NEG_INF)
                b_j = bias_ref[hx, pl.ds(tile * bpt + j, 1), :]
                parts.append(s_j)
                biases.append(b_j)
            m_n = jnp.max(parts[0], axis=0, keepdims=True) + biases[0]
            for s_j, b_j in zip(parts[1:], biases[1:]):
                m_n = jnp.maximum(m_n, jnp.max(s_j, axis=0, keepdims=True) + b_j)
            if carry is None:
                p_t = jnp.concatenate([jnp.exp2(s_j + (b_j - m_n)) for s_j, b_j in zip(parts, biases)], axis=0)
                out += [m_n, _dot(vt_ref[hx, tile], p_t.astype(BF16))]
            else:
                m_c, acc_c = carry[2 * hx], carry[2 * hx + 1]
                m_n = jnp.maximum(m_c, m_n)
                p_t = jnp.concatenate([jnp.exp2(s_j + (b_j - m_n)) for s_j, b_j in zip(parts, biases)], axis=0)
                out += [m_n, acc_c * jnp.exp2(m_c - m_n) + _dot(vt_ref[hx, tile], p_t.astype(BF16))]
        return tuple(out)

    raw_scores(sa_ref, qi)
    raw_scores(sb_ref, 0)
    krow = lax.broadcasted_iota(I32, (MOBA_BLOCK, tq), 0)
    qcol = lax.broadcasted_iota(I32, (MOBA_BLOCK, tq), 1)
    state = fold(None, sa_ref, qi, causal=[krow + j * MOBA_BLOCK <= qcol for j in range(bpt)])

    def body(i, carry):
        t0 = 2 * i
        raw_scores(sa_ref, t0 + 1)
        carry = fold(carry, sb_ref, t0)
        raw_scores(sb_ref, jnp.minimum(t0 + 2, n_kv - 1))
        return fold(carry, sa_ref, t0 + 1)

    state = lax.fori_loop(0, qi // 2, body, state)
    _, acc_a, _, acc_b = lax.cond(lax.rem(qi, 2) == 1, lambda c: fold(c, sb_ref, qi - 1), lambda c: c, state)
    o_t = jnp.concatenate([acc[:HEAD_DIM] / acc[HEAD_DIM:HEAD_DIM + 1] for acc in (acc_a, acc_b)], axis=0)
    o_ref[0] = o_t.T.astype(o_ref.dtype)


def _attention(q, k, v):
    bsz, seq, aw = q.shape
    npair = aw // LANES
    tq = KV_TILE
    nb = seq // MOBA_BLOCK
    assert seq % KV_TILE == 0 and nb % 8 == 0
    return pl.pallas_call(
        _attn_body,
        grid=(bsz, npair, seq // tq),
        in_specs=[pl.BlockSpec((1, tq, LANES), lambda b, p, i: (b, i, p)),
                  pl.BlockSpec((1, seq, LANES), lambda b, p, i: (b, 0, p)),
                  pl.BlockSpec((1, seq, LANES), lambda b, p, i: (b, 0, p))],
        out_specs=pl.BlockSpec((1, tq, LANES), lambda b, p, i: (b, i, p)),
        out_shape=jax.ShapeDtypeStruct((bsz, seq, aw), BF16),
        scratch_shapes=[pltpu.VMEM((2, seq // KV_TILE, PV_ROWS, KV_TILE), BF16),
                        pltpu.VMEM((2, nb, LANES), F32),
                        pltpu.VMEM((2, nb, tq), F32),
                        pltpu.VMEM((2, KV_TILE, tq), F32),
                        pltpu.VMEM((2, KV_TILE, tq), F32)],
        compiler_params=_cparams("parallel", "parallel", "arbitrary"),
        name="attn",
    )(q, k, v)


def _merge_body(y_ref, a_ref, x_ref, wglu_ref, bglu_ref, gs_ref, ga_ref, wout_ref, gm_ref,
                gf_ref, shf_ref, scf_ref, rw_ref, rb_ref, x1_ref, h2_ref, idx_ref, wt_ref, cnt_ref, ys_ref):
    nblk = y_ref.shape[0]
    tm = x_ref.shape[1]
    for j in range(nblk):
        for t in range(SSM_CHUNK):
            ys_ref[j, pl.ds(t, tm // SSM_CHUNK, stride=SSM_CHUNK), :] = (
                y_ref[j, 0, :, t * LANES:(t + 1) * LANES].astype(F32))
    y = jnp.concatenate([ys_ref[j] for j in range(nblk)], axis=1)
    g = 0.5 * y * (1.0 + lax.erf(y * (2.0 ** -0.5)))
    glu = g * jax.nn.sigmoid(_dot(g.astype(BF16), wglu_ref[...]) + bglu_ref[...])
    ssm_n = _rms(glu, gs_ref[...])
    att_n = _rms(a_ref[0].astype(F32), ga_ref[...])
    merged = jnp.concatenate([ssm_n, att_n], axis=1).astype(BF16)
    mix = _dot(merged, wout_ref[...])
    x1 = x_ref[0] + gm_ref[0] * mix
    x1_ref[0] = x1
    h2 = _rms(x1, gf_ref[...]) * (1.0 + scf_ref[0]) + shf_ref[0]
    h2_ref[0] = h2
    logits = lax.dot_general(rw_ref[...], h2, (((1,), (1,)), ((), ())),
                             preferred_element_type=F32, precision=HIGHEST) + rb_ref[...]
    ne = logits.shape[0]
    sub = lax.broadcasted_iota(I32, (ne, tm), 0)
    vals, idxs = [], []
    for _ in range(TOP_K):
        mx = jnp.max(logits, axis=0, keepdims=True)
        ix = jnp.min(jnp.where(logits == mx, sub, ne), axis=0, keepdims=True)
        vals.append(mx)
        idxs.append(ix)
        logits = jnp.where(sub == ix, -jnp.inf, logits)
    exps = [jnp.exp(vv - vals[0]) for vv in vals]
    tot = exps[0]
    for e in exps[1:]:
        tot = tot + e
    row_i = lax.broadcasted_iota(I32, idx_ref.shape, 0)
    row_w = lax.broadcasted_iota(I32, wt_ref.shape, 0)
    idx_out = jnp.zeros(idx_ref.shape, I32)
    wt_out = jnp.zeros(wt_ref.shape, F32)
    for kk in range(TOP_K):
        idx_out = jnp.where(row_i == kk, idxs[kk], idx_out)
        wt_out = jnp.where(row_w == kk, exps[kk] / tot, wt_out)
    idx_ref[...] = idx_out
    wt_ref[...] = wt_out
    chosen = jnp.zeros((ne, tm), F32)
    for kk in range(TOP_K):
        chosen = chosen + jnp.where(sub == idxs[kk], 1.0, 0.0)
    cnt_ref[0] = jnp.sum(chosen, axis=1, keepdims=True).astype(I32)


def _merge(y4, attn, x, w_glu_bf, b_glu, g_ssm, g_attn, w_out_bf, mod3, g_ffn, router_w, router_b, tm=512):
    bsz, seq, d = x.shape
    nblk = y4.shape[0]
    ssm_w = nblk * LANES
    aw = attn.shape[-1]
    ne = router_w.shape[1]
    tm = min(tm, seq)
    spb = seq // tm
    n_tok = bsz * seq
    rw_t = router_w.T
    row = lambda n: pl.BlockSpec((1, n), lambda b, i: (0, 0))
    full = lambda a: pl.BlockSpec(a.shape, lambda b, i: (0,) * a.ndim)
    modv = lambda j: pl.BlockSpec((1, 1, d), lambda b, i: (b, 0, j))
    return pl.pallas_call(
        _merge_body,
        grid=(bsz, seq // tm),
        in_specs=[pl.BlockSpec((nblk, 1, tm // SSM_CHUNK, SSM_CHUNK * LANES), lambda b, i: (0, b, i, 0)),
                  pl.BlockSpec((1, tm, aw), lambda b, i: (b, i, 0)),
                  pl.BlockSpec((1, tm, d), lambda b, i: (b, i, 0)),
                  full(w_glu_bf), row(ssm_w), row(ssm_w), row(aw), full(w_out_bf),
                  modv(2), row(d), modv(3), modv(4), full(rw_t),
                  pl.BlockSpec((ne, 1), lambda b, i: (0, 0))],
        out_specs=[pl.BlockSpec((1, tm, d), lambda b, i: (b, i, 0)),
                   pl.BlockSpec((1, tm, d), lambda b, i: (b, i, 0)),
                   pl.BlockSpec((TOP_K, tm), lambda b, i: (0, b * spb + i)),
                   pl.BlockSpec((WT_ROWS, tm), lambda b, i: (0, b * spb + i)),
                   pl.BlockSpec((1, ne, 1), lambda b, i: (b * spb + i, 0, 0))],
        out_shape=[jax.ShapeDtypeStruct((bsz, seq, d), F32),
                   jax.ShapeDtypeStruct((bsz, seq, d), F32),
                   jax.ShapeDtypeStruct((TOP_K, n_tok), I32),
                   jax.ShapeDtypeStruct((WT_ROWS, n_tok), F32),
                   jax.ShapeDtypeStruct((bsz * spb, ne, 1), I32)],
        scratch_shapes=[pltpu.VMEM((nblk, tm, LANES), F32)],
        compiler_params=_cparams("parallel", "parallel"),
        name="merge",
    )(y4, attn, x, w_glu_bf, b_glu.reshape(1, ssm_w), g_ssm.reshape(1, ssm_w), g_attn.reshape(1, aw),
      w_out_bf, mod3, g_ffn.reshape(1, d), mod3, mod3, rw_t, router_b.reshape(ne, 1))


def _segment_copies(n_seg, meta, make_copy):
    def body(e, _):
        src0, dst0, units = meta(e)
        for sz in SEG_UNITS:
            off = units & ~(2 * sz - 1)

            @pl.when((units & sz) != 0)
            def _(off=off, sz=sz):
                make_copy(src0 + off, dst0 + off, sz).start()
        return 0

    lax.fori_loop(0, n_seg, body, 0)


def _local_slots(idx, lstart_col, ne):
    tb = idx.shape[1]
    sub = lax.broadcasted_iota(I32, (ne, tb), 0)
    hits = [idx[kk:kk + 1, :] == sub for kk in range(TOP_K)]
    onehot = jnp.zeros((ne, tb), F32)
    for h in hits:
        onehot = onehot + jnp.where(h, 1.0, 0.0)
    r = lax.broadcasted_iota(I32, (tb, tb), 0)
    c = lax.broadcasted_iota(I32, (tb, tb), 1)
    tri = jnp.where(r < c, 1.0, 0.0).astype(BF16)
    slot = _dot(onehot.astype(BF16), tri) + lstart_col
    return [jnp.sum(jnp.where(h, slot, 0.0), axis=0, keepdims=True) for h in hits]


def _dispatch_body(lstart_s, pcnt_s, base_s, tail_dst_s, tail_units_s, tail_total_s,
                   idx_ref, lcol_ref, h_ref, xs_ref, ls_ref, sorted_ref, zero_ref, sem, *, ne):
    i = pl.program_id(0)
    nt = pl.num_programs(0)
    tb = h_ref.shape[0]
    lb = sorted_ref.shape[1] * SEG_ALIGN
    slot = lax.rem(i, 2)

    @pl.when(i == 0)
    def _():
        zero_ref[...] = jnp.zeros(zero_ref.shape, F32)
        _segment_copies(
            ne, lambda e: (0, tail_dst_s[e] // SEG_ALIGN, tail_units_s[e]),
            lambda src, dst, n: pltpu.make_async_copy(zero_ref.at[pl.ds(src, n)], xs_ref.at[pl.ds(dst, n)],
                                                      sem.at[2]))

    def tile_units(t):
        return (lstart_s[t * ne + ne - 1] + pcnt_s[t * ne + ne - 1]) // SEG_ALIGN

    def wait_tile(t, sl):
        done = xs_ref.at[pl.ds(0, tile_units(t))]
        pltpu.make_async_copy(done, done, sem.at[sl]).wait()

    @pl.when(i >= 2)
    def _():
        wait_tile(i - 2, slot)

    slots = _local_slots(idx_ref[...], lcol_ref[0].astype(F32), ne)
    row_o = lax.broadcasted_iota(I32, ls_ref.shape, 0)
    ls_out = jnp.full(ls_ref.shape, -1.0, F32)
    for kk in range(TOP_K):
        ls_out = jnp.where(row_o == kk, slots[kk], ls_out)
    ls_ref[...] = ls_out.astype(I32)
    h = h_ref[...].astype(BF16)
    for r0 in range(0, lb, SORT_CHUNK):
        j = (lax.broadcasted_iota(I32, (SORT_CHUNK, tb), 0) + r0).astype(F32)
        pm = jnp.zeros((SORT_CHUNK, tb), F32)
        for kk in range(TOP_K):
            pm = jnp.where(j == slots[kk], 1.0, pm)
        sorted_ref[slot, r0 // SEG_ALIGN:(r0 + SORT_CHUNK) // SEG_ALIGN] = _dot(pm.astype(BF16), h).reshape(
            SORT_CHUNK // SEG_ALIGN, SEG_ALIGN, h.shape[1])

    def meta(e):
        return (lstart_s[i * ne + e] // SEG_ALIGN, base_s[i * ne + e] // SEG_ALIGN,
                pcnt_s[i * ne + e] // SEG_ALIGN)

    def make_copy(src, dst, n):
        return pltpu.make_async_copy(sorted_ref.at[slot, pl.ds(src, n)], xs_ref.at[pl.ds(dst, n)], sem.at[slot])

    _segment_copies(ne, meta, make_copy)

    @pl.when(i == nt - 1)
    def _():
        @pl.when(i >= 1)
        def _():
            wait_tile(i - 1, 1 - slot)
        wait_tile(i, slot)

        @pl.when(tail_total_s[0] > 0)
        def _():
            filled = xs_ref.at[pl.ds(0, tail_total_s[0])]
            pltpu.make_async_copy(filled, filled, sem.at[2]).wait()


def _dispatch(h2, idx_t, plan, tail, cap, tb, tg):
    n, d = h2.shape
    lstart, pcnt, base = plan
    nt, ne = lstart.shape
    lb = tb * TOP_K + ne * SEG_ALIGN
    assert lb % SORT_CHUNK == 0
    return pl.pallas_call(
        functools.partial(_dispatch_body, ne=ne),
        grid_spec=pltpu.PrefetchScalarGridSpec(
            num_scalar_prefetch=6,
            grid=(nt,),
            in_specs=[pl.BlockSpec((TOP_K, tb), lambda i, *_: (0, i)),
                      pl.BlockSpec((1, ne, 1), lambda i, *_: (i, 0, 0)),
                      pl.BlockSpec((tb, d), lambda i, *_: (i, 0))],
            out_specs=[pl.BlockSpec(memory_space=pl.ANY),
                       pl.BlockSpec((WT_ROWS, tb), lambda i, *_: (0, i))],
            scratch_shapes=[pltpu.VMEM((2, lb // SEG_ALIGN, SEG_ALIGN, d), F32),
                            pltpu.VMEM((tg // SEG_ALIGN, SEG_ALIGN, d), F32),
                            pltpu.SemaphoreType.DMA((3,))],
        ),
        out_shape=[jax.ShapeDtypeStruct((cap // SEG_ALIGN, SEG_ALIGN, d), F32),
                   jax.ShapeDtypeStruct((WT_ROWS, n), I32)],
        compiler_params=_cparams("arbitrary"),
        name="dispatch",
    )(lstart.reshape(-1), pcnt.reshape(-1), base.reshape(-1), *tail, idx_t, lstart.reshape(nt, ne, 1), h2)


def _route_plan(tile_cnt, tg, cap):
    cnt = tile_cnt[:, :, 0]
    ne = cnt.shape[1]
    pcnt = (cnt + SEG_ALIGN - 1) // SEG_ALIGN * SEG_ALIGN
    lstart = jnp.cumsum(pcnt, axis=1) - pcnt
    e_rows = jnp.sum(pcnt, axis=0)
    e_tiles = (e_rows + tg - 1) // tg
    e_end = jnp.cumsum(e_tiles)
    base = ((e_end - e_tiles) * tg)[None, :] + jnp.cumsum(pcnt, axis=0) - pcnt
    ii = jnp.arange(cap // tg, dtype=I32)
    used = ii < e_end[-1]
    tile = jnp.minimum(ii, e_end[-1] - 1)
    e_of = jnp.minimum(jnp.sum((e_end[None, :] <= tile[:, None]).astype(I32), axis=1), ne - 1)
    newe = (e_of != jnp.concatenate([jnp.full((1,), -1, I32), e_of[:-1]])).astype(I32)
    items = (tile.astype(I32), e_of.astype(I32), used.astype(I32), newe)
    tail = (e_end * tg - e_tiles * tg + e_rows, (e_tiles * tg - e_rows) // SEG_ALIGN)
    tail = (tail[0].astype(I32), tail[1].astype(I32), jnp.sum(tail[1]).astype(I32).reshape(1))
    return (lstart.astype(I32), pcnt.astype(I32), base.astype(I32)), tail, items


def _experts_body(tile_ref, exp_ref, used_ref, newe_ref,
                  xs_ref, wg_ref, bg_ref, wu_ref, bu_ref, wd_ref, bd_ref, ys_ref,
                  wg_bf, wu_bf, wd_bf):
    i = pl.program_id(0)

    @pl.when(newe_ref[i] == 1)
    def _cast():
        wg_bf[...] = wg_ref[0].astype(BF16)
        wu_bf[...] = wu_ref[0].astype(BF16)
        wd_bf[...] = wd_ref[0].astype(BF16)

    @pl.when(used_ref[i] == 1)
    def _compute():
        x = xs_ref[...].astype(BF16)
        gate = jnp.minimum(_dot(x, wg_bf[...]) + bg_ref[0], SWIGLU_LIMIT)
        lin = jnp.clip(_dot(x, wu_bf[...]) + bu_ref[0], -SWIGLU_LIMIT, SWIGLU_LIMIT)
        act = gate * jax.nn.sigmoid(SWIGLU_ALPHA * gate) * (lin + 1.0)
        ys_ref[...] = _dot(act.astype(BF16), wd_bf[...]) + bd_ref[0]


def _experts(xs, items, w_gate, b_gate, w_up, b_up, w_down, b_down, tg):
    nk, d = xs.shape
    ne, _, dff = w_gate.shape
    n_items = items[0].shape[0]
    wspec = lambda shp: pl.BlockSpec((1,) + shp, lambda i, t, e, u, nw: (e[i], 0, 0))
    return pl.pallas_call(
        _experts_body,
        grid_spec=pltpu.PrefetchScalarGridSpec(
            num_scalar_prefetch=4,
            grid=(n_items,),
            in_specs=[pl.BlockSpec((tg, d), lambda i, t, e, u, nw: (t[i], 0)),
                      wspec((d, dff)), wspec((1, dff)),
                      wspec((d, dff)), wspec((1, dff)),
                      wspec((dff, d)), wspec((1, d))],
            out_specs=pl.BlockSpec((tg, d), lambda i, t, e, u, nw: (t[i], 0)),
            scratch_shapes=[pltpu.VMEM((d, dff), BF16), pltpu.VMEM((d, dff), BF16), pltpu.VMEM((dff, d), BF16)],
        ),
        out_shape=jax.ShapeDtypeStruct((nk, d), F32),
        compiler_params=_cparams("arbitrary"),
        name="experts",
    )(*items, xs, w_gate, b_gate.reshape(ne, 1, dff), w_up, b_up.reshape(ne, 1, dff),
      w_down, b_down.reshape(ne, 1, d))


def _combine_body(lstart_s, pcnt_s, base_s, ls_ref, wt_ref, x1_ref, gf_ref, gfin_ref, sho_ref, sco_ref,
                  ys_ref, o_ref, buf, sem, *, ne, spb):
    i = pl.program_id(0) * spb + pl.program_id(1)
    nt = pl.num_programs(0) * spb
    tb = x1_ref.shape[1]
    lb = buf.shape[1] * SEG_ALIGN
    slot = lax.rem(i, 2)

    def fetch(t, sl):
        def meta(e):
            return (base_s[t * ne + e] // SEG_ALIGN, lstart_s[t * ne + e] // SEG_ALIGN,
                    pcnt_s[t * ne + e] // SEG_ALIGN)

        def make_copy(src, dst, n):
            return pltpu.make_async_copy(ys_ref.at[pl.ds(src, n)], buf.at[sl, pl.ds(dst, n)], sem.at[sl])

        _segment_copies(ne, meta, make_copy)

    @pl.when(i == 0)
    def _():
        buf[...] = jnp.zeros(buf.shape, F32)
        fetch(0, 0)

    @pl.when(i + 1 < nt)
    def _():
        fetch(i + 1, 1 - slot)

    total = (lstart_s[i * ne + ne - 1] + pcnt_s[i * ne + ne - 1]) // SEG_ALIGN
    done = buf.at[slot, pl.ds(0, total)]
    pltpu.make_async_copy(done, done, sem.at[slot]).wait()
    ls_col = ls_ref[...].astype(F32).T
    wt_col = wt_ref[...].T
    ffn = jnp.zeros((tb, x1_ref.shape[2]), F32)
    for r0 in range(0, lb, SORT_CHUNK):
        j = (lax.broadcasted_iota(I32, (tb, SORT_CHUNK), 1) + r0).astype(F32)
        wm = jnp.zeros((tb, SORT_CHUNK), F32)
        for kk in range(TOP_K):
            wm = jnp.where(j == ls_col[:, kk:kk + 1], wt_col[:, kk:kk + 1], wm)
        y_rows = buf[slot, r0 // SEG_ALIGN:(r0 + SORT_CHUNK) // SEG_ALIGN].reshape(SORT_CHUNK, x1_ref.shape[2])
        ffn = ffn + _dot(wm.astype(BF16), y_rows.astype(BF16))
    x2 = x1_ref[0] + gf_ref[0] * ffn
    o_ref[0] = _rms(x2, gfin_ref[...]) * (1.0 + sco_ref[0]) + sho_ref[0]


def _combine(plan, ls_t, wt_t, x1, mod3, g_final, fmod3, ys, tb):
    bsz, seq, d = x1.shape
    lstart, pcnt, base = plan
    nt, ne = lstart.shape
    spb = seq // tb
    lb = tb * TOP_K + ne * SEG_ALIGN
    return pl.pallas_call(
        functools.partial(_combine_body, ne=ne, spb=spb),
        grid_spec=pltpu.PrefetchScalarGridSpec(
            num_scalar_prefetch=3,
            grid=(bsz, spb),
            in_specs=[pl.BlockSpec((WT_ROWS, tb), lambda b, i, *_: (0, b * spb + i)),
                      pl.BlockSpec((WT_ROWS, tb), lambda b, i, *_: (0, b * spb + i)),
                      pl.BlockSpec((1, tb, d), lambda b, i, *_: (b, i, 0)),
                      pl.BlockSpec((1, 1, d), lambda b, i, *_: (b, 0, 5)),
                      pl.BlockSpec((1, d), lambda b, i, *_: (0, 0)),
                      pl.BlockSpec((1, 1, d), lambda b, i, *_: (b, 0, 0)),
                      pl.BlockSpec((1, 1, d), lambda b, i, *_: (b, 0, 1)),
                      pl.BlockSpec(memory_space=pl.ANY)],
            out_specs=pl.BlockSpec((1, tb, d), lambda b, i, *_: (b, i, 0)),
            scratch_shapes=[pltpu.VMEM((2, lb // SEG_ALIGN, SEG_ALIGN, d), F32), pltpu.SemaphoreType.DMA((2,))],
        ),
        out_shape=jax.ShapeDtypeStruct((bsz, seq, d), F32),
        compiler_params=_cparams("arbitrary", "arbitrary"),
        name="combine",
    )(lstart.reshape(-1), pcnt.reshape(-1), base.reshape(-1), ls_t, wt_t, x1, mod3, g_final.reshape(1, d),
      fmod3, fmod3, ys.reshape(-1, SEG_ALIGN, d))


def kernel(x, c, positions, ada_w, ada_b, final_ada_w, final_ada_b, norm_mix_g, norm_ffn_g, norm_final_g, w_in, ssm_lambda_re, ssm_lambda_im, ssm_log_dt, ssm_b_re, ssm_b_im, ssm_c_re, ssm_c_im, ssm_d, ssm_w_glu, ssm_b_glu, out_norm_ssm_g, out_norm_attn_g, w_out, router_w, router_b, exp_w_gate, exp_b_gate, exp_w_up, exp_b_up, exp_w_down, exp_b_down):
    bsz, seq, d = x.shape
    depth = ada_w.shape[0]
    ssm_w = ssm_d.shape[-1]
    attn_w = (w_in.shape[-1] - ssm_w) // 3
    ne = router_w.shape[-1]
    n_tok = bsz * seq
    nc = seq // SSM_CHUNK
    tg = 512

    rope_tab, rope_exp = _rope_tables(positions)
    fmod3 = _adaln(c, final_ada_w, final_ada_b).reshape(bsz, 1, 2 * d)
    for l in range(depth):
        mod3 = _adaln(c, ada_w[l], ada_b[l]).reshape(bsz, 1, -1)
        u4, q, k, v = _inproj(x, norm_mix_g[l], mod3, w_in[l].astype(BF16), rope_tab, rope_exp, ssm_w, attn_w)
        tables = _s5_params(ssm_lambda_re[l], ssm_lambda_im[l], ssm_log_dt[l], ssm_b_re[l], ssm_b_im[l],
                            ssm_c_re[l], ssm_c_im[l], ssm_d[l], nc)
        y4 = _s5(u4, *tables)
        attn = _attention(q, k, v)
        x1, h2, idx_t, wt_t, tile_cnt = _merge(y4, attn, x, ssm_w_glu[l].astype(BF16), ssm_b_glu[l], out_norm_ssm_g[l],
                                     out_norm_attn_g[l], w_out[l].astype(BF16), mod3, norm_ffn_g[l],
                                     router_w[l], router_b[l])
        tb = seq // (tile_cnt.shape[0] // bsz)
        cap = n_tok * TOP_K + tile_cnt.shape[0] * ne * SEG_ALIGN + ne * tg
        assert cap % tg == 0 and max(tb, tg) // SEG_ALIGN < 2 * SEG_UNITS[0]
        plan, tail, items = _route_plan(tile_cnt, tg, cap)
        xs, ls_t = _dispatch(h2.reshape(n_tok, d), idx_t, plan, tail, cap, tb, tg)
        ys = _experts(xs.reshape(cap, d), items, exp_w_gate[l], exp_b_gate[l], exp_w_up[l], exp_b_up[l],
                      exp_w_down[l], exp_b_down[l], tg)
        if l + 1 < depth:
            raise NotImplementedError("depth > 1 needs the non-final combine")
        x = _combine(plan, ls_t, wt_t, x1, mod3, norm_final_g, fmod3, ys, tb)
    return x
```

```python
import functools
import math

import jax
import jax.numpy as jnp
from jax import lax
from jax.experimental import pallas as pl
from jax.experimental.pallas import tpu as pltpu

F32 = jnp.float32
BF16 = jnp.bfloat16
I32 = jnp.int32
HIGHEST = lax.Precision.HIGHEST

LANES = 128
HEAD_DIM = 64
MOBA_BLOCK = 256
MOBA_TOPK = 3
ROT_DIM = HEAD_DIM // 4
ROPE_THETA = 500000.0
SSM_GROUP = 16
SSM_CHUNK = 16
TOP_K = 4
SWIGLU_ALPHA = 1.702
SWIGLU_LIMIT = 7.0
NORM_EPS = 1e-5
NEG_INF = -1e30
LOG2_E = math.log2(math.e)
KV_TILE = 2 * MOBA_BLOCK
PV_ROWS = HEAD_DIM + 8
WT_ROWS = 8
SEG_ALIGN = 8
SEG_UNITS = (64, 32, 16, 8, 4, 2, 1)
SORT_CHUNK = 256
VMEM_LIMIT = 56 * 1024 * 1024


def _cparams(*sem):
    return pltpu.CompilerParams(dimension_semantics=sem, vmem_limit_bytes=VMEM_LIMIT)


def _dot(a, b):
    return jnp.dot(a, b, preferred_element_type=F32)


def _rms(x, g):
    return x * lax.rsqrt(jnp.mean(x * x, axis=-1, keepdims=True) + NORM_EPS) * g


def _adaln_body(c_ref, w_ref, b_ref, o_ref):
    c = c_ref[...]
    ca = c * jax.nn.sigmoid(c)
    o_ref[...] = jnp.dot(ca, w_ref[...], preferred_element_type=F32, precision=HIGHEST) + b_ref[...]


def _adaln(c, w, b, tn=512):
    bsz, d = c.shape
    n = w.shape[1]
    return pl.pallas_call(
        _adaln_body,
        grid=(n // tn,),
        in_specs=[pl.BlockSpec((bsz, d), lambda j: (0, 0)),
                  pl.BlockSpec((d, tn), lambda j: (0, j)),
                  pl.BlockSpec((1, tn), lambda j: (0, j))],
        out_specs=pl.BlockSpec((bsz, tn), lambda j: (0, j)),
        out_shape=jax.ShapeDtypeStruct((bsz, n), F32),
        compiler_params=_cparams("parallel"),
        name="adaln",
    )(c, w, b.reshape(1, n))


def _inproj_body(x_ref, g_ref, sh_ref, sc_ref, w_ref, tab_ref, exp_ref, u_ref, q_ref, k_ref, v_ref, us_ref):
    x = x_ref[0]
    tm = x.shape[0]
    h = _rms(x, g_ref[...]) * (1.0 + sc_ref[0]) + sh_ref[0]
    proj = _dot(h.astype(BF16), w_ref[...])
    ssm_w = u_ref.shape[0] * LANES
    aw = q_ref.shape[2]
    for j in range(u_ref.shape[0]):
        us_ref[j] = proj[:, j * LANES:(j + 1) * LANES]
        for t in range(SSM_CHUNK):
            u_ref[j, 0, :, t * LANES:(t + 1) * LANES] = us_ref[
                j, pl.ds(t, tm // SSM_CHUNK, stride=SSM_CHUNK), :].astype(BF16)
    t_hi = tab_ref[0].astype(BF16)
    t_lo = (tab_ref[0] - t_hi.astype(F32)).astype(BF16)
    tn = (((0,), (0,)), ((), ()))
    tab = (lax.dot_general(t_hi, exp_ref[...], tn, preferred_element_type=F32)
           + lax.dot_general(t_lo, exp_ref[...], tn, preferred_element_type=F32))
    lane1 = lax.broadcasted_iota(I32, (tm, LANES), 1)
    cos1 = tab[:, :LANES] + jnp.where((lane1 & (HEAD_DIM - 1)) >= ROT_DIM, 1.0, 0.0)
    reps = aw // LANES
    cosf = jnp.concatenate([cos1] * reps, axis=1)
    sinf = jnp.concatenate([tab[:, LANES:]] * reps, axis=1)
    lane = lax.broadcasted_iota(I32, (tm, aw), 1)
    first_half = (lane & (HEAD_DIM - 1)) < (ROT_DIM // 2)

    def rope(t):
        partner = jnp.where(first_half, pltpu.roll(t, aw - ROT_DIM // 2, 1), pltpu.roll(t, ROT_DIM // 2, 1))
        return t * cosf + partner * sinf

    q = rope(proj[:, ssm_w:ssm_w + aw]) * (HEAD_DIM ** -0.5 * LOG2_E)
    k = rope(proj[:, ssm_w + aw:ssm_w + 2 * aw])
    q_ref[0] = q.astype(BF16)
    k_ref[0] = k.astype(BF16)
    v_ref[0] = proj[:, ssm_w + 2 * aw:ssm_w + 3 * aw].astype(BF16)


def _inproj(x, gain, mod3, w_in_bf, rope_tab, rope_exp, ssm_w, attn_w, tm=512):
    bsz, seq, d = x.shape
    n_u = ssm_w // LANES
    tm = min(tm, seq)
    return pl.pallas_call(
        _inproj_body,
        grid=(bsz, seq // tm),
        in_specs=[pl.BlockSpec((1, tm, d), lambda b, i: (b, i, 0)),
                  pl.BlockSpec((1, d), lambda b, i: (0, 0)),
                  pl.BlockSpec((1, 1, d), lambda b, i: (b, 0, 0)),
                  pl.BlockSpec((1, 1, d), lambda b, i: (b, 0, 1)),
                  pl.BlockSpec(w_in_bf.shape, lambda b, i: (0, 0)),
                  pl.BlockSpec((1, ROT_DIM, tm), lambda b, i: (b, 0, i)),
                  pl.BlockSpec(rope_exp.shape, lambda b, i: (0, 0))],
        out_specs=[pl.BlockSpec((n_u, 1, tm // SSM_CHUNK, SSM_CHUNK * LANES), lambda b, i: (0, b, i, 0)),
                   pl.BlockSpec((1, tm, attn_w), lambda b, i: (b, i, 0)),
                   pl.BlockSpec((1, tm, attn_w), lambda b, i: (b, i, 0)),
                   pl.BlockSpec((1, tm, attn_w), lambda b, i: (b, i, 0))],
        out_shape=[jax.ShapeDtypeStruct((n_u, bsz, seq // SSM_CHUNK, SSM_CHUNK * LANES), BF16),
                   jax.ShapeDtypeStruct((bsz, seq, attn_w), BF16),
                   jax.ShapeDtypeStruct((bsz, seq, attn_w), BF16),
                   jax.ShapeDtypeStruct((bsz, seq, attn_w), BF16)],
        scratch_shapes=[pltpu.VMEM((n_u, tm, LANES), F32)],
        compiler_params=_cparams("parallel", "parallel"),
        name="inproj",
    )(x, gain.reshape(1, d), mod3, mod3, w_in_bf, rope_tab, rope_exp)


def _rope_tables(positions):
    half = ROT_DIM // 2
    inv_freq = ROPE_THETA ** (-jnp.arange(0, ROT_DIM, 2, dtype=F32) / ROT_DIM)
    ang = positions.astype(F32)[:, None, :] * inv_freq[None, :, None]
    tab = jnp.concatenate([jnp.cos(ang), jnp.sin(ang)], axis=1)
    i = jnp.arange(ROT_DIM)[:, None]
    hl = jnp.arange(LANES)[None, :] % HEAD_DIM
    e_cos = jnp.where((i < half) & ((hl == i) | (hl == i + half)), 1.0, 0.0)
    e_sin = jnp.where(i >= half, jnp.where(hl == i - half, -1.0, jnp.where(hl == i, 1.0, 0.0)), 0.0)
    return tab, jnp.concatenate([e_cos, e_sin], axis=1).astype(BF16)


def _s5_params(lam_re, lam_im, log_dt, b_re, b_im, c_re, c_im, d_skip, n_chunks):
    g_all, p = lam_re.shape
    hc = b_re.shape[-1]
    t = SSM_CHUNK
    gpb = LANES // hc
    nblk = g_all // gpb
    lr, li = lam_re.astype(F32), lam_im.astype(F32)
    dt = jnp.exp(log_dt.astype(F32))[:, None]
    ldr, ldi = lr * dt, li * dt

    def apow(k):
        k = jnp.asarray(k, F32)[..., None, None]
        mag = jnp.exp(ldr * k)
        return mag * jnp.cos(ldi * k), mag * jnp.sin(ldi * k)

    ar, ai = apow(1.0)
    zr, zi = ar - 1.0, ai
    den = lr * lr + li * li
    cr = (zr * lr + zi * li) / den
    ci = (zi * lr - zr * li) / den
    bre, bim = b_re.astype(F32), b_im.astype(F32)
    bbr = cr[..., None] * bre - ci[..., None] * bim
    bbi = cr[..., None] * bim + ci[..., None] * bre
    cre, cim = c_re.astype(F32), c_im.astype(F32)
    def block_diag(a):
        w = a.shape[-1]
        wide = jnp.concatenate([a] * gpb, axis=-1)
        grp = jnp.arange(gpb)[:, None, None]
        lane_grp = (jnp.arange(gpb * w) // w)[None, None, :]
        wide = jnp.where(lane_grp == grp, wide, 0.0)
        return wide.reshape(a.shape[:-3] + (gpb * a.shape[-2], gpb * w))

    pr, pi = apow(jnp.arange(t))
    abr = pr[..., None] * bbr - pi[..., None] * bbi
    abi = pr[..., None] * bbi + pi[..., None] * bbr
    kf = (jnp.einsum('gnp,kgph->kghn', cre, abr, precision=HIGHEST)
          - jnp.einsum('gnp,kgph->kghn', cim, abi, precision=HIGHEST))
    kblk = block_diag(kf.reshape(t, nblk, gpb, hc, hc))
    kpad = jnp.concatenate([jnp.zeros_like(kblk[:1]), kblk], axis=0)
    pm = jnp.concatenate([kpad[0:t], kpad[1:t + 1]], axis=-1)
    kstack = jnp.transpose(pm[::-1], (1, 0, 2, 3)).reshape(nblk, t * LANES, 2 * LANES)

    def inject(ab):
        ab = jnp.transpose(ab[::-1].reshape(t, nblk, gpb, p, hc), (1, 0, 2, 4, 3))
        return block_diag(ab).reshape(nblk, t * LANES, gpb * p)

    bm = jnp.concatenate([inject(abr), inject(abi)], axis=-1)

    qr, qi = apow(jnp.arange(t) + 1.0)
    c_from_re = cre[None] * qr[:, :, None, :] - cim[None] * qi[:, :, None, :]
    c_from_im = -cre[None] * qi[:, :, None, :] - cim[None] * qr[:, :, None, :]

    def readout_t(cc):
        cc = jnp.transpose(cc.reshape(t, nblk, gpb, hc, p), (1, 0, 2, 3, 4))
        return block_diag(cc).reshape(nblk, t * LANES, gpb * p)

    cm = jnp.concatenate([readout_t(c_from_re), readout_t(c_from_im)], axis=-1)

    n_steps = max(1, int(math.log2(n_chunks)))
    shifts = [float(t * (1 << s)) for s in range(n_steps)]
    sr, si = apow(jnp.asarray(shifts))
    ap = jnp.concatenate([sr.reshape(n_steps, nblk, gpb * p), si.reshape(n_steps, nblk, gpb * p)], axis=-1)
    ap = jnp.transpose(ap, (1, 0, 2))
    dvec = jnp.tile(d_skip.astype(F32).reshape(nblk, 1, LANES), (1, 1, t))
    return kstack.astype(BF16), bm.astype(BF16), cm.astype(BF16), ap, dvec


def _s5_body(x_ref, ks_ref, bm_ref, cm_ref, ap_ref, d_ref, y_ref, h_ref):
    nc = x_ref.shape[2]
    t = SSM_CHUNK
    pad = h_ref.shape[0] - nc
    half = h_ref.shape[1] // 2
    x = x_ref[0, 0]
    h_ref[0:pad] = jnp.zeros((pad, h_ref.shape[1]), F32)
    h_ref[pad:pad + nc] = _dot(x, bm_ref[0])
    for step in range(ap_ref.shape[1]):
        d = 1 << step
        cur = h_ref[pad:pad + nc]
        sft = h_ref[pad - d:pad + nc - d]
        ar = ap_ref[0, step:step + 1, 0:half]
        ai = ap_ref[0, step:step + 1, half:]
        cr, ci = cur[:, :half], cur[:, half:]
        sr, si = sft[:, :half], sft[:, half:]
        h_ref[pad:pad + nc, 0:half] = cr + ar * sr - ai * si
        h_ref[pad:pad + nc, half:] = ci + ar * si + ai * sr
    h_prev = h_ref[pad - 1:pad + nc - 1]
    ys = lax.dot_general(h_prev.astype(BF16), cm_ref[0], (((1,), (1,)), ((), ())),
                         preferred_element_type=F32)
    for t0 in range(0, t, 2):
        lo, hi = t0 * LANES, (t0 + 2) * LANES
        conv = _dot(x[:, :hi], ks_ref[0, (t - 2 - t0) * LANES:, :])
        y = conv + ys[:, lo:hi] + d_ref[0][:, lo:hi] * x[:, lo:hi].astype(F32)
        y_ref[0, 0, :, lo:hi] = y.astype(y_ref.dtype)


def _s5(u4, kstack, bm, cm, ap, dvec):
    nblk, bsz, nc, w = u4.shape
    sw = bm.shape[-1]
    pad = max(nc // 2, 8)
    return pl.pallas_call(
        _s5_body,
        grid=(nblk, bsz),
        in_specs=[pl.BlockSpec((1, 1, nc, w), lambda j, b: (j, b, 0, 0)),
                  pl.BlockSpec((1,) + kstack.shape[1:], lambda j, b: (j, 0, 0)),
                  pl.BlockSpec((1,) + bm.shape[1:], lambda j, b: (j, 0, 0)),
                  pl.BlockSpec((1,) + cm.shape[1:], lambda j, b: (j, 0, 0)),
                  pl.BlockSpec((1,) + ap.shape[1:], lambda j, b: (j, 0, 0)),
                  pl.BlockSpec((1, 1, w), lambda j, b: (j, 0, 0))],
        out_specs=pl.BlockSpec((1, 1, nc, w), lambda j, b: (j, b, 0, 0)),
        out_shape=jax.ShapeDtypeStruct((nblk, bsz, nc, w), BF16),
        scratch_shapes=[pltpu.VMEM((pad + nc, sw), F32)],
        compiler_params=_cparams("parallel", "parallel"),
        name="s5",
    )(u4, kstack, bm, cm, ap, dvec)


def _attn_body(q_ref, k_ref, v_ref, o_ref, vt_ref, km_ref, bias_ref, sa_ref, sb_ref):
    qi = pl.program_id(2)
    seq = k_ref.shape[1]
    tq = q_ref.shape[1]
    nb = seq // MOBA_BLOCK
    n_kv = seq // KV_TILE
    bpt = KV_TILE // MOBA_BLOCK
    blk_shift = MOBA_BLOCK.bit_length() - 1

    @pl.when(qi == 0)
    def _build():
        k = k_ref[0].astype(F32)
        v = v_ref[0].astype(F32)
        ones = jnp.where(lax.broadcasted_iota(I32, (PV_ROWS - HEAD_DIM, KV_TILE), 0) == 0, 1.0, 0.0)
        for t in range(n_kv):
            v_t = v[t * KV_TILE:(t + 1) * KV_TILE].T
            vt_ref[0, t] = jnp.concatenate([v_t[:HEAD_DIM], ones], axis=0).astype(BF16)
            vt_ref[1, t] = jnp.concatenate([v_t[HEAD_DIM:], ones], axis=0).astype(BF16)
        km = jnp.mean(k.reshape(nb, MOBA_BLOCK, LANES), axis=1)
        lane_b = lax.broadcasted_iota(I32, (nb, LANES), 1)
        km_ref[0] = jnp.where(lane_b < HEAD_DIM, km, 0.0)
        km_ref[1] = jnp.where(lane_b >= HEAD_DIM, km, 0.0)

    q_t = q_ref[0].astype(F32).T
    feat = lax.broadcasted_iota(I32, (LANES, tq), 0)
    q_h = [jnp.where(feat < HEAD_DIM, q_t, 0.0).astype(BF16),
           jnp.where(feat >= HEAD_DIM, q_t, 0.0).astype(BF16)]

    jidx = lax.broadcasted_iota(I32, (nb, tq), 0)
    qblk = qi * (tq // MOBA_BLOCK) + lax.shift_right_logical(
        lax.broadcasted_iota(I32, (nb, tq), 1), blk_shift)
    for hx in range(2):
        g = _dot(km_ref[hx], q_t)
        cnt = jnp.zeros((nb, tq), F32)
        for jp in range(nb):
            row = g[jp:jp + 1, :]
            beats = jnp.where(row > g, 1.0, jnp.where(row == g, jnp.where(jp < jidx, 1.0, 0.0), 0.0))
            cnt = cnt + jnp.where(jp < qblk, beats, 0.0)
        past_sel = jnp.where(jidx < qblk, jnp.where(cnt < MOBA_TOPK, 1.0, 0.0), 0.0)
        sel = jnp.where(jidx == qblk, 1.0, past_sel)
        bias_ref[hx] = jnp.where(sel > 0.5, 0.0, NEG_INF)

    def raw_scores(dst_ref, tile):
        off = pl.multiple_of(tile * KV_TILE, KV_TILE)
        k_t = k_ref[0, pl.ds(off, KV_TILE), :]
        for hx in range(2):
            dst_ref[hx] = _dot(k_t, q_h[hx])

    def fold(carry, src_ref, tile, causal=None):
        out = []
        for hx in range(2):
            parts, biases = [], []
            for j in range(bpt):
                s_j = src_ref[hx, j * MOBA_BLOCK:(j + 1) * MOBA_BLOCK, :]
                if causal is not None:
                    s_j = jnp.where(causal[j], s_j, NEG_INF)
                b_j = bias_ref[hx, pl.ds(tile * bpt + j, 1), :]
                parts.append(s_j)
                biases.append(b_j)
            m_n = jnp.max(parts[0], axis=0, keepdims=True) + biases[0]
            for s_j, b_j in zip(parts[1:], biases[1:]):
                m_n = jnp.maximum(m_n, jnp.max(s_j, axis=0, keepdims=True) + b_j)
            if carry is None:
                p_t = jnp.concatenate([jnp.exp2(s_j + (b_j - m_n)) for s_j, b_j in zip(parts, biases)], axis=0)
                out += [m_n, _dot(vt_ref[hx, tile], p_t.astype(BF16))]
            else:
                m_c, acc_c = carry[2 * hx], carry[2 * hx + 1]
                m_n = jnp.maximum(m_c, m_n)
                p_t = jnp.concatenate([jnp.exp2(s_j + (b_j - m_n)) for s_j, b_j in zip(parts, biases)], axis=0)
                out += [m_n, acc_c * jnp.exp2(m_c - m_n) + _dot(vt_ref[hx, tile], p_t.astype(BF16))]
        return tuple(out)

    raw_scores(sa_ref, qi)
    raw_scores(sb_ref, 0)
    krow = lax.broadcasted_iota(I32, (MOBA_BLOCK, tq), 0)
    qcol = lax.broadcasted_iota(I32, (MOBA_BLOCK, tq), 1)
    state = fold(None, sa_ref, qi, causal=[krow + j * MOBA_BLOCK <= qcol for j in range(bpt)])

    def body(i, carry):
        t0 = 2 * i
        raw_scores(sa_ref, t0 + 1)
        carry = fold(carry, sb_ref, t0)
        raw_scores(sb_ref, jnp.minimum(t0 + 2, n_kv - 1))
        return fold(carry, sa_ref, t0 + 1)

    state = lax.fori_loop(0, qi // 2, body, state)
    _, acc_a, _, acc_b = lax.cond(lax.rem(qi, 2) == 1, lambda c: fold(c, sb_ref, qi - 1), lambda c: c, state)
    o_t = jnp.concatenate([acc[:HEAD_DIM] / acc[HEAD_DIM:HEAD_DIM + 1] for acc in (acc_a, acc_b)], axis=0)
    o_ref[0] = o_t.T.astype(o_ref.dtype)


def _attention(q, k, v):
    bsz, seq, aw = q.shape
    npair = aw // LANES
    tq = KV_TILE
    nb = seq // MOBA_BLOCK
    assert seq % KV_TILE == 0 and nb % 8 == 0
    return pl.pallas_call(
        _attn_body,
        grid=(bsz, npair, seq // tq),
        in_specs=[pl.BlockSpec((1, tq, LANES), lambda b, p, i: (b, i, p)),
                  pl.BlockSpec((1, seq, LANES), lambda b, p, i: (b, 0, p)),
                  pl.BlockSpec((1, seq, LANES), lambda b, p, i: (b, 0, p))],
        out_specs=pl.BlockSpec((1, tq, LANES), lambda b, p, i: (b, i, p)),
        out_shape=jax.ShapeDtypeStruct((bsz, seq, aw), BF16),
        scratch_shapes=[pltpu.VMEM((2, seq // KV_TILE, PV_ROWS, KV_TILE), BF16),
                        pltpu.VMEM((2, nb, LANES), F32),
                        pltpu.VMEM((2, nb, tq), F32),
                        pltpu.VMEM((2, KV_TILE, tq), F32),
                        pltpu.VMEM((2, KV_TILE, tq), F32)],
        compiler_params=_cparams("parallel", "parallel", "arbitrary"),
        name="attn",
    )(q, k, v)


def _merge_body(y_ref, a_ref, x_ref, wglu_ref, bglu_ref, gs_ref, ga_ref, wout_ref, gm_ref,
                gf_ref, shf_ref, scf_ref, rw_ref, rb_ref, x1_ref, h2_ref, idx_ref, wt_ref, cnt_ref, ys_ref):
    nblk = y_ref.shape[0]
    tm = x_ref.shape[1]
    for j in range(nblk):
        for t in range(SSM_CHUNK):
            ys_ref[j, pl.ds(t, tm // SSM_CHUNK, stride=SSM_CHUNK), :] = (
                y_ref[j, 0, :, t * LANES:(t + 1) * LANES].astype(F32))
    y = jnp.concatenate([ys_ref[j] for j in range(nblk)], axis=1)
    g = 0.5 * y * (1.0 + lax.erf(y * (2.0 ** -0.5)))
    glu = g * jax.nn.sigmoid(_dot(g.astype(BF16), wglu_ref[...]) + bglu_ref[...])
    ssm_n = _rms(glu, gs_ref[...])
    att_n = _rms(a_ref[0].astype(F32), ga_ref[...])
    merged = jnp.concatenate([ssm_n, att_n], axis=1).astype(BF16)
    mix = _dot(merged, wout_ref[...])
    x1 = x_ref[0] + gm_ref[0] * mix
    x1_ref[0] = x1
    h2 = _rms(x1, gf_ref[...]) * (1.0 + scf_ref[0]) + shf_ref[0]
    h2_ref[0] = h2.astype(BF16)
    logits = lax.dot_general(rw_ref[...], h2, (((1,), (1,)), ((), ())),
                             preferred_element_type=F32, precision=HIGHEST) + rb_ref[...]
    ne = logits.shape[0]
    sub = lax.broadcasted_iota(I32, (ne, tm), 0)
    vals, idxs = [], []
    for _ in range(TOP_K):
        mx = jnp.max(logits, axis=0, keepdims=True)
        ix = jnp.min(jnp.where(logits == mx, sub, ne), axis=0, keepdims=True)
        vals.append(mx)
        idxs.append(ix)
        logits = jnp.where(sub == ix, -jnp.inf, logits)
    exps = [jnp.exp(vv - vals[0]) for vv in vals]
    tot = exps[0]
    for e in exps[1:]:
        tot = tot + e
    row_i = lax.broadcasted_iota(I32, idx_ref.shape, 0)
    row_w = lax.broadcasted_iota(I32, wt_ref.shape, 0)
    idx_out = jnp.zeros(idx_ref.shape, I32)
    wt_out = jnp.zeros(wt_ref.shape, F32)
    for kk in range(TOP_K):
        idx_out = jnp.where(row_i == kk, idxs[kk], idx_out)
        wt_out = jnp.where(row_w == kk, exps[kk] / tot, wt_out)
    idx_ref[...] = idx_out
    wt_ref[...] = wt_out
    chosen = jnp.zeros((ne, tm), F32)
    for kk in range(TOP_K):
        chosen = chosen + jnp.where(sub == idxs[kk], 1.0, 0.0)
    cnt_ref[0] = jnp.sum(chosen, axis=1, keepdims=True).astype(I32)


def _merge(y4, attn, x, w_glu_bf, b_glu, g_ssm, g_attn, w_out_bf, mod3, g_ffn, router_w, router_b, tm=512):
    bsz, seq, d = x.shape
    nblk = y4.shape[0]
    ssm_w = nblk * LANES
    aw = attn.shape[-1]
    ne = router_w.shape[1]
    tm = min(tm, seq)
    spb = seq // tm
    n_tok = bsz * seq
    rw_t = router_w.T
    row = lambda n: pl.BlockSpec((1, n), lambda b, i: (0, 0))
    full = lambda a: pl.BlockSpec(a.shape, lambda b, i: (0,) * a.ndim)
    modv = lambda j: pl.BlockSpec((1, 1, d), lambda b, i: (b, 0, j))
    return pl.pallas_call(
        _merge_body,
        grid=(bsz, seq // tm),
        in_specs=[pl.BlockSpec((nblk, 1, tm // SSM_CHUNK, SSM_CHUNK * LANES), lambda b, i: (0, b, i, 0)),
                  pl.BlockSpec((1, tm, aw), lambda b, i: (b, i, 0)),
                  pl.BlockSpec((1, tm, d), lambda b, i: (b, i, 0)),
                  full(w_glu_bf), row(ssm_w), row(ssm_w), row(aw), full(w_out_bf),
                  modv(2), row(d), modv(3), modv(4), full(rw_t),
                  pl.BlockSpec((ne, 1), lambda b, i: (0, 0))],
        out_specs=[pl.BlockSpec((1, tm, d), lambda b, i: (b, i, 0)),
                   pl.BlockSpec((1, tm, d), lambda b, i: (b, i, 0)),
                   pl.BlockSpec((TOP_K, tm), lambda b, i: (0, b * spb + i)),
                   pl.BlockSpec((WT_ROWS, tm), lambda b, i: (0, b * spb + i)),
                   pl.BlockSpec((1, ne, 1), lambda b, i: (b * spb + i, 0, 0))],
        out_shape=[jax.ShapeDtypeStruct((bsz, seq, d), F32),
                   jax.ShapeDtypeStruct((bsz, seq, d), BF16),
                   jax.ShapeDtypeStruct((TOP_K, n_tok), I32),
                   jax.ShapeDtypeStruct((WT_ROWS, n_tok), F32),
                   jax.ShapeDtypeStruct((bsz * spb, ne, 1), I32)],
        scratch_shapes=[pltpu.VMEM((nblk, tm, LANES), F32)],
        compiler_params=_cparams("parallel", "parallel"),
        name="merge",
    )(y4, attn, x, w_glu_bf, b_glu.reshape(1, ssm_w), g_ssm.reshape(1, ssm_w), g_attn.reshape(1, aw),
      w_out_bf, mod3, g_ffn.reshape(1, d), mod3, mod3, rw_t, router_b.reshape(ne, 1))


def _segment_copies(n_seg, meta, make_copy):
    def body(e, _):
        src0, dst0, units = meta(e)
        for sz in SEG_UNITS:
            off = units & ~(2 * sz - 1)

            @pl.when((units & sz) != 0)
            def _(off=off, sz=sz):
                make_copy(src0 + off, dst0 + off, sz).start()
        return 0

    lax.fori_loop(0, n_seg, body, 0)


def _local_slots(idx, lstart_col, ne):
    tb = idx.shape[1]
    sub = lax.broadcasted_iota(I32, (ne, tb), 0)
    hits = [idx[kk:kk + 1, :] == sub for kk in range(TOP_K)]
    onehot = jnp.zeros((ne, tb), F32)
    for h in hits:
        onehot = onehot + jnp.where(h, 1.0, 0.0)
    r = lax.broadcasted_iota(I32, (tb, tb), 0)
    c = lax.broadcasted_iota(I32, (tb, tb), 1)
    tri = jnp.where(r < c, 1.0, 0.0).astype(BF16)
    slot = _dot(onehot.astype(BF16), tri) + lstart_col
    return [jnp.sum(jnp.where(h, slot, 0.0), axis=0, keepdims=True) for h in hits]


def _dispatch_body(lstart_s, pcnt_s, base_s, tail_dst_s, tail_units_s, tail_total_s,
                   idx_ref, lcol_ref, h_ref, xs_ref, ls_ref, sorted_ref, zero_ref, sem, *, ne):
    i = pl.program_id(0)
    nt = pl.num_programs(0)
    tb = h_ref.shape[0]
    lb = sorted_ref.shape[1] * SEG_ALIGN
    slot = lax.rem(i, 2)

    @pl.when(i == 0)
    def _():
        zero_ref[...] = jnp.zeros(zero_ref.shape, F32)
        _segment_copies(
            ne, lambda e: (0, tail_dst_s[e] // SEG_ALIGN, tail_units_s[e]),
            lambda src, dst, n: pltpu.make_async_copy(zero_ref.at[pl.ds(src, n)], xs_ref.at[pl.ds(dst, n)],
                                                      sem.at[2]))

    def tile_units(t):
        return (lstart_s[t * ne + ne - 1] + pcnt_s[t * ne + ne - 1]) // SEG_ALIGN

    def wait_tile(t, sl):
        done = xs_ref.at[pl.ds(0, tile_units(t))]
        pltpu.make_async_copy(done, done, sem.at[sl]).wait()

    @pl.when(i >= 2)
    def _():
        wait_tile(i - 2, slot)

    slots = _local_slots(idx_ref[...], lcol_ref[0].astype(F32), ne)
    row_o = lax.broadcasted_iota(I32, ls_ref.shape, 0)
    ls_out = jnp.full(ls_ref.shape, -1.0, F32)
    for kk in range(TOP_K):
        ls_out = jnp.where(row_o == kk, slots[kk], ls_out)
    ls_ref[...] = ls_out.astype(I32)
    h = h_ref[...]
    for r0 in range(0, lb, SORT_CHUNK):
        j = (lax.broadcasted_iota(I32, (SORT_CHUNK, tb), 0) + r0).astype(F32)
        pm = jnp.zeros((SORT_CHUNK, tb), F32)
        for kk in range(TOP_K):
            pm = jnp.where(j == slots[kk], 1.0, pm)
        sorted_ref[slot, r0 // SEG_ALIGN:(r0 + SORT_CHUNK) // SEG_ALIGN] = _dot(pm.astype(BF16), h).reshape(
            SORT_CHUNK // SEG_ALIGN, SEG_ALIGN, h.shape[1])

    def meta(e):
        return (lstart_s[i * ne + e] // SEG_ALIGN, base_s[i * ne + e] // SEG_ALIGN,
                pcnt_s[i * ne + e] // SEG_ALIGN)

    def make_copy(src, dst, n):
        return pltpu.make_async_copy(sorted_ref.at[slot, pl.ds(src, n)], xs_ref.at[pl.ds(dst, n)], sem.at[slot])

    _segment_copies(ne, meta, make_copy)

    @pl.when(i == nt - 1)
    def _():
        @pl.when(i >= 1)
        def _():
            wait_tile(i - 1, 1 - slot)
        wait_tile(i, slot)

        @pl.when(tail_total_s[0] > 0)
        def _():
            filled = xs_ref.at[pl.ds(0, tail_total_s[0])]
            pltpu.make_async_copy(filled, filled, sem.at[2]).wait()


def _dispatch(h2, idx_t, plan, tail, cap, tb, tg):
    n, d = h2.shape
    lstart, pcnt, base = plan
    nt, ne = lstart.shape
    lb = tb * TOP_K + ne * SEG_ALIGN
    assert lb % SORT_CHUNK == 0
    return pl.pallas_call(
        functools.partial(_dispatch_body, ne=ne),
        grid_spec=pltpu.PrefetchScalarGridSpec(
            num_scalar_prefetch=6,
            grid=(nt,),
            in_specs=[pl.BlockSpec((TOP_K, tb), lambda i, *_: (0, i)),
                      pl.BlockSpec((1, ne, 1), lambda i, *_: (i, 0, 0)),
                      pl.BlockSpec((tb, d), lambda i, *_: (i, 0))],
            out_specs=[pl.BlockSpec(memory_space=pl.ANY),
                       pl.BlockSpec((WT_ROWS, tb), lambda i, *_: (0, i))],
            scratch_shapes=[pltpu.VMEM((2, lb // SEG_ALIGN, SEG_ALIGN, d), F32),
                            pltpu.VMEM((tg // SEG_ALIGN, SEG_ALIGN, d), F32),
                            pltpu.SemaphoreType.DMA((3,))],
        ),
        out_shape=[jax.ShapeDtypeStruct((cap // SEG_ALIGN, SEG_ALIGN, d), F32),
                   jax.ShapeDtypeStruct((WT_ROWS, n), I32)],
        compiler_params=_cparams("arbitrary"),
        name="dispatch",
    )(lstart.reshape(-1), pcnt.reshape(-1), base.reshape(-1), *tail, idx_t, lstart.reshape(nt, ne, 1), h2)


def _route_plan(tile_cnt, tg, cap):
    cnt = tile_cnt[:, :, 0]
    ne = cnt.shape[1]
    pcnt = (cnt + SEG_ALIGN - 1) // SEG_ALIGN * SEG_ALIGN
    lstart = jnp.cumsum(pcnt, axis=1) - pcnt
    e_rows = jnp.sum(pcnt, axis=0)
    e_tiles = (e_rows + tg - 1) // tg
    e_end = jnp.cumsum(e_tiles)
    base = ((e_end - e_tiles) * tg)[None, :] + jnp.cumsum(pcnt, axis=0) - pcnt
    ii = jnp.arange(cap // tg, dtype=I32)
    used = ii < e_end[-1]
    tile = jnp.minimum(ii, e_end[-1] - 1)
    e_of = jnp.minimum(jnp.sum((e_end[None, :] <= tile[:, None]).astype(I32), axis=1), ne - 1)
    newe = (e_of != jnp.concatenate([jnp.full((1,), -1, I32), e_of[:-1]])).astype(I32)
    items = (tile.astype(I32), e_of.astype(I32), used.astype(I32), newe)
    tail = (e_end * tg - e_tiles * tg + e_rows, (e_tiles * tg - e_rows) // SEG_ALIGN)
    tail = (tail[0].astype(I32), tail[1].astype(I32), jnp.sum(tail[1]).astype(I32).reshape(1))
    return (lstart.astype(I32), pcnt.astype(I32), base.astype(I32)), tail, items


def _experts_body(tile_ref, exp_ref, used_ref, newe_ref,
                  xs_ref, wg_ref, bg_ref, wu_ref, bu_ref, wd_ref, bd_ref, ys_ref,
                  wg_bf, wu_bf, wd_bf):
    i = pl.program_id(0)

    @pl.when(newe_ref[i] == 1)
    def _cast():
        wg_bf[...] = wg_ref[0].astype(BF16)
        wu_bf[...] = wu_ref[0].astype(BF16)
        wd_bf[...] = wd_ref[0].astype(BF16)

    @pl.when(used_ref[i] == 1)
    def _compute():
        x = xs_ref[...].astype(BF16)
        gate = jnp.minimum(_dot(x, wg_bf[...]) + bg_ref[0], SWIGLU_LIMIT)
        lin = jnp.clip(_dot(x, wu_bf[...]) + bu_ref[0], -SWIGLU_LIMIT, SWIGLU_LIMIT)
        act = gate * jax.nn.sigmoid(SWIGLU_ALPHA * gate) * (lin + 1.0)
        ys_ref[...] = _dot(act.astype(BF16), wd_bf[...]) + bd_ref[0]


def _experts(xs, items, w_gate, b_gate, w_up, b_up, w_down, b_down, tg):
    nk, d = xs.shape
    ne, _, dff = w_gate.shape
    n_items = items[0].shape[0]
    wspec = lambda shp: pl.BlockSpec((1,) + shp, lambda i, t, e, u, nw: (e[i], 0, 0))
    return pl.pallas_call(
        _experts_body,
        grid_spec=pltpu.PrefetchScalarGridSpec(
            num_scalar_prefetch=4,
            grid=(n_items,),
            in_specs=[pl.BlockSpec((tg, d), lambda i, t, e, u, nw: (t[i], 0)),
                      wspec((d, dff)), wspec((1, dff)),
                      wspec((d, dff)), wspec((1, dff)),
                      wspec((dff, d)), wspec((1, d))],
            out_specs=pl.BlockSpec((tg, d), lambda i, t, e, u, nw: (t[i], 0)),
            scratch_shapes=[pltpu.VMEM((d, dff), BF16), pltpu.VMEM((d, dff), BF16), pltpu.VMEM((dff, d), BF16)],
        ),
        out_shape=jax.ShapeDtypeStruct((nk, d), F32),
        compiler_params=_cparams("arbitrary"),
        name="experts",
    )(*items, xs, w_gate, b_gate.reshape(ne, 1, dff), w_up, b_up.reshape(ne, 1, dff),
      w_down, b_down.reshape(ne, 1, d))


def _combine_body(lstart_s, pcnt_s, base_s, ls_ref, wt_ref, x1_ref, gf_ref, gfin_ref, sho_ref, sco_ref,
                  ys_ref, o_ref, buf, sem, *, ne, spb):
    i = pl.program_id(0) * spb + pl.program_id(1)
    nt = pl.num_programs(0) * spb
    tb = x1_ref.shape[1]
    lb = buf.shape[1] * SEG_ALIGN
    slot = lax.rem(i, 2)

    def fetch(t, sl):
        def meta(e):
            return (base_s[t * ne + e] // SEG_ALIGN, lstart_s[t * ne + e] // SEG_ALIGN,
                    pcnt_s[t * ne + e] // SEG_ALIGN)

        def make_copy(src, dst, n):
            return pltpu.make_async_copy(ys_ref.at[pl.ds(src, n)], buf.at[sl, pl.ds(dst, n)], sem.at[sl])

        _segment_copies(ne, meta, make_copy)

    @pl.when(i == 0)
    def _():
        buf[...] = jnp.zeros(buf.shape, F32)
        fetch(0, 0)

    @pl.when(i + 1 < nt)
    def _():
        fetch(i + 1, 1 - slot)

    total = (lstart_s[i * ne + ne - 1] + pcnt_s[i * ne + ne - 1]) // SEG_ALIGN
    done = buf.at[slot, pl.ds(0, total)]
    pltpu.make_async_copy(done, done, sem.at[slot]).wait()
    ls_col = ls_ref[...].astype(F32).T
    wt_col = wt_ref[...].T
    ffn = jnp.zeros((tb, x1_ref.shape[2]), F32)
    for r0 in range(0, lb, SORT_CHUNK):
        j = (lax.broadcasted_iota(I32, (tb, SORT_CHUNK), 1) + r0).astype(F32)
        wm = jnp.zeros((tb, SORT_CHUNK), F32)
        for kk in range(TOP_K):
            wm = jnp.where(j == ls_col[:, kk:kk + 1], wt_col[:, kk:kk + 1], wm)
        y_rows = buf[slot, r0 // SEG_ALIGN:(r0 + SORT_CHUNK) // SEG_ALIGN].reshape(SORT_CHUNK, x1_ref.shape[2])
        ffn = ffn + _dot(wm.astype(BF16), y_rows.astype(BF16))
    x2 = x1_ref[0] + gf_ref[0] * ffn
    o_ref[0] = _rms(x2, gfin_ref[...]) * (1.0 + sco_ref[0]) + sho_ref[0]


def _combine(plan, ls_t, wt_t, x1, mod3, g_final, fmod3, ys, tb):
    bsz, seq, d = x1.shape
    lstart, pcnt, base = plan
    nt, ne = lstart.shape
    spb = seq // tb
    lb = tb * TOP_K + ne * SEG_ALIGN
    return pl.pallas_call(
        functools.partial(_combine_body, ne=ne, spb=spb),
        grid_spec=pltpu.PrefetchScalarGridSpec(
            num_scalar_prefetch=3,
            grid=(bsz, spb),
            in_specs=[pl.BlockSpec((WT_ROWS, tb), lambda b, i, *_: (0, b * spb + i)),
                      pl.BlockSpec((WT_ROWS, tb), lambda b, i, *_: (0, b * spb + i)),
                      pl.BlockSpec((1, tb, d), lambda b, i, *_: (b, i, 0)),
                      pl.BlockSpec((1, 1, d), lambda b, i, *_: (b, 0, 5)),
                      pl.BlockSpec((1, d), lambda b, i, *_: (0, 0)),
                      pl.BlockSpec((1, 1, d), lambda b, i, *_: (b, 0, 0)),
                      pl.BlockSpec((1, 1, d), lambda b, i, *_: (b, 0, 1)),
                      pl.BlockSpec(memory_space=pl.ANY)],
            out_specs=pl.BlockSpec((1, tb, d), lambda b, i, *_: (b, i, 0)),
            scratch_shapes=[pltpu.VMEM((2, lb // SEG_ALIGN, SEG_ALIGN, d), F32), pltpu.SemaphoreType.DMA((2,))],
        ),
        out_shape=jax.ShapeDtypeStruct((bsz, seq, d), F32),
        compiler_params=_cparams("arbitrary", "arbitrary"),
        name="combine",
    )(lstart.reshape(-1), pcnt.reshape(-1), base.reshape(-1), ls_t, wt_t, x1, mod3, g_final.reshape(1, d),
      fmod3, fmod3, ys.reshape(-1, SEG_ALIGN, d))


def kernel(x, c, positions, ada_w, ada_b, final_ada_w, final_ada_b, norm_mix_g, norm_ffn_g, norm_final_g, w_in, ssm_lambda_re, ssm_lambda_im, ssm_log_dt, ssm_b_re, ssm_b_im, ssm_c_re, ssm_c_im, ssm_d, ssm_w_glu, ssm_b_glu, out_norm_ssm_g, out_norm_attn_g, w_out, router_w, router_b, exp_w_gate, exp_b_gate, exp_w_up, exp_b_up, exp_w_down, exp_b_down):
    bsz, seq, d = x.shape
    depth = ada_w.shape[0]
    ssm_w = ssm_d.shape[-1]
    attn_w = (w_in.shape[-1] - ssm_w) // 3
    ne = router_w.shape[-1]
    n_tok = bsz * seq
    nc = seq // SSM_CHUNK
    tg = 512

    rope_tab, rope_exp = _rope_tables(positions)
    fmod3 = _adaln(c, final_ada_w, final_ada_b).reshape(bsz, 1, 2 * d)
    for l in range(depth):
        mod3 = _adaln(c, ada_w[l], ada_b[l]).reshape(bsz, 1, -1)
        u4, q, k, v = _inproj(x, norm_mix_g[l], mod3, w_in[l].astype(BF16), rope_tab, rope_exp, ssm_w, attn_w)
        tables = _s5_params(ssm_lambda_re[l], ssm_lambda_im[l], ssm_log_dt[l], ssm_b_re[l], ssm_b_im[l],
                            ssm_c_re[l], ssm_c_im[l], ssm_d[l], nc)
        y4 = _s5(u4, *tables)
        attn = _attention(q, k, v)
        x1, h2, idx_t, wt_t, tile_cnt = _merge(y4, attn, x, ssm_w_glu[l].astype(BF16), ssm_b_glu[l], out_norm_ssm_g[l],
                                     out_norm_attn_g[l], w_out[l].astype(BF16), mod3, norm_ffn_g[l],
                                     router_w[l], router_b[l])
        tb = seq // (tile_cnt.shape[0] // bsz)
        cap = n_tok * TOP_K + tile_cnt.shape[0] * ne * SEG_ALIGN + ne * tg
        assert cap % tg == 0 and max(tb, tg) // SEG_ALIGN < 2 * SEG_UNITS[0]
        plan, tail, items = _route_plan(tile_cnt, tg, cap)
        xs, ls_t = _dispatch(h2.reshape(n_tok, d), idx_t, plan, tail, cap, tb, tg)
        ys = _experts(xs.reshape(cap, d), items, exp_w_gate[l], exp_b_gate[l], exp_w_up[l], exp_b_up[l],
                      exp_w_down[l], exp_b_down[l], tg)
        if l + 1 < depth:
            raise NotImplementedError("depth > 1 needs the non-final combine")
        x = _combine(plan, ls_t, wt_t, x1, mod3, norm_final_g, fmod3, ys, tb)
    return x
```

```python
import functools
import math

import jax
import jax.numpy as jnp
from jax import lax
from jax.experimental import pallas as pl
from jax.experimental.pallas import tpu as pltpu

F32 = jnp.float32
BF16 = jnp.bfloat16
I32 = jnp.int32
HIGHEST = lax.Precision.HIGHEST

LANES = 128
HEAD_DIM = 64
MOBA_BLOCK = 256
MOBA_TOPK = 3
ROT_DIM = HEAD_DIM // 4
ROPE_THETA = 500000.0
SSM_GROUP = 16
SSM_CHUNK = 16
TOP_K = 4
SWIGLU_ALPHA = 1.702
SWIGLU_LIMIT = 7.0
NORM_EPS = 1e-5
NEG_INF = -1e30
LOG2_E = math.log2(math.e)
KV_TILE = 2 * MOBA_BLOCK
PV_ROWS = HEAD_DIM + 8
WT_ROWS = 8
SEG_ALIGN = 8
SEG_UNITS = (64, 32, 16, 8, 4, 2, 1)
SORT_CHUNK = 256
VMEM_LIMIT = 56 * 1024 * 1024


def _cparams(*sem):
    return pltpu.CompilerParams(dimension_semantics=sem, vmem_limit_bytes=VMEM_LIMIT)


def _dot(a, b):
    return jnp.dot(a, b, preferred_element_type=F32)


def _rms(x, g):
    return x * lax.rsqrt(jnp.mean(x * x, axis=-1, keepdims=True) + NORM_EPS) * g


def _adaln_body(c_ref, w_ref, b_ref, o_ref):
    c = c_ref[...]
    ca = c * jax.nn.sigmoid(c)
    o_ref[...] = jnp.dot(ca, w_ref[...], preferred_element_type=F32, precision=HIGHEST) + b_ref[...]


def _adaln(c, w, b, tn=512):
    bsz, d = c.shape
    n = w.shape[1]
    return pl.pallas_call(
        _adaln_body,
        grid=(n // tn,),
        in_specs=[pl.BlockSpec((bsz, d), lambda j: (0, 0)),
                  pl.BlockSpec((d, tn), lambda j: (0, j)),
                  pl.BlockSpec((1, tn), lambda j: (0, j))],
        out_specs=pl.BlockSpec((bsz, tn), lambda j: (0, j)),
        out_shape=jax.ShapeDtypeStruct((bsz, n), F32),
        compiler_params=_cparams("parallel"),
        name="adaln",
    )(c, w, b.reshape(1, n))


def _inproj_body(x_ref, g_ref, sh_ref, sc_ref, w_ref, tab_ref, exp_ref, u_ref, q_ref, k_ref, v_ref, us_ref):
    x = x_ref[0]
    tm = x.shape[0]
    h = _rms(x, g_ref[...]) * (1.0 + sc_ref[0]) + sh_ref[0]
    proj = _dot(h.astype(BF16), w_ref[...])
    ssm_w = u_ref.shape[0] * LANES
    aw = q_ref.shape[2]
    for j in range(u_ref.shape[0]):
        us_ref[j] = proj[:, j * LANES:(j + 1) * LANES]
        for t in range(SSM_CHUNK):
            u_ref[j, 0, :, t * LANES:(t + 1) * LANES] = us_ref[
                j, pl.ds(t, tm // SSM_CHUNK, stride=SSM_CHUNK), :].astype(BF16)
    t_hi = tab_ref[0].astype(BF16)
    t_lo = (tab_ref[0] - t_hi.astype(F32)).astype(BF16)
    tn = (((0,), (0,)), ((), ()))
    tab = (lax.dot_general(t_hi, exp_ref[...], tn, preferred_element_type=F32)
           + lax.dot_general(t_lo, exp_ref[...], tn, preferred_element_type=F32))
    lane1 = lax.broadcasted_iota(I32, (tm, LANES), 1)
    cos1 = tab[:, :LANES] + jnp.where((lane1 & (HEAD_DIM - 1)) >= ROT_DIM, 1.0, 0.0)
    reps = aw // LANES
    cosf = jnp.concatenate([cos1] * reps, axis=1)
    sinf = jnp.concatenate([tab[:, LANES:]] * reps, axis=1)
    lane = lax.broadcasted_iota(I32, (tm, aw), 1)
    first_half = (lane & (HEAD_DIM - 1)) < (ROT_DIM // 2)

    def rope(t):
        partner = jnp.where(first_half, pltpu.roll(t, aw - ROT_DIM // 2, 1), pltpu.roll(t, ROT_DIM // 2, 1))
        return t * cosf + partner * sinf

    q = rope(proj[:, ssm_w:ssm_w + aw]) * (HEAD_DIM ** -0.5 * LOG2_E)
    k = rope(proj[:, ssm_w + aw:ssm_w + 2 * aw])
    q_ref[0] = q.astype(BF16)
    k_ref[0] = k.astype(BF16)
    v_ref[0] = proj[:, ssm_w + 2 * aw:ssm_w + 3 * aw].astype(BF16)


def _inproj(x, gain, mod3, w_in_bf, rope_tab, rope_exp, ssm_w, attn_w, tm=1024):
    bsz, seq, d = x.shape
    n_u = ssm_w // LANES
    tm = min(tm, seq)
    return pl.pallas_call(
        _inproj_body,
        grid=(bsz, seq // tm),
        in_specs=[pl.BlockSpec((1, tm, d), lambda b, i: (b, i, 0)),
                  pl.BlockSpec((1, d), lambda b, i: (0, 0)),
                  pl.BlockSpec((1, 1, d), lambda b, i: (b, 0, 0)),
                  pl.BlockSpec((1, 1, d), lambda b, i: (b, 0, 1)),
                  pl.BlockSpec(w_in_bf.shape, lambda b, i: (0, 0)),
                  pl.BlockSpec((1, ROT_DIM, tm), lambda b, i: (b, 0, i)),
                  pl.BlockSpec(rope_exp.shape, lambda b, i: (0, 0))],
        out_specs=[pl.BlockSpec((n_u, 1, tm // SSM_CHUNK, SSM_CHUNK * LANES), lambda b, i: (0, b, i, 0)),
                   pl.BlockSpec((1, tm, attn_w), lambda b, i: (b, i, 0)),
                   pl.BlockSpec((1, tm, attn_w), lambda b, i: (b, i, 0)),
                   pl.BlockSpec((1, tm, attn_w), lambda b, i: (b, i, 0))],
        out_shape=[jax.ShapeDtypeStruct((n_u, bsz, seq // SSM_CHUNK, SSM_CHUNK * LANES), BF16),
                   jax.ShapeDtypeStruct((bsz, seq, attn_w), BF16),
                   jax.ShapeDtypeStruct((bsz, seq, attn_w), BF16),
                   jax.ShapeDtypeStruct((bsz, seq, attn_w), BF16)],
        scratch_shapes=[pltpu.VMEM((n_u, tm, LANES), F32)],
        compiler_params=_cparams("parallel", "parallel"),
        name="inproj",
    )(x, gain.reshape(1, d), mod3, mod3, w_in_bf, rope_tab, rope_exp)


def _rope_tables(positions):
    half = ROT_DIM // 2
    inv_freq = ROPE_THETA ** (-jnp.arange(0, ROT_DIM, 2, dtype=F32) / ROT_DIM)
    ang = positions.astype(F32)[:, None, :] * inv_freq[None, :, None]
    tab = jnp.concatenate([jnp.cos(ang), jnp.sin(ang)], axis=1)
    i = jnp.arange(ROT_DIM)[:, None]
    hl = jnp.arange(LANES)[None, :] % HEAD_DIM
    e_cos = jnp.where((i < half) & ((hl == i) | (hl == i + half)), 1.0, 0.0)
    e_sin = jnp.where(i >= half, jnp.where(hl == i - half, -1.0, jnp.where(hl == i, 1.0, 0.0)), 0.0)
    return tab, jnp.concatenate([e_cos, e_sin], axis=1).astype(BF16)


def _s5_params(lam_re, lam_im, log_dt, b_re, b_im, c_re, c_im, d_skip, n_chunks):
    g_all, p = lam_re.shape
    hc = b_re.shape[-1]
    t = SSM_CHUNK
    gpb = LANES // hc
    nblk = g_all // gpb
    lr, li = lam_re.astype(F32), lam_im.astype(F32)
    dt = jnp.exp(log_dt.astype(F32))[:, None]
    ldr, ldi = lr * dt, li * dt

    def apow(k):
        k = jnp.asarray(k, F32)[..., None, None]
        mag = jnp.exp(ldr * k)
        return mag * jnp.cos(ldi * k), mag * jnp.sin(ldi * k)

    ar, ai = apow(1.0)
    zr, zi = ar - 1.0, ai
    den = lr * lr + li * li
    cr = (zr * lr + zi * li) / den
    ci = (zi * lr - zr * li) / den
    bre, bim = b_re.astype(F32), b_im.astype(F32)
    bbr = cr[..., None] * bre - ci[..., None] * bim
    bbi = cr[..., None] * bim + ci[..., None] * bre
    cre, cim = c_re.astype(F32), c_im.astype(F32)
    def block_diag(a):
        w = a.shape[-1]
        wide = jnp.concatenate([a] * gpb, axis=-1)
        grp = jnp.arange(gpb)[:, None, None]
        lane_grp = (jnp.arange(gpb * w) // w)[None, None, :]
        wide = jnp.where(lane_grp == grp, wide, 0.0)
        return wide.reshape(a.shape[:-3] + (gpb * a.shape[-2], gpb * w))

    pr, pi = apow(jnp.arange(t))
    abr = pr[..., None] * bbr - pi[..., None] * bbi
    abi = pr[..., None] * bbi + pi[..., None] * bbr
    kf = (jnp.einsum('gnp,kgph->kghn', cre, abr, precision=HIGHEST)
          - jnp.einsum('gnp,kgph->kghn', cim, abi, precision=HIGHEST))
    kblk = block_diag(kf.reshape(t, nblk, gpb, hc, hc))
    kpad = jnp.concatenate([jnp.zeros_like(kblk[:1]), kblk], axis=0)
    pm = jnp.concatenate([kpad[0:t], kpad[1:t + 1]], axis=-1)
    kstack = jnp.transpose(pm[::-1], (1, 0, 2, 3)).reshape(nblk, t * LANES, 2 * LANES)

    def inject(ab):
        ab = jnp.transpose(ab[::-1].reshape(t, nblk, gpb, p, hc), (1, 0, 2, 4, 3))
        return block_diag(ab).reshape(nblk, t * LANES, gpb * p)

    bm = jnp.concatenate([inject(abr), inject(abi)], axis=-1)

    qr, qi = apow(jnp.arange(t) + 1.0)
    c_from_re = cre[None] * qr[:, :, None, :] - cim[None] * qi[:, :, None, :]
    c_from_im = -cre[None] * qi[:, :, None, :] - cim[None] * qr[:, :, None, :]

    def readout_t(cc):
        cc = jnp.transpose(cc.reshape(t, nblk, gpb, hc, p), (1, 0, 2, 3, 4))
        return block_diag(cc).reshape(nblk, t * LANES, gpb * p)

    cm = jnp.swapaxes(jnp.concatenate([readout_t(c_from_re), readout_t(c_from_im)], axis=-1), 1, 2)

    n_steps = max(1, int(math.log2(n_chunks)))
    shifts = [float(t * (1 << s)) for s in range(n_steps)]
    sr, si = apow(jnp.asarray(shifts))
    ap = jnp.concatenate([sr.reshape(n_steps, nblk, gpb * p), si.reshape(n_steps, nblk, gpb * p)], axis=-1)
    ap = jnp.transpose(ap, (1, 0, 2))
    dvec = jnp.tile(d_skip.astype(F32).reshape(nblk, 1, LANES), (1, 1, t))
    return kstack.astype(BF16), bm.astype(BF16), cm.astype(BF16), ap, dvec


def _s5_body(x_ref, ks_ref, bm_ref, cm_ref, ap_ref, d_ref, y_ref, h_ref):
    nc = x_ref.shape[2]
    t = SSM_CHUNK
    pad = h_ref.shape[0] - nc
    half = h_ref.shape[1] // 2
    x = x_ref[0, 0]
    h_ref[0:pad] = jnp.zeros((pad, h_ref.shape[1]), F32)
    h_ref[pad:pad + nc] = _dot(x, bm_ref[0])
    for step in range(ap_ref.shape[1]):
        d = 1 << step
        cur = h_ref[pad:pad + nc]
        sft = h_ref[pad - d:pad + nc - d]
        ar = ap_ref[0, step:step + 1, 0:half]
        ai = ap_ref[0, step:step + 1, half:]
        cr, ci = cur[:, :half], cur[:, half:]
        sr, si = sft[:, :half], sft[:, half:]
        h_ref[pad:pad + nc, 0:half] = cr + ar * sr - ai * si
        h_ref[pad:pad + nc, half:] = ci + ar * si + ai * sr
    h_prev = h_ref[pad - 1:pad + nc - 1]
    ys = _dot(h_prev.astype(BF16), cm_ref[0])
    for t0 in range(0, t, 2):
        lo, hi = t0 * LANES, (t0 + 2) * LANES
        conv = _dot(x[:, :hi], ks_ref[0, (t - 2 - t0) * LANES:, :])
        y = conv + ys[:, lo:hi] + d_ref[0][:, lo:hi] * x[:, lo:hi].astype(F32)
        y_ref[0, 0, :, lo:hi] = y.astype(y_ref.dtype)


def _s5(u4, kstack, bm, cm, ap, dvec):
    nblk, bsz, nc, w = u4.shape
    sw = bm.shape[-1]
    pad = max(nc // 2, 8)
    return pl.pallas_call(
        _s5_body,
        grid=(nblk, bsz),
        in_specs=[pl.BlockSpec((1, 1, nc, w), lambda j, b: (j, b, 0, 0)),
                  pl.BlockSpec((1,) + kstack.shape[1:], lambda j, b: (j, 0, 0)),
                  pl.BlockSpec((1,) + bm.shape[1:], lambda j, b: (j, 0, 0)),
                  pl.BlockSpec((1,) + cm.shape[1:], lambda j, b: (j, 0, 0)),
                  pl.BlockSpec((1,) + ap.shape[1:], lambda j, b: (j, 0, 0)),
                  pl.BlockSpec((1, 1, w), lambda j, b: (j, 0, 0))],
        out_specs=pl.BlockSpec((1, 1, nc, w), lambda j, b: (j, b, 0, 0)),
        out_shape=jax.ShapeDtypeStruct((nblk, bsz, nc, w), BF16),
        scratch_shapes=[pltpu.VMEM((pad + nc, sw), F32)],
        compiler_params=_cparams("parallel", "parallel"),
        name="s5",
    )(u4, kstack, bm, cm, ap, dvec)


def _attn_body(q_ref, k_ref, v_ref, o_ref, vt_ref, km_ref, bias_ref, sa_ref, sb_ref):
    qi = pl.program_id(2)
    seq = k_ref.shape[1]
    tq = q_ref.shape[1]
    nb = seq // MOBA_BLOCK
    n_kv = seq // KV_TILE
    bpt = KV_TILE // MOBA_BLOCK
    blk_shift = MOBA_BLOCK.bit_length() - 1

    @pl.when(qi == 0)
    def _build():
        k = k_ref[0].astype(F32)
        v = v_ref[0].astype(F32)
        ones = jnp.where(lax.broadcasted_iota(I32, (PV_ROWS - HEAD_DIM, KV_TILE), 0) == 0, 1.0, 0.0)
        for t in range(n_kv):
            v_t = v[t * KV_TILE:(t + 1) * KV_TILE].T
            vt_ref[0, t] = jnp.concatenate([v_t[:HEAD_DIM], ones], axis=0).astype(BF16)
            vt_ref[1, t] = jnp.concatenate([v_t[HEAD_DIM:], ones], axis=0).astype(BF16)
        km = jnp.mean(k.reshape(nb, MOBA_BLOCK, LANES), axis=1)
        lane_b = lax.broadcasted_iota(I32, (nb, LANES), 1)
        km_ref[0] = jnp.where(lane_b < HEAD_DIM, km, 0.0)
        km_ref[1] = jnp.where(lane_b >= HEAD_DIM, km, 0.0)

    q_t = q_ref[0].astype(F32).T
    feat = lax.broadcasted_iota(I32, (LANES, tq), 0)
    q_h = [jnp.where(feat < HEAD_DIM, q_t, 0.0).astype(BF16),
           jnp.where(feat >= HEAD_DIM, q_t, 0.0).astype(BF16)]

    jidx = lax.broadcasted_iota(I32, (nb, tq), 0)
    qblk = qi * (tq // MOBA_BLOCK) + lax.shift_right_logical(
        lax.broadcasted_iota(I32, (nb, tq), 1), blk_shift)
    for hx in range(2):
        g = _dot(km_ref[hx], q_t)
        cnt = jnp.zeros((nb, tq), F32)
        for jp in range(nb):
            row = g[jp:jp + 1, :]
            beats = jnp.where(row > g, 1.0, jnp.where(row == g, jnp.where(jp < jidx, 1.0, 0.0), 0.0))
            cnt = cnt + jnp.where(jp < qblk, beats, 0.0)
        past_sel = jnp.where(jidx < qblk, jnp.where(cnt < MOBA_TOPK, 1.0, 0.0), 0.0)
        sel = jnp.where(jidx == qblk, 1.0, past_sel)
        bias_ref[hx] = jnp.where(sel > 0.5, 0.0, NEG_INF)

    def raw_scores(dst_ref, tile):
        off = pl.multiple_of(tile * KV_TILE, KV_TILE)
        k_t = k_ref[0, pl.ds(off, KV_TILE), :]
        for hx in range(2):
            dst_ref[hx] = _dot(k_t, q_h[hx])

    def fold(carry, src_ref, tile, causal=None):
        out = []
        for hx in range(2):
            parts, biases = [], []
            for j in range(bpt):
                s_j = src_ref[hx, j * MOBA_BLOCK:(j + 1) * MOBA_BLOCK, :]
                if causal is not None:
                    s_j = jnp.where(causal[j], s_j, NEG_INF)
                b_j = bias_ref[hx, pl.ds(tile * bpt + j, 1), :]
                parts.append(s_j)
                biases.append(b_j)
            m_n = jnp.max(parts[0], axis=0, keepdims=True) + biases[0]
            for s_j, b_j in zip(parts[1:], biases[1:]):
                m_n = jnp.maximum(m_n, jnp.max(s_j, axis=0, keepdims=True) + b_j)
            if carry is None:
                p_t = jnp.concatenate([jnp.exp2(s_j + (b_j - m_n)) for s_j, b_j in zip(parts, biases)], axis=0)
                out += [m_n, _dot(vt_ref[hx, tile], p_t.astype(BF16))]
            else:
                m_c, acc_c = carry[2 * hx], carry[2 * hx + 1]
                m_n = jnp.maximum(m_c, m_n)
                p_t = jnp.concatenate([jnp.exp2(s_j + (b_j - m_n)) for s_j, b_j in zip(parts, biases)], axis=0)
                out += [m_n, acc_c * jnp.exp2(m_c - m_n) + _dot(vt_ref[hx, tile], p_t.astype(BF16))]
        return tuple(out)

    raw_scores(sa_ref, qi)
    raw_scores(sb_ref, 0)
    krow = lax.broadcasted_iota(I32, (MOBA_BLOCK, tq), 0)
    qcol = lax.broadcasted_iota(I32, (MOBA_BLOCK, tq), 1)
    state = fold(None, sa_ref, qi, causal=[krow + j * MOBA_BLOCK <= qcol for j in range(bpt)])

    def body(i, carry):
        t0 = 2 * i
        raw_scores(sa_ref, t0 + 1)
        carry = fold(carry, sb_ref, t0)
        raw_scores(sb_ref, jnp.minimum(t0 + 2, n_kv - 1))
        return fold(carry, sa_ref, t0 + 1)

    state = lax.fori_loop(0, qi // 2, body, state)
    _, acc_a, _, acc_b = lax.cond(lax.rem(qi, 2) == 1, lambda c: fold(c, sb_ref, qi - 1), lambda c: c, state)
    o_t = jnp.concatenate([acc[:HEAD_DIM] / acc[HEAD_DIM:HEAD_DIM + 1] for acc in (acc_a, acc_b)], axis=0)
    o_ref[0] = o_t.T.astype(o_ref.dtype)


def _attention(q, k, v):
    bsz, seq, aw = q.shape
    npair = aw // LANES
    tq = KV_TILE
    nb = seq // MOBA_BLOCK
    assert seq % KV_TILE == 0 and nb % 8 == 0
    return pl.pallas_call(
        _attn_body,
        grid=(bsz, npair, seq // tq),
        in_specs=[pl.BlockSpec((1, tq, LANES), lambda b, p, i: (b, i, p)),
                  pl.BlockSpec((1, seq, LANES), lambda b, p, i: (b, 0, p)),
                  pl.BlockSpec((1, seq, LANES), lambda b, p, i: (b, 0, p))],
        out_specs=pl.BlockSpec((1, tq, LANES), lambda b, p, i: (b, i, p)),
        out_shape=jax.ShapeDtypeStruct((bsz, seq, aw), BF16),
        scratch_shapes=[pltpu.VMEM((2, seq // KV_TILE, PV_ROWS, KV_TILE), BF16),
                        pltpu.VMEM((2, nb, LANES), F32),
                        pltpu.VMEM((2, nb, tq), F32),
                        pltpu.VMEM((2, KV_TILE, tq), F32),
                        pltpu.VMEM((2, KV_TILE, tq), F32)],
        compiler_params=_cparams("parallel", "parallel", "arbitrary"),
        name="attn",
    )(q, k, v)


def _merge_body(y_ref, a_ref, x_ref, wglu_ref, bglu_ref, gs_ref, ga_ref, wout_ref, gm_ref,
                gf_ref, shf_ref, scf_ref, rw_ref, rb_ref, x1_ref, h2_ref, idx_ref, wt_ref, cnt_ref, ys_ref):
    nblk = y_ref.shape[0]
    tm = x_ref.shape[1]
    for j in range(nblk):
        for t in range(SSM_CHUNK):
            ys_ref[j, pl.ds(t, tm // SSM_CHUNK, stride=SSM_CHUNK), :] = (
                y_ref[j, 0, :, t * LANES:(t + 1) * LANES].astype(F32))
    y = jnp.concatenate([ys_ref[j] for j in range(nblk)], axis=1)
    g = 0.5 * y * (1.0 + lax.erf(y * (2.0 ** -0.5)))
    glu = g * jax.nn.sigmoid(_dot(g.astype(BF16), wglu_ref[...]) + bglu_ref[...])
    ssm_n = _rms(glu, gs_ref[...])
    att_n = _rms(a_ref[0].astype(F32), ga_ref[...])
    merged = jnp.concatenate([ssm_n, att_n], axis=1).astype(BF16)
    mix = _dot(merged, wout_ref[...])
    x1 = x_ref[0] + gm_ref[0] * mix
    x1_ref[0] = x1
    h2 = _rms(x1, gf_ref[...]) * (1.0 + scf_ref[0]) + shf_ref[0]
    h2_ref[0] = h2
    logits = lax.dot_general(rw_ref[...], h2, (((1,), (1,)), ((), ())),
                             preferred_element_type=F32, precision=HIGHEST) + rb_ref[...]
    ne = logits.shape[0]
    sub = lax.broadcasted_iota(I32, (ne, tm), 0)
    vals, idxs = [], []
    for _ in range(TOP_K):
        mx = jnp.max(logits, axis=0, keepdims=True)
        ix = jnp.min(jnp.where(logits == mx, sub, ne), axis=0, keepdims=True)
        vals.append(mx)
        idxs.append(ix)
        logits = jnp.where(sub == ix, -jnp.inf, logits)
    exps = [jnp.exp(vv - vals[0]) for vv in vals]
    tot = exps[0]
    for e in exps[1:]:
        tot = tot + e
    row_i = lax.broadcasted_iota(I32, idx_ref.shape, 0)
    row_w = lax.broadcasted_iota(I32, wt_ref.shape, 0)
    idx_out = jnp.zeros(idx_ref.shape, I32)
    wt_out = jnp.zeros(wt_ref.shape, F32)
    for kk in range(TOP_K):
        idx_out = jnp.where(row_i == kk, idxs[kk], idx_out)
        wt_out = jnp.where(row_w == kk, exps[kk] / tot, wt_out)
    idx_ref[...] = idx_out
    wt_ref[...] = wt_out
    chosen = jnp.zeros((ne, tm), F32)
    for kk in range(TOP_K):
        chosen = chosen + jnp.where(sub == idxs[kk], 1.0, 0.0)
    cnt_ref[0] = jnp.sum(chosen, axis=1, keepdims=True).astype(I32)


def _merge(y4, attn, x, w_glu_bf, b_glu, g_ssm, g_attn, w_out_bf, mod3, g_ffn, router_w, router_b, tm=512):
    bsz, seq, d = x.shape
    nblk = y4.shape[0]
    ssm_w = nblk * LANES
    aw = attn.shape[-1]
    ne = router_w.shape[1]
    tm = min(tm, seq)
    spb = seq // tm
    n_tok = bsz * seq
    rw_t = router_w.T
    row = lambda n: pl.BlockSpec((1, n), lambda b, i: (0, 0))
    full = lambda a: pl.BlockSpec(a.shape, lambda b, i: (0,) * a.ndim)
    modv = lambda j: pl.BlockSpec((1, 1, d), lambda b, i: (b, 0, j))
    return pl.pallas_call(
        _merge_body,
        grid=(bsz, seq // tm),
        in_specs=[pl.BlockSpec((nblk, 1, tm // SSM_CHUNK, SSM_CHUNK * LANES), lambda b, i: (0, b, i, 0)),
                  pl.BlockSpec((1, tm, aw), lambda b, i: (b, i, 0)),
                  pl.BlockSpec((1, tm, d), lambda b, i: (b, i, 0)),
                  full(w_glu_bf), row(ssm_w), row(ssm_w), row(aw), full(w_out_bf),
                  modv(2), row(d), modv(3), modv(4), full(rw_t),
                  pl.BlockSpec((ne, 1), lambda b, i: (0, 0))],
        out_specs=[pl.BlockSpec((1, tm, d), lambda b, i: (b, i, 0)),
                   pl.BlockSpec((1, tm, d), lambda b, i: (b, i, 0)),
                   pl.BlockSpec((TOP_K, tm), lambda b, i: (0, b * spb + i)),
                   pl.BlockSpec((WT_ROWS, tm), lambda b, i: (0, b * spb + i)),
                   pl.BlockSpec((1, ne, 1), lambda b, i: (b * spb + i, 0, 0))],
        out_shape=[jax.ShapeDtypeStruct((bsz, seq, d), F32),
                   jax.ShapeDtypeStruct((bsz, seq, d), F32),
                   jax.ShapeDtypeStruct((TOP_K, n_tok), I32),
                   jax.ShapeDtypeStruct((WT_ROWS, n_tok), F32),
                   jax.ShapeDtypeStruct((bsz * spb, ne, 1), I32)],
        scratch_shapes=[pltpu.VMEM((nblk, tm, LANES), F32)],
        compiler_params=_cparams("parallel", "parallel"),
        name="merge",
    )(y4, attn, x, w_glu_bf, b_glu.reshape(1, ssm_w), g_ssm.reshape(1, ssm_w), g_attn.reshape(1, aw),
      w_out_bf, mod3, g_ffn.reshape(1, d), mod3, mod3, rw_t, router_b.reshape(ne, 1))


def _segment_copies(n_seg, meta, make_copy):
    def body(e, _):
        src0, dst0, units = meta(e)
        for sz in SEG_UNITS:
            off = units & ~(2 * sz - 1)

            @pl.when((units & sz) != 0)
            def _(off=off, sz=sz):
                make_copy(src0 + off, dst0 + off, sz).start()
        return 0

    lax.fori_loop(0, n_seg, body, 0)


def _local_slots(idx, lstart_col, ne):
    tb = idx.shape[1]
    sub = lax.broadcasted_iota(I32, (ne, tb), 0)
    hits = [idx[kk:kk + 1, :] == sub for kk in range(TOP_K)]
    onehot = jnp.zeros((ne, tb), F32)
    for h in hits:
        onehot = onehot + jnp.where(h, 1.0, 0.0)
    r = lax.broadcasted_iota(I32, (tb, tb), 0)
    c = lax.broadcasted_iota(I32, (tb, tb), 1)
    tri = jnp.where(r < c, 1.0, 0.0).astype(BF16)
    slot = _dot(onehot.astype(BF16), tri) + lstart_col
    return [jnp.sum(jnp.where(h, slot, 0.0), axis=0, keepdims=True) for h in hits]


def _dispatch_body(lstart_s, pcnt_s, base_s, tail_dst_s, tail_units_s, tail_total_s,
                   idx_ref, lcol_ref, h_ref, xs_ref, ls_ref, sorted_ref, zero_ref, sem, *, ne):
    i = pl.program_id(0)
    nt = pl.num_programs(0)
    tb = h_ref.shape[0]
    lb = sorted_ref.shape[1] * SEG_ALIGN
    slot = lax.rem(i, 2)

    @pl.when(i == 0)
    def _():
        zero_ref[...] = jnp.zeros(zero_ref.shape, F32)
        _segment_copies(
            ne, lambda e: (0, tail_dst_s[e] // SEG_ALIGN, tail_units_s[e]),
            lambda src, dst, n: pltpu.make_async_copy(zero_ref.at[pl.ds(src, n)], xs_ref.at[pl.ds(dst, n)],
                                                      sem.at[2]))

    def tile_units(t):
        return (lstart_s[t * ne + ne - 1] + pcnt_s[t * ne + ne - 1]) // SEG_ALIGN

    def wait_tile(t, sl):
        done = xs_ref.at[pl.ds(0, tile_units(t))]
        pltpu.make_async_copy(done, done, sem.at[sl]).wait()

    @pl.when(i >= 2)
    def _():
        wait_tile(i - 2, slot)

    slots = _local_slots(idx_ref[...], lcol_ref[0].astype(F32), ne)
    row_o = lax.broadcasted_iota(I32, ls_ref.shape, 0)
    ls_out = jnp.full(ls_ref.shape, -1.0, F32)
    for kk in range(TOP_K):
        ls_out = jnp.where(row_o == kk, slots[kk], ls_out)
    ls_ref[...] = ls_out.astype(I32)
    h = h_ref[...].astype(BF16)
    for r0 in range(0, lb, SORT_CHUNK):
        j = (lax.broadcasted_iota(I32, (SORT_CHUNK, tb), 0) + r0).astype(F32)
        pm = jnp.zeros((SORT_CHUNK, tb), F32)
        for kk in range(TOP_K):
            pm = jnp.where(j == slots[kk], 1.0, pm)
        sorted_ref[slot, r0 // SEG_ALIGN:(r0 + SORT_CHUNK) // SEG_ALIGN] = _dot(pm.astype(BF16), h).reshape(
            SORT_CHUNK // SEG_ALIGN, SEG_ALIGN, h.shape[1])

    def meta(e):
        return (lstart_s[i * ne + e] // SEG_ALIGN, base_s[i * ne + e] // SEG_ALIGN,
                pcnt_s[i * ne + e] // SEG_ALIGN)

    def make_copy(src, dst, n):
        return pltpu.make_async_copy(sorted_ref.at[slot, pl.ds(src, n)], xs_ref.at[pl.ds(dst, n)], sem.at[slot])

    _segment_copies(ne, meta, make_copy)

    @pl.when(i == nt - 1)
    def _():
        @pl.when(i >= 1)
        def _():
            wait_tile(i - 1, 1 - slot)
        wait_tile(i, slot)

        @pl.when(tail_total_s[0] > 0)
        def _():
            filled = xs_ref.at[pl.ds(0, tail_total_s[0])]
            pltpu.make_async_copy(filled, filled, sem.at[2]).wait()


def _dispatch(h2, idx_t, plan, tail, cap, tb, tg):
    n, d = h2.shape
    lstart, pcnt, base = plan
    nt, ne = lstart.shape
    lb = tb * TOP_K + ne * SEG_ALIGN
    assert lb % SORT_CHUNK == 0
    return pl.pallas_call(
        functools.partial(_dispatch_body, ne=ne),
        grid_spec=pltpu.PrefetchScalarGridSpec(
            num_scalar_prefetch=6,
            grid=(nt,),
            in_specs=[pl.BlockSpec((TOP_K, tb), lambda i, *_: (0, i)),
                      pl.BlockSpec((1, ne, 1), lambda i, *_: (i, 0, 0)),
                      pl.BlockSpec((tb, d), lambda i, *_: (i, 0))],
            out_specs=[pl.BlockSpec(memory_space=pl.ANY),
                       pl.BlockSpec((WT_ROWS, tb), lambda i, *_: (0, i))],
            scratch_shapes=[pltpu.VMEM((2, lb // SEG_ALIGN, SEG_ALIGN, d), F32),
                            pltpu.VMEM((tg // SEG_ALIGN, SEG_ALIGN, d), F32),
                            pltpu.SemaphoreType.DMA((3,))],
        ),
        out_shape=[jax.ShapeDtypeStruct((cap // SEG_ALIGN, SEG_ALIGN, d), F32),
                   jax.ShapeDtypeStruct((WT_ROWS, n), I32)],
        compiler_params=_cparams("arbitrary"),
        name="dispatch",
    )(lstart.reshape(-1), pcnt.reshape(-1), base.reshape(-1), *tail, idx_t, lstart.reshape(nt, ne, 1), h2)


def _route_plan(tile_cnt, tg, cap):
    cnt = tile_cnt[:, :, 0]
    ne = cnt.shape[1]
    pcnt = (cnt + SEG_ALIGN - 1) // SEG_ALIGN * SEG_ALIGN
    lstart = jnp.cumsum(pcnt, axis=1) - pcnt
    e_rows = jnp.sum(pcnt, axis=0)
    e_tiles = (e_rows + tg - 1) // tg
    e_end = jnp.cumsum(e_tiles)
    base = ((e_end - e_tiles) * tg)[None, :] + jnp.cumsum(pcnt, axis=0) - pcnt
    ii = jnp.arange(cap // tg, dtype=I32)
    used = ii < e_end[-1]
    tile = jnp.minimum(ii, e_end[-1] - 1)
    e_of = jnp.minimum(jnp.sum((e_end[None, :] <= tile[:, None]).astype(I32), axis=1), ne - 1)
    newe = (e_of != jnp.concatenate([jnp.full((1,), -1, I32), e_of[:-1]])).astype(I32)
    items = (tile.astype(I32), e_of.astype(I32), used.astype(I32), newe)
    tail = (e_end * tg - e_tiles * tg + e_rows, (e_tiles * tg - e_rows) // SEG_ALIGN)
    tail = (tail[0].astype(I32), tail[1].astype(I32), jnp.sum(tail[1]).astype(I32).reshape(1))
    return (lstart.astype(I32), pcnt.astype(I32), base.astype(I32)), tail, items


def _experts_body(tile_ref, exp_ref, used_ref, newe_ref,
                  xs_ref, wg_ref, bg_ref, wu_ref, bu_ref, wd_ref, bd_ref, ys_ref,
                  wg_bf, wu_bf, wd_bf):
    i = pl.program_id(0)

    @pl.when(newe_ref[i] == 1)
    def _cast():
        wg_bf[...] = wg_ref[0].astype(BF16)
        wu_bf[...] = wu_ref[0].astype(BF16)
        wd_bf[...] = wd_ref[0].astype(BF16)

    @pl.when(used_ref[i] == 1)
    def _compute():
        x = xs_ref[...].astype(BF16)
        gate = jnp.minimum(_dot(x, wg_bf[...]) + bg_ref[0], SWIGLU_LIMIT)
        lin = jnp.clip(_dot(x, wu_bf[...]) + bu_ref[0], -SWIGLU_LIMIT, SWIGLU_LIMIT)
        act = gate * jax.nn.sigmoid(SWIGLU_ALPHA * gate) * (lin + 1.0)
        ys_ref[...] = _dot(act.astype(BF16), wd_bf[...]) + bd_ref[0]


def _experts(xs, items, w_gate, b_gate, w_up, b_up, w_down, b_down, tg):
    nk, d = xs.shape
    ne, _, dff = w_gate.shape
    n_items = items[0].shape[0]
    wspec = lambda shp: pl.BlockSpec((1,) + shp, lambda i, t, e, u, nw: (e[i], 0, 0))
    return pl.pallas_call(
        _experts_body,
        grid_spec=pltpu.PrefetchScalarGridSpec(
            num_scalar_prefetch=4,
            grid=(n_items,),
            in_specs=[pl.BlockSpec((tg, d), lambda i, t, e, u, nw: (t[i], 0)),
                      wspec((d, dff)), wspec((1, dff)),
                      wspec((d, dff)), wspec((1, dff)),
                      wspec((dff, d)), wspec((1, d))],
            out_specs=pl.BlockSpec((tg, d), lambda i, t, e, u, nw: (t[i], 0)),
            scratch_shapes=[pltpu.VMEM((d, dff), BF16), pltpu.VMEM((d, dff), BF16), pltpu.VMEM((dff, d), BF16)],
        ),
        out_shape=jax.ShapeDtypeStruct((nk, d), F32),
        compiler_params=_cparams("arbitrary"),
        name="experts",
    )(*items, xs, w_gate, b_gate.reshape(ne, 1, dff), w_up, b_up.reshape(ne, 1, dff),
      w_down, b_down.reshape(ne, 1, d))


def _combine_body(lstart_s, pcnt_s, base_s, ls_ref, wt_ref, x1_ref, gf_ref, gfin_ref, sho_ref, sco_ref,
                  ys_ref, o_ref, buf, sem, *, ne, spb):
    i = pl.program_id(0) * spb + pl.program_id(1)
    nt = pl.num_programs(0) * spb
    tb = x1_ref.shape[1]
    lb = buf.shape[1] * SEG_ALIGN
    slot = lax.rem(i, 2)

    def fetch(t, sl):
        def meta(e):
            return (base_s[t * ne + e] // SEG_ALIGN, lstart_s[t * ne + e] // SEG_ALIGN,
                    pcnt_s[t * ne + e] // SEG_ALIGN)

        def make_copy(src, dst, n):
            return pltpu.make_async_copy(ys_ref.at[pl.ds(src, n)], buf.at[sl, pl.ds(dst, n)], sem.at[sl])

        _segment_copies(ne, meta, make_copy)

    @pl.when(i == 0)
    def _():
        buf[...] = jnp.zeros(buf.shape, F32)
        fetch(0, 0)

    @pl.when(i + 1 < nt)
    def _():
        fetch(i + 1, 1 - slot)

    total = (lstart_s[i * ne + ne - 1] + pcnt_s[i * ne + ne - 1]) // SEG_ALIGN
    done = buf.at[slot, pl.ds(0, total)]
    pltpu.make_async_copy(done, done, sem.at[slot]).wait()
    ls_col = ls_ref[...].astype(F32).T
    wt_col = wt_ref[...].T
    ffn = jnp.zeros((tb, x1_ref.shape[2]), F32)
    for r0 in range(0, lb, SORT_CHUNK):
        j = (lax.broadcasted_iota(I32, (tb, SORT_CHUNK), 1) + r0).astype(F32)
        wm = jnp.zeros((tb, SORT_CHUNK), F32)
        for kk in range(TOP_K):
            wm = jnp.where(j == ls_col[:, kk:kk + 1], wt_col[:, kk:kk + 1], wm)
        y_rows = buf[slot, r0 // SEG_ALIGN:(r0 + SORT_CHUNK) // SEG_ALIGN].reshape(SORT_CHUNK, x1_ref.shape[2])
        ffn = ffn + _dot(wm.astype(BF16), y_rows.astype(BF16))
    x2 = x1_ref[0] + gf_ref[0] * ffn
    o_ref[0] = _rms(x2, gfin_ref[...]) * (1.0 + sco_ref[0]) + sho_ref[0]


def _combine(plan, ls_t, wt_t, x1, mod3, g_final, fmod3, ys, tb):
    bsz, seq, d = x1.shape
    lstart, pcnt, base = plan
    nt, ne = lstart.shape
    spb = seq // tb
    lb = tb * TOP_K + ne * SEG_ALIGN
    return pl.pallas_call(
        functools.partial(_combine_body, ne=ne, spb=spb),
        grid_spec=pltpu.PrefetchScalarGridSpec(
            num_scalar_prefetch=3,
            grid=(bsz, spb),
            in_specs=[pl.BlockSpec((WT_ROWS, tb), lambda b, i, *_: (0, b * spb + i)),
                      pl.BlockSpec((WT_ROWS, tb), lambda b, i, *_: (0, b * spb + i)),
                      pl.BlockSpec((1, tb, d), lambda b, i, *_: (b, i, 0)),
                      pl.BlockSpec((1, 1, d), lambda b, i, *_: (b, 0, 5)),
                      pl.BlockSpec((1, d), lambda b, i, *_: (0, 0)),
                      pl.BlockSpec((1, 1, d), lambda b, i, *_: (b, 0, 0)),
                      pl.BlockSpec((1, 1, d), lambda b, i, *_: (b, 0, 1)),
                      pl.BlockSpec(memory_space=pl.ANY)],
            out_specs=pl.BlockSpec((1, tb, d), lambda b, i, *_: (b, i, 0)),
            scratch_shapes=[pltpu.VMEM((2, lb // SEG_ALIGN, SEG_ALIGN, d), F32), pltpu.SemaphoreType.DMA((2,))],
        ),
        out_shape=jax.ShapeDtypeStruct((bsz, seq, d), F32),
        compiler_params=_cparams("arbitrary", "arbitrary"),
        name="combine",
    )(lstart.reshape(-1), pcnt.reshape(-1), base.reshape(-1), ls_t, wt_t, x1, mod3, g_final.reshape(1, d),
      fmod3, fmod3, ys.reshape(-1, SEG_ALIGN, d))


def kernel(x, c, positions, ada_w, ada_b, final_ada_w, final_ada_b, norm_mix_g, norm_ffn_g, norm_final_g, w_in, ssm_lambda_re, ssm_lambda_im, ssm_log_dt, ssm_b_re, ssm_b_im, ssm_c_re, ssm_c_im, ssm_d, ssm_w_glu, ssm_b_glu, out_norm_ssm_g, out_norm_attn_g, w_out, router_w, router_b, exp_w_gate, exp_b_gate, exp_w_up, exp_b_up, exp_w_down, exp_b_down):
    bsz, seq, d = x.shape
    depth = ada_w.shape[0]
    ssm_w = ssm_d.shape[-1]
    attn_w = (w_in.shape[-1] - ssm_w) // 3
    ne = router_w.shape[-1]
    n_tok = bsz * seq
    nc = seq // SSM_CHUNK
    tg = 512

    rope_tab, rope_exp = _rope_tables(positions)
    fmod3 = _adaln(c, final_ada_w, final_ada_b).reshape(bsz, 1, 2 * d)
    for l in range(depth):
        mod3 = _adaln(c, ada_w[l], ada_b[l]).reshape(bsz, 1, -1)
        u4, q, k, v = _inproj(x, norm_mix_g[l], mod3, w_in[l].astype(BF16), rope_tab, rope_exp, ssm_w, attn_w)
        tables = _s5_params(ssm_lambda_re[l], ssm_lambda_im[l], ssm_log_dt[l], ssm_b_re[l], ssm_b_im[l],
                            ssm_c_re[l], ssm_c_im[l], ssm_d[l], nc)
        y4 = _s5(u4, *tables)
        attn = _attention(q, k, v)
        x1, h2, idx_t, wt_t, tile_cnt = _merge(y4, attn, x, ssm_w_glu[l].astype(BF16), ssm_b_glu[l], out_norm_ssm_g[l],
                                     out_norm_attn_g[l], w_out[l].astype(BF16), mod3, norm_ffn_g[l],
                                     router_w[l], router_b[l])
        tb = seq // (tile_cnt.shape[0] // bsz)
        cap = n_tok * TOP_K + tile_cnt.shape[0] * ne * SEG_ALIGN + ne * tg
        assert cap % tg == 0 and max(tb, tg) // SEG_ALIGN < 2 * SEG_UNITS[0]
        plan, tail, items = _route_plan(tile_cnt, tg, cap)
        xs, ls_t = _dispatch(h2.reshape(n_tok, d), idx_t, plan, tail, cap, tb, tg)
        ys = _experts(xs.reshape(cap, d), items, exp_w_gate[l], exp_b_gate[l], exp_w_up[l], exp_b_up[l],
                      exp_w_down[l], exp_b_down[l], tg)
        if l + 1 < depth:
            raise NotImplementedError("depth > 1 needs the non-final combine")
        x = _combine(plan, ls_t, wt_t, x1, mod3, norm_final_g, fmod3, ys, tb)
    return x
```

```python
import functools
import math

import jax
import jax.numpy as jnp
from jax import lax
from jax.experimental import pallas as pl
from jax.experimental.pallas import tpu as pltpu

F32 = jnp.float32
BF16 = jnp.bfloat16
I32 = jnp.int32
HIGHEST = lax.Precision.HIGHEST

LANES = 128
HEAD_DIM = 64
MOBA_BLOCK = 256
MOBA_TOPK = 3
ROT_DIM = HEAD_DIM // 4
ROPE_THETA = 500000.0
SSM_CHUNK = 16
TOP_K = 4
SWIGLU_ALPHA = 1.702
SWIGLU_LIMIT = 7.0
NORM_EPS = 1e-5
NEG_INF = -1e30
LOG2_E = math.log2(math.e)
KV_TILE = 2 * MOBA_BLOCK
PV_ROWS = HEAD_DIM + 8
ATTN_PAIRS_PER_STEP = 2
WT_ROWS = 8
SEG_ALIGN = 8
SEG_UNITS = (64, 32, 16, 8, 4, 2, 1)
SORT_CHUNK = 256
VMEM_LIMIT = 56 * 1024 * 1024


def _cparams(*sem):
    return pltpu.CompilerParams(dimension_semantics=sem, vmem_limit_bytes=VMEM_LIMIT)


def _dot(a, b):
    return jnp.dot(a, b, preferred_element_type=F32)


def _rms(x, g):
    return x * lax.rsqrt(jnp.mean(x * x, axis=-1, keepdims=True) + NORM_EPS) * g


def _adaln_body(c_ref, w_ref, b_ref, o_ref):
    c = c_ref[...]
    ca = c * jax.nn.sigmoid(c)
    o_ref[...] = jnp.dot(ca, w_ref[...], preferred_element_type=F32, precision=HIGHEST) + b_ref[...]


def _adaln(c, w, b, tn=512):
    bsz, d = c.shape
    n = w.shape[1]
    return pl.pallas_call(
        _adaln_body,
        grid=(n // tn,),
        in_specs=[pl.BlockSpec((bsz, d), lambda j: (0, 0)),
                  pl.BlockSpec((d, tn), lambda j: (0, j)),
                  pl.BlockSpec((1, tn), lambda j: (0, j))],
        out_specs=pl.BlockSpec((bsz, tn), lambda j: (0, j)),
        out_shape=jax.ShapeDtypeStruct((bsz, n), F32),
        compiler_params=_cparams("parallel"),
        name="adaln",
    )(c, w, b.reshape(1, n))


def _inproj_body(x_ref, g_ref, sh_ref, sc_ref, w_ref, tab_ref, exp_ref, u_ref, q_ref, k_ref, v_ref, us_ref):
    x = x_ref[0]
    tm = x.shape[0]
    h = _rms(x, g_ref[...]) * (1.0 + sc_ref[0]) + sh_ref[0]
    proj = _dot(h.astype(BF16), w_ref[...])
    ssm_w = u_ref.shape[0] * LANES
    aw = q_ref.shape[2]
    for j in range(u_ref.shape[0]):
        us_ref[j] = proj[:, j * LANES:(j + 1) * LANES]
        for t in range(SSM_CHUNK):
            u_ref[j, 0, :, t * LANES:(t + 1) * LANES] = us_ref[
                j, pl.ds(t, tm // SSM_CHUNK, stride=SSM_CHUNK), :].astype(BF16)
    t_hi = tab_ref[0].astype(BF16)
    t_lo = (tab_ref[0] - t_hi.astype(F32)).astype(BF16)
    tn = (((0,), (0,)), ((), ()))
    tab = (lax.dot_general(t_hi, exp_ref[...], tn, preferred_element_type=F32)
           + lax.dot_general(t_lo, exp_ref[...], tn, preferred_element_type=F32))
    lane1 = lax.broadcasted_iota(I32, (tm, LANES), 1)
    cos1 = tab[:, :LANES] + jnp.where((lane1 & (HEAD_DIM - 1)) >= ROT_DIM, 1.0, 0.0)
    reps = aw // LANES
    cosf = jnp.concatenate([cos1] * reps, axis=1)
    sinf = jnp.concatenate([tab[:, LANES:]] * reps, axis=1)
    lane = lax.broadcasted_iota(I32, (tm, aw), 1)
    first_half = (lane & (HEAD_DIM - 1)) < (ROT_DIM // 2)

    def rope(t):
        partner = jnp.where(first_half, pltpu.roll(t, aw - ROT_DIM // 2, 1), pltpu.roll(t, ROT_DIM // 2, 1))
        return t * cosf + partner * sinf

    q = rope(proj[:, ssm_w:ssm_w + aw]) * (HEAD_DIM ** -0.5 * LOG2_E)
    k = rope(proj[:, ssm_w + aw:ssm_w + 2 * aw])
    q_ref[0] = q.astype(BF16)
    k_ref[0] = k.astype(BF16)
    v_ref[0] = proj[:, ssm_w + 2 * aw:ssm_w + 3 * aw].astype(BF16)


def _inproj(x, gain, mod3, w_in_bf, rope_tab, rope_exp, ssm_w, attn_w, tm=1024):
    bsz, seq, d = x.shape
    n_u = ssm_w // LANES
    tm = min(tm, seq)
    return pl.pallas_call(
        _inproj_body,
        grid=(bsz, seq // tm),
        in_specs=[pl.BlockSpec((1, tm, d), lambda b, i: (b, i, 0)),
                  pl.BlockSpec((1, d), lambda b, i: (0, 0)),
                  pl.BlockSpec((1, 1, d), lambda b, i: (b, 0, 0)),
                  pl.BlockSpec((1, 1, d), lambda b, i: (b, 0, 1)),
                  pl.BlockSpec(w_in_bf.shape, lambda b, i: (0, 0)),
                  pl.BlockSpec((1, ROT_DIM, tm), lambda b, i: (b, 0, i)),
                  pl.BlockSpec(rope_exp.shape, lambda b, i: (0, 0))],
        out_specs=[pl.BlockSpec((n_u, 1, tm // SSM_CHUNK, SSM_CHUNK * LANES), lambda b, i: (0, b, i, 0)),
                   pl.BlockSpec((1, tm, attn_w), lambda b, i: (b, i, 0)),
                   pl.BlockSpec((1, tm, attn_w), lambda b, i: (b, i, 0)),
                   pl.BlockSpec((1, tm, attn_w), lambda b, i: (b, i, 0))],
        out_shape=[jax.ShapeDtypeStruct((n_u, bsz, seq // SSM_CHUNK, SSM_CHUNK * LANES), BF16),
                   jax.ShapeDtypeStruct((bsz, seq, attn_w), BF16),
                   jax.ShapeDtypeStruct((bsz, seq, attn_w), BF16),
                   jax.ShapeDtypeStruct((bsz, seq, attn_w), BF16)],
        scratch_shapes=[pltpu.VMEM((n_u, tm, LANES), F32)],
        compiler_params=_cparams("parallel", "parallel"),
        name="inproj",
    )(x, gain.reshape(1, d), mod3, mod3, w_in_bf, rope_tab, rope_exp)


def _rope_tables(positions):
    half = ROT_DIM // 2
    inv_freq = ROPE_THETA ** (-jnp.arange(0, ROT_DIM, 2, dtype=F32) / ROT_DIM)
    ang = positions.astype(F32)[:, None, :] * inv_freq[None, :, None]
    tab = jnp.concatenate([jnp.cos(ang), jnp.sin(ang)], axis=1)
    i = jnp.arange(ROT_DIM)[:, None]
    hl = jnp.arange(LANES)[None, :] % HEAD_DIM
    e_cos = jnp.where((i < half) & ((hl == i) | (hl == i + half)), 1.0, 0.0)
    e_sin = jnp.where(i >= half, jnp.where(hl == i - half, -1.0, jnp.where(hl == i, 1.0, 0.0)), 0.0)
    return tab, jnp.concatenate([e_cos, e_sin], axis=1).astype(BF16)


def _s5_params(lam_re, lam_im, log_dt, b_re, b_im, c_re, c_im, d_skip, n_chunks):
    g_all, p = lam_re.shape
    hc = b_re.shape[-1]
    t = SSM_CHUNK
    gpb = LANES // hc
    nblk = g_all // gpb
    lr, li = lam_re.astype(F32), lam_im.astype(F32)
    dt = jnp.exp(log_dt.astype(F32))[:, None]
    ldr, ldi = lr * dt, li * dt

    def apow(k):
        k = jnp.asarray(k, F32)[..., None, None]
        mag = jnp.exp(ldr * k)
        return mag * jnp.cos(ldi * k), mag * jnp.sin(ldi * k)

    ar, ai = apow(1.0)
    zr, zi = ar - 1.0, ai
    den = lr * lr + li * li
    cr = (zr * lr + zi * li) / den
    ci = (zi * lr - zr * li) / den
    bre, bim = b_re.astype(F32), b_im.astype(F32)
    bbr = cr[..., None] * bre - ci[..., None] * bim
    bbi = cr[..., None] * bim + ci[..., None] * bre
    cre, cim = c_re.astype(F32), c_im.astype(F32)
    def block_diag(a):
        w = a.shape[-1]
        wide = jnp.concatenate([a] * gpb, axis=-1)
        grp = jnp.arange(gpb)[:, None, None]
        lane_grp = (jnp.arange(gpb * w) // w)[None, None, :]
        wide = jnp.where(lane_grp == grp, wide, 0.0)
        return wide.reshape(a.shape[:-3] + (gpb * a.shape[-2], gpb * w))

    pr, pi = apow(jnp.arange(t))
    abr = pr[..., None] * bbr - pi[..., None] * bbi
    abi = pr[..., None] * bbi + pi[..., None] * bbr
    kf = (jnp.einsum('gnp,kgph->kghn', cre, abr, precision=HIGHEST)
          - jnp.einsum('gnp,kgph->kghn', cim, abi, precision=HIGHEST))
    kblk = block_diag(kf.reshape(t, nblk, gpb, hc, hc))
    kpad = jnp.concatenate([jnp.zeros_like(kblk[:1]), kblk], axis=0)
    pm = jnp.concatenate([kpad[0:t], kpad[1:t + 1]], axis=-1)
    kstack = jnp.transpose(pm[::-1], (1, 0, 2, 3)).reshape(nblk, t * LANES, 2 * LANES)

    def inject(ab):
        ab = jnp.transpose(ab[::-1].reshape(t, nblk, gpb, p, hc), (1, 0, 2, 4, 3))
        return block_diag(ab).reshape(nblk, t * LANES, gpb * p)

    bm = jnp.concatenate([inject(abr), inject(abi)], axis=-1)

    qr, qi = apow(jnp.arange(t) + 1.0)
    c_from_re = cre[None] * qr[:, :, None, :] - cim[None] * qi[:, :, None, :]
    c_from_im = -cre[None] * qi[:, :, None, :] - cim[None] * qr[:, :, None, :]

    def readout_t(cc):
        cc = jnp.transpose(cc.reshape(t, nblk, gpb, hc, p), (1, 0, 2, 3, 4))
        return block_diag(cc).reshape(nblk, t * LANES, gpb * p)

    cm = jnp.swapaxes(jnp.concatenate([readout_t(c_from_re), readout_t(c_from_im)], axis=-1), 1, 2)

    n_steps = max(1, int(math.log2(n_chunks)))
    shifts = [float(t * (1 << s)) for s in range(n_steps)]
    sr, si = apow(jnp.asarray(shifts))
    ap = jnp.concatenate([sr.reshape(n_steps, nblk, gpb * p), si.reshape(n_steps, nblk, gpb * p)], axis=-1)
    ap = jnp.transpose(ap, (1, 0, 2))
    dvec = jnp.tile(d_skip.astype(F32).reshape(nblk, 1, LANES), (1, 1, t))
    return kstack.astype(BF16), bm.astype(BF16), cm.astype(BF16), ap, dvec


def _s5_body(x_ref, ks_ref, bm_ref, cm_ref, ap_ref, d_ref, y_ref, h_ref):
    nc = x_ref.shape[2]
    t = SSM_CHUNK
    pad = h_ref.shape[0] - nc
    half = h_ref.shape[1] // 2
    x = x_ref[0, 0]
    h_ref[0:pad] = jnp.zeros((pad, h_ref.shape[1]), F32)
    h_ref[pad:pad + nc] = _dot(x, bm_ref[0])
    for step in range(ap_ref.shape[1]):
        d = 1 << step
        cur = h_ref[pad:pad + nc]
        sft = h_ref[pad - d:pad + nc - d]
        ar = ap_ref[0, step:step + 1, 0:half]
        ai = ap_ref[0, step:step + 1, half:]
        cr, ci = cur[:, :half], cur[:, half:]
        sr, si = sft[:, :half], sft[:, half:]
        h_ref[pad:pad + nc, 0:half] = cr + ar * sr - ai * si
        h_ref[pad:pad + nc, half:] = ci + ar * si + ai * sr
    h_prev = h_ref[pad - 1:pad + nc - 1]
    ys = _dot(h_prev.astype(BF16), cm_ref[0])
    for t0 in range(0, t, 2):
        lo, hi = t0 * LANES, (t0 + 2) * LANES
        conv = _dot(x[:, :hi], ks_ref[0, (t - 2 - t0) * LANES:, :])
        y = conv + ys[:, lo:hi] + d_ref[0][:, lo:hi] * x[:, lo:hi].astype(F32)
        y_ref[0, 0, :, lo:hi] = y.astype(y_ref.dtype)


def _s5(u4, kstack, bm, cm, ap, dvec):
    nblk, bsz, nc, w = u4.shape
    sw = bm.shape[-1]
    pad = max(nc // 2, 8)
    return pl.pallas_call(
        _s5_body,
        grid=(nblk, bsz),
        in_specs=[pl.BlockSpec((1, 1, nc, w), lambda j, b: (j, b, 0, 0)),
                  pl.BlockSpec((1,) + kstack.shape[1:], lambda j, b: (j, 0, 0)),
                  pl.BlockSpec((1,) + bm.shape[1:], lambda j, b: (j, 0, 0)),
                  pl.BlockSpec((1,) + cm.shape[1:], lambda j, b: (j, 0, 0)),
                  pl.BlockSpec((1,) + ap.shape[1:], lambda j, b: (j, 0, 0)),
                  pl.BlockSpec((1, 1, w), lambda j, b: (j, 0, 0))],
        out_specs=pl.BlockSpec((1, 1, nc, w), lambda j, b: (j, b, 0, 0)),
        out_shape=jax.ShapeDtypeStruct((nblk, bsz, nc, w), BF16),
        scratch_shapes=[pltpu.VMEM((pad + nc, sw), F32)],
        compiler_params=_cparams("parallel", "parallel"),
        name="s5",
    )(u4, kstack, bm, cm, ap, dvec)


def _attn_body(q_ref, k_ref, v_ref, o_ref, vt_ref, km_ref, bias_ref, sa_ref, sb_ref):
    qi = pl.program_id(2)
    seq = k_ref.shape[1]
    tq = q_ref.shape[1]
    npair = q_ref.shape[2] // LANES
    nh = 2 * npair
    nb = seq // MOBA_BLOCK
    n_kv = seq // KV_TILE
    bpt = KV_TILE // MOBA_BLOCK
    blk_shift = MOBA_BLOCK.bit_length() - 1

    @pl.when(qi == 0)
    def _build():
        ones = jnp.where(lax.broadcasted_iota(I32, (PV_ROWS - HEAD_DIM, KV_TILE), 0) == 0, 1.0, 0.0)
        lane_b = lax.broadcasted_iota(I32, (nb, LANES), 1)
        for pr in range(npair):
            k = k_ref[0, :, pr * LANES:(pr + 1) * LANES].astype(F32)
            v = v_ref[0, :, pr * LANES:(pr + 1) * LANES].astype(F32)
            for t in range(n_kv):
                v_t = v[t * KV_TILE:(t + 1) * KV_TILE].T
                vt_ref[2 * pr, t] = jnp.concatenate([v_t[:HEAD_DIM], ones], axis=0).astype(BF16)
                vt_ref[2 * pr + 1, t] = jnp.concatenate([v_t[HEAD_DIM:], ones], axis=0).astype(BF16)
            km = jnp.mean(k.reshape(nb, MOBA_BLOCK, LANES), axis=1)
            km_ref[2 * pr] = jnp.where(lane_b < HEAD_DIM, km, 0.0)
            km_ref[2 * pr + 1] = jnp.where(lane_b >= HEAD_DIM, km, 0.0)

    feat = lax.broadcasted_iota(I32, (LANES, tq), 0)
    q_ts, q_h = [], []
    for pr in range(npair):
        q_t = q_ref[0, :, pr * LANES:(pr + 1) * LANES].astype(F32).T
        q_ts += [q_t, q_t]
        q_h += [jnp.where(feat < HEAD_DIM, q_t, 0.0).astype(BF16),
                jnp.where(feat >= HEAD_DIM, q_t, 0.0).astype(BF16)]

    jidx = lax.broadcasted_iota(I32, (nb, tq), 0)
    qblk = qi * (tq // MOBA_BLOCK) + lax.shift_right_logical(
        lax.broadcasted_iota(I32, (nb, tq), 1), blk_shift)
    for hx in range(nh):
        g = _dot(km_ref[hx], q_ts[hx])
        cnt = jnp.zeros((nb, tq), F32)
        for jp in range(nb):
            row = g[jp:jp + 1, :]
            beats = jnp.where(row > g, 1.0, jnp.where(row == g, jnp.where(jp < jidx, 1.0, 0.0), 0.0))
            cnt = cnt + jnp.where(jp < qblk, beats, 0.0)
        past_sel = jnp.where(jidx < qblk, jnp.where(cnt < MOBA_TOPK, 1.0, 0.0), 0.0)
        sel = jnp.where(jidx == qblk, 1.0, past_sel)
        bias_ref[hx] = jnp.where(sel > 0.5, 0.0, NEG_INF)

    def raw_scores(dst_ref, tile):
        off = pl.multiple_of(tile * KV_TILE, KV_TILE)
        for hx in range(nh):
            pr = hx // 2
            dst_ref[hx] = _dot(k_ref[0, pl.ds(off, KV_TILE), pr * LANES:(pr + 1) * LANES], q_h[hx])

    def fold(carry, src_ref, tile, causal=None):
        out = []
        for hx in range(nh):
            parts, biases = [], []
            for j in range(bpt):
                s_j = src_ref[hx, j * MOBA_BLOCK:(j + 1) * MOBA_BLOCK, :]
                if causal is not None:
                    s_j = jnp.where(causal[j], s_j, NEG_INF)
                b_j = bias_ref[hx, pl.ds(tile * bpt + j, 1), :]
                parts.append(s_j)
                biases.append(b_j)
            m_n = jnp.max(parts[0], axis=0, keepdims=True) + biases[0]
            for s_j, b_j in zip(parts[1:], biases[1:]):
                m_n = jnp.maximum(m_n, jnp.max(s_j, axis=0, keepdims=True) + b_j)
            if carry is None:
                p_t = jnp.concatenate([jnp.exp2(s_j + (b_j - m_n)) for s_j, b_j in zip(parts, biases)], axis=0)
                out += [m_n, _dot(vt_ref[hx, tile], p_t.astype(BF16))]
            else:
                m_c, acc_c = carry[2 * hx], carry[2 * hx + 1]
                m_n = jnp.maximum(m_c, m_n)
                p_t = jnp.concatenate([jnp.exp2(s_j + (b_j - m_n)) for s_j, b_j in zip(parts, biases)], axis=0)
                out += [m_n, acc_c * jnp.exp2(m_c - m_n) + _dot(vt_ref[hx, tile], p_t.astype(BF16))]
        return tuple(out)

    raw_scores(sa_ref, qi)
    raw_scores(sb_ref, 0)
    krow = lax.broadcasted_iota(I32, (MOBA_BLOCK, tq), 0)
    qcol = lax.broadcasted_iota(I32, (MOBA_BLOCK, tq), 1)
    state = fold(None, sa_ref, qi, causal=[krow + j * MOBA_BLOCK <= qcol for j in range(bpt)])

    def body(i, carry):
        t0 = 2 * i
        raw_scores(sa_ref, t0 + 1)
        carry = fold(carry, sb_ref, t0)
        raw_scores(sb_ref, jnp.minimum(t0 + 2, n_kv - 1))
        return fold(carry, sa_ref, t0 + 1)

    state = lax.fori_loop(0, qi // 2, body, state)
    state = lax.cond(lax.rem(qi, 2) == 1, lambda c: fold(c, sb_ref, qi - 1), lambda c: c, state)
    for pr in range(npair):
        accs = (state[4 * pr + 1], state[4 * pr + 3])
        o_t = jnp.concatenate([acc[:HEAD_DIM] / acc[HEAD_DIM:HEAD_DIM + 1] for acc in accs], axis=0)
        o_ref[0, :, pr * LANES:(pr + 1) * LANES] = o_t.T.astype(o_ref.dtype)


def _attention(q, k, v):
    bsz, seq, aw = q.shape
    pps = ATTN_PAIRS_PER_STEP
    wl = pps * LANES
    npair = aw // wl
    tq = KV_TILE
    nb = seq // MOBA_BLOCK
    assert seq % KV_TILE == 0 and nb % 8 == 0
    return pl.pallas_call(
        _attn_body,
        grid=(bsz, npair, seq // tq),
        in_specs=[pl.BlockSpec((1, tq, wl), lambda b, p, i: (b, i, p)),
                  pl.BlockSpec((1, seq, wl), lambda b, p, i: (b, 0, p)),
                  pl.BlockSpec((1, seq, wl), lambda b, p, i: (b, 0, p))],
        out_specs=pl.BlockSpec((1, tq, wl), lambda b, p, i: (b, i, p)),
        out_shape=jax.ShapeDtypeStruct((bsz, seq, aw), BF16),
        scratch_shapes=[pltpu.VMEM((2 * pps, seq // KV_TILE, PV_ROWS, KV_TILE), BF16),
                        pltpu.VMEM((2 * pps, nb, LANES), F32),
                        pltpu.VMEM((2 * pps, nb, tq), F32),
                        pltpu.VMEM((2 * pps, KV_TILE, tq), F32),
                        pltpu.VMEM((2 * pps, KV_TILE, tq), F32)],
        compiler_params=_cparams("parallel", "parallel", "arbitrary"),
        name="attn",
    )(q, k, v)


def _merge_body(y_ref, a_ref, x_ref, wglu_ref, bglu_ref, gs_ref, ga_ref, wout_ref, gm_ref,
                gf_ref, shf_ref, scf_ref, rw_ref, rb_ref, x1_ref, h2_ref, idx_ref, wt_ref, cnt_ref, ys_ref):
    nblk = y_ref.shape[0]
    tm = x_ref.shape[1]
    for j in range(nblk):
        for t in range(SSM_CHUNK):
            ys_ref[j, pl.ds(t, tm // SSM_CHUNK, stride=SSM_CHUNK), :] = (
                y_ref[j, 0, :, t * LANES:(t + 1) * LANES].astype(F32))
    y = jnp.concatenate([ys_ref[j] for j in range(nblk)], axis=1)
    g = 0.5 * y * (1.0 + lax.erf(y * (2.0 ** -0.5)))
    glu = g * jax.nn.sigmoid(_dot(g.astype(BF16), wglu_ref[...]) + bglu_ref[...])
    ssm_n = _rms(glu, gs_ref[...])
    att_n = _rms(a_ref[0].astype(F32), ga_ref[...])
    merged = jnp.concatenate([ssm_n, att_n], axis=1).astype(BF16)
    mix = _dot(merged, wout_ref[...])
    x1 = x_ref[0] + gm_ref[0] * mix
    x1_ref[0] = x1
    h2 = _rms(x1, gf_ref[...]) * (1.0 + scf_ref[0]) + shf_ref[0]
    h2_ref[0] = h2
    logits = lax.dot_general(rw_ref[...], h2, (((1,), (1,)), ((), ())),
                             preferred_element_type=F32, precision=HIGHEST) + rb_ref[...]
    ne = logits.shape[0]
    sub = lax.broadcasted_iota(I32, (ne, tm), 0)
    vals, idxs = [], []
    for _ in range(TOP_K):
        mx = jnp.max(logits, axis=0, keepdims=True)
        ix = jnp.min(jnp.where(logits == mx, sub, ne), axis=0, keepdims=True)
        vals.append(mx)
        idxs.append(ix)
        logits = jnp.where(sub == ix, -jnp.inf, logits)
    exps = [jnp.exp(vv - vals[0]) for vv in vals]
    tot = exps[0]
    for e in exps[1:]:
        tot = tot + e
    row_i = lax.broadcasted_iota(I32, idx_ref.shape, 0)
    row_w = lax.broadcasted_iota(I32, wt_ref.shape, 0)
    idx_out = jnp.zeros(idx_ref.shape, I32)
    wt_out = jnp.zeros(wt_ref.shape, F32)
    for kk in range(TOP_K):
        idx_out = jnp.where(row_i == kk, idxs[kk], idx_out)
        wt_out = jnp.where(row_w == kk, exps[kk] / tot, wt_out)
    idx_ref[...] = idx_out
    wt_ref[...] = wt_out
    chosen = jnp.zeros((ne, tm), F32)
    for kk in range(TOP_K):
        chosen = chosen + jnp.where(sub == idxs[kk], 1.0, 0.0)
    cnt_ref[0] = jnp.sum(chosen, axis=1, keepdims=True).astype(I32)


def _merge(y4, attn, x, w_glu_bf, b_glu, g_ssm, g_attn, w_out_bf, mod3, g_ffn, router_w, router_b, tm=512):
    bsz, seq, d = x.shape
    nblk = y4.shape[0]
    ssm_w = nblk * LANES
    aw = attn.shape[-1]
    ne = router_w.shape[1]
    tm = min(tm, seq)
    spb = seq // tm
    n_tok = bsz * seq
    rw_t = router_w.T
    row = lambda n: pl.BlockSpec((1, n), lambda b, i: (0, 0))
    full = lambda a: pl.BlockSpec(a.shape, lambda b, i: (0,) * a.ndim)
    modv = lambda j: pl.BlockSpec((1, 1, d), lambda b, i: (b, 0, j))
    return pl.pallas_call(
        _merge_body,
        grid=(bsz, seq // tm),
        in_specs=[pl.BlockSpec((nblk, 1, tm // SSM_CHUNK, SSM_CHUNK * LANES), lambda b, i: (0, b, i, 0)),
                  pl.BlockSpec((1, tm, aw), lambda b, i: (b, i, 0)),
                  pl.BlockSpec((1, tm, d), lambda b, i: (b, i, 0)),
                  full(w_glu_bf), row(ssm_w), row(ssm_w), row(aw), full(w_out_bf),
                  modv(2), row(d), modv(3), modv(4), full(rw_t),
                  pl.BlockSpec((ne, 1), lambda b, i: (0, 0))],
        out_specs=[pl.BlockSpec((1, tm, d), lambda b, i: (b, i, 0)),
                   pl.BlockSpec((1, tm, d), lambda b, i: (b, i, 0)),
                   pl.BlockSpec((TOP_K, tm), lambda b, i: (0, b * spb + i)),
                   pl.BlockSpec((WT_ROWS, tm), lambda b, i: (0, b * spb + i)),
                   pl.BlockSpec((1, ne, 1), lambda b, i: (b * spb + i, 0, 0))],
        out_shape=[jax.ShapeDtypeStruct((bsz, seq, d), F32),
                   jax.ShapeDtypeStruct((bsz, seq, d), F32),
                   jax.ShapeDtypeStruct((TOP_K, n_tok), I32),
                   jax.ShapeDtypeStruct((WT_ROWS, n_tok), F32),
                   jax.ShapeDtypeStruct((bsz * spb, ne, 1), I32)],
        scratch_shapes=[pltpu.VMEM((nblk, tm, LANES), F32)],
        compiler_params=_cparams("parallel", "parallel"),
        name="merge",
    )(y4, attn, x, w_glu_bf, b_glu.reshape(1, ssm_w), g_ssm.reshape(1, ssm_w), g_attn.reshape(1, aw),
      w_out_bf, mod3, g_ffn.reshape(1, d), mod3, mod3, rw_t, router_b.reshape(ne, 1))


def _segment_copies(n_seg, meta, make_copy):
    def body(e, _):
        src0, dst0, units = meta(e)
        for sz in SEG_UNITS:
            off = units & ~(2 * sz - 1)

            @pl.when((units & sz) != 0)
            def _(off=off, sz=sz):
                make_copy(src0 + off, dst0 + off, sz).start()
        return 0

    lax.fori_loop(0, n_seg, body, 0)


def _local_slots(idx, lstart_col, ne):
    tb = idx.shape[1]
    sub = lax.broadcasted_iota(I32, (ne, tb), 0)
    hits = [idx[kk:kk + 1, :] == sub for kk in range(TOP_K)]
    onehot = jnp.zeros((ne, tb), F32)
    for h in hits:
        onehot = onehot + jnp.where(h, 1.0, 0.0)
    r = lax.broadcasted_iota(I32, (tb, tb), 0)
    c = lax.broadcasted_iota(I32, (tb, tb), 1)
    tri = jnp.where(r < c, 1.0, 0.0).astype(BF16)
    slot = _dot(onehot.astype(BF16), tri) + lstart_col
    return [jnp.sum(jnp.where(h, slot, 0.0), axis=0, keepdims=True) for h in hits]


def _dispatch_body(lstart_s, pcnt_s, base_s, tail_dst_s, tail_units_s, tail_total_s,
                   idx_ref, lcol_ref, h_ref, xs_ref, ls_ref, sorted_ref, zero_ref, sem, *, ne):
    i = pl.program_id(0)
    nt = pl.num_programs(0)
    tb = h_ref.shape[0]
    lb = sorted_ref.shape[1] * SEG_ALIGN
    slot = lax.rem(i, 2)

    @pl.when(i == 0)
    def _():
        zero_ref[...] = jnp.zeros(zero_ref.shape, F32)
        _segment_copies(
            ne, lambda e: (0, tail_dst_s[e] // SEG_ALIGN, tail_units_s[e]),
            lambda src, dst, n: pltpu.make_async_copy(zero_ref.at[pl.ds(src, n)], xs_ref.at[pl.ds(dst, n)],
                                                      sem.at[2]))

    def tile_units(t):
        return (lstart_s[t * ne + ne - 1] + pcnt_s[t * ne + ne - 1]) // SEG_ALIGN

    def wait_tile(t, sl):
        done = xs_ref.at[pl.ds(0, tile_units(t))]
        pltpu.make_async_copy(done, done, sem.at[sl]).wait()

    @pl.when(i >= 2)
    def _():
        wait_tile(i - 2, slot)

    slots = _local_slots(idx_ref[...], lcol_ref[0].astype(F32), ne)
    row_o = lax.broadcasted_iota(I32, ls_ref.shape, 0)
    ls_out = jnp.full(ls_ref.shape, -1.0, F32)
    for kk in range(TOP_K):
        ls_out = jnp.where(row_o == kk, slots[kk], ls_out)
    ls_ref[...] = ls_out.astype(I32)
    h = h_ref[...].astype(BF16)
    for r0 in range(0, lb, SORT_CHUNK):
        j = (lax.broadcasted_iota(I32, (SORT_CHUNK, tb), 0) + r0).astype(F32)
        pm = jnp.zeros((SORT_CHUNK, tb), F32)
        for kk in range(TOP_K):
            pm = jnp.where(j == slots[kk], 1.0, pm)
        sorted_ref[slot, r0 // SEG_ALIGN:(r0 + SORT_CHUNK) // SEG_ALIGN] = _dot(pm.astype(BF16), h).reshape(
            SORT_CHUNK // SEG_ALIGN, SEG_ALIGN, h.shape[1])

    def meta(e):
        return (lstart_s[i * ne + e] // SEG_ALIGN, base_s[i * ne + e] // SEG_ALIGN,
                pcnt_s[i * ne + e] // SEG_ALIGN)

    def make_copy(src, dst, n):
        return pltpu.make_async_copy(sorted_ref.at[slot, pl.ds(src, n)], xs_ref.at[pl.ds(dst, n)], sem.at[slot])

    _segment_copies(ne, meta, make_copy)

    @pl.when(i == nt - 1)
    def _():
        @pl.when(i >= 1)
        def _():
            wait_tile(i - 1, 1 - slot)
        wait_tile(i, slot)

        @pl.when(tail_total_s[0] > 0)
        def _():
            filled = xs_ref.at[pl.ds(0, tail_total_s[0])]
            pltpu.make_async_copy(filled, filled, sem.at[2]).wait()


def _dispatch(h2, idx_t, plan, tail, cap, tb, tg):
    n, d = h2.shape
    lstart, pcnt, base = plan
    nt, ne = lstart.shape
    lb = tb * TOP_K + ne * SEG_ALIGN
    assert lb % SORT_CHUNK == 0
    return pl.pallas_call(
        functools.partial(_dispatch_body, ne=ne),
        grid_spec=pltpu.PrefetchScalarGridSpec(
            num_scalar_prefetch=6,
            grid=(nt,),
            in_specs=[pl.BlockSpec((TOP_K, tb), lambda i, *_: (0, i)),
                      pl.BlockSpec((1, ne, 1), lambda i, *_: (i, 0, 0)),
                      pl.BlockSpec((tb, d), lambda i, *_: (i, 0))],
            out_specs=[pl.BlockSpec(memory_space=pl.ANY),
                       pl.BlockSpec((WT_ROWS, tb), lambda i, *_: (0, i))],
            scratch_shapes=[pltpu.VMEM((2, lb // SEG_ALIGN, SEG_ALIGN, d), F32),
                            pltpu.VMEM((tg // SEG_ALIGN, SEG_ALIGN, d), F32),
                            pltpu.SemaphoreType.DMA((3,))],
        ),
        out_shape=[jax.ShapeDtypeStruct((cap // SEG_ALIGN, SEG_ALIGN, d), F32),
                   jax.ShapeDtypeStruct((WT_ROWS, n), I32)],
        compiler_params=_cparams("arbitrary"),
        name="dispatch",
    )(lstart.reshape(-1), pcnt.reshape(-1), base.reshape(-1), *tail, idx_t, lstart.reshape(nt, ne, 1), h2)


def _route_plan(tile_cnt, tg, cap):
    cnt = tile_cnt[:, :, 0]
    ne = cnt.shape[1]
    pcnt = (cnt + SEG_ALIGN - 1) // SEG_ALIGN * SEG_ALIGN
    lstart = jnp.cumsum(pcnt, axis=1) - pcnt
    e_rows = jnp.sum(pcnt, axis=0)
    e_tiles = (e_rows + tg - 1) // tg
    e_end = jnp.cumsum(e_tiles)
    base = ((e_end - e_tiles) * tg)[None, :] + jnp.cumsum(pcnt, axis=0) - pcnt
    ii = jnp.arange(cap // tg, dtype=I32)
    used = ii < e_end[-1]
    tile = jnp.minimum(ii, e_end[-1] - 1)
    e_of = jnp.minimum(jnp.sum((e_end[None, :] <= tile[:, None]).astype(I32), axis=1), ne - 1)
    newe = (e_of != jnp.concatenate([jnp.full((1,), -1, I32), e_of[:-1]])).astype(I32)
    items = (tile.astype(I32), e_of.astype(I32), used.astype(I32), newe)
    tail = (e_end * tg - e_tiles * tg + e_rows, (e_tiles * tg - e_rows) // SEG_ALIGN)
    tail = (tail[0].astype(I32), tail[1].astype(I32), jnp.sum(tail[1]).astype(I32).reshape(1))
    return (lstart.astype(I32), pcnt.astype(I32), base.astype(I32)), tail, items


def _experts_body(tile_ref, exp_ref, used_ref, newe_ref,
                  xs_ref, wg_ref, bg_ref, wu_ref, bu_ref, wd_ref, bd_ref, ys_ref,
                  wg_bf, wu_bf, wd_bf):
    i = pl.program_id(0)

    @pl.when(newe_ref[i] == 1)
    def _cast():
        wg_bf[...] = wg_ref[0].astype(BF16)
        wu_bf[...] = wu_ref[0].astype(BF16)
        wd_bf[...] = wd_ref[0].astype(BF16)

    @pl.when(used_ref[i] == 1)
    def _compute():
        x = xs_ref[...].astype(BF16)
        gate = jnp.minimum(_dot(x, wg_bf[...]) + bg_ref[0], SWIGLU_LIMIT)
        lin = jnp.clip(_dot(x, wu_bf[...]) + bu_ref[0], -SWIGLU_LIMIT, SWIGLU_LIMIT)
        act = gate * jax.nn.sigmoid(SWIGLU_ALPHA * gate) * (lin + 1.0)
        ys_ref[...] = _dot(act.astype(BF16), wd_bf[...]) + bd_ref[0]


def _experts(xs, items, w_gate, b_gate, w_up, b_up, w_down, b_down, tg):
    nk, d = xs.shape
    ne, _, dff = w_gate.shape
    n_items = items[0].shape[0]
    wspec = lambda shp: pl.BlockSpec((1,) + shp, lambda i, t, e, u, nw: (e[i], 0, 0))
    return pl.pallas_call(
        _experts_body,
        grid_spec=pltpu.PrefetchScalarGridSpec(
            num_scalar_prefetch=4,
            grid=(n_items,),
            in_specs=[pl.BlockSpec((tg, d), lambda i, t, e, u, nw: (t[i], 0)),
                      wspec((d, dff)), wspec((1, dff)),
                      wspec((d, dff)), wspec((1, dff)),
                      wspec((dff, d)), wspec((1, d))],
            out_specs=pl.BlockSpec((tg, d), lambda i, t, e, u, nw: (t[i], 0)),
            scratch_shapes=[pltpu.VMEM((d, dff), BF16), pltpu.VMEM((d, dff), BF16), pltpu.VMEM((dff, d), BF16)],
        ),
        out_shape=jax.ShapeDtypeStruct((nk, d), F32),
        compiler_params=_cparams("arbitrary"),
        name="experts",
    )(*items, xs, w_gate, b_gate.reshape(ne, 1, dff), w_up, b_up.reshape(ne, 1, dff),
      w_down, b_down.reshape(ne, 1, d))


def _combine_body(lstart_s, pcnt_s, base_s, ls_ref, wt_ref, x1_ref, gf_ref, gfin_ref, sho_ref, sco_ref,
                  ys_ref, o_ref, buf, sem, *, ne, spb):
    i = pl.program_id(0) * spb + pl.program_id(1)
    nt = pl.num_programs(0) * spb
    tb = x1_ref.shape[1]
    lb = buf.shape[1] * SEG_ALIGN
    slot = lax.rem(i, 2)

    def fetch(t, sl):
        def meta(e):
            return (base_s[t * ne + e] // SEG_ALIGN, lstart_s[t * ne + e] // SEG_ALIGN,
                    pcnt_s[t * ne + e] // SEG_ALIGN)

        def make_copy(src, dst, n):
            return pltpu.make_async_copy(ys_ref.at[pl.ds(src, n)], buf.at[sl, pl.ds(dst, n)], sem.at[sl])

        _segment_copies(ne, meta, make_copy)

    @pl.when(i == 0)
    def _():
        buf[...] = jnp.zeros(buf.shape, F32)
        fetch(0, 0)

    @pl.when(i + 1 < nt)
    def _():
        fetch(i + 1, 1 - slot)

    total = (lstart_s[i * ne + ne - 1] + pcnt_s[i * ne + ne - 1]) // SEG_ALIGN
    done = buf.at[slot, pl.ds(0, total)]
    pltpu.make_async_copy(done, done, sem.at[slot]).wait()
    ls_col = ls_ref[...].astype(F32).T
    wt_col = wt_ref[...].T
    ffn = jnp.zeros((tb, x1_ref.shape[2]), F32)
    for r0 in range(0, lb, SORT_CHUNK):
        j = (lax.broadcasted_iota(I32, (tb, SORT_CHUNK), 1) + r0).astype(F32)
        wm = jnp.zeros((tb, SORT_CHUNK), F32)
        for kk in range(TOP_K):
            wm = jnp.where(j == ls_col[:, kk:kk + 1], wt_col[:, kk:kk + 1], wm)
        y_rows = buf[slot, r0 // SEG_ALIGN:(r0 + SORT_CHUNK) // SEG_ALIGN].reshape(SORT_CHUNK, x1_ref.shape[2])
        ffn = ffn + _dot(wm.astype(BF16), y_rows.astype(BF16))
    x2 = x1_ref[0] + gf_ref[0] * ffn
    o_ref[0] = _rms(x2, gfin_ref[...]) * (1.0 + sco_ref[0]) + sho_ref[0]


def _combine(plan, ls_t, wt_t, x1, mod3, g_final, fmod3, ys, tb):
    bsz, seq, d = x1.shape
    lstart, pcnt, base = plan
    nt, ne = lstart.shape
    spb = seq // tb
    lb = tb * TOP_K + ne * SEG_ALIGN
    return pl.pallas_call(
        functools.partial(_combine_body, ne=ne, spb=spb),
        grid_spec=pltpu.PrefetchScalarGridSpec(
            num_scalar_prefetch=3,
            grid=(bsz, spb),
            in_specs=[pl.BlockSpec((WT_ROWS, tb), lambda b, i, *_: (0, b * spb + i)),
                      pl.BlockSpec((WT_ROWS, tb), lambda b, i, *_: (0, b * spb + i)),
                      pl.BlockSpec((1, tb, d), lambda b, i, *_: (b, i, 0)),
                      pl.BlockSpec((1, 1, d), lambda b, i, *_: (b, 0, 5)),
                      pl.BlockSpec((1, d), lambda b, i, *_: (0, 0)),
                      pl.BlockSpec((1, 1, d), lambda b, i, *_: (b, 0, 0)),
                      pl.BlockSpec((1, 1, d), lambda b, i, *_: (b, 0, 1)),
                      pl.BlockSpec(memory_space=pl.ANY)],
            out_specs=pl.BlockSpec((1, tb, d), lambda b, i, *_: (b, i, 0)),
            scratch_shapes=[pltpu.VMEM((2, lb // SEG_ALIGN, SEG_ALIGN, d), F32), pltpu.SemaphoreType.DMA((2,))],
        ),
        out_shape=jax.ShapeDtypeStruct((bsz, seq, d), F32),
        compiler_params=_cparams("arbitrary", "arbitrary"),
        name="combine",
    )(lstart.reshape(-1), pcnt.reshape(-1), base.reshape(-1), ls_t, wt_t, x1, mod3, g_final.reshape(1, d),
      fmod3, fmod3, ys.reshape(-1, SEG_ALIGN, d))


def kernel(x, c, positions, ada_w, ada_b, final_ada_w, final_ada_b, norm_mix_g, norm_ffn_g, norm_final_g, w_in, ssm_lambda_re, ssm_lambda_im, ssm_log_dt, ssm_b_re, ssm_b_im, ssm_c_re, ssm_c_im, ssm_d, ssm_w_glu, ssm_b_glu, out_norm_ssm_g, out_norm_attn_g, w_out, router_w, router_b, exp_w_gate, exp_b_gate, exp_w_up, exp_b_up, exp_w_down, exp_b_down):
    bsz, seq, d = x.shape
    depth = ada_w.shape[0]
    ssm_w = ssm_d.shape[-1]
    attn_w = (w_in.shape[-1] - ssm_w) // 3
    ne = router_w.shape[-1]
    n_tok = bsz * seq
    nc = seq // SSM_CHUNK
    tg = 512

    rope_tab, rope_exp = _rope_tables(positions)
    fmod3 = _adaln(c, final_ada_w, final_ada_b).reshape(bsz, 1, 2 * d)
    for l in range(depth):
        mod3 = _adaln(c, ada_w[l], ada_b[l]).reshape(bsz, 1, -1)
        u4, q, k, v = _inproj(x, norm_mix_g[l], mod3, w_in[l].astype(BF16), rope_tab, rope_exp, ssm_w, attn_w)
        tables = _s5_params(ssm_lambda_re[l], ssm_lambda_im[l], ssm_log_dt[l], ssm_b_re[l], ssm_b_im[l],
                            ssm_c_re[l], ssm_c_im[l], ssm_d[l], nc)
        y4 = _s5(u4, *tables)
        attn = _attention(q, k, v)
        x1, h2, idx_t, wt_t, tile_cnt = _merge(y4, attn, x, ssm_w_glu[l].astype(BF16), ssm_b_glu[l], out_norm_ssm_g[l],
                                     out_norm_attn_g[l], w_out[l].astype(BF16), mod3, norm_ffn_g[l],
                                     router_w[l], router_b[l])
        tb = seq // (tile_cnt.shape[0] // bsz)
        cap = n_tok * TOP_K + tile_cnt.shape[0] * ne * SEG_ALIGN + ne * tg
        assert cap % tg == 0 and max(tb, tg) // SEG_ALIGN < 2 * SEG_UNITS[0]
        plan, tail, items = _route_plan(tile_cnt, tg, cap)
        xs, ls_t = _dispatch(h2.reshape(n_tok, d), idx_t, plan, tail, cap, tb, tg)
        ys = _experts(xs.reshape(cap, d), items, exp_w_gate[l], exp_b_gate[l], exp_w_up[l], exp_b_up[l],
                      exp_w_down[l], exp_b_down[l], tg)
        if l + 1 < depth:
            raise NotImplementedError("depth > 1 needs the non-final combine")
        x = _combine(plan, ls_t, wt_t, x1, mod3, norm_final_g, fmod3, ys, tb)
    return x
```

```python
import functools
import math

import jax
import jax.numpy as jnp
from jax import lax
from jax.experimental import pallas as pl
from jax.experimental.pallas import tpu as pltpu

F32 = jnp.float32
BF16 = jnp.bfloat16
I32 = jnp.int32
HIGHEST = lax.Precision.HIGHEST

LANES = 128
HEAD_DIM = 64
MOBA_BLOCK = 256
MOBA_TOPK = 3
ROT_DIM = HEAD_DIM // 4
ROPE_THETA = 500000.0
SSM_CHUNK = 16
TOP_K = 4
SWIGLU_ALPHA = 1.702
SWIGLU_LIMIT = 7.0
NORM_EPS = 1e-5
NEG_INF = -1e30
LOG2_E = math.log2(math.e)
KV_TILE = 2 * MOBA_BLOCK
PV_ROWS = HEAD_DIM + 8
ATTN_PAIRS_PER_STEP = 4
WT_ROWS = 8
SEG_ALIGN = 8
SEG_UNITS = (64, 32, 16, 8, 4, 2, 1)
SORT_CHUNK = 256
VMEM_LIMIT = 56 * 1024 * 1024


def _cparams(*sem):
    return pltpu.CompilerParams(dimension_semantics=sem, vmem_limit_bytes=VMEM_LIMIT)


def _dot(a, b):
    return jnp.dot(a, b, preferred_element_type=F32)


def _rms(x, g):
    return x * lax.rsqrt(jnp.mean(x * x, axis=-1, keepdims=True) + NORM_EPS) * g


def _adaln_body(c_ref, w_ref, b_ref, o_ref):
    c = c_ref[...]
    ca = c * jax.nn.sigmoid(c)
    o_ref[...] = jnp.dot(ca, w_ref[...], preferred_element_type=F32, precision=HIGHEST) + b_ref[...]


def _adaln(c, w, b, tn=512):
    bsz, d = c.shape
    n = w.shape[1]
    return pl.pallas_call(
        _adaln_body,
        grid=(n // tn,),
        in_specs=[pl.BlockSpec((bsz, d), lambda j: (0, 0)),
                  pl.BlockSpec((d, tn), lambda j: (0, j)),
                  pl.BlockSpec((1, tn), lambda j: (0, j))],
        out_specs=pl.BlockSpec((bsz, tn), lambda j: (0, j)),
        out_shape=jax.ShapeDtypeStruct((bsz, n), F32),
        compiler_params=_cparams("parallel"),
        name="adaln",
    )(c, w, b.reshape(1, n))


def _inproj_body(x_ref, g_ref, sh_ref, sc_ref, w_ref, tab_ref, exp_ref, u_ref, q_ref, k_ref, v_ref, us_ref):
    x = x_ref[0]
    tm = x.shape[0]
    h = _rms(x, g_ref[...]) * (1.0 + sc_ref[0]) + sh_ref[0]
    proj = _dot(h.astype(BF16), w_ref[...])
    ssm_w = u_ref.shape[0] * LANES
    aw = q_ref.shape[2]
    for j in range(u_ref.shape[0]):
        us_ref[j] = proj[:, j * LANES:(j + 1) * LANES]
        for t in range(SSM_CHUNK):
            u_ref[j, 0, :, t * LANES:(t + 1) * LANES] = us_ref[
                j, pl.ds(t, tm // SSM_CHUNK, stride=SSM_CHUNK), :].astype(BF16)
    t_hi = tab_ref[0].astype(BF16)
    t_lo = (tab_ref[0] - t_hi.astype(F32)).astype(BF16)
    tn = (((0,), (0,)), ((), ()))
    tab = (lax.dot_general(t_hi, exp_ref[...], tn, preferred_element_type=F32)
           + lax.dot_general(t_lo, exp_ref[...], tn, preferred_element_type=F32))
    lane1 = lax.broadcasted_iota(I32, (tm, LANES), 1)
    cos1 = tab[:, :LANES] + jnp.where((lane1 & (HEAD_DIM - 1)) >= ROT_DIM, 1.0, 0.0)
    reps = aw // LANES
    cosf = jnp.concatenate([cos1] * reps, axis=1)
    sinf = jnp.concatenate([tab[:, LANES:]] * reps, axis=1)
    lane = lax.broadcasted_iota(I32, (tm, aw), 1)
    first_half = (lane & (HEAD_DIM - 1)) < (ROT_DIM // 2)

    def rope(t):
        partner = jnp.where(first_half, pltpu.roll(t, aw - ROT_DIM // 2, 1), pltpu.roll(t, ROT_DIM // 2, 1))
        return t * cosf + partner * sinf

    q = rope(proj[:, ssm_w:ssm_w + aw]) * (HEAD_DIM ** -0.5 * LOG2_E)
    k = rope(proj[:, ssm_w + aw:ssm_w + 2 * aw])
    q_ref[0] = q.astype(BF16)
    k_ref[0] = k.astype(BF16)
    v_ref[0] = proj[:, ssm_w + 2 * aw:ssm_w + 3 * aw].astype(BF16)


def _inproj(x, gain, mod3, w_in_bf, rope_tab, rope_exp, ssm_w, attn_w, tm=1024):
    bsz, seq, d = x.shape
    n_u = ssm_w // LANES
    tm = min(tm, seq)
    return pl.pallas_call(
        _inproj_body,
        grid=(bsz, seq // tm),
        in_specs=[pl.BlockSpec((1, tm, d), lambda b, i: (b, i, 0)),
                  pl.BlockSpec((1, d), lambda b, i: (0, 0)),
                  pl.BlockSpec((1, 1, d), lambda b, i: (b, 0, 0)),
                  pl.BlockSpec((1, 1, d), lambda b, i: (b, 0, 1)),
                  pl.BlockSpec(w_in_bf.shape, lambda b, i: (0, 0)),
                  pl.BlockSpec((1, ROT_DIM, tm), lambda b, i: (b, 0, i)),
                  pl.BlockSpec(rope_exp.shape, lambda b, i: (0, 0))],
        out_specs=[pl.BlockSpec((n_u, 1, tm // SSM_CHUNK, SSM_CHUNK * LANES), lambda b, i: (0, b, i, 0)),
                   pl.BlockSpec((1, tm, attn_w), lambda b, i: (b, i, 0)),
                   pl.BlockSpec((1, tm, attn_w), lambda b, i: (b, i, 0)),
                   pl.BlockSpec((1, tm, attn_w), lambda b, i: (b, i, 0))],
        out_shape=[jax.ShapeDtypeStruct((n_u, bsz, seq // SSM_CHUNK, SSM_CHUNK * LANES), BF16),
                   jax.ShapeDtypeStruct((bsz, seq, attn_w), BF16),
                   jax.ShapeDtypeStruct((bsz, seq, attn_w), BF16),
                   jax.ShapeDtypeStruct((bsz, seq, attn_w), BF16)],
        scratch_shapes=[pltpu.VMEM((n_u, tm, LANES), F32)],
        compiler_params=_cparams("parallel", "parallel"),
        name="inproj",
    )(x, gain.reshape(1, d), mod3, mod3, w_in_bf, rope_tab, rope_exp)


def _rope_tables(positions):
    half = ROT_DIM // 2
    inv_freq = ROPE_THETA ** (-jnp.arange(0, ROT_DIM, 2, dtype=F32) / ROT_DIM)
    ang = positions.astype(F32)[:, None, :] * inv_freq[None, :, None]
    tab = jnp.concatenate([jnp.cos(ang), jnp.sin(ang)], axis=1)
    i = jnp.arange(ROT_DIM)[:, None]
    hl = jnp.arange(LANES)[None, :] % HEAD_DIM
    e_cos = jnp.where((i < half) & ((hl == i) | (hl == i + half)), 1.0, 0.0)
    e_sin = jnp.where(i >= half, jnp.where(hl == i - half, -1.0, jnp.where(hl == i, 1.0, 0.0)), 0.0)
    return tab, jnp.concatenate([e_cos, e_sin], axis=1).astype(BF16)


def _s5_params(lam_re, lam_im, log_dt, b_re, b_im, c_re, c_im, d_skip, n_chunks):
    g_all, p = lam_re.shape
    hc = b_re.shape[-1]
    t = SSM_CHUNK
    gpb = LANES // hc
    nblk = g_all // gpb
    lr, li = lam_re.astype(F32), lam_im.astype(F32)
    dt = jnp.exp(log_dt.astype(F32))[:, None]
    ldr, ldi = lr * dt, li * dt

    def apow(k):
        k = jnp.asarray(k, F32)[..., None, None]
        mag = jnp.exp(ldr * k)
        return mag * jnp.cos(ldi * k), mag * jnp.sin(ldi * k)

    ar, ai = apow(1.0)
    zr, zi = ar - 1.0, ai
    den = lr * lr + li * li
    cr = (zr * lr + zi * li) / den
    ci = (zi * lr - zr * li) / den
    bre, bim = b_re.astype(F32), b_im.astype(F32)
    bbr = cr[..., None] * bre - ci[..., None] * bim
    bbi = cr[..., None] * bim + ci[..., None] * bre
    cre, cim = c_re.astype(F32), c_im.astype(F32)
    def block_diag(a):
        w = a.shape[-1]
        wide = jnp.concatenate([a] * gpb, axis=-1)
        grp = jnp.arange(gpb)[:, None, None]
        lane_grp = (jnp.arange(gpb * w) // w)[None, None, :]
        wide = jnp.where(lane_grp == grp, wide, 0.0)
        return wide.reshape(a.shape[:-3] + (gpb * a.shape[-2], gpb * w))

    pr, pi = apow(jnp.arange(t))
    abr = pr[..., None] * bbr - pi[..., None] * bbi
    abi = pr[..., None] * bbi + pi[..., None] * bbr
    kf = (jnp.einsum('gnp,kgph->kghn', cre, abr, precision=HIGHEST)
          - jnp.einsum('gnp,kgph->kghn', cim, abi, precision=HIGHEST))
    kblk = block_diag(kf.reshape(t, nblk, gpb, hc, hc))
    kpad = jnp.concatenate([jnp.zeros_like(kblk[:1]), kblk], axis=0)
    pm = jnp.concatenate([kpad[0:t], kpad[1:t + 1]], axis=-1)
    kstack = jnp.transpose(pm[::-1], (1, 0, 2, 3)).reshape(nblk, t * LANES, 2 * LANES)

    def inject(ab):
        ab = jnp.transpose(ab[::-1].reshape(t, nblk, gpb, p, hc), (1, 0, 2, 4, 3))
        return block_diag(ab).reshape(nblk, t * LANES, gpb * p)

    bm = jnp.concatenate([inject(abr), inject(abi)], axis=-1)

    qr, qi = apow(jnp.arange(t) + 1.0)
    c_from_re = cre[None] * qr[:, :, None, :] - cim[None] * qi[:, :, None, :]
    c_from_im = -cre[None] * qi[:, :, None, :] - cim[None] * qr[:, :, None, :]

    def readout_t(cc):
        cc = jnp.transpose(cc.reshape(t, nblk, gpb, hc, p), (1, 0, 2, 3, 4))
        return block_diag(cc).reshape(nblk, t * LANES, gpb * p)

    cm = jnp.swapaxes(jnp.concatenate([readout_t(c_from_re), readout_t(c_from_im)], axis=-1), 1, 2)

    n_steps = max(1, int(math.log2(n_chunks)))
    shifts = [float(t * (1 << s)) for s in range(n_steps)]
    sr, si = apow(jnp.asarray(shifts))
    ap = jnp.concatenate([sr.reshape(n_steps, nblk, gpb * p), si.reshape(n_steps, nblk, gpb * p)], axis=-1)
    ap = jnp.transpose(ap, (1, 0, 2))
    dvec = jnp.tile(d_skip.astype(F32).reshape(nblk, 1, LANES), (1, 1, t))
    return kstack.astype(BF16), bm.astype(BF16), cm.astype(BF16), ap, dvec


def _s5_body(x_ref, ks_ref, bm_ref, cm_ref, ap_ref, d_ref, y_ref, h_ref):
    nc = x_ref.shape[2]
    t = SSM_CHUNK
    pad = h_ref.shape[0] - nc
    half = h_ref.shape[1] // 2
    x = x_ref[0, 0]
    h_ref[0:pad] = jnp.zeros((pad, h_ref.shape[1]), F32)
    h_ref[pad:pad + nc] = _dot(x, bm_ref[0])
    for step in range(ap_ref.shape[1]):
        d = 1 << step
        cur = h_ref[pad:pad + nc]
        sft = h_ref[pad - d:pad + nc - d]
        ar = ap_ref[0, step:step + 1, 0:half]
        ai = ap_ref[0, step:step + 1, half:]
        cr, ci = cur[:, :half], cur[:, half:]
        sr, si = sft[:, :half], sft[:, half:]
        h_ref[pad:pad + nc, 0:half] = cr + ar * sr - ai * si
        h_ref[pad:pad + nc, half:] = ci + ar * si + ai * sr
    h_prev = h_ref[pad - 1:pad + nc - 1]
    ys = _dot(h_prev.astype(BF16), cm_ref[0])
    for t0 in range(0, t, 2):
        lo, hi = t0 * LANES, (t0 + 2) * LANES
        conv = _dot(x[:, :hi], ks_ref[0, (t - 2 - t0) * LANES:, :])
        y = conv + ys[:, lo:hi] + d_ref[0][:, lo:hi] * x[:, lo:hi].astype(F32)
        y_ref[0, 0, :, lo:hi] = y.astype(y_ref.dtype)


def _s5(u4, kstack, bm, cm, ap, dvec):
    nblk, bsz, nc, w = u4.shape
    sw = bm.shape[-1]
    pad = max(nc // 2, 8)
    return pl.pallas_call(
        _s5_body,
        grid=(nblk, bsz),
        in_specs=[pl.BlockSpec((1, 1, nc, w), lambda j, b: (j, b, 0, 0)),
                  pl.BlockSpec((1,) + kstack.shape[1:], lambda j, b: (j, 0, 0)),
                  pl.BlockSpec((1,) + bm.shape[1:], lambda j, b: (j, 0, 0)),
                  pl.BlockSpec((1,) + cm.shape[1:], lambda j, b: (j, 0, 0)),
                  pl.BlockSpec((1,) + ap.shape[1:], lambda j, b: (j, 0, 0)),
                  pl.BlockSpec((1, 1, w), lambda j, b: (j, 0, 0))],
        out_specs=pl.BlockSpec((1, 1, nc, w), lambda j, b: (j, b, 0, 0)),
        out_shape=jax.ShapeDtypeStruct((nblk, bsz, nc, w), BF16),
        scratch_shapes=[pltpu.VMEM((pad + nc, sw), F32)],
        compiler_params=_cparams("parallel", "parallel"),
        name="s5",
    )(u4, kstack, bm, cm, ap, dvec)


def _attn_body(q_ref, k_ref, v_ref, o_ref, vt_ref, km_ref, bias_ref, sa_ref, sb_ref):
    qi = pl.program_id(2)
    seq = k_ref.shape[1]
    tq = q_ref.shape[1]
    npair = q_ref.shape[2] // LANES
    nh = 2 * npair
    nb = seq // MOBA_BLOCK
    n_kv = seq // KV_TILE
    bpt = KV_TILE // MOBA_BLOCK
    blk_shift = MOBA_BLOCK.bit_length() - 1

    @pl.when(qi == 0)
    def _build():
        ones = jnp.where(lax.broadcasted_iota(I32, (PV_ROWS - HEAD_DIM, KV_TILE), 0) == 0, 1.0, 0.0)
        lane_b = lax.broadcasted_iota(I32, (nb, LANES), 1)
        for pr in range(npair):
            k = k_ref[0, :, pr * LANES:(pr + 1) * LANES].astype(F32)
            v = v_ref[0, :, pr * LANES:(pr + 1) * LANES].astype(F32)
            for t in range(n_kv):
                v_t = v[t * KV_TILE:(t + 1) * KV_TILE].T
                vt_ref[2 * pr, t] = jnp.concatenate([v_t[:HEAD_DIM], ones], axis=0).astype(BF16)
                vt_ref[2 * pr + 1, t] = jnp.concatenate([v_t[HEAD_DIM:], ones], axis=0).astype(BF16)
            km = jnp.mean(k.reshape(nb, MOBA_BLOCK, LANES), axis=1)
            km_ref[2 * pr] = jnp.where(lane_b < HEAD_DIM, km, 0.0)
            km_ref[2 * pr + 1] = jnp.where(lane_b >= HEAD_DIM, km, 0.0)

    feat = lax.broadcasted_iota(I32, (LANES, tq), 0)
    q_ts, q_h = [], []
    for pr in range(npair):
        q_t = q_ref[0, :, pr * LANES:(pr + 1) * LANES].astype(F32).T
        q_ts += [q_t, q_t]
        q_h += [jnp.where(feat < HEAD_DIM, q_t, 0.0).astype(BF16),
                jnp.where(feat >= HEAD_DIM, q_t, 0.0).astype(BF16)]

    jidx = lax.broadcasted_iota(I32, (nb, tq), 0)
    qblk = qi * (tq // MOBA_BLOCK) + lax.shift_right_logical(
        lax.broadcasted_iota(I32, (nb, tq), 1), blk_shift)
    for hx in range(nh):
        g = _dot(km_ref[hx], q_ts[hx])
        cnt = jnp.zeros((nb, tq), F32)
        for jp in range(nb):
            row = g[jp:jp + 1, :]
            beats = jnp.where(row > g, 1.0, jnp.where(row == g, jnp.where(jp < jidx, 1.0, 0.0), 0.0))
            cnt = cnt + jnp.where(jp < qblk, beats, 0.0)
        past_sel = jnp.where(jidx < qblk, jnp.where(cnt < MOBA_TOPK, 1.0, 0.0), 0.0)
        sel = jnp.where(jidx == qblk, 1.0, past_sel)
        bias_ref[hx] = jnp.where(sel > 0.5, 0.0, NEG_INF)

    def raw_scores(dst_ref, tile):
        off = pl.multiple_of(tile * KV_TILE, KV_TILE)
        for hx in range(nh):
            pr = hx // 2
            dst_ref[hx] = _dot(k_ref[0, pl.ds(off, KV_TILE), pr * LANES:(pr + 1) * LANES], q_h[hx])

    def fold(carry, src_ref, tile, causal=None):
        out = []
        for hx in range(nh):
            parts, biases = [], []
            for j in range(bpt):
                s_j = src_ref[hx, j * MOBA_BLOCK:(j + 1) * MOBA_BLOCK, :]
                if causal is not None:
                    s_j = jnp.where(causal[j], s_j, NEG_INF)
                b_j = bias_ref[hx, pl.ds(tile * bpt + j, 1), :]
                parts.append(s_j)
                biases.append(b_j)
            m_n = jnp.max(parts[0], axis=0, keepdims=True) + biases[0]
            for s_j, b_j in zip(parts[1:], biases[1:]):
                m_n = jnp.maximum(m_n, jnp.max(s_j, axis=0, keepdims=True) + b_j)
            if carry is None:
                p_t = jnp.concatenate([jnp.exp2(s_j + (b_j - m_n)) for s_j, b_j in zip(parts, biases)], axis=0)
                out += [m_n, _dot(vt_ref[hx, tile], p_t.astype(BF16))]
            else:
                m_c, acc_c = carry[2 * hx], carry[2 * hx + 1]
                m_n = jnp.maximum(m_c, m_n)
                p_t = jnp.concatenate([jnp.exp2(s_j + (b_j - m_n)) for s_j, b_j in zip(parts, biases)], axis=0)
                out += [m_n, acc_c * jnp.exp2(m_c - m_n) + _dot(vt_ref[hx, tile], p_t.astype(BF16))]
        return tuple(out)

    raw_scores(sa_ref, qi)
    raw_scores(sb_ref, 0)
    krow = lax.broadcasted_iota(I32, (MOBA_BLOCK, tq), 0)
    qcol = lax.broadcasted_iota(I32, (MOBA_BLOCK, tq), 1)
    state = fold(None, sa_ref, qi, causal=[krow + j * MOBA_BLOCK <= qcol for j in range(bpt)])

    def body(i, carry):
        t0 = 2 * i
        raw_scores(sa_ref, t0 + 1)
        carry = fold(carry, sb_ref, t0)
        raw_scores(sb_ref, jnp.minimum(t0 + 2, n_kv - 1))
        return fold(carry, sa_ref, t0 + 1)

    state = lax.fori_loop(0, qi // 2, body, state)
    state = lax.cond(lax.rem(qi, 2) == 1, lambda c: fold(c, sb_ref, qi - 1), lambda c: c, state)
    for pr in range(npair):
        accs = (state[4 * pr + 1], state[4 * pr + 3])
        o_t = jnp.concatenate([acc[:HEAD_DIM] / acc[HEAD_DIM:HEAD_DIM + 1] for acc in accs], axis=0)
        o_ref[0, :, pr * LANES:(pr + 1) * LANES] = o_t.T.astype(o_ref.dtype)


def _attention(q, k, v):
    bsz, seq, aw = q.shape
    pps = ATTN_PAIRS_PER_STEP
    wl = pps * LANES
    npair = aw // wl
    tq = KV_TILE
    nb = seq // MOBA_BLOCK
    assert seq % KV_TILE == 0 and nb % 8 == 0
    return pl.pallas_call(
        _attn_body,
        grid=(bsz, npair, seq // tq),
        in_specs=[pl.BlockSpec((1, tq, wl), lambda b, p, i: (b, i, p)),
                  pl.BlockSpec((1, seq, wl), lambda b, p, i: (b, 0, p)),
                  pl.BlockSpec((1, seq, wl), lambda b, p, i: (b, 0, p))],
        out_specs=pl.BlockSpec((1, tq, wl), lambda b, p, i: (b, i, p)),
        out_shape=jax.ShapeDtypeStruct((bsz, seq, aw), BF16),
        scratch_shapes=[pltpu.VMEM((2 * pps, seq // KV_TILE, PV_ROWS, KV_TILE), BF16),
                        pltpu.VMEM((2 * pps, nb, LANES), F32),
                        pltpu.VMEM((2 * pps, nb, tq), F32),
                        pltpu.VMEM((2 * pps, KV_TILE, tq), F32),
                        pltpu.VMEM((2 * pps, KV_TILE, tq), F32)],
        compiler_params=_cparams("parallel", "parallel", "arbitrary"),
        name="attn",
    )(q, k, v)


def _merge_body(y_ref, a_ref, x_ref, wglu_ref, bglu_ref, gs_ref, ga_ref, wout_ref, gm_ref,
                gf_ref, shf_ref, scf_ref, rw_ref, rb_ref, x1_ref, h2_ref, idx_ref, wt_ref, cnt_ref, ys_ref):
    nblk = y_ref.shape[0]
    tm = x_ref.shape[1]
    for j in range(nblk):
        for t in range(SSM_CHUNK):
            ys_ref[j, pl.ds(t, tm // SSM_CHUNK, stride=SSM_CHUNK), :] = (
                y_ref[j, 0, :, t * LANES:(t + 1) * LANES].astype(F32))
    y = jnp.concatenate([ys_ref[j] for j in range(nblk)], axis=1)
    g = 0.5 * y * (1.0 + lax.erf(y * (2.0 ** -0.5)))
    glu = g * jax.nn.sigmoid(_dot(g.astype(BF16), wglu_ref[...]) + bglu_ref[...])
    ssm_n = _rms(glu, gs_ref[...])
    att_n = _rms(a_ref[0].astype(F32), ga_ref[...])
    merged = jnp.concatenate([ssm_n, att_n], axis=1).astype(BF16)
    mix = _dot(merged, wout_ref[...])
    x1 = x_ref[0] + gm_ref[0] * mix
    x1_ref[0] = x1
    h2 = _rms(x1, gf_ref[...]) * (1.0 + scf_ref[0]) + shf_ref[0]
    h2_ref[0] = h2
    logits = lax.dot_general(rw_ref[...], h2, (((1,), (1,)), ((), ())),
                             preferred_element_type=F32, precision=HIGHEST) + rb_ref[...]
    ne = logits.shape[0]
    sub = lax.broadcasted_iota(I32, (ne, tm), 0)
    vals, idxs = [], []
    for _ in range(TOP_K):
        mx = jnp.max(logits, axis=0, keepdims=True)
        ix = jnp.min(jnp.where(logits == mx, sub, ne), axis=0, keepdims=True)
        vals.append(mx)
        idxs.append(ix)
        logits = jnp.where(sub == ix, -jnp.inf, logits)
    exps = [jnp.exp(vv - vals[0]) for vv in vals]
    tot = exps[0]
    for e in exps[1:]:
        tot = tot + e
    row_i = lax.broadcasted_iota(I32, idx_ref.shape, 0)
    row_w = lax.broadcasted_iota(I32, wt_ref.shape, 0)
    idx_out = jnp.zeros(idx_ref.shape, I32)
    wt_out = jnp.zeros(wt_ref.shape, F32)
    for kk in range(TOP_K):
        idx_out = jnp.where(row_i == kk, idxs[kk], idx_out)
        wt_out = jnp.where(row_w == kk, exps[kk] / tot, wt_out)
    idx_ref[...] = idx_out
    wt_ref[...] = wt_out
    chosen = jnp.zeros((ne, tm), F32)
    for kk in range(TOP_K):
        chosen = chosen + jnp.where(sub == idxs[kk], 1.0, 0.0)
    cnt_ref[0] = jnp.sum(chosen, axis=1, keepdims=True).astype(I32)


def _merge(y4, attn, x, w_glu_bf, b_glu, g_ssm, g_attn, w_out_bf, mod3, g_ffn, router_w, router_b, tm=512):
    bsz, seq, d = x.shape
    nblk = y4.shape[0]
    ssm_w = nblk * LANES
    aw = attn.shape[-1]
    ne = router_w.shape[1]
    tm = min(tm, seq)
    spb = seq // tm
    n_tok = bsz * seq
    rw_t = router_w.T
    row = lambda n: pl.BlockSpec((1, n), lambda b, i: (0, 0))
    full = lambda a: pl.BlockSpec(a.shape, lambda b, i: (0,) * a.ndim)
    modv = lambda j: pl.BlockSpec((1, 1, d), lambda b, i: (b, 0, j))
    return pl.pallas_call(
        _merge_body,
        grid=(bsz, seq // tm),
        in_specs=[pl.BlockSpec((nblk, 1, tm // SSM_CHUNK, SSM_CHUNK * LANES), lambda b, i: (0, b, i, 0)),
                  pl.BlockSpec((1, tm, aw), lambda b, i: (b, i, 0)),
                  pl.BlockSpec((1, tm, d), lambda b, i: (b, i, 0)),
                  full(w_glu_bf), row(ssm_w), row(ssm_w), row(aw), full(w_out_bf),
                  modv(2), row(d), modv(3), modv(4), full(rw_t),
                  pl.BlockSpec((ne, 1), lambda b, i: (0, 0))],
        out_specs=[pl.BlockSpec((1, tm, d), lambda b, i: (b, i, 0)),
                   pl.BlockSpec((1, tm, d), lambda b, i: (b, i, 0)),
                   pl.BlockSpec((TOP_K, tm), lambda b, i: (0, b * spb + i)),
                   pl.BlockSpec((WT_ROWS, tm), lambda b, i: (0, b * spb + i)),
                   pl.BlockSpec((1, ne, 1), lambda b, i: (b * spb + i, 0, 0))],
        out_shape=[jax.ShapeDtypeStruct((bsz, seq, d), F32),
                   jax.ShapeDtypeStruct((bsz, seq, d), F32),
                   jax.ShapeDtypeStruct((TOP_K, n_tok), I32),
                   jax.ShapeDtypeStruct((WT_ROWS, n_tok), F32),
                   jax.ShapeDtypeStruct((bsz * spb, ne, 1), I32)],
        scratch_shapes=[pltpu.VMEM((nblk, tm, LANES), F32)],
        compiler_params=_cparams("parallel", "parallel"),
        name="merge",
    )(y4, attn, x, w_glu_bf, b_glu.reshape(1, ssm_w), g_ssm.reshape(1, ssm_w), g_attn.reshape(1, aw),
      w_out_bf, mod3, g_ffn.reshape(1, d), mod3, mod3, rw_t, router_b.reshape(ne, 1))


def _segment_copies(n_seg, meta, make_copy):
    def body(e, _):
        src0, dst0, units = meta(e)
        for sz in SEG_UNITS:
            off = units & ~(2 * sz - 1)

            @pl.when((units & sz) != 0)
            def _(off=off, sz=sz):
                make_copy(src0 + off, dst0 + off, sz).start()
        return 0

    lax.fori_loop(0, n_seg, body, 0)


def _local_slots(idx, lstart_col, ne):
    tb = idx.shape[1]
    sub = lax.broadcasted_iota(I32, (ne, tb), 0)
    hits = [idx[kk:kk + 1, :] == sub for kk in range(TOP_K)]
    onehot = jnp.zeros((ne, tb), F32)
    for h in hits:
        onehot = onehot + jnp.where(h, 1.0, 0.0)
    r = lax.broadcasted_iota(I32, (tb, tb), 0)
    c = lax.broadcasted_iota(I32, (tb, tb), 1)
    tri = jnp.where(r < c, 1.0, 0.0).astype(BF16)
    slot = _dot(onehot.astype(BF16), tri) + lstart_col
    return [jnp.sum(jnp.where(h, slot, 0.0), axis=0, keepdims=True) for h in hits]


def _dispatch_body(lstart_s, pcnt_s, base_s, tail_dst_s, tail_units_s, tail_total_s,
                   idx_ref, lcol_ref, h_ref, xs_ref, ls_ref, sorted_ref, zero_ref, sem, *, ne):
    i = pl.program_id(0)
    nt = pl.num_programs(0)
    tb = h_ref.shape[0]
    lb = sorted_ref.shape[1] * SEG_ALIGN
    slot = lax.rem(i, 2)

    @pl.when(i == 0)
    def _():
        zero_ref[...] = jnp.zeros(zero_ref.shape, F32)
        _segment_copies(
            ne, lambda e: (0, tail_dst_s[e] // SEG_ALIGN, tail_units_s[e]),
            lambda src, dst, n: pltpu.make_async_copy(zero_ref.at[pl.ds(src, n)], xs_ref.at[pl.ds(dst, n)],
                                                      sem.at[2]))

    def tile_units(t):
        return (lstart_s[t * ne + ne - 1] + pcnt_s[t * ne + ne - 1]) // SEG_ALIGN

    def wait_tile(t, sl):
        done = xs_ref.at[pl.ds(0, tile_units(t))]
        pltpu.make_async_copy(done, done, sem.at[sl]).wait()

    @pl.when(i >= 2)
    def _():
        wait_tile(i - 2, slot)

    slots = _local_slots(idx_ref[...], lcol_ref[0].astype(F32), ne)
    row_o = lax.broadcasted_iota(I32, ls_ref.shape, 0)
    ls_out = jnp.full(ls_ref.shape, -1.0, F32)
    for kk in range(TOP_K):
        ls_out = jnp.where(row_o == kk, slots[kk], ls_out)
    ls_ref[...] = ls_out.astype(I32)
    h = h_ref[...].astype(BF16)
    for r0 in range(0, lb, SORT_CHUNK):
        j = (lax.broadcasted_iota(I32, (SORT_CHUNK, tb), 0) + r0).astype(F32)
        pm = jnp.zeros((SORT_CHUNK, tb), F32)
        for kk in range(TOP_K):
            pm = jnp.where(j == slots[kk], 1.0, pm)
        sorted_ref[slot, r0 // SEG_ALIGN:(r0 + SORT_CHUNK) // SEG_ALIGN] = _dot(pm.astype(BF16), h).reshape(
            SORT_CHUNK // SEG_ALIGN, SEG_ALIGN, h.shape[1])

    def meta(e):
        return (lstart_s[i * ne + e] // SEG_ALIGN, base_s[i * ne + e] // SEG_ALIGN,
                pcnt_s[i * ne + e] // SEG_ALIGN)

    def make_copy(src, dst, n):
        return pltpu.make_async_copy(sorted_ref.at[slot, pl.ds(src, n)], xs_ref.at[pl.ds(dst, n)], sem.at[slot])

    _segment_copies(ne, meta, make_copy)

    @pl.when(i == nt - 1)
    def _():
        @pl.when(i >= 1)
        def _():
            wait_tile(i - 1, 1 - slot)
        wait_tile(i, slot)

        @pl.when(tail_total_s[0] > 0)
        def _():
            filled = xs_ref.at[pl.ds(0, tail_total_s[0])]
            pltpu.make_async_copy(filled, filled, sem.at[2]).wait()


def _dispatch(h2, idx_t, plan, tail, cap, tb, tg):
    n, d = h2.shape
    lstart, pcnt, base = plan
    nt, ne = lstart.shape
    lb = tb * TOP_K + ne * SEG_ALIGN
    assert lb % SORT_CHUNK == 0
    return pl.pallas_call(
        functools.partial(_dispatch_body, ne=ne),
        grid_spec=pltpu.PrefetchScalarGridSpec(
            num_scalar_prefetch=6,
            grid=(nt,),
            in_specs=[pl.BlockSpec((TOP_K, tb), lambda i, *_: (0, i)),
                      pl.BlockSpec((1, ne, 1), lambda i, *_: (i, 0, 0)),
                      pl.BlockSpec((tb, d), lambda i, *_: (i, 0))],
            out_specs=[pl.BlockSpec(memory_space=pl.ANY),
                       pl.BlockSpec((WT_ROWS, tb), lambda i, *_: (0, i))],
            scratch_shapes=[pltpu.VMEM((2, lb // SEG_ALIGN, SEG_ALIGN, d), F32),
                            pltpu.VMEM((tg // SEG_ALIGN, SEG_ALIGN, d), F32),
                            pltpu.SemaphoreType.DMA((3,))],
        ),
        out_shape=[jax.ShapeDtypeStruct((cap // SEG_ALIGN, SEG_ALIGN, d), F32),
                   jax.ShapeDtypeStruct((WT_ROWS, n), I32)],
        compiler_params=_cparams("arbitrary"),
        name="dispatch",
    )(lstart.reshape(-1), pcnt.reshape(-1), base.reshape(-1), *tail, idx_t, lstart.reshape(nt, ne, 1), h2)


def _route_plan(tile_cnt, tg, cap):
    cnt = tile_cnt[:, :, 0]
    ne = cnt.shape[1]
    pcnt = (cnt + SEG_ALIGN - 1) // SEG_ALIGN * SEG_ALIGN
    lstart = jnp.cumsum(pcnt, axis=1) - pcnt
    e_rows = jnp.sum(pcnt, axis=0)
    e_tiles = (e_rows + tg - 1) // tg
    e_end = jnp.cumsum(e_tiles)
    base = ((e_end - e_tiles) * tg)[None, :] + jnp.cumsum(pcnt, axis=0) - pcnt
    ii = jnp.arange(cap // tg, dtype=I32)
    used = ii < e_end[-1]
    tile = jnp.minimum(ii, e_end[-1] - 1)
    e_of = jnp.minimum(jnp.sum((e_end[None, :] <= tile[:, None]).astype(I32), axis=1), ne - 1)
    newe = (e_of != jnp.concatenate([jnp.full((1,), -1, I32), e_of[:-1]])).astype(I32)
    items = (tile.astype(I32), e_of.astype(I32), used.astype(I32), newe)
    tail = (e_end * tg - e_tiles * tg + e_rows, (e_tiles * tg - e_rows) // SEG_ALIGN)
    tail = (tail[0].astype(I32), tail[1].astype(I32), jnp.sum(tail[1]).astype(I32).reshape(1))
    return (lstart.astype(I32), pcnt.astype(I32), base.astype(I32)), tail, items


def _experts_body(tile_ref, exp_ref, used_ref, newe_ref,
                  xs_ref, wg_ref, bg_ref, wu_ref, bu_ref, wd_ref, bd_ref, ys_ref,
                  wg_bf, wu_bf, wd_bf):
    i = pl.program_id(0)

    @pl.when(newe_ref[i] == 1)
    def _cast():
        wg_bf[...] = wg_ref[0].astype(BF16)
        wu_bf[...] = wu_ref[0].astype(BF16)
        wd_bf[...] = wd_ref[0].astype(BF16)

    @pl.when(used_ref[i] == 1)
    def _compute():
        x = xs_ref[...].astype(BF16)
        gate = jnp.minimum(_dot(x, wg_bf[...]) + bg_ref[0], SWIGLU_LIMIT)
        lin = jnp.clip(_dot(x, wu_bf[...]) + bu_ref[0], -SWIGLU_LIMIT, SWIGLU_LIMIT)
        act = gate * jax.nn.sigmoid(SWIGLU_ALPHA * gate) * (lin + 1.0)
        ys_ref[...] = _dot(act.astype(BF16), wd_bf[...]) + bd_ref[0]


def _experts(xs, items, w_gate, b_gate, w_up, b_up, w_down, b_down, tg):
    nk, d = xs.shape
    ne, _, dff = w_gate.shape
    n_items = items[0].shape[0]
    wspec = lambda shp: pl.BlockSpec((1,) + shp, lambda i, t, e, u, nw: (e[i], 0, 0))
    return pl.pallas_call(
        _experts_body,
        grid_spec=pltpu.PrefetchScalarGridSpec(
            num_scalar_prefetch=4,
            grid=(n_items,),
            in_specs=[pl.BlockSpec((tg, d), lambda i, t, e, u, nw: (t[i], 0)),
                      wspec((d, dff)), wspec((1, dff)),
                      wspec((d, dff)), wspec((1, dff)),
                      wspec((dff, d)), wspec((1, d))],
            out_specs=pl.BlockSpec((tg, d), lambda i, t, e, u, nw: (t[i], 0)),
            scratch_shapes=[pltpu.VMEM((d, dff), BF16), pltpu.VMEM((d, dff), BF16), pltpu.VMEM((dff, d), BF16)],
        ),
        out_shape=jax.ShapeDtypeStruct((nk, d), F32),
        compiler_params=_cparams("arbitrary"),
        name="experts",
    )(*items, xs, w_gate, b_gate.reshape(ne, 1, dff), w_up, b_up.reshape(ne, 1, dff),
      w_down, b_down.reshape(ne, 1, d))


def _combine_body(lstart_s, pcnt_s, base_s, ls_ref, wt_ref, x1_ref, gf_ref, gfin_ref, sho_ref, sco_ref,
                  ys_ref, o_ref, buf, sem, *, ne, spb):
    i = pl.program_id(0) * spb + pl.program_id(1)
    nt = pl.num_programs(0) * spb
    tb = x1_ref.shape[1]
    lb = buf.shape[1] * SEG_ALIGN
    slot = lax.rem(i, 2)

    def fetch(t, sl):
        def meta(e):
            return (base_s[t * ne + e] // SEG_ALIGN, lstart_s[t * ne + e] // SEG_ALIGN,
                    pcnt_s[t * ne + e] // SEG_ALIGN)

        def make_copy(src, dst, n):
            return pltpu.make_async_copy(ys_ref.at[pl.ds(src, n)], buf.at[sl, pl.ds(dst, n)], sem.at[sl])

        _segment_copies(ne, meta, make_copy)

    @pl.when(i == 0)
    def _():
        buf[...] = jnp.zeros(buf.shape, F32)
        fetch(0, 0)

    @pl.when(i + 1 < nt)
    def _():
        fetch(i + 1, 1 - slot)

    total = (lstart_s[i * ne + ne - 1] + pcnt_s[i * ne + ne - 1]) // SEG_ALIGN
    done = buf.at[slot, pl.ds(0, total)]
    pltpu.make_async_copy(done, done, sem.at[slot]).wait()
    ls_col = ls_ref[...].astype(F32).T
    wt_col = wt_ref[...].T
    ffn = jnp.zeros((tb, x1_ref.shape[2]), F32)
    for r0 in range(0, lb, SORT_CHUNK):
        j = (lax.broadcasted_iota(I32, (tb, SORT_CHUNK), 1) + r0).astype(F32)
        wm = jnp.zeros((tb, SORT_CHUNK), F32)
        for kk in range(TOP_K):
            wm = jnp.where(j == ls_col[:, kk:kk + 1], wt_col[:, kk:kk + 1], wm)
        y_rows = buf[slot, r0 // SEG_ALIGN:(r0 + SORT_CHUNK) // SEG_ALIGN].reshape(SORT_CHUNK, x1_ref.shape[2])
        ffn = ffn + _dot(wm.astype(BF16), y_rows.astype(BF16))
    x2 = x1_ref[0] + gf_ref[0] * ffn
    o_ref[0] = _rms(x2, gfin_ref[...]) * (1.0 + sco_ref[0]) + sho_ref[0]


def _combine(plan, ls_t, wt_t, x1, mod3, g_final, fmod3, ys, tb):
    bsz, seq, d = x1.shape
    lstart, pcnt, base = plan
    nt, ne = lstart.shape
    spb = seq // tb
    lb = tb * TOP_K + ne * SEG_ALIGN
    return pl.pallas_call(
        functools.partial(_combine_body, ne=ne, spb=spb),
        grid_spec=pltpu.PrefetchScalarGridSpec(
            num_scalar_prefetch=3,
            grid=(bsz, spb),
            in_specs=[pl.BlockSpec((WT_ROWS, tb), lambda b, i, *_: (0, b * spb + i)),
                      pl.BlockSpec((WT_ROWS, tb), lambda b, i, *_: (0, b * spb + i)),
                      pl.BlockSpec((1, tb, d), lambda b, i, *_: (b, i, 0)),
                      pl.BlockSpec((1, 1, d), lambda b, i, *_: (b, 0, 5)),
                      pl.BlockSpec((1, d), lambda b, i, *_: (0, 0)),
                      pl.BlockSpec((1, 1, d), lambda b, i, *_: (b, 0, 0)),
                      pl.BlockSpec((1, 1, d), lambda b, i, *_: (b, 0, 1)),
                      pl.BlockSpec(memory_space=pl.ANY)],
            out_specs=pl.BlockSpec((1, tb, d), lambda b, i, *_: (b, i, 0)),
            scratch_shapes=[pltpu.VMEM((2, lb // SEG_ALIGN, SEG_ALIGN, d), F32), pltpu.SemaphoreType.DMA((2,))],
        ),
        out_shape=jax.ShapeDtypeStruct((bsz, seq, d), F32),
        compiler_params=_cparams("arbitrary", "arbitrary"),
        name="combine",
    )(lstart.reshape(-1), pcnt.reshape(-1), base.reshape(-1), ls_t, wt_t, x1, mod3, g_final.reshape(1, d),
      fmod3, fmod3, ys.reshape(-1, SEG_ALIGN, d))


def kernel(x, c, positions, ada_w, ada_b, final_ada_w, final_ada_b, norm_mix_g, norm_ffn_g, norm_final_g, w_in, ssm_lambda_re, ssm_lambda_im, ssm_log_dt, ssm_b_re, ssm_b_im, ssm_c_re, ssm_c_im, ssm_d, ssm_w_glu, ssm_b_glu, out_norm_ssm_g, out_norm_attn_g, w_out, router_w, router_b, exp_w_gate, exp_b_gate, exp_w_up, exp_b_up, exp_w_down, exp_b_down):
    bsz, seq, d = x.shape
    depth = ada_w.shape[0]
    ssm_w = ssm_d.shape[-1]
    attn_w = (w_in.shape[-1] - ssm_w) // 3
    ne = router_w.shape[-1]
    n_tok = bsz * seq
    nc = seq // SSM_CHUNK
    tg = 512

    rope_tab, rope_exp = _rope_tables(positions)
    fmod3 = _adaln(c, final_ada_w, final_ada_b).reshape(bsz, 1, 2 * d)
    for l in range(depth):
        mod3 = _adaln(c, ada_w[l], ada_b[l]).reshape(bsz, 1, -1)
        u4, q, k, v = _inproj(x, norm_mix_g[l], mod3, w_in[l].astype(BF16), rope_tab, rope_exp, ssm_w, attn_w)
        tables = _s5_params(ssm_lambda_re[l], ssm_lambda_im[l], ssm_log_dt[l], ssm_b_re[l], ssm_b_im[l],
                            ssm_c_re[l], ssm_c_im[l], ssm_d[l], nc)
        y4 = _s5(u4, *tables)
        attn = _attention(q, k, v)
        x1, h2, idx_t, wt_t, tile_cnt = _merge(y4, attn, x, ssm_w_glu[l].astype(BF16), ssm_b_glu[l], out_norm_ssm_g[l],
                                     out_norm_attn_g[l], w_out[l].astype(BF16), mod3, norm_ffn_g[l],
                                     router_w[l], router_b[l])
        tb = seq // (tile_cnt.shape[0] // bsz)
        cap = n_tok * TOP_K + tile_cnt.shape[0] * ne * SEG_ALIGN + ne * tg
        assert cap % tg == 0 and max(tb, tg) // SEG_ALIGN < 2 * SEG_UNITS[0]
        plan, tail, items = _route_plan(tile_cnt, tg, cap)
        xs, ls_t = _dispatch(h2.reshape(n_tok, d), idx_t, plan, tail, cap, tb, tg)
        ys = _experts(xs.reshape(cap, d), items, exp_w_gate[l], exp_b_gate[l], exp_w_up[l], exp_b_up[l],
                      exp_w_down[l], exp_b_down[l], tg)
        if l + 1 < depth:
            raise NotImplementedError("depth > 1 needs the non-final combine")
        x = _combine(plan, ls_t, wt_t, x1, mod3, norm_final_g, fmod3, ys, tb)
    return x
```

```python
import functools
import math

import jax
import jax.numpy as jnp
from jax import lax
from jax.experimental import pallas as pl
from jax.experimental.pallas import tpu as pltpu

F32 = jnp.float32
BF16 = jnp.bfloat16
I32 = jnp.int32
HIGHEST = lax.Precision.HIGHEST

LANES = 128
HEAD_DIM = 64
MOBA_BLOCK = 256
MOBA_TOPK = 3
ROT_DIM = HEAD_DIM // 4
ROPE_THETA = 500000.0
SSM_CHUNK = 16
TOP_K = 4
SWIGLU_ALPHA = 1.702
SWIGLU_LIMIT = 7.0
NORM_EPS = 1e-5
NEG_INF = -1e30
LOG2_E = math.log2(math.e)
KV_TILE = 2 * MOBA_BLOCK
PV_ROWS = HEAD_DIM + 8
ATTN_PAIRS_PER_STEP = 4
WT_ROWS = 8
SEG_ALIGN = 8
SEG_UNITS = (64, 32, 16, 8, 4, 2, 1)
SORT_CHUNK = 256
TILES_PER_STEP = 2
VMEM_LIMIT = 56 * 1024 * 1024


def _cparams(*sem):
    return pltpu.CompilerParams(dimension_semantics=sem, vmem_limit_bytes=VMEM_LIMIT)


def _dot(a, b):
    return jnp.dot(a, b, preferred_element_type=F32)


def _rms(x, g):
    return x * lax.rsqrt(jnp.mean(x * x, axis=-1, keepdims=True) + NORM_EPS) * g


def _adaln_body(c_ref, w_ref, b_ref, o_ref):
    c = c_ref[...]
    ca = c * jax.nn.sigmoid(c)
    o_ref[...] = jnp.dot(ca, w_ref[...], preferred_element_type=F32, precision=HIGHEST) + b_ref[...]


def _adaln(c, w, b, tn=512):
    bsz, d = c.shape
    n = w.shape[1]
    return pl.pallas_call(
        _adaln_body,
        grid=(n // tn,),
        in_specs=[pl.BlockSpec((bsz, d), lambda j: (0, 0)),
                  pl.BlockSpec((d, tn), lambda j: (0, j)),
                  pl.BlockSpec((1, tn), lambda j: (0, j))],
        out_specs=pl.BlockSpec((bsz, tn), lambda j: (0, j)),
        out_shape=jax.ShapeDtypeStruct((bsz, n), F32),
        compiler_params=_cparams("parallel"),
        name="adaln",
    )(c, w, b.reshape(1, n))


def _inproj_body(x_ref, g_ref, sh_ref, sc_ref, w_ref, tab_ref, exp_ref, u_ref, q_ref, k_ref, v_ref, us_ref):
    x = x_ref[0]
    tm = x.shape[0]
    h = _rms(x, g_ref[...]) * (1.0 + sc_ref[0]) + sh_ref[0]
    proj = _dot(h.astype(BF16), w_ref[...])
    ssm_w = u_ref.shape[0] * LANES
    aw = q_ref.shape[2]
    for j in range(u_ref.shape[0]):
        us_ref[j] = proj[:, j * LANES:(j + 1) * LANES]
        for t in range(SSM_CHUNK):
            u_ref[j, 0, :, t * LANES:(t + 1) * LANES] = us_ref[
                j, pl.ds(t, tm // SSM_CHUNK, stride=SSM_CHUNK), :].astype(BF16)
    t_hi = tab_ref[0].astype(BF16)
    t_lo = (tab_ref[0] - t_hi.astype(F32)).astype(BF16)
    tn = (((0,), (0,)), ((), ()))
    tab = (lax.dot_general(t_hi, exp_ref[...], tn, preferred_element_type=F32)
           + lax.dot_general(t_lo, exp_ref[...], tn, preferred_element_type=F32))
    lane1 = lax.broadcasted_iota(I32, (tm, LANES), 1)
    cos1 = tab[:, :LANES] + jnp.where((lane1 & (HEAD_DIM - 1)) >= ROT_DIM, 1.0, 0.0)
    reps = aw // LANES
    cosf = jnp.concatenate([cos1] * reps, axis=1)
    sinf = jnp.concatenate([tab[:, LANES:]] * reps, axis=1)
    lane = lax.broadcasted_iota(I32, (tm, aw), 1)
    first_half = (lane & (HEAD_DIM - 1)) < (ROT_DIM // 2)

    def rope(t):
        partner = jnp.where(first_half, pltpu.roll(t, aw - ROT_DIM // 2, 1), pltpu.roll(t, ROT_DIM // 2, 1))
        return t * cosf + partner * sinf

    q = rope(proj[:, ssm_w:ssm_w + aw]) * (HEAD_DIM ** -0.5 * LOG2_E)
    k = rope(proj[:, ssm_w + aw:ssm_w + 2 * aw])
    q_ref[0] = q.astype(BF16)
    k_ref[0] = k.astype(BF16)
    v_ref[0] = proj[:, ssm_w + 2 * aw:ssm_w + 3 * aw].astype(BF16)


def _inproj(x, gain, mod3, w_in_bf, rope_tab, rope_exp, ssm_w, attn_w, tm=1024):
    bsz, seq, d = x.shape
    n_u = ssm_w // LANES
    tm = min(tm, seq)
    return pl.pallas_call(
        _inproj_body,
        grid=(bsz, seq // tm),
        in_specs=[pl.BlockSpec((1, tm, d), lambda b, i: (b, i, 0)),
                  pl.BlockSpec((1, d), lambda b, i: (0, 0)),
                  pl.BlockSpec((1, 1, d), lambda b, i: (b, 0, 0)),
                  pl.BlockSpec((1, 1, d), lambda b, i: (b, 0, 1)),
                  pl.BlockSpec(w_in_bf.shape, lambda b, i: (0, 0)),
                  pl.BlockSpec((1, ROT_DIM, tm), lambda b, i: (b, 0, i)),
                  pl.BlockSpec(rope_exp.shape, lambda b, i: (0, 0))],
        out_specs=[pl.BlockSpec((n_u, 1, tm // SSM_CHUNK, SSM_CHUNK * LANES), lambda b, i: (0, b, i, 0)),
                   pl.BlockSpec((1, tm, attn_w), lambda b, i: (b, i, 0)),
                   pl.BlockSpec((1, tm, attn_w), lambda b, i: (b, i, 0)),
                   pl.BlockSpec((1, tm, attn_w), lambda b, i: (b, i, 0))],
        out_shape=[jax.ShapeDtypeStruct((n_u, bsz, seq // SSM_CHUNK, SSM_CHUNK * LANES), BF16),
                   jax.ShapeDtypeStruct((bsz, seq, attn_w), BF16),
                   jax.ShapeDtypeStruct((bsz, seq, attn_w), BF16),
                   jax.ShapeDtypeStruct((bsz, seq, attn_w), BF16)],
        scratch_shapes=[pltpu.VMEM((n_u, tm, LANES), F32)],
        compiler_params=_cparams("parallel", "parallel"),
        name="inproj",
    )(x, gain.reshape(1, d), mod3, mod3, w_in_bf, rope_tab, rope_exp)


def _rope_tables(positions):
    half = ROT_DIM // 2
    inv_freq = ROPE_THETA ** (-jnp.arange(0, ROT_DIM, 2, dtype=F32) / ROT_DIM)
    ang = positions.astype(F32)[:, None, :] * inv_freq[None, :, None]
    tab = jnp.concatenate([jnp.cos(ang), jnp.sin(ang)], axis=1)
    i = jnp.arange(ROT_DIM)[:, None]
    hl = jnp.arange(LANES)[None, :] % HEAD_DIM
    e_cos = jnp.where((i < half) & ((hl == i) | (hl == i + half)), 1.0, 0.0)
    e_sin = jnp.where(i >= half, jnp.where(hl == i - half, -1.0, jnp.where(hl == i, 1.0, 0.0)), 0.0)
    return tab, jnp.concatenate([e_cos, e_sin], axis=1).astype(BF16)


def _s5_params(lam_re, lam_im, log_dt, b_re, b_im, c_re, c_im, d_skip, n_chunks):
    g_all, p = lam_re.shape
    hc = b_re.shape[-1]
    t = SSM_CHUNK
    gpb = LANES // hc
    nblk = g_all // gpb
    lr, li = lam_re.astype(F32), lam_im.astype(F32)
    dt = jnp.exp(log_dt.astype(F32))[:, None]
    ldr, ldi = lr * dt, li * dt

    def apow(k):
        k = jnp.asarray(k, F32)[..., None, None]
        mag = jnp.exp(ldr * k)
        return mag * jnp.cos(ldi * k), mag * jnp.sin(ldi * k)

    ar, ai = apow(1.0)
    zr, zi = ar - 1.0, ai
    den = lr * lr + li * li
    cr = (zr * lr + zi * li) / den
    ci = (zi * lr - zr * li) / den
    bre, bim = b_re.astype(F32), b_im.astype(F32)
    bbr = cr[..., None] * bre - ci[..., None] * bim
    bbi = cr[..., None] * bim + ci[..., None] * bre
    cre, cim = c_re.astype(F32), c_im.astype(F32)
    def block_diag(a):
        w = a.shape[-1]
        wide = jnp.concatenate([a] * gpb, axis=-1)
        grp = jnp.arange(gpb)[:, None, None]
        lane_grp = (jnp.arange(gpb * w) // w)[None, None, :]
        wide = jnp.where(lane_grp == grp, wide, 0.0)
        return wide.reshape(a.shape[:-3] + (gpb * a.shape[-2], gpb * w))

    pr, pi = apow(jnp.arange(t))
    abr = pr[..., None] * bbr - pi[..., None] * bbi
    abi = pr[..., None] * bbi + pi[..., None] * bbr
    kf = (jnp.einsum('gnp,kgph->kghn', cre, abr, precision=HIGHEST)
          - jnp.einsum('gnp,kgph->kghn', cim, abi, precision=HIGHEST))
    kblk = block_diag(kf.reshape(t, nblk, gpb, hc, hc))
    kpad = jnp.concatenate([jnp.zeros_like(kblk[:1]), kblk], axis=0)
    pm = jnp.concatenate([kpad[0:t], kpad[1:t + 1]], axis=-1)
    kstack = jnp.transpose(pm[::-1], (1, 0, 2, 3)).reshape(nblk, t * LANES, 2 * LANES)

    def inject(ab):
        ab = jnp.transpose(ab[::-1].reshape(t, nblk, gpb, p, hc), (1, 0, 2, 4, 3))
        return block_diag(ab).reshape(nblk, t * LANES, gpb * p)

    bm = jnp.concatenate([inject(abr), inject(abi)], axis=-1)

    qr, qi = apow(jnp.arange(t) + 1.0)
    c_from_re = cre[None] * qr[:, :, None, :] - cim[None] * qi[:, :, None, :]
    c_from_im = -cre[None] * qi[:, :, None, :] - cim[None] * qr[:, :, None, :]

    def readout_t(cc):
        cc = jnp.transpose(cc.reshape(t, nblk, gpb, hc, p), (1, 0, 2, 3, 4))
        return block_diag(cc).reshape(nblk, t * LANES, gpb * p)

    cm = jnp.swapaxes(jnp.concatenate([readout_t(c_from_re), readout_t(c_from_im)], axis=-1), 1, 2)

    n_steps = max(1, int(math.log2(n_chunks)))
    shifts = [float(t * (1 << s)) for s in range(n_steps)]
    sr, si = apow(jnp.asarray(shifts))
    ap = jnp.concatenate([sr.reshape(n_steps, nblk, gpb * p), si.reshape(n_steps, nblk, gpb * p)], axis=-1)
    ap = jnp.transpose(ap, (1, 0, 2))
    dvec = jnp.tile(d_skip.astype(F32).reshape(nblk, 1, LANES), (1, 1, t))
    return kstack.astype(BF16), bm.astype(BF16), cm.astype(BF16), ap, dvec


def _s5_body(x_ref, ks_ref, bm_ref, cm_ref, ap_ref, d_ref, y_ref, h_ref):
    nc = x_ref.shape[2]
    t = SSM_CHUNK
    pad = h_ref.shape[0] - nc
    half = h_ref.shape[1] // 2
    x = x_ref[0, 0]
    h_ref[0:pad] = jnp.zeros((pad, h_ref.shape[1]), F32)
    h_ref[pad:pad + nc] = _dot(x, bm_ref[0])
    for step in range(ap_ref.shape[1]):
        d = 1 << step
        cur = h_ref[pad:pad + nc]
        sft = h_ref[pad - d:pad + nc - d]
        ar = ap_ref[0, step:step + 1, 0:half]
        ai = ap_ref[0, step:step + 1, half:]
        cr, ci = cur[:, :half], cur[:, half:]
        sr, si = sft[:, :half], sft[:, half:]
        h_ref[pad:pad + nc, 0:half] = cr + ar * sr - ai * si
        h_ref[pad:pad + nc, half:] = ci + ar * si + ai * sr
    h_prev = h_ref[pad - 1:pad + nc - 1]
    ys = _dot(h_prev.astype(BF16), cm_ref[0])
    for t0 in range(0, t, 2):
        lo, hi = t0 * LANES, (t0 + 2) * LANES
        conv = _dot(x[:, :hi], ks_ref[0, (t - 2 - t0) * LANES:, :])
        y = conv + ys[:, lo:hi] + d_ref[0][:, lo:hi] * x[:, lo:hi].astype(F32)
        y_ref[0, 0, :, lo:hi] = y.astype(y_ref.dtype)


def _s5(u4, kstack, bm, cm, ap, dvec):
    nblk, bsz, nc, w = u4.shape
    sw = bm.shape[-1]
    pad = max(nc // 2, 8)
    return pl.pallas_call(
        _s5_body,
        grid=(nblk, bsz),
        in_specs=[pl.BlockSpec((1, 1, nc, w), lambda j, b: (j, b, 0, 0)),
                  pl.BlockSpec((1,) + kstack.shape[1:], lambda j, b: (j, 0, 0)),
                  pl.BlockSpec((1,) + bm.shape[1:], lambda j, b: (j, 0, 0)),
                  pl.BlockSpec((1,) + cm.shape[1:], lambda j, b: (j, 0, 0)),
                  pl.BlockSpec((1,) + ap.shape[1:], lambda j, b: (j, 0, 0)),
                  pl.BlockSpec((1, 1, w), lambda j, b: (j, 0, 0))],
        out_specs=pl.BlockSpec((1, 1, nc, w), lambda j, b: (j, b, 0, 0)),
        out_shape=jax.ShapeDtypeStruct((nblk, bsz, nc, w), BF16),
        scratch_shapes=[pltpu.VMEM((pad + nc, sw), F32)],
        compiler_params=_cparams("parallel", "parallel"),
        name="s5",
    )(u4, kstack, bm, cm, ap, dvec)


def _attn_body(q_ref, k_ref, v_ref, o_ref, vt_ref, km_ref, bias_ref, sa_ref, sb_ref):
    qi = pl.program_id(2)
    seq = k_ref.shape[1]
    tq = q_ref.shape[1]
    npair = q_ref.shape[2] // LANES
    nh = 2 * npair
    nb = seq // MOBA_BLOCK
    n_kv = seq // KV_TILE
    bpt = KV_TILE // MOBA_BLOCK
    blk_shift = MOBA_BLOCK.bit_length() - 1

    @pl.when(qi == 0)
    def _build():
        ones = jnp.where(lax.broadcasted_iota(I32, (PV_ROWS - HEAD_DIM, KV_TILE), 0) == 0, 1.0, 0.0)
        lane_b = lax.broadcasted_iota(I32, (nb, LANES), 1)
        for pr in range(npair):
            k = k_ref[0, :, pr * LANES:(pr + 1) * LANES].astype(F32)
            v = v_ref[0, :, pr * LANES:(pr + 1) * LANES].astype(F32)
            for t in range(n_kv):
                v_t = v[t * KV_TILE:(t + 1) * KV_TILE].T
                vt_ref[2 * pr, t] = jnp.concatenate([v_t[:HEAD_DIM], ones], axis=0).astype(BF16)
                vt_ref[2 * pr + 1, t] = jnp.concatenate([v_t[HEAD_DIM:], ones], axis=0).astype(BF16)
            km = jnp.mean(k.reshape(nb, MOBA_BLOCK, LANES), axis=1)
            km_ref[2 * pr] = jnp.where(lane_b < HEAD_DIM, km, 0.0)
            km_ref[2 * pr + 1] = jnp.where(lane_b >= HEAD_DIM, km, 0.0)

    feat = lax.broadcasted_iota(I32, (LANES, tq), 0)
    q_ts, q_h = [], []
    for pr in range(npair):
        q_t = q_ref[0, :, pr * LANES:(pr + 1) * LANES].astype(F32).T
        q_ts += [q_t, q_t]
        q_h += [jnp.where(feat < HEAD_DIM, q_t, 0.0).astype(BF16),
                jnp.where(feat >= HEAD_DIM, q_t, 0.0).astype(BF16)]

    jidx = lax.broadcasted_iota(I32, (nb, tq), 0)
    qblk = qi * (tq // MOBA_BLOCK) + lax.shift_right_logical(
        lax.broadcasted_iota(I32, (nb, tq), 1), blk_shift)
    for hx in range(nh):
        g = _dot(km_ref[hx], q_ts[hx])
        cnt = jnp.zeros((nb, tq), F32)
        for jp in range(nb):
            row = g[jp:jp + 1, :]
            beats = jnp.where(row > g, 1.0, jnp.where(row == g, jnp.where(jp < jidx, 1.0, 0.0), 0.0))
            cnt = cnt + jnp.where(jp < qblk, beats, 0.0)
        past_sel = jnp.where(jidx < qblk, jnp.where(cnt < MOBA_TOPK, 1.0, 0.0), 0.0)
        sel = jnp.where(jidx == qblk, 1.0, past_sel)
        bias_ref[hx] = jnp.where(sel > 0.5, 0.0, NEG_INF)

    def raw_scores(dst_ref, tile):
        off = pl.multiple_of(tile * KV_TILE, KV_TILE)
        for hx in range(nh):
            pr = hx // 2
            dst_ref[hx] = _dot(k_ref[0, pl.ds(off, KV_TILE), pr * LANES:(pr + 1) * LANES], q_h[hx])

    def fold(carry, src_ref, tile, causal=None):
        out = []
        for hx in range(nh):
            parts, biases = [], []
            for j in range(bpt):
                s_j = src_ref[hx, j * MOBA_BLOCK:(j + 1) * MOBA_BLOCK, :]
                if causal is not None:
                    s_j = jnp.where(causal[j], s_j, NEG_INF)
                b_j = bias_ref[hx, pl.ds(tile * bpt + j, 1), :]
                parts.append(s_j)
                biases.append(b_j)
            m_n = jnp.max(parts[0], axis=0, keepdims=True) + biases[0]
            for s_j, b_j in zip(parts[1:], biases[1:]):
                m_n = jnp.maximum(m_n, jnp.max(s_j, axis=0, keepdims=True) + b_j)
            if carry is None:
                p_t = jnp.concatenate([jnp.exp2(s_j + (b_j - m_n)) for s_j, b_j in zip(parts, biases)], axis=0)
                out += [m_n, _dot(vt_ref[hx, tile], p_t.astype(BF16))]
            else:
                m_c, acc_c = carry[2 * hx], carry[2 * hx + 1]
                m_n = jnp.maximum(m_c, m_n)
                p_t = jnp.concatenate([jnp.exp2(s_j + (b_j - m_n)) for s_j, b_j in zip(parts, biases)], axis=0)
                out += [m_n, acc_c * jnp.exp2(m_c - m_n) + _dot(vt_ref[hx, tile], p_t.astype(BF16))]
        return tuple(out)

    raw_scores(sa_ref, qi)
    raw_scores(sb_ref, 0)
    krow = lax.broadcasted_iota(I32, (MOBA_BLOCK, tq), 0)
    qcol = lax.broadcasted_iota(I32, (MOBA_BLOCK, tq), 1)
    state = fold(None, sa_ref, qi, causal=[krow + j * MOBA_BLOCK <= qcol for j in range(bpt)])

    def body(i, carry):
        t0 = 2 * i
        raw_scores(sa_ref, t0 + 1)
        carry = fold(carry, sb_ref, t0)
        raw_scores(sb_ref, jnp.minimum(t0 + 2, n_kv - 1))
        return fold(carry, sa_ref, t0 + 1)

    state = lax.fori_loop(0, qi // 2, body, state)
    state = lax.cond(lax.rem(qi, 2) == 1, lambda c: fold(c, sb_ref, qi - 1), lambda c: c, state)
    for pr in range(npair):
        accs = (state[4 * pr + 1], state[4 * pr + 3])
        o_t = jnp.concatenate([acc[:HEAD_DIM] / acc[HEAD_DIM:HEAD_DIM + 1] for acc in accs], axis=0)
        o_ref[0, :, pr * LANES:(pr + 1) * LANES] = o_t.T.astype(o_ref.dtype)


def _attention(q, k, v):
    bsz, seq, aw = q.shape
    pps = ATTN_PAIRS_PER_STEP
    wl = pps * LANES
    npair = aw // wl
    tq = KV_TILE
    nb = seq // MOBA_BLOCK
    assert seq % KV_TILE == 0 and nb % 8 == 0
    return pl.pallas_call(
        _attn_body,
        grid=(bsz, npair, seq // tq),
        in_specs=[pl.BlockSpec((1, tq, wl), lambda b, p, i: (b, i, p)),
                  pl.BlockSpec((1, seq, wl), lambda b, p, i: (b, 0, p)),
                  pl.BlockSpec((1, seq, wl), lambda b, p, i: (b, 0, p))],
        out_specs=pl.BlockSpec((1, tq, wl), lambda b, p, i: (b, i, p)),
        out_shape=jax.ShapeDtypeStruct((bsz, seq, aw), BF16),
        scratch_shapes=[pltpu.VMEM((2 * pps, seq // KV_TILE, PV_ROWS, KV_TILE), BF16),
                        pltpu.VMEM((2 * pps, nb, LANES), F32),
                        pltpu.VMEM((2 * pps, nb, tq), F32),
                        pltpu.VMEM((2 * pps, KV_TILE, tq), F32),
                        pltpu.VMEM((2 * pps, KV_TILE, tq), F32)],
        compiler_params=_cparams("parallel", "parallel", "arbitrary"),
        name="attn",
    )(q, k, v)


def _merge_body(y_ref, a_ref, x_ref, wglu_ref, bglu_ref, gs_ref, ga_ref, wout_ref, gm_ref,
                gf_ref, shf_ref, scf_ref, rw_ref, rb_ref, x1_ref, h2_ref, idx_ref, wt_ref, cnt_ref, ys_ref):
    nblk = y_ref.shape[0]
    tm = x_ref.shape[1]
    for j in range(nblk):
        for t in range(SSM_CHUNK):
            ys_ref[j, pl.ds(t, tm // SSM_CHUNK, stride=SSM_CHUNK), :] = (
                y_ref[j, 0, :, t * LANES:(t + 1) * LANES].astype(F32))
    y = jnp.concatenate([ys_ref[j] for j in range(nblk)], axis=1)
    g = 0.5 * y * (1.0 + lax.erf(y * (2.0 ** -0.5)))
    glu = g * jax.nn.sigmoid(_dot(g.astype(BF16), wglu_ref[...]) + bglu_ref[...])
    ssm_n = _rms(glu, gs_ref[...])
    att_n = _rms(a_ref[0].astype(F32), ga_ref[...])
    merged = jnp.concatenate([ssm_n, att_n], axis=1).astype(BF16)
    mix = _dot(merged, wout_ref[...])
    x1 = x_ref[0] + gm_ref[0] * mix
    x1_ref[0] = x1
    h2 = _rms(x1, gf_ref[...]) * (1.0 + scf_ref[0]) + shf_ref[0]
    h2_ref[0] = h2
    logits = lax.dot_general(rw_ref[...], h2, (((1,), (1,)), ((), ())),
                             preferred_element_type=F32, precision=HIGHEST) + rb_ref[...]
    ne = logits.shape[0]
    sub = lax.broadcasted_iota(I32, (ne, tm), 0)
    vals, idxs = [], []
    for _ in range(TOP_K):
        mx = jnp.max(logits, axis=0, keepdims=True)
        ix = jnp.min(jnp.where(logits == mx, sub, ne), axis=0, keepdims=True)
        vals.append(mx)
        idxs.append(ix)
        logits = jnp.where(sub == ix, -jnp.inf, logits)
    exps = [jnp.exp(vv - vals[0]) for vv in vals]
    tot = exps[0]
    for e in exps[1:]:
        tot = tot + e
    row_i = lax.broadcasted_iota(I32, idx_ref.shape, 0)
    row_w = lax.broadcasted_iota(I32, wt_ref.shape, 0)
    idx_out = jnp.zeros(idx_ref.shape, I32)
    wt_out = jnp.zeros(wt_ref.shape, F32)
    for kk in range(TOP_K):
        idx_out = jnp.where(row_i == kk, idxs[kk], idx_out)
        wt_out = jnp.where(row_w == kk, exps[kk] / tot, wt_out)
    idx_ref[...] = idx_out
    wt_ref[...] = wt_out
    chosen = jnp.zeros((ne, tm), F32)
    for kk in range(TOP_K):
        chosen = chosen + jnp.where(sub == idxs[kk], 1.0, 0.0)
    cnt_ref[0] = jnp.sum(chosen, axis=1, keepdims=True).astype(I32)


def _merge(y4, attn, x, w_glu_bf, b_glu, g_ssm, g_attn, w_out_bf, mod3, g_ffn, router_w, router_b, tm=512):
    bsz, seq, d = x.shape
    nblk = y4.shape[0]
    ssm_w = nblk * LANES
    aw = attn.shape[-1]
    ne = router_w.shape[1]
    tm = min(tm, seq)
    spb = seq // tm
    n_tok = bsz * seq
    rw_t = router_w.T
    row = lambda n: pl.BlockSpec((1, n), lambda b, i: (0, 0))
    full = lambda a: pl.BlockSpec(a.shape, lambda b, i: (0,) * a.ndim)
    modv = lambda j: pl.BlockSpec((1, 1, d), lambda b, i: (b, 0, j))
    return pl.pallas_call(
        _merge_body,
        grid=(bsz, seq // tm),
        in_specs=[pl.BlockSpec((nblk, 1, tm // SSM_CHUNK, SSM_CHUNK * LANES), lambda b, i: (0, b, i, 0)),
                  pl.BlockSpec((1, tm, aw), lambda b, i: (b, i, 0)),
                  pl.BlockSpec((1, tm, d), lambda b, i: (b, i, 0)),
                  full(w_glu_bf), row(ssm_w), row(ssm_w), row(aw), full(w_out_bf),
                  modv(2), row(d), modv(3), modv(4), full(rw_t),
                  pl.BlockSpec((ne, 1), lambda b, i: (0, 0))],
        out_specs=[pl.BlockSpec((1, tm, d), lambda b, i: (b, i, 0)),
                   pl.BlockSpec((1, tm, d), lambda b, i: (b, i, 0)),
                   pl.BlockSpec((TOP_K, tm), lambda b, i: (0, b * spb + i)),
                   pl.BlockSpec((WT_ROWS, tm), lambda b, i: (0, b * spb + i)),
                   pl.BlockSpec((1, ne, 1), lambda b, i: (b * spb + i, 0, 0))],
        out_shape=[jax.ShapeDtypeStruct((bsz, seq, d), F32),
                   jax.ShapeDtypeStruct((bsz, seq, d), F32),
                   jax.ShapeDtypeStruct((TOP_K, n_tok), I32),
                   jax.ShapeDtypeStruct((WT_ROWS, n_tok), F32),
                   jax.ShapeDtypeStruct((bsz * spb, ne, 1), I32)],
        scratch_shapes=[pltpu.VMEM((nblk, tm, LANES), F32)],
        compiler_params=_cparams("parallel", "parallel"),
        name="merge",
    )(y4, attn, x, w_glu_bf, b_glu.reshape(1, ssm_w), g_ssm.reshape(1, ssm_w), g_attn.reshape(1, aw),
      w_out_bf, mod3, g_ffn.reshape(1, d), mod3, mod3, rw_t, router_b.reshape(ne, 1))


def _segment_copies(n_seg, meta, make_copy):
    def body(e, _):
        src0, dst0, units = meta(e)
        for sz in SEG_UNITS:
            off = units & ~(2 * sz - 1)

            @pl.when((units & sz) != 0)
            def _(off=off, sz=sz):
                make_copy(src0 + off, dst0 + off, sz).start()
        return 0

    lax.fori_loop(0, n_seg, body, 0)


def _local_slots(idx, lstart_col, ne):
    tb = idx.shape[1]
    sub = lax.broadcasted_iota(I32, (ne, tb), 0)
    hits = [idx[kk:kk + 1, :] == sub for kk in range(TOP_K)]
    onehot = jnp.zeros((ne, tb), F32)
    for h in hits:
        onehot = onehot + jnp.where(h, 1.0, 0.0)
    r = lax.broadcasted_iota(I32, (tb, tb), 0)
    c = lax.broadcasted_iota(I32, (tb, tb), 1)
    tri = jnp.where(r < c, 1.0, 0.0).astype(BF16)
    slot = _dot(onehot.astype(BF16), tri) + lstart_col
    return [jnp.sum(jnp.where(h, slot, 0.0), axis=0, keepdims=True) for h in hits]


def _dispatch_body(lstart_s, pcnt_s, base_s, tail_dst_s, tail_units_s, tail_total_s,
                   idx_ref, lcol_ref, h_ref, xs_ref, ls_ref, sorted_ref, zero_ref, sem, *, ne):
    i = pl.program_id(0)
    n_steps = pl.num_programs(0)
    nsub = lcol_ref.shape[0]
    tb = h_ref.shape[0] // nsub
    lb = sorted_ref.shape[1] * SEG_ALIGN

    @pl.when(i == 0)
    def _():
        zero_ref[...] = jnp.zeros(zero_ref.shape, F32)
        _segment_copies(
            ne, lambda e: (0, tail_dst_s[e] // SEG_ALIGN, tail_units_s[e]),
            lambda src, dst, n: pltpu.make_async_copy(zero_ref.at[pl.ds(src, n)], xs_ref.at[pl.ds(dst, n)],
                                                      sem.at[nsub]))

    def tile_units(t):
        return (lstart_s[t * ne + ne - 1] + pcnt_s[t * ne + ne - 1]) // SEG_ALIGN

    def wait_tile(t, sl):
        done = xs_ref.at[pl.ds(0, tile_units(t))]
        pltpu.make_async_copy(done, done, sem.at[sl]).wait()

    for sl in range(nsub):
        t = i * nsub + sl
        cols = slice(sl * tb, (sl + 1) * tb)

        @pl.when(i >= 1)
        def _(t=t, sl=sl):
            wait_tile(t - nsub, sl)

        slots = _local_slots(idx_ref[:, cols], lcol_ref[sl].astype(F32), ne)
        row_o = lax.broadcasted_iota(I32, (ls_ref.shape[0], tb), 0)
        ls_out = jnp.full((ls_ref.shape[0], tb), -1.0, F32)
        for kk in range(TOP_K):
            ls_out = jnp.where(row_o == kk, slots[kk], ls_out)
        ls_ref[:, cols] = ls_out.astype(I32)
        h = h_ref[cols, :].astype(BF16)
        for r0 in range(0, lb, SORT_CHUNK):
            j = (lax.broadcasted_iota(I32, (SORT_CHUNK, tb), 0) + r0).astype(F32)
            pm = jnp.zeros((SORT_CHUNK, tb), F32)
            for kk in range(TOP_K):
                pm = jnp.where(j == slots[kk], 1.0, pm)
            sorted_ref[sl, r0 // SEG_ALIGN:(r0 + SORT_CHUNK) // SEG_ALIGN] = _dot(pm.astype(BF16), h).reshape(
                SORT_CHUNK // SEG_ALIGN, SEG_ALIGN, h.shape[1])

        def meta(e, t=t):
            return (lstart_s[t * ne + e] // SEG_ALIGN, base_s[t * ne + e] // SEG_ALIGN,
                    pcnt_s[t * ne + e] // SEG_ALIGN)

        def make_copy(src, dst, n, sl=sl):
            return pltpu.make_async_copy(sorted_ref.at[sl, pl.ds(src, n)], xs_ref.at[pl.ds(dst, n)], sem.at[sl])

        _segment_copies(ne, meta, make_copy)

    @pl.when(i == n_steps - 1)
    def _():
        for sl in range(nsub):
            wait_tile(i * nsub + sl, sl)

        @pl.when(tail_total_s[0] > 0)
        def _():
            filled = xs_ref.at[pl.ds(0, tail_total_s[0])]
            pltpu.make_async_copy(filled, filled, sem.at[nsub]).wait()


def _dispatch(h2, idx_t, plan, tail, cap, tb, tg):
    n, d = h2.shape
    lstart, pcnt, base = plan
    nt, ne = lstart.shape
    nsub = TILES_PER_STEP if nt % TILES_PER_STEP == 0 else 1
    lb = tb * TOP_K + ne * SEG_ALIGN
    assert lb % SORT_CHUNK == 0
    return pl.pallas_call(
        functools.partial(_dispatch_body, ne=ne),
        grid_spec=pltpu.PrefetchScalarGridSpec(
            num_scalar_prefetch=6,
            grid=(nt // nsub,),
            in_specs=[pl.BlockSpec((TOP_K, nsub * tb), lambda i, *_: (0, i)),
                      pl.BlockSpec((nsub, ne, 1), lambda i, *_: (i, 0, 0)),
                      pl.BlockSpec((nsub * tb, d), lambda i, *_: (i, 0))],
            out_specs=[pl.BlockSpec(memory_space=pl.ANY),
                       pl.BlockSpec((WT_ROWS, nsub * tb), lambda i, *_: (0, i))],
            scratch_shapes=[pltpu.VMEM((nsub, lb // SEG_ALIGN, SEG_ALIGN, d), F32),
                            pltpu.VMEM((tg // SEG_ALIGN, SEG_ALIGN, d), F32),
                            pltpu.SemaphoreType.DMA((nsub + 1,))],
        ),
        out_shape=[jax.ShapeDtypeStruct((cap // SEG_ALIGN, SEG_ALIGN, d), F32),
                   jax.ShapeDtypeStruct((WT_ROWS, n), I32)],
        compiler_params=_cparams("arbitrary"),
        name="dispatch",
    )(lstart.reshape(-1), pcnt.reshape(-1), base.reshape(-1), *tail, idx_t, lstart.reshape(nt, ne, 1), h2)


def _route_plan(tile_cnt, tg, cap):
    cnt = tile_cnt[:, :, 0]
    ne = cnt.shape[1]
    pcnt = (cnt + SEG_ALIGN - 1) // SEG_ALIGN * SEG_ALIGN
    lstart = jnp.cumsum(pcnt, axis=1) - pcnt
    e_rows = jnp.sum(pcnt, axis=0)
    e_tiles = (e_rows + tg - 1) // tg
    e_end = jnp.cumsum(e_tiles)
    base = ((e_end - e_tiles) * tg)[None, :] + jnp.cumsum(pcnt, axis=0) - pcnt
    ii = jnp.arange(cap // tg, dtype=I32)
    used = ii < e_end[-1]
    tile = jnp.minimum(ii, e_end[-1] - 1)
    e_of = jnp.minimum(jnp.sum((e_end[None, :] <= tile[:, None]).astype(I32), axis=1), ne - 1)
    newe = (e_of != jnp.concatenate([jnp.full((1,), -1, I32), e_of[:-1]])).astype(I32)
    items = (tile.astype(I32), e_of.astype(I32), used.astype(I32), newe)
    tail = (e_end * tg - e_tiles * tg + e_rows, (e_tiles * tg - e_rows) // SEG_ALIGN)
    tail = (tail[0].astype(I32), tail[1].astype(I32), jnp.sum(tail[1]).astype(I32).reshape(1))
    return (lstart.astype(I32), pcnt.astype(I32), base.astype(I32)), tail, items


def _experts_body(tile_ref, exp_ref, used_ref, newe_ref,
                  xs_ref, wg_ref, bg_ref, wu_ref, bu_ref, wd_ref, bd_ref, ys_ref,
                  wg_bf, wu_bf, wd_bf):
    i = pl.program_id(0)

    @pl.when(newe_ref[i] == 1)
    def _cast():
        wg_bf[...] = wg_ref[0].astype(BF16)
        wu_bf[...] = wu_ref[0].astype(BF16)
        wd_bf[...] = wd_ref[0].astype(BF16)

    @pl.when(used_ref[i] == 1)
    def _compute():
        x = xs_ref[...].astype(BF16)
        gate = jnp.minimum(_dot(x, wg_bf[...]) + bg_ref[0], SWIGLU_LIMIT)
        lin = jnp.clip(_dot(x, wu_bf[...]) + bu_ref[0], -SWIGLU_LIMIT, SWIGLU_LIMIT)
        act = gate * jax.nn.sigmoid(SWIGLU_ALPHA * gate) * (lin + 1.0)
        ys_ref[...] = _dot(act.astype(BF16), wd_bf[...]) + bd_ref[0]


def _experts(xs, items, w_gate, b_gate, w_up, b_up, w_down, b_down, tg):
    nk, d = xs.shape
    ne, _, dff = w_gate.shape
    n_items = items[0].shape[0]
    wspec = lambda shp: pl.BlockSpec((1,) + shp, lambda i, t, e, u, nw: (e[i], 0, 0))
    return pl.pallas_call(
        _experts_body,
        grid_spec=pltpu.PrefetchScalarGridSpec(
            num_scalar_prefetch=4,
            grid=(n_items,),
            in_specs=[pl.BlockSpec((tg, d), lambda i, t, e, u, nw: (t[i], 0)),
                      wspec((d, dff)), wspec((1, dff)),
                      wspec((d, dff)), wspec((1, dff)),
                      wspec((dff, d)), wspec((1, d))],
            out_specs=pl.BlockSpec((tg, d), lambda i, t, e, u, nw: (t[i], 0)),
            scratch_shapes=[pltpu.VMEM((d, dff), BF16), pltpu.VMEM((d, dff), BF16), pltpu.VMEM((dff, d), BF16)],
        ),
        out_shape=jax.ShapeDtypeStruct((nk, d), F32),
        compiler_params=_cparams("arbitrary"),
        name="experts",
    )(*items, xs, w_gate, b_gate.reshape(ne, 1, dff), w_up, b_up.reshape(ne, 1, dff),
      w_down, b_down.reshape(ne, 1, d))


def _combine_body(lstart_s, pcnt_s, base_s, ls_ref, wt_ref, x1_ref, gf_ref, gfin_ref, sho_ref, sco_ref,
                  ys_ref, o_ref, buf, sem, *, ne, spb):
    i = pl.program_id(0) * spb + pl.program_id(1)
    nsub = buf.shape[0]
    nt = pl.num_programs(0) * spb * nsub
    tb = x1_ref.shape[1] // nsub
    lb = buf.shape[1] * SEG_ALIGN

    def fetch(t, sl):
        def meta(e):
            return (base_s[t * ne + e] // SEG_ALIGN, lstart_s[t * ne + e] // SEG_ALIGN,
                    pcnt_s[t * ne + e] // SEG_ALIGN)

        def make_copy(src, dst, n):
            return pltpu.make_async_copy(ys_ref.at[pl.ds(src, n)], buf.at[sl, pl.ds(dst, n)], sem.at[sl])

        _segment_copies(ne, meta, make_copy)

    @pl.when(i == 0)
    def _():
        buf[...] = jnp.zeros(buf.shape, F32)
        fetch(0, 0)

    for sl in range(nsub):
        t = i * nsub + sl
        rows = slice(sl * tb, (sl + 1) * tb)

        @pl.when(t + 1 < nt)
        def _(t=t, sl=sl):
            fetch(t + 1, (sl + 1) % nsub)

        total = (lstart_s[t * ne + ne - 1] + pcnt_s[t * ne + ne - 1]) // SEG_ALIGN
        done = buf.at[sl, pl.ds(0, total)]
        pltpu.make_async_copy(done, done, sem.at[sl]).wait()
        ls_col = ls_ref[:, rows].astype(F32).T
        wt_col = wt_ref[:, rows].T
        ffn = jnp.zeros((tb, x1_ref.shape[2]), F32)
        for r0 in range(0, lb, SORT_CHUNK):
            j = (lax.broadcasted_iota(I32, (tb, SORT_CHUNK), 1) + r0).astype(F32)
            wm = jnp.zeros((tb, SORT_CHUNK), F32)
            for kk in range(TOP_K):
                wm = jnp.where(j == ls_col[:, kk:kk + 1], wt_col[:, kk:kk + 1], wm)
            y_rows = buf[sl, r0 // SEG_ALIGN:(r0 + SORT_CHUNK) // SEG_ALIGN].reshape(SORT_CHUNK, x1_ref.shape[2])
            ffn = ffn + _dot(wm.astype(BF16), y_rows.astype(BF16))
        x2 = x1_ref[0, rows, :] + gf_ref[0] * ffn
        o_ref[0, rows, :] = _rms(x2, gfin_ref[...]) * (1.0 + sco_ref[0]) + sho_ref[0]


def _combine(plan, ls_t, wt_t, x1, mod3, g_final, fmod3, ys, tb):
    bsz, seq, d = x1.shape
    lstart, pcnt, base = plan
    nt, ne = lstart.shape
    nsub = TILES_PER_STEP if (seq // tb) % TILES_PER_STEP == 0 else 1
    ts = nsub * tb
    spb = seq // ts
    lb = tb * TOP_K + ne * SEG_ALIGN
    return pl.pallas_call(
        functools.partial(_combine_body, ne=ne, spb=spb),
        grid_spec=pltpu.PrefetchScalarGridSpec(
            num_scalar_prefetch=3,
            grid=(bsz, spb),
            in_specs=[pl.BlockSpec((WT_ROWS, ts), lambda b, i, *_: (0, b * spb + i)),
                      pl.BlockSpec((WT_ROWS, ts), lambda b, i, *_: (0, b * spb + i)),
                      pl.BlockSpec((1, ts, d), lambda b, i, *_: (b, i, 0)),
                      pl.BlockSpec((1, 1, d), lambda b, i, *_: (b, 0, 5)),
                      pl.BlockSpec((1, d), lambda b, i, *_: (0, 0)),
                      pl.BlockSpec((1, 1, d), lambda b, i, *_: (b, 0, 0)),
                      pl.BlockSpec((1, 1, d), lambda b, i, *_: (b, 0, 1)),
                      pl.BlockSpec(memory_space=pl.ANY)],
            out_specs=pl.BlockSpec((1, ts, d), lambda b, i, *_: (b, i, 0)),
            scratch_shapes=[pltpu.VMEM((nsub, lb // SEG_ALIGN, SEG_ALIGN, d), F32),
                            pltpu.SemaphoreType.DMA((nsub,))],
        ),
        out_shape=jax.ShapeDtypeStruct((bsz, seq, d), F32),
        compiler_params=_cparams("arbitrary", "arbitrary"),
        name="combine",
    )(lstart.reshape(-1), pcnt.reshape(-1), base.reshape(-1), ls_t, wt_t, x1, mod3, g_final.reshape(1, d),
      fmod3, fmod3, ys.reshape(-1, SEG_ALIGN, d))


def kernel(x, c, positions, ada_w, ada_b, final_ada_w, final_ada_b, norm_mix_g, norm_ffn_g, norm_final_g, w_in, ssm_lambda_re, ssm_lambda_im, ssm_log_dt, ssm_b_re, ssm_b_im, ssm_c_re, ssm_c_im, ssm_d, ssm_w_glu, ssm_b_glu, out_norm_ssm_g, out_norm_attn_g, w_out, router_w, router_b, exp_w_gate, exp_b_gate, exp_w_up, exp_b_up, exp_w_down, exp_b_down):
    bsz, seq, d = x.shape
    depth = ada_w.shape[0]
    ssm_w = ssm_d.shape[-1]
    attn_w = (w_in.shape[-1] - ssm_w) // 3
    ne = router_w.shape[-1]
    n_tok = bsz * seq
    nc = seq // SSM_CHUNK
    tg = 512

    rope_tab, rope_exp = _rope_tables(positions)
    fmod3 = _adaln(c, final_ada_w, final_ada_b).reshape(bsz, 1, 2 * d)
    for l in range(depth):
        mod3 = _adaln(c, ada_w[l], ada_b[l]).reshape(bsz, 1, -1)
        u4, q, k, v = _inproj(x, norm_mix_g[l], mod3, w_in[l].astype(BF16), rope_tab, rope_exp, ssm_w, attn_w)
        tables = _s5_params(ssm_lambda_re[l], ssm_lambda_im[l], ssm_log_dt[l], ssm_b_re[l], ssm_b_im[l],
                            ssm_c_re[l], ssm_c_im[l], ssm_d[l], nc)
        y4 = _s5(u4, *tables)
        attn = _attention(q, k, v)
        x1, h2, idx_t, wt_t, tile_cnt = _merge(y4, attn, x, ssm_w_glu[l].astype(BF16), ssm_b_glu[l], out_norm_ssm_g[l],
                                     out_norm_attn_g[l], w_out[l].astype(BF16), mod3, norm_ffn_g[l],
                                     router_w[l], router_b[l])
        tb = seq // (tile_cnt.shape[0] // bsz)
        cap = n_tok * TOP_K + tile_cnt.shape[0] * ne * SEG_ALIGN + ne * tg
        assert cap % tg == 0 and max(tb, tg) // SEG_ALIGN < 2 * SEG_UNITS[0]
        plan, tail, items = _route_plan(tile_cnt, tg, cap)
        xs, ls_t = _dispatch(h2.reshape(n_tok, d), idx_t, plan, tail, cap, tb, tg)
        ys = _experts(xs.reshape(cap, d), items, exp_w_gate[l], exp_b_gate[l], exp_w_up[l], exp_b_up[l],
                      exp_w_down[l], exp_b_down[l], tg)
        if l + 1 < depth:
            raise NotImplementedError("depth > 1 needs the non-final combine")
        x = _combine(plan, ls_t, wt_t, x1, mod3, norm_final_g, fmod3, ys, tb)
    return x
```

```python
import functools
import math

import jax
import jax.numpy as jnp
from jax import lax
from jax.experimental import pallas as pl
from jax.experimental.pallas import tpu as pltpu

F32 = jnp.float32
BF16 = jnp.bfloat16
I32 = jnp.int32
HIGHEST = lax.Precision.HIGHEST

LANES = 128
HEAD_DIM = 64
MOBA_BLOCK = 256
MOBA_TOPK = 3
ROT_DIM = HEAD_DIM // 4
ROPE_THETA = 500000.0
SSM_CHUNK = 16
TOP_K = 4
SWIGLU_ALPHA = 1.702
SWIGLU_LIMIT = 7.0
NORM_EPS = 1e-5
NEG_INF = -1e30
LOG2_E = math.log2(math.e)
KV_TILE = 2 * MOBA_BLOCK
PV_ROWS = HEAD_DIM + 8
ATTN_PAIRS_PER_STEP = 4
WT_ROWS = 8
SEG_ALIGN = 8
SEG_UNITS = (64, 32, 16, 8, 4, 2, 1)
SORT_CHUNK = 256
TILES_PER_STEP = 2
VMEM_LIMIT = 56 * 1024 * 1024


def _cparams(*sem):
    return pltpu.CompilerParams(dimension_semantics=sem, vmem_limit_bytes=VMEM_LIMIT)


def _dot(a, b):
    return jnp.dot(a, b, preferred_element_type=F32)


def _rms(x, g):
    return x * lax.rsqrt(jnp.mean(x * x, axis=-1, keepdims=True) + NORM_EPS) * g


def _adaln_body(c_ref, w_ref, b_ref, o_ref):
    c = c_ref[...]
    ca = c * jax.nn.sigmoid(c)
    o_ref[...] = jnp.dot(ca, w_ref[...], preferred_element_type=F32, precision=HIGHEST) + b_ref[...]


def _adaln(c, w, b, tn=512):
    bsz, d = c.shape
    n = w.shape[1]
    return pl.pallas_call(
        _adaln_body,
        grid=(n // tn,),
        in_specs=[pl.BlockSpec((bsz, d), lambda j: (0, 0)),
                  pl.BlockSpec((d, tn), lambda j: (0, j)),
                  pl.BlockSpec((1, tn), lambda j: (0, j))],
        out_specs=pl.BlockSpec((bsz, tn), lambda j: (0, j)),
        out_shape=jax.ShapeDtypeStruct((bsz, n), F32),
        compiler_params=_cparams("parallel"),
        name="adaln",
    )(c, w, b.reshape(1, n))


def _inproj_body(x_ref, g_ref, sh_ref, sc_ref, w_ref, tab_ref, exp_ref, u_ref, q_ref, k_ref, v_ref, us_ref):
    x = x_ref[0]
    tm = x.shape[0]
    h = _rms(x, g_ref[...]) * (1.0 + sc_ref[0]) + sh_ref[0]
    proj = _dot(h.astype(BF16), w_ref[...])
    ssm_w = u_ref.shape[0] * LANES
    aw = q_ref.shape[2]
    for j in range(u_ref.shape[0]):
        us_ref[j] = proj[:, j * LANES:(j + 1) * LANES]
        for t in range(SSM_CHUNK):
            u_ref[j, 0, :, t * LANES:(t + 1) * LANES] = us_ref[
                j, pl.ds(t, tm // SSM_CHUNK, stride=SSM_CHUNK), :].astype(BF16)
    t_hi = tab_ref[0].astype(BF16)
    t_lo = (tab_ref[0] - t_hi.astype(F32)).astype(BF16)
    tn = (((0,), (0,)), ((), ()))
    tab = (lax.dot_general(t_hi, exp_ref[...], tn, preferred_element_type=F32)
           + lax.dot_general(t_lo, exp_ref[...], tn, preferred_element_type=F32))
    lane1 = lax.broadcasted_iota(I32, (tm, LANES), 1)
    cos1 = tab[:, :LANES] + jnp.where((lane1 & (HEAD_DIM - 1)) >= ROT_DIM, 1.0, 0.0)
    reps = aw // LANES
    cosf = jnp.concatenate([cos1] * reps, axis=1)
    sinf = jnp.concatenate([tab[:, LANES:]] * reps, axis=1)
    lane = lax.broadcasted_iota(I32, (tm, aw), 1)
    first_half = (lane & (HEAD_DIM - 1)) < (ROT_DIM // 2)

    def rope(t):
        partner = jnp.where(first_half, pltpu.roll(t, aw - ROT_DIM // 2, 1), pltpu.roll(t, ROT_DIM // 2, 1))
        return t * cosf + partner * sinf

    q = rope(proj[:, ssm_w:ssm_w + aw]) * (HEAD_DIM ** -0.5 * LOG2_E)
    k = rope(proj[:, ssm_w + aw:ssm_w + 2 * aw])
    q_ref[0] = q.astype(BF16)
    k_ref[0] = k.astype(BF16)
    v_ref[0] = proj[:, ssm_w + 2 * aw:ssm_w + 3 * aw].astype(BF16)


def _inproj(x, gain, mod3, w_in_bf, rope_tab, rope_exp, ssm_w, attn_w, tm=1024):
    bsz, seq, d = x.shape
    n_u = ssm_w // LANES
    tm = min(tm, seq)
    return pl.pallas_call(
        _inproj_body,
        grid=(bsz, seq // tm),
        in_specs=[pl.BlockSpec((1, tm, d), lambda b, i: (b, i, 0)),
                  pl.BlockSpec((1, d), lambda b, i: (0, 0)),
                  pl.BlockSpec((1, 1, d), lambda b, i: (b, 0, 0)),
                  pl.BlockSpec((1, 1, d), lambda b, i: (b, 0, 1)),
                  pl.BlockSpec(w_in_bf.shape, lambda b, i: (0, 0)),
                  pl.BlockSpec((1, ROT_DIM, tm), lambda b, i: (b, 0, i)),
                  pl.BlockSpec(rope_exp.shape, lambda b, i: (0, 0))],
        out_specs=[pl.BlockSpec((n_u, 1, tm // SSM_CHUNK, SSM_CHUNK * LANES), lambda b, i: (0, b, i, 0)),
                   pl.BlockSpec((1, tm, attn_w), lambda b, i: (b, i, 0)),
                   pl.BlockSpec((1, tm, attn_w), lambda b, i: (b, i, 0)),
                   pl.BlockSpec((1, tm, attn_w), lambda b, i: (b, i, 0))],
        out_shape=[jax.ShapeDtypeStruct((n_u, bsz, seq // SSM_CHUNK, SSM_CHUNK * LANES), BF16),
                   jax.ShapeDtypeStruct((bsz, seq, attn_w), BF16),
                   jax.ShapeDtypeStruct((bsz, seq, attn_w), BF16),
                   jax.ShapeDtypeStruct((bsz, seq, attn_w), BF16)],
        scratch_shapes=[pltpu.VMEM((n_u, tm, LANES), F32)],
        compiler_params=_cparams("parallel", "parallel"),
        name="inproj",
    )(x, gain.reshape(1, d), mod3, mod3, w_in_bf, rope_tab, rope_exp)


def _rope_tables(positions):
    half = ROT_DIM // 2
    inv_freq = ROPE_THETA ** (-jnp.arange(0, ROT_DIM, 2, dtype=F32) / ROT_DIM)
    ang = positions.astype(F32)[:, None, :] * inv_freq[None, :, None]
    tab = jnp.concatenate([jnp.cos(ang), jnp.sin(ang)], axis=1)
    i = jnp.arange(ROT_DIM)[:, None]
    hl = jnp.arange(LANES)[None, :] % HEAD_DIM
    e_cos = jnp.where((i < half) & ((hl == i) | (hl == i + half)), 1.0, 0.0)
    e_sin = jnp.where(i >= half, jnp.where(hl == i - half, -1.0, jnp.where(hl == i, 1.0, 0.0)), 0.0)
    return tab, jnp.concatenate([e_cos, e_sin], axis=1).astype(BF16)


def _s5_params(lam_re, lam_im, log_dt, b_re, b_im, c_re, c_im, d_skip, n_chunks):
    g_all, p = lam_re.shape
    hc = b_re.shape[-1]
    t = SSM_CHUNK
    gpb = LANES // hc
    nblk = g_all // gpb
    lr, li = lam_re.astype(F32), lam_im.astype(F32)
    dt = jnp.exp(log_dt.astype(F32))[:, None]
    ldr, ldi = lr * dt, li * dt

    def apow(k):
        k = jnp.asarray(k, F32)[..., None, None]
        mag = jnp.exp(ldr * k)
        return mag * jnp.cos(ldi * k), mag * jnp.sin(ldi * k)

    ar, ai = apow(1.0)
    zr, zi = ar - 1.0, ai
    den = lr * lr + li * li
    cr = (zr * lr + zi * li) / den
    ci = (zi * lr - zr * li) / den
    bre, bim = b_re.astype(F32), b_im.astype(F32)
    bbr = cr[..., None] * bre - ci[..., None] * bim
    bbi = cr[..., None] * bim + ci[..., None] * bre
    cre, cim = c_re.astype(F32), c_im.astype(F32)
    def block_diag(a):
        w = a.shape[-1]
        wide = jnp.concatenate([a] * gpb, axis=-1)
        grp = jnp.arange(gpb)[:, None, None]
        lane_grp = (jnp.arange(gpb * w) // w)[None, None, :]
        wide = jnp.where(lane_grp == grp, wide, 0.0)
        return wide.reshape(a.shape[:-3] + (gpb * a.shape[-2], gpb * w))

    pr, pi = apow(jnp.arange(t))
    abr = pr[..., None] * bbr - pi[..., None] * bbi
    abi = pr[..., None] * bbi + pi[..., None] * bbr
    kf = (jnp.einsum('gnp,kgph->kghn', cre, abr, precision=HIGHEST)
          - jnp.einsum('gnp,kgph->kghn', cim, abi, precision=HIGHEST))
    kblk = block_diag(kf.reshape(t, nblk, gpb, hc, hc))
    kpad = jnp.concatenate([jnp.zeros_like(kblk[:1]), kblk], axis=0)
    pm = jnp.concatenate([kpad[0:t], kpad[1:t + 1]], axis=-1)
    kstack = jnp.transpose(pm[::-1], (1, 0, 2, 3)).reshape(nblk, t * LANES, 2 * LANES)

    def inject(ab):
        ab = jnp.transpose(ab[::-1].reshape(t, nblk, gpb, p, hc), (1, 0, 2, 4, 3))
        return block_diag(ab).reshape(nblk, t * LANES, gpb * p)

    bm = jnp.concatenate([inject(abr), inject(abi)], axis=-1)

    qr, qi = apow(jnp.arange(t) + 1.0)
    c_from_re = cre[None] * qr[:, :, None, :] - cim[None] * qi[:, :, None, :]
    c_from_im = -cre[None] * qi[:, :, None, :] - cim[None] * qr[:, :, None, :]

    def readout_t(cc):
        cc = jnp.transpose(cc.reshape(t, nblk, gpb, hc, p), (1, 0, 2, 3, 4))
        return block_diag(cc).reshape(nblk, t * LANES, gpb * p)

    cm = jnp.swapaxes(jnp.concatenate([readout_t(c_from_re), readout_t(c_from_im)], axis=-1), 1, 2)

    n_steps = max(1, int(math.log2(n_chunks)))
    shifts = [float(t * (1 << s)) for s in range(n_steps)]
    sr, si = apow(jnp.asarray(shifts))
    ap = jnp.concatenate([sr.reshape(n_steps, nblk, gpb * p), si.reshape(n_steps, nblk, gpb * p)], axis=-1)
    ap = jnp.transpose(ap, (1, 0, 2))
    dvec = jnp.tile(d_skip.astype(F32).reshape(nblk, 1, LANES), (1, 1, t))
    return kstack.astype(BF16), bm.astype(BF16), cm.astype(BF16), ap, dvec


def _s5_body(x_ref, ks_ref, bm_ref, cm_ref, ap_ref, d_ref, y_ref, h_ref):
    nc = x_ref.shape[2]
    t = SSM_CHUNK
    pad = h_ref.shape[0] - nc
    half = h_ref.shape[1] // 2
    x = x_ref[0, 0]
    h_ref[0:pad] = jnp.zeros((pad, h_ref.shape[1]), F32)
    h_ref[pad:pad + nc] = _dot(x, bm_ref[0])
    for step in range(ap_ref.shape[1]):
        d = 1 << step
        cur = h_ref[pad:pad + nc]
        sft = h_ref[pad - d:pad + nc - d]
        ar = ap_ref[0, step:step + 1, 0:half]
        ai = ap_ref[0, step:step + 1, half:]
        cr, ci = cur[:, :half], cur[:, half:]
        sr, si = sft[:, :half], sft[:, half:]
        h_ref[pad:pad + nc, 0:half] = cr + ar * sr - ai * si
        h_ref[pad:pad + nc, half:] = ci + ar * si + ai * sr
    h_prev = h_ref[pad - 1:pad + nc - 1]
    ys = _dot(h_prev.astype(BF16), cm_ref[0])
    for t0 in range(0, t, 2):
        lo, hi = t0 * LANES, (t0 + 2) * LANES
        conv = _dot(x[:, :hi], ks_ref[0, (t - 2 - t0) * LANES:, :])
        y = conv + ys[:, lo:hi] + d_ref[0][:, lo:hi] * x[:, lo:hi].astype(F32)
        y_ref[0, 0, :, lo:hi] = y.astype(y_ref.dtype)


def _s5(u4, kstack, bm, cm, ap, dvec):
    nblk, bsz, nc, w = u4.shape
    sw = bm.shape[-1]
    pad = max(nc // 2, 8)
    return pl.pallas_call(
        _s5_body,
        grid=(nblk, bsz),
        in_specs=[pl.BlockSpec((1, 1, nc, w), lambda j, b: (j, b, 0, 0)),
                  pl.BlockSpec((1,) + kstack.shape[1:], lambda j, b: (j, 0, 0)),
                  pl.BlockSpec((1,) + bm.shape[1:], lambda j, b: (j, 0, 0)),
                  pl.BlockSpec((1,) + cm.shape[1:], lambda j, b: (j, 0, 0)),
                  pl.BlockSpec((1,) + ap.shape[1:], lambda j, b: (j, 0, 0)),
                  pl.BlockSpec((1, 1, w), lambda j, b: (j, 0, 0))],
        out_specs=pl.BlockSpec((1, 1, nc, w), lambda j, b: (j, b, 0, 0)),
        out_shape=jax.ShapeDtypeStruct((nblk, bsz, nc, w), BF16),
        scratch_shapes=[pltpu.VMEM((pad + nc, sw), F32)],
        compiler_params=_cparams("parallel", "parallel"),
        name="s5",
    )(u4, kstack, bm, cm, ap, dvec)


def _attn_body(q_ref, k_ref, v_ref, o_ref, vt_ref, km_ref, bias_ref, sa_ref, sb_ref):
    qi = pl.program_id(2)
    seq = k_ref.shape[1]
    tq = q_ref.shape[1]
    npair = q_ref.shape[2] // LANES
    nh = 2 * npair
    nb = seq // MOBA_BLOCK
    n_kv = seq // KV_TILE
    bpt = KV_TILE // MOBA_BLOCK
    blk_shift = MOBA_BLOCK.bit_length() - 1

    @pl.when(qi == 0)
    def _build():
        ones = jnp.where(lax.broadcasted_iota(I32, (PV_ROWS - HEAD_DIM, KV_TILE), 0) == 0, 1.0, 0.0)
        lane_b = lax.broadcasted_iota(I32, (nb, LANES), 1)
        for pr in range(npair):
            k = k_ref[0, :, pr * LANES:(pr + 1) * LANES].astype(F32)
            v = v_ref[0, :, pr * LANES:(pr + 1) * LANES].astype(F32)
            for t in range(n_kv):
                v_t = v[t * KV_TILE:(t + 1) * KV_TILE].T
                vt_ref[2 * pr, t] = jnp.concatenate([v_t[:HEAD_DIM], ones], axis=0).astype(BF16)
                vt_ref[2 * pr + 1, t] = jnp.concatenate([v_t[HEAD_DIM:], ones], axis=0).astype(BF16)
            km = jnp.mean(k.reshape(nb, MOBA_BLOCK, LANES), axis=1)
            km_ref[2 * pr] = jnp.where(lane_b < HEAD_DIM, km, 0.0)
            km_ref[2 * pr + 1] = jnp.where(lane_b >= HEAD_DIM, km, 0.0)

    feat = lax.broadcasted_iota(I32, (LANES, tq), 0)
    q_ts, q_h = [], []
    for pr in range(npair):
        q_t = q_ref[0, :, pr * LANES:(pr + 1) * LANES].astype(F32).T
        q_ts += [q_t, q_t]
        q_h += [jnp.where(feat < HEAD_DIM, q_t, 0.0).astype(BF16),
                jnp.where(feat >= HEAD_DIM, q_t, 0.0).astype(BF16)]

    jidx = lax.broadcasted_iota(I32, (nb, tq), 0)
    qblk = qi * (tq // MOBA_BLOCK) + lax.shift_right_logical(
        lax.broadcasted_iota(I32, (nb, tq), 1), blk_shift)
    for hx in range(nh):
        g = _dot(km_ref[hx], q_ts[hx])
        cnt = jnp.zeros((nb, tq), F32)
        for jp in range(nb):
            row = g[jp:jp + 1, :]
            beats = jnp.where(row > g, 1.0, jnp.where(row == g, jnp.where(jp < jidx, 1.0, 0.0), 0.0))
            cnt = cnt + jnp.where(jp < qblk, beats, 0.0)
        past_sel = jnp.where(jidx < qblk, jnp.where(cnt < MOBA_TOPK, 1.0, 0.0), 0.0)
        sel = jnp.where(jidx == qblk, 1.0, past_sel)
        bias_ref[hx] = jnp.where(sel > 0.5, 0.0, NEG_INF)

    def raw_scores(dst_ref, tile):
        off = pl.multiple_of(tile * KV_TILE, KV_TILE)
        for hx in range(nh):
            pr = hx // 2
            dst_ref[hx] = _dot(k_ref[0, pl.ds(off, KV_TILE), pr * LANES:(pr + 1) * LANES], q_h[hx])

    def fold(carry, src_ref, tile, causal=None):
        out = []
        for hx in range(nh):
            parts, biases = [], []
            for j in range(bpt):
                s_j = src_ref[hx, j * MOBA_BLOCK:(j + 1) * MOBA_BLOCK, :]
                if causal is not None:
                    s_j = jnp.where(causal[j], s_j, NEG_INF)
                b_j = bias_ref[hx, pl.ds(tile * bpt + j, 1), :]
                parts.append(s_j)
                biases.append(b_j)
            m_n = jnp.max(parts[0], axis=0, keepdims=True) + biases[0]
            for s_j, b_j in zip(parts[1:], biases[1:]):
                m_n = jnp.maximum(m_n, jnp.max(s_j, axis=0, keepdims=True) + b_j)
            if carry is None:
                p_t = jnp.concatenate([jnp.exp2(s_j + (b_j - m_n)) for s_j, b_j in zip(parts, biases)], axis=0)
                out += [m_n, _dot(vt_ref[hx, tile], p_t.astype(BF16))]
            else:
                m_c, acc_c = carry[2 * hx], carry[2 * hx + 1]
                m_n = jnp.maximum(m_c, m_n)
                p_t = jnp.concatenate([jnp.exp2(s_j + (b_j - m_n)) for s_j, b_j in zip(parts, biases)], axis=0)
                out += [m_n, acc_c * jnp.exp2(m_c - m_n) + _dot(vt_ref[hx, tile], p_t.astype(BF16))]
        return tuple(out)

    raw_scores(sa_ref, qi)
    raw_scores(sb_ref, 0)
    krow = lax.broadcasted_iota(I32, (MOBA_BLOCK, tq), 0)
    qcol = lax.broadcasted_iota(I32, (MOBA_BLOCK, tq), 1)
    state = fold(None, sa_ref, qi, causal=[krow + j * MOBA_BLOCK <= qcol for j in range(bpt)])

    def body(i, carry):
        t0 = 2 * i
        raw_scores(sa_ref, t0 + 1)
        carry = fold(carry, sb_ref, t0)
        raw_scores(sb_ref, jnp.minimum(t0 + 2, n_kv - 1))
        return fold(carry, sa_ref, t0 + 1)

    state = lax.fori_loop(0, qi // 2, body, state)
    state = lax.cond(lax.rem(qi, 2) == 1, lambda c: fold(c, sb_ref, qi - 1), lambda c: c, state)
    for pr in range(npair):
        accs = (state[4 * pr + 1], state[4 * pr + 3])
        o_t = jnp.concatenate([acc[:HEAD_DIM] / acc[HEAD_DIM:HEAD_DIM + 1] for acc in accs], axis=0)
        o_ref[0, :, pr * LANES:(pr + 1) * LANES] = o_t.T.astype(o_ref.dtype)


def _attention(q, k, v):
    bsz, seq, aw = q.shape
    pps = ATTN_PAIRS_PER_STEP
    wl = pps * LANES
    npair = aw // wl
    tq = KV_TILE
    nb = seq // MOBA_BLOCK
    assert seq % KV_TILE == 0 and nb % 8 == 0
    return pl.pallas_call(
        _attn_body,
        grid=(bsz, npair, seq // tq),
        in_specs=[pl.BlockSpec((1, tq, wl), lambda b, p, i: (b, i, p)),
                  pl.BlockSpec((1, seq, wl), lambda b, p, i: (b, 0, p)),
                  pl.BlockSpec((1, seq, wl), lambda b, p, i: (b, 0, p))],
        out_specs=pl.BlockSpec((1, tq, wl), lambda b, p, i: (b, i, p)),
        out_shape=jax.ShapeDtypeStruct((bsz, seq, aw), BF16),
        scratch_shapes=[pltpu.VMEM((2 * pps, seq // KV_TILE, PV_ROWS, KV_TILE), BF16),
                        pltpu.VMEM((2 * pps, nb, LANES), F32),
                        pltpu.VMEM((2 * pps, nb, tq), F32),
                        pltpu.VMEM((2 * pps, KV_TILE, tq), F32),
                        pltpu.VMEM((2 * pps, KV_TILE, tq), F32)],
        compiler_params=_cparams("parallel", "parallel", "arbitrary"),
        name="attn",
    )(q, k, v)


def _merge_body(y_ref, a_ref, x_ref, wglu_ref, bglu_ref, gs_ref, ga_ref, wout_ref, gm_ref,
                gf_ref, shf_ref, scf_ref, rw_ref, rb_ref, x1_ref, h2_ref, idx_ref, wt_ref, cnt_ref, ys_ref):
    nblk = y_ref.shape[0]
    nsub = cnt_ref.shape[0]
    tm = x_ref.shape[1] // nsub
    ne = rw_ref.shape[0]
    nch = tm // SSM_CHUNK
    sub = lax.broadcasted_iota(I32, (ne, tm), 0)
    row_i = lax.broadcasted_iota(I32, (idx_ref.shape[0], tm), 0)
    row_w = lax.broadcasted_iota(I32, (wt_ref.shape[0], tm), 0)
    for sl in range(nsub):
        rows = slice(sl * tm, (sl + 1) * tm)
        for j in range(nblk):
            for t in range(SSM_CHUNK):
                ys_ref[j, pl.ds(t, nch, stride=SSM_CHUNK), :] = (
                    y_ref[j, 0, sl * nch:(sl + 1) * nch, t * LANES:(t + 1) * LANES].astype(F32))
        y = jnp.concatenate([ys_ref[j] for j in range(nblk)], axis=1)
        g = 0.5 * y * (1.0 + lax.erf(y * (2.0 ** -0.5)))
        glu = g * jax.nn.sigmoid(_dot(g.astype(BF16), wglu_ref[...]) + bglu_ref[...])
        ssm_n = _rms(glu, gs_ref[...])
        att_n = _rms(a_ref[0, rows, :].astype(F32), ga_ref[...])
        merged = jnp.concatenate([ssm_n, att_n], axis=1).astype(BF16)
        mix = _dot(merged, wout_ref[...])
        x1 = x_ref[0, rows, :] + gm_ref[0] * mix
        x1_ref[0, rows, :] = x1
        h2 = _rms(x1, gf_ref[...]) * (1.0 + scf_ref[0]) + shf_ref[0]
        h2_ref[0, rows, :] = h2
        logits = lax.dot_general(rw_ref[...], h2, (((1,), (1,)), ((), ())),
                                 preferred_element_type=F32, precision=HIGHEST) + rb_ref[...]
        vals, idxs = [], []
        for _ in range(TOP_K):
            mx = jnp.max(logits, axis=0, keepdims=True)
            ix = jnp.min(jnp.where(logits == mx, sub, ne), axis=0, keepdims=True)
            vals.append(mx)
            idxs.append(ix)
            logits = jnp.where(sub == ix, -jnp.inf, logits)
        exps = [jnp.exp(vv - vals[0]) for vv in vals]
        tot = exps[0]
        for e in exps[1:]:
            tot = tot + e
        idx_out = jnp.zeros(row_i.shape, I32)
        wt_out = jnp.zeros(row_w.shape, F32)
        chosen = jnp.zeros((ne, tm), F32)
        for kk in range(TOP_K):
            idx_out = jnp.where(row_i == kk, idxs[kk], idx_out)
            wt_out = jnp.where(row_w == kk, exps[kk] / tot, wt_out)
            chosen = chosen + jnp.where(sub == idxs[kk], 1.0, 0.0)
        idx_ref[:, rows] = idx_out
        wt_ref[:, rows] = wt_out
        cnt_ref[sl] = jnp.sum(chosen, axis=1, keepdims=True).astype(I32)


def _merge(y4, attn, x, w_glu_bf, b_glu, g_ssm, g_attn, w_out_bf, mod3, g_ffn, router_w, router_b, tm=512):
    bsz, seq, d = x.shape
    nblk = y4.shape[0]
    ssm_w = nblk * LANES
    aw = attn.shape[-1]
    ne = router_w.shape[1]
    tm = min(tm, seq)
    nsub = TILES_PER_STEP if (seq // tm) % TILES_PER_STEP == 0 else 1
    ts = nsub * tm
    spb = seq // ts
    n_tok = bsz * seq
    rw_t = router_w.T
    row = lambda n: pl.BlockSpec((1, n), lambda b, i: (0, 0))
    full = lambda a: pl.BlockSpec(a.shape, lambda b, i: (0,) * a.ndim)
    modv = lambda j: pl.BlockSpec((1, 1, d), lambda b, i: (b, 0, j))
    return pl.pallas_call(
        _merge_body,
        grid=(bsz, spb),
        in_specs=[pl.BlockSpec((nblk, 1, ts // SSM_CHUNK, SSM_CHUNK * LANES), lambda b, i: (0, b, i, 0)),
                  pl.BlockSpec((1, ts, aw), lambda b, i: (b, i, 0)),
                  pl.BlockSpec((1, ts, d), lambda b, i: (b, i, 0)),
                  full(w_glu_bf), row(ssm_w), row(ssm_w), row(aw), full(w_out_bf),
                  modv(2), row(d), modv(3), modv(4), full(rw_t),
                  pl.BlockSpec((ne, 1), lambda b, i: (0, 0))],
        out_specs=[pl.BlockSpec((1, ts, d), lambda b, i: (b, i, 0)),
                   pl.BlockSpec((1, ts, d), lambda b, i: (b, i, 0)),
                   pl.BlockSpec((TOP_K, ts), lambda b, i: (0, b * spb + i)),
                   pl.BlockSpec((WT_ROWS, ts), lambda b, i: (0, b * spb + i)),
                   pl.BlockSpec((nsub, ne, 1), lambda b, i: (b * spb + i, 0, 0))],
        out_shape=[jax.ShapeDtypeStruct((bsz, seq, d), F32),
                   jax.ShapeDtypeStruct((bsz, seq, d), F32),
                   jax.ShapeDtypeStruct((TOP_K, n_tok), I32),
                   jax.ShapeDtypeStruct((WT_ROWS, n_tok), F32),
                   jax.ShapeDtypeStruct((bsz * spb * nsub, ne, 1), I32)],
        scratch_shapes=[pltpu.VMEM((nblk, tm, LANES), F32)],
        compiler_params=_cparams("parallel", "parallel"),
        name="merge",
    )(y4, attn, x, w_glu_bf, b_glu.reshape(1, ssm_w), g_ssm.reshape(1, ssm_w), g_attn.reshape(1, aw),
      w_out_bf, mod3, g_ffn.reshape(1, d), mod3, mod3, rw_t, router_b.reshape(ne, 1))


def _segment_copies(n_seg, meta, make_copy):
    def body(e, _):
        src0, dst0, units = meta(e)
        for sz in SEG_UNITS:
            off = units & ~(2 * sz - 1)

            @pl.when((units & sz) != 0)
            def _(off=off, sz=sz):
                make_copy(src0 + off, dst0 + off, sz).start()
        return 0

    lax.fori_loop(0, n_seg, body, 0)


def _local_slots(idx, lstart_col, ne):
    tb = idx.shape[1]
    sub = lax.broadcasted_iota(I32, (ne, tb), 0)
    hits = [idx[kk:kk + 1, :] == sub for kk in range(TOP_K)]
    onehot = jnp.zeros((ne, tb), F32)
    for h in hits:
        onehot = onehot + jnp.where(h, 1.0, 0.0)
    r = lax.broadcasted_iota(I32, (tb, tb), 0)
    c = lax.broadcasted_iota(I32, (tb, tb), 1)
    tri = jnp.where(r < c, 1.0, 0.0).astype(BF16)
    slot = _dot(onehot.astype(BF16), tri) + lstart_col
    return [jnp.sum(jnp.where(h, slot, 0.0), axis=0, keepdims=True) for h in hits]


def _dispatch_body(lstart_s, pcnt_s, base_s, tail_dst_s, tail_units_s, tail_total_s,
                   idx_ref, lcol_ref, h_ref, xs_ref, ls_ref, sorted_ref, zero_ref, sem, *, ne):
    i = pl.program_id(0)
    n_steps = pl.num_programs(0)
    nsub = lcol_ref.shape[0]
    tb = h_ref.shape[0] // nsub
    lb = sorted_ref.shape[1] * SEG_ALIGN

    @pl.when(i == 0)
    def _():
        zero_ref[...] = jnp.zeros(zero_ref.shape, F32)
        _segment_copies(
            ne, lambda e: (0, tail_dst_s[e] // SEG_ALIGN, tail_units_s[e]),
            lambda src, dst, n: pltpu.make_async_copy(zero_ref.at[pl.ds(src, n)], xs_ref.at[pl.ds(dst, n)],
                                                      sem.at[nsub]))

    def tile_units(t):
        return (lstart_s[t * ne + ne - 1] + pcnt_s[t * ne + ne - 1]) // SEG_ALIGN

    def wait_tile(t, sl):
        done = xs_ref.at[pl.ds(0, tile_units(t))]
        pltpu.make_async_copy(done, done, sem.at[sl]).wait()

    for sl in range(nsub):
        t = i * nsub + sl
        cols = slice(sl * tb, (sl + 1) * tb)

        @pl.when(i >= 1)
        def _(t=t, sl=sl):
            wait_tile(t - nsub, sl)

        slots = _local_slots(idx_ref[:, cols], lcol_ref[sl].astype(F32), ne)
        row_o = lax.broadcasted_iota(I32, (ls_ref.shape[0], tb), 0)
        ls_out = jnp.full((ls_ref.shape[0], tb), -1.0, F32)
        for kk in range(TOP_K):
            ls_out = jnp.where(row_o == kk, slots[kk], ls_out)
        ls_ref[:, cols] = ls_out.astype(I32)
        h = h_ref[cols, :].astype(BF16)
        for r0 in range(0, lb, SORT_CHUNK):
            j = (lax.broadcasted_iota(I32, (SORT_CHUNK, tb), 0) + r0).astype(F32)
            pm = jnp.zeros((SORT_CHUNK, tb), F32)
            for kk in range(TOP_K):
                pm = jnp.where(j == slots[kk], 1.0, pm)
            sorted_ref[sl, r0 // SEG_ALIGN:(r0 + SORT_CHUNK) // SEG_ALIGN] = _dot(pm.astype(BF16), h).reshape(
                SORT_CHUNK // SEG_ALIGN, SEG_ALIGN, h.shape[1])

        def meta(e, t=t):
            return (lstart_s[t * ne + e] // SEG_ALIGN, base_s[t * ne + e] // SEG_ALIGN,
                    pcnt_s[t * ne + e] // SEG_ALIGN)

        def make_copy(src, dst, n, sl=sl):
            return pltpu.make_async_copy(sorted_ref.at[sl, pl.ds(src, n)], xs_ref.at[pl.ds(dst, n)], sem.at[sl])

        _segment_copies(ne, meta, make_copy)

    @pl.when(i == n_steps - 1)
    def _():
        for sl in range(nsub):
            wait_tile(i * nsub + sl, sl)

        @pl.when(tail_total_s[0] > 0)
        def _():
            filled = xs_ref.at[pl.ds(0, tail_total_s[0])]
            pltpu.make_async_copy(filled, filled, sem.at[nsub]).wait()


def _dispatch(h2, idx_t, plan, tail, cap, tb, tg):
    n, d = h2.shape
    lstart, pcnt, base = plan
    nt, ne = lstart.shape
    nsub = TILES_PER_STEP if nt % TILES_PER_STEP == 0 else 1
    lb = tb * TOP_K + ne * SEG_ALIGN
    assert lb % SORT_CHUNK == 0
    return pl.pallas_call(
        functools.partial(_dispatch_body, ne=ne),
        grid_spec=pltpu.PrefetchScalarGridSpec(
            num_scalar_prefetch=6,
            grid=(nt // nsub,),
            in_specs=[pl.BlockSpec((TOP_K, nsub * tb), lambda i, *_: (0, i)),
                      pl.BlockSpec((nsub, ne, 1), lambda i, *_: (i, 0, 0)),
                      pl.BlockSpec((nsub * tb, d), lambda i, *_: (i, 0))],
            out_specs=[pl.BlockSpec(memory_space=pl.ANY),
                       pl.BlockSpec((WT_ROWS, nsub * tb), lambda i, *_: (0, i))],
            scratch_shapes=[pltpu.VMEM((nsub, lb // SEG_ALIGN, SEG_ALIGN, d), F32),
                            pltpu.VMEM((tg // SEG_ALIGN, SEG_ALIGN, d), F32),
                            pltpu.SemaphoreType.DMA((nsub + 1,))],
        ),
        out_shape=[jax.ShapeDtypeStruct((cap // SEG_ALIGN, SEG_ALIGN, d), F32),
                   jax.ShapeDtypeStruct((WT_ROWS, n), I32)],
        compiler_params=_cparams("arbitrary"),
        name="dispatch",
    )(lstart.reshape(-1), pcnt.reshape(-1), base.reshape(-1), *tail, idx_t, lstart.reshape(nt, ne, 1), h2)


def _route_plan(tile_cnt, tg, cap):
    cnt = tile_cnt[:, :, 0]
    ne = cnt.shape[1]
    pcnt = (cnt + SEG_ALIGN - 1) // SEG_ALIGN * SEG_ALIGN
    lstart = jnp.cumsum(pcnt, axis=1) - pcnt
    e_rows = jnp.sum(pcnt, axis=0)
    e_tiles = (e_rows + tg - 1) // tg
    e_end = jnp.cumsum(e_tiles)
    base = ((e_end - e_tiles) * tg)[None, :] + jnp.cumsum(pcnt, axis=0) - pcnt
    ii = jnp.arange(cap // tg, dtype=I32)
    used = ii < e_end[-1]
    tile = jnp.minimum(ii, e_end[-1] - 1)
    e_of = jnp.minimum(jnp.sum((e_end[None, :] <= tile[:, None]).astype(I32), axis=1), ne - 1)
    newe = (e_of != jnp.concatenate([jnp.full((1,), -1, I32), e_of[:-1]])).astype(I32)
    items = (tile.astype(I32), e_of.astype(I32), used.astype(I32), newe)
    tail = (e_end * tg - e_tiles * tg + e_rows, (e_tiles * tg - e_rows) // SEG_ALIGN)
    tail = (tail[0].astype(I32), tail[1].astype(I32), jnp.sum(tail[1]).astype(I32).reshape(1))
    return (lstart.astype(I32), pcnt.astype(I32), base.astype(I32)), tail, items


def _experts_body(tile_ref, exp_ref, used_ref, newe_ref,
                  xs_ref, wg_ref, bg_ref, wu_ref, bu_ref, wd_ref, bd_ref, ys_ref,
                  wg_bf, wu_bf, wd_bf):
    i = pl.program_id(0)

    @pl.when(newe_ref[i] == 1)
    def _cast():
        wg_bf[...] = wg_ref[0].astype(BF16)
        wu_bf[...] = wu_ref[0].astype(BF16)
        wd_bf[...] = wd_ref[0].astype(BF16)

    @pl.when(used_ref[i] == 1)
    def _compute():
        x = xs_ref[...].astype(BF16)
        gate = jnp.minimum(_dot(x, wg_bf[...]) + bg_ref[0], SWIGLU_LIMIT)
        lin = jnp.clip(_dot(x, wu_bf[...]) + bu_ref[0], -SWIGLU_LIMIT, SWIGLU_LIMIT)
        act = gate * jax.nn.sigmoid(SWIGLU_ALPHA * gate) * (lin + 1.0)
        ys_ref[...] = _dot(act.astype(BF16), wd_bf[...]) + bd_ref[0]


def _experts(xs, items, w_gate, b_gate, w_up, b_up, w_down, b_down, tg):
    nk, d = xs.shape
    ne, _, dff = w_gate.shape
    n_items = items[0].shape[0]
    wspec = lambda shp: pl.BlockSpec((1,) + shp, lambda i, t, e, u, nw: (e[i], 0, 0))
    return pl.pallas_call(
        _experts_body,
        grid_spec=pltpu.PrefetchScalarGridSpec(
            num_scalar_prefetch=4,
            grid=(n_items,),
            in_specs=[pl.BlockSpec((tg, d), lambda i, t, e, u, nw: (t[i], 0)),
                      wspec((d, dff)), wspec((1, dff)),
                      wspec((d, dff)), wspec((1, dff)),
                      wspec((dff, d)), wspec((1, d))],
            out_specs=pl.BlockSpec((tg, d), lambda i, t, e, u, nw: (t[i], 0)),
            scratch_shapes=[pltpu.VMEM((d, dff), BF16), pltpu.VMEM((d, dff), BF16), pltpu.VMEM((dff, d), BF16)],
        ),
        out_shape=jax.ShapeDtypeStruct((nk, d), F32),
        compiler_params=_cparams("arbitrary"),
        name="experts",
    )(*items, xs, w_gate, b_gate.reshape(ne, 1, dff), w_up, b_up.reshape(ne, 1, dff),
      w_down, b_down.reshape(ne, 1, d))


def _combine_body(lstart_s, pcnt_s, base_s, ls_ref, wt_ref, x1_ref, gf_ref, gfin_ref, sho_ref, sco_ref,
                  ys_ref, o_ref, buf, sem, *, ne, spb):
    i = pl.program_id(0) * spb + pl.program_id(1)
    nsub = buf.shape[0]
    nt = pl.num_programs(0) * spb * nsub
    tb = x1_ref.shape[1] // nsub
    lb = buf.shape[1] * SEG_ALIGN

    def fetch(t, sl):
        def meta(e):
            return (base_s[t * ne + e] // SEG_ALIGN, lstart_s[t * ne + e] // SEG_ALIGN,
                    pcnt_s[t * ne + e] // SEG_ALIGN)

        def make_copy(src, dst, n):
            return pltpu.make_async_copy(ys_ref.at[pl.ds(src, n)], buf.at[sl, pl.ds(dst, n)], sem.at[sl])

        _segment_copies(ne, meta, make_copy)

    @pl.when(i == 0)
    def _():
        buf[...] = jnp.zeros(buf.shape, F32)
        fetch(0, 0)

    for sl in range(nsub):
        t = i * nsub + sl
        rows = slice(sl * tb, (sl + 1) * tb)

        @pl.when(t + 1 < nt)
        def _(t=t, sl=sl):
            fetch(t + 1, (sl + 1) % nsub)

        total = (lstart_s[t * ne + ne - 1] + pcnt_s[t * ne + ne - 1]) // SEG_ALIGN
        done = buf.at[sl, pl.ds(0, total)]
        pltpu.make_async_copy(done, done, sem.at[sl]).wait()
        ls_col = ls_ref[:, rows].astype(F32).T
        wt_col = wt_ref[:, rows].T
        ffn = jnp.zeros((tb, x1_ref.shape[2]), F32)
        for r0 in range(0, lb, SORT_CHUNK):
            j = (lax.broadcasted_iota(I32, (tb, SORT_CHUNK), 1) + r0).astype(F32)
            wm = jnp.zeros((tb, SORT_CHUNK), F32)
            for kk in range(TOP_K):
                wm = jnp.where(j == ls_col[:, kk:kk + 1], wt_col[:, kk:kk + 1], wm)
            y_rows = buf[sl, r0 // SEG_ALIGN:(r0 + SORT_CHUNK) // SEG_ALIGN].reshape(SORT_CHUNK, x1_ref.shape[2])
            ffn = ffn + _dot(wm.astype(BF16), y_rows.astype(BF16))
        x2 = x1_ref[0, rows, :] + gf_ref[0] * ffn
        o_ref[0, rows, :] = _rms(x2, gfin_ref[...]) * (1.0 + sco_ref[0]) + sho_ref[0]


def _combine(plan, ls_t, wt_t, x1, mod3, g_final, fmod3, ys, tb):
    bsz, seq, d = x1.shape
    lstart, pcnt, base = plan
    nt, ne = lstart.shape
    nsub = TILES_PER_STEP if (seq // tb) % TILES_PER_STEP == 0 else 1
    ts = nsub * tb
    spb = seq // ts
    lb = tb * TOP_K + ne * SEG_ALIGN
    return pl.pallas_call(
        functools.partial(_combine_body, ne=ne, spb=spb),
        grid_spec=pltpu.PrefetchScalarGridSpec(
            num_scalar_prefetch=3,
            grid=(bsz, spb),
            in_specs=[pl.BlockSpec((WT_ROWS, ts), lambda b, i, *_: (0, b * spb + i)),
                      pl.BlockSpec((WT_ROWS, ts), lambda b, i, *_: (0, b * spb + i)),
                      pl.BlockSpec((1, ts, d), lambda b, i, *_: (b, i, 0)),
                      pl.BlockSpec((1, 1, d), lambda b, i, *_: (b, 0, 5)),
                      pl.BlockSpec((1, d), lambda b, i, *_: (0, 0)),
                      pl.BlockSpec((1, 1, d), lambda b, i, *_: (b, 0, 0)),
                      pl.BlockSpec((1, 1, d), lambda b, i, *_: (b, 0, 1)),
                      pl.BlockSpec(memory_space=pl.ANY)],
            out_specs=pl.BlockSpec((1, ts, d), lambda b, i, *_: (b, i, 0)),
            scratch_shapes=[pltpu.VMEM((nsub, lb // SEG_ALIGN, SEG_ALIGN, d), F32),
                            pltpu.SemaphoreType.DMA((nsub,))],
        ),
        out_shape=jax.ShapeDtypeStruct((bsz, seq, d), F32),
        compiler_params=_cparams("arbitrary", "arbitrary"),
        name="combine",
    )(lstart.reshape(-1), pcnt.reshape(-1), base.reshape(-1), ls_t, wt_t, x1, mod3, g_final.reshape(1, d),
      fmod3, fmod3, ys.reshape(-1, SEG_ALIGN, d))


def kernel(x, c, positions, ada_w, ada_b, final_ada_w, final_ada_b, norm_mix_g, norm_ffn_g, norm_final_g, w_in, ssm_lambda_re, ssm_lambda_im, ssm_log_dt, ssm_b_re, ssm_b_im, ssm_c_re, ssm_c_im, ssm_d, ssm_w_glu, ssm_b_glu, out_norm_ssm_g, out_norm_attn_g, w_out, router_w, router_b, exp_w_gate, exp_b_gate, exp_w_up, exp_b_up, exp_w_down, exp_b_down):
    bsz, seq, d = x.shape
    depth = ada_w.shape[0]
    ssm_w = ssm_d.shape[-1]
    attn_w = (w_in.shape[-1] - ssm_w) // 3
    ne = router_w.shape[-1]
    n_tok = bsz * seq
    nc = seq // SSM_CHUNK
    tg = 512

    rope_tab, rope_exp = _rope_tables(positions)
    fmod3 = _adaln(c, final_ada_w, final_ada_b).reshape(bsz, 1, 2 * d)
    for l in range(depth):
        mod3 = _adaln(c, ada_w[l], ada_b[l]).reshape(bsz, 1, -1)
        u4, q, k, v = _inproj(x, norm_mix_g[l], mod3, w_in[l].astype(BF16), rope_tab, rope_exp, ssm_w, attn_w)
        tables = _s5_params(ssm_lambda_re[l], ssm_lambda_im[l], ssm_log_dt[l], ssm_b_re[l], ssm_b_im[l],
                            ssm_c_re[l], ssm_c_im[l], ssm_d[l], nc)
        y4 = _s5(u4, *tables)
        attn = _attention(q, k, v)
        x1, h2, idx_t, wt_t, tile_cnt = _merge(y4, attn, x, ssm_w_glu[l].astype(BF16), ssm_b_glu[l], out_norm_ssm_g[l],
                                     out_norm_attn_g[l], w_out[l].astype(BF16), mod3, norm_ffn_g[l],
                                     router_w[l], router_b[l])
        tb = seq // (tile_cnt.shape[0] // bsz)
        cap = n_tok * TOP_K + tile_cnt.shape[0] * ne * SEG_ALIGN + ne * tg
        assert cap % tg == 0 and max(tb, tg) // SEG_ALIGN < 2 * SEG_UNITS[0]
        plan, tail, items = _route_plan(tile_cnt, tg, cap)
        xs, ls_t = _dispatch(h2.reshape(n_tok, d), idx_t, plan, tail, cap, tb, tg)
        ys = _experts(xs.reshape(cap, d), items, exp_w_gate[l], exp_b_gate[l], exp_w_up[l], exp_b_up[l],
                      exp_w_down[l], exp_b_down[l], tg)
        if l + 1 < depth:
            raise NotImplementedError("depth > 1 needs the non-final combine")
        x = _combine(plan, ls_t, wt_t, x1, mod3, norm_final_g, fmod3, ys, tb)
    return x
```

```python
import functools
import math

import jax
import jax.numpy as jnp
from jax import lax
from jax.experimental import pallas as pl
from jax.experimental.pallas import tpu as pltpu

F32 = jnp.float32
BF16 = jnp.bfloat16
I32 = jnp.int32
HIGHEST = lax.Precision.HIGHEST

LANES = 128
HEAD_DIM = 64
MOBA_BLOCK = 256
MOBA_TOPK = 3
ROT_DIM = HEAD_DIM // 4
ROPE_THETA = 500000.0
SSM_CHUNK = 16
TOP_K = 4
SWIGLU_ALPHA = 1.702
SWIGLU_LIMIT = 7.0
NORM_EPS = 1e-5
NEG_INF = -1e30
LOG2_E = math.log2(math.e)
KV_TILE = 2 * MOBA_BLOCK
PV_ROWS = HEAD_DIM + 8
ATTN_PAIRS_PER_STEP = 4
S5_SEQS_PER_STEP = 2
WT_ROWS = 8
SEG_ALIGN = 8
SEG_UNITS = (64, 32, 16, 8, 4, 2, 1)
SORT_CHUNK = 256
TILES_PER_STEP = 2
VMEM_LIMIT = 56 * 1024 * 1024


def _cparams(*sem):
    return pltpu.CompilerParams(dimension_semantics=sem, vmem_limit_bytes=VMEM_LIMIT)


def _dot(a, b):
    return jnp.dot(a, b, preferred_element_type=F32)


def _rms(x, g):
    return x * lax.rsqrt(jnp.mean(x * x, axis=-1, keepdims=True) + NORM_EPS) * g


def _adaln_body(c_ref, w_ref, b_ref, o_ref):
    c = c_ref[...]
    ca = c * jax.nn.sigmoid(c)
    o_ref[...] = jnp.dot(ca, w_ref[...], preferred_element_type=F32, precision=HIGHEST) + b_ref[...]


def _adaln(c, w, b, tn=512):
    bsz, d = c.shape
    n = w.shape[1]
    return pl.pallas_call(
        _adaln_body,
        grid=(n // tn,),
        in_specs=[pl.BlockSpec((bsz, d), lambda j: (0, 0)),
                  pl.BlockSpec((d, tn), lambda j: (0, j)),
                  pl.BlockSpec((1, tn), lambda j: (0, j))],
        out_specs=pl.BlockSpec((bsz, tn), lambda j: (0, j)),
        out_shape=jax.ShapeDtypeStruct((bsz, n), F32),
        compiler_params=_cparams("parallel"),
        name="adaln",
    )(c, w, b.reshape(1, n))


def _inproj_body(x_ref, g_ref, sh_ref, sc_ref, w_ref, tab_ref, exp_ref, u_ref, q_ref, k_ref, v_ref, us_ref):
    x = x_ref[0]
    tm = x.shape[0]
    h = _rms(x, g_ref[...]) * (1.0 + sc_ref[0]) + sh_ref[0]
    proj = _dot(h.astype(BF16), w_ref[...])
    ssm_w = u_ref.shape[0] * LANES
    aw = q_ref.shape[2]
    for j in range(u_ref.shape[0]):
        us_ref[j] = proj[:, j * LANES:(j + 1) * LANES]
        for t in range(SSM_CHUNK):
            u_ref[j, 0, :, t * LANES:(t + 1) * LANES] = us_ref[
                j, pl.ds(t, tm // SSM_CHUNK, stride=SSM_CHUNK), :].astype(BF16)
    t_hi = tab_ref[0].astype(BF16)
    t_lo = (tab_ref[0] - t_hi.astype(F32)).astype(BF16)
    tn = (((0,), (0,)), ((), ()))
    tab = (lax.dot_general(t_hi, exp_ref[...], tn, preferred_element_type=F32)
           + lax.dot_general(t_lo, exp_ref[...], tn, preferred_element_type=F32))
    lane1 = lax.broadcasted_iota(I32, (tm, LANES), 1)
    cos1 = tab[:, :LANES] + jnp.where((lane1 & (HEAD_DIM - 1)) >= ROT_DIM, 1.0, 0.0)
    reps = aw // LANES
    cosf = jnp.concatenate([cos1] * reps, axis=1)
    sinf = jnp.concatenate([tab[:, LANES:]] * reps, axis=1)
    lane = lax.broadcasted_iota(I32, (tm, aw), 1)
    first_half = (lane & (HEAD_DIM - 1)) < (ROT_DIM // 2)

    def rope(t):
        partner = jnp.where(first_half, pltpu.roll(t, aw - ROT_DIM // 2, 1), pltpu.roll(t, ROT_DIM // 2, 1))
        return t * cosf + partner * sinf

    q = rope(proj[:, ssm_w:ssm_w + aw]) * (HEAD_DIM ** -0.5 * LOG2_E)
    k = rope(proj[:, ssm_w + aw:ssm_w + 2 * aw])
    q_ref[0] = q.astype(BF16)
    k_ref[0] = k.astype(BF16)
    v_ref[0] = proj[:, ssm_w + 2 * aw:ssm_w + 3 * aw].astype(BF16)


def _inproj(x, gain, mod3, w_in_bf, rope_tab, rope_exp, ssm_w, attn_w, tm=1024):
    bsz, seq, d = x.shape
    n_u = ssm_w // LANES
    tm = min(tm, seq)
    return pl.pallas_call(
        _inproj_body,
        grid=(bsz, seq // tm),
        in_specs=[pl.BlockSpec((1, tm, d), lambda b, i: (b, i, 0)),
                  pl.BlockSpec((1, d), lambda b, i: (0, 0)),
                  pl.BlockSpec((1, 1, d), lambda b, i: (b, 0, 0)),
                  pl.BlockSpec((1, 1, d), lambda b, i: (b, 0, 1)),
                  pl.BlockSpec(w_in_bf.shape, lambda b, i: (0, 0)),
                  pl.BlockSpec((1, ROT_DIM, tm), lambda b, i: (b, 0, i)),
                  pl.BlockSpec(rope_exp.shape, lambda b, i: (0, 0))],
        out_specs=[pl.BlockSpec((n_u, 1, tm // SSM_CHUNK, SSM_CHUNK * LANES), lambda b, i: (0, b, i, 0)),
                   pl.BlockSpec((1, tm, attn_w), lambda b, i: (b, i, 0)),
                   pl.BlockSpec((1, tm, attn_w), lambda b, i: (b, i, 0)),
                   pl.BlockSpec((1, tm, attn_w), lambda b, i: (b, i, 0))],
        out_shape=[jax.ShapeDtypeStruct((n_u, bsz, seq // SSM_CHUNK, SSM_CHUNK * LANES), BF16),
                   jax.ShapeDtypeStruct((bsz, seq, attn_w), BF16),
                   jax.ShapeDtypeStruct((bsz, seq, attn_w), BF16),
                   jax.ShapeDtypeStruct((bsz, seq, attn_w), BF16)],
        scratch_shapes=[pltpu.VMEM((n_u, tm, LANES), F32)],
        compiler_params=_cparams("parallel", "parallel"),
        name="inproj",
    )(x, gain.reshape(1, d), mod3, mod3, w_in_bf, rope_tab, rope_exp)


def _rope_tables(positions):
    half = ROT_DIM // 2
    inv_freq = ROPE_THETA ** (-jnp.arange(0, ROT_DIM, 2, dtype=F32) / ROT_DIM)
    ang = positions.astype(F32)[:, None, :] * inv_freq[None, :, None]
    tab = jnp.concatenate([jnp.cos(ang), jnp.sin(ang)], axis=1)
    i = jnp.arange(ROT_DIM)[:, None]
    hl = jnp.arange(LANES)[None, :] % HEAD_DIM
    e_cos = jnp.where((i < half) & ((hl == i) | (hl == i + half)), 1.0, 0.0)
    e_sin = jnp.where(i >= half, jnp.where(hl == i - half, -1.0, jnp.where(hl == i, 1.0, 0.0)), 0.0)
    return tab, jnp.concatenate([e_cos, e_sin], axis=1).astype(BF16)


def _s5_params(lam_re, lam_im, log_dt, b_re, b_im, c_re, c_im, d_skip, n_chunks):
    g_all, p = lam_re.shape
    hc = b_re.shape[-1]
    t = SSM_CHUNK
    gpb = LANES // hc
    nblk = g_all // gpb
    lr, li = lam_re.astype(F32), lam_im.astype(F32)
    dt = jnp.exp(log_dt.astype(F32))[:, None]
    ldr, ldi = lr * dt, li * dt

    def apow(k):
        k = jnp.asarray(k, F32)[..., None, None]
        mag = jnp.exp(ldr * k)
        return mag * jnp.cos(ldi * k), mag * jnp.sin(ldi * k)

    ar, ai = apow(1.0)
    zr, zi = ar - 1.0, ai
    den = lr * lr + li * li
    cr = (zr * lr + zi * li) / den
    ci = (zi * lr - zr * li) / den
    bre, bim = b_re.astype(F32), b_im.astype(F32)
    bbr = cr[..., None] * bre - ci[..., None] * bim
    bbi = cr[..., None] * bim + ci[..., None] * bre
    cre, cim = c_re.astype(F32), c_im.astype(F32)
    def block_diag(a):
        w = a.shape[-1]
        wide = jnp.concatenate([a] * gpb, axis=-1)
        grp = jnp.arange(gpb)[:, None, None]
        lane_grp = (jnp.arange(gpb * w) // w)[None, None, :]
        wide = jnp.where(lane_grp == grp, wide, 0.0)
        return wide.reshape(a.shape[:-3] + (gpb * a.shape[-2], gpb * w))

    pr, pi = apow(jnp.arange(t))
    abr = pr[..., None] * bbr - pi[..., None] * bbi
    abi = pr[..., None] * bbi + pi[..., None] * bbr
    kf = (jnp.einsum('gnp,kgph->kghn', cre, abr, precision=HIGHEST)
          - jnp.einsum('gnp,kgph->kghn', cim, abi, precision=HIGHEST))
    kblk = block_diag(kf.reshape(t, nblk, gpb, hc, hc))
    kpad = jnp.concatenate([jnp.zeros_like(kblk[:1]), kblk], axis=0)
    pm = jnp.concatenate([kpad[0:t], kpad[1:t + 1]], axis=-1)
    kstack = jnp.transpose(pm[::-1], (1, 0, 2, 3)).reshape(nblk, t * LANES, 2 * LANES)

    def inject(ab):
        ab = jnp.transpose(ab[::-1].reshape(t, nblk, gpb, p, hc), (1, 0, 2, 4, 3))
        return block_diag(ab).reshape(nblk, t * LANES, gpb * p)

    bm = jnp.concatenate([inject(abr), inject(abi)], axis=-1)

    qr, qi = apow(jnp.arange(t) + 1.0)
    c_from_re = cre[None] * qr[:, :, None, :] - cim[None] * qi[:, :, None, :]
    c_from_im = -cre[None] * qi[:, :, None, :] - cim[None] * qr[:, :, None, :]

    def readout_t(cc):
        cc = jnp.transpose(cc.reshape(t, nblk, gpb, hc, p), (1, 0, 2, 3, 4))
        return block_diag(cc).reshape(nblk, t * LANES, gpb * p)

    cm = jnp.swapaxes(jnp.concatenate([readout_t(c_from_re), readout_t(c_from_im)], axis=-1), 1, 2)

    n_steps = max(1, int(math.log2(n_chunks)))
    shifts = [float(t * (1 << s)) for s in range(n_steps)]
    sr, si = apow(jnp.asarray(shifts))
    ap = jnp.concatenate([sr.reshape(n_steps, nblk, gpb * p), si.reshape(n_steps, nblk, gpb * p)], axis=-1)
    ap = jnp.transpose(ap, (1, 0, 2))
    dvec = jnp.tile(d_skip.astype(F32).reshape(nblk, 1, LANES), (1, 1, t))
    return kstack.astype(BF16), bm.astype(BF16), cm.astype(BF16), ap, dvec


def _s5_body(x_ref, ks_ref, bm_ref, cm_ref, ap_ref, d_ref, y_ref, h_ref):
    nseq, nc = x_ref.shape[1], x_ref.shape[2]
    t = SSM_CHUNK
    pad = h_ref.shape[1] - nc
    half = h_ref.shape[2] // 2
    x = x_ref[0].reshape(nseq * nc, x_ref.shape[3])
    inj = _dot(x, bm_ref[0])
    h_prev = []
    for sq in range(nseq):
        h_ref[sq, 0:pad] = jnp.zeros((pad, h_ref.shape[2]), F32)
        h_ref[sq, pad:pad + nc] = inj[sq * nc:(sq + 1) * nc]
        for step in range(ap_ref.shape[1]):
            d = 1 << step
            cur = h_ref[sq, pad:pad + nc]
            sft = h_ref[sq, pad - d:pad + nc - d]
            ar = ap_ref[0, step:step + 1, 0:half]
            ai = ap_ref[0, step:step + 1, half:]
            cr, ci = cur[:, :half], cur[:, half:]
            sr, si = sft[:, :half], sft[:, half:]
            h_ref[sq, pad:pad + nc, 0:half] = cr + ar * sr - ai * si
            h_ref[sq, pad:pad + nc, half:] = ci + ar * si + ai * sr
        h_prev.append(h_ref[sq, pad - 1:pad + nc - 1])
    ys = _dot(jnp.concatenate(h_prev, axis=0).astype(BF16), cm_ref[0])
    for t0 in range(0, t, 2):
        lo, hi = t0 * LANES, (t0 + 2) * LANES
        conv = _dot(x[:, :hi], ks_ref[0, (t - 2 - t0) * LANES:, :])
        y = conv + ys[:, lo:hi] + d_ref[0][:, lo:hi] * x[:, lo:hi].astype(F32)
        y_ref[0, :, :, lo:hi] = y.reshape(nseq, nc, hi - lo).astype(y_ref.dtype)


def _s5(u4, kstack, bm, cm, ap, dvec):
    nblk, bsz, nc, w = u4.shape
    sw = bm.shape[-1]
    pad = max(nc // 2, 8)
    nseq = S5_SEQS_PER_STEP if bsz % S5_SEQS_PER_STEP == 0 else 1
    return pl.pallas_call(
        _s5_body,
        grid=(nblk, bsz // nseq),
        in_specs=[pl.BlockSpec((1, nseq, nc, w), lambda j, b: (j, b, 0, 0)),
                  pl.BlockSpec((1,) + kstack.shape[1:], lambda j, b: (j, 0, 0)),
                  pl.BlockSpec((1,) + bm.shape[1:], lambda j, b: (j, 0, 0)),
                  pl.BlockSpec((1,) + cm.shape[1:], lambda j, b: (j, 0, 0)),
                  pl.BlockSpec((1,) + ap.shape[1:], lambda j, b: (j, 0, 0)),
                  pl.BlockSpec((1, 1, w), lambda j, b: (j, 0, 0))],
        out_specs=pl.BlockSpec((1, nseq, nc, w), lambda j, b: (j, b, 0, 0)),
        out_shape=jax.ShapeDtypeStruct((nblk, bsz, nc, w), BF16),
        scratch_shapes=[pltpu.VMEM((nseq, pad + nc, sw), F32)],
        compiler_params=_cparams("parallel", "parallel"),
        name="s5",
    )(u4, kstack, bm, cm, ap, dvec)


def _attn_body(q_ref, k_ref, v_ref, o_ref, vt_ref, km_ref, bias_ref, sa_ref, sb_ref):
    qi = pl.program_id(2)
    seq = k_ref.shape[1]
    tq = q_ref.shape[1]
    npair = q_ref.shape[2] // LANES
    nh = 2 * npair
    nb = seq // MOBA_BLOCK
    n_kv = seq // KV_TILE
    bpt = KV_TILE // MOBA_BLOCK
    blk_shift = MOBA_BLOCK.bit_length() - 1

    @pl.when(qi == 0)
    def _build():
        ones = jnp.where(lax.broadcasted_iota(I32, (PV_ROWS - HEAD_DIM, KV_TILE), 0) == 0, 1.0, 0.0)
        lane_b = lax.broadcasted_iota(I32, (nb, LANES), 1)
        for pr in range(npair):
            k = k_ref[0, :, pr * LANES:(pr + 1) * LANES].astype(F32)
            v = v_ref[0, :, pr * LANES:(pr + 1) * LANES].astype(F32)
            for t in range(n_kv):
                v_t = v[t * KV_TILE:(t + 1) * KV_TILE].T
                vt_ref[2 * pr, t] = jnp.concatenate([v_t[:HEAD_DIM], ones], axis=0).astype(BF16)
                vt_ref[2 * pr + 1, t] = jnp.concatenate([v_t[HEAD_DIM:], ones], axis=0).astype(BF16)
            km = jnp.mean(k.reshape(nb, MOBA_BLOCK, LANES), axis=1)
            km_ref[2 * pr] = jnp.where(lane_b < HEAD_DIM, km, 0.0)
            km_ref[2 * pr + 1] = jnp.where(lane_b >= HEAD_DIM, km, 0.0)

    feat = lax.broadcasted_iota(I32, (LANES, tq), 0)
    q_ts, q_h = [], []
    for pr in range(npair):
        q_t = q_ref[0, :, pr * LANES:(pr + 1) * LANES].astype(F32).T
        q_ts += [q_t, q_t]
        q_h += [jnp.where(feat < HEAD_DIM, q_t, 0.0).astype(BF16),
                jnp.where(feat >= HEAD_DIM, q_t, 0.0).astype(BF16)]

    jidx = lax.broadcasted_iota(I32, (nb, tq), 0)
    qblk = qi * (tq // MOBA_BLOCK) + lax.shift_right_logical(
        lax.broadcasted_iota(I32, (nb, tq), 1), blk_shift)
    for hx in range(nh):
        g = _dot(km_ref[hx], q_ts[hx])
        cnt = jnp.zeros((nb, tq), F32)
        for jp in range(nb):
            row = g[jp:jp + 1, :]
            beats = jnp.where(row > g, 1.0, jnp.where(row == g, jnp.where(jp < jidx, 1.0, 0.0), 0.0))
            cnt = cnt + jnp.where(jp < qblk, beats, 0.0)
        past_sel = jnp.where(jidx < qblk, jnp.where(cnt < MOBA_TOPK, 1.0, 0.0), 0.0)
        sel = jnp.where(jidx == qblk, 1.0, past_sel)
        bias_ref[hx] = jnp.where(sel > 0.5, 0.0, NEG_INF)

    def raw_scores(dst_ref, tile):
        off = pl.multiple_of(tile * KV_TILE, KV_TILE)
        for hx in range(nh):
            pr = hx // 2
            dst_ref[hx] = _dot(k_ref[0, pl.ds(off, KV_TILE), pr * LANES:(pr + 1) * LANES], q_h[hx])

    def fold(carry, src_ref, tile, causal=None):
        out = []
        for hx in range(nh):
            parts, biases = [], []
            for j in range(bpt):
                s_j = src_ref[hx, j * MOBA_BLOCK:(j + 1) * MOBA_BLOCK, :]
                if causal is not None:
                    s_j = jnp.where(causal[j], s_j, NEG_INF)
                b_j = bias_ref[hx, pl.ds(tile * bpt + j, 1), :]
                parts.append(s_j)
                biases.append(b_j)
            m_n = jnp.max(parts[0], axis=0, keepdims=True) + biases[0]
            for s_j, b_j in zip(parts[1:], biases[1:]):
                m_n = jnp.maximum(m_n, jnp.max(s_j, axis=0, keepdims=True) + b_j)
            if carry is None:
                p_t = jnp.concatenate([jnp.exp2(s_j + (b_j - m_n)) for s_j, b_j in zip(parts, biases)], axis=0)
                out += [m_n, _dot(vt_ref[hx, tile], p_t.astype(BF16))]
            else:
                m_c, acc_c = carry[2 * hx], carry[2 * hx + 1]
                m_n = jnp.maximum(m_c, m_n)
                p_t = jnp.concatenate([jnp.exp2(s_j + (b_j - m_n)) for s_j, b_j in zip(parts, biases)], axis=0)
                out += [m_n, acc_c * jnp.exp2(m_c - m_n) + _dot(vt_ref[hx, tile], p_t.astype(BF16))]
        return tuple(out)

    raw_scores(sa_ref, qi)
    raw_scores(sb_ref, 0)
    krow = lax.broadcasted_iota(I32, (MOBA_BLOCK, tq), 0)
    qcol = lax.broadcasted_iota(I32, (MOBA_BLOCK, tq), 1)
    state = fold(None, sa_ref, qi, causal=[krow + j * MOBA_BLOCK <= qcol for j in range(bpt)])

    def body(i, carry):
        t0 = 2 * i
        raw_scores(sa_ref, t0 + 1)
        carry = fold(carry, sb_ref, t0)
        raw_scores(sb_ref, jnp.minimum(t0 + 2, n_kv - 1))
        return fold(carry, sa_ref, t0 + 1)

    state = lax.fori_loop(0, qi // 2, body, state)
    state = lax.cond(lax.rem(qi, 2) == 1, lambda c: fold(c, sb_ref, qi - 1), lambda c: c, state)
    for pr in range(npair):
        accs = (state[4 * pr + 1], state[4 * pr + 3])
        o_t = jnp.concatenate([acc[:HEAD_DIM] / acc[HEAD_DIM:HEAD_DIM + 1] for acc in accs], axis=0)
        o_ref[0, :, pr * LANES:(pr + 1) * LANES] = o_t.T.astype(o_ref.dtype)


def _attention(q, k, v):
    bsz, seq, aw = q.shape
    pps = ATTN_PAIRS_PER_STEP
    wl = pps * LANES
    npair = aw // wl
    tq = KV_TILE
    nb = seq // MOBA_BLOCK
    assert seq % KV_TILE == 0 and nb % 8 == 0
    return pl.pallas_call(
        _attn_body,
        grid=(bsz, npair, seq // tq),
        in_specs=[pl.BlockSpec((1, tq, wl), lambda b, p, i: (b, i, p)),
                  pl.BlockSpec((1, seq, wl), lambda b, p, i: (b, 0, p)),
                  pl.BlockSpec((1, seq, wl), lambda b, p, i: (b, 0, p))],
        out_specs=pl.BlockSpec((1, tq, wl), lambda b, p, i: (b, i, p)),
        out_shape=jax.ShapeDtypeStruct((bsz, seq, aw), BF16),
        scratch_shapes=[pltpu.VMEM((2 * pps, seq // KV_TILE, PV_ROWS, KV_TILE), BF16),
                        pltpu.VMEM((2 * pps, nb, LANES), F32),
                        pltpu.VMEM((2 * pps, nb, tq), F32),
                        pltpu.VMEM((2 * pps, KV_TILE, tq), F32),
                        pltpu.VMEM((2 * pps, KV_TILE, tq), F32)],
        compiler_params=_cparams("parallel", "parallel", "arbitrary"),
        name="attn",
    )(q, k, v)


def _merge_body(y_ref, a_ref, x_ref, wglu_ref, bglu_ref, gs_ref, ga_ref, wout_ref, gm_ref,
                gf_ref, shf_ref, scf_ref, rw_ref, rb_ref, x1_ref, h2_ref, idx_ref, wt_ref, cnt_ref, ys_ref):
    nblk = y_ref.shape[0]
    nsub = cnt_ref.shape[0]
    tm = x_ref.shape[1] // nsub
    ne = rw_ref.shape[0]
    nch = tm // SSM_CHUNK
    sub = lax.broadcasted_iota(I32, (ne, tm), 0)
    row_i = lax.broadcasted_iota(I32, (idx_ref.shape[0], tm), 0)
    row_w = lax.broadcasted_iota(I32, (wt_ref.shape[0], tm), 0)
    for sl in range(nsub):
        rows = slice(sl * tm, (sl + 1) * tm)
        for j in range(nblk):
            for t in range(SSM_CHUNK):
                ys_ref[j, pl.ds(t, nch, stride=SSM_CHUNK), :] = (
                    y_ref[j, 0, sl * nch:(sl + 1) * nch, t * LANES:(t + 1) * LANES].astype(F32))
        y = jnp.concatenate([ys_ref[j] for j in range(nblk)], axis=1)
        g = 0.5 * y * (1.0 + lax.erf(y * (2.0 ** -0.5)))
        glu = g * jax.nn.sigmoid(_dot(g.astype(BF16), wglu_ref[...]) + bglu_ref[...])
        ssm_n = _rms(glu, gs_ref[...])
        att_n = _rms(a_ref[0, rows, :].astype(F32), ga_ref[...])
        merged = jnp.concatenate([ssm_n, att_n], axis=1).astype(BF16)
        mix = _dot(merged, wout_ref[...])
        x1 = x_ref[0, rows, :] + gm_ref[0] * mix
        x1_ref[0, rows, :] = x1
        h2 = _rms(x1, gf_ref[...]) * (1.0 + scf_ref[0]) + shf_ref[0]
        h2_ref[0, rows, :] = h2
        logits = lax.dot_general(rw_ref[...], h2, (((1,), (1,)), ((), ())),
                                 preferred_element_type=F32, precision=HIGHEST) + rb_ref[...]
        vals, idxs = [], []
        for _ in range(TOP_K):
            mx = jnp.max(logits, axis=0, keepdims=True)
            ix = jnp.min(jnp.where(logits == mx, sub, ne), axis=0, keepdims=True)
            vals.append(mx)
            idxs.append(ix)
            logits = jnp.where(sub == ix, -jnp.inf, logits)
        exps = [jnp.exp(vv - vals[0]) for vv in vals]
        tot = exps[0]
        for e in exps[1:]:
            tot = tot + e
        idx_out = jnp.zeros(row_i.shape, I32)
        wt_out = jnp.zeros(row_w.shape, F32)
        chosen = jnp.zeros((ne, tm), F32)
        for kk in range(TOP_K):
            idx_out = jnp.where(row_i == kk, idxs[kk], idx_out)
            wt_out = jnp.where(row_w == kk, exps[kk] / tot, wt_out)
            chosen = chosen + jnp.where(sub == idxs[kk], 1.0, 0.0)
        idx_ref[:, rows] = idx_out
        wt_ref[:, rows] = wt_out
        cnt_ref[sl] = jnp.sum(chosen, axis=1, keepdims=True).astype(I32)


def _merge(y4, attn, x, w_glu_bf, b_glu, g_ssm, g_attn, w_out_bf, mod3, g_ffn, router_w, router_b, tm=512):
    bsz, seq, d = x.shape
    nblk = y4.shape[0]
    ssm_w = nblk * LANES
    aw = attn.shape[-1]
    ne = router_w.shape[1]
    tm = min(tm, seq)
    nsub = TILES_PER_STEP if (seq // tm) % TILES_PER_STEP == 0 else 1
    ts = nsub * tm
    spb = seq // ts
    n_tok = bsz * seq
    rw_t = router_w.T
    row = lambda n: pl.BlockSpec((1, n), lambda b, i: (0, 0))
    full = lambda a: pl.BlockSpec(a.shape, lambda b, i: (0,) * a.ndim)
    modv = lambda j: pl.BlockSpec((1, 1, d), lambda b, i: (b, 0, j))
    return pl.pallas_call(
        _merge_body,
        grid=(bsz, spb),
        in_specs=[pl.BlockSpec((nblk, 1, ts // SSM_CHUNK, SSM_CHUNK * LANES), lambda b, i: (0, b, i, 0)),
                  pl.BlockSpec((1, ts, aw), lambda b, i: (b, i, 0)),
                  pl.BlockSpec((1, ts, d), lambda b, i: (b, i, 0)),
                  full(w_glu_bf), row(ssm_w), row(ssm_w), row(aw), full(w_out_bf),
                  modv(2), row(d), modv(3), modv(4), full(rw_t),
                  pl.BlockSpec((ne, 1), lambda b, i: (0, 0))],
        out_specs=[pl.BlockSpec((1, ts, d), lambda b, i: (b, i, 0)),
                   pl.BlockSpec((1, ts, d), lambda b, i: (b, i, 0)),
                   pl.BlockSpec((TOP_K, ts), lambda b, i: (0, b * spb + i)),
                   pl.BlockSpec((WT_ROWS, ts), lambda b, i: (0, b * spb + i)),
                   pl.BlockSpec((nsub, ne, 1), lambda b, i: (b * spb + i, 0, 0))],
        out_shape=[jax.ShapeDtypeStruct((bsz, seq, d), F32),
                   jax.ShapeDtypeStruct((bsz, seq, d), F32),
                   jax.ShapeDtypeStruct((TOP_K, n_tok), I32),
                   jax.ShapeDtypeStruct((WT_ROWS, n_tok), F32),
                   jax.ShapeDtypeStruct((bsz * spb * nsub, ne, 1), I32)],
        scratch_shapes=[pltpu.VMEM((nblk, tm, LANES), F32)],
        compiler_params=_cparams("parallel", "parallel"),
        name="merge",
    )(y4, attn, x, w_glu_bf, b_glu.reshape(1, ssm_w), g_ssm.reshape(1, ssm_w), g_attn.reshape(1, aw),
      w_out_bf, mod3, g_ffn.reshape(1, d), mod3, mod3, rw_t, router_b.reshape(ne, 1))


def _segment_copies(n_seg, meta, make_copy):
    def body(e, _):
        src0, dst0, units = meta(e)
        for sz in SEG_UNITS:
            off = units & ~(2 * sz - 1)

            @pl.when((units & sz) != 0)
            def _(off=off, sz=sz):
                make_copy(src0 + off, dst0 + off, sz).start()
        return 0

    lax.fori_loop(0, n_seg, body, 0)


def _local_slots(idx, lstart_col, ne):
    tb = idx.shape[1]
    sub = lax.broadcasted_iota(I32, (ne, tb), 0)
    hits = [idx[kk:kk + 1, :] == sub for kk in range(TOP_K)]
    onehot = jnp.zeros((ne, tb), F32)
    for h in hits:
        onehot = onehot + jnp.where(h, 1.0, 0.0)
    r = lax.broadcasted_iota(I32, (tb, tb), 0)
    c = lax.broadcasted_iota(I32, (tb, tb), 1)
    tri = jnp.where(r < c, 1.0, 0.0).astype(BF16)
    slot = _dot(onehot.astype(BF16), tri) + lstart_col
    return [jnp.sum(jnp.where(h, slot, 0.0), axis=0, keepdims=True) for h in hits]


def _dispatch_body(lstart_s, pcnt_s, base_s, tail_dst_s, tail_units_s, tail_total_s,
                   idx_ref, lcol_ref, h_ref, xs_ref, ls_ref, sorted_ref, zero_ref, sem, *, ne):
    i = pl.program_id(0)
    n_steps = pl.num_programs(0)
    nsub = lcol_ref.shape[0]
    tb = h_ref.shape[0] // nsub
    lb = sorted_ref.shape[1] * SEG_ALIGN

    @pl.when(i == 0)
    def _():
        zero_ref[...] = jnp.zeros(zero_ref.shape, F32)
        _segment_copies(
            ne, lambda e: (0, tail_dst_s[e] // SEG_ALIGN, tail_units_s[e]),
            lambda src, dst, n: pltpu.make_async_copy(zero_ref.at[pl.ds(src, n)], xs_ref.at[pl.ds(dst, n)],
                                                      sem.at[nsub]))

    def tile_units(t):
        return (lstart_s[t * ne + ne - 1] + pcnt_s[t * ne + ne - 1]) // SEG_ALIGN

    def wait_tile(t, sl):
        done = xs_ref.at[pl.ds(0, tile_units(t))]
        pltpu.make_async_copy(done, done, sem.at[sl]).wait()

    for sl in range(nsub):
        t = i * nsub + sl
        cols = slice(sl * tb, (sl + 1) * tb)

        @pl.when(i >= 1)
        def _(t=t, sl=sl):
            wait_tile(t - nsub, sl)

        slots = _local_slots(idx_ref[:, cols], lcol_ref[sl].astype(F32), ne)
        row_o = lax.broadcasted_iota(I32, (ls_ref.shape[0], tb), 0)
        ls_out = jnp.full((ls_ref.shape[0], tb), -1.0, F32)
        for kk in range(TOP_K):
            ls_out = jnp.where(row_o == kk, slots[kk], ls_out)
        ls_ref[:, cols] = ls_out.astype(I32)
        h = h_ref[cols, :].astype(BF16)
        for r0 in range(0, lb, SORT_CHUNK):
            j = (lax.broadcasted_iota(I32, (SORT_CHUNK, tb), 0) + r0).astype(F32)
            pm = jnp.zeros((SORT_CHUNK, tb), F32)
            for kk in range(TOP_K):
                pm = jnp.where(j == slots[kk], 1.0, pm)
            sorted_ref[sl, r0 // SEG_ALIGN:(r0 + SORT_CHUNK) // SEG_ALIGN] = _dot(pm.astype(BF16), h).reshape(
                SORT_CHUNK // SEG_ALIGN, SEG_ALIGN, h.shape[1])

        def meta(e, t=t):
            return (lstart_s[t * ne + e] // SEG_ALIGN, base_s[t * ne + e] // SEG_ALIGN,
                    pcnt_s[t * ne + e] // SEG_ALIGN)

        def make_copy(src, dst, n, sl=sl):
            return pltpu.make_async_copy(sorted_ref.at[sl, pl.ds(src, n)], xs_ref.at[pl.ds(dst, n)], sem.at[sl])

        _segment_copies(ne, meta, make_copy)

    @pl.when(i == n_steps - 1)
    def _():
        for sl in range(nsub):
            wait_tile(i * nsub + sl, sl)

        @pl.when(tail_total_s[0] > 0)
        def _():
            filled = xs_ref.at[pl.ds(0, tail_total_s[0])]
            pltpu.make_async_copy(filled, filled, sem.at[nsub]).wait()


def _dispatch(h2, idx_t, plan, tail, cap, tb, tg):
    n, d = h2.shape
    lstart, pcnt, base = plan
    nt, ne = lstart.shape
    nsub = TILES_PER_STEP if nt % TILES_PER_STEP == 0 else 1
    lb = tb * TOP_K + ne * SEG_ALIGN
    assert lb % SORT_CHUNK == 0
    return pl.pallas_call(
        functools.partial(_dispatch_body, ne=ne),
        grid_spec=pltpu.PrefetchScalarGridSpec(
            num_scalar_prefetch=6,
            grid=(nt // nsub,),
            in_specs=[pl.BlockSpec((TOP_K, nsub * tb), lambda i, *_: (0, i)),
                      pl.BlockSpec((nsub, ne, 1), lambda i, *_: (i, 0, 0)),
                      pl.BlockSpec((nsub * tb, d), lambda i, *_: (i, 0))],
            out_specs=[pl.BlockSpec(memory_space=pl.ANY),
                       pl.BlockSpec((WT_ROWS, nsub * tb), lambda i, *_: (0, i))],
            scratch_shapes=[pltpu.VMEM((nsub, lb // SEG_ALIGN, SEG_ALIGN, d), F32),
                            pltpu.VMEM((tg // SEG_ALIGN, SEG_ALIGN, d), F32),
                            pltpu.SemaphoreType.DMA((nsub + 1,))],
        ),
        out_shape=[jax.ShapeDtypeStruct((cap // SEG_ALIGN, SEG_ALIGN, d), F32),
                   jax.ShapeDtypeStruct((WT_ROWS, n), I32)],
        compiler_params=_cparams("arbitrary"),
        name="dispatch",
    )(lstart.reshape(-1), pcnt.reshape(-1), base.reshape(-1), *tail, idx_t, lstart.reshape(nt, ne, 1), h2)


def _route_plan(tile_cnt, tg, cap):
    cnt = tile_cnt[:, :, 0]
    ne = cnt.shape[1]
    pcnt = (cnt + SEG_ALIGN - 1) // SEG_ALIGN * SEG_ALIGN
    lstart = jnp.cumsum(pcnt, axis=1) - pcnt
    e_rows = jnp.sum(pcnt, axis=0)
    e_tiles = (e_rows + tg - 1) // tg
    e_end = jnp.cumsum(e_tiles)
    base = ((e_end - e_tiles) * tg)[None, :] + jnp.cumsum(pcnt, axis=0) - pcnt
    ii = jnp.arange(cap // tg, dtype=I32)
    used = ii < e_end[-1]
    tile = jnp.minimum(ii, e_end[-1] - 1)
    e_of = jnp.minimum(jnp.sum((e_end[None, :] <= tile[:, None]).astype(I32), axis=1), ne - 1)
    newe = (e_of != jnp.concatenate([jnp.full((1,), -1, I32), e_of[:-1]])).astype(I32)
    items = (tile.astype(I32), e_of.astype(I32), used.astype(I32), newe)
    tail = (e_end * tg - e_tiles * tg + e_rows, (e_tiles * tg - e_rows) // SEG_ALIGN)
    tail = (tail[0].astype(I32), tail[1].astype(I32), jnp.sum(tail[1]).astype(I32).reshape(1))
    return (lstart.astype(I32), pcnt.astype(I32), base.astype(I32)), tail, items


def _experts_body(tile_ref, exp_ref, used_ref, newe_ref,
                  xs_ref, wg_ref, bg_ref, wu_ref, bu_ref, wd_ref, bd_ref, ys_ref,
                  wg_bf, wu_bf, wd_bf):
    i = pl.program_id(0)

    @pl.when(newe_ref[i] == 1)
    def _cast():
        wg_bf[...] = wg_ref[0].astype(BF16)
        wu_bf[...] = wu_ref[0].astype(BF16)
        wd_bf[...] = wd_ref[0].astype(BF16)

    @pl.when(used_ref[i] == 1)
    def _compute():
        x = xs_ref[...].astype(BF16)
        gate = jnp.minimum(_dot(x, wg_bf[...]) + bg_ref[0], SWIGLU_LIMIT)
        lin = jnp.clip(_dot(x, wu_bf[...]) + bu_ref[0], -SWIGLU_LIMIT, SWIGLU_LIMIT)
        act = gate * jax.nn.sigmoid(SWIGLU_ALPHA * gate) * (lin + 1.0)
        ys_ref[...] = _dot(act.astype(BF16), wd_bf[...]) + bd_ref[0]


def _experts(xs, items, w_gate, b_gate, w_up, b_up, w_down, b_down, tg):
    nk, d = xs.shape
    ne, _, dff = w_gate.shape
    n_items = items[0].shape[0]
    wspec = lambda shp: pl.BlockSpec((1,) + shp, lambda i, t, e, u, nw: (e[i], 0, 0))
    return pl.pallas_call(
        _experts_body,
        grid_spec=pltpu.PrefetchScalarGridSpec(
            num_scalar_prefetch=4,
            grid=(n_items,),
            in_specs=[pl.BlockSpec((tg, d), lambda i, t, e, u, nw: (t[i], 0)),
                      wspec((d, dff)), wspec((1, dff)),
                      wspec((d, dff)), wspec((1, dff)),
                      wspec((dff, d)), wspec((1, d))],
            out_specs=pl.BlockSpec((tg, d), lambda i, t, e, u, nw: (t[i], 0)),
            scratch_shapes=[pltpu.VMEM((d, dff), BF16), pltpu.VMEM((d, dff), BF16), pltpu.VMEM((dff, d), BF16)],
        ),
        out_shape=jax.ShapeDtypeStruct((nk, d), F32),
        compiler_params=_cparams("arbitrary"),
        name="experts",
    )(*items, xs, w_gate, b_gate.reshape(ne, 1, dff), w_up, b_up.reshape(ne, 1, dff),
      w_down, b_down.reshape(ne, 1, d))


def _combine_body(lstart_s, pcnt_s, base_s, ls_ref, wt_ref, x1_ref, gf_ref, gfin_ref, sho_ref, sco_ref,
                  ys_ref, o_ref, buf, sem, *, ne, spb):
    i = pl.program_id(0) * spb + pl.program_id(1)
    nsub = buf.shape[0]
    nt = pl.num_programs(0) * spb * nsub
    tb = x1_ref.shape[1] // nsub
    lb = buf.shape[1] * SEG_ALIGN

    def fetch(t, sl):
        def meta(e):
            return (base_s[t * ne + e] // SEG_ALIGN, lstart_s[t * ne + e] // SEG_ALIGN,
                    pcnt_s[t * ne + e] // SEG_ALIGN)

        def make_copy(src, dst, n):
            return pltpu.make_async_copy(ys_ref.at[pl.ds(src, n)], buf.at[sl, pl.ds(dst, n)], sem.at[sl])

        _segment_copies(ne, meta, make_copy)

    @pl.when(i == 0)
    def _():
        buf[...] = jnp.zeros(buf.shape, F32)
        fetch(0, 0)

    for sl in range(nsub):
        t = i * nsub + sl
        rows = slice(sl * tb, (sl + 1) * tb)

        @pl.when(t + 1 < nt)
        def _(t=t, sl=sl):
            fetch(t + 1, (sl + 1) % nsub)

        total = (lstart_s[t * ne + ne - 1] + pcnt_s[t * ne + ne - 1]) // SEG_ALIGN
        done = buf.at[sl, pl.ds(0, total)]
        pltpu.make_async_copy(done, done, sem.at[sl]).wait()
        ls_col = ls_ref[:, rows].astype(F32).T
        wt_col = wt_ref[:, rows].T
        ffn = jnp.zeros((tb, x1_ref.shape[2]), F32)
        for r0 in range(0, lb, SORT_CHUNK):
            j = (lax.broadcasted_iota(I32, (tb, SORT_CHUNK), 1) + r0).astype(F32)
            wm = jnp.zeros((tb, SORT_CHUNK), F32)
            for kk in range(TOP_K):
                wm = jnp.where(j == ls_col[:, kk:kk + 1], wt_col[:, kk:kk + 1], wm)
            y_rows = buf[sl, r0 // SEG_ALIGN:(r0 + SORT_CHUNK) // SEG_ALIGN].reshape(SORT_CHUNK, x1_ref.shape[2])
            ffn = ffn + _dot(wm.astype(BF16), y_rows.astype(BF16))
        x2 = x1_ref[0, rows, :] + gf_ref[0] * ffn
        o_ref[0, rows, :] = _rms(x2, gfin_ref[...]) * (1.0 + sco_ref[0]) + sho_ref[0]


def _combine(plan, ls_t, wt_t, x1, mod3, g_final, fmod3, ys, tb):
    bsz, seq, d = x1.shape
    lstart, pcnt, base = plan
    nt, ne = lstart.shape
    nsub = TILES_PER_STEP if (seq // tb) % TILES_PER_STEP == 0 else 1
    ts = nsub * tb
    spb = seq // ts
    lb = tb * TOP_K + ne * SEG_ALIGN
    return pl.pallas_call(
        functools.partial(_combine_body, ne=ne, spb=spb),
        grid_spec=pltpu.PrefetchScalarGridSpec(
            num_scalar_prefetch=3,
            grid=(bsz, spb),
            in_specs=[pl.BlockSpec((WT_ROWS, ts), lambda b, i, *_: (0, b * spb + i)),
                      pl.BlockSpec((WT_ROWS, ts), lambda b, i, *_: (0, b * spb + i)),
                      pl.BlockSpec((1, ts, d), lambda b, i, *_: (b, i, 0)),
                      pl.BlockSpec((1, 1, d), lambda b, i, *_: (b, 0, 5)),
                      pl.BlockSpec((1, d), lambda b, i, *_: (0, 0)),
                      pl.BlockSpec((1, 1, d), lambda b, i, *_: (b, 0, 0)),
                      pl.BlockSpec((1, 1, d), lambda b, i, *_: (b, 0, 1)),
                      pl.BlockSpec(memory_space=pl.ANY)],
            out_specs=pl.BlockSpec((1, ts, d), lambda b, i, *_: (b, i, 0)),
            scratch_shapes=[pltpu.VMEM((nsub, lb // SEG_ALIGN, SEG_ALIGN, d), F32),
                            pltpu.SemaphoreType.DMA((nsub,))],
        ),
        out_shape=jax.ShapeDtypeStruct((bsz, seq, d), F32),
        compiler_params=_cparams("arbitrary", "arbitrary"),
        name="combine",
    )(lstart.reshape(-1), pcnt.reshape(-1), base.reshape(-1), ls_t, wt_t, x1, mod3, g_final.reshape(1, d),
      fmod3, fmod3, ys.reshape(-1, SEG_ALIGN, d))


def kernel(x, c, positions, ada_w, ada_b, final_ada_w, final_ada_b, norm_mix_g, norm_ffn_g, norm_final_g, w_in, ssm_lambda_re, ssm_lambda_im, ssm_log_dt, ssm_b_re, ssm_b_im, ssm_c_re, ssm_c_im, ssm_d, ssm_w_glu, ssm_b_glu, out_norm_ssm_g, out_norm_attn_g, w_out, router_w, router_b, exp_w_gate, exp_b_gate, exp_w_up, exp_b_up, exp_w_down, exp_b_down):
    bsz, seq, d = x.shape
    depth = ada_w.shape[0]
    ssm_w = ssm_d.shape[-1]
    attn_w = (w_in.shape[-1] - ssm_w) // 3
    ne = router_w.shape[-1]
    n_tok = bsz * seq
    nc = seq // SSM_CHUNK
    tg = 512

    rope_tab, rope_exp = _rope_tables(positions)
    fmod3 = _adaln(c, final_ada_w, final_ada_b).reshape(bsz, 1, 2 * d)
    for l in range(depth):
        mod3 = _adaln(c, ada_w[l], ada_b[l]).reshape(bsz, 1, -1)
        u4, q, k, v = _inproj(x, norm_mix_g[l], mod3, w_in[l].astype(BF16), rope_tab, rope_exp, ssm_w, attn_w)
        tables = _s5_params(ssm_lambda_re[l], ssm_lambda_im[l], ssm_log_dt[l], ssm_b_re[l], ssm_b_im[l],
                            ssm_c_re[l], ssm_c_im[l], ssm_d[l], nc)
        y4 = _s5(u4, *tables)
        attn = _attention(q, k, v)
        x1, h2, idx_t, wt_t, tile_cnt = _merge(y4, attn, x, ssm_w_glu[l].astype(BF16), ssm_b_glu[l], out_norm_ssm_g[l],
                                     out_norm_attn_g[l], w_out[l].astype(BF16), mod3, norm_ffn_g[l],
                                     router_w[l], router_b[l])
        tb = seq // (tile_cnt.shape[0] // bsz)
        cap = n_tok * TOP_K + tile_cnt.shape[0] * ne * SEG_ALIGN + ne * tg
        assert cap % tg == 0 and max(tb, tg) // SEG_ALIGN < 2 * SEG_UNITS[0]
        plan, tail, items = _route_plan(tile_cnt, tg, cap)
        xs, ls_t = _dispatch(h2.reshape(n_tok, d), idx_t, plan, tail, cap, tb, tg)
        ys = _experts(xs.reshape(cap, d), items, exp_w_gate[l], exp_b_gate[l], exp_w_up[l], exp_b_up[l],
                      exp_w_down[l], exp_b_down[l], tg)
        if l + 1 < depth:
            raise NotImplementedError("depth > 1 needs the non-final combine")
        x = _combine(plan, ls_t, wt_t, x1, mod3, norm_final_g, fmod3, ys, tb)
    return x
```

```python
import functools
import math

import jax
import jax.numpy as jnp
from jax import lax
from jax.experimental import pallas as pl
from jax.experimental.pallas import tpu as pltpu

F32 = jnp.float32
BF16 = jnp.bfloat16
I32 = jnp.int32
HIGHEST = lax.Precision.HIGHEST

LANES = 128
HEAD_DIM = 64
MOBA_BLOCK = 256
MOBA_TOPK = 3
ROT_DIM = HEAD_DIM // 4
ROPE_THETA = 500000.0
SSM_CHUNK = 16
TOP_K = 4
SWIGLU_ALPHA = 1.702
SWIGLU_LIMIT = 7.0
NORM_EPS = 1e-5
NEG_INF = -1e30
LOG2_E = math.log2(math.e)
KV_TILE = 2 * MOBA_BLOCK
PV_ROWS = HEAD_DIM + 8
ATTN_PAIRS_PER_STEP = 4
S5_SEQS_PER_STEP = 2
WT_ROWS = 8
SEG_ALIGN = 8
SEG_UNITS = (64, 32, 16, 8, 4, 2, 1)
SORT_CHUNK = 256
TILES_PER_STEP = 2
VMEM_LIMIT = 56 * 1024 * 1024


def _cparams(*sem):
    return pltpu.CompilerParams(dimension_semantics=sem, vmem_limit_bytes=VMEM_LIMIT)


def _dot(a, b):
    return jnp.dot(a, b, preferred_element_type=F32)


def _rms(x, g):
    return x * lax.rsqrt(jnp.mean(x * x, axis=-1, keepdims=True) + NORM_EPS) * g


def _adaln_body(c_ref, w_ref, b_ref, o_ref):
    c = c_ref[...]
    ca = c * jax.nn.sigmoid(c)
    o_ref[...] = jnp.dot(ca, w_ref[...], preferred_element_type=F32, precision=HIGHEST) + b_ref[...]


def _adaln(c, w, b, tn=512):
    bsz, d = c.shape
    n = w.shape[1]
    return pl.pallas_call(
        _adaln_body,
        grid=(n // tn,),
        in_specs=[pl.BlockSpec((bsz, d), lambda j: (0, 0)),
                  pl.BlockSpec((d, tn), lambda j: (0, j)),
                  pl.BlockSpec((1, tn), lambda j: (0, j))],
        out_specs=pl.BlockSpec((bsz, tn), lambda j: (0, j)),
        out_shape=jax.ShapeDtypeStruct((bsz, n), F32),
        compiler_params=_cparams("parallel"),
        name="adaln",
    )(c, w, b.reshape(1, n))


def _inproj_body(x_ref, g_ref, sh_ref, sc_ref, w_ref, tab_ref, exp_ref, u_ref, q_ref, k_ref, v_ref, us_ref):
    x = x_ref[0]
    tm = x.shape[0]
    h = _rms(x, g_ref[...]) * (1.0 + sc_ref[0]) + sh_ref[0]
    proj = _dot(h.astype(BF16), w_ref[...])
    ssm_w = u_ref.shape[0] * LANES
    aw = q_ref.shape[2]
    for j in range(u_ref.shape[0]):
        us_ref[j] = proj[:, j * LANES:(j + 1) * LANES]
        for t in range(SSM_CHUNK):
            u_ref[j, 0, :, t * LANES:(t + 1) * LANES] = us_ref[
                j, pl.ds(t, tm // SSM_CHUNK, stride=SSM_CHUNK), :].astype(BF16)
    t_hi = tab_ref[0].astype(BF16)
    t_lo = (tab_ref[0] - t_hi.astype(F32)).astype(BF16)
    tn = (((0,), (0,)), ((), ()))
    tab = (lax.dot_general(t_hi, exp_ref[...], tn, preferred_element_type=F32)
           + lax.dot_general(t_lo, exp_ref[...], tn, preferred_element_type=F32))
    lane1 = lax.broadcasted_iota(I32, (tm, LANES), 1)
    cos1 = tab[:, :LANES] + jnp.where((lane1 & (HEAD_DIM - 1)) >= ROT_DIM, 1.0, 0.0)
    reps = aw // LANES
    cosf = jnp.concatenate([cos1] * reps, axis=1)
    sinf = jnp.concatenate([tab[:, LANES:]] * reps, axis=1)
    lane = lax.broadcasted_iota(I32, (tm, aw), 1)
    first_half = (lane & (HEAD_DIM - 1)) < (ROT_DIM // 2)

    def rope(t):
        partner = jnp.where(first_half, pltpu.roll(t, aw - ROT_DIM // 2, 1), pltpu.roll(t, ROT_DIM // 2, 1))
        return t * cosf + partner * sinf

    q = rope(proj[:, ssm_w:ssm_w + aw]) * (HEAD_DIM ** -0.5 * LOG2_E)
    k = rope(proj[:, ssm_w + aw:ssm_w + 2 * aw])
    q_ref[0] = q.astype(BF16)
    k_ref[0] = k.astype(BF16)
    v_ref[0] = proj[:, ssm_w + 2 * aw:ssm_w + 3 * aw].astype(BF16)


def _inproj(x, gain, mod3, w_in_bf, rope_tab, rope_exp, ssm_w, attn_w, tm=1024):
    bsz, seq, d = x.shape
    n_u = ssm_w // LANES
    tm = min(tm, seq)
    return pl.pallas_call(
        _inproj_body,
        grid=(bsz, seq // tm),
        in_specs=[pl.BlockSpec((1, tm, d), lambda b, i: (b, i, 0)),
                  pl.BlockSpec((1, d), lambda b, i: (0, 0)),
                  pl.BlockSpec((1, 1, d), lambda b, i: (b, 0, 0)),
                  pl.BlockSpec((1, 1, d), lambda b, i: (b, 0, 1)),
                  pl.BlockSpec(w_in_bf.shape, lambda b, i: (0, 0)),
                  pl.BlockSpec((1, ROT_DIM, tm), lambda b, i: (b, 0, i)),
                  pl.BlockSpec(rope_exp.shape, lambda b, i: (0, 0))],
        out_specs=[pl.BlockSpec((n_u, 1, tm // SSM_CHUNK, SSM_CHUNK * LANES), lambda b, i: (0, b, i, 0)),
                   pl.BlockSpec((1, tm, attn_w), lambda b, i: (b, i, 0)),
                   pl.BlockSpec((1, tm, attn_w), lambda b, i: (b, i, 0)),
                   pl.BlockSpec((1, tm, attn_w), lambda b, i: (b, i, 0))],
        out_shape=[jax.ShapeDtypeStruct((n_u, bsz, seq // SSM_CHUNK, SSM_CHUNK * LANES), BF16),
                   jax.ShapeDtypeStruct((bsz, seq, attn_w), BF16),
                   jax.ShapeDtypeStruct((bsz, seq, attn_w), BF16),
                   jax.ShapeDtypeStruct((bsz, seq, attn_w), BF16)],
        scratch_shapes=[pltpu.VMEM((n_u, tm, LANES), F32)],
        compiler_params=_cparams("parallel", "parallel"),
        name="inproj",
    )(x, gain.reshape(1, d), mod3, mod3, w_in_bf, rope_tab, rope_exp)


def _rope_tables(positions):
    half = ROT_DIM // 2
    inv_freq = ROPE_THETA ** (-jnp.arange(0, ROT_DIM, 2, dtype=F32) / ROT_DIM)
    ang = positions.astype(F32)[:, None, :] * inv_freq[None, :, None]
    tab = jnp.concatenate([jnp.cos(ang), jnp.sin(ang)], axis=1)
    i = jnp.arange(ROT_DIM)[:, None]
    hl = jnp.arange(LANES)[None, :] % HEAD_DIM
    e_cos = jnp.where((i < half) & ((hl == i) | (hl == i + half)), 1.0, 0.0)
    e_sin = jnp.where(i >= half, jnp.where(hl == i - half, -1.0, jnp.where(hl == i, 1.0, 0.0)), 0.0)
    return tab, jnp.concatenate([e_cos, e_sin], axis=1).astype(BF16)


def _s5_params(lam_re, lam_im, log_dt, b_re, b_im, c_re, c_im, d_skip, n_chunks):
    g_all, p = lam_re.shape
    hc = b_re.shape[-1]
    t = SSM_CHUNK
    gpb = LANES // hc
    nblk = g_all // gpb
    lr, li = lam_re.astype(F32), lam_im.astype(F32)
    dt = jnp.exp(log_dt.astype(F32))[:, None]
    ldr, ldi = lr * dt, li * dt

    def apow(k):
        k = jnp.asarray(k, F32)[..., None, None]
        mag = jnp.exp(ldr * k)
        return mag * jnp.cos(ldi * k), mag * jnp.sin(ldi * k)

    ar, ai = apow(1.0)
    zr, zi = ar - 1.0, ai
    den = lr * lr + li * li
    cr = (zr * lr + zi * li) / den
    ci = (zi * lr - zr * li) / den
    bre, bim = b_re.astype(F32), b_im.astype(F32)
    bbr = cr[..., None] * bre - ci[..., None] * bim
    bbi = cr[..., None] * bim + ci[..., None] * bre
    cre, cim = c_re.astype(F32), c_im.astype(F32)
    def block_diag(a):
        w = a.shape[-1]
        wide = jnp.concatenate([a] * gpb, axis=-1)
        grp = jnp.arange(gpb)[:, None, None]
        lane_grp = (jnp.arange(gpb * w) // w)[None, None, :]
        wide = jnp.where(lane_grp == grp, wide, 0.0)
        return wide.reshape(a.shape[:-3] + (gpb * a.shape[-2], gpb * w))

    pr, pi = apow(jnp.arange(t))
    abr = pr[..., None] * bbr - pi[..., None] * bbi
    abi = pr[..., None] * bbi + pi[..., None] * bbr
    kf = (jnp.einsum('gnp,kgph->kghn', cre, abr, precision=HIGHEST)
          - jnp.einsum('gnp,kgph->kghn', cim, abi, precision=HIGHEST))
    kblk = block_diag(kf.reshape(t, nblk, gpb, hc, hc))
    kpad = jnp.concatenate([jnp.zeros_like(kblk[:1]), kblk], axis=0)
    pm = jnp.concatenate([kpad[0:t], kpad[1:t + 1]], axis=-1)
    kstack = jnp.transpose(pm[::-1], (1, 0, 2, 3)).reshape(nblk, t * LANES, 2 * LANES)

    def inject(ab):
        ab = jnp.transpose(ab[::-1].reshape(t, nblk, gpb, p, hc), (1, 0, 2, 4, 3))
        return block_diag(ab).reshape(nblk, t * LANES, gpb * p)

    bm = jnp.concatenate([inject(abr), inject(abi)], axis=-1)

    qr, qi = apow(jnp.arange(t) + 1.0)
    c_from_re = cre[None] * qr[:, :, None, :] - cim[None] * qi[:, :, None, :]
    c_from_im = -cre[None] * qi[:, :, None, :] - cim[None] * qr[:, :, None, :]

    def readout_t(cc):
        cc = jnp.transpose(cc.reshape(t, nblk, gpb, hc, p), (1, 0, 2, 3, 4))
        return block_diag(cc).reshape(nblk, t * LANES, gpb * p)

    cm = jnp.swapaxes(jnp.concatenate([readout_t(c_from_re), readout_t(c_from_im)], axis=-1), 1, 2)

    n_steps = max(1, int(math.log2(n_chunks)))
    shifts = [float(t * (1 << s)) for s in range(n_steps)]
    sr, si = apow(jnp.asarray(shifts))
    ap = jnp.concatenate([sr.reshape(n_steps, nblk, gpb * p), si.reshape(n_steps, nblk, gpb * p)], axis=-1)
    ap = jnp.transpose(ap, (1, 0, 2))
    dvec = jnp.tile(d_skip.astype(F32).reshape(nblk, 1, LANES), (1, 1, t))
    return kstack.astype(BF16), bm.astype(BF16), cm.astype(BF16), ap, dvec


def _s5_body(x_ref, ks_ref, bm_ref, cm_ref, ap_ref, d_ref, y_ref, h_ref):
    nseq, nc = x_ref.shape[1], x_ref.shape[2]
    t = SSM_CHUNK
    pad = h_ref.shape[1] - nc
    half = h_ref.shape[2] // 2
    x = x_ref[0].reshape(nseq * nc, x_ref.shape[3])
    inj = _dot(x, bm_ref[0])
    h_prev = []
    for sq in range(nseq):
        h_ref[sq, 0:pad] = jnp.zeros((pad, h_ref.shape[2]), F32)
        h_ref[sq, pad:pad + nc] = inj[sq * nc:(sq + 1) * nc]
        for step in range(ap_ref.shape[1]):
            d = 1 << step
            cur = h_ref[sq, pad:pad + nc]
            sft = h_ref[sq, pad - d:pad + nc - d]
            ar = ap_ref[0, step:step + 1, 0:half]
            ai = ap_ref[0, step:step + 1, half:]
            cr, ci = cur[:, :half], cur[:, half:]
            sr, si = sft[:, :half], sft[:, half:]
            h_ref[sq, pad:pad + nc, 0:half] = cr + ar * sr - ai * si
            h_ref[sq, pad:pad + nc, half:] = ci + ar * si + ai * sr
        h_prev.append(h_ref[sq, pad - 1:pad + nc - 1])
    ys = _dot(jnp.concatenate(h_prev, axis=0).astype(BF16), cm_ref[0])
    for t0 in range(0, t, 2):
        lo, hi = t0 * LANES, (t0 + 2) * LANES
        conv = _dot(x[:, :hi], ks_ref[0, (t - 2 - t0) * LANES:, :])
        y = conv + ys[:, lo:hi] + d_ref[0][:, lo:hi] * x[:, lo:hi].astype(F32)
        y_ref[0, :, :, lo:hi] = y.reshape(nseq, nc, hi - lo).astype(y_ref.dtype)


def _s5(u4, kstack, bm, cm, ap, dvec):
    nblk, bsz, nc, w = u4.shape
    sw = bm.shape[-1]
    pad = max(nc // 2, 8)
    nseq = S5_SEQS_PER_STEP if bsz % S5_SEQS_PER_STEP == 0 else 1
    return pl.pallas_call(
        _s5_body,
        grid=(nblk, bsz // nseq),
        in_specs=[pl.BlockSpec((1, nseq, nc, w), lambda j, b: (j, b, 0, 0)),
                  pl.BlockSpec((1,) + kstack.shape[1:], lambda j, b: (j, 0, 0)),
                  pl.BlockSpec((1,) + bm.shape[1:], lambda j, b: (j, 0, 0)),
                  pl.BlockSpec((1,) + cm.shape[1:], lambda j, b: (j, 0, 0)),
                  pl.BlockSpec((1,) + ap.shape[1:], lambda j, b: (j, 0, 0)),
                  pl.BlockSpec((1, 1, w), lambda j, b: (j, 0, 0))],
        out_specs=pl.BlockSpec((1, nseq, nc, w), lambda j, b: (j, b, 0, 0)),
        out_shape=jax.ShapeDtypeStruct((nblk, bsz, nc, w), BF16),
        scratch_shapes=[pltpu.VMEM((nseq, pad + nc, sw), F32)],
        compiler_params=_cparams("parallel", "parallel"),
        name="s5",
    )(u4, kstack, bm, cm, ap, dvec)


def _attn_body(q_ref, k_ref, v_ref, o_ref, vt_ref, km_ref, bias_ref, sa_ref, sb_ref):
    qi = pl.program_id(2)
    seq = k_ref.shape[1]
    tq = q_ref.shape[1]
    npair = q_ref.shape[2] // LANES
    nh = 2 * npair
    nb = seq // MOBA_BLOCK
    n_kv = seq // KV_TILE
    bpt = KV_TILE // MOBA_BLOCK
    blk_shift = MOBA_BLOCK.bit_length() - 1

    @pl.when(qi == 0)
    def _build():
        ones = jnp.where(lax.broadcasted_iota(I32, (PV_ROWS - HEAD_DIM, KV_TILE), 0) == 0, 1.0, 0.0)
        lane_b = lax.broadcasted_iota(I32, (nb, LANES), 1)
        for pr in range(npair):
            k = k_ref[0, :, pr * LANES:(pr + 1) * LANES].astype(F32)
            v = v_ref[0, :, pr * LANES:(pr + 1) * LANES].astype(F32)
            for t in range(n_kv):
                v_t = v[t * KV_TILE:(t + 1) * KV_TILE].T
                vt_ref[2 * pr, t] = jnp.concatenate([v_t[:HEAD_DIM], ones], axis=0).astype(BF16)
                vt_ref[2 * pr + 1, t] = jnp.concatenate([v_t[HEAD_DIM:], ones], axis=0).astype(BF16)
            km = jnp.mean(k.reshape(nb, MOBA_BLOCK, LANES), axis=1)
            km_ref[2 * pr] = jnp.where(lane_b < HEAD_DIM, km, 0.0)
            km_ref[2 * pr + 1] = jnp.where(lane_b >= HEAD_DIM, km, 0.0)

    feat = lax.broadcasted_iota(I32, (LANES, tq), 0)
    q_ts, q_h = [], []
    for pr in range(npair):
        q_t = q_ref[0, :, pr * LANES:(pr + 1) * LANES].astype(F32).T
        q_ts += [q_t, q_t]
        q_h += [jnp.where(feat < HEAD_DIM, q_t, 0.0).astype(BF16),
                jnp.where(feat >= HEAD_DIM, q_t, 0.0).astype(BF16)]

    jidx = lax.broadcasted_iota(I32, (nb, tq), 0)
    qblk = qi * (tq // MOBA_BLOCK) + lax.shift_right_logical(
        lax.broadcasted_iota(I32, (nb, tq), 1), blk_shift)
    for hx in range(nh):
        g = _dot(km_ref[hx], q_ts[hx])
        cnt = jnp.zeros((nb, tq), F32)
        for jp in range(nb):
            row = g[jp:jp + 1, :]
            beats = jnp.where(row > g, 1.0, jnp.where(row == g, jnp.where(jp < jidx, 1.0, 0.0), 0.0))
            cnt = cnt + jnp.where(jp < qblk, beats, 0.0)
        past_sel = jnp.where(jidx < qblk, jnp.where(cnt < MOBA_TOPK, 1.0, 0.0), 0.0)
        sel = jnp.where(jidx == qblk, 1.0, past_sel)
        bias_ref[hx] = jnp.where(sel > 0.5, 0.0, NEG_INF)

    def raw_scores(dst_ref, tile):
        off = pl.multiple_of(tile * KV_TILE, KV_TILE)
        for hx in range(nh):
            pr = hx // 2
            dst_ref[hx] = _dot(k_ref[0, pl.ds(off, KV_TILE), pr * LANES:(pr + 1) * LANES], q_h[hx])

    def fold(carry, src_ref, tile, causal=None):
        out = []
        for hx in range(nh):
            parts, biases = [], []
            for j in range(bpt):
                s_j = src_ref[hx, j * MOBA_BLOCK:(j + 1) * MOBA_BLOCK, :]
                if causal is not None:
                    s_j = jnp.where(causal[j], s_j, NEG_INF)
                b_j = bias_ref[hx, pl.ds(tile * bpt + j, 1), :]
                parts.append(s_j)
                biases.append(b_j)
            m_n = jnp.max(parts[0], axis=0, keepdims=True) + biases[0]
            for s_j, b_j in zip(parts[1:], biases[1:]):
                m_n = jnp.maximum(m_n, jnp.max(s_j, axis=0, keepdims=True) + b_j)
            if carry is None:
                p_t = jnp.concatenate([jnp.exp2(s_j + (b_j - m_n)) for s_j, b_j in zip(parts, biases)], axis=0)
                out += [m_n, _dot(vt_ref[hx, tile], p_t.astype(BF16))]
            else:
                m_c, acc_c = carry[2 * hx], carry[2 * hx + 1]
                m_n = jnp.maximum(m_c, m_n)
                p_t = jnp.concatenate([jnp.exp2(s_j + (b_j - m_n)) for s_j, b_j in zip(parts, biases)], axis=0)
                out += [m_n, acc_c * jnp.exp2(m_c - m_n) + _dot(vt_ref[hx, tile], p_t.astype(BF16))]
        return tuple(out)

    raw_scores(sa_ref, qi)
    raw_scores(sb_ref, 0)
    krow = lax.broadcasted_iota(I32, (MOBA_BLOCK, tq), 0)
    qcol = lax.broadcasted_iota(I32, (MOBA_BLOCK, tq), 1)
    state = fold(None, sa_ref, qi, causal=[krow + j * MOBA_BLOCK <= qcol for j in range(bpt)])

    def body(i, carry):
        t0 = 2 * i
        raw_scores(sa_ref, t0 + 1)
        carry = fold(carry, sb_ref, t0)
        raw_scores(sb_ref, jnp.minimum(t0 + 2, n_kv - 1))
        return fold(carry, sa_ref, t0 + 1)

    state = lax.fori_loop(0, qi // 2, body, state)
    state = lax.cond(lax.rem(qi, 2) == 1, lambda c: fold(c, sb_ref, qi - 1), lambda c: c, state)
    for pr in range(npair):
        accs = (state[4 * pr + 1], state[4 * pr + 3])
        o_t = jnp.concatenate([acc[:HEAD_DIM] / acc[HEAD_DIM:HEAD_DIM + 1] for acc in accs], axis=0)
        o_ref[0, :, pr * LANES:(pr + 1) * LANES] = o_t.T.astype(o_ref.dtype)


def _attention(q, k, v):
    bsz, seq, aw = q.shape
    pps = ATTN_PAIRS_PER_STEP
    wl = pps * LANES
    npair = aw // wl
    tq = KV_TILE
    nb = seq // MOBA_BLOCK
    assert seq % KV_TILE == 0 and nb % 8 == 0
    return pl.pallas_call(
        _attn_body,
        grid=(bsz, npair, seq // tq),
        in_specs=[pl.BlockSpec((1, tq, wl), lambda b, p, i: (b, i, p)),
                  pl.BlockSpec((1, seq, wl), lambda b, p, i: (b, 0, p)),
                  pl.BlockSpec((1, seq, wl), lambda b, p, i: (b, 0, p))],
        out_specs=pl.BlockSpec((1, tq, wl), lambda b, p, i: (b, i, p)),
        out_shape=jax.ShapeDtypeStruct((bsz, seq, aw), BF16),
        scratch_shapes=[pltpu.VMEM((2 * pps, seq // KV_TILE, PV_ROWS, KV_TILE), BF16),
                        pltpu.VMEM((2 * pps, nb, LANES), F32),
                        pltpu.VMEM((2 * pps, nb, tq), F32),
                        pltpu.VMEM((2 * pps, KV_TILE, tq), F32),
                        pltpu.VMEM((2 * pps, KV_TILE, tq), F32)],
        compiler_params=_cparams("parallel", "parallel", "arbitrary"),
        name="attn",
    )(q, k, v)


def _merge_body(y_ref, a_ref, x_ref, wglu_ref, bglu_ref, gs_ref, ga_ref, wout_ref, gm_ref,
                gf_ref, shf_ref, scf_ref, rw_ref, rb_ref, x1_ref, h2_ref, idx_ref, wt_ref, cnt_ref, ys_ref):
    nblk = y_ref.shape[0]
    nsub = cnt_ref.shape[0]
    tm = x_ref.shape[1] // nsub
    ne = rw_ref.shape[0]
    nch = tm // SSM_CHUNK
    sub = lax.broadcasted_iota(I32, (ne, tm), 0)
    row_i = lax.broadcasted_iota(I32, (idx_ref.shape[0], tm), 0)
    row_w = lax.broadcasted_iota(I32, (wt_ref.shape[0], tm), 0)
    for sl in range(nsub):
        rows = slice(sl * tm, (sl + 1) * tm)
        for j in range(nblk):
            for t in range(SSM_CHUNK):
                ys_ref[j, pl.ds(t, nch, stride=SSM_CHUNK), :] = (
                    y_ref[j, 0, sl * nch:(sl + 1) * nch, t * LANES:(t + 1) * LANES].astype(F32))
        y = jnp.concatenate([ys_ref[j] for j in range(nblk)], axis=1)
        g = 0.5 * y * (1.0 + lax.erf(y * (2.0 ** -0.5)))
        glu = g * jax.nn.sigmoid(_dot(g.astype(BF16), wglu_ref[...]) + bglu_ref[...])
        ssm_n = _rms(glu, gs_ref[...])
        att_n = _rms(a_ref[0, rows, :].astype(F32), ga_ref[...])
        merged = jnp.concatenate([ssm_n, att_n], axis=1).astype(BF16)
        mix = _dot(merged, wout_ref[...])
        x1 = x_ref[0, rows, :] + gm_ref[0] * mix
        x1_ref[0, rows, :] = x1
        h2 = _rms(x1, gf_ref[...]) * (1.0 + scf_ref[0]) + shf_ref[0]
        h2_ref[0, rows, :] = h2
        logits = lax.dot_general(rw_ref[...], h2, (((1,), (1,)), ((), ())),
                                 preferred_element_type=F32, precision=HIGHEST) + rb_ref[...]
        vals, idxs = [], []
        for _ in range(TOP_K):
            mx = jnp.max(logits, axis=0, keepdims=True)
            ix = jnp.min(jnp.where(logits == mx, sub, ne), axis=0, keepdims=True)
            vals.append(mx)
            idxs.append(ix)
            logits = jnp.where(sub == ix, -jnp.inf, logits)
        exps = [jnp.exp(vv - vals[0]) for vv in vals]
        tot = exps[0]
        for e in exps[1:]:
            tot = tot + e
        idx_out = jnp.zeros(row_i.shape, I32)
        wt_out = jnp.zeros(row_w.shape, F32)
        chosen = jnp.zeros((ne, tm), F32)
        for kk in range(TOP_K):
            idx_out = jnp.where(row_i == kk, idxs[kk], idx_out)
            wt_out = jnp.where(row_w == kk, exps[kk] / tot, wt_out)
            chosen = chosen + jnp.where(sub == idxs[kk], 1.0, 0.0)
        idx_ref[:, rows] = idx_out
        wt_ref[:, rows] = wt_out
        cnt_ref[sl] = jnp.sum(chosen, axis=1, keepdims=True).astype(I32)


def _merge(y4, attn, x, w_glu_bf, b_glu, g_ssm, g_attn, w_out_bf, mod3, g_ffn, router_w, router_b, tm=512):
    bsz, seq, d = x.shape
    nblk = y4.shape[0]
    ssm_w = nblk * LANES
    aw = attn.shape[-1]
    ne = router_w.shape[1]
    tm = min(tm, seq)
    nsub = TILES_PER_STEP if (seq // tm) % TILES_PER_STEP == 0 else 1
    ts = nsub * tm
    spb = seq // ts
    n_tok = bsz * seq
    rw_t = router_w.T
    row = lambda n: pl.BlockSpec((1, n), lambda b, i: (0, 0))
    full = lambda a: pl.BlockSpec(a.shape, lambda b, i: (0,) * a.ndim)
    modv = lambda j: pl.BlockSpec((1, 1, d), lambda b, i: (b, 0, j))
    return pl.pallas_call(
        _merge_body,
        grid=(bsz, spb),
        in_specs=[pl.BlockSpec((nblk, 1, ts // SSM_CHUNK, SSM_CHUNK * LANES), lambda b, i: (0, b, i, 0)),
                  pl.BlockSpec((1, ts, aw), lambda b, i: (b, i, 0)),
                  pl.BlockSpec((1, ts, d), lambda b, i: (b, i, 0)),
                  full(w_glu_bf), row(ssm_w), row(ssm_w), row(aw), full(w_out_bf),
                  modv(2), row(d), modv(3), modv(4), full(rw_t),
                  pl.BlockSpec((ne, 1), lambda b, i: (0, 0))],
        out_specs=[pl.BlockSpec((1, ts, d), lambda b, i: (b, i, 0)),
                   pl.BlockSpec((1, ts, d), lambda b, i: (b, i, 0)),
                   pl.BlockSpec((TOP_K, ts), lambda b, i: (0, b * spb + i)),
                   pl.BlockSpec((WT_ROWS, ts), lambda b, i: (0, b * spb + i)),
                   pl.BlockSpec((nsub, ne, 1), lambda b, i: (b * spb + i, 0, 0))],
        out_shape=[jax.ShapeDtypeStruct((bsz, seq, d), F32),
                   jax.ShapeDtypeStruct((bsz, seq, d), F32),
                   jax.ShapeDtypeStruct((TOP_K, n_tok), I32),
                   jax.ShapeDtypeStruct((WT_ROWS, n_tok), F32),
                   jax.ShapeDtypeStruct((bsz * spb * nsub, ne, 1), I32)],
        scratch_shapes=[pltpu.VMEM((nblk, tm, LANES), F32)],
        compiler_params=_cparams("parallel", "parallel"),
        name="merge",
    )(y4, attn, x, w_glu_bf, b_glu.reshape(1, ssm_w), g_ssm.reshape(1, ssm_w), g_attn.reshape(1, aw),
      w_out_bf, mod3, g_ffn.reshape(1, d), mod3, mod3, rw_t, router_b.reshape(ne, 1))


def _segment_copies(n_seg, meta, make_copy, split_priority=False):
    def body(e, _):
        src0, dst0, units = meta(e)
        for k, sz in enumerate(SEG_UNITS):
            off = units & ~(2 * sz - 1)

            @pl.when((units & sz) != 0)
            def _(off=off, sz=sz, k=k):
                make_copy(src0 + off, dst0 + off, sz).start(priority=k % 2 if split_priority else 0)
        return 0

    lax.fori_loop(0, n_seg, body, 0)


def _local_slots(idx, lstart_col, ne):
    tb = idx.shape[1]
    sub = lax.broadcasted_iota(I32, (ne, tb), 0)
    hits = [idx[kk:kk + 1, :] == sub for kk in range(TOP_K)]
    onehot = jnp.zeros((ne, tb), F32)
    for h in hits:
        onehot = onehot + jnp.where(h, 1.0, 0.0)
    r = lax.broadcasted_iota(I32, (tb, tb), 0)
    c = lax.broadcasted_iota(I32, (tb, tb), 1)
    tri = jnp.where(r < c, 1.0, 0.0).astype(BF16)
    slot = _dot(onehot.astype(BF16), tri) + lstart_col
    return [jnp.sum(jnp.where(h, slot, 0.0), axis=0, keepdims=True) for h in hits]


def _dispatch_body(lstart_s, pcnt_s, base_s, tail_dst_s, tail_units_s, tail_total_s,
                   idx_ref, lcol_ref, h_ref, xs_ref, ls_ref, sorted_ref, zero_ref, sem, *, ne):
    i = pl.program_id(0)
    n_steps = pl.num_programs(0)
    nsub = lcol_ref.shape[0]
    tb = h_ref.shape[0] // nsub
    lb = sorted_ref.shape[1] * SEG_ALIGN

    @pl.when(i == 0)
    def _():
        zero_ref[...] = jnp.zeros(zero_ref.shape, F32)
        _segment_copies(
            ne, lambda e: (0, tail_dst_s[e] // SEG_ALIGN, tail_units_s[e]),
            lambda src, dst, n: pltpu.make_async_copy(zero_ref.at[pl.ds(src, n)], xs_ref.at[pl.ds(dst, n)],
                                                      sem.at[nsub]))

    def tile_units(t):
        return (lstart_s[t * ne + ne - 1] + pcnt_s[t * ne + ne - 1]) // SEG_ALIGN

    def wait_tile(t, sl):
        done = xs_ref.at[pl.ds(0, tile_units(t))]
        pltpu.make_async_copy(done, done, sem.at[sl]).wait()

    for sl in range(nsub):
        t = i * nsub + sl
        cols = slice(sl * tb, (sl + 1) * tb)

        @pl.when(i >= 1)
        def _(t=t, sl=sl):
            wait_tile(t - nsub, sl)

        slots = _local_slots(idx_ref[:, cols], lcol_ref[sl].astype(F32), ne)
        row_o = lax.broadcasted_iota(I32, (ls_ref.shape[0], tb), 0)
        ls_out = jnp.full((ls_ref.shape[0], tb), -1.0, F32)
        for kk in range(TOP_K):
            ls_out = jnp.where(row_o == kk, slots[kk], ls_out)
        ls_ref[:, cols] = ls_out.astype(I32)
        h = h_ref[cols, :].astype(BF16)
        for r0 in range(0, lb, SORT_CHUNK):
            j = (lax.broadcasted_iota(I32, (SORT_CHUNK, tb), 0) + r0).astype(F32)
            pm = jnp.zeros((SORT_CHUNK, tb), F32)
            for kk in range(TOP_K):
                pm = jnp.where(j == slots[kk], 1.0, pm)
            sorted_ref[sl, r0 // SEG_ALIGN:(r0 + SORT_CHUNK) // SEG_ALIGN] = _dot(pm.astype(BF16), h).reshape(
                SORT_CHUNK // SEG_ALIGN, SEG_ALIGN, h.shape[1])

        def meta(e, t=t):
            return (lstart_s[t * ne + e] // SEG_ALIGN, base_s[t * ne + e] // SEG_ALIGN,
                    pcnt_s[t * ne + e] // SEG_ALIGN)

        def make_copy(src, dst, n, sl=sl):
            return pltpu.make_async_copy(sorted_ref.at[sl, pl.ds(src, n)], xs_ref.at[pl.ds(dst, n)], sem.at[sl])

        _segment_copies(ne, meta, make_copy, split_priority=True)

    @pl.when(i == n_steps - 1)
    def _():
        for sl in range(nsub):
            wait_tile(i * nsub + sl, sl)

        @pl.when(tail_total_s[0] > 0)
        def _():
            filled = xs_ref.at[pl.ds(0, tail_total_s[0])]
            pltpu.make_async_copy(filled, filled, sem.at[nsub]).wait()


def _dispatch(h2, idx_t, plan, tail, cap, tb, tg):
    n, d = h2.shape
    lstart, pcnt, base = plan
    nt, ne = lstart.shape
    nsub = TILES_PER_STEP if nt % TILES_PER_STEP == 0 else 1
    lb = tb * TOP_K + ne * SEG_ALIGN
    assert lb % SORT_CHUNK == 0
    return pl.pallas_call(
        functools.partial(_dispatch_body, ne=ne),
        grid_spec=pltpu.PrefetchScalarGridSpec(
            num_scalar_prefetch=6,
            grid=(nt // nsub,),
            in_specs=[pl.BlockSpec((TOP_K, nsub * tb), lambda i, *_: (0, i)),
                      pl.BlockSpec((nsub, ne, 1), lambda i, *_: (i, 0, 0)),
                      pl.BlockSpec((nsub * tb, d), lambda i, *_: (i, 0))],
            out_specs=[pl.BlockSpec(memory_space=pl.ANY),
                       pl.BlockSpec((WT_ROWS, nsub * tb), lambda i, *_: (0, i))],
            scratch_shapes=[pltpu.VMEM((nsub, lb // SEG_ALIGN, SEG_ALIGN, d), F32),
                            pltpu.VMEM((tg // SEG_ALIGN, SEG_ALIGN, d), F32),
                            pltpu.SemaphoreType.DMA((nsub + 1,))],
        ),
        out_shape=[jax.ShapeDtypeStruct((cap // SEG_ALIGN, SEG_ALIGN, d), F32),
                   jax.ShapeDtypeStruct((WT_ROWS, n), I32)],
        compiler_params=_cparams("arbitrary"),
        name="dispatch",
    )(lstart.reshape(-1), pcnt.reshape(-1), base.reshape(-1), *tail, idx_t, lstart.reshape(nt, ne, 1), h2)


def _route_plan(tile_cnt, tg, cap):
    cnt = tile_cnt[:, :, 0]
    ne = cnt.shape[1]
    pcnt = (cnt + SEG_ALIGN - 1) // SEG_ALIGN * SEG_ALIGN
    lstart = jnp.cumsum(pcnt, axis=1) - pcnt
    e_rows = jnp.sum(pcnt, axis=0)
    e_tiles = (e_rows + tg - 1) // tg
    e_end = jnp.cumsum(e_tiles)
    base = ((e_end - e_tiles) * tg)[None, :] + jnp.cumsum(pcnt, axis=0) - pcnt
    ii = jnp.arange(cap // tg, dtype=I32)
    used = ii < e_end[-1]
    tile = jnp.minimum(ii, e_end[-1] - 1)
    e_of = jnp.minimum(jnp.sum((e_end[None, :] <= tile[:, None]).astype(I32), axis=1), ne - 1)
    newe = (e_of != jnp.concatenate([jnp.full((1,), -1, I32), e_of[:-1]])).astype(I32)
    items = (tile.astype(I32), e_of.astype(I32), used.astype(I32), newe)
    tail = (e_end * tg - e_tiles * tg + e_rows, (e_tiles * tg - e_rows) // SEG_ALIGN)
    tail = (tail[0].astype(I32), tail[1].astype(I32), jnp.sum(tail[1]).astype(I32).reshape(1))
    return (lstart.astype(I32), pcnt.astype(I32), base.astype(I32)), tail, items


def _experts_body(tile_ref, exp_ref, used_ref, newe_ref,
                  xs_ref, wg_ref, bg_ref, wu_ref, bu_ref, wd_ref, bd_ref, ys_ref,
                  wg_bf, wu_bf, wd_bf):
    i = pl.program_id(0)

    @pl.when(newe_ref[i] == 1)
    def _cast():
        wg_bf[...] = wg_ref[0].astype(BF16)
        wu_bf[...] = wu_ref[0].astype(BF16)
        wd_bf[...] = wd_ref[0].astype(BF16)

    @pl.when(used_ref[i] == 1)
    def _compute():
        x = xs_ref[...].astype(BF16)
        gate = jnp.minimum(_dot(x, wg_bf[...]) + bg_ref[0], SWIGLU_LIMIT)
        lin = jnp.clip(_dot(x, wu_bf[...]) + bu_ref[0], -SWIGLU_LIMIT, SWIGLU_LIMIT)
        act = gate * jax.nn.sigmoid(SWIGLU_ALPHA * gate) * (lin + 1.0)
        ys_ref[...] = _dot(act.astype(BF16), wd_bf[...]) + bd_ref[0]


def _experts(xs, items, w_gate, b_gate, w_up, b_up, w_down, b_down, tg):
    nk, d = xs.shape
    ne, _, dff = w_gate.shape
    n_items = items[0].shape[0]
    wspec = lambda shp: pl.BlockSpec((1,) + shp, lambda i, t, e, u, nw: (e[i], 0, 0))
    return pl.pallas_call(
        _experts_body,
        grid_spec=pltpu.PrefetchScalarGridSpec(
            num_scalar_prefetch=4,
            grid=(n_items,),
            in_specs=[pl.BlockSpec((tg, d), lambda i, t, e, u, nw: (t[i], 0)),
                      wspec((d, dff)), wspec((1, dff)),
                      wspec((d, dff)), wspec((1, dff)),
                      wspec((dff, d)), wspec((1, d))],
            out_specs=pl.BlockSpec((tg, d), lambda i, t, e, u, nw: (t[i], 0)),
            scratch_shapes=[pltpu.VMEM((d, dff), BF16), pltpu.VMEM((d, dff), BF16), pltpu.VMEM((dff, d), BF16)],
        ),
        out_shape=jax.ShapeDtypeStruct((nk, d), F32),
        compiler_params=_cparams("arbitrary"),
        name="experts",
    )(*items, xs, w_gate, b_gate.reshape(ne, 1, dff), w_up, b_up.reshape(ne, 1, dff),
      w_down, b_down.reshape(ne, 1, d))


def _combine_body(lstart_s, pcnt_s, base_s, ls_ref, wt_ref, x1_ref, gf_ref, gfin_ref, sho_ref, sco_ref,
                  ys_ref, o_ref, buf, sem, *, ne, spb):
    i = pl.program_id(0) * spb + pl.program_id(1)
    nsub = buf.shape[0]
    nt = pl.num_programs(0) * spb * nsub
    tb = x1_ref.shape[1] // nsub
    lb = buf.shape[1] * SEG_ALIGN

    def fetch(t, sl):
        def meta(e):
            return (base_s[t * ne + e] // SEG_ALIGN, lstart_s[t * ne + e] // SEG_ALIGN,
                    pcnt_s[t * ne + e] // SEG_ALIGN)

        def make_copy(src, dst, n):
            return pltpu.make_async_copy(ys_ref.at[pl.ds(src, n)], buf.at[sl, pl.ds(dst, n)], sem.at[sl])

        _segment_copies(ne, meta, make_copy)

    @pl.when(i == 0)
    def _():
        buf[...] = jnp.zeros(buf.shape, F32)
        fetch(0, 0)

    for sl in range(nsub):
        t = i * nsub + sl
        rows = slice(sl * tb, (sl + 1) * tb)

        @pl.when(t + 1 < nt)
        def _(t=t, sl=sl):
            fetch(t + 1, (sl + 1) % nsub)

        total = (lstart_s[t * ne + ne - 1] + pcnt_s[t * ne + ne - 1]) // SEG_ALIGN
        done = buf.at[sl, pl.ds(0, total)]
        pltpu.make_async_copy(done, done, sem.at[sl]).wait()
        ls_col = ls_ref[:, rows].astype(F32).T
        wt_col = wt_ref[:, rows].T
        ffn = jnp.zeros((tb, x1_ref.shape[2]), F32)
        for r0 in range(0, lb, SORT_CHUNK):
            j = (lax.broadcasted_iota(I32, (tb, SORT_CHUNK), 1) + r0).astype(F32)
            wm = jnp.zeros((tb, SORT_CHUNK), F32)
            for kk in range(TOP_K):
                wm = jnp.where(j == ls_col[:, kk:kk + 1], wt_col[:, kk:kk + 1], wm)
            y_rows = buf[sl, r0 // SEG_ALIGN:(r0 + SORT_CHUNK) // SEG_ALIGN].reshape(SORT_CHUNK, x1_ref.shape[2])
            ffn = ffn + _dot(wm.astype(BF16), y_rows.astype(BF16))
        x2 = x1_ref[0, rows, :] + gf_ref[0] * ffn
        o_ref[0, rows, :] = _rms(x2, gfin_ref[...]) * (1.0 + sco_ref[0]) + sho_ref[0]


def _combine(plan, ls_t, wt_t, x1, mod3, g_final, fmod3, ys, tb):
    bsz, seq, d = x1.shape
    lstart, pcnt, base = plan
    nt, ne = lstart.shape
    nsub = TILES_PER_STEP if (seq // tb) % TILES_PER_STEP == 0 else 1
    ts = nsub * tb
    spb = seq // ts
    lb = tb * TOP_K + ne * SEG_ALIGN
    return pl.pallas_call(
        functools.partial(_combine_body, ne=ne, spb=spb),
        grid_spec=pltpu.PrefetchScalarGridSpec(
            num_scalar_prefetch=3,
            grid=(bsz, spb),
            in_specs=[pl.BlockSpec((WT_ROWS, ts), lambda b, i, *_: (0, b * spb + i)),
                      pl.BlockSpec((WT_ROWS, ts), lambda b, i, *_: (0, b * spb + i)),
                      pl.BlockSpec((1, ts, d), lambda b, i, *_: (b, i, 0)),
                      pl.BlockSpec((1, 1, d), lambda b, i, *_: (b, 0, 5)),
                      pl.BlockSpec((1, d), lambda b, i, *_: (0, 0)),
                      pl.BlockSpec((1, 1, d), lambda b, i, *_: (b, 0, 0)),
                      pl.BlockSpec((1, 1, d), lambda b, i, *_: (b, 0, 1)),
                      pl.BlockSpec(memory_space=pl.ANY)],
            out_specs=pl.BlockSpec((1, ts, d), lambda b, i, *_: (b, i, 0)),
            scratch_shapes=[pltpu.VMEM((nsub, lb // SEG_ALIGN, SEG_ALIGN, d), F32),
                            pltpu.SemaphoreType.DMA((nsub,))],
        ),
        out_shape=jax.ShapeDtypeStruct((bsz, seq, d), F32),
        compiler_params=_cparams("arbitrary", "arbitrary"),
        name="combine",
    )(lstart.reshape(-1), pcnt.reshape(-1), base.reshape(-1), ls_t, wt_t, x1, mod3, g_final.reshape(1, d),
      fmod3, fmod3, ys.reshape(-1, SEG_ALIGN, d))


def kernel(x, c, positions, ada_w, ada_b, final_ada_w, final_ada_b, norm_mix_g, norm_ffn_g, norm_final_g, w_in, ssm_lambda_re, ssm_lambda_im, ssm_log_dt, ssm_b_re, ssm_b_im, ssm_c_re, ssm_c_im, ssm_d, ssm_w_glu, ssm_b_glu, out_norm_ssm_g, out_norm_attn_g, w_out, router_w, router_b, exp_w_gate, exp_b_gate, exp_w_up, exp_b_up, exp_w_down, exp_b_down):
    bsz, seq, d = x.shape
    depth = ada_w.shape[0]
    ssm_w = ssm_d.shape[-1]
    attn_w = (w_in.shape[-1] - ssm_w) // 3
    ne = router_w.shape[-1]
    n_tok = bsz * seq
    nc = seq // SSM_CHUNK
    tg = 512

    rope_tab, rope_exp = _rope_tables(positions)
    fmod3 = _adaln(c, final_ada_w, final_ada_b).reshape(bsz, 1, 2 * d)
    for l in range(depth):
        mod3 = _adaln(c, ada_w[l], ada_b[l]).reshape(bsz, 1, -1)
        u4, q, k, v = _inproj(x, norm_mix_g[l], mod3, w_in[l].astype(BF16), rope_tab, rope_exp, ssm_w, attn_w)
        tables = _s5_params(ssm_lambda_re[l], ssm_lambda_im[l], ssm_log_dt[l], ssm_b_re[l], ssm_b_im[l],
                            ssm_c_re[l], ssm_c_im[l], ssm_d[l], nc)
        y4 = _s5(u4, *tables)
        attn = _attention(q, k, v)
        x1, h2, idx_t, wt_t, tile_cnt = _merge(y4, attn, x, ssm_w_glu[l].astype(BF16), ssm_b_glu[l], out_norm_ssm_g[l],
                                     out_norm_attn_g[l], w_out[l].astype(BF16), mod3, norm_ffn_g[l],
                                     router_w[l], router_b[l])
        tb = seq // (tile_cnt.shape[0] // bsz)
        cap = n_tok * TOP_K + tile_cnt.shape[0] * ne * SEG_ALIGN + ne * tg
        assert cap % tg == 0 and max(tb, tg) // SEG_ALIGN < 2 * SEG_UNITS[0]
        plan, tail, items = _route_plan(tile_cnt, tg, cap)
        xs, ls_t = _dispatch(h2.reshape(n_tok, d), idx_t, plan, tail, cap, tb, tg)
        ys = _experts(xs.reshape(cap, d), items, exp_w_gate[l], exp_b_gate[l], exp_w_up[l], exp_b_up[l],
                      exp_w_down[l], exp_b_down[l], tg)
        if l + 1 < depth:
            raise NotImplementedError("depth > 1 needs the non-final combine")
        x = _combine(plan, ls_t, wt_t, x1, mod3, norm_final_g, fmod3, ys, tb)
    return x
```
